```python
import math
import jax, jax.numpy as jnp
from jax import lax
import numpy as np

D_MODEL = 1024
BATCH = 32
SEQ = 2048
DEPTH = 1

CHUNK = 64
MIX_WIDTH = D_MODEL
RET_HEADS = 4
RET_VWIDTH = MIX_WIDTH // 2
RET_DV = RET_VWIDTH // RET_HEADS
RET_DK = RET_DV // 2
RET_QKWIDTH = RET_HEADS * RET_DK
POOL_WIDTH = MIX_WIDTH - RET_VWIDTH
POOL_WINDOWS = (2, 4, 8, 16)
POOL_GROUPS = len(POOL_WINDOWS)
POOL_GC = POOL_WIDTH // POOL_GROUPS
IN_WIDTH = 2 * RET_QKWIDTH + 2 * RET_VWIDTH + POOL_WIDTH
D_FF = 2816
ROPE_BASE = 10000.0
RMS_EPS = 1e-6
GN_EPS = 1e-5

kernel_name = "hybrid_retention_multiscale_pool_macaron"


def rms_norm(x, g):
    xf = x.astype(jnp.float32)
    y = xf * lax.rsqrt(jnp.mean(xf * xf, axis=-1, keepdims=True) + RMS_EPS)
    return (y * g.astype(jnp.float32)).astype(x.dtype)


def swiglu(x, w_gate, w_up, w_down):
    return (jax.nn.silu(x @ w_gate) * (x @ w_up)) @ w_down


def rotary(x, pos):
    d = x.shape[-1]
    half = d // 2
    freqs = ROPE_BASE ** (-jnp.arange(half, dtype=jnp.float32) * 2.0 / d)
    ang = pos.astype(jnp.float32)[:, None] * freqs[None, :]
    cos = jnp.cos(ang)[None, :, None, :].astype(x.dtype)
    sin = jnp.sin(ang)[None, :, None, :].astype(x.dtype)
    x1, x2 = x[..., :half], x[..., half:]
    return jnp.concatenate([x1 * cos - x2 * sin, x1 * sin + x2 * cos], axis=-1)


def retention_chunkwise(q, k, v):
    b, s, h, dk = q.shape
    dv = v.shape[-1]
    n = s // CHUNK
    dt = q.dtype
    gamma = 1.0 - 2.0 ** (-5.0 - jnp.arange(h, dtype=jnp.float32))
    log_g = jnp.log(gamma)
    idx = jnp.arange(CHUNK, dtype=jnp.float32)
    d_intra = jnp.exp(log_g[:, None, None] * jnp.abs(idx[:, None] - idx[None, :])).astype(dt)
    d_key = jnp.exp(log_g[:, None] * (CHUNK - 1.0 - idx)[None, :]).astype(dt)
    d_query = jnp.exp(log_g[:, None] * (idx + 1.0)[None, :]).astype(dt)
    d_chunk = jnp.exp(log_g * CHUNK).astype(dt)

    qc = (q * (dk ** -0.5)).reshape(b, n, CHUNK, h, dk)
    kc = k.reshape(b, n, CHUNK, h, dk)
    vc = v.reshape(b, n, CHUNK, h, dv)

    scores = jnp.einsum('bnihd,bnjhd->bnhij', qc, kc) * d_intra[None, None]
    intra = jnp.einsum('bnhij,bnjhe->bnihe', scores, vc)

    kv = jnp.einsum('bnjhd,hj,bnjhe->nbhde', kc, d_key, vc)

    def step(state, kv_n):
        return state * d_chunk[None, :, None, None] + kv_n, state

    init = jnp.zeros((b, h, dk, dv), dtype=kv.dtype)
    _, s_prev = lax.scan(step, init, kv)
    cross = jnp.einsum('bnihd,nbhde,hi->bnihe', qc, s_prev, d_query)
    return (intra + cross).reshape(b, s, h, dv)


def head_group_norm(o, gain):
    b, s, h, dv = o.shape
    of = o.astype(jnp.float32)
    mu = jnp.mean(of, axis=-1, keepdims=True)
    var = jnp.mean(jnp.square(of - mu), axis=-1, keepdims=True)
    y = ((of - mu) * lax.rsqrt(var + GN_EPS)).reshape(b, s, h * dv)
    return (y * gain.astype(jnp.float32)).astype(o.dtype)


def multiscale_pool(u, w_pool, scale):
    b, s, _ = u.shape
    ug = u.reshape(b, s, POOL_GROUPS, POOL_GC)
    cs = jnp.cumsum(ug.astype(jnp.float32), axis=1)
    cs = jnp.concatenate([jnp.zeros((b, 1, POOL_GROUPS, POOL_GC), jnp.float32), cs], axis=1)
    t = jnp.arange(s)
    win = jnp.array(POOL_WINDOWS, dtype=jnp.int32)
    lo = jnp.maximum(t[:, None] + 1 - win[None, :], 0)
    cnt = (t[:, None] + 1 - lo).astype(jnp.float32)
    cs_lo = cs[:, lo, jnp.arange(POOL_GROUPS)[None, :], :]
    mean = (cs[:, 1:] - cs_lo) / cnt[None, :, :, None]
    pooled = mean.astype(u.dtype) - ug
    y = jnp.einsum('bsgc,gcd->bsgd', pooled, w_pool).reshape(b, s, POOL_WIDTH)
    return y * scale


def hybrid_mixer(h, w_in, w_out, ret_gn_gain, pool_w, pool_scale):
    b, s, _ = h.shape
    p = h @ w_in
    o1 = RET_QKWIDTH
    o2 = o1 + RET_QKWIDTH
    o3 = o2 + RET_VWIDTH
    o4 = o3 + RET_VWIDTH
    q = p[..., :o1].reshape(b, s, RET_HEADS, RET_DK)
    k = p[..., o1:o2].reshape(b, s, RET_HEADS, RET_DK)
    v = p[..., o2:o3].reshape(b, s, RET_HEADS, RET_DV)
    g = p[..., o3:o4]
    u = p[..., o4:]
    pos = jnp.arange(s)
    q = rotary(q, pos)
    k = rotary(k, pos)
    ret = retention_chunkwise(q, k, v)
    ret = jax.nn.silu(g) * head_group_norm(ret, ret_gn_gain)
    pool = multiscale_pool(u, pool_w, pool_scale)
    return jnp.concatenate([ret, pool], axis=-1) @ w_out


def _fwd_setup_inputs(seed: int = 0) -> dict:
    key = jax.random.key(seed)
    ks = jax.random.split(key, 20)
    f32 = jnp.float32
    nrm = lambda k_, shape, fan: jax.random.normal(k_, shape, f32) * (fan ** -0.5)
    gain = lambda k_, shape: 1.0 + 0.02 * jax.random.normal(k_, shape, f32)
    L = DEPTH
    return {
        "x": jax.random.normal(ks[0], (BATCH, SEQ, D_MODEL), f32),
        "norm_ffn1": gain(ks[1], (L, D_MODEL)),
        "ffn1_gate": nrm(ks[2], (L, D_MODEL, D_FF), D_MODEL),
        "ffn1_up": nrm(ks[3], (L, D_MODEL, D_FF), D_MODEL),
        "ffn1_down": nrm(ks[4], (L, D_FF, D_MODEL), D_FF),
        "norm_mix": gain(ks[5], (L, D_MODEL)),
        "w_in": nrm(ks[6], (L, D_MODEL, IN_WIDTH), D_MODEL),
        "ret_gn_gain": gain(ks[7], (L, RET_VWIDTH)),
        "pool_w": nrm(ks[8], (L, POOL_GROUPS, POOL_GC, POOL_GC), POOL_GC),
        "pool_scale": gain(ks[9], (L, POOL_WIDTH)),
        "w_out": nrm(ks[10], (L, MIX_WIDTH, D_MODEL), MIX_WIDTH),
        "norm_ffn2": gain(ks[11], (L, D_MODEL)),
        "ffn2_gate": nrm(ks[12], (L, D_MODEL, D_FF), D_MODEL),
        "ffn2_up": nrm(ks[13], (L, D_MODEL, D_FF), D_MODEL),
        "ffn2_down": nrm(ks[14], (L, D_FF, D_MODEL), D_FF),
        "norm_final": gain(ks[15], (D_MODEL,)),
    }


def _fwd_reference(x, norm_ffn1, ffn1_gate, ffn1_up, ffn1_down, norm_mix, w_in,
              ret_gn_gain, pool_w, pool_scale, w_out, norm_ffn2, ffn2_gate,
              ffn2_up, ffn2_down, norm_final):
    for l in range(DEPTH):
        x = x + 0.5 * swiglu(rms_norm(x, norm_ffn1[l]), ffn1_gate[l], ffn1_up[l], ffn1_down[l])
        x = x + hybrid_mixer(rms_norm(x, norm_mix[l]), w_in[l], w_out[l],
                             ret_gn_gain[l], pool_w[l], pool_scale[l])
        x = x + 0.5 * swiglu(rms_norm(x, norm_ffn2[l]), ffn2_gate[l], ffn2_up[l], ffn2_down[l])
    return rms_norm(x, norm_final)


import jax as _jax
import jax.numpy as _jnp

TWIN_FORMAT = 'train_step'
FWD_PARAMS = ['x', 'norm_ffn1', 'ffn1_gate', 'ffn1_up', 'ffn1_down', 'norm_mix', 'w_in', 'ret_gn_gain', 'pool_w', 'pool_scale', 'w_out', 'norm_ffn2', 'ffn2_gate', 'ffn2_up', 'ffn2_down', 'norm_final']
TWIN_WEIGHTS = ['norm_ffn1', 'ffn1_gate', 'ffn1_up', 'ffn1_down', 'norm_mix', 'w_in', 'ret_gn_gain', 'pool_w', 'pool_scale', 'w_out', 'norm_ffn2', 'ffn2_gate', 'ffn2_up', 'ffn2_down', 'norm_final']
TWIN_DIFF_INPUT = 'x'
TWIN_INPUTS = ['x', 'norm_ffn1', 'ffn1_gate', 'ffn1_up', 'ffn1_down', 'norm_mix', 'w_in', 'ret_gn_gain', 'pool_w', 'pool_scale', 'w_out', 'norm_ffn2', 'ffn2_gate', 'ffn2_up', 'ffn2_down', 'norm_final', 'loss_target', 'm_norm_ffn1', 'm_ffn1_gate', 'm_ffn1_up', 'm_ffn1_down', 'm_norm_mix', 'm_w_in', 'm_ret_gn_gain', 'm_pool_w', 'm_pool_scale', 'm_w_out', 'm_norm_ffn2', 'm_ffn2_gate', 'm_ffn2_up', 'm_ffn2_down', 'm_norm_final', 'v_norm_ffn1', 'v_ffn1_gate', 'v_ffn1_up', 'v_ffn1_down', 'v_norm_mix', 'v_w_in', 'v_ret_gn_gain', 'v_pool_w', 'v_pool_scale', 'v_w_out', 'v_norm_ffn2', 'v_ffn2_gate', 'v_ffn2_up', 'v_ffn2_down', 'v_norm_final']
TWIN_OUTPUTS = ['loss', 'grad_x', 'grad_norm_ffn1', 'grad_ffn1_gate', 'grad_ffn1_up', 'grad_ffn1_down', 'grad_norm_mix', 'grad_w_in', 'grad_ret_gn_gain', 'grad_pool_w', 'grad_pool_scale', 'grad_w_out', 'grad_norm_ffn2', 'grad_ffn2_gate', 'grad_ffn2_up', 'grad_ffn2_down', 'grad_norm_final', 'delta_norm_ffn1', 'delta_ffn1_gate', 'delta_ffn1_up', 'delta_ffn1_down', 'delta_norm_mix', 'delta_w_in', 'delta_ret_gn_gain', 'delta_pool_w', 'delta_pool_scale', 'delta_w_out', 'delta_norm_ffn2', 'delta_ffn2_gate', 'delta_ffn2_up', 'delta_ffn2_down', 'delta_norm_final', 'new_m_norm_ffn1', 'new_m_ffn1_gate', 'new_m_ffn1_up', 'new_m_ffn1_down', 'new_m_norm_mix', 'new_m_w_in', 'new_m_ret_gn_gain', 'new_m_pool_w', 'new_m_pool_scale', 'new_m_w_out', 'new_m_norm_ffn2', 'new_m_ffn2_gate', 'new_m_ffn2_up', 'new_m_ffn2_down', 'new_m_norm_final', 'new_v_norm_ffn1', 'new_v_ffn1_gate', 'new_v_ffn1_up', 'new_v_ffn1_down', 'new_v_norm_mix', 'new_v_w_in', 'new_v_ret_gn_gain', 'new_v_pool_w', 'new_v_pool_scale', 'new_v_w_out', 'new_v_norm_ffn2', 'new_v_ffn2_gate', 'new_v_ffn2_up', 'new_v_ffn2_down', 'new_v_norm_final']
TWIN_LEAF_KINDS = {'loss': 'loss', 'grad_x': 'grad_x', 'grad_norm_ffn1': 'grad_w', 'grad_ffn1_gate': 'grad_w', 'grad_ffn1_up': 'grad_w', 'grad_ffn1_down': 'grad_w', 'grad_norm_mix': 'grad_w', 'grad_w_in': 'grad_w', 'grad_ret_gn_gain': 'grad_w', 'grad_pool_w': 'grad_w', 'grad_pool_scale': 'grad_w', 'grad_w_out': 'grad_w', 'grad_norm_ffn2': 'grad_w', 'grad_ffn2_gate': 'grad_w', 'grad_ffn2_up': 'grad_w', 'grad_ffn2_down': 'grad_w', 'grad_norm_final': 'grad_w', 'delta_norm_ffn1': 'delta_w', 'delta_ffn1_gate': 'delta_w', 'delta_ffn1_up': 'delta_w', 'delta_ffn1_down': 'delta_w', 'delta_norm_mix': 'delta_w', 'delta_w_in': 'delta_w', 'delta_ret_gn_gain': 'delta_w', 'delta_pool_w': 'delta_w', 'delta_pool_scale': 'delta_w', 'delta_w_out': 'delta_w', 'delta_norm_ffn2': 'delta_w', 'delta_ffn2_gate': 'delta_w', 'delta_ffn2_up': 'delta_w', 'delta_ffn2_down': 'delta_w', 'delta_norm_final': 'delta_w', 'new_m_norm_ffn1': 'new_m', 'new_m_ffn1_gate': 'new_m', 'new_m_ffn1_up': 'new_m', 'new_m_ffn1_down': 'new_m', 'new_m_norm_mix': 'new_m', 'new_m_w_in': 'new_m', 'new_m_ret_gn_gain': 'new_m', 'new_m_pool_w': 'new_m', 'new_m_pool_scale': 'new_m', 'new_m_w_out': 'new_m', 'new_m_norm_ffn2': 'new_m', 'new_m_ffn2_gate': 'new_m', 'new_m_ffn2_up': 'new_m', 'new_m_ffn2_down': 'new_m', 'new_m_norm_final': 'new_m', 'new_v_norm_ffn1': 'new_v', 'new_v_ffn1_gate': 'new_v', 'new_v_ffn1_up': 'new_v', 'new_v_ffn1_down': 'new_v', 'new_v_norm_mix': 'new_v', 'new_v_w_in': 'new_v', 'new_v_ret_gn_gain': 'new_v', 'new_v_pool_w': 'new_v', 'new_v_pool_scale': 'new_v', 'new_v_w_out': 'new_v', 'new_v_norm_ffn2': 'new_v', 'new_v_ffn2_gate': 'new_v', 'new_v_ffn2_up': 'new_v', 'new_v_ffn2_down': 'new_v', 'new_v_norm_final': 'new_v'}


def _forward(args):
    return _fwd_reference(*[args[k] for k in FWD_PARAMS])


def _output_shape():
    out = _jax.eval_shape(lambda: _forward(_fwd_setup_inputs(0)))
    return out.shape, out.dtype

N_MICROBATCH = 1
ADAM_LR = 0.001
ADAM_B1 = 0.9
ADAM_B2 = 0.999
ADAM_EPS = 1e-08
ADAM_WD = 0.01
ADAM_STEP = 10
PER_EXAMPLE_BATCH_AXIS = {'x': 0, 'loss_target': 0}
SHARED_INPUTS = []
_WEIGHT_DTYPES = {'norm_ffn1': _jnp.float32, 'ffn1_gate': _jnp.float32, 'ffn1_up': _jnp.float32, 'ffn1_down': _jnp.float32, 'norm_mix': _jnp.float32, 'w_in': _jnp.float32, 'ret_gn_gain': _jnp.float32, 'pool_w': _jnp.float32, 'pool_scale': _jnp.float32, 'w_out': _jnp.float32, 'norm_ffn2': _jnp.float32, 'ffn2_gate': _jnp.float32, 'ffn2_up': _jnp.float32, 'ffn2_down': _jnp.float32, 'norm_final': _jnp.float32}
MOMENT_SCALE = {'norm_ffn1': 1.388837e-01, 'ffn1_gate': 5.268181e-02, 'ffn1_up': 5.095401e-02, 'ffn1_down': 8.468882e-02, 'norm_mix': 2.200993e-01, 'w_in': 1.495332e-01, 'ret_gn_gain': 1.209163e-01, 'pool_w': 1.749874e-01, 'pool_scale': 2.197955e-01, 'w_out': 1.510851e-01, 'norm_ffn2': 8.542650e-02, 'ffn2_gate': 3.535541e-02, 'ffn2_up': 3.426810e-02, 'ffn2_down': 5.681174e-02, 'norm_final': 6.376574e+01}


def _to_microbatches(a, axis):
    t = _jnp.moveaxis(a, axis, 0)
    t = t.reshape((N_MICROBATCH, t.shape[0] // N_MICROBATCH) + t.shape[1:])
    return _jnp.moveaxis(t, 1, axis + 1)


def setup_inputs(seed: int = 0) -> dict:
    inp = _fwd_setup_inputs(seed)
    key = _jax.random.fold_in(_jax.random.key(seed), 7919)
    shape, _ = _output_shape()
    out = dict(inp)
    out["loss_target"] = _jax.random.normal(_jax.random.fold_in(key, 0), shape, _jnp.float32)
    for i, name in enumerate(TWIN_WEIGHTS):
        w = inp[name].astype(_jnp.float32)
        if MOMENT_SCALE is None:
            s = _jnp.sqrt(_jnp.mean(_jnp.square(w)) + 1e-30)
        else:
            s = MOMENT_SCALE[name]
        km, kv = _jax.random.split(_jax.random.fold_in(key, i + 1))
        out[name] = w
        out["m_" + name] = s * _jax.random.normal(km, w.shape, _jnp.float32)
        out["v_" + name] = (s * s) * _jax.random.uniform(kv, w.shape, _jnp.float32, 0.5, 1.5)
    if N_MICROBATCH > 1:
        for name, axis in PER_EXAMPLE_BATCH_AXIS.items():
            out[name] = _to_microbatches(out[name], axis)
    return {'x': out['x'], 'norm_ffn1': out['norm_ffn1'], 'ffn1_gate': out['ffn1_gate'], 'ffn1_up': out['ffn1_up'], 'ffn1_down': out['ffn1_down'], 'norm_mix': out['norm_mix'], 'w_in': out['w_in'], 'ret_gn_gain': out['ret_gn_gain'], 'pool_w': out['pool_w'], 'pool_scale': out['pool_scale'], 'w_out': out['w_out'], 'norm_ffn2': out['norm_ffn2'], 'ffn2_gate': out['ffn2_gate'], 'ffn2_up': out['ffn2_up'], 'ffn2_down': out['ffn2_down'], 'norm_final': out['norm_final'], 'loss_target': out['loss_target'], 'm_norm_ffn1': out['m_norm_ffn1'], 'm_ffn1_gate': out['m_ffn1_gate'], 'm_ffn1_up': out['m_ffn1_up'], 'm_ffn1_down': out['m_ffn1_down'], 'm_norm_mix': out['m_norm_mix'], 'm_w_in': out['m_w_in'], 'm_ret_gn_gain': out['m_ret_gn_gain'], 'm_pool_w': out['m_pool_w'], 'm_pool_scale': out['m_pool_scale'], 'm_w_out': out['m_w_out'], 'm_norm_ffn2': out['m_norm_ffn2'], 'm_ffn2_gate': out['m_ffn2_gate'], 'm_ffn2_up': out['m_ffn2_up'], 'm_ffn2_down': out['m_ffn2_down'], 'm_norm_final': out['m_norm_final'], 'v_norm_ffn1': out['v_norm_ffn1'], 'v_ffn1_gate': out['v_ffn1_gate'], 'v_ffn1_up': out['v_ffn1_up'], 'v_ffn1_down': out['v_ffn1_down'], 'v_norm_mix': out['v_norm_mix'], 'v_w_in': out['v_w_in'], 'v_ret_gn_gain': out['v_ret_gn_gain'], 'v_pool_w': out['v_pool_w'], 'v_pool_scale': out['v_pool_scale'], 'v_w_out': out['v_w_out'], 'v_norm_ffn2': out['v_norm_ffn2'], 'v_ffn2_gate': out['v_ffn2_gate'], 'v_ffn2_up': out['v_ffn2_up'], 'v_ffn2_down': out['v_ffn2_down'], 'v_norm_final': out['v_norm_final']}


def _loss(weights, diff, rest, loss_target):
    with _jax.named_scope("forward"):
        args = {**rest, TWIN_DIFF_INPUT: diff, **{k: w.astype(_WEIGHT_DTYPES[k]) for k, w in weights.items()}}
        y = _forward(args)
    with _jax.named_scope("loss_head"):
        err = _jnp.square(y.astype(_jnp.float32) - loss_target)
        return 0.5 * _jnp.sum(_jnp.mean(err, axis=-1)) if err.ndim else 0.5 * err


def _adamw(w, g, m, v):
    m = ADAM_B1 * m + (1.0 - ADAM_B1) * g
    v = ADAM_B2 * v + (1.0 - ADAM_B2) * _jnp.square(g)
    m_hat = m / (1.0 - ADAM_B1 ** ADAM_STEP)
    v_hat = v / (1.0 - ADAM_B2 ** ADAM_STEP)
    delta = -ADAM_LR * (m_hat / (_jnp.sqrt(v_hat) + ADAM_EPS) + ADAM_WD * w)
    return delta, m, v


def reference(x, norm_ffn1, ffn1_gate, ffn1_up, ffn1_down, norm_mix, w_in, ret_gn_gain, pool_w, pool_scale, w_out, norm_ffn2, ffn2_gate, ffn2_up, ffn2_down, norm_final, loss_target, m_norm_ffn1, m_ffn1_gate, m_ffn1_up, m_ffn1_down, m_norm_mix, m_w_in, m_ret_gn_gain, m_pool_w, m_pool_scale, m_w_out, m_norm_ffn2, m_ffn2_gate, m_ffn2_up, m_ffn2_down, m_norm_final, v_norm_ffn1, v_ffn1_gate, v_ffn1_up, v_ffn1_down, v_norm_mix, v_w_in, v_ret_gn_gain, v_pool_w, v_pool_scale, v_w_out, v_norm_ffn2, v_ffn2_gate, v_ffn2_up, v_ffn2_down, v_norm_final):
    given = dict(x=x, norm_ffn1=norm_ffn1, ffn1_gate=ffn1_gate, ffn1_up=ffn1_up, ffn1_down=ffn1_down, norm_mix=norm_mix, w_in=w_in, ret_gn_gain=ret_gn_gain, pool_w=pool_w, pool_scale=pool_scale, w_out=w_out, norm_ffn2=norm_ffn2, ffn2_gate=ffn2_gate, ffn2_up=ffn2_up, ffn2_down=ffn2_down, norm_final=norm_final, loss_target=loss_target, m_norm_ffn1=m_norm_ffn1, m_ffn1_gate=m_ffn1_gate, m_ffn1_up=m_ffn1_up, m_ffn1_down=m_ffn1_down, m_norm_mix=m_norm_mix, m_w_in=m_w_in, m_ret_gn_gain=m_ret_gn_gain, m_pool_w=m_pool_w, m_pool_scale=m_pool_scale, m_w_out=m_w_out, m_norm_ffn2=m_norm_ffn2, m_ffn2_gate=m_ffn2_gate, m_ffn2_up=m_ffn2_up, m_ffn2_down=m_ffn2_down, m_norm_final=m_norm_final, v_norm_ffn1=v_norm_ffn1, v_ffn1_gate=v_ffn1_gate, v_ffn1_up=v_ffn1_up, v_ffn1_down=v_ffn1_down, v_norm_mix=v_norm_mix, v_w_in=v_w_in, v_ret_gn_gain=v_ret_gn_gain, v_pool_w=v_pool_w, v_pool_scale=v_pool_scale, v_w_out=v_w_out, v_norm_ffn2=v_norm_ffn2, v_ffn2_gate=v_ffn2_gate, v_ffn2_up=v_ffn2_up, v_ffn2_down=v_ffn2_down, v_norm_final=v_norm_final)
    weights = {n: given[n] for n in TWIN_WEIGHTS}
    shared = {n: given[n] for n in SHARED_INPUTS}
    per_example = {n: given[n] for n in ['x']}
    grad_fn = _jax.value_and_grad(_loss, argnums=(0, 1))

    def one_microbatch(ex, loss_target):
        ex = dict(ex)
        diff = ex.pop(TWIN_DIFF_INPUT)
        return grad_fn(weights, diff, {**shared, **ex}, loss_target)

    if N_MICROBATCH == 1:
        loss, (grad_w, grad_x) = one_microbatch(per_example, given["loss_target"])
    else:
        def body(carry, xs):
            loss_sum, grad_sum = carry
            l_k, (gw_k, gx_k) = one_microbatch(xs[0], xs[1])
            with _jax.named_scope("update"):
                return (loss_sum + l_k, _jax.tree.map(_jnp.add, grad_sum, gw_k)), gx_k

        init = (_jnp.zeros((), _jnp.float32), _jax.tree.map(_jnp.zeros_like, weights))
        (loss, grad_w), grad_x = _jax.lax.scan(body, init, (per_example, given["loss_target"]))
    with _jax.named_scope("update"):
        delta_w, new_m, new_v = {}, {}, {}
        for n in TWIN_WEIGHTS:
            delta_w[n], new_m[n], new_v[n] = _adamw(weights[n], grad_w[n], given["m_" + n], given["v_" + n])
    return (loss, grad_x, *[grad_w[n] for n in TWIN_WEIGHTS], *[delta_w[n] for n in TWIN_WEIGHTS],
            *[new_m[n] for n in TWIN_WEIGHTS], *[new_v[n] for n in TWIN_WEIGHTS])
```

```python
import functools
import math

import jax
import jax.numpy as jnp
from jax import lax
from jax.experimental import pallas as pl
from jax.experimental.pallas import tpu as pltpu

F32 = jnp.float32
BF16 = jnp.bfloat16
MESH = pl.DeviceIdType.MESH

N_CHIPS = 4
N_DEV = 8
CHUNK = 64
RET_HEADS = 4
RET_DK = 64
RET_DV = 128
QK_W = RET_HEADS * RET_DK
V_W = RET_HEADS * RET_DV
POOL_WINDOWS = (2, 4, 8, 16)
POOL_GC = 128
ROPE_BASE = 10000.0
RMS_EPS = 1e-6
GN_EPS = 1e-5
ADAM_LR = 0.001
ADAM_B1 = 0.9
ADAM_B2 = 0.999
ADAM_EPS = 1e-08
ADAM_WD = 0.01
ADAM_STEP = 10
RET_BLOCK = 256
VMEM_LIMIT = 56 * 1024 * 1024

NN = (((1,), (0,)), ((), ()))
NT = (((1,), (1,)), ((), ()))
TN = (((0,), (0,)), ((), ()))


def _dot(a, b, dims=NN):
    return lax.dot_general(a, b, dims, preferred_element_type=F32)


def _sigmoid(x):
    return 1.0 / (1.0 + jnp.exp(-x))


def _params(n_grid):
    return pltpu.CompilerParams(dimension_semantics=("arbitrary",) * n_grid, vmem_limit_bytes=VMEM_LIMIT)


def _rms_parts(x):
    r = lax.rsqrt(jnp.mean(x * x, axis=-1, keepdims=True) + RMS_EPS)
    return r, x * r


def _rms_bwd(dn, xhat, r, gain, dres):
    dxh = dn * gain
    return dres + r * (dxh - xhat * jnp.mean(dxh * xhat, axis=-1, keepdims=True))


def ffn_fwd(x, gain, wg, wu, wd, tm, name):
    T, D = x.shape
    J, _, FS = wg.shape

    def body(x_ref, g_ref, wg_ref, wu_ref, wd_ref, xo_ref, a_ref, b_ref, n_sc, acc):
        j = pl.program_id(1)

        @pl.when(j == 0)
        def _():
            _, xhat = _rms_parts(x_ref[...])
            n_sc[...] = (xhat * g_ref[...]).astype(BF16)
            acc[...] = jnp.zeros_like(acc)

        n = n_sc[...]
        a = _dot(n, wg_ref[...])
        b = _dot(n, wu_ref[...])
        a_ref[...] = a.astype(BF16)
        b_ref[...] = b.astype(BF16)
        h = (a * _sigmoid(a) * b).astype(BF16)
        acc[...] += _dot(h, wd_ref[...])

        @pl.when(j == J - 1)
        def _():
            xo_ref[...] = x_ref[...] + 0.5 * acc[...]

    return pl.pallas_call(
        body, name=name, grid=(T // tm, J),
        in_specs=[pl.BlockSpec((tm, D), lambda i, j: (i, 0)),
                  pl.BlockSpec((1, D), lambda i, j: (0, 0)),
                  pl.BlockSpec((None, D, FS), lambda i, j: (j, 0, 0)),
                  pl.BlockSpec((None, D, FS), lambda i, j: (j, 0, 0)),
                  pl.BlockSpec((None, FS, D), lambda i, j: (j, 0, 0))],
        out_specs=[pl.BlockSpec((tm, D), lambda i, j: (i, 0)),
                   pl.BlockSpec((None, tm, FS), lambda i, j: (j, i, 0)),
                   pl.BlockSpec((None, tm, FS), lambda i, j: (j, i, 0))],
        out_shape=[jax.ShapeDtypeStruct((T, D), F32),
                   jax.ShapeDtypeStruct((J, T, FS), BF16),
                   jax.ShapeDtypeStruct((J, T, FS), BF16)],
        scratch_shapes=[pltpu.VMEM((tm, D), BF16), pltpu.VMEM((tm, D), F32)],
        compiler_params=_params(2),
    )(x, gain, wg, wu, wd)


def ffn_bwd(x, dout, gain, a_s, b_s, wg, wu, wd, tm, name):
    T, D = x.shape
    J, _, FS = wg.shape
    nt = T // tm

    def body(x_ref, do_ref, g_ref, a_ref, b_ref, wg_ref, wu_ref, wd_ref,
             dx_ref, da_ref, db_ref, h_ref, n_ref, dgain_ref, df_sc, acc):
        i = pl.program_id(0)
        j = pl.program_id(1)

        @pl.when(jnp.logical_and(i == 0, j == 0))
        def _():
            dgain_ref[...] = jnp.zeros_like(dgain_ref)

        @pl.when(j == 0)
        def _():
            _, xhat = _rms_parts(x_ref[...])
            n_ref[...] = (xhat * g_ref[...]).astype(BF16)
            df_sc[...] = (0.5 * do_ref[...]).astype(BF16)
            acc[...] = jnp.zeros_like(acc)

        a = a_ref[...].astype(F32)
        b = b_ref[...].astype(F32)
        dh = _dot(df_sc[...], wd_ref[...], NT)
        sg = _sigmoid(a)
        si = a * sg
        da = (dh * b * (sg * (1.0 + a * (1.0 - sg)))).astype(BF16)
        db = (dh * si).astype(BF16)
        da_ref[...] = da
        db_ref[...] = db
        h_ref[...] = (si * b).astype(BF16)
        acc[...] += _dot(da, wg_ref[...], NT) + _dot(db, wu_ref[...], NT)

        @pl.when(j == J - 1)
        def _():
            r, xhat = _rms_parts(x_ref[...])
            dn = acc[...]
            dgain_ref[...] += jnp.sum(dn * xhat, axis=0, keepdims=True)
            dx_ref[...] = _rms_bwd(dn, xhat, r, g_ref[...], do_ref[...])

    tok = pl.BlockSpec((tm, D), lambda i, j: (i, 0))
    sh = pl.BlockSpec((None, tm, FS), lambda i, j: (j, i, 0))
    return pl.pallas_call(
        body, name=name, grid=(nt, J),
        in_specs=[tok, tok, pl.BlockSpec((1, D), lambda i, j: (0, 0)), sh, sh,
                  pl.BlockSpec((None, D, FS), lambda i, j: (j, 0, 0)),
                  pl.BlockSpec((None, D, FS), lambda i, j: (j, 0, 0)),
                  pl.BlockSpec((None, FS, D), lambda i, j: (j, 0, 0))],
        out_specs=[tok, sh, sh, sh, tok, pl.BlockSpec((1, D), lambda i, j: (0, 0))],
        out_shape=[jax.ShapeDtypeStruct((T, D), F32),
                   jax.ShapeDtypeStruct((J, T, FS), BF16),
                   jax.ShapeDtypeStruct((J, T, FS), BF16),
                   jax.ShapeDtypeStruct((J, T, FS), BF16),
                   jax.ShapeDtypeStruct((T, D), BF16),
                   jax.ShapeDtypeStruct((1, D), F32)],
        scratch_shapes=[pltpu.VMEM((tm, D), BF16), pltpu.VMEM((tm, D), F32)],
        compiler_params=_params(2),
    )(x, dout, gain, a_s, b_s, wg, wu, wd)


def tn_matmul(a, b, tk, name, b_scale=None):
    Ja, T, M = a.shape
    Jb, _, N = b.shape
    J = max(Ja, Jb)
    nk = T // tk

    def body(a_ref, b_ref, o_ref):
        k = pl.program_id(1)

        @pl.when(k == 0)
        def _():
            o_ref[...] = jnp.zeros_like(o_ref)

        bv = b_ref[...]
        if b_scale is not None:
            bv = bv * b_scale
        o_ref[...] += _dot(a_ref[...].astype(BF16), bv.astype(BF16), TN)

    a_map = (lambda j, k: (j, k, 0)) if Ja > 1 else (lambda j, k: (0, k, 0))
    b_map = (lambda j, k: (j, k, 0)) if Jb > 1 else (lambda j, k: (0, k, 0))
    return pl.pallas_call(
        body, name=name, grid=(J, nk),
        in_specs=[pl.BlockSpec((None, tk, M), a_map), pl.BlockSpec((None, tk, N), b_map)],
        out_specs=pl.BlockSpec((None, M, N), lambda j, k: (j, 0, 0)),
        out_shape=jax.ShapeDtypeStruct((J, M, N), F32),
        compiler_params=_params(2),
    )(a, b)


def in_proj(x, gain, win, tm, name):
    T, D = x.shape
    J, _, W = win.shape

    def body(x_ref, g_ref, w_ref, p_ref, n_sc):
        j = pl.program_id(1)

        @pl.when(j == 0)
        def _():
            _, xhat = _rms_parts(x_ref[...])
            n_sc[...] = (xhat * g_ref[...]).astype(BF16)

        p_ref[...] = _dot(n_sc[...], w_ref[...])

    return pl.pallas_call(
        body, name=name, grid=(T // tm, J),
        in_specs=[pl.BlockSpec((tm, D), lambda i, j: (i, 0)),
                  pl.BlockSpec((1, D), lambda i, j: (0, 0)),
                  pl.BlockSpec((None, D, W), lambda i, j: (j, 0, 0))],
        out_specs=pl.BlockSpec((None, tm, W), lambda i, j: (j, i, 0)),
        out_shape=jax.ShapeDtypeStruct((J, T, W), F32),
        scratch_shapes=[pltpu.VMEM((tm, D), BF16)],
        compiler_params=_params(2),
    )(x, gain, win)


def mixer_constants(S):
    B = RET_BLOCK
    half = RET_DK // 2
    freqs = ROPE_BASE ** (-jnp.arange(half, dtype=F32) * 2.0 / RET_DK)
    ang = jnp.arange(S, dtype=F32)[:, None] * freqs[None, :]
    cos = jnp.tile(jnp.cos(ang), (1, 2 * RET_HEADS))
    sin = jnp.tile(jnp.concatenate([-jnp.sin(ang), jnp.sin(ang)], axis=1), (1, RET_HEADS))
    gamma = 1.0 - 2.0 ** (-5.0 - jnp.arange(RET_HEADS, dtype=F32))
    log_g = jnp.log(gamma)
    idx = jnp.arange(B, dtype=F32)
    ci = jnp.arange(B) // CHUNK
    dist = jnp.abs(idx[:, None] - idx[None, :])
    m_intra = jnp.exp(log_g[:, None, None] * dist[None]) * (ci[None, :] <= ci[:, None])[None].astype(F32)
    lg_lane = jnp.repeat(log_g, RET_DK)
    d_q = jnp.exp(lg_lane[None, :] * (idx[:, None] + 1.0))
    d_k = jnp.exp(lg_lane[None, :] * (B - 1.0 - idx[:, None]))
    d_s = jnp.broadcast_to(jnp.exp(lg_lane * B)[:, None], (QK_W, V_W))
    bm = (jnp.arange(QK_W)[:, None] // RET_DK == jnp.arange(V_W)[None, :] // RET_DV).astype(F32)
    win = jnp.repeat(jnp.array(POOL_WINDOWS, F32), POOL_GC)
    invc = 1.0 / jnp.minimum(jnp.arange(S, dtype=F32)[:, None] + 1.0, win[None, :])
    return dict(cos=cos, sin=sin, m=m_intra, dq=d_q, dk=d_k, ds=d_s, bm=bm, invc=invc)


def _swap_halves(x):
    w = x.shape[1]
    lane = lax.broadcasted_iota(jnp.int32, x.shape, 1)
    return jnp.where((lane % RET_DK) < RET_DK // 2, pltpu.roll(x, w - RET_DK // 2, 1), pltpu.roll(x, RET_DK // 2, 1))


def _rot(x, cos, sin):
    return x * cos + _swap_halves(x) * sin


def _rot_t(d, cos, sin):
    return d * cos + _swap_halves(d * sin)


def _lane_groups(parts):
    return jnp.concatenate([p[:, POOL_GC * g:POOL_GC * (g + 1)] for g, p in enumerate(parts)], axis=1)


def _head_mask(shape, h):
    lane = lax.broadcasted_iota(jnp.int32, shape, 1)
    return (lane // RET_DK) == h


def _group_norm(o):
    yh, rs = [], []
    for h in range(RET_HEADS):
        oh = o[:, RET_DV * h:RET_DV * (h + 1)]
        xc = oh - jnp.mean(oh, axis=-1, keepdims=True)
        r = lax.rsqrt(jnp.mean(xc * xc, axis=-1, keepdims=True) + GN_EPS)
        yh.append(xc * r)
        rs.append(r)
    return yh, rs


def mixer_fwd(p, x1, consts, gn_gain, pool_w, pool_scale, wout, n_seq, S, name):
    _, T, _ = p.shape
    D = x1.shape[1]
    B = RET_BLOCK
    nb = S // B

    def body(p_ref, x1_ref, cos_ref, sin_ref, invc_ref, m_ref, dq_ref, dk_ref, ds_ref, bm_ref,
             gain_ref, pw_ref, sc_ref, wout_ref,
             x2_ref, m_out, o_out, st_out, pooled_out, state, prev_u):
        blk = pl.program_id(1)

        @pl.when(blk == 0)
        def _():
            state[...] = jnp.zeros_like(state)
            prev_u[...] = jnp.zeros_like(prev_u)

        qk = p_ref[0]
        v = p_ref[1]
        g = p_ref[2]
        u = p_ref[3]
        cos = cos_ref[...]
        sin = sin_ref[...]
        qr = _rot(qk[:, :QK_W], cos, sin) * (RET_DK ** -0.5)
        kr = _rot(qk[:, QK_W:], cos, sin)
        qb = qr.astype(BF16)
        kb = kr.astype(BF16)
        vb = v.astype(BF16)
        st = state[...]
        st_out[...] = st
        cross = _dot((qr * dq_ref[...]).astype(BF16), st.astype(BF16))
        o_parts = []
        for h in range(RET_HEADS):
            qm = jnp.where(_head_mask(qb.shape, h), qb, jnp.zeros_like(qb))
            sc = (_dot(qm, kb, NT) * m_ref[h]).astype(BF16)
            o_parts.append(_dot(sc, vb[:, RET_DV * h:RET_DV * (h + 1)]) + cross[:, RET_DV * h:RET_DV * (h + 1)])
        o = jnp.concatenate(o_parts, axis=1)
        o_out[...] = o
        kv = _dot((kr * dk_ref[...]).astype(BF16), vb, TN)
        state[...] = st * ds_ref[...] + kv * bm_ref[...]

        yh, _ = _group_norm(o)
        r = g * _sigmoid(g) * (jnp.concatenate(yh, axis=1) * gain_ref[...])

        ext = jnp.concatenate([prev_u[...], u], axis=0)
        sums = []
        run = ext
        for k in (1, 2, 4, 8):
            run = run + pltpu.roll(run, k, 0)
            sums.append(run[B:, :])
        prev_u[...] = u
        pooled = (_lane_groups(sums) * invc_ref[...] - u).astype(BF16)
        pooled_out[...] = pooled
        yp = [_dot(pooled[:, POOL_GC * gi:POOL_GC * (gi + 1)], pw_ref[gi].astype(BF16)) for gi in range(len(POOL_WINDOWS))]
        s = jnp.concatenate(yp, axis=1) * sc_ref[...]
        m = jnp.concatenate([r, s], axis=1).astype(BF16)
        m_out[...] = m
        x2_ref[...] = x1_ref[...] + _dot(m, wout_ref[...])

    tokmap = lambda b, k: (b * nb + k, 0)
    seqmap = lambda b, k: (k, 0)
    const2 = lambda b, k: (0, 0)
    const3 = lambda b, k: (0, 0, 0)
    return pl.pallas_call(
        body, name=name, grid=(n_seq, nb),
        in_specs=[pl.BlockSpec((4, B, V_W), lambda b, k: (0, b * nb + k, 0)),
                  pl.BlockSpec((B, D), tokmap),
                  pl.BlockSpec((B, QK_W), seqmap), pl.BlockSpec((B, QK_W), seqmap), pl.BlockSpec((B, V_W), seqmap),
                  pl.BlockSpec((RET_HEADS, B, B), const3),
                  pl.BlockSpec((B, QK_W), const2), pl.BlockSpec((B, QK_W), const2),
                  pl.BlockSpec((QK_W, V_W), const2), pl.BlockSpec((QK_W, V_W), const2),
                  pl.BlockSpec((1, V_W), const2), pl.BlockSpec((4, POOL_GC, POOL_GC), const3),
                  pl.BlockSpec((1, V_W), const2), pl.BlockSpec((D, D), const2)],
        out_specs=[pl.BlockSpec((B, D), tokmap), pl.BlockSpec((B, D), tokmap), pl.BlockSpec((B, V_W), tokmap),
                   pl.BlockSpec((None, QK_W, V_W), lambda b, k: (b * nb + k, 0, 0)),
                   pl.BlockSpec((B, V_W), tokmap)],
        out_shape=[jax.ShapeDtypeStruct((T, D), F32), jax.ShapeDtypeStruct((T, D), BF16),
                   jax.ShapeDtypeStruct((T, V_W), F32), jax.ShapeDtypeStruct((T // B, QK_W, V_W), F32),
                   jax.ShapeDtypeStruct((T, V_W), BF16)],
        scratch_shapes=[pltpu.VMEM((QK_W, V_W), F32), pltpu.VMEM((B, V_W), F32)],
        compiler_params=_params(2),
    )(p, x1, consts["cos"], consts["sin"], consts["invc"], consts["m"], consts["dq"], consts["dk"],
      consts["ds"], consts["bm"], gn_gain, pool_w, pool_scale, wout)


def mixer_bwd(dx2, x1, p, o_s, st_s, pooled_s, consts, gmix, gn_gain, pool_w, pool_scale, wout, win, n_seq, S, name):
    T, D = dx2.shape
    B = RET_BLOCK
    nb = S // B
    NG = len(POOL_WINDOWS)

    def body(dx2_ref, x1_ref, p_ref, o_ref, st_ref, pooled_ref, cos_ref, sin_ref, invc_ref, m_ref, dq_ref, dk_ref,
             ds_ref, bm_ref, gmix_ref, gain_ref, pw_ref, sc_ref, wout_ref, win_ref,
             dx1_ref, dp_ref, n2_ref, dgmix_ref, dgain_ref, dscale_ref, dpw_ref, gstate, next_e):
        b = pl.program_id(0)
        blk = pl.program_id(1)

        @pl.when(jnp.logical_and(b == 0, blk == 0))
        def _():
            dgmix_ref[...] = jnp.zeros_like(dgmix_ref)
            dgain_ref[...] = jnp.zeros_like(dgain_ref)
            dscale_ref[...] = jnp.zeros_like(dscale_ref)
            dpw_ref[...] = jnp.zeros_like(dpw_ref)

        @pl.when(blk == 0)
        def _():
            gstate[...] = jnp.zeros_like(gstate)
            next_e[...] = jnp.zeros_like(next_e)

        dx2v = dx2_ref[...]
        dm = _dot(dx2v.astype(BF16), wout_ref[...], NT)
        dr = dm[:, :V_W]
        dsv = dm[:, V_W:]

        pooled = pooled_ref[...]
        scale = sc_ref[...]
        dyp = (dsv * scale).astype(BF16)
        yp, dpl = [], []
        for gi in range(NG):
            sl = slice(POOL_GC * gi, POOL_GC * (gi + 1))
            pwb = pw_ref[gi].astype(BF16)
            yp.append(_dot(pooled[:, sl], pwb))
            dpw_ref[gi] += _dot(pooled[:, sl], dyp[:, sl], TN)
            dpl.append(_dot(dyp[:, sl], pwb, NT))
        dscale_ref[...] += jnp.sum(dsv * jnp.concatenate(yp, axis=1), axis=0, keepdims=True)
        dpooled = jnp.concatenate(dpl, axis=1)
        e = dpooled * invc_ref[...]
        ext = jnp.concatenate([e, next_e[...]], axis=0)
        sums = []
        run = ext
        for k in (1, 2, 4, 8):
            run = run + pltpu.roll(run, 2 * B - k, 0)
            sums.append(run[:B, :])
        next_e[...] = e
        du = _lane_groups(sums) - dpooled

        qk = p_ref[0]
        v = p_ref[1]
        g = p_ref[2]
        sg = _sigmoid(g)
        si = g * sg
        yh, rs = _group_norm(o_ref[...])
        yhat = jnp.concatenate(yh, axis=1)
        gain = gain_ref[...]
        dg = dr * (yhat * gain) * (sg * (1.0 + g * (1.0 - sg)))
        dy = dr * si
        dgain_ref[...] += jnp.sum(dy * yhat, axis=0, keepdims=True)
        dyh = dy * gain
        do_parts = []
        for h in range(RET_HEADS):
            sl = slice(RET_DV * h, RET_DV * (h + 1))
            dh_ = dyh[:, sl]
            m1 = jnp.mean(dh_, axis=-1, keepdims=True)
            m2 = jnp.mean(dh_ * yh[h], axis=-1, keepdims=True)
            do_parts.append(rs[h] * (dh_ - m1 - yh[h] * m2))
        dob = jnp.concatenate(do_parts, axis=1).astype(BF16)

        cos = cos_ref[...]
        sin = sin_ref[...]
        qr = _rot(qk[:, :QK_W], cos, sin) * (RET_DK ** -0.5)
        kr = _rot(qk[:, QK_W:], cos, sin)
        qb = qr.astype(BF16)
        kb = kr.astype(BF16)
        vb = v.astype(BF16)
        dqd = dq_ref[...]
        dkd = dk_ref[...]
        stb = st_ref[...].astype(BF16)
        gs = gstate[...]
        gb = gs.astype(BF16)
        dqs = _dot(dob, stb, NT) * dqd
        dkr = _dot(vb, gb, NT) * dkd
        dv_cross = _dot((kr * dkd).astype(BF16), gb)
        ds_cross = _dot((qr * dqd).astype(BF16), dob, TN) * bm_ref[...]
        gstate[...] = ds_cross + gs * ds_ref[...]
        dv_parts = []
        for h in range(RET_HEADS):
            sl = slice(RET_DV * h, RET_DV * (h + 1))
            hm = _head_mask(qb.shape, h)
            qm = jnp.where(hm, qb, jnp.zeros_like(qb))
            mh = m_ref[h]
            sc = (_dot(qm, kb, NT) * mh).astype(BF16)
            dsc = (_dot(dob[:, sl], vb[:, sl], NT) * mh).astype(BF16)
            dqs = dqs + jnp.where(hm, _dot(dsc, kb), 0.0)
            dkr = dkr + jnp.where(hm, _dot(dsc, qb, TN), 0.0)
            dv_parts.append(_dot(sc, dob[:, sl], TN) + dv_cross[:, sl])
        dq = _rot_t(dqs * (RET_DK ** -0.5), cos, sin)
        dk = _rot_t(dkr, cos, sin)
        dp = [jnp.concatenate([dq, dk], axis=1).astype(BF16), jnp.concatenate(dv_parts, axis=1).astype(BF16),
              dg.astype(BF16), du.astype(BF16)]
        dn = jnp.zeros((B, D), F32)
        for jj in range(4):
            dp_ref[jj] = dp[jj]
            dn = dn + _dot(dp[jj], win_ref[jj], NT)

        x1v = x1_ref[...]
        r, xhat = _rms_parts(x1v)
        gm = gmix_ref[...]
        n2_ref[...] = (xhat * gm).astype(BF16)
        dgmix_ref[...] += jnp.sum(dn * xhat, axis=0, keepdims=True)
        dx1_ref[...] = _rms_bwd(dn, xhat, r, gm, dx2v)

    rev = lambda b, k: (b * nb + (nb - 1 - k), 0)
    seqrev = lambda b, k: (nb - 1 - k, 0)
    const2 = lambda b, k: (0, 0)
    const3 = lambda b, k: (0, 0, 0)
    return pl.pallas_call(
        body, name=name, grid=(n_seq, nb),
        in_specs=[pl.BlockSpec((B, D), rev), pl.BlockSpec((B, D), rev),
                  pl.BlockSpec((4, B, V_W), lambda b, k: (0, b * nb + (nb - 1 - k), 0)),
                  pl.BlockSpec((B, V_W), rev),
                  pl.BlockSpec((None, QK_W, V_W), lambda b, k: (b * nb + (nb - 1 - k), 0, 0)),
                  pl.BlockSpec((B, V_W), rev),
                  pl.BlockSpec((B, QK_W), seqrev), pl.BlockSpec((B, QK_W), seqrev), pl.BlockSpec((B, V_W), seqrev),
                  pl.BlockSpec((RET_HEADS, B, B), const3),
                  pl.BlockSpec((B, QK_W), const2), pl.BlockSpec((B, QK_W), const2),
                  pl.BlockSpec((QK_W, V_W), const2), pl.BlockSpec((QK_W, V_W), const2),
                  pl.BlockSpec((1, D), const2), pl.BlockSpec((1, V_W), const2),
                  pl.BlockSpec((NG, POOL_GC, POOL_GC), const3), pl.BlockSpec((1, V_W), const2),
                  pl.BlockSpec((D, D), const2), pl.BlockSpec((4, D, V_W), const3)],
        out_specs=[pl.BlockSpec((B, D), rev),
                   pl.BlockSpec((4, B, V_W), lambda b, k: (0, b * nb + (nb - 1 - k), 0)),
                   pl.BlockSpec((B, D), rev),
                   pl.BlockSpec((1, D), const2), pl.BlockSpec((1, V_W), const2), pl.BlockSpec((1, V_W), const2),
                   pl.BlockSpec((NG, POOL_GC, POOL_GC), const3)],
        out_shape=[jax.ShapeDtypeStruct((T, D), F32), jax.ShapeDtypeStruct((4, T, V_W), BF16),
                   jax.ShapeDtypeStruct((T, D), BF16),
                   jax.ShapeDtypeStruct((1, D), F32), jax.ShapeDtypeStruct((1, V_W), F32),
                   jax.ShapeDtypeStruct((1, V_W), F32), jax.ShapeDtypeStruct((NG, POOL_GC, POOL_GC), F32)],
        scratch_shapes=[pltpu.VMEM((QK_W, V_W), F32), pltpu.VMEM((B, V_W), F32)],
        compiler_params=_params(2),
    )(dx2, x1, p, o_s, st_s, pooled_s, consts["cos"], consts["sin"], consts["invc"], consts["m"], consts["dq"],
      consts["dk"], consts["ds"], consts["bm"], gmix, gn_gain, pool_w, pool_scale, wout, win)


def loss_bwd(x3, tgt, gain, tm, name):
    T, D = x3.shape

    def body(x_ref, t_ref, g_ref, dx_ref, lacc_ref, dg_ref):
        @pl.when(pl.program_id(0) == 0)
        def _():
            lacc_ref[...] = jnp.zeros_like(lacc_ref)
            dg_ref[...] = jnp.zeros_like(dg_ref)

        r, xhat = _rms_parts(x_ref[...])
        gv = g_ref[...]
        err = xhat * gv - t_ref[...]
        lacc_ref[...] += jnp.sum(err * err, axis=0, keepdims=True)
        dy = err * (1.0 / D)
        dg_ref[...] += jnp.sum(dy * xhat, axis=0, keepdims=True)
        dx_ref[...] = _rms_bwd(dy, xhat, r, gv, 0.0)

    tok = pl.BlockSpec((tm, D), lambda i: (i, 0))
    vec = pl.BlockSpec((1, D), lambda i: (0, 0))
    return pl.pallas_call(
        body, name=name, grid=(T // tm,),
        in_specs=[tok, tok, vec], out_specs=[tok, vec, vec],
        out_shape=[jax.ShapeDtypeStruct((T, D), F32), jax.ShapeDtypeStruct((1, D), F32), jax.ShapeDtypeStruct((1, D), F32)],
        compiler_params=_params(1),
    )(x3, tgt, gain)


def adamw(w, g, m, v, name):
    R, C = w.shape
    tr = R
    for cand in (256, 176, 128, 184, 8):
        if R % cand == 0:
            tr = cand
            break

    def body(w_ref, g_ref, m_ref, v_ref, d_ref, nm_ref, nv_ref):
        gv = g_ref[...]
        m2 = ADAM_B1 * m_ref[...] + (1.0 - ADAM_B1) * gv
        v2 = ADAM_B2 * v_ref[...] + (1.0 - ADAM_B2) * (gv * gv)
        m_hat = m2 / (1.0 - ADAM_B1 ** ADAM_STEP)
        v_hat = v2 / (1.0 - ADAM_B2 ** ADAM_STEP)
        d_ref[...] = -ADAM_LR * (m_hat / (jnp.sqrt(v_hat) + ADAM_EPS) + ADAM_WD * w_ref[...])
        nm_ref[...] = m2
        nv_ref[...] = v2

    blk = pl.BlockSpec((tr, C), lambda i: (i, 0))
    sds = jax.ShapeDtypeStruct((R, C), F32)
    return pl.pallas_call(
        body, name=name, grid=(R // tr,), in_specs=[blk] * 4, out_specs=[blk] * 3, out_shape=[sds] * 3,
        compiler_params=_params(1),
    )(w, g, m, v)


def _place():
    x, y, c = lax.axis_index("x"), lax.axis_index("y"), lax.axis_index("c")
    other_chips = [(1 - x, y), (x, 1 - y), (1 - x, 1 - y)]
    return x, y, c, other_chips


ANY = pl.BlockSpec(memory_space=pl.ANY)


def gather_weights(shards):
    n = len(shards)

    def body(*refs):
        ins, outs = refs[:n], refs[n:2 * n]
        loc_sem, ici_send, ici_recv, d2d_send, d2d_recv = refs[2 * n:]
        x, y, c, chips = _place()
        sibling = (x, y, 1 - c)
        mine = 2 * x + y
        started = []
        for w in range(n):
            cp = pltpu.make_async_copy(ins[w], outs[w].at[mine], loc_sem.at[w])
            cp.start()
            started.append(cp)
        sends = []
        for w in range(n):
            for k, (px, py) in enumerate(chips):
                cp = pltpu.make_async_remote_copy(
                    src_ref=ins[w].at[c], dst_ref=outs[w].at[mine, c],
                    send_sem=ici_send.at[w, k], recv_sem=ici_recv.at[w, k],
                    device_id=(px, py, c), device_id_type=MESH)
                cp.start()
                sends.append(cp)
        for w in range(n):
            for k, (px, py) in enumerate(chips):
                landed = outs[w].at[2 * px + py, c]
                pltpu.make_async_remote_copy(
                    src_ref=landed, dst_ref=landed, send_sem=ici_send.at[w, k], recv_sem=ici_recv.at[w, k],
                    device_id=(px, py, c), device_id_type=MESH).wait_recv()
                cp = pltpu.make_async_remote_copy(
                    src_ref=landed, dst_ref=landed, send_sem=d2d_send.at[w, k], recv_sem=d2d_recv.at[w, k],
                    device_id=sibling, device_id_type=MESH)
                cp.start()
                sends.append(cp)
        for w in range(n):
            for k, (px, py) in enumerate(chips):
                landed = outs[w].at[2 * px + py, 1 - c]
                pltpu.make_async_remote_copy(
                    src_ref=landed, dst_ref=landed, send_sem=d2d_send.at[w, k], recv_sem=d2d_recv.at[w, k],
                    device_id=sibling, device_id_type=MESH).wait_recv()
        for cp in sends:
            cp.wait_send()
        for cp in started:
            cp.wait()

    return pl.pallas_call(
        body, name="gather_weights",
        in_specs=[ANY] * n, out_specs=[ANY] * n,
        out_shape=[jax.ShapeDtypeStruct((N_CHIPS,) + s.shape, s.dtype) for s in shards],
        scratch_shapes=[pltpu.SemaphoreType.DMA((n,)), pltpu.SemaphoreType.DMA((n, 3)), pltpu.SemaphoreType.DMA((n, 3)),
                        pltpu.SemaphoreType.DMA((n, 3)), pltpu.SemaphoreType.DMA((n, 3))],
    )(*shards)


def send_to_sibling_other_half(grads):
    n = len(grads)

    def body(*refs):
        ins, outs = refs[:n], refs[n:2 * n]
        send_sem, recv_sem = refs[2 * n:]
        x, y, c, _ = _place()
        cps = []
        for w in range(n):
            for j in range(N_CHIPS):
                cp = pltpu.make_async_remote_copy(
                    src_ref=ins[w].at[j, 1 - c], dst_ref=outs[w].at[j],
                    send_sem=send_sem.at[w, j], recv_sem=recv_sem.at[w, j],
                    device_id=(x, y, 1 - c), device_id_type=MESH)
                cp.start()
                cps.append(cp)
        for cp in cps:
            cp.wait()

    return pl.pallas_call(
        body, name="grads_to_sibling",
        in_specs=[ANY] * n, out_specs=[ANY] * n,
        out_shape=[jax.ShapeDtypeStruct((g.shape[0],) + g.shape[2:], g.dtype) for g in grads],
        scratch_shapes=[pltpu.SemaphoreType.DMA((n, N_CHIPS)), pltpu.SemaphoreType.DMA((n, N_CHIPS))],
    )(*grads)


def add_own_half(g, recv, name):
    J, _, Rh, C = g.shape
    c_arr = jnp.reshape(lax.axis_index("c"), (1,)).astype(jnp.int32)

    def body(c_ref, g_ref, r_ref, o_ref):
        o_ref[...] = g_ref[...] + r_ref[...]

    return pl.pallas_call(
        body, name=name,
        grid_spec=pltpu.PrefetchScalarGridSpec(
            num_scalar_prefetch=1, grid=(J,),
            in_specs=[pl.BlockSpec((None, None, Rh, C), lambda j, c_ref: (j, c_ref[0], 0, 0)),
                      pl.BlockSpec((None, Rh, C), lambda j, c_ref: (j, 0, 0))],
            out_specs=pl.BlockSpec((None, Rh, C), lambda j, c_ref: (j, 0, 0))),
        out_shape=jax.ShapeDtypeStruct((J, Rh, C), F32),
        compiler_params=_params(1),
    )(c_arr, g, recv)


def exchange_between_chips(sums):
    n = len(sums)

    def body(*refs):
        ins, outs = refs[:n], refs[n:2 * n]
        loc_sem, send_sem, recv_sem = refs[2 * n:]
        x, y, c, chips = _place()
        mine = 2 * x + y
        cps = []
        for w in range(n):
            cp = pltpu.make_async_copy(ins[w].at[mine], outs[w].at[mine], loc_sem.at[w])
            cp.start()
            cps.append(cp)
            for k, (px, py) in enumerate(chips):
                cp = pltpu.make_async_remote_copy(
                    src_ref=ins[w].at[2 * px + py], dst_ref=outs[w].at[mine],
                    send_sem=send_sem.at[w, k], recv_sem=recv_sem.at[w, k],
                    device_id=(px, py, c), device_id_type=MESH)
                cp.start()
                cps.append(cp)
        for cp in cps:
            cp.wait()

    return pl.pallas_call(
        body, name="grads_between_chips",
        in_specs=[ANY] * n, out_specs=[ANY] * n,
        out_shape=[jax.ShapeDtypeStruct(s.shape, s.dtype) for s in sums],
        scratch_shapes=[pltpu.SemaphoreType.DMA((n,)), pltpu.SemaphoreType.DMA((n, 3)), pltpu.SemaphoreType.DMA((n, 3))],
    )(*sums)


def sum_chips(parts, name):
    J, Rh, C = parts.shape
    tr = Rh // 2

    def body(p_ref, o_ref):
        o_ref[...] = ((p_ref[0] + p_ref[1]) + p_ref[2]) + p_ref[3]

    return pl.pallas_call(
        body, name=name, grid=(Rh // tr,),
        in_specs=[pl.BlockSpec((J, tr, C), lambda i: (0, i, 0))],
        out_specs=pl.BlockSpec((tr, C), lambda i: (i, 0)),
        out_shape=jax.ShapeDtypeStruct((Rh, C), F32),
        compiler_params=_params(1),
    )(parts)


def share_with_sibling(halves):
    n = len(halves)

    def body(*refs):
        ins, outs = refs[:n], refs[n:2 * n]
        loc_sem, send_sem, recv_sem = refs[2 * n:]
        x, y, c, _ = _place()
        cps = []
        for w in range(n):
            cp = pltpu.make_async_copy(ins[w], outs[w].at[c], loc_sem.at[w])
            cp.start()
            cps.append(cp)
            cp = pltpu.make_async_remote_copy(
                src_ref=ins[w], dst_ref=outs[w].at[c], send_sem=send_sem.at[w], recv_sem=recv_sem.at[w],
                device_id=(x, y, 1 - c), device_id_type=MESH)
            cp.start()
            cps.append(cp)
        for cp in cps:
            cp.wait()

    return pl.pallas_call(
        body, name="grads_back_to_sibling",
        in_specs=[ANY] * n, out_specs=[ANY] * n,
        out_shape=[jax.ShapeDtypeStruct((2,) + h.shape, h.dtype) for h in halves],
        scratch_shapes=[pltpu.SemaphoreType.DMA((n,)), pltpu.SemaphoreType.DMA((n,)), pltpu.SemaphoreType.DMA((n,))],
    )(*halves)


def all_reduce_small(part):
    R, C = part.shape

    def body(x_ref, o_ref, all_ref, send_sem, recv_sem):
        x, y, c, _ = _place()
        me = 4 * x + 2 * y + c
        all_ref[me] = x_ref[...]
        cps = []
        for k in range(1, N_DEV):
            fx, fy, fc = (k >> 2) & 1, (k >> 1) & 1, k & 1
            peer = (x ^ fx, y ^ fy, c ^ fc)
            cp = pltpu.make_async_remote_copy(
                src_ref=x_ref, dst_ref=all_ref.at[me], send_sem=send_sem.at[k - 1], recv_sem=recv_sem.at[k - 1],
                device_id=peer, device_id_type=MESH)
            cp.start()
            cps.append(cp)
        for k in range(1, N_DEV):
            fx, fy, fc = (k >> 2) & 1, (k >> 1) & 1, k & 1
            peer = (x ^ fx, y ^ fy, c ^ fc)
            src = all_ref.at[4 * peer[0] + 2 * peer[1] + peer[2]]
            pltpu.make_async_remote_copy(
                src_ref=x_ref, dst_ref=src, send_sem=send_sem.at[k - 1], recv_sem=recv_sem.at[k - 1],
                device_id=peer, device_id_type=MESH).wait_recv()
        for cp in cps:
            cp.wait_send()
        acc = all_ref[0]
        for d in range(1, N_DEV):
            acc = acc + all_ref[d]
        o_ref[...] = acc

    return pl.pallas_call(
        body, name="all_reduce_small",
        in_specs=[pl.BlockSpec(memory_space=pltpu.VMEM)], out_specs=pl.BlockSpec(memory_space=pltpu.VMEM),
        out_shape=jax.ShapeDtypeStruct((R, C), F32),
        scratch_shapes=[pltpu.VMEM((N_DEV, R, C), F32), pltpu.SemaphoreType.DMA((N_DEV - 1,)),
                        pltpu.SemaphoreType.DMA((N_DEV - 1,))],
        compiler_params=pltpu.CompilerParams(vmem_limit_bytes=VMEM_LIMIT),
    )(part)


SMALL = ("norm_ffn1", "norm_mix", "norm_ffn2", "norm_final", "ret_gn_gain", "pool_scale", "pool_w")
BIG = ("ffn1_gate", "ffn1_up", "ffn1_down", "w_in", "w_out", "ffn2_gate", "ffn2_up", "ffn2_down")
ORDER = ("norm_ffn1", "ffn1_gate", "ffn1_up", "ffn1_down", "norm_mix", "w_in", "ret_gn_gain", "pool_w", "pool_scale",
         "w_out", "norm_ffn2", "ffn2_gate", "ffn2_up", "ffn2_down", "norm_final")


def _pack_small(d):
    return jnp.concatenate([d[k].reshape(-1, 128) for k in SMALL], axis=0)


def _unpack_small(packed, shapes):
    out, row = {}, 0
    for k in SMALL:
        n = math.prod(shapes[k]) // 128
        out[k] = packed[row:row + n].reshape(shapes[k])
        row += n
    return out


def kernel(x, norm_ffn1, ffn1_gate, ffn1_up, ffn1_down, norm_mix, w_in, ret_gn_gain, pool_w, pool_scale, w_out, norm_ffn2, ffn2_gate, ffn2_up, ffn2_down, norm_final, loss_target, m_norm_ffn1, m_ffn1_gate, m_ffn1_up, m_ffn1_down, m_norm_mix, m_w_in, m_ret_gn_gain, m_pool_w, m_pool_scale, m_w_out, m_norm_ffn2, m_ffn2_gate, m_ffn2_up, m_ffn2_down, m_norm_final, v_norm_ffn1, v_ffn1_gate, v_ffn1_up, v_ffn1_down, v_norm_mix, v_w_in, v_ret_gn_gain, v_pool_w, v_pool_scale, v_w_out, v_norm_ffn2, v_ffn2_gate, v_ffn2_up, v_ffn2_down, v_norm_final):
    W = dict(norm_ffn1=norm_ffn1, ffn1_gate=ffn1_gate, ffn1_up=ffn1_up, ffn1_down=ffn1_down, norm_mix=norm_mix, w_in=w_in,
             ret_gn_gain=ret_gn_gain, pool_w=pool_w, pool_scale=pool_scale, w_out=w_out, norm_ffn2=norm_ffn2,
             ffn2_gate=ffn2_gate, ffn2_up=ffn2_up, ffn2_down=ffn2_down, norm_final=norm_final)
    M = dict(norm_ffn1=m_norm_ffn1, ffn1_gate=m_ffn1_gate, ffn1_up=m_ffn1_up, ffn1_down=m_ffn1_down, norm_mix=m_norm_mix,
             w_in=m_w_in, ret_gn_gain=m_ret_gn_gain, pool_w=m_pool_w, pool_scale=m_pool_scale, w_out=m_w_out,
             norm_ffn2=m_norm_ffn2, ffn2_gate=m_ffn2_gate, ffn2_up=m_ffn2_up, ffn2_down=m_ffn2_down, norm_final=m_norm_final)
    V = dict(norm_ffn1=v_norm_ffn1, ffn1_gate=v_ffn1_gate, ffn1_up=v_ffn1_up, ffn1_down=v_ffn1_down, norm_mix=v_norm_mix,
             w_in=v_w_in, ret_gn_gain=v_ret_gn_gain, pool_w=v_pool_w, pool_scale=v_pool_scale, w_out=v_w_out,
             norm_ffn2=v_norm_ffn2, ffn2_gate=v_ffn2_gate, ffn2_up=v_ffn2_up, ffn2_down=v_ffn2_down, norm_final=v_norm_final)

    n_seq, S, D = x.shape
    T = n_seq * S
    tm = min(512, T // 2)
    xf = x.reshape(T, D)
    tgt = loss_target.reshape(T, D)

    local2d = {k: W[k].reshape(W[k].shape[1:]) for k in BIG}
    halves = [local2d[k].astype(BF16).reshape(2, local2d[k].shape[0] // 2, local2d[k].shape[1]) for k in BIG]
    gathered = gather_weights(halves)
    full = {k: g.reshape((N_CHIPS,) + local2d[k].shape) for k, g in zip(BIG, gathered)}
    wout_full = full["w_out"].reshape(D, D)
    g1 = norm_ffn1.reshape(1, D)
    gm = norm_mix.reshape(1, D)
    g3 = norm_ffn2.reshape(1, D)
    gf = norm_final.reshape(1, D)
    gn_gain = ret_gn_gain.reshape(1, V_W)
    pscale = pool_scale.reshape(1, V_W)
    pw = pool_w.reshape(len(POOL_WINDOWS), POOL_GC, POOL_GC)
    consts = mixer_constants(S)

    x1, a1, b1 = ffn_fwd(xf, g1, full["ffn1_gate"], full["ffn1_up"], full["ffn1_down"], tm, "ffn1_fwd")
    p = in_proj(x1, gm, full["w_in"], tm, "in_proj")
    x2, m_s, o_s, st_s, pooled_s = mixer_fwd(p, x1, consts, gn_gain, pw, pscale, wout_full, n_seq, S, "mixer_fwd")
    x3, a2, b2 = ffn_fwd(x2, g3, full["ffn2_gate"], full["ffn2_up"], full["ffn2_down"], tm, "ffn2_fwd")

    dx3, lacc, dgf = loss_bwd(x3, tgt, gf, tm, "loss_bwd")
    loss = lax.psum(jnp.sum(lacc) * (0.5 / D), ("x", "y", "c"))
    dx2, da2, db2, h2, n3, dg3 = ffn_bwd(x2, dx3, g3, a2, b2, full["ffn2_gate"], full["ffn2_up"], full["ffn2_down"], tm, "ffn2_bwd")
    grads = {}
    grads["ffn2_gate"] = tn_matmul(n3[None], da2, tm, "ffn2_dgate")
    grads["ffn2_up"] = tn_matmul(n3[None], db2, tm, "ffn2_dup")
    grads["ffn2_down"] = tn_matmul(h2, dx3[None], tm, "ffn2_ddown", b_scale=0.5)
    dx1, dp, n2, dgm, dgn, dps, dpw = mixer_bwd(dx2, x1, p, o_s, st_s, pooled_s, consts, gm, gn_gain, pw, pscale,
                                                wout_full, full["w_in"], n_seq, S, "mixer_bwd")
    grads["w_out"] = tn_matmul(m_s[None], dx2[None], tm, "dw_out").reshape(N_CHIPS, D // N_CHIPS, D)
    grads["w_in"] = tn_matmul(n2[None], dp, tm, "dw_in")
    dx0, da1, db1, h1, n1, dg1 = ffn_bwd(xf, dx1, g1, a1, b1, full["ffn1_gate"], full["ffn1_up"], full["ffn1_down"], tm, "ffn1_bwd")
    grads["ffn1_gate"] = tn_matmul(n1[None], da1, tm, "ffn1_dgate")
    grads["ffn1_up"] = tn_matmul(n1[None], db1, tm, "ffn1_dup")
    grads["ffn1_down"] = tn_matmul(h1, dx1[None], tm, "ffn1_ddown", b_scale=0.5)

    g4 = [grads[k].reshape(N_CHIPS, 2, grads[k].shape[1] // 2, grads[k].shape[2]) for k in BIG]
    from_sibling = send_to_sibling_other_half(g4)
    chip_sums = [add_own_half(g, r, "add_sibling_" + k) for g, r, k in zip(g4, from_sibling, BIG)]
    per_chip = exchange_between_chips(chip_sums)
    my_half = [sum_chips(pc, "sum_chips_" + k) for pc, k in zip(per_chip, BIG)]
    both = share_with_sibling(my_half)
    grad_big = {k: b.reshape(local2d[k].shape) for k, b in zip(BIG, both)}

    small_part = dict(norm_ffn1=dg1, norm_mix=dgm, norm_ffn2=dg3, norm_final=dgf, ret_gn_gain=dgn, pool_scale=dps, pool_w=dpw)
    small_sum = all_reduce_small(_pack_small(small_part))

    out_g, out_d, out_m, out_v = {}, {}, {}, {}
    for k in BIG:
        shp = W[k].shape
        d_, m_, v_ = adamw(local2d[k], grad_big[k], M[k].reshape(local2d[k].shape), V[k].reshape(local2d[k].shape), "adamw_" + k)
        out_g[k], out_d[k], out_m[k], out_v[k] = grad_big[k].reshape(shp), d_.reshape(shp), m_.reshape(shp), v_.reshape(shp)
    shapes = {k: W[k].shape for k in SMALL}
    d_, m_, v_ = adamw(_pack_small(W), small_sum, _pack_small(M), _pack_small(V), "adamw_small")
    for dst, packed in ((out_g, small_sum), (out_d, d_), (out_m, m_), (out_v, v_)):
        dst.update(_unpack_small(packed, shapes))

    grad_x = dx0.reshape(n_seq, S, D)
    return (loss, grad_x, *[out_g[k] for k in ORDER], *[out_d[k] for k in ORDER],
            *[out_m[k] for k in ORDER], *[out_v[k] for k in ORDER])
```

```python
import functools
import math

import jax
import jax.numpy as jnp
from jax import lax
from jax.experimental import pallas as pl
from jax.experimental.pallas import tpu as pltpu

F32 = jnp.float32
BF16 = jnp.bfloat16
MESH = pl.DeviceIdType.MESH

N_CHIPS = 4
N_DEV = 8
CHUNK = 64
RET_HEADS = 4
RET_DK = 64
RET_DV = 128
QK_W = RET_HEADS * RET_DK
V_W = RET_HEADS * RET_DV
POOL_WINDOWS = (2, 4, 8, 16)
POOL_GC = 128
ROPE_BASE = 10000.0
RMS_EPS = 1e-6
GN_EPS = 1e-5
ADAM_LR = 0.001
ADAM_B1 = 0.9
ADAM_B2 = 0.999
ADAM_EPS = 1e-08
ADAM_WD = 0.01
ADAM_STEP = 10
RET_BLOCK = 256
VMEM_LIMIT = 56 * 1024 * 1024

NN = (((1,), (0,)), ((), ()))
NT = (((1,), (1,)), ((), ()))
TN = (((0,), (0,)), ((), ()))


def _dot(a, b, dims=NN):
    return lax.dot_general(a, b, dims, preferred_element_type=F32)


def _sigmoid(x):
    return 1.0 / (1.0 + jnp.exp(-x))


def _params(n_grid):
    return pltpu.CompilerParams(dimension_semantics=("arbitrary",) * n_grid, vmem_limit_bytes=VMEM_LIMIT)


def _rms_parts(x):
    r = lax.rsqrt(jnp.mean(x * x, axis=-1, keepdims=True) + RMS_EPS)
    return r, x * r


def _rms_bwd(dn, xhat, r, gain, dres):
    dxh = dn * gain
    return dres + r * (dxh - xhat * jnp.mean(dxh * xhat, axis=-1, keepdims=True))


def _w3_specs(FS, D):
    return [pl.BlockSpec((None, None, FS, D), functools.partial(lambda i, j, k: (k, j, 0, 0), k=k)) for k in range(3)]


def ffn_fwd(x, gain, w3, tm, name):
    T, D = x.shape
    _, J, FS, _ = w3.shape

    def body(x_ref, g_ref, wg_ref, wu_ref, wd_ref, xo_ref, a_ref, b_ref, n_sc, acc):
        j = pl.program_id(1)

        @pl.when(j == 0)
        def _():
            _, xhat = _rms_parts(x_ref[...])
            n_sc[...] = (xhat * g_ref[...]).astype(BF16)
            acc[...] = jnp.zeros_like(acc)

        n = n_sc[...]
        a = _dot(n, wg_ref[...], NT)
        b = _dot(n, wu_ref[...], NT)
        a_ref[...] = a.astype(BF16)
        b_ref[...] = b.astype(BF16)
        h = (a * _sigmoid(a) * b).astype(BF16)
        acc[...] += _dot(h, wd_ref[...])

        @pl.when(j == J - 1)
        def _():
            xo_ref[...] = x_ref[...] + 0.5 * acc[...]

    return pl.pallas_call(
        body, name=name, grid=(T // tm, J),
        in_specs=[pl.BlockSpec((tm, D), lambda i, j: (i, 0)),
                  pl.BlockSpec((1, D), lambda i, j: (0, 0))] + _w3_specs(FS, D),
        out_specs=[pl.BlockSpec((tm, D), lambda i, j: (i, 0)),
                   pl.BlockSpec((None, tm, FS), lambda i, j: (j, i, 0)),
                   pl.BlockSpec((None, tm, FS), lambda i, j: (j, i, 0))],
        out_shape=[jax.ShapeDtypeStruct((T, D), F32),
                   jax.ShapeDtypeStruct((J, T, FS), BF16),
                   jax.ShapeDtypeStruct((J, T, FS), BF16)],
        scratch_shapes=[pltpu.VMEM((tm, D), BF16), pltpu.VMEM((tm, D), F32)],
        compiler_params=_params(2),
    )(x, gain, w3, w3, w3)


def ffn_bwd(x, dout, gain, a_s, b_s, w3, tm, name):
    T, D = x.shape
    _, J, FS, _ = w3.shape
    nt = T // tm

    def body(x_ref, do_ref, g_ref, a_ref, b_ref, wg_ref, wu_ref, wd_ref,
             dx_ref, da_ref, db_ref, h_ref, n_ref, df_ref, dgain_ref, acc):
        i = pl.program_id(0)
        j = pl.program_id(1)

        @pl.when(jnp.logical_and(i == 0, j == 0))
        def _():
            dgain_ref[...] = jnp.zeros_like(dgain_ref)

        @pl.when(j == 0)
        def _():
            _, xhat = _rms_parts(x_ref[...])
            n_ref[...] = (xhat * g_ref[...]).astype(BF16)
            df_ref[...] = (0.5 * do_ref[...]).astype(BF16)
            acc[...] = jnp.zeros_like(acc)

        a = a_ref[...].astype(F32)
        b = b_ref[...].astype(F32)
        dh = _dot(df_ref[...], wd_ref[...], NT)
        sg = _sigmoid(a)
        si = a * sg
        da = (dh * b * (sg * (1.0 + a * (1.0 - sg)))).astype(BF16)
        db = (dh * si).astype(BF16)
        da_ref[...] = da
        db_ref[...] = db
        h_ref[...] = (si * b).astype(BF16)
        acc[...] += _dot(da, wg_ref[...]) + _dot(db, wu_ref[...])

        @pl.when(j == J - 1)
        def _():
            r, xhat = _rms_parts(x_ref[...])
            dn = acc[...]
            dgain_ref[...] += jnp.sum(dn * xhat, axis=0, keepdims=True)
            dx_ref[...] = _rms_bwd(dn, xhat, r, g_ref[...], do_ref[...])

    tok = pl.BlockSpec((tm, D), lambda i, j: (i, 0))
    sh = pl.BlockSpec((None, tm, FS), lambda i, j: (j, i, 0))
    return pl.pallas_call(
        body, name=name, grid=(nt, J),
        in_specs=[tok, tok, pl.BlockSpec((1, D), lambda i, j: (0, 0)), sh, sh] + _w3_specs(FS, D),
        out_specs=[tok, sh, sh, sh, tok, tok, pl.BlockSpec((1, D), lambda i, j: (0, 0))],
        out_shape=[jax.ShapeDtypeStruct((T, D), F32),
                   jax.ShapeDtypeStruct((J, T, FS), BF16),
                   jax.ShapeDtypeStruct((J, T, FS), BF16),
                   jax.ShapeDtypeStruct((J, T, FS), BF16),
                   jax.ShapeDtypeStruct((T, D), BF16),
                   jax.ShapeDtypeStruct((T, D), BF16),
                   jax.ShapeDtypeStruct((1, D), F32)],
        scratch_shapes=[pltpu.VMEM((tm, D), F32)],
        compiler_params=_params(2),
    )(x, dout, gain, a_s, b_s, w3, w3, w3)


def ffn_wgrad(da, db, h, n, df, tk, name):
    J, T, FS = da.shape
    D = n.shape[1]

    def body(da_ref, db_ref, h_ref, n_ref, df_ref, o_ref):
        @pl.when(pl.program_id(1) == 0)
        def _():
            o_ref[...] = jnp.zeros_like(o_ref)

        nv = n_ref[...]
        o_ref[0] += _dot(da_ref[...], nv, TN)
        o_ref[1] += _dot(db_ref[...], nv, TN)
        o_ref[2] += _dot(h_ref[...], df_ref[...], TN)

    sh = pl.BlockSpec((None, tk, FS), lambda j, k: (j, k, 0))
    tok = pl.BlockSpec((tk, D), lambda j, k: (k, 0))
    return pl.pallas_call(
        body, name=name, grid=(J, T // tk),
        in_specs=[sh, sh, sh, tok, tok],
        out_specs=pl.BlockSpec((3, None, FS, D), lambda j, k: (0, j, 0, 0)),
        out_shape=jax.ShapeDtypeStruct((3, J, FS, D), F32),
        compiler_params=_params(2),
    )(da, db, h, n, df)


def tn_shared(a, b, tk, name):
    T, M = a.shape
    J, _, N = b.shape

    def body(a_ref, b_ref, o_ref):
        @pl.when(pl.program_id(0) == 0)
        def _():
            o_ref[...] = jnp.zeros_like(o_ref)

        a_t = a_ref[...].astype(BF16).T
        for j in range(J):
            o_ref[j] += _dot(a_t, b_ref[j].astype(BF16))

    return pl.pallas_call(
        body, name=name, grid=(T // tk,),
        in_specs=[pl.BlockSpec((tk, M), lambda k: (k, 0)), pl.BlockSpec((J, tk, N), lambda k: (0, k, 0))],
        out_specs=pl.BlockSpec((J, M, N), lambda k: (0, 0, 0)),
        out_shape=jax.ShapeDtypeStruct((J, M, N), F32),
        compiler_params=_params(1),
    )(a, b)


def in_proj(x, gain, win, tm, name):
    T, D = x.shape
    J, _, W = win.shape

    def body(x_ref, g_ref, w_ref, p_ref):
        _, xhat = _rms_parts(x_ref[...])
        n = (xhat * g_ref[...]).astype(BF16)
        for j in range(J):
            p_ref[j] = _dot(n, w_ref[j])

    return pl.pallas_call(
        body, name=name, grid=(T // tm,),
        in_specs=[pl.BlockSpec((tm, D), lambda i: (i, 0)),
                  pl.BlockSpec((1, D), lambda i: (0, 0)),
                  pl.BlockSpec((J, D, W), lambda i: (0, 0, 0))],
        out_specs=pl.BlockSpec((J, tm, W), lambda i: (0, i, 0)),
        out_shape=jax.ShapeDtypeStruct((J, T, W), F32),
        compiler_params=_params(1),
    )(x, gain, win)


def mixer_constants(S):
    B = RET_BLOCK
    half = RET_DK // 2
    freqs = ROPE_BASE ** (-jnp.arange(half, dtype=F32) * 2.0 / RET_DK)
    ang = jnp.arange(S, dtype=F32)[:, None] * freqs[None, :]
    cos = jnp.tile(jnp.cos(ang), (1, 2 * RET_HEADS))
    sin = jnp.tile(jnp.concatenate([-jnp.sin(ang), jnp.sin(ang)], axis=1), (1, RET_HEADS))
    gamma = 1.0 - 2.0 ** (-5.0 - jnp.arange(RET_HEADS, dtype=F32))
    log_g = jnp.log(gamma)
    idx = jnp.arange(B, dtype=F32)
    ci = jnp.arange(B) // CHUNK
    dist = jnp.abs(idx[:, None] - idx[None, :])
    m_intra = jnp.exp(log_g[:, None, None] * dist[None]) * (ci[None, :] <= ci[:, None])[None].astype(F32)
    lg_lane = jnp.repeat(log_g, RET_DK)
    d_q = jnp.exp(lg_lane[None, :] * (idx[:, None] + 1.0))
    d_k = jnp.exp(lg_lane[None, :] * (B - 1.0 - idx[:, None]))
    d_s = jnp.broadcast_to(jnp.exp(lg_lane * B)[:, None], (QK_W, V_W))
    bm = (jnp.arange(QK_W)[:, None] // RET_DK == jnp.arange(V_W)[None, :] // RET_DV).astype(F32)
    win = jnp.repeat(jnp.array(POOL_WINDOWS, F32), POOL_GC)
    invc = 1.0 / jnp.minimum(jnp.arange(S, dtype=F32)[:, None] + 1.0, win[None, :])
    return dict(cos=cos, sin=sin, m=m_intra, dq=d_q, dk=d_k, ds=d_s, bm=bm, invc=invc)


def _swap_halves(x):
    w = x.shape[1]
    lane = lax.broadcasted_iota(jnp.int32, x.shape, 1)
    return jnp.where((lane % RET_DK) < RET_DK // 2, pltpu.roll(x, w - RET_DK // 2, 1), pltpu.roll(x, RET_DK // 2, 1))


def _rot(x, cos, sin):
    return x * cos + _swap_halves(x) * sin


def _rot_t(d, cos, sin):
    return d * cos + _swap_halves(d * sin)


def _lane_groups(parts):
    return jnp.concatenate([p[:, POOL_GC * g:POOL_GC * (g + 1)] for g, p in enumerate(parts)], axis=1)


def _head_mask(shape, h):
    lane = lax.broadcasted_iota(jnp.int32, shape, 1)
    return (lane // RET_DK) == h


def _group_norm(o):
    yh, rs = [], []
    for h in range(RET_HEADS):
        oh = o[:, RET_DV * h:RET_DV * (h + 1)]
        xc = oh - jnp.mean(oh, axis=-1, keepdims=True)
        r = lax.rsqrt(jnp.mean(xc * xc, axis=-1, keepdims=True) + GN_EPS)
        yh.append(xc * r)
        rs.append(r)
    return yh, rs


def mixer_fwd(p, x1, consts, gn_gain, pool_w, pool_scale, wout, n_seq, S, name):
    _, T, _ = p.shape
    D = x1.shape[1]
    B = RET_BLOCK
    nb = S // B

    def body(p_ref, x1_ref, cos_ref, sin_ref, invc_ref, m_ref, dq_ref, dk_ref, ds_ref, bm_ref,
             gain_ref, pw_ref, sc_ref, wout_ref,
             x2_ref, m_out, o_out, st_out, pooled_out, state, prev_u):
        blk = pl.program_id(1)

        @pl.when(blk == 0)
        def _():
            state[...] = jnp.zeros_like(state)
            prev_u[...] = jnp.zeros_like(prev_u)

        qk = p_ref[0]
        v = p_ref[1]
        g = p_ref[2]
        u = p_ref[3]
        cos = cos_ref[...]
        sin = sin_ref[...]
        qr = _rot(qk[:, :QK_W], cos, sin) * (RET_DK ** -0.5)
        kr = _rot(qk[:, QK_W:], cos, sin)
        qb = qr.astype(BF16)
        kb = kr.astype(BF16)
        vb = v.astype(BF16)
        st = state[...]
        st_out[...] = st
        cross = _dot((qr * dq_ref[...]).astype(BF16), st.astype(BF16))
        o_parts = []
        for h in range(RET_HEADS):
            qm = jnp.where(_head_mask(qb.shape, h), qb, jnp.zeros_like(qb))
            sc = (_dot(qm, kb, NT) * m_ref[h]).astype(BF16)
            o_parts.append(_dot(sc, vb[:, RET_DV * h:RET_DV * (h + 1)]) + cross[:, RET_DV * h:RET_DV * (h + 1)])
        o = jnp.concatenate(o_parts, axis=1)
        o_out[...] = o
        kv = _dot((kr * dk_ref[...]).astype(BF16), vb, TN)
        state[...] = st * ds_ref[...] + kv * bm_ref[...]

        yh, _ = _group_norm(o)
        r = g * _sigmoid(g) * (jnp.concatenate(yh, axis=1) * gain_ref[...])

        ext = jnp.concatenate([prev_u[...], u], axis=0)
        sums = []
        run = ext
        for k in (1, 2, 4, 8):
            run = run + pltpu.roll(run, k, 0)
            sums.append(run[B:, :])
        prev_u[...] = u
        pooled = (_lane_groups(sums) * invc_ref[...] - u).astype(BF16)
        pooled_out[...] = pooled
        yp = [_dot(pooled[:, POOL_GC * gi:POOL_GC * (gi + 1)], pw_ref[gi].astype(BF16)) for gi in range(len(POOL_WINDOWS))]
        s = jnp.concatenate(yp, axis=1) * sc_ref[...]
        m = jnp.concatenate([r, s], axis=1).astype(BF16)
        m_out[...] = m
        x2_ref[...] = x1_ref[...] + _dot(m, wout_ref[...])

    tokmap = lambda b, k: (b * nb + k, 0)
    seqmap = lambda b, k: (k, 0)
    const2 = lambda b, k: (0, 0)
    const3 = lambda b, k: (0, 0, 0)
    return pl.pallas_call(
        body, name=name, grid=(n_seq, nb),
        in_specs=[pl.BlockSpec((4, B, V_W), lambda b, k: (0, b * nb + k, 0)),
                  pl.BlockSpec((B, D), tokmap),
                  pl.BlockSpec((B, QK_W), seqmap), pl.BlockSpec((B, QK_W), seqmap), pl.BlockSpec((B, V_W), seqmap),
                  pl.BlockSpec((RET_HEADS, B, B), const3),
                  pl.BlockSpec((B, QK_W), const2), pl.BlockSpec((B, QK_W), const2),
                  pl.BlockSpec((QK_W, V_W), const2), pl.BlockSpec((QK_W, V_W), const2),
                  pl.BlockSpec((1, V_W), const2), pl.BlockSpec((4, POOL_GC, POOL_GC), const3),
                  pl.BlockSpec((1, V_W), const2), pl.BlockSpec((D, D), const2)],
        out_specs=[pl.BlockSpec((B, D), tokmap), pl.BlockSpec((B, D), tokmap), pl.BlockSpec((B, V_W), tokmap),
                   pl.BlockSpec((None, QK_W, V_W), lambda b, k: (b * nb + k, 0, 0)),
                   pl.BlockSpec((B, V_W), tokmap)],
        out_shape=[jax.ShapeDtypeStruct((T, D), F32), jax.ShapeDtypeStruct((T, D), BF16),
                   jax.ShapeDtypeStruct((T, V_W), F32), jax.ShapeDtypeStruct((T // B, QK_W, V_W), F32),
                   jax.ShapeDtypeStruct((T, V_W), BF16)],
        scratch_shapes=[pltpu.VMEM((QK_W, V_W), F32), pltpu.VMEM((B, V_W), F32)],
        compiler_params=_params(2),
    )(p, x1, consts["cos"], consts["sin"], consts["invc"], consts["m"], consts["dq"], consts["dk"],
      consts["ds"], consts["bm"], gn_gain, pool_w, pool_scale, wout)


def mixer_bwd(dx2, x1, p, o_s, st_s, pooled_s, consts, gmix, gn_gain, pool_w, pool_scale, wout, win, n_seq, S, name):
    T, D = dx2.shape
    B = RET_BLOCK
    nb = S // B
    NG = len(POOL_WINDOWS)

    def body(dx2_ref, x1_ref, p_ref, o_ref, st_ref, pooled_ref, cos_ref, sin_ref, invc_ref, m_ref, dq_ref, dk_ref,
             ds_ref, bm_ref, gmix_ref, gain_ref, pw_ref, sc_ref, wout_ref, win_ref,
             dx1_ref, dp_ref, n2_ref, dgmix_ref, dgain_ref, dscale_ref, dpw_ref, gstate, next_e):
        b = pl.program_id(0)
        blk = pl.program_id(1)

        @pl.when(jnp.logical_and(b == 0, blk == 0))
        def _():
            dgmix_ref[...] = jnp.zeros_like(dgmix_ref)
            dgain_ref[...] = jnp.zeros_like(dgain_ref)
            dscale_ref[...] = jnp.zeros_like(dscale_ref)
            dpw_ref[...] = jnp.zeros_like(dpw_ref)

        @pl.when(blk == 0)
        def _():
            gstate[...] = jnp.zeros_like(gstate)
            next_e[...] = jnp.zeros_like(next_e)

        dx2v = dx2_ref[...]
        dm = _dot(dx2v.astype(BF16), wout_ref[...], NT)
        dr = dm[:, :V_W]
        dsv = dm[:, V_W:]

        pooled = pooled_ref[...]
        scale = sc_ref[...]
        dyp = (dsv * scale).astype(BF16)
        yp, dpl = [], []
        for gi in range(NG):
            sl = slice(POOL_GC * gi, POOL_GC * (gi + 1))
            pwb = pw_ref[gi].astype(BF16)
            yp.append(_dot(pooled[:, sl], pwb))
            dpw_ref[gi] += _dot(pooled[:, sl], dyp[:, sl], TN)
            dpl.append(_dot(dyp[:, sl], pwb, NT))
        dscale_ref[...] += jnp.sum(dsv * jnp.concatenate(yp, axis=1), axis=0, keepdims=True)
        dpooled = jnp.concatenate(dpl, axis=1)
        e = dpooled * invc_ref[...]
        ext = jnp.concatenate([e, next_e[...]], axis=0)
        sums = []
        run = ext
        for k in (1, 2, 4, 8):
            run = run + pltpu.roll(run, 2 * B - k, 0)
            sums.append(run[:B, :])
        next_e[...] = e
        du = _lane_groups(sums) - dpooled

        qk = p_ref[0]
        v = p_ref[1]
        g = p_ref[2]
        sg = _sigmoid(g)
        si = g * sg
        yh, rs = _group_norm(o_ref[...])
        yhat = jnp.concatenate(yh, axis=1)
        gain = gain_ref[...]
        dg = dr * (yhat * gain) * (sg * (1.0 + g * (1.0 - sg)))
        dy = dr * si
        dgain_ref[...] += jnp.sum(dy * yhat, axis=0, keepdims=True)
        dyh = dy * gain
        do_parts = []
        for h in range(RET_HEADS):
            sl = slice(RET_DV * h, RET_DV * (h + 1))
            dh_ = dyh[:, sl]
            m1 = jnp.mean(dh_, axis=-1, keepdims=True)
            m2 = jnp.mean(dh_ * yh[h], axis=-1, keepdims=True)
            do_parts.append(rs[h] * (dh_ - m1 - yh[h] * m2))
        dob = jnp.concatenate(do_parts, axis=1).astype(BF16)

        cos = cos_ref[...]
        sin = sin_ref[...]
        qr = _rot(qk[:, :QK_W], cos, sin) * (RET_DK ** -0.5)
        kr = _rot(qk[:, QK_W:], cos, sin)
        qb = qr.astype(BF16)
        kb = kr.astype(BF16)
        vb = v.astype(BF16)
        dqd = dq_ref[...]
        dkd = dk_ref[...]
        stb = st_ref[...].astype(BF16)
        gs = gstate[...]
        gb = gs.astype(BF16)
        dqs = _dot(dob, stb, NT) * dqd
        dkr = _dot(vb, gb, NT) * dkd
        dv_cross = _dot((kr * dkd).astype(BF16), gb)
        ds_cross = _dot((qr * dqd).astype(BF16), dob, TN) * bm_ref[...]
        gstate[...] = ds_cross + gs * ds_ref[...]
        dv_parts = []
        for h in range(RET_HEADS):
            sl = slice(RET_DV * h, RET_DV * (h + 1))
            hm = _head_mask(qb.shape, h)
            qm = jnp.where(hm, qb, jnp.zeros_like(qb))
            mh = m_ref[h]
            sc = (_dot(qm, kb, NT) * mh).astype(BF16)
            dsc = (_dot(dob[:, sl], vb[:, sl], NT) * mh).astype(BF16)
            dqs = dqs + jnp.where(hm, _dot(dsc, kb), 0.0)
            dkr = dkr + jnp.where(hm, _dot(dsc, qb, TN), 0.0)
            dv_parts.append(_dot(sc, dob[:, sl], TN) + dv_cross[:, sl])
        dq = _rot_t(dqs * (RET_DK ** -0.5), cos, sin)
        dk = _rot_t(dkr, cos, sin)
        dp = [jnp.concatenate([dq, dk], axis=1).astype(BF16), jnp.concatenate(dv_parts, axis=1).astype(BF16),
              dg.astype(BF16), du.astype(BF16)]
        dn = jnp.zeros((B, D), F32)
        for jj in range(4):
            dp_ref[jj] = dp[jj]
            dn = dn + _dot(dp[jj], win_ref[jj], NT)

        x1v = x1_ref[...]
        r, xhat = _rms_parts(x1v)
        gm = gmix_ref[...]
        n2_ref[...] = (xhat * gm).astype(BF16)
        dgmix_ref[...] += jnp.sum(dn * xhat, axis=0, keepdims=True)
        dx1_ref[...] = _rms_bwd(dn, xhat, r, gm, dx2v)

    rev = lambda b, k: (b * nb + (nb - 1 - k), 0)
    seqrev = lambda b, k: (nb - 1 - k, 0)
    const2 = lambda b, k: (0, 0)
    const3 = lambda b, k: (0, 0, 0)
    return pl.pallas_call(
        body, name=name, grid=(n_seq, nb),
        in_specs=[pl.BlockSpec((B, D), rev), pl.BlockSpec((B, D), rev),
                  pl.BlockSpec((4, B, V_W), lambda b, k: (0, b * nb + (nb - 1 - k), 0)),
                  pl.BlockSpec((B, V_W), rev),
                  pl.BlockSpec((None, QK_W, V_W), lambda b, k: (b * nb + (nb - 1 - k), 0, 0)),
                  pl.BlockSpec((B, V_W), rev),
                  pl.BlockSpec((B, QK_W), seqrev), pl.BlockSpec((B, QK_W), seqrev), pl.BlockSpec((B, V_W), seqrev),
                  pl.BlockSpec((RET_HEADS, B, B), const3),
                  pl.BlockSpec((B, QK_W), const2), pl.BlockSpec((B, QK_W), const2),
                  pl.BlockSpec((QK_W, V_W), const2), pl.BlockSpec((QK_W, V_W), const2),
                  pl.BlockSpec((1, D), const2), pl.BlockSpec((1, V_W), const2),
                  pl.BlockSpec((NG, POOL_GC, POOL_GC), const3), pl.BlockSpec((1, V_W), const2),
                  pl.BlockSpec((D, D), const2), pl.BlockSpec((4, D, V_W), const3)],
        out_specs=[pl.BlockSpec((B, D), rev),
                   pl.BlockSpec((4, B, V_W), lambda b, k: (0, b * nb + (nb - 1 - k), 0)),
                   pl.BlockSpec((B, D), rev),
                   pl.BlockSpec((1, D), const2), pl.BlockSpec((1, V_W), const2), pl.BlockSpec((1, V_W), const2),
                   pl.BlockSpec((NG, POOL_GC, POOL_GC), const3)],
        out_shape=[jax.ShapeDtypeStruct((T, D), F32), jax.ShapeDtypeStruct((4, T, V_W), BF16),
                   jax.ShapeDtypeStruct((T, D), BF16),
                   jax.ShapeDtypeStruct((1, D), F32), jax.ShapeDtypeStruct((1, V_W), F32),
                   jax.ShapeDtypeStruct((1, V_W), F32), jax.ShapeDtypeStruct((NG, POOL_GC, POOL_GC), F32)],
        scratch_shapes=[pltpu.VMEM((QK_W, V_W), F32), pltpu.VMEM((B, V_W), F32)],
        compiler_params=_params(2),
    )(dx2, x1, p, o_s, st_s, pooled_s, consts["cos"], consts["sin"], consts["invc"], consts["m"], consts["dq"],
      consts["dk"], consts["ds"], consts["bm"], gmix, gn_gain, pool_w, pool_scale, wout, win)


def loss_bwd(x3, tgt, gain, tm, name):
    T, D = x3.shape

    def body(x_ref, t_ref, g_ref, dx_ref, lacc_ref, dg_ref):
        @pl.when(pl.program_id(0) == 0)
        def _():
            lacc_ref[...] = jnp.zeros_like(lacc_ref)
            dg_ref[...] = jnp.zeros_like(dg_ref)

        r, xhat = _rms_parts(x_ref[...])
        gv = g_ref[...]
        err = xhat * gv - t_ref[...]
        lacc_ref[...] += jnp.sum(err * err, axis=0, keepdims=True)
        dy = err * (1.0 / D)
        dg_ref[...] += jnp.sum(dy * xhat, axis=0, keepdims=True)
        dx_ref[...] = _rms_bwd(dy, xhat, r, gv, 0.0)

    tok = pl.BlockSpec((tm, D), lambda i: (i, 0))
    vec = pl.BlockSpec((1, D), lambda i: (0, 0))
    return pl.pallas_call(
        body, name=name, grid=(T // tm,),
        in_specs=[tok, tok, vec], out_specs=[tok, vec, vec],
        out_shape=[jax.ShapeDtypeStruct((T, D), F32), jax.ShapeDtypeStruct((1, D), F32), jax.ShapeDtypeStruct((1, D), F32)],
        compiler_params=_params(1),
    )(x3, tgt, gain)


def _adamw_math(w, g, m, v):
    m2 = ADAM_B1 * m + (1.0 - ADAM_B1) * g
    v2 = ADAM_B2 * v + (1.0 - ADAM_B2) * (g * g)
    m_hat = m2 / (1.0 - ADAM_B1 ** ADAM_STEP)
    v_hat = v2 / (1.0 - ADAM_B2 ** ADAM_STEP)
    return -ADAM_LR * (m_hat / (jnp.sqrt(v_hat) + ADAM_EPS) + ADAM_WD * w), m2, v2


def adamw(w, g, m, v, name):
    R, C = w.shape

    def body(w_ref, g_ref, m_ref, v_ref, d_ref, nm_ref, nv_ref):
        d_ref[...], nm_ref[...], nv_ref[...] = _adamw_math(w_ref[...], g_ref[...], m_ref[...], v_ref[...])

    blk = pl.BlockSpec((R, C), lambda i: (0, 0))
    sds = jax.ShapeDtypeStruct((R, C), F32)
    return pl.pallas_call(
        body, name=name, grid=(1,), in_specs=[blk] * 4, out_specs=[blk] * 3, out_shape=[sds] * 3,
        compiler_params=_params(1),
    )(w, g, m, v)


def _place():
    x, y, c = lax.axis_index("x"), lax.axis_index("y"), lax.axis_index("c")
    other_chips = [(1 - x, y), (x, 1 - y), (1 - x, 1 - y)]
    return x, y, c, other_chips


ANY = pl.BlockSpec(memory_space=pl.ANY)


def gather_weights(shards):
    n = len(shards)

    def body(*refs):
        ins, outs, bufs = refs[:n], refs[n:2 * n], refs[2 * n:3 * n]
        ld_sem, st_sem, ici_send, ici_recv, d2d_send, d2d_recv = refs[3 * n:]
        x, y, c, chips = _place()
        sibling = (x, y, 1 - c)
        mine = 2 * x + y
        loads = [pltpu.make_async_copy(ins[w], bufs[w], ld_sem.at[w]) for w in range(n)]
        for cp in loads:
            cp.start()
        stores, sends = [], []
        for w in range(n):
            loads[w].wait()
            cp = pltpu.make_async_copy(bufs[w], outs[w].at[:, mine], st_sem.at[w])
            cp.start()
            stores.append(cp)
            for k, (px, py) in enumerate(chips):
                cp = pltpu.make_async_remote_copy(
                    src_ref=bufs[w].at[:, c], dst_ref=outs[w].at[:, mine, c],
                    send_sem=ici_send.at[w, k], recv_sem=ici_recv.at[w, k],
                    device_id=(px, py, c), device_id_type=MESH)
                cp.start()
                sends.append(cp)
        for w in range(n):
            for k, (px, py) in enumerate(chips):
                landed = outs[w].at[:, 2 * px + py, c]
                pltpu.make_async_remote_copy(
                    src_ref=landed, dst_ref=landed, send_sem=ici_send.at[w, k], recv_sem=ici_recv.at[w, k],
                    device_id=(px, py, c), device_id_type=MESH).wait_recv()
                cp = pltpu.make_async_remote_copy(
                    src_ref=landed, dst_ref=landed, send_sem=d2d_send.at[w, k], recv_sem=d2d_recv.at[w, k],
                    device_id=sibling, device_id_type=MESH)
                cp.start()
                sends.append(cp)
        for w in range(n):
            for k, (px, py) in enumerate(chips):
                landed = outs[w].at[:, 2 * px + py, 1 - c]
                pltpu.make_async_remote_copy(
                    src_ref=landed, dst_ref=landed, send_sem=d2d_send.at[w, k], recv_sem=d2d_recv.at[w, k],
                    device_id=sibling, device_id_type=MESH).wait_recv()
        for cp in sends:
            cp.wait_send()
        for cp in stores:
            cp.wait()

    return pl.pallas_call(
        body, name="gather_weights",
        in_specs=[ANY] * n, out_specs=[ANY] * n,
        out_shape=[jax.ShapeDtypeStruct((s.shape[0], N_CHIPS) + s.shape[1:], s.dtype) for s in shards],
        scratch_shapes=[pltpu.VMEM(s.shape, s.dtype) for s in shards] +
                       [pltpu.SemaphoreType.DMA((n,)), pltpu.SemaphoreType.DMA((n,)),
                        pltpu.SemaphoreType.DMA((n, 3)), pltpu.SemaphoreType.DMA((n, 3)),
                        pltpu.SemaphoreType.DMA((n, 3)), pltpu.SemaphoreType.DMA((n, 3))],
        compiler_params=pltpu.CompilerParams(vmem_limit_bytes=VMEM_LIMIT),
    )(*shards)


def send_to_sibling_other_half(grads):
    n = len(grads)

    def body(*refs):
        ins, outs = refs[:n], refs[n:2 * n]
        send_sem, recv_sem = refs[2 * n:]
        x, y, c, _ = _place()
        cps = []
        for w in range(n):
            cp = pltpu.make_async_remote_copy(
                src_ref=ins[w].at[:, :, 1 - c], dst_ref=outs[w],
                send_sem=send_sem.at[w], recv_sem=recv_sem.at[w],
                device_id=(x, y, 1 - c), device_id_type=MESH)
            cp.start()
            cps.append(cp)
        for cp in cps:
            cp.wait()

    return pl.pallas_call(
        body, name="grads_to_sibling",
        in_specs=[ANY] * n, out_specs=[ANY] * n,
        out_shape=[jax.ShapeDtypeStruct(g.shape[:2] + g.shape[3:], g.dtype) for g in grads],
        scratch_shapes=[pltpu.SemaphoreType.DMA((n,)), pltpu.SemaphoreType.DMA((n,))],
    )(*grads)


def _core_index():
    return jnp.reshape(lax.axis_index("c"), (1,)).astype(jnp.int32)


def _chip_index():
    return jnp.reshape(2 * lax.axis_index("x") + lax.axis_index("y"), (1,)).astype(jnp.int32)


def add_own_half(g, recv, name):
    L, J, _, Rh, C = g.shape

    def body(c_ref, g_ref, r_ref, o_ref):
        o_ref[...] = (g_ref[...] + r_ref[...]).astype(BF16)

    return pl.pallas_call(
        body, name=name,
        grid_spec=pltpu.PrefetchScalarGridSpec(
            num_scalar_prefetch=1, grid=(L, J),
            in_specs=[pl.BlockSpec((None, None, None, Rh, C), lambda l, j, c_ref: (l, j, c_ref[0], 0, 0)),
                      pl.BlockSpec((None, None, Rh, C), lambda l, j, c_ref: (l, j, 0, 0))],
            out_specs=pl.BlockSpec((None, None, Rh, C), lambda l, j, c_ref: (l, j, 0, 0))),
        out_shape=jax.ShapeDtypeStruct((L, J, Rh, C), BF16),
        compiler_params=_params(2),
    )(_core_index(), g, recv)


def exchange_between_chips(sums):
    n = len(sums)

    def body(*refs):
        ins, outs = refs[:n], refs[n:2 * n]
        send_sem, recv_sem = refs[2 * n:]
        x, y, c, chips = _place()
        cps = []
        for w in range(n):
            for k, (px, py) in enumerate(chips):
                cp = pltpu.make_async_remote_copy(
                    src_ref=ins[w].at[:, 2 * px + py], dst_ref=outs[w].at[k],
                    send_sem=send_sem.at[w, k], recv_sem=recv_sem.at[w, k],
                    device_id=(px, py, c), device_id_type=MESH)
                cp.start()
                cps.append(cp)
        for cp in cps:
            cp.wait()

    return pl.pallas_call(
        body, name="grads_between_chips",
        in_specs=[ANY] * n, out_specs=[ANY] * n,
        out_shape=[jax.ShapeDtypeStruct((3, s.shape[0]) + s.shape[2:], s.dtype) for s in sums],
        scratch_shapes=[pltpu.SemaphoreType.DMA((n, 3)), pltpu.SemaphoreType.DMA((n, 3))],
    )(*sums)


def sum_chips(own, recv, name):
    L, _, Rh, C = own.shape

    def body(chip_ref, o_ref, r_ref, out_ref):
        acc = o_ref[...].astype(F32)
        for k in range(3):
            acc = acc + r_ref[k].astype(F32)
        out_ref[...] = acc

    return pl.pallas_call(
        body, name=name,
        grid_spec=pltpu.PrefetchScalarGridSpec(
            num_scalar_prefetch=1, grid=(L,),
            in_specs=[pl.BlockSpec((None, None, Rh, C), lambda l, chip_ref: (l, chip_ref[0], 0, 0)),
                      pl.BlockSpec((3, None, Rh, C), lambda l, chip_ref: (0, l, 0, 0))],
            out_specs=pl.BlockSpec((None, Rh, C), lambda l, chip_ref: (l, 0, 0))),
        out_shape=jax.ShapeDtypeStruct((L, Rh, C), F32),
        compiler_params=_params(1),
    )(_chip_index(), own, recv)


def share_with_sibling(halves):
    n = len(halves)

    def body(*refs):
        ins, outs = refs[:n], refs[n:2 * n]
        send_sem, recv_sem = refs[2 * n:]
        x, y, c, _ = _place()
        cps = []
        for w in range(n):
            cp = pltpu.make_async_remote_copy(
                src_ref=ins[w], dst_ref=outs[w], send_sem=send_sem.at[w], recv_sem=recv_sem.at[w],
                device_id=(x, y, 1 - c), device_id_type=MESH)
            cp.start()
            cps.append(cp)
        for cp in cps:
            cp.wait()

    return pl.pallas_call(
        body, name="grads_back_to_sibling",
        in_specs=[ANY] * n, out_specs=[ANY] * n,
        out_shape=[jax.ShapeDtypeStruct(h.shape, h.dtype) for h in halves],
        scratch_shapes=[pltpu.SemaphoreType.DMA((n,)), pltpu.SemaphoreType.DMA((n,))],
    )(*halves)


def adamw_shard(w, m, v, own, sib, l, name):
    R, C = w.shape
    Rh = R // 2

    def body(c_ref, w_ref, m_ref, v_ref, o_ref, s_ref, d_ref, nm_ref, nv_ref, g_ref):
        gv = jnp.where(pl.program_id(0) == c_ref[0], o_ref[...], s_ref[...])
        d, m2, v2 = _adamw_math(w_ref[...], gv, m_ref[...], v_ref[...])
        d_ref[...] = d
        nm_ref[...] = m2
        nv_ref[...] = v2
        g_ref[...] = gv

    blk = pl.BlockSpec((Rh, C), lambda h, c_ref: (h, 0))
    half = pl.BlockSpec((None, Rh, C), lambda h, c_ref: (l, 0, 0))
    sds = jax.ShapeDtypeStruct((R, C), F32)
    return pl.pallas_call(
        body, name=name,
        grid_spec=pltpu.PrefetchScalarGridSpec(
            num_scalar_prefetch=1, grid=(2,), in_specs=[blk, blk, blk, half, half], out_specs=[blk] * 4),
        out_shape=[sds] * 4,
        compiler_params=_params(1),
    )(_core_index(), w, m, v, own, sib)


def all_reduce_small(part):
    R, C = part.shape

    def body(x_ref, o_ref, all_ref, send_sem, recv_sem):
        x, y, c, _ = _place()
        me = 4 * x + 2 * y + c
        all_ref[me] = x_ref[...]
        cps = []
        for k in range(1, N_DEV):
            fx, fy, fc = (k >> 2) & 1, (k >> 1) & 1, k & 1
            peer = (x ^ fx, y ^ fy, c ^ fc)
            cp = pltpu.make_async_remote_copy(
                src_ref=x_ref, dst_ref=all_ref.at[me], send_sem=send_sem.at[k - 1], recv_sem=recv_sem.at[k - 1],
                device_id=peer, device_id_type=MESH)
            cp.start()
            cps.append(cp)
        for k in range(1, N_DEV):
            fx, fy, fc = (k >> 2) & 1, (k >> 1) & 1, k & 1
            peer = (x ^ fx, y ^ fy, c ^ fc)
            src = all_ref.at[4 * peer[0] + 2 * peer[1] + peer[2]]
            pltpu.make_async_remote_copy(
                src_ref=x_ref, dst_ref=src, send_sem=send_sem.at[k - 1], recv_sem=recv_sem.at[k - 1],
                device_id=peer, device_id_type=MESH).wait_recv()
        for cp in cps:
            cp.wait_send()
        acc = all_ref[0]
        for d in range(1, N_DEV):
            acc = acc + all_ref[d]
        o_ref[...] = acc

    return pl.pallas_call(
        body, name="all_reduce_small",
        in_specs=[pl.BlockSpec(memory_space=pltpu.VMEM)], out_specs=pl.BlockSpec(memory_space=pltpu.VMEM),
        out_shape=jax.ShapeDtypeStruct((R, C), F32),
        scratch_shapes=[pltpu.VMEM((N_DEV, R, C), F32), pltpu.SemaphoreType.DMA((N_DEV - 1,)),
                        pltpu.SemaphoreType.DMA((N_DEV - 1,))],
        compiler_params=pltpu.CompilerParams(vmem_limit_bytes=VMEM_LIMIT),
    )(part)


SMALL = ("norm_ffn1", "norm_mix", "norm_ffn2", "norm_final", "ret_gn_gain", "pool_scale", "pool_w")
GROUPS = (("ffn1", ("ffn1_gate", "ffn1_up", "ffn1_down")), ("w_in", ("w_in",)), ("w_out", ("w_out",)),
          ("ffn2", ("ffn2_gate", "ffn2_up", "ffn2_down")))
TRANSPOSED = ("ffn1_gate", "ffn1_up", "ffn2_gate", "ffn2_up")
IN_W = 2 * QK_W + 3 * V_W
ORDER = ("norm_ffn1", "ffn1_gate", "ffn1_up", "ffn1_down", "norm_mix", "w_in", "ret_gn_gain", "pool_w", "pool_scale",
         "w_out", "norm_ffn2", "ffn2_gate", "ffn2_up", "ffn2_down", "norm_final")


def _pack_small(d):
    return jnp.concatenate([d[k].reshape(-1, 128) for k in SMALL], axis=0)


def _unpack_small(packed, shapes):
    out, row = {}, 0
    for k in SMALL:
        n = math.prod(shapes[k]) // 128
        out[k] = packed[row:row + n].reshape(shapes[k])
        row += n
    return out


def kernel(x, norm_ffn1, ffn1_gate, ffn1_up, ffn1_down, norm_mix, w_in, ret_gn_gain, pool_w, pool_scale, w_out, norm_ffn2, ffn2_gate, ffn2_up, ffn2_down, norm_final, loss_target, m_norm_ffn1, m_ffn1_gate, m_ffn1_up, m_ffn1_down, m_norm_mix, m_w_in, m_ret_gn_gain, m_pool_w, m_pool_scale, m_w_out, m_norm_ffn2, m_ffn2_gate, m_ffn2_up, m_ffn2_down, m_norm_final, v_norm_ffn1, v_ffn1_gate, v_ffn1_up, v_ffn1_down, v_norm_mix, v_w_in, v_ret_gn_gain, v_pool_w, v_pool_scale, v_w_out, v_norm_ffn2, v_ffn2_gate, v_ffn2_up, v_ffn2_down, v_norm_final):
    W = dict(norm_ffn1=norm_ffn1, ffn1_gate=ffn1_gate, ffn1_up=ffn1_up, ffn1_down=ffn1_down, norm_mix=norm_mix, w_in=w_in,
             ret_gn_gain=ret_gn_gain, pool_w=pool_w, pool_scale=pool_scale, w_out=w_out, norm_ffn2=norm_ffn2,
             ffn2_gate=ffn2_gate, ffn2_up=ffn2_up, ffn2_down=ffn2_down, norm_final=norm_final)
    M = dict(norm_ffn1=m_norm_ffn1, ffn1_gate=m_ffn1_gate, ffn1_up=m_ffn1_up, ffn1_down=m_ffn1_down, norm_mix=m_norm_mix,
             w_in=m_w_in, ret_gn_gain=m_ret_gn_gain, pool_w=m_pool_w, pool_scale=m_pool_scale, w_out=m_w_out,
             norm_ffn2=m_norm_ffn2, ffn2_gate=m_ffn2_gate, ffn2_up=m_ffn2_up, ffn2_down=m_ffn2_down, norm_final=m_norm_final)
    V = dict(norm_ffn1=v_norm_ffn1, ffn1_gate=v_ffn1_gate, ffn1_up=v_ffn1_up, ffn1_down=v_ffn1_down, norm_mix=v_norm_mix,
             w_in=v_w_in, ret_gn_gain=v_ret_gn_gain, pool_w=v_pool_w, pool_scale=v_pool_scale, w_out=v_w_out,
             norm_ffn2=v_norm_ffn2, ffn2_gate=v_ffn2_gate, ffn2_up=v_ffn2_up, ffn2_down=v_ffn2_down, norm_final=v_norm_final)

    n_seq, S, D = x.shape
    T = n_seq * S
    tm = min(512, T // 2)
    tm_fwd = min(1024, T // 2)
    tk = min(1024, T // 2)
    xf = x.reshape(T, D)
    tgt = loss_target.reshape(T, D)

    def local(d, k):
        a = d[k][0]
        return a.T if k in TRANSPOSED else a

    def to_out(k, a):
        return (a.T if k in TRANSPOSED else a)[None]

    loc_bf = []
    for _, members in GROUPS:
        st = jnp.stack([local(W, k).astype(BF16) for k in members])
        loc_bf.append(st.reshape(st.shape[0], 2, st.shape[1] // 2, st.shape[2]))
    gathered = gather_weights(loc_bf)
    FS = loc_bf[0].shape[2] * 2
    w3_1 = gathered[0].reshape(3, N_CHIPS, FS, D)
    win_full = gathered[1].reshape(N_CHIPS, D, IN_W // N_CHIPS)
    wout_full = gathered[2].reshape(D, D)
    w3_2 = gathered[3].reshape(3, N_CHIPS, FS, D)
    g1 = norm_ffn1.reshape(1, D)
    gm = norm_mix.reshape(1, D)
    g3 = norm_ffn2.reshape(1, D)
    gf = norm_final.reshape(1, D)
    gn_gain = ret_gn_gain.reshape(1, V_W)
    pscale = pool_scale.reshape(1, V_W)
    pw = pool_w.reshape(len(POOL_WINDOWS), POOL_GC, POOL_GC)
    consts = mixer_constants(S)

    x1, a1, b1 = ffn_fwd(xf, g1, w3_1, tm_fwd, "ffn1_fwd")
    p = in_proj(x1, gm, win_full, tm, "in_proj")
    x2, m_s, o_s, st_s, pooled_s = mixer_fwd(p, x1, consts, gn_gain, pw, pscale, wout_full, n_seq, S, "mixer_fwd")
    x3, a2, b2 = ffn_fwd(x2, g3, w3_2, tm_fwd, "ffn2_fwd")

    dx3, lacc, dgf = loss_bwd(x3, tgt, gf, tm, "loss_bwd")
    loss = lax.psum(jnp.sum(lacc) * (0.5 / D), ("x", "y", "c"))
    dx2, da2, db2, h2, n3, df2, dg3 = ffn_bwd(x2, dx3, g3, a2, b2, w3_2, tm, "ffn2_bwd")
    gr_ffn2 = ffn_wgrad(da2, db2, h2, n3, df2, tk, "ffn2_wgrad")
    dx1, dp, n2, dgm, dgn, dps, dpw = mixer_bwd(dx2, x1, p, o_s, st_s, pooled_s, consts, gm, gn_gain, pw, pscale,
                                                wout_full, win_full, n_seq, S, "mixer_bwd")
    gr_wout = tn_shared(m_s, dx2[None], tk, "dw_out")
    gr_win = tn_shared(n2, dp, tk, "dw_in")
    dx0, da1, db1, h1, n1, df1, dg1 = ffn_bwd(xf, dx1, g1, a1, b1, w3_1, tm, "ffn1_bwd")
    gr_ffn1 = ffn_wgrad(da1, db1, h1, n1, df1, tk, "ffn1_wgrad")

    g5 = [g.reshape(lb.shape[0], N_CHIPS, 2, lb.shape[2], lb.shape[3])
          for g, lb in zip((gr_ffn1, gr_win, gr_wout, gr_ffn2), loc_bf)]
    from_sibling = send_to_sibling_other_half(g5)
    chip_sums = [add_own_half(g, r, "add_sibling_" + gn) for g, r, (gn, _) in zip(g5, from_sibling, GROUPS)]
    per_chip = exchange_between_chips(chip_sums)
    my_half = [sum_chips(cs, pc, "sum_chips_" + gn) for cs, pc, (gn, _) in zip(chip_sums, per_chip, GROUPS)]
    sib_half = share_with_sibling(my_half)

    small_part = dict(norm_ffn1=dg1, norm_mix=dgm, norm_ffn2=dg3, norm_final=dgf, ret_gn_gain=dgn, pool_scale=dps, pool_w=dpw)
    small_sum = all_reduce_small(_pack_small(small_part))

    out_g, out_d, out_m, out_v = {}, {}, {}, {}
    for (_, members), own, sib in zip(GROUPS, my_half, sib_half):
        for l, k in enumerate(members):
            d_, m_, v_, g_ = adamw_shard(local(W, k), local(M, k), local(V, k), own, sib, l, "adamw_" + k)
            out_g[k], out_d[k], out_m[k], out_v[k] = to_out(k, g_), to_out(k, d_), to_out(k, m_), to_out(k, v_)
    shapes = {k: W[k].shape for k in SMALL}
    d_, m_, v_ = adamw(_pack_small(W), small_sum, _pack_small(M), _pack_small(V), "adamw_small")
    for dst, packed in ((out_g, small_sum), (out_d, d_), (out_m, m_), (out_v, v_)):
        dst.update(_unpack_small(packed, shapes))

    grad_x = dx0.reshape(n_seq, S, D)
    return (loss, grad_x, *[out_g[k] for k in ORDER], *[out_d[k] for k in ORDER],
            *[out_m[k] for k in ORDER], *[out_v[k] for k in ORDER])
```

```python
import functools
import math
from typing import Callable, NamedTuple, Optional

import jax
import jax.numpy as jnp
from jax import lax
from jax.experimental import pallas as pl
from jax.experimental.pallas import tpu as pltpu

F32 = jnp.float32
BF16 = jnp.bfloat16
MESH = pl.DeviceIdType.MESH

N_CHIPS = 4
N_DEV = 8
CHUNK = 64
RET_HEADS = 4
RET_DK = 64
RET_DV = 128
QK_W = RET_HEADS * RET_DK
V_W = RET_HEADS * RET_DV
POOL_WINDOWS = (2, 4, 8, 16)
POOL_GC = 128
ROPE_BASE = 10000.0
RMS_EPS = 1e-6
GN_EPS = 1e-5
ADAM_LR = 0.001
ADAM_B1 = 0.9
ADAM_B2 = 0.999
ADAM_EPS = 1e-08
ADAM_WD = 0.01
ADAM_STEP = 10
RET_BLOCK = 256
VMEM_LIMIT = 56 * 1024 * 1024

NN = (((1,), (0,)), ((), ()))
NT = (((1,), (1,)), ((), ()))
TN = (((0,), (0,)), ((), ()))


def _dot(a, b, dims=NN):
    return lax.dot_general(a, b, dims, preferred_element_type=F32)


def _sigmoid(x):
    return 1.0 / (1.0 + jnp.exp(-x))


def _params(n_grid):
    return pltpu.CompilerParams(dimension_semantics=("arbitrary",) * n_grid, vmem_limit_bytes=VMEM_LIMIT)


class CommPlan(NamedTuple):
    ins: tuple
    out_shape: tuple
    sems: tuple
    start: Callable
    finish: Callable
    mid: Optional[Callable] = None
    mid_at: float = 0.85


ANY = pl.BlockSpec(memory_space=pl.ANY)


def _call(body, plan, *, name, grid, in_specs, out_specs, out_shape, args, scratch_shapes=()):
    n_grid = len(grid)
    if plan is None:
        return pl.pallas_call(body, name=name, grid=grid, in_specs=in_specs, out_specs=out_specs, out_shape=out_shape,
                              scratch_shapes=scratch_shapes, compiler_params=_params(n_grid))(*args)
    n_in, n_out, n_sc = len(in_specs), len(out_specs), len(scratch_shapes)
    p_in, p_out = len(plan.ins), len(plan.out_shape)
    total = math.prod(grid)
    mid_step = min(total - 1, int(plan.mid_at * total))

    def riding(*refs):
        ins, pins = refs[:n_in], refs[n_in:n_in + p_in]
        o0 = n_in + p_in
        outs, pouts = refs[o0:o0 + n_out], refs[o0 + n_out:o0 + n_out + p_out]
        s0 = o0 + n_out + p_out
        scratch, psems = refs[s0:s0 + n_sc], refs[s0 + n_sc:]
        step = pl.program_id(0)
        for d in range(1, n_grid):
            step = step * grid[d] + pl.program_id(d)

        @pl.when(step == 0)
        def _():
            plan.start(pins, pouts, psems)

        body(*ins, *outs, *scratch)

        if plan.mid is not None:
            @pl.when(step == mid_step)
            def _():
                plan.mid(pins, pouts, psems)

        @pl.when(step == total - 1)
        def _():
            plan.finish(pins, pouts, psems)

    res = pl.pallas_call(
        riding, name=name, grid=grid, in_specs=list(in_specs) + [ANY] * p_in, out_specs=list(out_specs) + [ANY] * p_out,
        out_shape=list(out_shape) + list(plan.out_shape), scratch_shapes=list(scratch_shapes) + list(plan.sems),
        compiler_params=_params(n_grid))(*args, *plan.ins)
    return res[:n_out], res[n_out:]


def _run_plan(plan, name):
    p_in, p_out = len(plan.ins), len(plan.out_shape)

    def body(*refs):
        pins, pouts, psems = refs[:p_in], refs[p_in:p_in + p_out], refs[p_in + p_out:]
        plan.start(pins, pouts, psems)
        if plan.mid is not None:
            plan.mid(pins, pouts, psems)
        plan.finish(pins, pouts, psems)

    return pl.pallas_call(body, name=name, in_specs=[ANY] * p_in, out_specs=[ANY] * p_out,
                          out_shape=list(plan.out_shape), scratch_shapes=list(plan.sems))(*plan.ins)


def _rms_parts(x):
    r = lax.rsqrt(jnp.mean(x * x, axis=-1, keepdims=True) + RMS_EPS)
    return r, x * r


def _rms_bwd(dn, xhat, r, gain, dres):
    dxh = dn * gain
    return dres + r * (dxh - xhat * jnp.mean(dxh * xhat, axis=-1, keepdims=True))


def _w3_specs(FS, D):
    return [pl.BlockSpec((None, None, FS, D), functools.partial(lambda i, j, k: (k, j, 0, 0), k=k)) for k in range(3)]


def ffn_fwd(x, gain, w3, tm, name, plan=None):
    T, D = x.shape
    _, J, FS, _ = w3.shape

    def body(x_ref, g_ref, wg_ref, wu_ref, wd_ref, xo_ref, a_ref, b_ref, n_sc, acc):
        j = pl.program_id(1)

        @pl.when(j == 0)
        def _():
            _, xhat = _rms_parts(x_ref[...])
            n_sc[...] = (xhat * g_ref[...]).astype(BF16)
            acc[...] = jnp.zeros_like(acc)

        n = n_sc[...]
        a = _dot(n, wg_ref[...], NT)
        b = _dot(n, wu_ref[...], NT)
        a_ref[...] = a.astype(BF16)
        b_ref[...] = b.astype(BF16)
        h = (a * _sigmoid(a) * b).astype(BF16)
        acc[...] += _dot(h, wd_ref[...])

        @pl.when(j == J - 1)
        def _():
            xo_ref[...] = x_ref[...] + 0.5 * acc[...]

    return _call(
        body, plan, name=name, grid=(T // tm, J),
        in_specs=[pl.BlockSpec((tm, D), lambda i, j: (i, 0)),
                  pl.BlockSpec((1, D), lambda i, j: (0, 0))] + _w3_specs(FS, D),
        out_specs=[pl.BlockSpec((tm, D), lambda i, j: (i, 0)),
                   pl.BlockSpec((None, tm, FS), lambda i, j: (j, i, 0)),
                   pl.BlockSpec((None, tm, FS), lambda i, j: (j, i, 0))],
        out_shape=[jax.ShapeDtypeStruct((T, D), F32),
                   jax.ShapeDtypeStruct((J, T, FS), BF16),
                   jax.ShapeDtypeStruct((J, T, FS), BF16)],
        scratch_shapes=[pltpu.VMEM((tm, D), BF16), pltpu.VMEM((tm, D), F32)],
        args=(x, gain, w3, w3, w3))


def ffn_bwd(x, dout, gain, a_s, b_s, w3, tm, name, plan=None):
    T, D = x.shape
    _, J, FS, _ = w3.shape
    nt = T // tm

    def body(x_ref, do_ref, g_ref, a_ref, b_ref, wg_ref, wu_ref, wd_ref,
             dx_ref, da_ref, db_ref, h_ref, n_ref, df_ref, dgain_ref, acc):
        i = pl.program_id(0)
        j = pl.program_id(1)

        @pl.when(jnp.logical_and(i == 0, j == 0))
        def _():
            dgain_ref[...] = jnp.zeros_like(dgain_ref)

        @pl.when(j == 0)
        def _():
            _, xhat = _rms_parts(x_ref[...])
            n_ref[...] = (xhat * g_ref[...]).astype(BF16)
            df_ref[...] = (0.5 * do_ref[...]).astype(BF16)
            acc[...] = jnp.zeros_like(acc)

        a = a_ref[...].astype(F32)
        b = b_ref[...].astype(F32)
        dh = _dot(df_ref[...], wd_ref[...], NT)
        sg = _sigmoid(a)
        si = a * sg
        da = (dh * b * (sg * (1.0 + a * (1.0 - sg)))).astype(BF16)
        db = (dh * si).astype(BF16)
        da_ref[...] = da
        db_ref[...] = db
        h_ref[...] = (si * b).astype(BF16)
        acc[...] += _dot(da, wg_ref[...]) + _dot(db, wu_ref[...])

        @pl.when(j == J - 1)
        def _():
            r, xhat = _rms_parts(x_ref[...])
            dn = acc[...]
            dgain_ref[...] += jnp.sum(dn * xhat, axis=0, keepdims=True)
            dx_ref[...] = _rms_bwd(dn, xhat, r, g_ref[...], do_ref[...])

    tok = pl.BlockSpec((tm, D), lambda i, j: (i, 0))
    sh = pl.BlockSpec((None, tm, FS), lambda i, j: (j, i, 0))
    return _call(
        body, plan, name=name, grid=(nt, J),
        in_specs=[tok, tok, pl.BlockSpec((1, D), lambda i, j: (0, 0)), sh, sh] + _w3_specs(FS, D),
        out_specs=[tok, sh, sh, sh, tok, tok, pl.BlockSpec((1, D), lambda i, j: (0, 0))],
        out_shape=[jax.ShapeDtypeStruct((T, D), F32),
                   jax.ShapeDtypeStruct((J, T, FS), BF16),
                   jax.ShapeDtypeStruct((J, T, FS), BF16),
                   jax.ShapeDtypeStruct((J, T, FS), BF16),
                   jax.ShapeDtypeStruct((T, D), BF16),
                   jax.ShapeDtypeStruct((T, D), BF16),
                   jax.ShapeDtypeStruct((1, D), F32)],
        scratch_shapes=[pltpu.VMEM((tm, D), F32)],
        args=(x, dout, gain, a_s, b_s, w3, w3, w3))


def ffn_wgrad(da, db, h, n, df, tk, name, plan=None):
    J, T, FS = da.shape
    D = n.shape[1]

    def body(da_ref, db_ref, h_ref, n_ref, df_ref, o_ref):
        @pl.when(pl.program_id(1) == 0)
        def _():
            o_ref[...] = jnp.zeros_like(o_ref)

        nv = n_ref[...]
        o_ref[0] += _dot(da_ref[...], nv, TN)
        o_ref[1] += _dot(db_ref[...], nv, TN)
        o_ref[2] += _dot(h_ref[...], df_ref[...], TN)

    sh = pl.BlockSpec((None, tk, FS), lambda j, k: (j, k, 0))
    tok = pl.BlockSpec((tk, D), lambda j, k: (k, 0))
    return _call(
        body, plan, name=name, grid=(J, T // tk),
        in_specs=[sh, sh, sh, tok, tok],
        out_specs=[pl.BlockSpec((3, None, FS, D), lambda j, k: (0, j, 0, 0))],
        out_shape=[jax.ShapeDtypeStruct((3, J, FS, D), F32)],
        args=(da, db, h, n, df))


def tn_shared(a, b, tk, name, plan=None):
    T, M = a.shape
    J, _, N = b.shape

    def body(a_ref, b_ref, o_ref):
        @pl.when(pl.program_id(0) == 0)
        def _():
            o_ref[...] = jnp.zeros_like(o_ref)

        a_t = a_ref[...].astype(BF16).T
        for j in range(J):
            o_ref[j] += _dot(a_t, b_ref[j].astype(BF16))

    return _call(
        body, plan, name=name, grid=(T // tk,),
        in_specs=[pl.BlockSpec((tk, M), lambda k: (k, 0)), pl.BlockSpec((J, tk, N), lambda k: (0, k, 0))],
        out_specs=[pl.BlockSpec((J, M, N), lambda k: (0, 0, 0))],
        out_shape=[jax.ShapeDtypeStruct((J, M, N), F32)],
        args=(a, b))


def in_proj(x, gain, win, tm, name):
    T, D = x.shape
    J, _, W = win.shape

    def body(x_ref, g_ref, w_ref, p_ref):
        _, xhat = _rms_parts(x_ref[...])
        n = (xhat * g_ref[...]).astype(BF16)
        for j in range(J):
            p_ref[j] = _dot(n, w_ref[j])

    return pl.pallas_call(
        body, name=name, grid=(T // tm,),
        in_specs=[pl.BlockSpec((tm, D), lambda i: (i, 0)),
                  pl.BlockSpec((1, D), lambda i: (0, 0)),
                  pl.BlockSpec((J, D, W), lambda i: (0, 0, 0))],
        out_specs=pl.BlockSpec((J, tm, W), lambda i: (0, i, 0)),
        out_shape=jax.ShapeDtypeStruct((J, T, W), F32),
        compiler_params=_params(1),
    )(x, gain, win)


def mixer_constants(S):
    B = RET_BLOCK
    half = RET_DK // 2
    freqs = ROPE_BASE ** (-jnp.arange(half, dtype=F32) * 2.0 / RET_DK)
    ang = jnp.arange(S, dtype=F32)[:, None] * freqs[None, :]
    cos = jnp.tile(jnp.cos(ang), (1, 2 * RET_HEADS))
    sin = jnp.tile(jnp.concatenate([-jnp.sin(ang), jnp.sin(ang)], axis=1), (1, RET_HEADS))
    gamma = 1.0 - 2.0 ** (-5.0 - jnp.arange(RET_HEADS, dtype=F32))
    log_g = jnp.log(gamma)
    idx = jnp.arange(B, dtype=F32)
    ci = jnp.arange(B) // CHUNK
    dist = jnp.abs(idx[:, None] - idx[None, :])
    m_intra = jnp.exp(log_g[:, None, None] * dist[None]) * (ci[None, :] <= ci[:, None])[None].astype(F32)
    lg_lane = jnp.repeat(log_g, RET_DK)
    d_q = jnp.exp(lg_lane[None, :] * (idx[:, None] + 1.0))
    d_k = jnp.exp(lg_lane[None, :] * (B - 1.0 - idx[:, None]))
    d_s = jnp.broadcast_to(jnp.exp(lg_lane * B)[:, None], (QK_W, V_W))
    bm = (jnp.arange(QK_W)[:, None] // RET_DK == jnp.arange(V_W)[None, :] // RET_DV).astype(F32)
    win = jnp.repeat(jnp.array(POOL_WINDOWS, F32), POOL_GC)
    invc = 1.0 / jnp.minimum(jnp.arange(S, dtype=F32)[:, None] + 1.0, win[None, :])
    return dict(cos=cos, sin=sin, m=m_intra, dq=d_q, dk=d_k, ds=d_s, bm=bm, invc=invc)


def _swap_halves(x):
    w = x.shape[1]
    lane = lax.broadcasted_iota(jnp.int32, x.shape, 1)
    return jnp.where((lane % RET_DK) < RET_DK // 2, pltpu.roll(x, w - RET_DK // 2, 1), pltpu.roll(x, RET_DK // 2, 1))


def _rot(x, cos, sin):
    return x * cos + _swap_halves(x) * sin


def _rot_t(d, cos, sin):
    return d * cos + _swap_halves(d * sin)


def _lane_groups(parts):
    return jnp.concatenate([p[:, POOL_GC * g:POOL_GC * (g + 1)] for g, p in enumerate(parts)], axis=1)


def _head_mask(shape, h):
    lane = lax.broadcasted_iota(jnp.int32, shape, 1)
    return (lane // RET_DK) == h


def _group_norm(o):
    yh, rs = [], []
    for h in range(RET_HEADS):
        oh = o[:, RET_DV * h:RET_DV * (h + 1)]
        xc = oh - jnp.mean(oh, axis=-1, keepdims=True)
        r = lax.rsqrt(jnp.mean(xc * xc, axis=-1, keepdims=True) + GN_EPS)
        yh.append(xc * r)
        rs.append(r)
    return yh, rs


def mixer_fwd(p, x1, consts, gn_gain, pool_w, pool_scale, wout, n_seq, S, name):
    _, T, _ = p.shape
    D = x1.shape[1]
    B = RET_BLOCK
    nb = S // B

    def body(p_ref, x1_ref, cos_ref, sin_ref, invc_ref, m_ref, dq_ref, dk_ref, ds_ref, bm_ref,
             gain_ref, pw_ref, sc_ref, wout_ref,
             x2_ref, m_out, o_out, st_out, pooled_out, state, prev_u):
        blk = pl.program_id(1)

        @pl.when(blk == 0)
        def _():
            state[...] = jnp.zeros_like(state)
            prev_u[...] = jnp.zeros_like(prev_u)

        qk = p_ref[0]
        v = p_ref[1]
        g = p_ref[2]
        u = p_ref[3]
        cos = cos_ref[...]
        sin = sin_ref[...]
        qr = _rot(qk[:, :QK_W], cos, sin) * (RET_DK ** -0.5)
        kr = _rot(qk[:, QK_W:], cos, sin)
        qb = qr.astype(BF16)
        kb = kr.astype(BF16)
        vb = v.astype(BF16)
        st = state[...]
        st_out[...] = st
        cross = _dot((qr * dq_ref[...]).astype(BF16), st.astype(BF16))
        o_parts = []
        for h in range(RET_HEADS):
            qm = jnp.where(_head_mask(qb.shape, h), qb, jnp.zeros_like(qb))
            sc = (_dot(qm, kb, NT) * m_ref[h]).astype(BF16)
            o_parts.append(_dot(sc, vb[:, RET_DV * h:RET_DV * (h + 1)]) + cross[:, RET_DV * h:RET_DV * (h + 1)])
        o = jnp.concatenate(o_parts, axis=1)
        o_out[...] = o
        kv = _dot((kr * dk_ref[...]).astype(BF16), vb, TN)
        state[...] = st * ds_ref[...] + kv * bm_ref[...]

        yh, _ = _group_norm(o)
        r = g * _sigmoid(g) * (jnp.concatenate(yh, axis=1) * gain_ref[...])

        ext = jnp.concatenate([prev_u[...], u], axis=0)
        sums = []
        run = ext
        for k in (1, 2, 4, 8):
            run = run + pltpu.roll(run, k, 0)
            sums.append(run[B:, :])
        prev_u[...] = u
        pooled = (_lane_groups(sums) * invc_ref[...] - u).astype(BF16)
        pooled_out[...] = pooled
        yp = [_dot(pooled[:, POOL_GC * gi:POOL_GC * (gi + 1)], pw_ref[gi].astype(BF16)) for gi in range(len(POOL_WINDOWS))]
        s = jnp.concatenate(yp, axis=1) * sc_ref[...]
        m = jnp.concatenate([r, s], axis=1).astype(BF16)
        m_out[...] = m
        x2_ref[...] = x1_ref[...] + _dot(m, wout_ref[...])

    tokmap = lambda b, k: (b * nb + k, 0)
    seqmap = lambda b, k: (k, 0)
    const2 = lambda b, k: (0, 0)
    const3 = lambda b, k: (0, 0, 0)
    return pl.pallas_call(
        body, name=name, grid=(n_seq, nb),
        in_specs=[pl.BlockSpec((4, B, V_W), lambda b, k: (0, b * nb + k, 0)),
                  pl.BlockSpec((B, D), tokmap),
                  pl.BlockSpec((B, QK_W), seqmap), pl.BlockSpec((B, QK_W), seqmap), pl.BlockSpec((B, V_W), seqmap),
                  pl.BlockSpec((RET_HEADS, B, B), const3),
                  pl.BlockSpec((B, QK_W), const2), pl.BlockSpec((B, QK_W), const2),
                  pl.BlockSpec((QK_W, V_W), const2), pl.BlockSpec((QK_W, V_W), const2),
                  pl.BlockSpec((1, V_W), const2), pl.BlockSpec((4, POOL_GC, POOL_GC), const3),
                  pl.BlockSpec((1, V_W), const2), pl.BlockSpec((D, D), const2)],
        out_specs=[pl.BlockSpec((B, D), tokmap), pl.BlockSpec((B, D), tokmap), pl.BlockSpec((B, V_W), tokmap),
                   pl.BlockSpec((None, QK_W, V_W), lambda b, k: (b * nb + k, 0, 0)),
                   pl.BlockSpec((B, V_W), tokmap)],
        out_shape=[jax.ShapeDtypeStruct((T, D), F32), jax.ShapeDtypeStruct((T, D), BF16),
                   jax.ShapeDtypeStruct((T, V_W), F32), jax.ShapeDtypeStruct((T // B, QK_W, V_W), F32),
                   jax.ShapeDtypeStruct((T, V_W), BF16)],
        scratch_shapes=[pltpu.VMEM((QK_W, V_W), F32), pltpu.VMEM((B, V_W), F32)],
        compiler_params=_params(2),
    )(p, x1, consts["cos"], consts["sin"], consts["invc"], consts["m"], consts["dq"], consts["dk"],
      consts["ds"], consts["bm"], gn_gain, pool_w, pool_scale, wout)


def mixer_bwd(dx2, x1, p, o_s, st_s, pooled_s, consts, gmix, gn_gain, pool_w, pool_scale, wout, win, n_seq, S, name,
              plan=None):
    T, D = dx2.shape
    B = RET_BLOCK
    nb = S // B
    NG = len(POOL_WINDOWS)

    def body(dx2_ref, x1_ref, p_ref, o_ref, st_ref, pooled_ref, cos_ref, sin_ref, invc_ref, m_ref, dq_ref, dk_ref,
             ds_ref, bm_ref, gmix_ref, gain_ref, pw_ref, sc_ref, wout_ref, win_ref,
             dx1_ref, dp_ref, n2_ref, dgmix_ref, dgain_ref, dscale_ref, dpw_ref, gstate, next_e):
        b = pl.program_id(0)
        blk = pl.program_id(1)

        @pl.when(jnp.logical_and(b == 0, blk == 0))
        def _():
            dgmix_ref[...] = jnp.zeros_like(dgmix_ref)
            dgain_ref[...] = jnp.zeros_like(dgain_ref)
            dscale_ref[...] = jnp.zeros_like(dscale_ref)
            dpw_ref[...] = jnp.zeros_like(dpw_ref)

        @pl.when(blk == 0)
        def _():
            gstate[...] = jnp.zeros_like(gstate)
            next_e[...] = jnp.zeros_like(next_e)

        dx2v = dx2_ref[...]
        dm = _dot(dx2v.astype(BF16), wout_ref[...], NT)
        dr = dm[:, :V_W]
        dsv = dm[:, V_W:]

        pooled = pooled_ref[...]
        scale = sc_ref[...]
        dyp = (dsv * scale).astype(BF16)
        yp, dpl = [], []
        for gi in range(NG):
            sl = slice(POOL_GC * gi, POOL_GC * (gi + 1))
            pwb = pw_ref[gi].astype(BF16)
            yp.append(_dot(pooled[:, sl], pwb))
            dpw_ref[gi] += _dot(pooled[:, sl], dyp[:, sl], TN)
            dpl.append(_dot(dyp[:, sl], pwb, NT))
        dscale_ref[...] += jnp.sum(dsv * jnp.concatenate(yp, axis=1), axis=0, keepdims=True)
        dpooled = jnp.concatenate(dpl, axis=1)
        e = dpooled * invc_ref[...]
        ext = jnp.concatenate([e, next_e[...]], axis=0)
        sums = []
        run = ext
        for k in (1, 2, 4, 8):
            run = run + pltpu.roll(run, 2 * B - k, 0)
            sums.append(run[:B, :])
        next_e[...] = e
        du = _lane_groups(sums) - dpooled

        qk = p_ref[0]
        v = p_ref[1]
        g = p_ref[2]
        sg = _sigmoid(g)
        si = g * sg
        yh, rs = _group_norm(o_ref[...])
        yhat = jnp.concatenate(yh, axis=1)
        gain = gain_ref[...]
        dg = dr * (yhat * gain) * (sg * (1.0 + g * (1.0 - sg)))
        dy = dr * si
        dgain_ref[...] += jnp.sum(dy * yhat, axis=0, keepdims=True)
        dyh = dy * gain
        do_parts = []
        for h in range(RET_HEADS):
            sl = slice(RET_DV * h, RET_DV * (h + 1))
            dh_ = dyh[:, sl]
            m1 = jnp.mean(dh_, axis=-1, keepdims=True)
            m2 = jnp.mean(dh_ * yh[h], axis=-1, keepdims=True)
            do_parts.append(rs[h] * (dh_ - m1 - yh[h] * m2))
        dob = jnp.concatenate(do_parts, axis=1).astype(BF16)

        cos = cos_ref[...]
        sin = sin_ref[...]
        qr = _rot(qk[:, :QK_W], cos, sin) * (RET_DK ** -0.5)
        kr = _rot(qk[:, QK_W:], cos, sin)
        qb = qr.astype(BF16)
        kb = kr.astype(BF16)
        vb = v.astype(BF16)
        dqd = dq_ref[...]
        dkd = dk_ref[...]
        stb = st_ref[...].astype(BF16)
        gs = gstate[...]
        gb = gs.astype(BF16)
        dqs = _dot(dob, stb, NT) * dqd
        dkr = _dot(vb, gb, NT) * dkd
        dv_cross = _dot((kr * dkd).astype(BF16), gb)
        ds_cross = _dot((qr * dqd).astype(BF16), dob, TN) * bm_ref[...]
        gstate[...] = ds_cross + gs * ds_ref[...]
        dv_parts = []
        for h in range(RET_HEADS):
            sl = slice(RET_DV * h, RET_DV * (h + 1))
            hm = _head_mask(qb.shape, h)
            qm = jnp.where(hm, qb, jnp.zeros_like(qb))
            mh = m_ref[h]
            sc = (_dot(qm, kb, NT) * mh).astype(BF16)
            dsc = (_dot(dob[:, sl], vb[:, sl], NT) * mh).astype(BF16)
            dqs = dqs + jnp.where(hm, _dot(dsc, kb), 0.0)
            dkr = dkr + jnp.where(hm, _dot(dsc, qb, TN), 0.0)
            dv_parts.append(_dot(sc, dob[:, sl], TN) + dv_cross[:, sl])
        dq = _rot_t(dqs * (RET_DK ** -0.5), cos, sin)
        dk = _rot_t(dkr, cos, sin)
        dp = [jnp.concatenate([dq, dk], axis=1).astype(BF16), jnp.concatenate(dv_parts, axis=1).astype(BF16),
              dg.astype(BF16), du.astype(BF16)]
        dn = jnp.zeros((B, D), F32)
        for jj in range(4):
            dp_ref[jj] = dp[jj]
            dn = dn + _dot(dp[jj], win_ref[jj], NT)

        x1v = x1_ref[...]
        r, xhat = _rms_parts(x1v)
        gm = gmix_ref[...]
        n2_ref[...] = (xhat * gm).astype(BF16)
        dgmix_ref[...] += jnp.sum(dn * xhat, axis=0, keepdims=True)
        dx1_ref[...] = _rms_bwd(dn, xhat, r, gm, dx2v)

    rev = lambda b, k: (b * nb + (nb - 1 - k), 0)
    seqrev = lambda b, k: (nb - 1 - k, 0)
    const2 = lambda b, k: (0, 0)
    const3 = lambda b, k: (0, 0, 0)
    return _call(
        body, plan, name=name, grid=(n_seq, nb),
        in_specs=[pl.BlockSpec((B, D), rev), pl.BlockSpec((B, D), rev),
                  pl.BlockSpec((4, B, V_W), lambda b, k: (0, b * nb + (nb - 1 - k), 0)),
                  pl.BlockSpec((B, V_W), rev),
                  pl.BlockSpec((None, QK_W, V_W), lambda b, k: (b * nb + (nb - 1 - k), 0, 0)),
                  pl.BlockSpec((B, V_W), rev),
                  pl.BlockSpec((B, QK_W), seqrev), pl.BlockSpec((B, QK_W), seqrev), pl.BlockSpec((B, V_W), seqrev),
                  pl.BlockSpec((RET_HEADS, B, B), const3),
                  pl.BlockSpec((B, QK_W), const2), pl.BlockSpec((B, QK_W), const2),
                  pl.BlockSpec((QK_W, V_W), const2), pl.BlockSpec((QK_W, V_W), const2),
                  pl.BlockSpec((1, D), const2), pl.BlockSpec((1, V_W), const2),
                  pl.BlockSpec((NG, POOL_GC, POOL_GC), const3), pl.BlockSpec((1, V_W), const2),
                  pl.BlockSpec((D, D), const2), pl.BlockSpec((4, D, V_W), const3)],
        out_specs=[pl.BlockSpec((B, D), rev),
                   pl.BlockSpec((4, B, V_W), lambda b, k: (0, b * nb + (nb - 1 - k), 0)),
                   pl.BlockSpec((B, D), rev),
                   pl.BlockSpec((1, D), const2), pl.BlockSpec((1, V_W), const2), pl.BlockSpec((1, V_W), const2),
                   pl.BlockSpec((NG, POOL_GC, POOL_GC), const3)],
        out_shape=[jax.ShapeDtypeStruct((T, D), F32), jax.ShapeDtypeStruct((4, T, V_W), BF16),
                   jax.ShapeDtypeStruct((T, D), BF16),
                   jax.ShapeDtypeStruct((1, D), F32), jax.ShapeDtypeStruct((1, V_W), F32),
                   jax.ShapeDtypeStruct((1, V_W), F32), jax.ShapeDtypeStruct((NG, POOL_GC, POOL_GC), F32)],
        scratch_shapes=[pltpu.VMEM((QK_W, V_W), F32), pltpu.VMEM((B, V_W), F32)],
        args=(dx2, x1, p, o_s, st_s, pooled_s, consts["cos"], consts["sin"], consts["invc"], consts["m"], consts["dq"],
              consts["dk"], consts["ds"], consts["bm"], gmix, gn_gain, pool_w, pool_scale, wout, win))


def loss_bwd(x3, tgt, gain, tm, name):
    T, D = x3.shape

    def body(x_ref, t_ref, g_ref, dx_ref, lacc_ref, dg_ref):
        @pl.when(pl.program_id(0) == 0)
        def _():
            lacc_ref[...] = jnp.zeros_like(lacc_ref)
            dg_ref[...] = jnp.zeros_like(dg_ref)

        r, xhat = _rms_parts(x_ref[...])
        gv = g_ref[...]
        err = xhat * gv - t_ref[...]
        lacc_ref[...] += jnp.sum(err * err, axis=0, keepdims=True)
        dy = err * (1.0 / D)
        dg_ref[...] += jnp.sum(dy * xhat, axis=0, keepdims=True)
        dx_ref[...] = _rms_bwd(dy, xhat, r, gv, 0.0)

    tok = pl.BlockSpec((tm, D), lambda i: (i, 0))
    vec = pl.BlockSpec((1, D), lambda i: (0, 0))
    return pl.pallas_call(
        body, name=name, grid=(T // tm,),
        in_specs=[tok, tok, vec], out_specs=[tok, vec, vec],
        out_shape=[jax.ShapeDtypeStruct((T, D), F32), jax.ShapeDtypeStruct((1, D), F32), jax.ShapeDtypeStruct((1, D), F32)],
        compiler_params=_params(1),
    )(x3, tgt, gain)


def _adamw_math(w, g, m, v):
    m2 = ADAM_B1 * m + (1.0 - ADAM_B1) * g
    v2 = ADAM_B2 * v + (1.0 - ADAM_B2) * (g * g)
    m_hat = m2 / (1.0 - ADAM_B1 ** ADAM_STEP)
    v_hat = v2 / (1.0 - ADAM_B2 ** ADAM_STEP)
    return -ADAM_LR * (m_hat / (jnp.sqrt(v_hat) + ADAM_EPS) + ADAM_WD * w), m2, v2


def adamw(w, g, m, v, name):
    R, C = w.shape

    def body(w_ref, g_ref, m_ref, v_ref, d_ref, nm_ref, nv_ref):
        d_ref[...], nm_ref[...], nv_ref[...] = _adamw_math(w_ref[...], g_ref[...], m_ref[...], v_ref[...])

    blk = pl.BlockSpec((R, C), lambda i: (0, 0))
    sds = jax.ShapeDtypeStruct((R, C), F32)
    return pl.pallas_call(
        body, name=name, grid=(1,), in_specs=[blk] * 4, out_specs=[blk] * 3, out_shape=[sds] * 3,
        compiler_params=_params(1),
    )(w, g, m, v)


def _place():
    x, y, c = lax.axis_index("x"), lax.axis_index("y"), lax.axis_index("c")
    other_chips = [(1 - x, y), (x, 1 - y), (1 - x, 1 - y)]
    return x, y, c, other_chips


def _exchange_plan(ins, out_shape, copies, n_copies):
    def descriptors(pins, pouts, psems):
        send, recv = psems
        return [pltpu.make_async_remote_copy(src_ref=s, dst_ref=d, send_sem=send.at[i], recv_sem=recv.at[i],
                                             device_id=dev, device_id_type=MESH)
                for i, (s, d, dev) in enumerate(copies(pins, pouts))]

    def start(pins, pouts, psems):
        for cp in descriptors(pins, pouts, psems):
            cp.start()

    def finish(pins, pouts, psems):
        for cp in descriptors(pins, pouts, psems):
            cp.wait()

    return CommPlan(tuple(ins), tuple(out_shape),
                    (pltpu.SemaphoreType.DMA((n_copies,)), pltpu.SemaphoreType.DMA((n_copies,))), start, finish)


def gather_plan(shards):
    n = len(shards)

    def start(pins, pouts, psems):
        x, y, c, chips = _place()
        mine = 2 * x + y
        for w in range(n):
            for k, (px, py) in enumerate(chips):
                pltpu.make_async_remote_copy(
                    src_ref=pins[w].at[:, c], dst_ref=pouts[w].at[:, mine, c],
                    send_sem=psems[0].at[w, k], recv_sem=psems[1].at[w, k],
                    device_id=(px, py, c), device_id_type=MESH).start()

    def mid(pins, pouts, psems):
        x, y, c, chips = _place()
        for w in range(n):
            for k, (px, py) in enumerate(chips):
                landed = pouts[w].at[:, 2 * px + py, c]
                pltpu.make_async_remote_copy(
                    src_ref=landed, dst_ref=landed, send_sem=psems[0].at[w, k], recv_sem=psems[1].at[w, k],
                    device_id=(px, py, c), device_id_type=MESH).wait_recv()
                pltpu.make_async_remote_copy(
                    src_ref=landed, dst_ref=landed, send_sem=psems[2].at[w, k], recv_sem=psems[3].at[w, k],
                    device_id=(x, y, 1 - c), device_id_type=MESH).start()

    def finish(pins, pouts, psems):
        x, y, c, chips = _place()
        mine = 2 * x + y
        for w in range(n):
            for k, (px, py) in enumerate(chips):
                landed = pouts[w].at[:, 2 * px + py, 1 - c]
                cp = pltpu.make_async_remote_copy(
                    src_ref=landed, dst_ref=landed, send_sem=psems[2].at[w, k], recv_sem=psems[3].at[w, k],
                    device_id=(x, y, 1 - c), device_id_type=MESH)
                cp.wait_recv()
                cp.wait_send()
                pltpu.make_async_remote_copy(
                    src_ref=pins[w].at[:, c], dst_ref=pouts[w].at[:, mine, c],
                    send_sem=psems[0].at[w, k], recv_sem=psems[1].at[w, k],
                    device_id=(px, py, c), device_id_type=MESH).wait_send()

    return CommPlan(tuple(shards),
                    tuple(jax.ShapeDtypeStruct((s.shape[0], N_CHIPS) + s.shape[1:], s.dtype) for s in shards),
                    tuple(pltpu.SemaphoreType.DMA((n, 3)) for _ in range(4)), start, finish, mid)


def place_own(gathered, shard, name):
    L, _, Rh, C = shard.shape

    def body(chip_ref, g_ref, s_ref, o_ref):
        o_ref[...] = s_ref[...]

    return pl.pallas_call(
        body, name=name,
        grid_spec=pltpu.PrefetchScalarGridSpec(
            num_scalar_prefetch=1, grid=(L,),
            in_specs=[ANY, pl.BlockSpec((None, 2, Rh, C), lambda l, chip_ref: (l, 0, 0, 0))],
            out_specs=pl.BlockSpec((None, None, 2, Rh, C), lambda l, chip_ref: (l, chip_ref[0], 0, 0, 0))),
        out_shape=jax.ShapeDtypeStruct(gathered.shape, gathered.dtype),
        input_output_aliases={1: 0},
        compiler_params=_params(1),
    )(_chip_index(), gathered, shard)


def gather_weights(shards):
    n = len(shards)

    def body(*refs):
        ins, outs, bufs = refs[:n], refs[n:2 * n], refs[2 * n:3 * n]
        ld_sem, st_sem, ici_send, ici_recv, d2d_send, d2d_recv = refs[3 * n:]
        x, y, c, chips = _place()
        sibling = (x, y, 1 - c)
        mine = 2 * x + y
        loads = [pltpu.make_async_copy(ins[w], bufs[w], ld_sem.at[w]) for w in range(n)]
        for cp in loads:
            cp.start()
        stores, sends = [], []
        for w in range(n):
            loads[w].wait()
            cp = pltpu.make_async_copy(bufs[w], outs[w].at[:, mine], st_sem.at[w])
            cp.start()
            stores.append(cp)
            for k, (px, py) in enumerate(chips):
                cp = pltpu.make_async_remote_copy(
                    src_ref=bufs[w].at[:, c], dst_ref=outs[w].at[:, mine, c],
                    send_sem=ici_send.at[w, k], recv_sem=ici_recv.at[w, k],
                    device_id=(px, py, c), device_id_type=MESH)
                cp.start()
                sends.append(cp)
        for w in range(n):
            for k, (px, py) in enumerate(chips):
                landed = outs[w].at[:, 2 * px + py, c]
                pltpu.make_async_remote_copy(
                    src_ref=landed, dst_ref=landed, send_sem=ici_send.at[w, k], recv_sem=ici_recv.at[w, k],
                    device_id=(px, py, c), device_id_type=MESH).wait_recv()
                cp = pltpu.make_async_remote_copy(
                    src_ref=landed, dst_ref=landed, send_sem=d2d_send.at[w, k], recv_sem=d2d_recv.at[w, k],
                    device_id=sibling, device_id_type=MESH)
                cp.start()
                sends.append(cp)
        for w in range(n):
            for k, (px, py) in enumerate(chips):
                landed = outs[w].at[:, 2 * px + py, 1 - c]
                pltpu.make_async_remote_copy(
                    src_ref=landed, dst_ref=landed, send_sem=d2d_send.at[w, k], recv_sem=d2d_recv.at[w, k],
                    device_id=sibling, device_id_type=MESH).wait_recv()
        for cp in sends:
            cp.wait_send()
        for cp in stores:
            cp.wait()

    return pl.pallas_call(
        body, name="gather_weights",
        in_specs=[ANY] * n, out_specs=[ANY] * n,
        out_shape=[jax.ShapeDtypeStruct((s.shape[0], N_CHIPS) + s.shape[1:], s.dtype) for s in shards],
        scratch_shapes=[pltpu.VMEM(s.shape, s.dtype) for s in shards] +
                       [pltpu.SemaphoreType.DMA((n,)), pltpu.SemaphoreType.DMA((n,)),
                        pltpu.SemaphoreType.DMA((n, 3)), pltpu.SemaphoreType.DMA((n, 3)),
                        pltpu.SemaphoreType.DMA((n, 3)), pltpu.SemaphoreType.DMA((n, 3))],
        compiler_params=pltpu.CompilerParams(vmem_limit_bytes=VMEM_LIMIT),
    )(*shards)


def send_to_sibling_other_half(grads):
    def copies(ins, outs):
        x, y, c, _ = _place()
        return [(ins[w].at[:, :, 1 - c], outs[w], (x, y, 1 - c)) for w in range(len(grads))]

    return _exchange_plan(grads, [jax.ShapeDtypeStruct(g.shape[:2] + g.shape[3:], g.dtype) for g in grads], copies,
                          len(grads))


def _core_index():
    return jnp.reshape(lax.axis_index("c"), (1,)).astype(jnp.int32)


def _chip_index():
    return jnp.reshape(2 * lax.axis_index("x") + lax.axis_index("y"), (1,)).astype(jnp.int32)


def add_own_half(g, recv, name):
    L, J, _, Rh, C = g.shape

    def body(c_ref, g_ref, r_ref, o_ref):
        o_ref[...] = (g_ref[...] + r_ref[...]).astype(BF16)

    return pl.pallas_call(
        body, name=name,
        grid_spec=pltpu.PrefetchScalarGridSpec(
            num_scalar_prefetch=1, grid=(L, J),
            in_specs=[pl.BlockSpec((None, None, None, Rh, C), lambda l, j, c_ref: (l, j, c_ref[0], 0, 0)),
                      pl.BlockSpec((None, None, Rh, C), lambda l, j, c_ref: (l, j, 0, 0))],
            out_specs=pl.BlockSpec((None, None, Rh, C), lambda l, j, c_ref: (l, j, 0, 0))),
        out_shape=jax.ShapeDtypeStruct((L, J, Rh, C), BF16),
        compiler_params=_params(2),
    )(_core_index(), g, recv)


def exchange_between_chips(sums):
    def copies(ins, outs):
        x, y, c, chips = _place()
        return [(ins[w].at[:, 2 * px + py], outs[w].at[k], (px, py, c))
                for w in range(len(sums)) for k, (px, py) in enumerate(chips)]

    return _exchange_plan(sums, [jax.ShapeDtypeStruct((3, s.shape[0]) + s.shape[2:], s.dtype) for s in sums], copies,
                          3 * len(sums))


def sum_chips(own, recv, name):
    L, _, Rh, C = own.shape

    def body(chip_ref, o_ref, r_ref, out_ref):
        acc = o_ref[...].astype(F32)
        for k in range(3):
            acc = acc + r_ref[k].astype(F32)
        out_ref[...] = acc

    return pl.pallas_call(
        body, name=name,
        grid_spec=pltpu.PrefetchScalarGridSpec(
            num_scalar_prefetch=1, grid=(L,),
            in_specs=[pl.BlockSpec((None, None, Rh, C), lambda l, chip_ref: (l, chip_ref[0], 0, 0)),
                      pl.BlockSpec((3, None, Rh, C), lambda l, chip_ref: (0, l, 0, 0))],
            out_specs=pl.BlockSpec((None, Rh, C), lambda l, chip_ref: (l, 0, 0))),
        out_shape=jax.ShapeDtypeStruct((L, Rh, C), F32),
        compiler_params=_params(1),
    )(_chip_index(), own, recv)


def share_with_sibling(halves):
    def copies(ins, outs):
        x, y, c, _ = _place()
        return [(ins[w], outs[w], (x, y, 1 - c)) for w in range(len(halves))]

    return _exchange_plan(halves, [jax.ShapeDtypeStruct(h.shape, h.dtype) for h in halves], copies, len(halves))


def adamw_shard(w, m, v, own, sib, l, name):
    R, C = w.shape
    Rh = R // 2

    def body(c_ref, w_ref, m_ref, v_ref, o_ref, s_ref, d_ref, nm_ref, nv_ref, g_ref):
        gv = jnp.where(pl.program_id(0) == c_ref[0], o_ref[...], s_ref[...])
        d, m2, v2 = _adamw_math(w_ref[...], gv, m_ref[...], v_ref[...])
        d_ref[...] = d
        nm_ref[...] = m2
        nv_ref[...] = v2
        g_ref[...] = gv

    blk = pl.BlockSpec((Rh, C), lambda h, c_ref: (h, 0))
    half = pl.BlockSpec((None, Rh, C), lambda h, c_ref: (l, 0, 0))
    sds = jax.ShapeDtypeStruct((R, C), F32)
    return pl.pallas_call(
        body, name=name,
        grid_spec=pltpu.PrefetchScalarGridSpec(
            num_scalar_prefetch=1, grid=(2,), in_specs=[blk, blk, blk, half, half], out_specs=[blk] * 4),
        out_shape=[sds] * 4,
        compiler_params=_params(1),
    )(_core_index(), w, m, v, own, sib)


def all_reduce_small(part):
    R, C = part.shape

    def body(x_ref, o_ref, all_ref, send_sem, recv_sem):
        x, y, c, _ = _place()
        me = 4 * x + 2 * y + c
        all_ref[me] = x_ref[...]
        cps = []
        for k in range(1, N_DEV):
            fx, fy, fc = (k >> 2) & 1, (k >> 1) & 1, k & 1
            peer = (x ^ fx, y ^ fy, c ^ fc)
            cp = pltpu.make_async_remote_copy(
                src_ref=x_ref, dst_ref=all_ref.at[me], send_sem=send_sem.at[k - 1], recv_sem=recv_sem.at[k - 1],
                device_id=peer, device_id_type=MESH)
            cp.start()
            cps.append(cp)
        for k in range(1, N_DEV):
            fx, fy, fc = (k >> 2) & 1, (k >> 1) & 1, k & 1
            peer = (x ^ fx, y ^ fy, c ^ fc)
            src = all_ref.at[4 * peer[0] + 2 * peer[1] + peer[2]]
            pltpu.make_async_remote_copy(
                src_ref=x_ref, dst_ref=src, send_sem=send_sem.at[k - 1], recv_sem=recv_sem.at[k - 1],
                device_id=peer, device_id_type=MESH).wait_recv()
        for cp in cps:
            cp.wait_send()
        acc = all_ref[0]
        for d in range(1, N_DEV):
            acc = acc + all_ref[d]
        o_ref[...] = acc

    return pl.pallas_call(
        body, name="all_reduce_small",
        in_specs=[pl.BlockSpec(memory_space=pltpu.VMEM)], out_specs=pl.BlockSpec(memory_space=pltpu.VMEM),
        out_shape=jax.ShapeDtypeStruct((R, C), F32),
        scratch_shapes=[pltpu.VMEM((N_DEV, R, C), F32), pltpu.SemaphoreType.DMA((N_DEV - 1,)),
                        pltpu.SemaphoreType.DMA((N_DEV - 1,))],
        compiler_params=pltpu.CompilerParams(vmem_limit_bytes=VMEM_LIMIT),
    )(part)


SMALL = ("norm_ffn1", "norm_mix", "norm_ffn2", "norm_final", "ret_gn_gain", "pool_scale", "pool_w")
GROUPS = (("ffn1", ("ffn1_gate", "ffn1_up", "ffn1_down")), ("w_in", ("w_in",)), ("w_out", ("w_out",)),
          ("ffn2", ("ffn2_gate", "ffn2_up", "ffn2_down")))
TRANSPOSED = ("ffn1_gate", "ffn1_up", "ffn2_gate", "ffn2_up")
IN_W = 2 * QK_W + 3 * V_W
ORDER = ("norm_ffn1", "ffn1_gate", "ffn1_up", "ffn1_down", "norm_mix", "w_in", "ret_gn_gain", "pool_w", "pool_scale",
         "w_out", "norm_ffn2", "ffn2_gate", "ffn2_up", "ffn2_down", "norm_final")


def _pack_small(d):
    return jnp.concatenate([d[k].reshape(-1, 128) for k in SMALL], axis=0)


def _unpack_small(packed, shapes):
    out, row = {}, 0
    for k in SMALL:
        n = math.prod(shapes[k]) // 128
        out[k] = packed[row:row + n].reshape(shapes[k])
        row += n
    return out


def kernel(x, norm_ffn1, ffn1_gate, ffn1_up, ffn1_down, norm_mix, w_in, ret_gn_gain, pool_w, pool_scale, w_out, norm_ffn2, ffn2_gate, ffn2_up, ffn2_down, norm_final, loss_target, m_norm_ffn1, m_ffn1_gate, m_ffn1_up, m_ffn1_down, m_norm_mix, m_w_in, m_ret_gn_gain, m_pool_w, m_pool_scale, m_w_out, m_norm_ffn2, m_ffn2_gate, m_ffn2_up, m_ffn2_down, m_norm_final, v_norm_ffn1, v_ffn1_gate, v_ffn1_up, v_ffn1_down, v_norm_mix, v_w_in, v_ret_gn_gain, v_pool_w, v_pool_scale, v_w_out, v_norm_ffn2, v_ffn2_gate, v_ffn2_up, v_ffn2_down, v_norm_final):
    W = dict(norm_ffn1=norm_ffn1, ffn1_gate=ffn1_gate, ffn1_up=ffn1_up, ffn1_down=ffn1_down, norm_mix=norm_mix, w_in=w_in,
             ret_gn_gain=ret_gn_gain, pool_w=pool_w, pool_scale=pool_scale, w_out=w_out, norm_ffn2=norm_ffn2,
             ffn2_gate=ffn2_gate, ffn2_up=ffn2_up, ffn2_down=ffn2_down, norm_final=norm_final)
    M = dict(norm_ffn1=m_norm_ffn1, ffn1_gate=m_ffn1_gate, ffn1_up=m_ffn1_up, ffn1_down=m_ffn1_down, norm_mix=m_norm_mix,
             w_in=m_w_in, ret_gn_gain=m_ret_gn_gain, pool_w=m_pool_w, pool_scale=m_pool_scale, w_out=m_w_out,
             norm_ffn2=m_norm_ffn2, ffn2_gate=m_ffn2_gate, ffn2_up=m_ffn2_up, ffn2_down=m_ffn2_down, norm_final=m_norm_final)
    V = dict(norm_ffn1=v_norm_ffn1, ffn1_gate=v_ffn1_gate, ffn1_up=v_ffn1_up, ffn1_down=v_ffn1_down, norm_mix=v_norm_mix,
             w_in=v_w_in, ret_gn_gain=v_ret_gn_gain, pool_w=v_pool_w, pool_scale=v_pool_scale, w_out=v_w_out,
             norm_ffn2=v_norm_ffn2, ffn2_gate=v_ffn2_gate, ffn2_up=v_ffn2_up, ffn2_down=v_ffn2_down, norm_final=v_norm_final)

    n_seq, S, D = x.shape
    T = n_seq * S
    tm = min(512, T // 2)
    tm_fwd = min(1024, T // 2)
    tk = min(1024, T // 2)
    xf = x.reshape(T, D)
    tgt = loss_target.reshape(T, D)

    def local(d, k):
        a = d[k][0]
        return a.T if k in TRANSPOSED else a

    def to_out(k, a):
        return (a.T if k in TRANSPOSED else a)[None]

    loc_bf = []
    for _, members in GROUPS:
        st = jnp.stack([local(W, k).astype(BF16) for k in members])
        loc_bf.append(st.reshape(st.shape[0], 2, st.shape[1] // 2, st.shape[2]))
    FS = loc_bf[0].shape[2] * 2
    w3_1 = gather_weights(loc_bf[:1])[0].reshape(3, N_CHIPS, FS, D)
    g1 = norm_ffn1.reshape(1, D)
    gm = norm_mix.reshape(1, D)
    g3 = norm_ffn2.reshape(1, D)
    gf = norm_final.reshape(1, D)
    gn_gain = ret_gn_gain.reshape(1, V_W)
    pscale = pool_scale.reshape(1, V_W)
    pw = pool_w.reshape(len(POOL_WINDOWS), POOL_GC, POOL_GC)
    consts = mixer_constants(S)

    def halves5(g, lb):
        return g.reshape(lb.shape[0], N_CHIPS, 2, lb.shape[2], lb.shape[3])

    (x1, a1, b1), landed = ffn_fwd(xf, g1, w3_1, tm_fwd, "ffn1_fwd", plan=gather_plan(loc_bf[1:]))
    gathered = [place_own(g, lb, "place_own_" + gn) for g, lb, (gn, _) in zip(landed, loc_bf[1:], GROUPS[1:])]
    win_full = gathered[0].reshape(N_CHIPS, D, IN_W // N_CHIPS)
    wout_full = gathered[1].reshape(D, D)
    w3_2 = gathered[2].reshape(3, N_CHIPS, FS, D)
    p = in_proj(x1, gm, win_full, tm, "in_proj")
    x2, m_s, o_s, st_s, pooled_s = mixer_fwd(p, x1, consts, gn_gain, pw, pscale, wout_full, n_seq, S, "mixer_fwd")
    x3, a2, b2 = ffn_fwd(x2, g3, w3_2, tm_fwd, "ffn2_fwd")

    dx3, lacc, dgf = loss_bwd(x3, tgt, gf, tm, "loss_bwd")
    loss = lax.psum(jnp.sum(lacc) * (0.5 / D), ("x", "y", "c"))
    dx2, da2, db2, h2, n3, df2, dg3 = ffn_bwd(x2, dx3, g3, a2, b2, w3_2, tm, "ffn2_bwd")
    (gr_ffn2,) = ffn_wgrad(da2, db2, h2, n3, df2, tk, "ffn2_wgrad")
    g_ffn2 = halves5(gr_ffn2, loc_bf[3])
    (dx1, dp, n2, dgm, dgn, dps, dpw), (sb_ffn2,) = mixer_bwd(
        dx2, x1, p, o_s, st_s, pooled_s, consts, gm, gn_gain, pw, pscale, wout_full, win_full, n_seq, S, "mixer_bwd",
        plan=send_to_sibling_other_half([g_ffn2]))
    cs_ffn2 = add_own_half(g_ffn2, sb_ffn2, "add_sibling_ffn2")
    (dx0, da1, db1, h1, n1, df1, dg1), (pc_ffn2,) = ffn_bwd(
        xf, dx1, g1, a1, b1, w3_1, tm, "ffn1_bwd", plan=exchange_between_chips([cs_ffn2]))
    mh_ffn2 = sum_chips(cs_ffn2, pc_ffn2, "sum_chips_ffn2")
    (gr_ffn1,), (sh_ffn2,) = ffn_wgrad(da1, db1, h1, n1, df1, tk, "ffn1_wgrad", plan=share_with_sibling([mh_ffn2]))
    g_ffn1 = halves5(gr_ffn1, loc_bf[0])
    (gr_wout,), (sb_ffn1,) = tn_shared(m_s, dx2[None], tk, "dw_out", plan=send_to_sibling_other_half([g_ffn1]))
    cs_ffn1 = add_own_half(g_ffn1, sb_ffn1, "add_sibling_ffn1")
    (gr_win,), (pc_ffn1,) = tn_shared(n2, dp, tk, "dw_in", plan=exchange_between_chips([cs_ffn1]))
    mh_ffn1 = sum_chips(cs_ffn1, pc_ffn1, "sum_chips_ffn1")
    g_mix = [halves5(gr_win, loc_bf[1]), halves5(gr_wout, loc_bf[2])]
    sb_mix = _run_plan(send_to_sibling_other_half(g_mix), "mixer_grads_to_sibling")
    cs_mix = [add_own_half(g, r, "add_sibling_" + gn) for g, r, gn in zip(g_mix, sb_mix, ("w_in", "w_out"))]
    pc_mix = _run_plan(exchange_between_chips(cs_mix), "mixer_grads_between_chips")
    mh_mix = [sum_chips(cs, pc, "sum_chips_" + gn) for cs, pc, gn in zip(cs_mix, pc_mix, ("w_in", "w_out"))]
    sh_ffn1, sh_win, sh_wout = _run_plan(share_with_sibling([mh_ffn1] + mh_mix), "grads_back_to_sibling")
    my_half = [mh_ffn1, mh_mix[0], mh_mix[1], mh_ffn2]
    sib_half = [sh_ffn1, sh_win, sh_wout, sh_ffn2]

    small_part = dict(norm_ffn1=dg1, norm_mix=dgm, norm_ffn2=dg3, norm_final=dgf, ret_gn_gain=dgn, pool_scale=dps, pool_w=dpw)
    small_sum = all_reduce_small(_pack_small(small_part))

    out_g, out_d, out_m, out_v = {}, {}, {}, {}
    for (_, members), own, sib in zip(GROUPS, my_half, sib_half):
        for l, k in enumerate(members):
            d_, m_, v_, g_ = adamw_shard(local(W, k), local(M, k), local(V, k), own, sib, l, "adamw_" + k)
            out_g[k], out_d[k], out_m[k], out_v[k] = to_out(k, g_), to_out(k, d_), to_out(k, m_), to_out(k, v_)
    shapes = {k: W[k].shape for k in SMALL}
    d_, m_, v_ = adamw(_pack_small(W), small_sum, _pack_small(M), _pack_small(V), "adamw_small")
    for dst, packed in ((out_g, small_sum), (out_d, d_), (out_m, m_), (out_v, v_)):
        dst.update(_unpack_small(packed, shapes))

    grad_x = dx0.reshape(n_seq, S, D)
    return (loss, grad_x, *[out_g[k] for k in ORDER], *[out_d[k] for k in ORDER],
            *[out_m[k] for k in ORDER], *[out_v[k] for k in ORDER])
```

```python
import functools
import math
from typing import Callable, NamedTuple, Optional

import jax
import jax.numpy as jnp
from jax import lax
from jax.experimental import pallas as pl
from jax.experimental.pallas import tpu as pltpu

F32 = jnp.float32
BF16 = jnp.bfloat16
MESH = pl.DeviceIdType.MESH

N_CHIPS = 4
N_DEV = 8
CHUNK = 64
RET_HEADS = 4
RET_DK = 64
RET_DV = 128
QK_W = RET_HEADS * RET_DK
V_W = RET_HEADS * RET_DV
POOL_WINDOWS = (2, 4, 8, 16)
POOL_GC = 128
ROPE_BASE = 10000.0
RMS_EPS = 1e-6
GN_EPS = 1e-5
ADAM_LR = 0.001
ADAM_B1 = 0.9
ADAM_B2 = 0.999
ADAM_EPS = 1e-08
ADAM_WD = 0.01
ADAM_STEP = 10
MXU_W = 256
RET_BLOCK = MXU_W
VMEM_LIMIT = 56 * 1024 * 1024

NN = (((1,), (0,)), ((), ()))
NT = (((1,), (1,)), ((), ()))
TN = (((0,), (0,)), ((), ()))


def _dot(a, b, dims=NN):
    return lax.dot_general(a, b, dims, preferred_element_type=F32)


def _sigmoid(x):
    return 0.5 * jnp.tanh(0.5 * x) + 0.5


def _params(n_grid):
    return pltpu.CompilerParams(dimension_semantics=("arbitrary",) * n_grid, vmem_limit_bytes=VMEM_LIMIT)


class CommPlan(NamedTuple):
    ins: tuple
    out_shape: tuple
    sems: tuple
    start: Callable
    finish: Callable
    mid: Optional[Callable] = None
    mid_at: float = 0.85


ANY = pl.BlockSpec(memory_space=pl.ANY)


def _call(body, plan, *, name, grid, in_specs, out_specs, out_shape, args, scratch_shapes=()):
    n_grid = len(grid)
    if plan is None:
        return pl.pallas_call(body, name=name, grid=grid, in_specs=in_specs, out_specs=out_specs, out_shape=out_shape,
                              scratch_shapes=scratch_shapes, compiler_params=_params(n_grid))(*args)
    n_in, n_out, n_sc = len(in_specs), len(out_specs), len(scratch_shapes)
    p_in, p_out = len(plan.ins), len(plan.out_shape)
    total = math.prod(grid)
    mid_step = min(total - 1, int(plan.mid_at * total))

    def riding(*refs):
        ins, pins = refs[:n_in], refs[n_in:n_in + p_in]
        o0 = n_in + p_in
        outs, pouts = refs[o0:o0 + n_out], refs[o0 + n_out:o0 + n_out + p_out]
        s0 = o0 + n_out + p_out
        scratch, psems = refs[s0:s0 + n_sc], refs[s0 + n_sc:]
        step = pl.program_id(0)
        for d in range(1, n_grid):
            step = step * grid[d] + pl.program_id(d)

        @pl.when(step == 0)
        def _():
            plan.start(pins, pouts, psems)

        body(*ins, *outs, *scratch)

        if plan.mid is not None:
            @pl.when(step == mid_step)
            def _():
                plan.mid(pins, pouts, psems)

        @pl.when(step == total - 1)
        def _():
            plan.finish(pins, pouts, psems)

    res = pl.pallas_call(
        riding, name=name, grid=grid, in_specs=list(in_specs) + [ANY] * p_in, out_specs=list(out_specs) + [ANY] * p_out,
        out_shape=list(out_shape) + list(plan.out_shape), scratch_shapes=list(scratch_shapes) + list(plan.sems),
        compiler_params=_params(n_grid))(*args, *plan.ins)
    return res[:n_out], res[n_out:]


def _run_plan(plan, name):
    p_in, p_out = len(plan.ins), len(plan.out_shape)

    def body(*refs):
        pins, pouts, psems = refs[:p_in], refs[p_in:p_in + p_out], refs[p_in + p_out:]
        plan.start(pins, pouts, psems)
        if plan.mid is not None:
            plan.mid(pins, pouts, psems)
        plan.finish(pins, pouts, psems)

    return pl.pallas_call(body, name=name, in_specs=[ANY] * p_in, out_specs=[ANY] * p_out,
                          out_shape=list(plan.out_shape), scratch_shapes=list(plan.sems))(*plan.ins)


def _rms_parts(x):
    r = lax.rsqrt(jnp.mean(x * x, axis=-1, keepdims=True) + RMS_EPS)
    return r, x * r


def _rms_bwd(dn, xhat, r, gain, dres):
    dxh = dn * gain
    return dres + r * (dxh - xhat * jnp.mean(dxh * xhat, axis=-1, keepdims=True))


def _w3_specs(FS, D):
    return [pl.BlockSpec((None, None, FS, D), functools.partial(lambda i, j, k: (k, j, 0, 0), k=k)) for k in range(3)]


def ffn_fwd(x, gain, w3, tm, name, plan=None):
    T, D = x.shape
    _, J, FS, _ = w3.shape

    def body(x_ref, g_ref, wg_ref, wu_ref, wd_ref, xo_ref, a_ref, b_ref, n_ref, acc):
        j = pl.program_id(1)

        @pl.when(j == 0)
        def _():
            _, xhat = _rms_parts(x_ref[...])
            n_ref[...] = (xhat * g_ref[...]).astype(BF16)
            acc[...] = jnp.zeros_like(acc)

        n = n_ref[...]
        a = _dot(n, wg_ref[...], NT)
        b = _dot(n, wu_ref[...], NT)
        a_ref[...] = a.astype(BF16)
        b_ref[...] = b.astype(BF16)
        h = (a * _sigmoid(a) * b).astype(BF16)
        acc[...] += _dot(h, wd_ref[...])

        @pl.when(j == J - 1)
        def _():
            xo_ref[...] = x_ref[...] + 0.5 * acc[...]

    return _call(
        body, plan, name=name, grid=(T // tm, J),
        in_specs=[pl.BlockSpec((tm, D), lambda i, j: (i, 0)),
                  pl.BlockSpec((1, D), lambda i, j: (0, 0))] + _w3_specs(FS, D),
        out_specs=[pl.BlockSpec((tm, D), lambda i, j: (i, 0)),
                   pl.BlockSpec((None, tm, FS), lambda i, j: (j, i, 0)),
                   pl.BlockSpec((None, tm, FS), lambda i, j: (j, i, 0)),
                   pl.BlockSpec((tm, D), lambda i, j: (i, 0))],
        out_shape=[jax.ShapeDtypeStruct((T, D), F32),
                   jax.ShapeDtypeStruct((J, T, FS), BF16),
                   jax.ShapeDtypeStruct((J, T, FS), BF16),
                   jax.ShapeDtypeStruct((T, D), BF16)],
        scratch_shapes=[pltpu.VMEM((tm, D), F32)],
        args=(x, gain, w3, w3, w3))


def ffn_bwd(df, a_s, b_s, w3, tm, name, plan=None):
    T, D = df.shape
    _, J, FS, _ = w3.shape

    chunks = [slice(s, min(s + MXU_W, FS)) for s in range(0, FS, MXU_W)]
    nc = len(chunks)

    def body(df_ref, a_ref, b_ref, wg_ref, wu_ref, wd_ref, dn_ref, da_ref, db_ref, h_ref):
        @pl.when(pl.program_id(1) == 0)
        def _():
            dn_ref[...] = jnp.zeros_like(dn_ref)

        def elementwise(dh, sl):
            a = a_ref[:, sl].astype(F32)
            b = b_ref[:, sl].astype(F32)
            sg = _sigmoid(a)
            si = a * sg
            da = (dh * b * (sg + si * (1.0 - sg))).astype(BF16)
            db = (dh * si).astype(BF16)
            da_ref[:, sl] = da
            db_ref[:, sl] = db
            h_ref[:, sl] = (si * b).astype(BF16)
            return da, db

        df = df_ref[...]
        dh, dab, acc = [None] * nc, [None] * nc, None
        for step in range(nc + 2):
            if step < nc:
                dh[step] = _dot(df, wd_ref[chunks[step], :], NT)
            if 1 <= step <= nc:
                dab[step - 1] = elementwise(dh[step - 1], chunks[step - 1])
            if step >= 2:
                sl = chunks[step - 2]
                part = _dot(dab[step - 2][0], wg_ref[sl, :]) + _dot(dab[step - 2][1], wu_ref[sl, :])
                acc = part if acc is None else acc + part
        dn_ref[...] += acc

    tok = pl.BlockSpec((tm, D), lambda i, j: (i, 0))
    sh = pl.BlockSpec((None, tm, FS), lambda i, j: (j, i, 0))
    return _call(
        body, plan, name=name, grid=(T // tm, J),
        in_specs=[tok, sh, sh] + _w3_specs(FS, D),
        out_specs=[tok, sh, sh, sh],
        out_shape=[jax.ShapeDtypeStruct((T, D), F32),
                   jax.ShapeDtypeStruct((J, T, FS), BF16),
                   jax.ShapeDtypeStruct((J, T, FS), BF16),
                   jax.ShapeDtypeStruct((J, T, FS), BF16)],
        args=(df, a_s, b_s, w3, w3, w3))


def rms_bwd(x, dout, dn, gain, tm, name):
    T, D = x.shape

    def body(x_ref, do_ref, dn_ref, g_ref, dx_ref, dgain_ref):
        @pl.when(pl.program_id(0) == 0)
        def _():
            dgain_ref[...] = jnp.zeros_like(dgain_ref)

        r, xhat = _rms_parts(x_ref[...])
        dn = dn_ref[...]
        dgain_ref[...] += jnp.sum(dn * xhat, axis=0, keepdims=True)
        dx_ref[...] = _rms_bwd(dn, xhat, r, g_ref[...], do_ref[...])

    tok = pl.BlockSpec((tm, D), lambda i: (i, 0))
    vec = pl.BlockSpec((1, D), lambda i: (0, 0))
    return pl.pallas_call(
        body, name=name, grid=(T // tm,), in_specs=[tok, tok, tok, vec], out_specs=[tok, vec],
        out_shape=[jax.ShapeDtypeStruct((T, D), F32), jax.ShapeDtypeStruct((1, D), F32)],
        compiler_params=_params(1),
    )(x, dout, dn, gain)


def ffn_wgrad(da, db, h, n, df, tk, name, plan=None):
    J, T, FS = da.shape
    D = n.shape[1]

    def body(da_ref, db_ref, h_ref, n_ref, df_ref, o_ref):
        @pl.when(pl.program_id(1) == 0)
        def _():
            o_ref[...] = jnp.zeros_like(o_ref)

        nv = n_ref[...]
        o_ref[0] += _dot(da_ref[...], nv, TN)
        o_ref[1] += _dot(db_ref[...], nv, TN)
        o_ref[2] += _dot(h_ref[...], df_ref[...], TN)

    sh = pl.BlockSpec((None, tk, FS), lambda j, k: (j, k, 0))
    tok = pl.BlockSpec((tk, D), lambda j, k: (k, 0))
    return _call(
        body, plan, name=name, grid=(J, T // tk),
        in_specs=[sh, sh, sh, tok, tok],
        out_specs=[pl.BlockSpec((3, None, FS, D), lambda j, k: (0, j, 0, 0))],
        out_shape=[jax.ShapeDtypeStruct((3, J, FS, D), F32)],
        args=(da, db, h, n, df))


def tn_shared(a, b, tk, name, plan=None):
    T, M = a.shape
    J, _, N = b.shape

    def body(a_ref, b_ref, o_ref):
        @pl.when(pl.program_id(0) == 0)
        def _():
            o_ref[...] = jnp.zeros_like(o_ref)

        a_t = a_ref[...].astype(BF16).T
        for j in range(J):
            o_ref[j] += _dot(a_t, b_ref[j].astype(BF16))

    return _call(
        body, plan, name=name, grid=(T // tk,),
        in_specs=[pl.BlockSpec((tk, M), lambda k: (k, 0)), pl.BlockSpec((J, tk, N), lambda k: (0, k, 0))],
        out_specs=[pl.BlockSpec((J, M, N), lambda k: (0, 0, 0))],
        out_shape=[jax.ShapeDtypeStruct((J, M, N), F32)],
        args=(a, b))


def in_proj(x, gain, win, tm, name):
    T, D = x.shape
    J, _, W = win.shape

    def body(x_ref, g_ref, w_ref, p_ref):
        _, xhat = _rms_parts(x_ref[...])
        n = (xhat * g_ref[...]).astype(BF16)
        for j in range(J):
            p_ref[j] = _dot(n, w_ref[j])

    return pl.pallas_call(
        body, name=name, grid=(T // tm,),
        in_specs=[pl.BlockSpec((tm, D), lambda i: (i, 0)),
                  pl.BlockSpec((1, D), lambda i: (0, 0)),
                  pl.BlockSpec((J, D, W), lambda i: (0, 0, 0))],
        out_specs=pl.BlockSpec((J, tm, W), lambda i: (0, i, 0)),
        out_shape=jax.ShapeDtypeStruct((J, T, W), F32),
        compiler_params=_params(1),
    )(x, gain, win)


def mixer_constants(S):
    B = RET_BLOCK
    half = RET_DK // 2
    freqs = ROPE_BASE ** (-jnp.arange(half, dtype=F32) * 2.0 / RET_DK)
    ang = jnp.arange(S, dtype=F32)[:, None] * freqs[None, :]
    cos = jnp.tile(jnp.cos(ang), (1, 2 * RET_HEADS))
    sin = jnp.tile(jnp.concatenate([-jnp.sin(ang), jnp.sin(ang)], axis=1), (1, RET_HEADS))
    gamma = 1.0 - 2.0 ** (-5.0 - jnp.arange(RET_HEADS, dtype=F32))
    log_g = jnp.log(gamma)
    idx = jnp.arange(B, dtype=F32)
    ci = jnp.arange(B) // CHUNK
    dist = jnp.abs(idx[:, None] - idx[None, :])
    m_intra = jnp.exp(log_g[:, None, None] * dist[None]) * (ci[None, :] <= ci[:, None])[None].astype(F32)
    lg_lane = jnp.repeat(log_g, RET_DK)
    d_q = jnp.exp(lg_lane[None, :] * (idx[:, None] + 1.0))
    d_k = jnp.exp(lg_lane[None, :] * (B - 1.0 - idx[:, None]))
    d_s = jnp.broadcast_to(jnp.exp(lg_lane * B)[:, None], (QK_W, V_W))
    bm = (jnp.arange(QK_W)[:, None] // RET_DK == jnp.arange(V_W)[None, :] // RET_DV).astype(F32)
    win = jnp.repeat(jnp.array(POOL_WINDOWS, F32), POOL_GC)
    invc = 1.0 / jnp.minimum(jnp.arange(S, dtype=F32)[:, None] + 1.0, win[None, :])
    return dict(cos=cos, sin=sin, m=m_intra, dq=d_q, dk=d_k, ds=d_s, bm=bm, invc=invc)


def _swap_halves(x):
    w = x.shape[1]
    lane = lax.broadcasted_iota(jnp.int32, x.shape, 1)
    return jnp.where((lane % RET_DK) < RET_DK // 2, pltpu.roll(x, w - RET_DK // 2, 1), pltpu.roll(x, RET_DK // 2, 1))


def _rot(x, cos, sin):
    return x * cos + _swap_halves(x) * sin


def _rot_t(d, cos, sin):
    return d * cos + _swap_halves(d * sin)


def _lane_groups(parts):
    return jnp.concatenate([p[:, POOL_GC * g:POOL_GC * (g + 1)] for g, p in enumerate(parts)], axis=1)


def _head_mask(shape, h):
    lane = lax.broadcasted_iota(jnp.int32, shape, 1)
    return (lane // RET_DK) == h


def _group_norm(o):
    yh, rs = [], []
    for h in range(RET_HEADS):
        oh = o[:, RET_DV * h:RET_DV * (h + 1)]
        xc = oh - jnp.mean(oh, axis=-1, keepdims=True)
        r = lax.rsqrt(jnp.mean(xc * xc, axis=-1, keepdims=True) + GN_EPS)
        yh.append(xc * r)
        rs.append(r)
    return yh, rs


def mixer_fwd(p, x1, consts, gn_gain, pool_w, pool_scale, wout, n_seq, S, name):
    _, T, _ = p.shape
    D = x1.shape[1]
    B = RET_BLOCK
    nb = S // B

    def body(p_ref, x1_ref, cos_ref, sin_ref, invc_ref, m_ref, dq_ref, dk_ref, ds_ref, bm_ref,
             gain_ref, pw_ref, sc_ref, wout_ref,
             x2_ref, m_out, o_out, st_out, pooled_out, state, prev_u):
        blk = pl.program_id(1)

        @pl.when(blk == 0)
        def _():
            state[...] = jnp.zeros_like(state)
            prev_u[...] = jnp.zeros_like(prev_u)

        qk = p_ref[0]
        v = p_ref[1]
        g = p_ref[2]
        u = p_ref[3]
        cos = cos_ref[...]
        sin = sin_ref[...]
        qr = _rot(qk[:, :QK_W], cos, sin) * (RET_DK ** -0.5)
        kr = _rot(qk[:, QK_W:], cos, sin)
        qb = qr.astype(BF16)
        kb = kr.astype(BF16)
        vb = v.astype(BF16)
        st = state[...]
        st_out[...] = st
        cross = _dot((qr * dq_ref[...]).astype(BF16), st.astype(BF16))
        o_parts = []
        for h in range(RET_HEADS):
            qm = jnp.where(_head_mask(qb.shape, h), qb, jnp.zeros_like(qb))
            sc = (_dot(qm, kb, NT) * m_ref[h]).astype(BF16)
            o_parts.append(_dot(sc, vb[:, RET_DV * h:RET_DV * (h + 1)]) + cross[:, RET_DV * h:RET_DV * (h + 1)])
        o = jnp.concatenate(o_parts, axis=1)
        o_out[...] = o
        kv = _dot((kr * dk_ref[...]).astype(BF16), vb, TN)
        state[...] = st * ds_ref[...] + kv * bm_ref[...]

        yh, _ = _group_norm(o)
        r = g * _sigmoid(g) * (jnp.concatenate(yh, axis=1) * gain_ref[...])

        ext = jnp.concatenate([prev_u[...], u], axis=0)
        sums = []
        run = ext
        for k in (1, 2, 4, 8):
            run = run + pltpu.roll(run, k, 0)
            sums.append(run[B:, :])
        prev_u[...] = u
        pooled = (_lane_groups(sums) * invc_ref[...] - u).astype(BF16)
        pooled_out[...] = pooled
        yp = [_dot(pooled[:, POOL_GC * gi:POOL_GC * (gi + 1)], pw_ref[gi].astype(BF16)) for gi in range(len(POOL_WINDOWS))]
        s = jnp.concatenate(yp, axis=1) * sc_ref[...]
        m = jnp.concatenate([r, s], axis=1).astype(BF16)
        m_out[...] = m
        x2_ref[...] = x1_ref[...] + _dot(m, wout_ref[...])

    tokmap = lambda b, k: (b * nb + k, 0)
    seqmap = lambda b, k: (k, 0)
    const2 = lambda b, k: (0, 0)
    const3 = lambda b, k: (0, 0, 0)
    return pl.pallas_call(
        body, name=name, grid=(n_seq, nb),
        in_specs=[pl.BlockSpec((4, B, V_W), lambda b, k: (0, b * nb + k, 0)),
                  pl.BlockSpec((B, D), tokmap),
                  pl.BlockSpec((B, QK_W), seqmap), pl.BlockSpec((B, QK_W), seqmap), pl.BlockSpec((B, V_W), seqmap),
                  pl.BlockSpec((RET_HEADS, B, B), const3),
                  pl.BlockSpec((B, QK_W), const2), pl.BlockSpec((B, QK_W), const2),
                  pl.BlockSpec((QK_W, V_W), const2), pl.BlockSpec((QK_W, V_W), const2),
                  pl.BlockSpec((1, V_W), const2), pl.BlockSpec((4, POOL_GC, POOL_GC), const3),
                  pl.BlockSpec((1, V_W), const2), pl.BlockSpec((D, D), const2)],
        out_specs=[pl.BlockSpec((B, D), tokmap), pl.BlockSpec((B, D), tokmap), pl.BlockSpec((B, V_W), tokmap),
                   pl.BlockSpec((None, QK_W, V_W), lambda b, k: (b * nb + k, 0, 0)),
                   pl.BlockSpec((B, V_W), tokmap)],
        out_shape=[jax.ShapeDtypeStruct((T, D), F32), jax.ShapeDtypeStruct((T, D), BF16),
                   jax.ShapeDtypeStruct((T, V_W), F32), jax.ShapeDtypeStruct((T // B, QK_W, V_W), F32),
                   jax.ShapeDtypeStruct((T, V_W), BF16)],
        scratch_shapes=[pltpu.VMEM((QK_W, V_W), F32), pltpu.VMEM((B, V_W), F32)],
        compiler_params=_params(2),
    )(p, x1, consts["cos"], consts["sin"], consts["invc"], consts["m"], consts["dq"], consts["dk"],
      consts["ds"], consts["bm"], gn_gain, pool_w, pool_scale, wout)


def mixer_bwd(dx2, x1, p, o_s, st_s, pooled_s, consts, gmix, gn_gain, pool_w, pool_scale, wout, win, n_seq, S, name,
              plan=None):
    T, D = dx2.shape
    B = RET_BLOCK
    nb = S // B
    NG = len(POOL_WINDOWS)

    def body(dx2_ref, x1_ref, p_ref, o_ref, st_ref, pooled_ref, cos_ref, sin_ref, invc_ref, m_ref, dq_ref, dk_ref,
             ds_ref, bm_ref, gmix_ref, gain_ref, pw_ref, sc_ref, wout_ref, win_ref,
             dx1_ref, df1_ref, dp_ref, n2_ref, dgmix_ref, dgain_ref, dscale_ref, dpw_ref, gstate, next_e):
        b = pl.program_id(0)
        blk = pl.program_id(1)

        @pl.when(jnp.logical_and(b == 0, blk == 0))
        def _():
            dgmix_ref[...] = jnp.zeros_like(dgmix_ref)
            dgain_ref[...] = jnp.zeros_like(dgain_ref)
            dscale_ref[...] = jnp.zeros_like(dscale_ref)
            dpw_ref[...] = jnp.zeros_like(dpw_ref)

        @pl.when(blk == 0)
        def _():
            gstate[...] = jnp.zeros_like(gstate)
            next_e[...] = jnp.zeros_like(next_e)

        dx2v = dx2_ref[...]
        dm = _dot(dx2v.astype(BF16), wout_ref[...], NT)
        dr = dm[:, :V_W]
        dsv = dm[:, V_W:]

        pooled = pooled_ref[...]
        scale = sc_ref[...]
        dyp = (dsv * scale).astype(BF16)
        yp, dpl = [], []
        for gi in range(NG):
            sl = slice(POOL_GC * gi, POOL_GC * (gi + 1))
            pwb = pw_ref[gi].astype(BF16)
            yp.append(_dot(pooled[:, sl], pwb))
            dpw_ref[gi] += _dot(pooled[:, sl], dyp[:, sl], TN)
            dpl.append(_dot(dyp[:, sl], pwb, NT))
        dscale_ref[...] += jnp.sum(dsv * jnp.concatenate(yp, axis=1), axis=0, keepdims=True)
        dpooled = jnp.concatenate(dpl, axis=1)
        e = dpooled * invc_ref[...]
        ext = jnp.concatenate([e, next_e[...]], axis=0)
        sums = []
        run = ext
        for k in (1, 2, 4, 8):
            run = run + pltpu.roll(run, 2 * B - k, 0)
            sums.append(run[:B, :])
        next_e[...] = e
        du = _lane_groups(sums) - dpooled

        qk = p_ref[0]
        v = p_ref[1]
        g = p_ref[2]
        sg = _sigmoid(g)
        si = g * sg
        yh, rs = _group_norm(o_ref[...])
        yhat = jnp.concatenate(yh, axis=1)
        gain = gain_ref[...]
        dg = dr * (yhat * gain) * (sg * (1.0 + g * (1.0 - sg)))
        dy = dr * si
        dgain_ref[...] += jnp.sum(dy * yhat, axis=0, keepdims=True)
        dyh = dy * gain
        do_parts = []
        for h in range(RET_HEADS):
            sl = slice(RET_DV * h, RET_DV * (h + 1))
            dh_ = dyh[:, sl]
            m1 = jnp.mean(dh_, axis=-1, keepdims=True)
            m2 = jnp.mean(dh_ * yh[h], axis=-1, keepdims=True)
            do_parts.append(rs[h] * (dh_ - m1 - yh[h] * m2))
        dob = jnp.concatenate(do_parts, axis=1).astype(BF16)

        cos = cos_ref[...]
        sin = sin_ref[...]
        qr = _rot(qk[:, :QK_W], cos, sin) * (RET_DK ** -0.5)
        kr = _rot(qk[:, QK_W:], cos, sin)
        qb = qr.astype(BF16)
        kb = kr.astype(BF16)
        vb = v.astype(BF16)
        dqd = dq_ref[...]
        dkd = dk_ref[...]
        stb = st_ref[...].astype(BF16)
        gs = gstate[...]
        gb = gs.astype(BF16)
        dqs = _dot(dob, stb, NT) * dqd
        dkr = _dot(vb, gb, NT) * dkd
        dv_cross = _dot((kr * dkd).astype(BF16), gb)
        ds_cross = _dot((qr * dqd).astype(BF16), dob, TN) * bm_ref[...]
        gstate[...] = ds_cross + gs * ds_ref[...]
        dv_parts = []
        for h in range(RET_HEADS):
            sl = slice(RET_DV * h, RET_DV * (h + 1))
            hm = _head_mask(qb.shape, h)
            qm = jnp.where(hm, qb, jnp.zeros_like(qb))
            mh = m_ref[h]
            sc = (_dot(qm, kb, NT) * mh).astype(BF16)
            dsc = (_dot(dob[:, sl], vb[:, sl], NT) * mh).astype(BF16)
            dqs = dqs + jnp.where(hm, _dot(dsc, kb), 0.0)
            dkr = dkr + jnp.where(hm, _dot(dsc, qb, TN), 0.0)
            dv_parts.append(_dot(sc, dob[:, sl], TN) + dv_cross[:, sl])
        dq = _rot_t(dqs * (RET_DK ** -0.5), cos, sin)
        dk = _rot_t(dkr, cos, sin)
        dp = [jnp.concatenate([dq, dk], axis=1).astype(BF16), jnp.concatenate(dv_parts, axis=1).astype(BF16),
              dg.astype(BF16), du.astype(BF16)]
        dn = jnp.zeros((B, D), F32)
        for jj in range(4):
            dp_ref[jj] = dp[jj]
            dn = dn + _dot(dp[jj], win_ref[jj], NT)

        x1v = x1_ref[...]
        r, xhat = _rms_parts(x1v)
        gm = gmix_ref[...]
        n2_ref[...] = (xhat * gm).astype(BF16)
        dgmix_ref[...] += jnp.sum(dn * xhat, axis=0, keepdims=True)
        dx1 = _rms_bwd(dn, xhat, r, gm, dx2v)
        dx1_ref[...] = dx1
        df1_ref[...] = (0.5 * dx1).astype(BF16)

    rev = lambda b, k: (b * nb + (nb - 1 - k), 0)
    seqrev = lambda b, k: (nb - 1 - k, 0)
    const2 = lambda b, k: (0, 0)
    const3 = lambda b, k: (0, 0, 0)
    return _call(
        body, plan, name=name, grid=(n_seq, nb),
        in_specs=[pl.BlockSpec((B, D), rev), pl.BlockSpec((B, D), rev),
                  pl.BlockSpec((4, B, V_W), lambda b, k: (0, b * nb + (nb - 1 - k), 0)),
                  pl.BlockSpec((B, V_W), rev),
                  pl.BlockSpec((None, QK_W, V_W), lambda b, k: (b * nb + (nb - 1 - k), 0, 0)),
                  pl.BlockSpec((B, V_W), rev),
                  pl.BlockSpec((B, QK_W), seqrev), pl.BlockSpec((B, QK_W), seqrev), pl.BlockSpec((B, V_W), seqrev),
                  pl.BlockSpec((RET_HEADS, B, B), const3),
                  pl.BlockSpec((B, QK_W), const2), pl.BlockSpec((B, QK_W), const2),
                  pl.BlockSpec((QK_W, V_W), const2), pl.BlockSpec((QK_W, V_W), const2),
                  pl.BlockSpec((1, D), const2), pl.BlockSpec((1, V_W), const2),
                  pl.BlockSpec((NG, POOL_GC, POOL_GC), const3), pl.BlockSpec((1, V_W), const2),
                  pl.BlockSpec((D, D), const2), pl.BlockSpec((4, D, V_W), const3)],
        out_specs=[pl.BlockSpec((B, D), rev), pl.BlockSpec((B, D), rev),
                   pl.BlockSpec((4, B, V_W), lambda b, k: (0, b * nb + (nb - 1 - k), 0)),
                   pl.BlockSpec((B, D), rev),
                   pl.BlockSpec((1, D), const2), pl.BlockSpec((1, V_W), const2), pl.BlockSpec((1, V_W), const2),
                   pl.BlockSpec((NG, POOL_GC, POOL_GC), const3)],
        out_shape=[jax.ShapeDtypeStruct((T, D), F32), jax.ShapeDtypeStruct((T, D), BF16),
                   jax.ShapeDtypeStruct((4, T, V_W), BF16),
                   jax.ShapeDtypeStruct((T, D), BF16),
                   jax.ShapeDtypeStruct((1, D), F32), jax.ShapeDtypeStruct((1, V_W), F32),
                   jax.ShapeDtypeStruct((1, V_W), F32), jax.ShapeDtypeStruct((NG, POOL_GC, POOL_GC), F32)],
        scratch_shapes=[pltpu.VMEM((QK_W, V_W), F32), pltpu.VMEM((B, V_W), F32)],
        args=(dx2, x1, p, o_s, st_s, pooled_s, consts["cos"], consts["sin"], consts["invc"], consts["m"], consts["dq"],
              consts["dk"], consts["ds"], consts["bm"], gmix, gn_gain, pool_w, pool_scale, wout, win))


def loss_bwd(x3, tgt, gain, tm, name):
    T, D = x3.shape

    def body(x_ref, t_ref, g_ref, dx_ref, df_ref, lacc_ref, dg_ref):
        @pl.when(pl.program_id(0) == 0)
        def _():
            lacc_ref[...] = jnp.zeros_like(lacc_ref)
            dg_ref[...] = jnp.zeros_like(dg_ref)

        r, xhat = _rms_parts(x_ref[...])
        gv = g_ref[...]
        err = xhat * gv - t_ref[...]
        lacc_ref[...] += jnp.sum(err * err, axis=0, keepdims=True)
        dy = err * (1.0 / D)
        dg_ref[...] += jnp.sum(dy * xhat, axis=0, keepdims=True)
        dx = _rms_bwd(dy, xhat, r, gv, 0.0)
        dx_ref[...] = dx
        df_ref[...] = (0.5 * dx).astype(BF16)

    tok = pl.BlockSpec((tm, D), lambda i: (i, 0))
    vec = pl.BlockSpec((1, D), lambda i: (0, 0))
    return pl.pallas_call(
        body, name=name, grid=(T // tm,),
        in_specs=[tok, tok, vec], out_specs=[tok, tok, vec, vec],
        out_shape=[jax.ShapeDtypeStruct((T, D), F32), jax.ShapeDtypeStruct((T, D), BF16),
                   jax.ShapeDtypeStruct((1, D), F32), jax.ShapeDtypeStruct((1, D), F32)],
        compiler_params=_params(1),
    )(x3, tgt, gain)


def _adamw_math(w, g, m, v):
    m2 = ADAM_B1 * m + (1.0 - ADAM_B1) * g
    v2 = ADAM_B2 * v + (1.0 - ADAM_B2) * (g * g)
    m_hat = m2 / (1.0 - ADAM_B1 ** ADAM_STEP)
    v_hat = v2 / (1.0 - ADAM_B2 ** ADAM_STEP)
    return -ADAM_LR * (m_hat / (jnp.sqrt(v_hat) + ADAM_EPS) + ADAM_WD * w), m2, v2


def adamw(w, g, m, v, name):
    R, C = w.shape

    def body(w_ref, g_ref, m_ref, v_ref, d_ref, nm_ref, nv_ref):
        d_ref[...], nm_ref[...], nv_ref[...] = _adamw_math(w_ref[...], g_ref[...], m_ref[...], v_ref[...])

    blk = pl.BlockSpec((R, C), lambda i: (0, 0))
    sds = jax.ShapeDtypeStruct((R, C), F32)
    return pl.pallas_call(
        body, name=name, grid=(1,), in_specs=[blk] * 4, out_specs=[blk] * 3, out_shape=[sds] * 3,
        compiler_params=_params(1),
    )(w, g, m, v)


def _place():
    x, y, c = lax.axis_index("x"), lax.axis_index("y"), lax.axis_index("c")
    other_chips = [(1 - x, y), (x, 1 - y), (1 - x, 1 - y)]
    return x, y, c, other_chips


def _exchange_plan(ins, out_shape, copies, n_copies):
    def descriptors(pins, pouts, psems):
        send, recv = psems
        return [pltpu.make_async_remote_copy(src_ref=s, dst_ref=d, send_sem=send.at[i], recv_sem=recv.at[i],
                                             device_id=dev, device_id_type=MESH)
                for i, (s, d, dev) in enumerate(copies(pins, pouts))]

    def start(pins, pouts, psems):
        for cp in descriptors(pins, pouts, psems):
            cp.start()

    def finish(pins, pouts, psems):
        for cp in descriptors(pins, pouts, psems):
            cp.wait()

    return CommPlan(tuple(ins), tuple(out_shape),
                    (pltpu.SemaphoreType.DMA((n_copies,)), pltpu.SemaphoreType.DMA((n_copies,))), start, finish)


def gather_plan(shards):
    n = len(shards)

    def start(pins, pouts, psems):
        x, y, c, chips = _place()
        mine = 2 * x + y
        for w in range(n):
            for k, (px, py) in enumerate(chips):
                pltpu.make_async_remote_copy(
                    src_ref=pins[w].at[:, c], dst_ref=pouts[w].at[:, mine, c],
                    send_sem=psems[0].at[w, k], recv_sem=psems[1].at[w, k],
                    device_id=(px, py, c), device_id_type=MESH).start()

    def mid(pins, pouts, psems):
        x, y, c, chips = _place()
        for w in range(n):
            for k, (px, py) in enumerate(chips):
                landed = pouts[w].at[:, 2 * px + py, c]
                pltpu.make_async_remote_copy(
                    src_ref=landed, dst_ref=landed, send_sem=psems[0].at[w, k], recv_sem=psems[1].at[w, k],
                    device_id=(px, py, c), device_id_type=MESH).wait_recv()
                pltpu.make_async_remote_copy(
                    src_ref=landed, dst_ref=landed, send_sem=psems[2].at[w, k], recv_sem=psems[3].at[w, k],
                    device_id=(x, y, 1 - c), device_id_type=MESH).start()

    def finish(pins, pouts, psems):
        x, y, c, chips = _place()
        mine = 2 * x + y
        for w in range(n):
            for k, (px, py) in enumerate(chips):
                landed = pouts[w].at[:, 2 * px + py, 1 - c]
                cp = pltpu.make_async_remote_copy(
                    src_ref=landed, dst_ref=landed, send_sem=psems[2].at[w, k], recv_sem=psems[3].at[w, k],
                    device_id=(x, y, 1 - c), device_id_type=MESH)
                cp.wait_recv()
                cp.wait_send()
                pltpu.make_async_remote_copy(
                    src_ref=pins[w].at[:, c], dst_ref=pouts[w].at[:, mine, c],
                    send_sem=psems[0].at[w, k], recv_sem=psems[1].at[w, k],
                    device_id=(px, py, c), device_id_type=MESH).wait_send()

    return CommPlan(tuple(shards),
                    tuple(jax.ShapeDtypeStruct((s.shape[0], N_CHIPS) + s.shape[1:], s.dtype) for s in shards),
                    tuple(pltpu.SemaphoreType.DMA((n, 3)) for _ in range(4)), start, finish, mid)


def place_own(gathered, shard, name):
    L, _, Rh, C = shard.shape

    def body(chip_ref, g_ref, s_ref, o_ref):
        o_ref[...] = s_ref[...]

    return pl.pallas_call(
        body, name=name,
        grid_spec=pltpu.PrefetchScalarGridSpec(
            num_scalar_prefetch=1, grid=(L,),
            in_specs=[ANY, pl.BlockSpec((None, 2, Rh, C), lambda l, chip_ref: (l, 0, 0, 0))],
            out_specs=pl.BlockSpec((None, None, 2, Rh, C), lambda l, chip_ref: (l, chip_ref[0], 0, 0, 0))),
        out_shape=jax.ShapeDtypeStruct(gathered.shape, gathered.dtype),
        input_output_aliases={1: 0},
        compiler_params=_params(1),
    )(_chip_index(), gathered, shard)


def gather_weights(shards):
    n = len(shards)

    def body(*refs):
        ins, outs, bufs = refs[:n], refs[n:2 * n], refs[2 * n:3 * n]
        ld_sem, st_sem, ici_send, ici_recv, d2d_send, d2d_recv = refs[3 * n:]
        x, y, c, chips = _place()
        sibling = (x, y, 1 - c)
        mine = 2 * x + y
        loads = [pltpu.make_async_copy(ins[w], bufs[w], ld_sem.at[w]) for w in range(n)]
        for cp in loads:
            cp.start()
        stores, sends = [], []
        for w in range(n):
            loads[w].wait()
            cp = pltpu.make_async_copy(bufs[w], outs[w].at[:, mine], st_sem.at[w])
            cp.start()
            stores.append(cp)
            for k, (px, py) in enumerate(chips):
                cp = pltpu.make_async_remote_copy(
                    src_ref=bufs[w].at[:, c], dst_ref=outs[w].at[:, mine, c],
                    send_sem=ici_send.at[w, k], recv_sem=ici_recv.at[w, k],
                    device_id=(px, py, c), device_id_type=MESH)
                cp.start()
                sends.append(cp)
        for w in range(n):
            for k, (px, py) in enumerate(chips):
                landed = outs[w].at[:, 2 * px + py, c]
                pltpu.make_async_remote_copy(
                    src_ref=landed, dst_ref=landed, send_sem=ici_send.at[w, k], recv_sem=ici_recv.at[w, k],
                    device_id=(px, py, c), device_id_type=MESH).wait_recv()
                cp = pltpu.make_async_remote_copy(
                    src_ref=landed, dst_ref=landed, send_sem=d2d_send.at[w, k], recv_sem=d2d_recv.at[w, k],
                    device_id=sibling, device_id_type=MESH)
                cp.start()
                sends.append(cp)
        for w in range(n):
            for k, (px, py) in enumerate(chips):
                landed = outs[w].at[:, 2 * px + py, 1 - c]
                pltpu.make_async_remote_copy(
                    src_ref=landed, dst_ref=landed, send_sem=d2d_send.at[w, k], recv_sem=d2d_recv.at[w, k],
                    device_id=sibling, device_id_type=MESH).wait_recv()
        for cp in sends:
            cp.wait_send()
        for cp in stores:
            cp.wait()

    return pl.pallas_call(
        body, name="gather_weights",
        in_specs=[ANY] * n, out_specs=[ANY] * n,
        out_shape=[jax.ShapeDtypeStruct((s.shape[0], N_CHIPS) + s.shape[1:], s.dtype) for s in shards],
        scratch_shapes=[pltpu.VMEM(s.shape, s.dtype) for s in shards] +
                       [pltpu.SemaphoreType.DMA((n,)), pltpu.SemaphoreType.DMA((n,)),
                        pltpu.SemaphoreType.DMA((n, 3)), pltpu.SemaphoreType.DMA((n, 3)),
                        pltpu.SemaphoreType.DMA((n, 3)), pltpu.SemaphoreType.DMA((n, 3))],
        compiler_params=pltpu.CompilerParams(vmem_limit_bytes=VMEM_LIMIT),
    )(*shards)


def send_to_sibling_other_half(grads):
    def copies(ins, outs):
        x, y, c, _ = _place()
        return [(ins[w].at[:, :, 1 - c], outs[w], (x, y, 1 - c)) for w in range(len(grads))]

    return _exchange_plan(grads, [jax.ShapeDtypeStruct(g.shape[:2] + g.shape[3:], g.dtype) for g in grads], copies,
                          len(grads))


def _core_index():
    return jnp.reshape(lax.axis_index("c"), (1,)).astype(jnp.int32)


def _chip_index():
    return jnp.reshape(2 * lax.axis_index("x") + lax.axis_index("y"), (1,)).astype(jnp.int32)


def add_own_half(g, recv, name):
    L, J, _, Rh, C = g.shape

    def body(c_ref, g_ref, r_ref, o_ref):
        o_ref[...] = (g_ref[...] + r_ref[...]).astype(BF16)

    return pl.pallas_call(
        body, name=name,
        grid_spec=pltpu.PrefetchScalarGridSpec(
            num_scalar_prefetch=1, grid=(L, J),
            in_specs=[pl.BlockSpec((None, None, None, Rh, C), lambda l, j, c_ref: (l, j, c_ref[0], 0, 0)),
                      pl.BlockSpec((None, None, Rh, C), lambda l, j, c_ref: (l, j, 0, 0))],
            out_specs=pl.BlockSpec((None, None, Rh, C), lambda l, j, c_ref: (l, j, 0, 0))),
        out_shape=jax.ShapeDtypeStruct((L, J, Rh, C), BF16),
        compiler_params=_params(2),
    )(_core_index(), g, recv)


def exchange_between_chips(sums):
    def copies(ins, outs):
        x, y, c, chips = _place()
        return [(ins[w].at[:, 2 * px + py], outs[w].at[k], (px, py, c))
                for w in range(len(sums)) for k, (px, py) in enumerate(chips)]

    return _exchange_plan(sums, [jax.ShapeDtypeStruct((3, s.shape[0]) + s.shape[2:], s.dtype) for s in sums], copies,
                          3 * len(sums))


def sum_chips(own, recv, name):
    L, _, Rh, C = own.shape

    def body(chip_ref, o_ref, r_ref, out_ref):
        acc = o_ref[...].astype(F32)
        for k in range(3):
            acc = acc + r_ref[k].astype(F32)
        out_ref[...] = acc

    return pl.pallas_call(
        body, name=name,
        grid_spec=pltpu.PrefetchScalarGridSpec(
            num_scalar_prefetch=1, grid=(L,),
            in_specs=[pl.BlockSpec((None, None, Rh, C), lambda l, chip_ref: (l, chip_ref[0], 0, 0)),
                      pl.BlockSpec((3, None, Rh, C), lambda l, chip_ref: (0, l, 0, 0))],
            out_specs=pl.BlockSpec((None, Rh, C), lambda l, chip_ref: (l, 0, 0))),
        out_shape=jax.ShapeDtypeStruct((L, Rh, C), F32),
        compiler_params=_params(1),
    )(_chip_index(), own, recv)


def share_with_sibling(halves):
    def copies(ins, outs):
        x, y, c, _ = _place()
        return [(ins[w], outs[w], (x, y, 1 - c)) for w in range(len(halves))]

    return _exchange_plan(halves, [jax.ShapeDtypeStruct(h.shape, h.dtype) for h in halves], copies, len(halves))


def adamw_shard(w, m, v, own, sib, l, name):
    R, C = w.shape
    Rh = R // 2

    def body(c_ref, w_ref, m_ref, v_ref, o_ref, s_ref, d_ref, nm_ref, nv_ref, g_ref):
        gv = jnp.where(pl.program_id(0) == c_ref[0], o_ref[...], s_ref[...])
        d, m2, v2 = _adamw_math(w_ref[...], gv, m_ref[...], v_ref[...])
        d_ref[...] = d
        nm_ref[...] = m2
        nv_ref[...] = v2
        g_ref[...] = gv

    blk = pl.BlockSpec((Rh, C), lambda h, c_ref: (h, 0))
    half = pl.BlockSpec((None, Rh, C), lambda h, c_ref: (l, 0, 0))
    sds = jax.ShapeDtypeStruct((R, C), F32)
    return pl.pallas_call(
        body, name=name,
        grid_spec=pltpu.PrefetchScalarGridSpec(
            num_scalar_prefetch=1, grid=(2,), in_specs=[blk, blk, blk, half, half], out_specs=[blk] * 4),
        out_shape=[sds] * 4,
        compiler_params=_params(1),
    )(_core_index(), w, m, v, own, sib)


def all_reduce_small(part):
    R, C = part.shape

    def body(x_ref, o_ref, all_ref, send_sem, recv_sem):
        x, y, c, _ = _place()
        me = 4 * x + 2 * y + c
        all_ref[me] = x_ref[...]
        cps = []
        for k in range(1, N_DEV):
            fx, fy, fc = (k >> 2) & 1, (k >> 1) & 1, k & 1
            peer = (x ^ fx, y ^ fy, c ^ fc)
            cp = pltpu.make_async_remote_copy(
                src_ref=x_ref, dst_ref=all_ref.at[me], send_sem=send_sem.at[k - 1], recv_sem=recv_sem.at[k - 1],
                device_id=peer, device_id_type=MESH)
            cp.start()
            cps.append(cp)
        for k in range(1, N_DEV):
            fx, fy, fc = (k >> 2) & 1, (k >> 1) & 1, k & 1
            peer = (x ^ fx, y ^ fy, c ^ fc)
            src = all_ref.at[4 * peer[0] + 2 * peer[1] + peer[2]]
            pltpu.make_async_remote_copy(
                src_ref=x_ref, dst_ref=src, send_sem=send_sem.at[k - 1], recv_sem=recv_sem.at[k - 1],
                device_id=peer, device_id_type=MESH).wait_recv()
        for cp in cps:
            cp.wait_send()
        acc = all_ref[0]
        for d in range(1, N_DEV):
            acc = acc + all_ref[d]
        o_ref[...] = acc

    return pl.pallas_call(
        body, name="all_reduce_small",
        in_specs=[pl.BlockSpec(memory_space=pltpu.VMEM)], out_specs=pl.BlockSpec(memory_space=pltpu.VMEM),
        out_shape=jax.ShapeDtypeStruct((R, C), F32),
        scratch_shapes=[pltpu.VMEM((N_DEV, R, C), F32), pltpu.SemaphoreType.DMA((N_DEV - 1,)),
                        pltpu.SemaphoreType.DMA((N_DEV - 1,))],
        compiler_params=pltpu.CompilerParams(vmem_limit_bytes=VMEM_LIMIT),
    )(part)


SMALL = ("norm_ffn1", "norm_mix", "norm_ffn2", "norm_final", "ret_gn_gain", "pool_scale", "pool_w")
GROUPS = (("ffn1", ("ffn1_gate", "ffn1_up", "ffn1_down")), ("w_in", ("w_in",)), ("w_out", ("w_out",)),
          ("ffn2", ("ffn2_gate", "ffn2_up", "ffn2_down")))
TRANSPOSED = ("ffn1_gate", "ffn1_up", "ffn2_gate", "ffn2_up")
IN_W = 2 * QK_W + 3 * V_W
ORDER = ("norm_ffn1", "ffn1_gate", "ffn1_up", "ffn1_down", "norm_mix", "w_in", "ret_gn_gain", "pool_w", "pool_scale",
         "w_out", "norm_ffn2", "ffn2_gate", "ffn2_up", "ffn2_down", "norm_final")


def _pack_small(d):
    return jnp.concatenate([d[k].reshape(-1, 128) for k in SMALL], axis=0)


def _unpack_small(packed, shapes):
    out, row = {}, 0
    for k in SMALL:
        n = math.prod(shapes[k]) // 128
        out[k] = packed[row:row + n].reshape(shapes[k])
        row += n
    return out


def kernel(x, norm_ffn1, ffn1_gate, ffn1_up, ffn1_down, norm_mix, w_in, ret_gn_gain, pool_w, pool_scale, w_out, norm_ffn2, ffn2_gate, ffn2_up, ffn2_down, norm_final, loss_target, m_norm_ffn1, m_ffn1_gate, m_ffn1_up, m_ffn1_down, m_norm_mix, m_w_in, m_ret_gn_gain, m_pool_w, m_pool_scale, m_w_out, m_norm_ffn2, m_ffn2_gate, m_ffn2_up, m_ffn2_down, m_norm_final, v_norm_ffn1, v_ffn1_gate, v_ffn1_up, v_ffn1_down, v_norm_mix, v_w_in, v_ret_gn_gain, v_pool_w, v_pool_scale, v_w_out, v_norm_ffn2, v_ffn2_gate, v_ffn2_up, v_ffn2_down, v_norm_final):
    W = dict(norm_ffn1=norm_ffn1, ffn1_gate=ffn1_gate, ffn1_up=ffn1_up, ffn1_down=ffn1_down, norm_mix=norm_mix, w_in=w_in,
             ret_gn_gain=ret_gn_gain, pool_w=pool_w, pool_scale=pool_scale, w_out=w_out, norm_ffn2=norm_ffn2,
             ffn2_gate=ffn2_gate, ffn2_up=ffn2_up, ffn2_down=ffn2_down, norm_final=norm_final)
    M = dict(norm_ffn1=m_norm_ffn1, ffn1_gate=m_ffn1_gate, ffn1_up=m_ffn1_up, ffn1_down=m_ffn1_down, norm_mix=m_norm_mix,
             w_in=m_w_in, ret_gn_gain=m_ret_gn_gain, pool_w=m_pool_w, pool_scale=m_pool_scale, w_out=m_w_out,
             norm_ffn2=m_norm_ffn2, ffn2_gate=m_ffn2_gate, ffn2_up=m_ffn2_up, ffn2_down=m_ffn2_down, norm_final=m_norm_final)
    V = dict(norm_ffn1=v_norm_ffn1, ffn1_gate=v_ffn1_gate, ffn1_up=v_ffn1_up, ffn1_down=v_ffn1_down, norm_mix=v_norm_mix,
             w_in=v_w_in, ret_gn_gain=v_ret_gn_gain, pool_w=v_pool_w, pool_scale=v_pool_scale, w_out=v_w_out,
             norm_ffn2=v_norm_ffn2, ffn2_gate=v_ffn2_gate, ffn2_up=v_ffn2_up, ffn2_down=v_ffn2_down, norm_final=v_norm_final)

    n_seq, S, D = x.shape
    T = n_seq * S
    tm = min(512, T // 2)
    tm_fwd = min(1024, T // 2)
    tk = min(1024, T // 2)
    xf = x.reshape(T, D)
    tgt = loss_target.reshape(T, D)

    def local(d, k):
        a = d[k][0]
        return a.T if k in TRANSPOSED else a

    def to_out(k, a):
        return (a.T if k in TRANSPOSED else a)[None]

    loc_bf = []
    for _, members in GROUPS:
        st = jnp.stack([local(W, k).astype(BF16) for k in members])
        loc_bf.append(st.reshape(st.shape[0], 2, st.shape[1] // 2, st.shape[2]))
    FS = loc_bf[0].shape[2] * 2
    w3_1 = gather_weights(loc_bf[:1])[0].reshape(3, N_CHIPS, FS, D)
    g1 = norm_ffn1.reshape(1, D)
    gm = norm_mix.reshape(1, D)
    g3 = norm_ffn2.reshape(1, D)
    gf = norm_final.reshape(1, D)
    gn_gain = ret_gn_gain.reshape(1, V_W)
    pscale = pool_scale.reshape(1, V_W)
    pw = pool_w.reshape(len(POOL_WINDOWS), POOL_GC, POOL_GC)
    consts = mixer_constants(S)

    def halves5(g, lb):
        return g.reshape(lb.shape[0], N_CHIPS, 2, lb.shape[2], lb.shape[3])

    (x1, a1, b1, n1), landed = ffn_fwd(xf, g1, w3_1, tm_fwd, "ffn1_fwd", plan=gather_plan(loc_bf[1:]))
    gathered = [place_own(g, lb, "place_own_" + gn) for g, lb, (gn, _) in zip(landed, loc_bf[1:], GROUPS[1:])]
    win_full = gathered[0].reshape(N_CHIPS, D, IN_W // N_CHIPS)
    wout_full = gathered[1].reshape(D, D)
    w3_2 = gathered[2].reshape(3, N_CHIPS, FS, D)
    p = in_proj(x1, gm, win_full, tm, "in_proj")
    x2, m_s, o_s, st_s, pooled_s = mixer_fwd(p, x1, consts, gn_gain, pw, pscale, wout_full, n_seq, S, "mixer_fwd")
    x3, a2, b2, n3 = ffn_fwd(x2, g3, w3_2, tm_fwd, "ffn2_fwd")

    dx3, df2, lacc, dgf = loss_bwd(x3, tgt, gf, tm, "loss_bwd")
    loss = lax.psum(jnp.sum(lacc) * (0.5 / D), ("x", "y", "c"))
    dn3, da2, db2, h2 = ffn_bwd(df2, a2, b2, w3_2, tm_fwd, "ffn2_bwd")
    dx2, dg3 = rms_bwd(x2, dx3, dn3, g3, tm, "ffn2_rms_bwd")
    (gr_ffn2,) = ffn_wgrad(da2, db2, h2, n3, df2, tk, "ffn2_wgrad")
    g_ffn2 = halves5(gr_ffn2, loc_bf[3])
    (dx1, df1, dp, n2, dgm, dgn, dps, dpw), (sb_ffn2,) = mixer_bwd(
        dx2, x1, p, o_s, st_s, pooled_s, consts, gm, gn_gain, pw, pscale, wout_full, win_full, n_seq, S, "mixer_bwd",
        plan=send_to_sibling_other_half([g_ffn2]))
    cs_ffn2 = add_own_half(g_ffn2, sb_ffn2, "add_sibling_ffn2")
    (dn1, da1, db1, h1), (pc_ffn2,) = ffn_bwd(
        df1, a1, b1, w3_1, tm_fwd, "ffn1_bwd", plan=exchange_between_chips([cs_ffn2]))
    dx0, dg1 = rms_bwd(xf, dx1, dn1, g1, tm, "ffn1_rms_bwd")
    mh_ffn2 = sum_chips(cs_ffn2, pc_ffn2, "sum_chips_ffn2")
    (gr_ffn1,), (sh_ffn2,) = ffn_wgrad(da1, db1, h1, n1, df1, tk, "ffn1_wgrad", plan=share_with_sibling([mh_ffn2]))
    g_ffn1 = halves5(gr_ffn1, loc_bf[0])
    (gr_wout,), (sb_ffn1,) = tn_shared(m_s, dx2[None], tk, "dw_out", plan=send_to_sibling_other_half([g_ffn1]))
    cs_ffn1 = add_own_half(g_ffn1, sb_ffn1, "add_sibling_ffn1")
    (gr_win,), (pc_ffn1,) = tn_shared(n2, dp, tk, "dw_in", plan=exchange_between_chips([cs_ffn1]))
    mh_ffn1 = sum_chips(cs_ffn1, pc_ffn1, "sum_chips_ffn1")
    g_mix = [halves5(gr_win, loc_bf[1]), halves5(gr_wout, loc_bf[2])]
    sb_mix = _run_plan(send_to_sibling_other_half(g_mix), "mixer_grads_to_sibling")
    cs_mix = [add_own_half(g, r, "add_sibling_" + gn) for g, r, gn in zip(g_mix, sb_mix, ("w_in", "w_out"))]
    pc_mix = _run_plan(exchange_between_chips(cs_mix), "mixer_grads_between_chips")
    mh_mix = [sum_chips(cs, pc, "sum_chips_" + gn) for cs, pc, gn in zip(cs_mix, pc_mix, ("w_in", "w_out"))]
    sh_ffn1, sh_win, sh_wout = _run_plan(share_with_sibling([mh_ffn1] + mh_mix), "grads_back_to_sibling")
    my_half = [mh_ffn1, mh_mix[0], mh_mix[1], mh_ffn2]
    sib_half = [sh_ffn1, sh_win, sh_wout, sh_ffn2]

    small_part = dict(norm_ffn1=dg1, norm_mix=dgm, norm_ffn2=dg3, norm_final=dgf, ret_gn_gain=dgn, pool_scale=dps, pool_w=dpw)
    small_sum = all_reduce_small(_pack_small(small_part))

    out_g, out_d, out_m, out_v = {}, {}, {}, {}
    for (_, members), own, sib in zip(GROUPS, my_half, sib_half):
        for l, k in enumerate(members):
            d_, m_, v_, g_ = adamw_shard(local(W, k), local(M, k), local(V, k), own, sib, l, "adamw_" + k)
            out_g[k], out_d[k], out_m[k], out_v[k] = to_out(k, g_), to_out(k, d_), to_out(k, m_), to_out(k, v_)
    shapes = {k: W[k].shape for k in SMALL}
    d_, m_, v_ = adamw(_pack_small(W), small_sum, _pack_small(M), _pack_small(V), "adamw_small")
    for dst, packed in ((out_g, small_sum), (out_d, d_), (out_m, m_), (out_v, v_)):
        dst.update(_unpack_small(packed, shapes))

    grad_x = dx0.reshape(n_seq, S, D)
    return (loss, grad_x, *[out_g[k] for k in ORDER], *[out_d[k] for k in ORDER],
            *[out_m[k] for k in ORDER], *[out_v[k] for k in ORDER])
```

```python
import functools
import math
from typing import Callable, NamedTuple, Optional

import jax
import jax.numpy as jnp
from jax import lax
from jax.experimental import pallas as pl
from jax.experimental.pallas import tpu as pltpu

F32 = jnp.float32
BF16 = jnp.bfloat16
MESH = pl.DeviceIdType.MESH

N_CHIPS = 4
N_DEV = 8
CHUNK = 64
RET_HEADS = 4
RET_DK = 64
RET_DV = 128
QK_W = RET_HEADS * RET_DK
V_W = RET_HEADS * RET_DV
POOL_WINDOWS = (2, 4, 8, 16)
POOL_GC = 128
ROPE_BASE = 10000.0
RMS_EPS = 1e-6
GN_EPS = 1e-5
ADAM_LR = 0.001
ADAM_B1 = 0.9
ADAM_B2 = 0.999
ADAM_EPS = 1e-08
ADAM_WD = 0.01
ADAM_STEP = 10
MXU_W = 256
RET_BLOCK = MXU_W
VMEM_LIMIT = 56 * 1024 * 1024

NN = (((1,), (0,)), ((), ()))
NT = (((1,), (1,)), ((), ()))
TN = (((0,), (0,)), ((), ()))


def _dot(a, b, dims=NN):
    return lax.dot_general(a, b, dims, preferred_element_type=F32)


def _sigmoid(x):
    return 0.5 * jnp.tanh(0.5 * x) + 0.5


def _params(n_grid):
    return pltpu.CompilerParams(dimension_semantics=("arbitrary",) * n_grid, vmem_limit_bytes=VMEM_LIMIT)


class CommPlan(NamedTuple):
    ins: tuple
    out_shape: tuple
    sems: tuple
    start: Callable
    finish: Callable
    mid: Optional[Callable] = None
    mid_at: float = 0.85


ANY = pl.BlockSpec(memory_space=pl.ANY)


def _call(body, plan, *, name, grid, in_specs, out_specs, out_shape, args, scratch_shapes=()):
    n_grid = len(grid)
    if plan is None:
        return pl.pallas_call(body, name=name, grid=grid, in_specs=in_specs, out_specs=out_specs, out_shape=out_shape,
                              scratch_shapes=scratch_shapes, compiler_params=_params(n_grid))(*args)
    n_in, n_out, n_sc = len(in_specs), len(out_specs), len(scratch_shapes)
    p_in, p_out = len(plan.ins), len(plan.out_shape)
    total = math.prod(grid)
    mid_step = min(total - 1, int(plan.mid_at * total))

    def riding(*refs):
        ins, pins = refs[:n_in], refs[n_in:n_in + p_in]
        o0 = n_in + p_in
        outs, pouts = refs[o0:o0 + n_out], refs[o0 + n_out:o0 + n_out + p_out]
        s0 = o0 + n_out + p_out
        scratch, psems = refs[s0:s0 + n_sc], refs[s0 + n_sc:]
        step = pl.program_id(0)
        for d in range(1, n_grid):
            step = step * grid[d] + pl.program_id(d)

        @pl.when(step == 0)
        def _():
            plan.start(pins, pouts, psems)

        body(*ins, *outs, *scratch)

        if plan.mid is not None:
            @pl.when(step == mid_step)
            def _():
                plan.mid(pins, pouts, psems)

        @pl.when(step == total - 1)
        def _():
            plan.finish(pins, pouts, psems)

    res = pl.pallas_call(
        riding, name=name, grid=grid, in_specs=list(in_specs) + [ANY] * p_in, out_specs=list(out_specs) + [ANY] * p_out,
        out_shape=list(out_shape) + list(plan.out_shape), scratch_shapes=list(scratch_shapes) + list(plan.sems),
        compiler_params=_params(n_grid))(*args, *plan.ins)
    return res[:n_out], res[n_out:]


def _run_plan(plan, name):
    p_in, p_out = len(plan.ins), len(plan.out_shape)

    def body(*refs):
        pins, pouts, psems = refs[:p_in], refs[p_in:p_in + p_out], refs[p_in + p_out:]
        plan.start(pins, pouts, psems)
        if plan.mid is not None:
            plan.mid(pins, pouts, psems)
        plan.finish(pins, pouts, psems)

    return pl.pallas_call(body, name=name, in_specs=[ANY] * p_in, out_specs=[ANY] * p_out,
                          out_shape=list(plan.out_shape), scratch_shapes=list(plan.sems))(*plan.ins)


def _rms_parts(x):
    r = lax.rsqrt(jnp.mean(x * x, axis=-1, keepdims=True) + RMS_EPS)
    return r, x * r


def _rms_bwd(dn, xhat, r, gain, dres):
    dxh = dn * gain
    return dres + r * (dxh - xhat * jnp.mean(dxh * xhat, axis=-1, keepdims=True))


def _w3_specs(FS, D):
    return [pl.BlockSpec((None, None, FS, D), functools.partial(lambda i, j, k: (k, j, 0, 0), k=k)) for k in range(3)]


def ffn_fwd(x, gain, w3, tm, name, plan=None):
    T, D = x.shape
    _, J, FS, _ = w3.shape

    def body(x_ref, g_ref, wg_ref, wu_ref, wd_ref, xo_ref, a_ref, b_ref, n_ref, acc):
        j = pl.program_id(1)

        @pl.when(j == 0)
        def _():
            _, xhat = _rms_parts(x_ref[...])
            n_ref[...] = (xhat * g_ref[...]).astype(BF16)
            acc[...] = jnp.zeros_like(acc)

        n = n_ref[...]
        a = _dot(n, wg_ref[...], NT)
        b = _dot(n, wu_ref[...], NT)
        a_ref[...] = a.astype(BF16)
        b_ref[...] = b.astype(BF16)
        h = (a * _sigmoid(a) * b).astype(BF16)
        acc[...] += _dot(h, wd_ref[...])

        @pl.when(j == J - 1)
        def _():
            xo_ref[...] = x_ref[...] + 0.5 * acc[...]

    return _call(
        body, plan, name=name, grid=(T // tm, J),
        in_specs=[pl.BlockSpec((tm, D), lambda i, j: (i, 0)),
                  pl.BlockSpec((1, D), lambda i, j: (0, 0))] + _w3_specs(FS, D),
        out_specs=[pl.BlockSpec((tm, D), lambda i, j: (i, 0)),
                   pl.BlockSpec((None, tm, FS), lambda i, j: (j, i, 0)),
                   pl.BlockSpec((None, tm, FS), lambda i, j: (j, i, 0)),
                   pl.BlockSpec((tm, D), lambda i, j: (i, 0))],
        out_shape=[jax.ShapeDtypeStruct((T, D), F32),
                   jax.ShapeDtypeStruct((J, T, FS), BF16),
                   jax.ShapeDtypeStruct((J, T, FS), BF16),
                   jax.ShapeDtypeStruct((T, D), BF16)],
        scratch_shapes=[pltpu.VMEM((tm, D), F32)],
        args=(x, gain, w3, w3, w3))


def ffn_bwd(df, a_s, b_s, w3, tm, name, plan=None):
    T, D = df.shape
    _, J, FS, _ = w3.shape

    chunks = [slice(s, min(s + MXU_W, FS)) for s in range(0, FS, MXU_W)]
    nc = len(chunks)

    def body(df_ref, a_ref, b_ref, wg_ref, wu_ref, wd_ref, dn_ref, da_ref, db_ref, h_ref):
        @pl.when(pl.program_id(1) == 0)
        def _():
            dn_ref[...] = jnp.zeros_like(dn_ref)

        def elementwise(dh, sl):
            a = a_ref[:, sl].astype(F32)
            b = b_ref[:, sl].astype(F32)
            sg = _sigmoid(a)
            si = a * sg
            da = (dh * b * (sg + si * (1.0 - sg))).astype(BF16)
            db = (dh * si).astype(BF16)
            da_ref[:, sl] = da
            db_ref[:, sl] = db
            h_ref[:, sl] = (si * b).astype(BF16)
            return da, db

        df = df_ref[...]
        dh, dab, acc = [None] * nc, [None] * nc, None
        for step in range(nc + 2):
            if step < nc:
                dh[step] = _dot(df, wd_ref[chunks[step], :], NT)
            if 1 <= step <= nc:
                dab[step - 1] = elementwise(dh[step - 1], chunks[step - 1])
            if step >= 2:
                sl = chunks[step - 2]
                part = _dot(dab[step - 2][0], wg_ref[sl, :]) + _dot(dab[step - 2][1], wu_ref[sl, :])
                acc = part if acc is None else acc + part
        dn_ref[...] += acc

    tok = pl.BlockSpec((tm, D), lambda i, j: (i, 0))
    sh = pl.BlockSpec((None, tm, FS), lambda i, j: (j, i, 0))
    return _call(
        body, plan, name=name, grid=(T // tm, J),
        in_specs=[tok, sh, sh] + _w3_specs(FS, D),
        out_specs=[tok, sh, sh, sh],
        out_shape=[jax.ShapeDtypeStruct((T, D), F32),
                   jax.ShapeDtypeStruct((J, T, FS), BF16),
                   jax.ShapeDtypeStruct((J, T, FS), BF16),
                   jax.ShapeDtypeStruct((J, T, FS), BF16)],
        args=(df, a_s, b_s, w3, w3, w3))


def rms_bwd(x, dout, dn, gain, tm, name, plan=None):
    T, D = x.shape

    def body(x_ref, do_ref, dn_ref, g_ref, dx_ref, dgain_ref):
        @pl.when(pl.program_id(0) == 0)
        def _():
            dgain_ref[...] = jnp.zeros_like(dgain_ref)

        r, xhat = _rms_parts(x_ref[...])
        dn = dn_ref[...]
        dgain_ref[...] += jnp.sum(dn * xhat, axis=0, keepdims=True)
        dx_ref[...] = _rms_bwd(dn, xhat, r, g_ref[...], do_ref[...])

    tok = pl.BlockSpec((tm, D), lambda i: (i, 0))
    vec = pl.BlockSpec((1, D), lambda i: (0, 0))
    return _call(
        body, plan, name=name, grid=(T // tm,), in_specs=[tok, tok, tok, vec], out_specs=[tok, vec],
        out_shape=[jax.ShapeDtypeStruct((T, D), F32), jax.ShapeDtypeStruct((1, D), F32)],
        args=(x, dout, dn, gain))


def ffn_wgrad(da, db, h, n, df, tk, name, plan=None):
    J, T, FS = da.shape
    D = n.shape[1]

    def body(da_ref, db_ref, h_ref, n_ref, df_ref, o_ref):
        @pl.when(pl.program_id(1) == 0)
        def _():
            o_ref[...] = jnp.zeros_like(o_ref)

        nv = n_ref[...]
        o_ref[0] += _dot(da_ref[...], nv, TN)
        o_ref[1] += _dot(db_ref[...], nv, TN)
        o_ref[2] += _dot(h_ref[...], df_ref[...], TN)

    sh = pl.BlockSpec((None, tk, FS), lambda j, k: (j, k, 0))
    tok = pl.BlockSpec((tk, D), lambda j, k: (k, 0))
    return _call(
        body, plan, name=name, grid=(J, T // tk),
        in_specs=[sh, sh, sh, tok, tok],
        out_specs=[pl.BlockSpec((3, None, FS, D), lambda j, k: (0, j, 0, 0))],
        out_shape=[jax.ShapeDtypeStruct((3, J, FS, D), F32)],
        args=(da, db, h, n, df))


def tn_shared(a, b, tk, name, plan=None):
    T, M = a.shape
    J, _, N = b.shape

    def body(a_ref, b_ref, o_ref):
        @pl.when(pl.program_id(0) == 0)
        def _():
            o_ref[...] = jnp.zeros_like(o_ref)

        a_t = a_ref[...].astype(BF16).T
        for j in range(J):
            o_ref[j] += _dot(a_t, b_ref[j].astype(BF16))

    return _call(
        body, plan, name=name, grid=(T // tk,),
        in_specs=[pl.BlockSpec((tk, M), lambda k: (k, 0)), pl.BlockSpec((J, tk, N), lambda k: (0, k, 0))],
        out_specs=[pl.BlockSpec((J, M, N), lambda k: (0, 0, 0))],
        out_shape=[jax.ShapeDtypeStruct((J, M, N), F32)],
        args=(a, b))


def in_proj(x, gain, win, tm, name):
    T, D = x.shape
    J, _, W = win.shape

    def body(x_ref, g_ref, w_ref, p_ref):
        _, xhat = _rms_parts(x_ref[...])
        n = (xhat * g_ref[...]).astype(BF16)
        for j in range(J):
            p_ref[j] = _dot(n, w_ref[j])

    return pl.pallas_call(
        body, name=name, grid=(T // tm,),
        in_specs=[pl.BlockSpec((tm, D), lambda i: (i, 0)),
                  pl.BlockSpec((1, D), lambda i: (0, 0)),
                  pl.BlockSpec((J, D, W), lambda i: (0, 0, 0))],
        out_specs=pl.BlockSpec((J, tm, W), lambda i: (0, i, 0)),
        out_shape=jax.ShapeDtypeStruct((J, T, W), F32),
        compiler_params=_params(1),
    )(x, gain, win)


def mixer_constants(S):
    B = RET_BLOCK
    half = RET_DK // 2
    freqs = ROPE_BASE ** (-jnp.arange(half, dtype=F32) * 2.0 / RET_DK)
    ang = jnp.arange(S, dtype=F32)[:, None] * freqs[None, :]
    cos = jnp.tile(jnp.cos(ang), (1, 2 * RET_HEADS))
    sin = jnp.tile(jnp.concatenate([-jnp.sin(ang), jnp.sin(ang)], axis=1), (1, RET_HEADS))
    gamma = 1.0 - 2.0 ** (-5.0 - jnp.arange(RET_HEADS, dtype=F32))
    log_g = jnp.log(gamma)
    idx = jnp.arange(B, dtype=F32)
    ci = jnp.arange(B) // CHUNK
    dist = jnp.abs(idx[:, None] - idx[None, :])
    m_intra = jnp.exp(log_g[:, None, None] * dist[None]) * (ci[None, :] <= ci[:, None])[None].astype(F32)
    lg_lane = jnp.repeat(log_g, RET_DK)
    d_q = jnp.exp(lg_lane[None, :] * (idx[:, None] + 1.0))
    d_k = jnp.exp(lg_lane[None, :] * (B - 1.0 - idx[:, None]))
    d_s = jnp.broadcast_to(jnp.exp(lg_lane * B)[:, None], (QK_W, V_W))
    bm = (jnp.arange(QK_W)[:, None] // RET_DK == jnp.arange(V_W)[None, :] // RET_DV).astype(F32)
    win = jnp.repeat(jnp.array(POOL_WINDOWS, F32), POOL_GC)
    invc = 1.0 / jnp.minimum(jnp.arange(S, dtype=F32)[:, None] + 1.0, win[None, :])
    return dict(cos=cos, sin=sin, m=m_intra, dq=d_q, dk=d_k, ds=d_s, bm=bm, invc=invc)


def _swap_halves(x):
    w = x.shape[1]
    lane = lax.broadcasted_iota(jnp.int32, x.shape, 1)
    return jnp.where((lane % RET_DK) < RET_DK // 2, pltpu.roll(x, w - RET_DK // 2, 1), pltpu.roll(x, RET_DK // 2, 1))


def _rot(x, cos, sin):
    return x * cos + _swap_halves(x) * sin


def _rot_t(d, cos, sin):
    return d * cos + _swap_halves(d * sin)


def _lane_groups(parts):
    return jnp.concatenate([p[:, POOL_GC * g:POOL_GC * (g + 1)] for g, p in enumerate(parts)], axis=1)


def _head_mask(shape, h):
    lane = lax.broadcasted_iota(jnp.int32, shape, 1)
    return (lane // RET_DK) == h


def _group_norm(o):
    yh, rs = [], []
    for h in range(RET_HEADS):
        oh = o[:, RET_DV * h:RET_DV * (h + 1)]
        xc = oh - jnp.mean(oh, axis=-1, keepdims=True)
        r = lax.rsqrt(jnp.mean(xc * xc, axis=-1, keepdims=True) + GN_EPS)
        yh.append(xc * r)
        rs.append(r)
    return yh, rs


def mixer_fwd(p, x1, consts, gn_gain, pool_w, pool_scale, wout, n_seq, S, name):
    _, T, _ = p.shape
    D = x1.shape[1]
    B = RET_BLOCK
    nb = S // B

    def body(p_ref, x1_ref, cos_ref, sin_ref, invc_ref, m_ref, dq_ref, dk_ref, ds_ref, bm_ref,
             gain_ref, pw_ref, sc_ref, wout_ref,
             x2_ref, m_out, o_out, st_out, pooled_out, state, prev_u):
        blk = pl.program_id(1)

        @pl.when(blk == 0)
        def _():
            state[...] = jnp.zeros_like(state)
            prev_u[...] = jnp.zeros_like(prev_u)

        qk = p_ref[0]
        v = p_ref[1]
        g = p_ref[2]
        u = p_ref[3]
        cos = cos_ref[...]
        sin = sin_ref[...]
        qr = _rot(qk[:, :QK_W], cos, sin) * (RET_DK ** -0.5)
        kr = _rot(qk[:, QK_W:], cos, sin)
        qb = qr.astype(BF16)
        kb = kr.astype(BF16)
        vb = v.astype(BF16)
        st = state[...]
        st_out[...] = st
        cross = _dot((qr * dq_ref[...]).astype(BF16), st.astype(BF16))
        o_parts = []
        for h in range(RET_HEADS):
            qm = jnp.where(_head_mask(qb.shape, h), qb, jnp.zeros_like(qb))
            sc = (_dot(qm, kb, NT) * m_ref[h]).astype(BF16)
            o_parts.append(_dot(sc, vb[:, RET_DV * h:RET_DV * (h + 1)]) + cross[:, RET_DV * h:RET_DV * (h + 1)])
        o = jnp.concatenate(o_parts, axis=1)
        o_out[...] = o
        kv = _dot((kr * dk_ref[...]).astype(BF16), vb, TN)
        state[...] = st * ds_ref[...] + kv * bm_ref[...]

        yh, _ = _group_norm(o)
        r = g * _sigmoid(g) * (jnp.concatenate(yh, axis=1) * gain_ref[...])

        ext = jnp.concatenate([prev_u[...], u], axis=0)
        sums = []
        run = ext
        for k in (1, 2, 4, 8):
            run = run + pltpu.roll(run, k, 0)
            sums.append(run[B:, :])
        prev_u[...] = u
        pooled = (_lane_groups(sums) * invc_ref[...] - u).astype(BF16)
        pooled_out[...] = pooled
        yp = [_dot(pooled[:, POOL_GC * gi:POOL_GC * (gi + 1)], pw_ref[gi].astype(BF16)) for gi in range(len(POOL_WINDOWS))]
        s = jnp.concatenate(yp, axis=1) * sc_ref[...]
        m = jnp.concatenate([r, s], axis=1).astype(BF16)
        m_out[...] = m
        x2_ref[...] = x1_ref[...] + _dot(m, wout_ref[...])

    tokmap = lambda b, k: (b * nb + k, 0)
    seqmap = lambda b, k: (k, 0)
    const2 = lambda b, k: (0, 0)
    const3 = lambda b, k: (0, 0, 0)
    return pl.pallas_call(
        body, name=name, grid=(n_seq, nb),
        in_specs=[pl.BlockSpec((4, B, V_W), lambda b, k: (0, b * nb + k, 0)),
                  pl.BlockSpec((B, D), tokmap),
                  pl.BlockSpec((B, QK_W), seqmap), pl.BlockSpec((B, QK_W), seqmap), pl.BlockSpec((B, V_W), seqmap),
                  pl.BlockSpec((RET_HEADS, B, B), const3),
                  pl.BlockSpec((B, QK_W), const2), pl.BlockSpec((B, QK_W), const2),
                  pl.BlockSpec((QK_W, V_W), const2), pl.BlockSpec((QK_W, V_W), const2),
                  pl.BlockSpec((1, V_W), const2), pl.BlockSpec((4, POOL_GC, POOL_GC), const3),
                  pl.BlockSpec((1, V_W), const2), pl.BlockSpec((D, D), const2)],
        out_specs=[pl.BlockSpec((B, D), tokmap), pl.BlockSpec((B, D), tokmap), pl.BlockSpec((B, V_W), tokmap),
                   pl.BlockSpec((None, QK_W, V_W), lambda b, k: (b * nb + k, 0, 0)),
                   pl.BlockSpec((B, V_W), tokmap)],
        out_shape=[jax.ShapeDtypeStruct((T, D), F32), jax.ShapeDtypeStruct((T, D), BF16),
                   jax.ShapeDtypeStruct((T, V_W), F32), jax.ShapeDtypeStruct((T // B, QK_W, V_W), F32),
                   jax.ShapeDtypeStruct((T, V_W), BF16)],
        scratch_shapes=[pltpu.VMEM((QK_W, V_W), F32), pltpu.VMEM((B, V_W), F32)],
        compiler_params=_params(2),
    )(p, x1, consts["cos"], consts["sin"], consts["invc"], consts["m"], consts["dq"], consts["dk"],
      consts["ds"], consts["bm"], gn_gain, pool_w, pool_scale, wout)


def mixer_bwd(dx3, dn3, x2, g3, x1, p, o_s, st_s, pooled_s, consts, gmix, gn_gain, pool_w, pool_scale, wout, win,
              n_seq, S, name, plan=None):
    T, D = x1.shape
    B = RET_BLOCK
    nb = S // B
    NG = len(POOL_WINDOWS)

    def body(dx3_ref, dn3_ref, x2_ref, g3_ref, x1_ref, p_ref, o_ref, st_ref, pooled_ref, cos_ref, sin_ref, invc_ref,
             m_ref, dq_ref, dk_ref, ds_ref, bm_ref, gmix_ref, gain_ref, pw_ref, sc_ref, wout_ref, win_ref,
             dx1_ref, df1_ref, dx2_ref, dp_ref, n2_ref, dg3_ref, dgmix_ref, dgain_ref, dscale_ref, dpw_ref,
             gstate, next_e):
        b = pl.program_id(0)
        blk = pl.program_id(1)

        @pl.when(jnp.logical_and(b == 0, blk == 0))
        def _():
            dg3_ref[...] = jnp.zeros_like(dg3_ref)
            dgmix_ref[...] = jnp.zeros_like(dgmix_ref)
            dgain_ref[...] = jnp.zeros_like(dgain_ref)
            dscale_ref[...] = jnp.zeros_like(dscale_ref)
            dpw_ref[...] = jnp.zeros_like(dpw_ref)

        @pl.when(blk == 0)
        def _():
            gstate[...] = jnp.zeros_like(gstate)
            next_e[...] = jnp.zeros_like(next_e)

        r2, xhat2 = _rms_parts(x2_ref[...])
        dn3 = dn3_ref[...]
        dg3_ref[...] += jnp.sum(dn3 * xhat2, axis=0, keepdims=True)
        dx2v = _rms_bwd(dn3, xhat2, r2, g3_ref[...], dx3_ref[...])
        dx2b = dx2v.astype(BF16)
        dx2_ref[...] = dx2b
        dm = _dot(dx2b, wout_ref[...], NT)
        dr = dm[:, :V_W]
        dsv = dm[:, V_W:]

        pooled = pooled_ref[...]
        scale = sc_ref[...]
        dyp = (dsv * scale).astype(BF16)
        yp, dpl = [], []
        for gi in range(NG):
            sl = slice(POOL_GC * gi, POOL_GC * (gi + 1))
            pwb = pw_ref[gi].astype(BF16)
            yp.append(_dot(pooled[:, sl], pwb))
            dpw_ref[gi] += _dot(pooled[:, sl], dyp[:, sl], TN)
            dpl.append(_dot(dyp[:, sl], pwb, NT))
        dscale_ref[...] += jnp.sum(dsv * jnp.concatenate(yp, axis=1), axis=0, keepdims=True)
        dpooled = jnp.concatenate(dpl, axis=1)
        e = dpooled * invc_ref[...]
        ext = jnp.concatenate([e, next_e[...]], axis=0)
        sums = []
        run = ext
        for k in (1, 2, 4, 8):
            run = run + pltpu.roll(run, 2 * B - k, 0)
            sums.append(run[:B, :])
        next_e[...] = e
        du = _lane_groups(sums) - dpooled

        qk = p_ref[0]
        v = p_ref[1]
        g = p_ref[2]
        sg = _sigmoid(g)
        si = g * sg
        yh, rs = _group_norm(o_ref[...])
        yhat = jnp.concatenate(yh, axis=1)
        gain = gain_ref[...]
        dg = dr * (yhat * gain) * (sg * (1.0 + g * (1.0 - sg)))
        dy = dr * si
        dgain_ref[...] += jnp.sum(dy * yhat, axis=0, keepdims=True)
        dyh = dy * gain
        do_parts = []
        for h in range(RET_HEADS):
            sl = slice(RET_DV * h, RET_DV * (h + 1))
            dh_ = dyh[:, sl]
            m1 = jnp.mean(dh_, axis=-1, keepdims=True)
            m2 = jnp.mean(dh_ * yh[h], axis=-1, keepdims=True)
            do_parts.append(rs[h] * (dh_ - m1 - yh[h] * m2))
        dob = jnp.concatenate(do_parts, axis=1).astype(BF16)

        cos = cos_ref[...]
        sin = sin_ref[...]
        qr = _rot(qk[:, :QK_W], cos, sin) * (RET_DK ** -0.5)
        kr = _rot(qk[:, QK_W:], cos, sin)
        qb = qr.astype(BF16)
        kb = kr.astype(BF16)
        vb = v.astype(BF16)
        dqd = dq_ref[...]
        dkd = dk_ref[...]
        stb = st_ref[...].astype(BF16)
        gs = gstate[...]
        gb = gs.astype(BF16)
        dqs = _dot(dob, stb, NT) * dqd
        dkr = _dot(vb, gb, NT) * dkd
        dv_cross = _dot((kr * dkd).astype(BF16), gb)
        ds_cross = _dot((qr * dqd).astype(BF16), dob, TN) * bm_ref[...]
        gstate[...] = ds_cross + gs * ds_ref[...]
        dv_parts = []
        for h in range(RET_HEADS):
            sl = slice(RET_DV * h, RET_DV * (h + 1))
            hm = _head_mask(qb.shape, h)
            qm = jnp.where(hm, qb, jnp.zeros_like(qb))
            mh = m_ref[h]
            sc = (_dot(qm, kb, NT) * mh).astype(BF16)
            dsc = (_dot(dob[:, sl], vb[:, sl], NT) * mh).astype(BF16)
            dqs = dqs + jnp.where(hm, _dot(dsc, kb), 0.0)
            dkr = dkr + jnp.where(hm, _dot(dsc, qb, TN), 0.0)
            dv_parts.append(_dot(sc, dob[:, sl], TN) + dv_cross[:, sl])
        dq = _rot_t(dqs * (RET_DK ** -0.5), cos, sin)
        dk = _rot_t(dkr, cos, sin)
        dp = [jnp.concatenate([dq, dk], axis=1).astype(BF16), jnp.concatenate(dv_parts, axis=1).astype(BF16),
              dg.astype(BF16), du.astype(BF16)]
        dn = jnp.zeros((B, D), F32)
        for jj in range(4):
            dp_ref[jj] = dp[jj]
            dn = dn + _dot(dp[jj], win_ref[jj], NT)

        x1v = x1_ref[...]
        r, xhat = _rms_parts(x1v)
        gm = gmix_ref[...]
        n2_ref[...] = (xhat * gm).astype(BF16)
        dgmix_ref[...] += jnp.sum(dn * xhat, axis=0, keepdims=True)
        dx1 = _rms_bwd(dn, xhat, r, gm, dx2v)
        dx1_ref[...] = dx1
        df1_ref[...] = (0.5 * dx1).astype(BF16)

    rev = lambda b, k: (b * nb + (nb - 1 - k), 0)
    seqrev = lambda b, k: (nb - 1 - k, 0)
    const2 = lambda b, k: (0, 0)
    const3 = lambda b, k: (0, 0, 0)
    return _call(
        body, plan, name=name, grid=(n_seq, nb),
        in_specs=[pl.BlockSpec((B, D), rev), pl.BlockSpec((B, D), rev), pl.BlockSpec((B, D), rev),
                  pl.BlockSpec((1, D), const2), pl.BlockSpec((B, D), rev),
                  pl.BlockSpec((4, B, V_W), lambda b, k: (0, b * nb + (nb - 1 - k), 0)),
                  pl.BlockSpec((B, V_W), rev),
                  pl.BlockSpec((None, QK_W, V_W), lambda b, k: (b * nb + (nb - 1 - k), 0, 0)),
                  pl.BlockSpec((B, V_W), rev),
                  pl.BlockSpec((B, QK_W), seqrev), pl.BlockSpec((B, QK_W), seqrev), pl.BlockSpec((B, V_W), seqrev),
                  pl.BlockSpec((RET_HEADS, B, B), const3),
                  pl.BlockSpec((B, QK_W), const2), pl.BlockSpec((B, QK_W), const2),
                  pl.BlockSpec((QK_W, V_W), const2), pl.BlockSpec((QK_W, V_W), const2),
                  pl.BlockSpec((1, D), const2), pl.BlockSpec((1, V_W), const2),
                  pl.BlockSpec((NG, POOL_GC, POOL_GC), const3), pl.BlockSpec((1, V_W), const2),
                  pl.BlockSpec((D, D), const2), pl.BlockSpec((4, D, V_W), const3)],
        out_specs=[pl.BlockSpec((B, D), rev), pl.BlockSpec((B, D), rev), pl.BlockSpec((B, D), rev),
                   pl.BlockSpec((4, B, V_W), lambda b, k: (0, b * nb + (nb - 1 - k), 0)),
                   pl.BlockSpec((B, D), rev),
                   pl.BlockSpec((1, D), const2), pl.BlockSpec((1, D), const2), pl.BlockSpec((1, V_W), const2),
                   pl.BlockSpec((1, V_W), const2), pl.BlockSpec((NG, POOL_GC, POOL_GC), const3)],
        out_shape=[jax.ShapeDtypeStruct((T, D), F32), jax.ShapeDtypeStruct((T, D), BF16),
                   jax.ShapeDtypeStruct((T, D), BF16), jax.ShapeDtypeStruct((4, T, V_W), BF16),
                   jax.ShapeDtypeStruct((T, D), BF16),
                   jax.ShapeDtypeStruct((1, D), F32), jax.ShapeDtypeStruct((1, D), F32), jax.ShapeDtypeStruct((1, V_W), F32),
                   jax.ShapeDtypeStruct((1, V_W), F32), jax.ShapeDtypeStruct((NG, POOL_GC, POOL_GC), F32)],
        scratch_shapes=[pltpu.VMEM((QK_W, V_W), F32), pltpu.VMEM((B, V_W), F32)],
        args=(dx3, dn3, x2, g3, x1, p, o_s, st_s, pooled_s, consts["cos"], consts["sin"], consts["invc"], consts["m"], consts["dq"],
              consts["dk"], consts["ds"], consts["bm"], gmix, gn_gain, pool_w, pool_scale, wout, win))


def loss_bwd(x3, tgt, gain, tm, name):
    T, D = x3.shape

    def body(x_ref, t_ref, g_ref, dx_ref, df_ref, lacc_ref, dg_ref):
        @pl.when(pl.program_id(0) == 0)
        def _():
            lacc_ref[...] = jnp.zeros_like(lacc_ref)
            dg_ref[...] = jnp.zeros_like(dg_ref)

        r, xhat = _rms_parts(x_ref[...])
        gv = g_ref[...]
        err = xhat * gv - t_ref[...]
        lacc_ref[...] += jnp.sum(err * err, axis=0, keepdims=True)
        dy = err * (1.0 / D)
        dg_ref[...] += jnp.sum(dy * xhat, axis=0, keepdims=True)
        dx = _rms_bwd(dy, xhat, r, gv, 0.0)
        dx_ref[...] = dx
        df_ref[...] = (0.5 * dx).astype(BF16)

    tok = pl.BlockSpec((tm, D), lambda i: (i, 0))
    vec = pl.BlockSpec((1, D), lambda i: (0, 0))
    return pl.pallas_call(
        body, name=name, grid=(T // tm,),
        in_specs=[tok, tok, vec], out_specs=[tok, tok, vec, vec],
        out_shape=[jax.ShapeDtypeStruct((T, D), F32), jax.ShapeDtypeStruct((T, D), BF16),
                   jax.ShapeDtypeStruct((1, D), F32), jax.ShapeDtypeStruct((1, D), F32)],
        compiler_params=_params(1),
    )(x3, tgt, gain)


def _adamw_math(w, g, m, v):
    m2 = ADAM_B1 * m + (1.0 - ADAM_B1) * g
    v2 = ADAM_B2 * v + (1.0 - ADAM_B2) * (g * g)
    m_hat = m2 / (1.0 - ADAM_B1 ** ADAM_STEP)
    v_hat = v2 / (1.0 - ADAM_B2 ** ADAM_STEP)
    return -ADAM_LR * (m_hat / (jnp.sqrt(v_hat) + ADAM_EPS) + ADAM_WD * w), m2, v2


def adamw(w, g, m, v, name):
    R, C = w.shape

    def body(w_ref, g_ref, m_ref, v_ref, d_ref, nm_ref, nv_ref):
        d_ref[...], nm_ref[...], nv_ref[...] = _adamw_math(w_ref[...], g_ref[...], m_ref[...], v_ref[...])

    blk = pl.BlockSpec((R, C), lambda i: (0, 0))
    sds = jax.ShapeDtypeStruct((R, C), F32)
    return pl.pallas_call(
        body, name=name, grid=(1,), in_specs=[blk] * 4, out_specs=[blk] * 3, out_shape=[sds] * 3,
        compiler_params=_params(1),
    )(w, g, m, v)


def _place():
    x, y, c = lax.axis_index("x"), lax.axis_index("y"), lax.axis_index("c")
    other_chips = [(1 - x, y), (x, 1 - y), (1 - x, 1 - y)]
    return x, y, c, other_chips


def _exchange_plan(ins, out_shape, copies, n_copies):
    def descriptors(pins, pouts, psems):
        send, recv = psems
        return [pltpu.make_async_remote_copy(src_ref=s, dst_ref=d, send_sem=send.at[i], recv_sem=recv.at[i],
                                             device_id=dev, device_id_type=MESH)
                for i, (s, d, dev) in enumerate(copies(pins, pouts))]

    def start(pins, pouts, psems):
        for cp in descriptors(pins, pouts, psems):
            cp.start()

    def finish(pins, pouts, psems):
        for cp in descriptors(pins, pouts, psems):
            cp.wait()

    return CommPlan(tuple(ins), tuple(out_shape),
                    (pltpu.SemaphoreType.DMA((n_copies,)), pltpu.SemaphoreType.DMA((n_copies,))), start, finish)


def _combine(a, b):
    assert a.mid is None and b.mid is None
    ni, no, ns = len(a.ins), len(a.out_shape), len(a.sems)

    def start(pins, pouts, psems):
        a.start(pins[:ni], pouts[:no], psems[:ns])
        b.start(pins[ni:], pouts[no:], psems[ns:])

    def finish(pins, pouts, psems):
        a.finish(pins[:ni], pouts[:no], psems[:ns])
        b.finish(pins[ni:], pouts[no:], psems[ns:])

    return CommPlan(a.ins + b.ins, a.out_shape + b.out_shape, a.sems + b.sems, start, finish)


def gather_plan(shards):
    n = len(shards)

    def start(pins, pouts, psems):
        x, y, c, chips = _place()
        mine = 2 * x + y
        for w in range(n):
            for k, (px, py) in enumerate(chips):
                pltpu.make_async_remote_copy(
                    src_ref=pins[w].at[:, c], dst_ref=pouts[w].at[:, mine, c],
                    send_sem=psems[0].at[w, k], recv_sem=psems[1].at[w, k],
                    device_id=(px, py, c), device_id_type=MESH).start()

    def mid(pins, pouts, psems):
        x, y, c, chips = _place()
        for w in range(n):
            for k, (px, py) in enumerate(chips):
                landed = pouts[w].at[:, 2 * px + py, c]
                pltpu.make_async_remote_copy(
                    src_ref=landed, dst_ref=landed, send_sem=psems[0].at[w, k], recv_sem=psems[1].at[w, k],
                    device_id=(px, py, c), device_id_type=MESH).wait_recv()
                pltpu.make_async_remote_copy(
                    src_ref=landed, dst_ref=landed, send_sem=psems[2].at[w, k], recv_sem=psems[3].at[w, k],
                    device_id=(x, y, 1 - c), device_id_type=MESH).start()

    def finish(pins, pouts, psems):
        x, y, c, chips = _place()
        mine = 2 * x + y
        for w in range(n):
            for k, (px, py) in enumerate(chips):
                landed = pouts[w].at[:, 2 * px + py, 1 - c]
                cp = pltpu.make_async_remote_copy(
                    src_ref=landed, dst_ref=landed, send_sem=psems[2].at[w, k], recv_sem=psems[3].at[w, k],
                    device_id=(x, y, 1 - c), device_id_type=MESH)
                cp.wait_recv()
                cp.wait_send()
                pltpu.make_async_remote_copy(
                    src_ref=pins[w].at[:, c], dst_ref=pouts[w].at[:, mine, c],
                    send_sem=psems[0].at[w, k], recv_sem=psems[1].at[w, k],
                    device_id=(px, py, c), device_id_type=MESH).wait_send()

    return CommPlan(tuple(shards),
                    tuple(jax.ShapeDtypeStruct((s.shape[0], N_CHIPS) + s.shape[1:], s.dtype) for s in shards),
                    tuple(pltpu.SemaphoreType.DMA((n, 3)) for _ in range(4)), start, finish, mid)


def place_own(gathered, shard, name):
    L, _, Rh, C = shard.shape

    def body(chip_ref, g_ref, s_ref, o_ref):
        o_ref[...] = s_ref[...]

    return pl.pallas_call(
        body, name=name,
        grid_spec=pltpu.PrefetchScalarGridSpec(
            num_scalar_prefetch=1, grid=(L,),
            in_specs=[ANY, pl.BlockSpec((None, 2, Rh, C), lambda l, chip_ref: (l, 0, 0, 0))],
            out_specs=pl.BlockSpec((None, None, 2, Rh, C), lambda l, chip_ref: (l, chip_ref[0], 0, 0, 0))),
        out_shape=jax.ShapeDtypeStruct(gathered.shape, gathered.dtype),
        input_output_aliases={1: 0},
        compiler_params=_params(1),
    )(_chip_index(), gathered, shard)


def gather_weights(shards):
    n = len(shards)

    def body(*refs):
        ins, outs, bufs = refs[:n], refs[n:2 * n], refs[2 * n:3 * n]
        ld_sem, st_sem, ici_send, ici_recv, d2d_send, d2d_recv = refs[3 * n:]
        x, y, c, chips = _place()
        sibling = (x, y, 1 - c)
        mine = 2 * x + y
        loads = [pltpu.make_async_copy(ins[w], bufs[w], ld_sem.at[w]) for w in range(n)]
        for cp in loads:
            cp.start()
        stores, sends = [], []
        for w in range(n):
            loads[w].wait()
            cp = pltpu.make_async_copy(bufs[w], outs[w].at[:, mine], st_sem.at[w])
            cp.start()
            stores.append(cp)
            for k, (px, py) in enumerate(chips):
                cp = pltpu.make_async_remote_copy(
                    src_ref=bufs[w].at[:, c], dst_ref=outs[w].at[:, mine, c],
                    send_sem=ici_send.at[w, k], recv_sem=ici_recv.at[w, k],
                    device_id=(px, py, c), device_id_type=MESH)
                cp.start()
                sends.append(cp)
        for w in range(n):
            for k, (px, py) in enumerate(chips):
                landed = outs[w].at[:, 2 * px + py, c]
                pltpu.make_async_remote_copy(
                    src_ref=landed, dst_ref=landed, send_sem=ici_send.at[w, k], recv_sem=ici_recv.at[w, k],
                    device_id=(px, py, c), device_id_type=MESH).wait_recv()
                cp = pltpu.make_async_remote_copy(
                    src_ref=landed, dst_ref=landed, send_sem=d2d_send.at[w, k], recv_sem=d2d_recv.at[w, k],
                    device_id=sibling, device_id_type=MESH)
                cp.start()
                sends.append(cp)
        for w in range(n):
            for k, (px, py) in enumerate(chips):
                landed = outs[w].at[:, 2 * px + py, 1 - c]
                pltpu.make_async_remote_copy(
                    src_ref=landed, dst_ref=landed, send_sem=d2d_send.at[w, k], recv_sem=d2d_recv.at[w, k],
                    device_id=sibling, device_id_type=MESH).wait_recv()
        for cp in sends:
            cp.wait_send()
        for cp in stores:
            cp.wait()

    return pl.pallas_call(
        body, name="gather_weights",
        in_specs=[ANY] * n, out_specs=[ANY] * n,
        out_shape=[jax.ShapeDtypeStruct((s.shape[0], N_CHIPS) + s.shape[1:], s.dtype) for s in shards],
        scratch_shapes=[pltpu.VMEM(s.shape, s.dtype) for s in shards] +
                       [pltpu.SemaphoreType.DMA((n,)), pltpu.SemaphoreType.DMA((n,)),
                        pltpu.SemaphoreType.DMA((n, 3)), pltpu.SemaphoreType.DMA((n, 3)),
                        pltpu.SemaphoreType.DMA((n, 3)), pltpu.SemaphoreType.DMA((n, 3))],
        compiler_params=pltpu.CompilerParams(vmem_limit_bytes=VMEM_LIMIT),
    )(*shards)


def send_to_sibling_other_half(grads):
    def copies(ins, outs):
        x, y, c, _ = _place()
        return [(ins[w].at[:, :, 1 - c], outs[w], (x, y, 1 - c)) for w in range(len(grads))]

    return _exchange_plan(grads, [jax.ShapeDtypeStruct(g.shape[:2] + g.shape[3:], g.dtype) for g in grads], copies,
                          len(grads))


def _core_index():
    return jnp.reshape(lax.axis_index("c"), (1,)).astype(jnp.int32)


def _chip_index():
    return jnp.reshape(2 * lax.axis_index("x") + lax.axis_index("y"), (1,)).astype(jnp.int32)


def add_own_half(g, recv, name):
    L, J, _, Rh, C = g.shape

    def body(c_ref, g_ref, r_ref, o_ref):
        o_ref[...] = (g_ref[...] + r_ref[...]).astype(BF16)

    return pl.pallas_call(
        body, name=name,
        grid_spec=pltpu.PrefetchScalarGridSpec(
            num_scalar_prefetch=1, grid=(L, J),
            in_specs=[pl.BlockSpec((None, None, None, Rh, C), lambda l, j, c_ref: (l, j, c_ref[0], 0, 0)),
                      pl.BlockSpec((None, None, Rh, C), lambda l, j, c_ref: (l, j, 0, 0))],
            out_specs=pl.BlockSpec((None, None, Rh, C), lambda l, j, c_ref: (l, j, 0, 0))),
        out_shape=jax.ShapeDtypeStruct((L, J, Rh, C), BF16),
        compiler_params=_params(2),
    )(_core_index(), g, recv)


def exchange_between_chips(sums):
    def copies(ins, outs):
        x, y, c, chips = _place()
        return [(ins[w].at[:, 2 * px + py], outs[w].at[k], (px, py, c))
                for w in range(len(sums)) for k, (px, py) in enumerate(chips)]

    return _exchange_plan(sums, [jax.ShapeDtypeStruct((3, s.shape[0]) + s.shape[2:], s.dtype) for s in sums], copies,
                          3 * len(sums))


def sum_chips(own, recv, name):
    L, _, Rh, C = own.shape

    def body(chip_ref, o_ref, r_ref, out_ref):
        acc = o_ref[...].astype(F32)
        for k in range(3):
            acc = acc + r_ref[k].astype(F32)
        out_ref[...] = acc

    return pl.pallas_call(
        body, name=name,
        grid_spec=pltpu.PrefetchScalarGridSpec(
            num_scalar_prefetch=1, grid=(L,),
            in_specs=[pl.BlockSpec((None, None, Rh, C), lambda l, chip_ref: (l, chip_ref[0], 0, 0)),
                      pl.BlockSpec((3, None, Rh, C), lambda l, chip_ref: (0, l, 0, 0))],
            out_specs=pl.BlockSpec((None, Rh, C), lambda l, chip_ref: (l, 0, 0))),
        out_shape=jax.ShapeDtypeStruct((L, Rh, C), F32),
        compiler_params=_params(1),
    )(_chip_index(), own, recv)


def share_with_sibling(halves):
    def copies(ins, outs):
        x, y, c, _ = _place()
        return [(ins[w], outs[w], (x, y, 1 - c)) for w in range(len(halves))]

    return _exchange_plan(halves, [jax.ShapeDtypeStruct(h.shape, h.dtype) for h in halves], copies, len(halves))


def adamw_group(ws, ms, vs, own, sib, name, plan=None):
    L = len(ws)
    R, C = ws[0].shape
    Rh = R // 2
    tr = Rh // 2
    nr = Rh // tr

    def body(*refs):
        w, m, v = refs[:L], refs[L:2 * L], refs[2 * L:3 * L]
        own_ref, sib_ref = refs[3 * L], refs[3 * L + 1]
        outs = refs[3 * L + 2:]
        mine = pl.program_id(0) == lax.axis_index("c")
        for l in range(L):
            gv = jnp.where(mine, own_ref[l], sib_ref[l])
            d, m2, v2 = _adamw_math(w[l][...], gv, m[l][...], v[l][...])
            outs[4 * l][...] = d
            outs[4 * l + 1][...] = m2
            outs[4 * l + 2][...] = v2
            outs[4 * l + 3][...] = gv

    blk = pl.BlockSpec((tr, C), lambda h, r: (h * nr + r, 0))
    half = pl.BlockSpec((L, tr, C), lambda h, r: (0, r, 0))
    sds = jax.ShapeDtypeStruct((R, C), F32)
    return _call(body, plan, name=name, grid=(2, nr), in_specs=[blk] * (3 * L) + [half, half],
                 out_specs=[blk] * (4 * L), out_shape=[sds] * (4 * L), args=(*ws, *ms, *vs, own, sib))


def gather_small_plan(part):
    def copies(ins, outs):
        x, y, c, _ = _place()
        me = 4 * x + 2 * y + c
        return [(ins[0], outs[0].at[me], (x ^ ((k >> 2) & 1), y ^ ((k >> 1) & 1), c ^ (k & 1))) for k in range(1, N_DEV)]

    return _exchange_plan([part], [jax.ShapeDtypeStruct((N_DEV,) + part.shape, part.dtype)], copies, N_DEV - 1)


def sum_small(part, gathered, name):
    R, C = part.shape

    def body(p_ref, g_ref, o_ref):
        d = pl.program_id(0)
        me = 4 * lax.axis_index("x") + 2 * lax.axis_index("y") + lax.axis_index("c")
        val = jnp.where(d == me, p_ref[...], g_ref[...])

        @pl.when(d == 0)
        def _():
            o_ref[...] = val

        @pl.when(d > 0)
        def _():
            o_ref[...] += val

    return pl.pallas_call(
        body, name=name, grid=(N_DEV,),
        in_specs=[pl.BlockSpec((R, C), lambda d: (0, 0)), pl.BlockSpec((None, R, C), lambda d: (d, 0, 0))],
        out_specs=pl.BlockSpec((R, C), lambda d: (0, 0)),
        out_shape=jax.ShapeDtypeStruct((R, C), F32),
        compiler_params=_params(1),
    )(part, gathered)


SMALL = ("norm_ffn1", "norm_mix", "norm_ffn2", "norm_final", "ret_gn_gain", "pool_scale", "pool_w")
GROUPS = (("ffn1", ("ffn1_gate", "ffn1_up", "ffn1_down")), ("w_in", ("w_in",)), ("w_out", ("w_out",)),
          ("ffn2", ("ffn2_gate", "ffn2_up", "ffn2_down")))
TRANSPOSED = ("ffn1_gate", "ffn1_up", "ffn2_gate", "ffn2_up")
IN_W = 2 * QK_W + 3 * V_W
ORDER = ("norm_ffn1", "ffn1_gate", "ffn1_up", "ffn1_down", "norm_mix", "w_in", "ret_gn_gain", "pool_w", "pool_scale",
         "w_out", "norm_ffn2", "ffn2_gate", "ffn2_up", "ffn2_down", "norm_final")


def _pack_small(d):
    return jnp.concatenate([d[k].reshape(-1, 128) for k in SMALL], axis=0)


def _unpack_small(packed, shapes):
    out, row = {}, 0
    for k in SMALL:
        n = math.prod(shapes[k]) // 128
        out[k] = packed[row:row + n].reshape(shapes[k])
        row += n
    return out


def kernel(x, norm_ffn1, ffn1_gate, ffn1_up, ffn1_down, norm_mix, w_in, ret_gn_gain, pool_w, pool_scale, w_out, norm_ffn2, ffn2_gate, ffn2_up, ffn2_down, norm_final, loss_target, m_norm_ffn1, m_ffn1_gate, m_ffn1_up, m_ffn1_down, m_norm_mix, m_w_in, m_ret_gn_gain, m_pool_w, m_pool_scale, m_w_out, m_norm_ffn2, m_ffn2_gate, m_ffn2_up, m_ffn2_down, m_norm_final, v_norm_ffn1, v_ffn1_gate, v_ffn1_up, v_ffn1_down, v_norm_mix, v_w_in, v_ret_gn_gain, v_pool_w, v_pool_scale, v_w_out, v_norm_ffn2, v_ffn2_gate, v_ffn2_up, v_ffn2_down, v_norm_final):
    W = dict(norm_ffn1=norm_ffn1, ffn1_gate=ffn1_gate, ffn1_up=ffn1_up, ffn1_down=ffn1_down, norm_mix=norm_mix, w_in=w_in,
             ret_gn_gain=ret_gn_gain, pool_w=pool_w, pool_scale=pool_scale, w_out=w_out, norm_ffn2=norm_ffn2,
             ffn2_gate=ffn2_gate, ffn2_up=ffn2_up, ffn2_down=ffn2_down, norm_final=norm_final)
    M = dict(norm_ffn1=m_norm_ffn1, ffn1_gate=m_ffn1_gate, ffn1_up=m_ffn1_up, ffn1_down=m_ffn1_down, norm_mix=m_norm_mix,
             w_in=m_w_in, ret_gn_gain=m_ret_gn_gain, pool_w=m_pool_w, pool_scale=m_pool_scale, w_out=m_w_out,
             norm_ffn2=m_norm_ffn2, ffn2_gate=m_ffn2_gate, ffn2_up=m_ffn2_up, ffn2_down=m_ffn2_down, norm_final=m_norm_final)
    V = dict(norm_ffn1=v_norm_ffn1, ffn1_gate=v_ffn1_gate, ffn1_up=v_ffn1_up, ffn1_down=v_ffn1_down, norm_mix=v_norm_mix,
             w_in=v_w_in, ret_gn_gain=v_ret_gn_gain, pool_w=v_pool_w, pool_scale=v_pool_scale, w_out=v_w_out,
             norm_ffn2=v_norm_ffn2, ffn2_gate=v_ffn2_gate, ffn2_up=v_ffn2_up, ffn2_down=v_ffn2_down, norm_final=v_norm_final)

    n_seq, S, D = x.shape
    T = n_seq * S
    tm = min(512, T // 2)
    tm_fwd = min(1024, T // 2)
    tk = min(1024, T // 2)
    xf = x.reshape(T, D)
    tgt = loss_target.reshape(T, D)

    def local(d, k):
        a = d[k][0]
        return a.T if k in TRANSPOSED else a

    def to_out(k, a):
        return (a.T if k in TRANSPOSED else a)[None]

    loc_bf = []
    for _, members in GROUPS:
        st = jnp.stack([local(W, k).astype(BF16) for k in members])
        loc_bf.append(st.reshape(st.shape[0], 2, st.shape[1] // 2, st.shape[2]))
    FS = loc_bf[0].shape[2] * 2
    w3_1 = gather_weights(loc_bf[:1])[0].reshape(3, N_CHIPS, FS, D)
    g1 = norm_ffn1.reshape(1, D)
    gm = norm_mix.reshape(1, D)
    g3 = norm_ffn2.reshape(1, D)
    gf = norm_final.reshape(1, D)
    gn_gain = ret_gn_gain.reshape(1, V_W)
    pscale = pool_scale.reshape(1, V_W)
    pw = pool_w.reshape(len(POOL_WINDOWS), POOL_GC, POOL_GC)
    consts = mixer_constants(S)

    def halves5(g, lb):
        return g.reshape(lb.shape[0], N_CHIPS, 2, lb.shape[2], lb.shape[3])

    (x1, a1, b1, n1), landed = ffn_fwd(xf, g1, w3_1, tm_fwd, "ffn1_fwd", plan=gather_plan(loc_bf[1:]))
    gathered = [place_own(g, lb, "place_own_" + gn) for g, lb, (gn, _) in zip(landed, loc_bf[1:], GROUPS[1:])]
    win_full = gathered[0].reshape(N_CHIPS, D, IN_W // N_CHIPS)
    wout_full = gathered[1].reshape(D, D)
    w3_2 = gathered[2].reshape(3, N_CHIPS, FS, D)
    p = in_proj(x1, gm, win_full, tm, "in_proj")
    x2, m_s, o_s, st_s, pooled_s = mixer_fwd(p, x1, consts, gn_gain, pw, pscale, wout_full, n_seq, S, "mixer_fwd")
    x3, a2, b2, n3 = ffn_fwd(x2, g3, w3_2, tm_fwd, "ffn2_fwd")

    dx3, df2, lacc, dgf = loss_bwd(x3, tgt, gf, tm, "loss_bwd")
    dn3, da2, db2, h2 = ffn_bwd(df2, a2, b2, w3_2, tm_fwd, "ffn2_bwd")
    to_sib, to_chips, back = send_to_sibling_other_half, exchange_between_chips, share_with_sibling
    out_g, out_d, out_m, out_v = {}, {}, {}, {}

    def update(gi, own, sib, plan=None):
        members = GROUPS[gi][1]
        res = adamw_group([local(W, k) for k in members], [local(M, k) for k in members], [local(V, k) for k in members],
                          own, sib, "adamw_" + GROUPS[gi][0], plan=plan)
        res, landed = res if plan is not None else (res, None)
        for l, k in enumerate(members):
            out_d[k], out_m[k], out_v[k], out_g[k] = (to_out(k, r) for r in res[4 * l:4 * l + 4])
        return landed

    (gr_ffn2,) = ffn_wgrad(da2, db2, h2, n3, df2, tk, "ffn2_wgrad")
    g_ffn2 = halves5(gr_ffn2, loc_bf[3])
    (dx1, df1, dx2b, dp, n2, dg3, dgm, dgn, dps, dpw), (sb_ffn2,) = mixer_bwd(
        dx3, dn3, x2, g3, x1, p, o_s, st_s, pooled_s, consts, gm, gn_gain, pw, pscale, wout_full, win_full, n_seq, S,
        "mixer_bwd", plan=to_sib([g_ffn2]))
    cs_ffn2 = add_own_half(g_ffn2, sb_ffn2, "add_sibling_ffn2")
    (gr_wout,) = tn_shared(m_s, dx2b[None], tk, "dw_out")
    (gr_win,) = tn_shared(n2, dp, tk, "dw_in")
    g_mix = [halves5(gr_win, loc_bf[1]), halves5(gr_wout, loc_bf[2])]
    (dn1, da1, db1, h1), (pc_ffn2, sb_win, sb_wout) = ffn_bwd(
        df1, a1, b1, w3_1, tm_fwd, "ffn1_bwd", plan=_combine(to_chips([cs_ffn2]), to_sib(g_mix)))
    mh_ffn2 = sum_chips(cs_ffn2, pc_ffn2, "sum_chips_ffn2")
    cs_mix = [add_own_half(g, r, "add_sibling_" + gn) for g, r, gn in zip(g_mix, (sb_win, sb_wout), ("w_in", "w_out"))]
    (gr_ffn1,), (sh_ffn2, pc_win, pc_wout) = ffn_wgrad(
        da1, db1, h1, n1, df1, tk, "ffn1_wgrad", plan=_combine(back([mh_ffn2]), to_chips(cs_mix)))
    mh_mix = [sum_chips(cs, pc, "sum_chips_" + gn) for cs, pc, gn in zip(cs_mix, (pc_win, pc_wout), ("w_in", "w_out"))]
    g_ffn1 = halves5(gr_ffn1, loc_bf[0])
    (dx0, dg1), (sb_ffn1, sh_win, sh_wout) = rms_bwd(
        xf, dx1, dn1, g1, tm, "ffn1_rms_bwd", plan=_combine(to_sib([g_ffn1]), back(mh_mix)))
    cs_ffn1 = add_own_half(g_ffn1, sb_ffn1, "add_sibling_ffn1")

    small_part = dict(norm_ffn1=dg1, norm_mix=dgm, norm_ffn2=dg3, norm_final=dgf, ret_gn_gain=dgn, pool_scale=dps, pool_w=dpw)
    part = jnp.concatenate([_pack_small(small_part), lacc.reshape(-1, 128)], axis=0)
    n_small = part.shape[0] - lacc.size // 128
    pc_ffn1, parts = update(3, mh_ffn2, sh_ffn2, plan=_combine(to_chips([cs_ffn1]), gather_small_plan(part)))
    update(1, mh_mix[0], sh_win)
    update(2, mh_mix[1], sh_wout)
    summed = sum_small(part, parts, "sum_small")
    small_sum = summed[:n_small]
    loss = jnp.sum(summed[n_small:]) * (0.5 / D)
    mh_ffn1 = sum_chips(cs_ffn1, pc_ffn1, "sum_chips_ffn1")
    (sh_ffn1,) = _run_plan(back([mh_ffn1]), "ffn1_grads_back_to_sibling")
    update(0, mh_ffn1, sh_ffn1)

    shapes = {k: W[k].shape for k in SMALL}
    d_, m_, v_ = adamw(_pack_small(W), small_sum, _pack_small(M), _pack_small(V), "adamw_small")
    for dst, packed in ((out_g, small_sum), (out_d, d_), (out_m, m_), (out_v, v_)):
        dst.update(_unpack_small(packed, shapes))

    grad_x = dx0.reshape(n_seq, S, D)
    return (loss, grad_x, *[out_g[k] for k in ORDER], *[out_d[k] for k in ORDER],
            *[out_m[k] for k in ORDER], *[out_v[k] for k in ORDER])
```

```python
import functools
import math
from typing import Callable, NamedTuple, Optional

import jax
import jax.numpy as jnp
from jax import lax
from jax.experimental import pallas as pl
from jax.experimental.pallas import tpu as pltpu

F32 = jnp.float32
BF16 = jnp.bfloat16
MESH = pl.DeviceIdType.MESH

N_CHIPS = 4
N_DEV = 8
CHUNK = 64
RET_HEADS = 4
RET_DK = 64
RET_DV = 128
QK_W = RET_HEADS * RET_DK
V_W = RET_HEADS * RET_DV
POOL_WINDOWS = (2, 4, 8, 16)
POOL_GC = 128
ROPE_BASE = 10000.0
RMS_EPS = 1e-6
GN_EPS = 1e-5
ADAM_LR = 0.001
ADAM_B1 = 0.9
ADAM_B2 = 0.999
ADAM_EPS = 1e-08
ADAM_WD = 0.01
ADAM_STEP = 10
MXU_W = 256
RET_BLOCK = MXU_W
VMEM_LIMIT = 56 * 1024 * 1024

NN = (((1,), (0,)), ((), ()))
NT = (((1,), (1,)), ((), ()))
TN = (((0,), (0,)), ((), ()))


def _dot(a, b, dims=NN):
    return lax.dot_general(a, b, dims, preferred_element_type=F32)


def _sigmoid(x):
    return 0.5 * jnp.tanh(0.5 * x) + 0.5


def _params(n_grid):
    return pltpu.CompilerParams(dimension_semantics=("arbitrary",) * n_grid, vmem_limit_bytes=VMEM_LIMIT)


class CommPlan(NamedTuple):
    ins: tuple
    out_shape: tuple
    sems: tuple
    start: Callable
    finish: Callable
    mid: Optional[Callable] = None
    mid_at: float = 0.85


ANY = pl.BlockSpec(memory_space=pl.ANY)


def _call(body, plan, *, name, grid, in_specs, out_specs, out_shape, args, scratch_shapes=()):
    n_grid = len(grid)
    if plan is None:
        return pl.pallas_call(body, name=name, grid=grid, in_specs=in_specs, out_specs=out_specs, out_shape=out_shape,
                              scratch_shapes=scratch_shapes, compiler_params=_params(n_grid))(*args)
    n_in, n_out, n_sc = len(in_specs), len(out_specs), len(scratch_shapes)
    p_in, p_out = len(plan.ins), len(plan.out_shape)
    total = math.prod(grid)
    mid_step = min(total - 1, int(plan.mid_at * total))

    def riding(*refs):
        ins, pins = refs[:n_in], refs[n_in:n_in + p_in]
        o0 = n_in + p_in
        outs, pouts = refs[o0:o0 + n_out], refs[o0 + n_out:o0 + n_out + p_out]
        s0 = o0 + n_out + p_out
        scratch, psems = refs[s0:s0 + n_sc], refs[s0 + n_sc:]
        step = pl.program_id(0)
        for d in range(1, n_grid):
            step = step * grid[d] + pl.program_id(d)

        @pl.when(step == 0)
        def _():
            plan.start(pins, pouts, psems)

        body(*ins, *outs, *scratch)

        if plan.mid is not None:
            @pl.when(step == mid_step)
            def _():
                plan.mid(pins, pouts, psems)

        @pl.when(step == total - 1)
        def _():
            plan.finish(pins, pouts, psems)

    res = pl.pallas_call(
        riding, name=name, grid=grid, in_specs=list(in_specs) + [ANY] * p_in, out_specs=list(out_specs) + [ANY] * p_out,
        out_shape=list(out_shape) + list(plan.out_shape), scratch_shapes=list(scratch_shapes) + list(plan.sems),
        compiler_params=_params(n_grid))(*args, *plan.ins)
    return res[:n_out], res[n_out:]


def _run_plan(plan, name):
    p_in, p_out = len(plan.ins), len(plan.out_shape)

    def body(*refs):
        pins, pouts, psems = refs[:p_in], refs[p_in:p_in + p_out], refs[p_in + p_out:]
        plan.start(pins, pouts, psems)
        if plan.mid is not None:
            plan.mid(pins, pouts, psems)
        plan.finish(pins, pouts, psems)

    return pl.pallas_call(body, name=name, in_specs=[ANY] * p_in, out_specs=[ANY] * p_out,
                          out_shape=list(plan.out_shape), scratch_shapes=list(plan.sems))(*plan.ins)


def _rms_parts(x):
    r = lax.rsqrt(jnp.mean(x * x, axis=-1, keepdims=True) + RMS_EPS)
    return r, x * r


def _rms_bwd(dn, xhat, r, gain, dres):
    dxh = dn * gain
    return dres + r * (dxh - xhat * jnp.mean(dxh * xhat, axis=-1, keepdims=True))


def _w3_specs(FS, D):
    return [pl.BlockSpec((None, None, FS, D), functools.partial(lambda i, j, k: (k, j, 0, 0), k=k)) for k in range(3)]


def ffn_fwd(x, gain, w3, tm, name, plan=None):
    T, D = x.shape
    _, J, FS, _ = w3.shape

    def body(x_ref, g_ref, wg_ref, wu_ref, wd_ref, xo_ref, a_ref, b_ref, n_ref, acc):
        j = pl.program_id(1)

        @pl.when(j == 0)
        def _():
            _, xhat = _rms_parts(x_ref[...])
            n_ref[...] = (xhat * g_ref[...]).astype(BF16)
            acc[...] = jnp.zeros_like(acc)

        n = n_ref[...]
        a = _dot(n, wg_ref[...], NT)
        b = _dot(n, wu_ref[...], NT)
        a_ref[...] = a.astype(BF16)
        b_ref[...] = b.astype(BF16)
        h = (a * _sigmoid(a) * b).astype(BF16)
        acc[...] += _dot(h, wd_ref[...])

        @pl.when(j == J - 1)
        def _():
            xo_ref[...] = x_ref[...] + 0.5 * acc[...]

    return _call(
        body, plan, name=name, grid=(T // tm, J),
        in_specs=[pl.BlockSpec((tm, D), lambda i, j: (i, 0)),
                  pl.BlockSpec((1, D), lambda i, j: (0, 0))] + _w3_specs(FS, D),
        out_specs=[pl.BlockSpec((tm, D), lambda i, j: (i, 0)),
                   pl.BlockSpec((None, tm, FS), lambda i, j: (j, i, 0)),
                   pl.BlockSpec((None, tm, FS), lambda i, j: (j, i, 0)),
                   pl.BlockSpec((tm, D), lambda i, j: (i, 0))],
        out_shape=[jax.ShapeDtypeStruct((T, D), F32),
                   jax.ShapeDtypeStruct((J, T, FS), BF16),
                   jax.ShapeDtypeStruct((J, T, FS), BF16),
                   jax.ShapeDtypeStruct((T, D), BF16)],
        scratch_shapes=[pltpu.VMEM((tm, D), F32)],
        args=(x, gain, w3, w3, w3))


def ffn_bwd(df, a_s, b_s, w3, tm, name, plan=None):
    T, D = df.shape
    _, J, FS, _ = w3.shape

    chunks = [slice(s, min(s + MXU_W, FS)) for s in range(0, FS, MXU_W)]
    nc = len(chunks)

    def body(df_ref, a_ref, b_ref, wg_ref, wu_ref, wd_ref, dn_ref, da_ref, db_ref, h_ref):
        @pl.when(pl.program_id(1) == 0)
        def _():
            dn_ref[...] = jnp.zeros_like(dn_ref)

        def elementwise(dh, sl):
            a = a_ref[:, sl].astype(F32)
            b = b_ref[:, sl].astype(F32)
            sg = _sigmoid(a)
            si = a * sg
            da = (dh * b * (sg + si * (1.0 - sg))).astype(BF16)
            db = (dh * si).astype(BF16)
            da_ref[:, sl] = da
            db_ref[:, sl] = db
            h_ref[:, sl] = (si * b).astype(BF16)
            return da, db

        df = df_ref[...]
        dh, dab, acc = [None] * nc, [None] * nc, None
        for step in range(nc + 2):
            if step < nc:
                dh[step] = _dot(df, wd_ref[chunks[step], :], NT)
            if 1 <= step <= nc:
                dab[step - 1] = elementwise(dh[step - 1], chunks[step - 1])
            if step >= 2:
                sl = chunks[step - 2]
                part = _dot(dab[step - 2][0], wg_ref[sl, :]) + _dot(dab[step - 2][1], wu_ref[sl, :])
                acc = part if acc is None else acc + part
        dn_ref[...] += acc

    tok = pl.BlockSpec((tm, D), lambda i, j: (i, 0))
    sh = pl.BlockSpec((None, tm, FS), lambda i, j: (j, i, 0))
    return _call(
        body, plan, name=name, grid=(T // tm, J),
        in_specs=[tok, sh, sh] + _w3_specs(FS, D),
        out_specs=[tok, sh, sh, sh],
        out_shape=[jax.ShapeDtypeStruct((T, D), F32),
                   jax.ShapeDtypeStruct((J, T, FS), BF16),
                   jax.ShapeDtypeStruct((J, T, FS), BF16),
                   jax.ShapeDtypeStruct((J, T, FS), BF16)],
        args=(df, a_s, b_s, w3, w3, w3))


def rms_bwd(x, dout, dn, gain, tm, name, plan=None):
    T, D = x.shape

    def body(x_ref, do_ref, dn_ref, g_ref, dx_ref, dgain_ref):
        @pl.when(pl.program_id(0) == 0)
        def _():
            dgain_ref[...] = jnp.zeros_like(dgain_ref)

        r, xhat = _rms_parts(x_ref[...])
        dn = dn_ref[...]
        dgain_ref[...] += jnp.sum(dn * xhat, axis=0, keepdims=True)
        dx_ref[...] = _rms_bwd(dn, xhat, r, g_ref[...], do_ref[...])

    tok = pl.BlockSpec((tm, D), lambda i: (i, 0))
    vec = pl.BlockSpec((1, D), lambda i: (0, 0))
    return _call(
        body, plan, name=name, grid=(T // tm,), in_specs=[tok, tok, tok, vec], out_specs=[tok, vec],
        out_shape=[jax.ShapeDtypeStruct((T, D), F32), jax.ShapeDtypeStruct((1, D), F32)],
        args=(x, dout, dn, gain))


def ffn_wgrad(da, db, h, n, df, tk, name, plan=None):
    J, T, FS = da.shape
    D = n.shape[1]

    def body(da_ref, db_ref, h_ref, n_ref, df_ref, o_ref):
        @pl.when(pl.program_id(1) == 0)
        def _():
            o_ref[...] = jnp.zeros_like(o_ref)

        nv = n_ref[...]
        o_ref[0] += _dot(da_ref[...], nv, TN)
        o_ref[1] += _dot(db_ref[...], nv, TN)
        o_ref[2] += _dot(h_ref[...], df_ref[...], TN)

    sh = pl.BlockSpec((None, tk, FS), lambda j, k: (j, k, 0))
    tok = pl.BlockSpec((tk, D), lambda j, k: (k, 0))
    return _call(
        body, plan, name=name, grid=(J, T // tk),
        in_specs=[sh, sh, sh, tok, tok],
        out_specs=[pl.BlockSpec((3, None, FS, D), lambda j, k: (0, j, 0, 0))],
        out_shape=[jax.ShapeDtypeStruct((3, J, FS, D), F32)],
        args=(da, db, h, n, df))


def wgrad_one(a, b, tk, name, plan=None):
    J, T, FS = a.shape
    D = b.shape[1]

    def body(a_ref, b_ref, o_ref):
        @pl.when(pl.program_id(1) == 0)
        def _():
            o_ref[...] = jnp.zeros_like(o_ref)

        o_ref[...] += _dot(a_ref[...], b_ref[...], TN)

    return _call(
        body, plan, name=name, grid=(J, T // tk),
        in_specs=[pl.BlockSpec((None, tk, FS), lambda j, k: (j, k, 0)), pl.BlockSpec((tk, D), lambda j, k: (k, 0))],
        out_specs=[pl.BlockSpec((None, FS, D), lambda j, k: (j, 0, 0))],
        out_shape=[jax.ShapeDtypeStruct((J, FS, D), F32)],
        args=(a, b))


def tn_shared(a, b, tk, name, plan=None):
    T, M = a.shape
    J, _, N = b.shape

    def body(a_ref, b_ref, o_ref):
        @pl.when(pl.program_id(0) == 0)
        def _():
            o_ref[...] = jnp.zeros_like(o_ref)

        a_t = a_ref[...].astype(BF16).T
        for j in range(J):
            o_ref[j] += _dot(a_t, b_ref[j].astype(BF16))

    return _call(
        body, plan, name=name, grid=(T // tk,),
        in_specs=[pl.BlockSpec((tk, M), lambda k: (k, 0)), pl.BlockSpec((J, tk, N), lambda k: (0, k, 0))],
        out_specs=[pl.BlockSpec((J, M, N), lambda k: (0, 0, 0))],
        out_shape=[jax.ShapeDtypeStruct((J, M, N), F32)],
        args=(a, b))


def in_proj(x, gain, win, tm, name):
    T, D = x.shape
    J, _, W = win.shape

    def body(x_ref, g_ref, w_ref, p_ref):
        _, xhat = _rms_parts(x_ref[...])
        n = (xhat * g_ref[...]).astype(BF16)
        for j in range(J):
            p_ref[j] = _dot(n, w_ref[j])

    return pl.pallas_call(
        body, name=name, grid=(T // tm,),
        in_specs=[pl.BlockSpec((tm, D), lambda i: (i, 0)),
                  pl.BlockSpec((1, D), lambda i: (0, 0)),
                  pl.BlockSpec((J, D, W), lambda i: (0, 0, 0))],
        out_specs=pl.BlockSpec((J, tm, W), lambda i: (0, i, 0)),
        out_shape=jax.ShapeDtypeStruct((J, T, W), F32),
        compiler_params=_params(1),
    )(x, gain, win)


def mixer_constants(S):
    B = RET_BLOCK
    half = RET_DK // 2
    freqs = ROPE_BASE ** (-jnp.arange(half, dtype=F32) * 2.0 / RET_DK)
    ang = jnp.arange(S, dtype=F32)[:, None] * freqs[None, :]
    cos = jnp.tile(jnp.cos(ang), (1, 2 * RET_HEADS))
    sin = jnp.tile(jnp.concatenate([-jnp.sin(ang), jnp.sin(ang)], axis=1), (1, RET_HEADS))
    gamma = 1.0 - 2.0 ** (-5.0 - jnp.arange(RET_HEADS, dtype=F32))
    log_g = jnp.log(gamma)
    idx = jnp.arange(B, dtype=F32)
    ci = jnp.arange(B) // CHUNK
    dist = jnp.abs(idx[:, None] - idx[None, :])
    m_intra = jnp.exp(log_g[:, None, None] * dist[None]) * (ci[None, :] <= ci[:, None])[None].astype(F32)
    lg_lane = jnp.repeat(log_g, RET_DK)
    d_q = jnp.exp(lg_lane[None, :] * (idx[:, None] + 1.0))
    d_k = jnp.exp(lg_lane[None, :] * (B - 1.0 - idx[:, None]))
    d_s = jnp.broadcast_to(jnp.exp(lg_lane * B)[:, None], (QK_W, V_W))
    bm = (jnp.arange(QK_W)[:, None] // RET_DK == jnp.arange(V_W)[None, :] // RET_DV).astype(F32)
    win = jnp.repeat(jnp.array(POOL_WINDOWS, F32), POOL_GC)
    invc = 1.0 / jnp.minimum(jnp.arange(S, dtype=F32)[:, None] + 1.0, win[None, :])
    return dict(cos=cos, sin=sin, m=m_intra, dq=d_q, dk=d_k, ds=d_s, bm=bm, invc=invc)


def _swap_halves(x):
    w = x.shape[1]
    lane = lax.broadcasted_iota(jnp.int32, x.shape, 1)
    return jnp.where((lane % RET_DK) < RET_DK // 2, pltpu.roll(x, w - RET_DK // 2, 1), pltpu.roll(x, RET_DK // 2, 1))


def _rot(x, cos, sin):
    return x * cos + _swap_halves(x) * sin


def _rot_t(d, cos, sin):
    return d * cos + _swap_halves(d * sin)


def _lane_groups(parts):
    return jnp.concatenate([p[:, POOL_GC * g:POOL_GC * (g + 1)] for g, p in enumerate(parts)], axis=1)


def _head_mask(shape, h):
    lane = lax.broadcasted_iota(jnp.int32, shape, 1)
    return (lane // RET_DK) == h


def _group_norm(o):
    yh, rs = [], []
    for h in range(RET_HEADS):
        oh = o[:, RET_DV * h:RET_DV * (h + 1)]
        xc = oh - jnp.mean(oh, axis=-1, keepdims=True)
        r = lax.rsqrt(jnp.mean(xc * xc, axis=-1, keepdims=True) + GN_EPS)
        yh.append(xc * r)
        rs.append(r)
    return yh, rs


def mixer_fwd(p, x1, consts, gn_gain, pool_w, pool_scale, wout, n_seq, S, name):
    _, T, _ = p.shape
    D = x1.shape[1]
    B = RET_BLOCK
    nb = S // B

    def body(p_ref, x1_ref, cos_ref, sin_ref, invc_ref, m_ref, dq_ref, dk_ref, ds_ref, bm_ref,
             gain_ref, pw_ref, sc_ref, wout_ref,
             x2_ref, m_out, o_out, st_out, pooled_out, state, prev_u):
        blk = pl.program_id(1)

        @pl.when(blk == 0)
        def _():
            state[...] = jnp.zeros_like(state)
            prev_u[...] = jnp.zeros_like(prev_u)

        qk = p_ref[0]
        v = p_ref[1]
        g = p_ref[2]
        u = p_ref[3]
        cos = cos_ref[...]
        sin = sin_ref[...]
        qr = _rot(qk[:, :QK_W], cos, sin) * (RET_DK ** -0.5)
        kr = _rot(qk[:, QK_W:], cos, sin)
        qb = qr.astype(BF16)
        kb = kr.astype(BF16)
        vb = v.astype(BF16)
        st = state[...]
        st_out[...] = st
        cross = _dot((qr * dq_ref[...]).astype(BF16), st.astype(BF16))
        o_parts = []
        for h in range(RET_HEADS):
            qm = jnp.where(_head_mask(qb.shape, h), qb, jnp.zeros_like(qb))
            sc = (_dot(qm, kb, NT) * m_ref[h]).astype(BF16)
            o_parts.append(_dot(sc, vb[:, RET_DV * h:RET_DV * (h + 1)]) + cross[:, RET_DV * h:RET_DV * (h + 1)])
        o = jnp.concatenate(o_parts, axis=1)
        o_out[...] = o
        kv = _dot((kr * dk_ref[...]).astype(BF16), vb, TN)
        state[...] = st * ds_ref[...] + kv * bm_ref[...]

        yh, _ = _group_norm(o)
        r = g * _sigmoid(g) * (jnp.concatenate(yh, axis=1) * gain_ref[...])

        ext = jnp.concatenate([prev_u[...], u], axis=0)
        sums = []
        run = ext
        for k in (1, 2, 4, 8):
            run = run + pltpu.roll(run, k, 0)
            sums.append(run[B:, :])
        prev_u[...] = u
        pooled = (_lane_groups(sums) * invc_ref[...] - u).astype(BF16)
        pooled_out[...] = pooled
        yp = [_dot(pooled[:, POOL_GC * gi:POOL_GC * (gi + 1)], pw_ref[gi].astype(BF16)) for gi in range(len(POOL_WINDOWS))]
        s = jnp.concatenate(yp, axis=1) * sc_ref[...]
        m = jnp.concatenate([r, s], axis=1).astype(BF16)
        m_out[...] = m
        x2_ref[...] = x1_ref[...] + _dot(m, wout_ref[...])

    tokmap = lambda b, k: (b * nb + k, 0)
    seqmap = lambda b, k: (k, 0)
    const2 = lambda b, k: (0, 0)
    const3 = lambda b, k: (0, 0, 0)
    return pl.pallas_call(
        body, name=name, grid=(n_seq, nb),
        in_specs=[pl.BlockSpec((4, B, V_W), lambda b, k: (0, b * nb + k, 0)),
                  pl.BlockSpec((B, D), tokmap),
                  pl.BlockSpec((B, QK_W), seqmap), pl.BlockSpec((B, QK_W), seqmap), pl.BlockSpec((B, V_W), seqmap),
                  pl.BlockSpec((RET_HEADS, B, B), const3),
                  pl.BlockSpec((B, QK_W), const2), pl.BlockSpec((B, QK_W), const2),
                  pl.BlockSpec((QK_W, V_W), const2), pl.BlockSpec((QK_W, V_W), const2),
                  pl.BlockSpec((1, V_W), const2), pl.BlockSpec((4, POOL_GC, POOL_GC), const3),
                  pl.BlockSpec((1, V_W), const2), pl.BlockSpec((D, D), const2)],
        out_specs=[pl.BlockSpec((B, D), tokmap), pl.BlockSpec((B, D), tokmap), pl.BlockSpec((B, V_W), tokmap),
                   pl.BlockSpec((None, QK_W, V_W), lambda b, k: (b * nb + k, 0, 0)),
                   pl.BlockSpec((B, V_W), tokmap)],
        out_shape=[jax.ShapeDtypeStruct((T, D), F32), jax.ShapeDtypeStruct((T, D), BF16),
                   jax.ShapeDtypeStruct((T, V_W), F32), jax.ShapeDtypeStruct((T // B, QK_W, V_W), F32),
                   jax.ShapeDtypeStruct((T, V_W), BF16)],
        scratch_shapes=[pltpu.VMEM((QK_W, V_W), F32), pltpu.VMEM((B, V_W), F32)],
        compiler_params=_params(2),
    )(p, x1, consts["cos"], consts["sin"], consts["invc"], consts["m"], consts["dq"], consts["dk"],
      consts["ds"], consts["bm"], gn_gain, pool_w, pool_scale, wout)


def mixer_bwd(dx3, dn3, x2, g3, x1, p, o_s, st_s, pooled_s, consts, gmix, gn_gain, pool_w, pool_scale, wout, win,
              n_seq, S, name, plan=None):
    T, D = x1.shape
    B = RET_BLOCK
    nb = S // B
    NG = len(POOL_WINDOWS)

    def body(dx3_ref, dn3_ref, x2_ref, g3_ref, x1_ref, p_ref, o_ref, st_ref, pooled_ref, cos_ref, sin_ref, invc_ref,
             m_ref, dq_ref, dk_ref, ds_ref, bm_ref, gmix_ref, gain_ref, pw_ref, sc_ref, wout_ref, win_ref,
             dx1_ref, df1_ref, dx2_ref, dp_ref, n2_ref, dg3_ref, dgmix_ref, dgain_ref, dscale_ref, dpw_ref,
             gstate, next_e):
        b = pl.program_id(0)
        blk = pl.program_id(1)

        @pl.when(jnp.logical_and(b == 0, blk == 0))
        def _():
            dg3_ref[...] = jnp.zeros_like(dg3_ref)
            dgmix_ref[...] = jnp.zeros_like(dgmix_ref)
            dgain_ref[...] = jnp.zeros_like(dgain_ref)
            dscale_ref[...] = jnp.zeros_like(dscale_ref)
            dpw_ref[...] = jnp.zeros_like(dpw_ref)

        @pl.when(blk == 0)
        def _():
            gstate[...] = jnp.zeros_like(gstate)
            next_e[...] = jnp.zeros_like(next_e)

        r2, xhat2 = _rms_parts(x2_ref[...])
        dn3 = dn3_ref[...]
        dg3_ref[...] += jnp.sum(dn3 * xhat2, axis=0, keepdims=True)
        dx2v = _rms_bwd(dn3, xhat2, r2, g3_ref[...], dx3_ref[...])
        dx2b = dx2v.astype(BF16)
        dx2_ref[...] = dx2b
        dm = _dot(dx2b, wout_ref[...], NT)
        dr = dm[:, :V_W]
        dsv = dm[:, V_W:]

        pooled = pooled_ref[...]
        scale = sc_ref[...]
        dyp = (dsv * scale).astype(BF16)
        yp, dpl = [], []
        for gi in range(NG):
            sl = slice(POOL_GC * gi, POOL_GC * (gi + 1))
            pwb = pw_ref[gi].astype(BF16)
            yp.append(_dot(pooled[:, sl], pwb))
            dpw_ref[gi] += _dot(pooled[:, sl], dyp[:, sl], TN)
            dpl.append(_dot(dyp[:, sl], pwb, NT))
        dscale_ref[...] += jnp.sum(dsv * jnp.concatenate(yp, axis=1), axis=0, keepdims=True)
        dpooled = jnp.concatenate(dpl, axis=1)
        e = dpooled * invc_ref[...]
        ext = jnp.concatenate([e, next_e[...]], axis=0)
        sums = []
        run = ext
        for k in (1, 2, 4, 8):
            run = run + pltpu.roll(run, 2 * B - k, 0)
            sums.append(run[:B, :])
        next_e[...] = e
        du = _lane_groups(sums) - dpooled

        qk = p_ref[0]
        v = p_ref[1]
        g = p_ref[2]
        sg = _sigmoid(g)
        si = g * sg
        yh, rs = _group_norm(o_ref[...])
        yhat = jnp.concatenate(yh, axis=1)
        gain = gain_ref[...]
        dg = dr * (yhat * gain) * (sg * (1.0 + g * (1.0 - sg)))
        dy = dr * si
        dgain_ref[...] += jnp.sum(dy * yhat, axis=0, keepdims=True)
        dyh = dy * gain
        do_parts = []
        for h in range(RET_HEADS):
            sl = slice(RET_DV * h, RET_DV * (h + 1))
            dh_ = dyh[:, sl]
            m1 = jnp.mean(dh_, axis=-1, keepdims=True)
            m2 = jnp.mean(dh_ * yh[h], axis=-1, keepdims=True)
            do_parts.append(rs[h] * (dh_ - m1 - yh[h] * m2))
        dob = jnp.concatenate(do_parts, axis=1).astype(BF16)

        cos = cos_ref[...]
        sin = sin_ref[...]
        qr = _rot(qk[:, :QK_W], cos, sin) * (RET_DK ** -0.5)
        kr = _rot(qk[:, QK_W:], cos, sin)
        qb = qr.astype(BF16)
        kb = kr.astype(BF16)
        vb = v.astype(BF16)
        dqd = dq_ref[...]
        dkd = dk_ref[...]
        stb = st_ref[...].astype(BF16)
        gs = gstate[...]
        gb = gs.astype(BF16)
        dqs = _dot(dob, stb, NT) * dqd
        dkr = _dot(vb, gb, NT) * dkd
        dv_cross = _dot((kr * dkd).astype(BF16), gb)
        ds_cross = _dot((qr * dqd).astype(BF16), dob, TN) * bm_ref[...]
        gstate[...] = ds_cross + gs * ds_ref[...]
        dv_parts = []
        for h in range(RET_HEADS):
            sl = slice(RET_DV * h, RET_DV * (h + 1))
            hm = _head_mask(qb.shape, h)
            qm = jnp.where(hm, qb, jnp.zeros_like(qb))
            mh = m_ref[h]
            sc = (_dot(qm, kb, NT) * mh).astype(BF16)
            dsc = (_dot(dob[:, sl], vb[:, sl], NT) * mh).astype(BF16)
            dqs = dqs + jnp.where(hm, _dot(dsc, kb), 0.0)
            dkr = dkr + jnp.where(hm, _dot(dsc, qb, TN), 0.0)
            dv_parts.append(_dot(sc, dob[:, sl], TN) + dv_cross[:, sl])
        dq = _rot_t(dqs * (RET_DK ** -0.5), cos, sin)
        dk = _rot_t(dkr, cos, sin)
        dp = [jnp.concatenate([dq, dk], axis=1).astype(BF16), jnp.concatenate(dv_parts, axis=1).astype(BF16),
              dg.astype(BF16), du.astype(BF16)]
        dn = jnp.zeros((B, D), F32)
        for jj in range(4):
            dp_ref[jj] = dp[jj]
            dn = dn + _dot(dp[jj], win_ref[jj], NT)

        x1v = x1_ref[...]
        r, xhat = _rms_parts(x1v)
        gm = gmix_ref[...]
        n2_ref[...] = (xhat * gm).astype(BF16)
        dgmix_ref[...] += jnp.sum(dn * xhat, axis=0, keepdims=True)
        dx1 = _rms_bwd(dn, xhat, r, gm, dx2v)
        dx1_ref[...] = dx1
        df1_ref[...] = (0.5 * dx1).astype(BF16)

    rev = lambda b, k: (b * nb + (nb - 1 - k), 0)
    seqrev = lambda b, k: (nb - 1 - k, 0)
    const2 = lambda b, k: (0, 0)
    const3 = lambda b, k: (0, 0, 0)
    return _call(
        body, plan, name=name, grid=(n_seq, nb),
        in_specs=[pl.BlockSpec((B, D), rev), pl.BlockSpec((B, D), rev), pl.BlockSpec((B, D), rev),
                  pl.BlockSpec((1, D), const2), pl.BlockSpec((B, D), rev),
                  pl.BlockSpec((4, B, V_W), lambda b, k: (0, b * nb + (nb - 1 - k), 0)),
                  pl.BlockSpec((B, V_W), rev),
                  pl.BlockSpec((None, QK_W, V_W), lambda b, k: (b * nb + (nb - 1 - k), 0, 0)),
                  pl.BlockSpec((B, V_W), rev),
                  pl.BlockSpec((B, QK_W), seqrev), pl.BlockSpec((B, QK_W), seqrev), pl.BlockSpec((B, V_W), seqrev),
                  pl.BlockSpec((RET_HEADS, B, B), const3),
                  pl.BlockSpec((B, QK_W), const2), pl.BlockSpec((B, QK_W), const2),
                  pl.BlockSpec((QK_W, V_W), const2), pl.BlockSpec((QK_W, V_W), const2),
                  pl.BlockSpec((1, D), const2), pl.BlockSpec((1, V_W), const2),
                  pl.BlockSpec((NG, POOL_GC, POOL_GC), const3), pl.BlockSpec((1, V_W), const2),
                  pl.BlockSpec((D, D), const2), pl.BlockSpec((4, D, V_W), const3)],
        out_specs=[pl.BlockSpec((B, D), rev), pl.BlockSpec((B, D), rev), pl.BlockSpec((B, D), rev),
                   pl.BlockSpec((4, B, V_W), lambda b, k: (0, b * nb + (nb - 1 - k), 0)),
                   pl.BlockSpec((B, D), rev),
                   pl.BlockSpec((1, D), const2), pl.BlockSpec((1, D), const2), pl.BlockSpec((1, V_W), const2),
                   pl.BlockSpec((1, V_W), const2), pl.BlockSpec((NG, POOL_GC, POOL_GC), const3)],
        out_shape=[jax.ShapeDtypeStruct((T, D), F32), jax.ShapeDtypeStruct((T, D), BF16),
                   jax.ShapeDtypeStruct((T, D), BF16), jax.ShapeDtypeStruct((4, T, V_W), BF16),
                   jax.ShapeDtypeStruct((T, D), BF16),
                   jax.ShapeDtypeStruct((1, D), F32), jax.ShapeDtypeStruct((1, D), F32), jax.ShapeDtypeStruct((1, V_W), F32),
                   jax.ShapeDtypeStruct((1, V_W), F32), jax.ShapeDtypeStruct((NG, POOL_GC, POOL_GC), F32)],
        scratch_shapes=[pltpu.VMEM((QK_W, V_W), F32), pltpu.VMEM((B, V_W), F32)],
        args=(dx3, dn3, x2, g3, x1, p, o_s, st_s, pooled_s, consts["cos"], consts["sin"], consts["invc"], consts["m"], consts["dq"],
              consts["dk"], consts["ds"], consts["bm"], gmix, gn_gain, pool_w, pool_scale, wout, win))


def loss_bwd(x3, tgt, gain, tm, name):
    T, D = x3.shape

    def body(x_ref, t_ref, g_ref, dx_ref, df_ref, lacc_ref, dg_ref):
        @pl.when(pl.program_id(0) == 0)
        def _():
            lacc_ref[...] = jnp.zeros_like(lacc_ref)
            dg_ref[...] = jnp.zeros_like(dg_ref)

        r, xhat = _rms_parts(x_ref[...])
        gv = g_ref[...]
        err = xhat * gv - t_ref[...]
        lacc_ref[...] += jnp.sum(err * err, axis=0, keepdims=True)
        dy = err * (1.0 / D)
        dg_ref[...] += jnp.sum(dy * xhat, axis=0, keepdims=True)
        dx = _rms_bwd(dy, xhat, r, gv, 0.0)
        dx_ref[...] = dx
        df_ref[...] = (0.5 * dx).astype(BF16)

    tok = pl.BlockSpec((tm, D), lambda i: (i, 0))
    vec = pl.BlockSpec((1, D), lambda i: (0, 0))
    return pl.pallas_call(
        body, name=name, grid=(T // tm,),
        in_specs=[tok, tok, vec], out_specs=[tok, tok, vec, vec],
        out_shape=[jax.ShapeDtypeStruct((T, D), F32), jax.ShapeDtypeStruct((T, D), BF16),
                   jax.ShapeDtypeStruct((1, D), F32), jax.ShapeDtypeStruct((1, D), F32)],
        compiler_params=_params(1),
    )(x3, tgt, gain)


def _adamw_math(w, g, m, v):
    m2 = ADAM_B1 * m + (1.0 - ADAM_B1) * g
    v2 = ADAM_B2 * v + (1.0 - ADAM_B2) * (g * g)
    m_hat = m2 / (1.0 - ADAM_B1 ** ADAM_STEP)
    v_hat = v2 / (1.0 - ADAM_B2 ** ADAM_STEP)
    return -ADAM_LR * (m_hat / (jnp.sqrt(v_hat) + ADAM_EPS) + ADAM_WD * w), m2, v2


def adamw(w, g, m, v, name):
    R, C = w.shape

    def body(w_ref, g_ref, m_ref, v_ref, d_ref, nm_ref, nv_ref):
        d_ref[...], nm_ref[...], nv_ref[...] = _adamw_math(w_ref[...], g_ref[...], m_ref[...], v_ref[...])

    blk = pl.BlockSpec((R, C), lambda i: (0, 0))
    sds = jax.ShapeDtypeStruct((R, C), F32)
    return pl.pallas_call(
        body, name=name, grid=(1,), in_specs=[blk] * 4, out_specs=[blk] * 3, out_shape=[sds] * 3,
        compiler_params=_params(1),
    )(w, g, m, v)


def _place():
    x, y, c = lax.axis_index("x"), lax.axis_index("y"), lax.axis_index("c")
    other_chips = [(1 - x, y), (x, 1 - y), (1 - x, 1 - y)]
    return x, y, c, other_chips


def _exchange_plan(ins, out_shape, copies, n_copies):
    def descriptors(pins, pouts, psems):
        send, recv = psems
        return [pltpu.make_async_remote_copy(src_ref=s, dst_ref=d, send_sem=send.at[i], recv_sem=recv.at[i],
                                             device_id=dev, device_id_type=MESH)
                for i, (s, d, dev) in enumerate(copies(pins, pouts))]

    def start(pins, pouts, psems):
        for cp in descriptors(pins, pouts, psems):
            cp.start()

    def finish(pins, pouts, psems):
        for cp in descriptors(pins, pouts, psems):
            cp.wait()

    return CommPlan(tuple(ins), tuple(out_shape),
                    (pltpu.SemaphoreType.DMA((n_copies,)), pltpu.SemaphoreType.DMA((n_copies,))), start, finish)


def _combine(a, b):
    assert a.mid is None and b.mid is None
    ni, no, ns = len(a.ins), len(a.out_shape), len(a.sems)

    def start(pins, pouts, psems):
        a.start(pins[:ni], pouts[:no], psems[:ns])
        b.start(pins[ni:], pouts[no:], psems[ns:])

    def finish(pins, pouts, psems):
        a.finish(pins[:ni], pouts[:no], psems[:ns])
        b.finish(pins[ni:], pouts[no:], psems[ns:])

    return CommPlan(a.ins + b.ins, a.out_shape + b.out_shape, a.sems + b.sems, start, finish)


def gather_plan(shards):
    n = len(shards)

    def start(pins, pouts, psems):
        x, y, c, chips = _place()
        mine = 2 * x + y
        for w in range(n):
            for k, (px, py) in enumerate(chips):
                pltpu.make_async_remote_copy(
                    src_ref=pins[w].at[:, c], dst_ref=pouts[w].at[:, mine, c],
                    send_sem=psems[0].at[w, k], recv_sem=psems[1].at[w, k],
                    device_id=(px, py, c), device_id_type=MESH).start()

    def mid(pins, pouts, psems):
        x, y, c, chips = _place()
        for w in range(n):
            for k, (px, py) in enumerate(chips):
                landed = pouts[w].at[:, 2 * px + py, c]
                pltpu.make_async_remote_copy(
                    src_ref=landed, dst_ref=landed, send_sem=psems[0].at[w, k], recv_sem=psems[1].at[w, k],
                    device_id=(px, py, c), device_id_type=MESH).wait_recv()
                pltpu.make_async_remote_copy(
                    src_ref=landed, dst_ref=landed, send_sem=psems[2].at[w, k], recv_sem=psems[3].at[w, k],
                    device_id=(x, y, 1 - c), device_id_type=MESH).start()

    def finish(pins, pouts, psems):
        x, y, c, chips = _place()
        mine = 2 * x + y
        for w in range(n):
            for k, (px, py) in enumerate(chips):
                landed = pouts[w].at[:, 2 * px + py, 1 - c]
                cp = pltpu.make_async_remote_copy(
                    src_ref=landed, dst_ref=landed, send_sem=psems[2].at[w, k], recv_sem=psems[3].at[w, k],
                    device_id=(x, y, 1 - c), device_id_type=MESH)
                cp.wait_recv()
                cp.wait_send()
                pltpu.make_async_remote_copy(
                    src_ref=pins[w].at[:, c], dst_ref=pouts[w].at[:, mine, c],
                    send_sem=psems[0].at[w, k], recv_sem=psems[1].at[w, k],
                    device_id=(px, py, c), device_id_type=MESH).wait_send()

    return CommPlan(tuple(shards),
                    tuple(jax.ShapeDtypeStruct((s.shape[0], N_CHIPS) + s.shape[1:], s.dtype) for s in shards),
                    tuple(pltpu.SemaphoreType.DMA((n, 3)) for _ in range(4)), start, finish, mid)


def place_own(gathered, shard, name):
    L, _, Rh, C = shard.shape

    def body(chip_ref, g_ref, s_ref, o_ref):
        o_ref[...] = s_ref[...]

    return pl.pallas_call(
        body, name=name,
        grid_spec=pltpu.PrefetchScalarGridSpec(
            num_scalar_prefetch=1, grid=(L,),
            in_specs=[ANY, pl.BlockSpec((None, 2, Rh, C), lambda l, chip_ref: (l, 0, 0, 0))],
            out_specs=pl.BlockSpec((None, None, 2, Rh, C), lambda l, chip_ref: (l, chip_ref[0], 0, 0, 0))),
        out_shape=jax.ShapeDtypeStruct(gathered.shape, gathered.dtype),
        input_output_aliases={1: 0},
        compiler_params=_params(1),
    )(_chip_index(), gathered, shard)


def gather_weights(shards):
    n = len(shards)

    def body(*refs):
        ins, outs, bufs = refs[:n], refs[n:2 * n], refs[2 * n:3 * n]
        ld_sem, st_sem, ici_send, ici_recv, d2d_send, d2d_recv = refs[3 * n:]
        x, y, c, chips = _place()
        sibling = (x, y, 1 - c)
        mine = 2 * x + y
        loads = [pltpu.make_async_copy(ins[w], bufs[w], ld_sem.at[w]) for w in range(n)]
        for cp in loads:
            cp.start()
        stores, sends = [], []
        for w in range(n):
            loads[w].wait()
            cp = pltpu.make_async_copy(bufs[w], outs[w].at[:, mine], st_sem.at[w])
            cp.start()
            stores.append(cp)
            for k, (px, py) in enumerate(chips):
                cp = pltpu.make_async_remote_copy(
                    src_ref=bufs[w].at[:, c], dst_ref=outs[w].at[:, mine, c],
                    send_sem=ici_send.at[w, k], recv_sem=ici_recv.at[w, k],
                    device_id=(px, py, c), device_id_type=MESH)
                cp.start()
                sends.append(cp)
        for w in range(n):
            for k, (px, py) in enumerate(chips):
                landed = outs[w].at[:, 2 * px + py, c]
                pltpu.make_async_remote_copy(
                    src_ref=landed, dst_ref=landed, send_sem=ici_send.at[w, k], recv_sem=ici_recv.at[w, k],
                    device_id=(px, py, c), device_id_type=MESH).wait_recv()
                cp = pltpu.make_async_remote_copy(
                    src_ref=landed, dst_ref=landed, send_sem=d2d_send.at[w, k], recv_sem=d2d_recv.at[w, k],
                    device_id=sibling, device_id_type=MESH)
                cp.start()
                sends.append(cp)
        for w in range(n):
            for k, (px, py) in enumerate(chips):
                landed = outs[w].at[:, 2 * px + py, 1 - c]
                pltpu.make_async_remote_copy(
                    src_ref=landed, dst_ref=landed, send_sem=d2d_send.at[w, k], recv_sem=d2d_recv.at[w, k],
                    device_id=sibling, device_id_type=MESH).wait_recv()
        for cp in sends:
            cp.wait_send()
        for cp in stores:
            cp.wait()

    return pl.pallas_call(
        body, name="gather_weights",
        in_specs=[ANY] * n, out_specs=[ANY] * n,
        out_shape=[jax.ShapeDtypeStruct((s.shape[0], N_CHIPS) + s.shape[1:], s.dtype) for s in shards],
        scratch_shapes=[pltpu.VMEM(s.shape, s.dtype) for s in shards] +
                       [pltpu.SemaphoreType.DMA((n,)), pltpu.SemaphoreType.DMA((n,)),
                        pltpu.SemaphoreType.DMA((n, 3)), pltpu.SemaphoreType.DMA((n, 3)),
                        pltpu.SemaphoreType.DMA((n, 3)), pltpu.SemaphoreType.DMA((n, 3))],
        compiler_params=pltpu.CompilerParams(vmem_limit_bytes=VMEM_LIMIT),
    )(*shards)


def send_to_sibling_other_half(grads):
    def copies(ins, outs):
        x, y, c, _ = _place()
        return [(ins[w].at[:, :, 1 - c], outs[w], (x, y, 1 - c)) for w in range(len(grads))]

    return _exchange_plan(grads, [jax.ShapeDtypeStruct(g.shape[:2] + g.shape[3:], g.dtype) for g in grads], copies,
                          len(grads))


def _core_index():
    return jnp.reshape(lax.axis_index("c"), (1,)).astype(jnp.int32)


def _chip_index():
    return jnp.reshape(2 * lax.axis_index("x") + lax.axis_index("y"), (1,)).astype(jnp.int32)


def add_own_half(g, recv, name):
    L, J, _, Rh, C = g.shape

    def body(c_ref, g_ref, r_ref, o_ref):
        o_ref[...] = (g_ref[...] + r_ref[...]).astype(BF16)

    return pl.pallas_call(
        body, name=name,
        grid_spec=pltpu.PrefetchScalarGridSpec(
            num_scalar_prefetch=1, grid=(L, J),
            in_specs=[pl.BlockSpec((None, None, None, Rh, C), lambda l, j, c_ref: (l, j, c_ref[0], 0, 0)),
                      pl.BlockSpec((None, None, Rh, C), lambda l, j, c_ref: (l, j, 0, 0))],
            out_specs=pl.BlockSpec((None, None, Rh, C), lambda l, j, c_ref: (l, j, 0, 0))),
        out_shape=jax.ShapeDtypeStruct((L, J, Rh, C), BF16),
        compiler_params=_params(2),
    )(_core_index(), g, recv)


def exchange_between_chips(sums):
    def copies(ins, outs):
        x, y, c, chips = _place()
        return [(ins[w].at[:, 2 * px + py], outs[w].at[k], (px, py, c))
                for w in range(len(sums)) for k, (px, py) in enumerate(chips)]

    return _exchange_plan(sums, [jax.ShapeDtypeStruct((3, s.shape[0]) + s.shape[2:], s.dtype) for s in sums], copies,
                          3 * len(sums))


def sum_chips(own, recv, name):
    L, _, Rh, C = own.shape

    def body(chip_ref, o_ref, r_ref, out_ref):
        acc = o_ref[...].astype(F32)
        for k in range(3):
            acc = acc + r_ref[k].astype(F32)
        out_ref[...] = acc

    return pl.pallas_call(
        body, name=name,
        grid_spec=pltpu.PrefetchScalarGridSpec(
            num_scalar_prefetch=1, grid=(L,),
            in_specs=[pl.BlockSpec((None, None, Rh, C), lambda l, chip_ref: (l, chip_ref[0], 0, 0)),
                      pl.BlockSpec((3, None, Rh, C), lambda l, chip_ref: (0, l, 0, 0))],
            out_specs=pl.BlockSpec((None, Rh, C), lambda l, chip_ref: (l, 0, 0))),
        out_shape=jax.ShapeDtypeStruct((L, Rh, C), F32),
        compiler_params=_params(1),
    )(_chip_index(), own, recv)


def share_with_sibling(halves):
    def copies(ins, outs):
        x, y, c, _ = _place()
        return [(ins[w], outs[w], (x, y, 1 - c)) for w in range(len(halves))]

    return _exchange_plan(halves, [jax.ShapeDtypeStruct(h.shape, h.dtype) for h in halves], copies, len(halves))


def adamw_group(ws, ms, vs, own, sib, name, plan=None):
    L = len(ws)
    R, C = ws[0].shape
    Rh = R // 2
    tr = Rh // 2
    nr = Rh // tr

    def body(*refs):
        w, m, v = refs[:L], refs[L:2 * L], refs[2 * L:3 * L]
        own_ref, sib_ref = refs[3 * L], refs[3 * L + 1]
        outs = refs[3 * L + 2:]
        mine = pl.program_id(0) == lax.axis_index("c")
        for l in range(L):
            gv = jnp.where(mine, own_ref[l], sib_ref[l])
            d, m2, v2 = _adamw_math(w[l][...], gv, m[l][...], v[l][...])
            outs[4 * l][...] = d
            outs[4 * l + 1][...] = m2
            outs[4 * l + 2][...] = v2
            outs[4 * l + 3][...] = gv

    blk = pl.BlockSpec((tr, C), lambda h, r: (h * nr + r, 0))
    half = pl.BlockSpec((L, tr, C), lambda h, r: (0, r, 0))
    sds = jax.ShapeDtypeStruct((R, C), F32)
    return _call(body, plan, name=name, grid=(2, nr), in_specs=[blk] * (3 * L) + [half, half],
                 out_specs=[blk] * (4 * L), out_shape=[sds] * (4 * L), args=(*ws, *ms, *vs, own, sib))


def gather_small_plan(part):
    def copies(ins, outs):
        x, y, c, _ = _place()
        me = 4 * x + 2 * y + c
        return [(ins[0], outs[0].at[me], (x ^ ((k >> 2) & 1), y ^ ((k >> 1) & 1), c ^ (k & 1))) for k in range(1, N_DEV)]

    return _exchange_plan([part], [jax.ShapeDtypeStruct((N_DEV,) + part.shape, part.dtype)], copies, N_DEV - 1)


def sum_small(part, gathered, name):
    R, C = part.shape

    def body(p_ref, g_ref, o_ref):
        d = pl.program_id(0)
        me = 4 * lax.axis_index("x") + 2 * lax.axis_index("y") + lax.axis_index("c")
        val = jnp.where(d == me, p_ref[...], g_ref[...])

        @pl.when(d == 0)
        def _():
            o_ref[...] = val

        @pl.when(d > 0)
        def _():
            o_ref[...] += val

    return pl.pallas_call(
        body, name=name, grid=(N_DEV,),
        in_specs=[pl.BlockSpec((R, C), lambda d: (0, 0)), pl.BlockSpec((None, R, C), lambda d: (d, 0, 0))],
        out_specs=pl.BlockSpec((R, C), lambda d: (0, 0)),
        out_shape=jax.ShapeDtypeStruct((R, C), F32),
        compiler_params=_params(1),
    )(part, gathered)


SMALL = ("norm_ffn1", "norm_mix", "norm_ffn2", "norm_final", "ret_gn_gain", "pool_scale", "pool_w")
GROUPS = (("ffn1", ("ffn1_gate", "ffn1_up", "ffn1_down")), ("w_in", ("w_in",)), ("w_out", ("w_out",)),
          ("ffn2", ("ffn2_gate", "ffn2_up", "ffn2_down")))
TRANSPOSED = ("ffn1_gate", "ffn1_up", "ffn2_gate", "ffn2_up")
IN_W = 2 * QK_W + 3 * V_W
ORDER = ("norm_ffn1", "ffn1_gate", "ffn1_up", "ffn1_down", "norm_mix", "w_in", "ret_gn_gain", "pool_w", "pool_scale",
         "w_out", "norm_ffn2", "ffn2_gate", "ffn2_up", "ffn2_down", "norm_final")


def _pack_small(d):
    return jnp.concatenate([d[k].reshape(-1, 128) for k in SMALL], axis=0)


def _unpack_small(packed, shapes):
    out, row = {}, 0
    for k in SMALL:
        n = math.prod(shapes[k]) // 128
        out[k] = packed[row:row + n].reshape(shapes[k])
        row += n
    return out


def kernel(x, norm_ffn1, ffn1_gate, ffn1_up, ffn1_down, norm_mix, w_in, ret_gn_gain, pool_w, pool_scale, w_out, norm_ffn2, ffn2_gate, ffn2_up, ffn2_down, norm_final, loss_target, m_norm_ffn1, m_ffn1_gate, m_ffn1_up, m_ffn1_down, m_norm_mix, m_w_in, m_ret_gn_gain, m_pool_w, m_pool_scale, m_w_out, m_norm_ffn2, m_ffn2_gate, m_ffn2_up, m_ffn2_down, m_norm_final, v_norm_ffn1, v_ffn1_gate, v_ffn1_up, v_ffn1_down, v_norm_mix, v_w_in, v_ret_gn_gain, v_pool_w, v_pool_scale, v_w_out, v_norm_ffn2, v_ffn2_gate, v_ffn2_up, v_ffn2_down, v_norm_final):
    W = dict(norm_ffn1=norm_ffn1, ffn1_gate=ffn1_gate, ffn1_up=ffn1_up, ffn1_down=ffn1_down, norm_mix=norm_mix, w_in=w_in,
             ret_gn_gain=ret_gn_gain, pool_w=pool_w, pool_scale=pool_scale, w_out=w_out, norm_ffn2=norm_ffn2,
             ffn2_gate=ffn2_gate, ffn2_up=ffn2_up, ffn2_down=ffn2_down, norm_final=norm_final)
    M = dict(norm_ffn1=m_norm_ffn1, ffn1_gate=m_ffn1_gate, ffn1_up=m_ffn1_up, ffn1_down=m_ffn1_down, norm_mix=m_norm_mix,
             w_in=m_w_in, ret_gn_gain=m_ret_gn_gain, pool_w=m_pool_w, pool_scale=m_pool_scale, w_out=m_w_out,
             norm_ffn2=m_norm_ffn2, ffn2_gate=m_ffn2_gate, ffn2_up=m_ffn2_up, ffn2_down=m_ffn2_down, norm_final=m_norm_final)
    V = dict(norm_ffn1=v_norm_ffn1, ffn1_gate=v_ffn1_gate, ffn1_up=v_ffn1_up, ffn1_down=v_ffn1_down, norm_mix=v_norm_mix,
             w_in=v_w_in, ret_gn_gain=v_ret_gn_gain, pool_w=v_pool_w, pool_scale=v_pool_scale, w_out=v_w_out,
             norm_ffn2=v_norm_ffn2, ffn2_gate=v_ffn2_gate, ffn2_up=v_ffn2_up, ffn2_down=v_ffn2_down, norm_final=v_norm_final)

    n_seq, S, D = x.shape
    T = n_seq * S
    tm = min(512, T // 2)
    tm_fwd = min(1024, T // 2)
    tk = min(1024, T // 2)
    xf = x.reshape(T, D)
    tgt = loss_target.reshape(T, D)

    def local(d, k):
        a = d[k][0]
        return a.T if k in TRANSPOSED else a

    def to_out(k, a):
        return (a.T if k in TRANSPOSED else a)[None]

    loc_bf = []
    for _, members in GROUPS:
        st = jnp.stack([local(W, k).astype(BF16) for k in members])
        loc_bf.append(st.reshape(st.shape[0], 2, st.shape[1] // 2, st.shape[2]))
    FS = loc_bf[0].shape[2] * 2
    w3_1 = gather_weights(loc_bf[:1])[0].reshape(3, N_CHIPS, FS, D)
    g1 = norm_ffn1.reshape(1, D)
    gm = norm_mix.reshape(1, D)
    g3 = norm_ffn2.reshape(1, D)
    gf = norm_final.reshape(1, D)
    gn_gain = ret_gn_gain.reshape(1, V_W)
    pscale = pool_scale.reshape(1, V_W)
    pw = pool_w.reshape(len(POOL_WINDOWS), POOL_GC, POOL_GC)
    consts = mixer_constants(S)

    def halves5(g, lb):
        return g.reshape(lb.shape[0], N_CHIPS, 2, lb.shape[2], lb.shape[3])

    (x1, a1, b1, n1), landed = ffn_fwd(xf, g1, w3_1, tm_fwd, "ffn1_fwd", plan=gather_plan(loc_bf[1:]))
    gathered = [place_own(g, lb, "place_own_" + gn) for g, lb, (gn, _) in zip(landed, loc_bf[1:], GROUPS[1:])]
    win_full = gathered[0].reshape(N_CHIPS, D, IN_W // N_CHIPS)
    wout_full = gathered[1].reshape(D, D)
    w3_2 = gathered[2].reshape(3, N_CHIPS, FS, D)
    p = in_proj(x1, gm, win_full, tm, "in_proj")
    x2, m_s, o_s, st_s, pooled_s = mixer_fwd(p, x1, consts, gn_gain, pw, pscale, wout_full, n_seq, S, "mixer_fwd")
    x3, a2, b2, n3 = ffn_fwd(x2, g3, w3_2, tm_fwd, "ffn2_fwd")

    dx3, df2, lacc, dgf = loss_bwd(x3, tgt, gf, tm, "loss_bwd")
    dn3, da2, db2, h2 = ffn_bwd(df2, a2, b2, w3_2, tm_fwd, "ffn2_bwd")
    to_sib, to_chips, back = send_to_sibling_other_half, exchange_between_chips, share_with_sibling
    out_g, out_d, out_m, out_v = {}, {}, {}, {}

    def update(members, own, sib, name):
        res = adamw_group([local(W, k) for k in members], [local(M, k) for k in members], [local(V, k) for k in members],
                          own, sib, "adamw_" + name)
        for l, k in enumerate(members):
            out_d[k], out_m[k], out_v[k], out_g[k] = (to_out(k, r) for r in res[4 * l:4 * l + 4])

    (gr_ffn2,) = ffn_wgrad(da2, db2, h2, n3, df2, tk, "ffn2_wgrad")
    g_ffn2 = halves5(gr_ffn2, loc_bf[3])
    (dx1, df1, dx2b, dp, n2, dg3, dgm, dgn, dps, dpw), (sb_ffn2,) = mixer_bwd(
        dx3, dn3, x2, g3, x1, p, o_s, st_s, pooled_s, consts, gm, gn_gain, pw, pscale, wout_full, win_full, n_seq, S,
        "mixer_bwd", plan=to_sib([g_ffn2]))
    cs_ffn2 = add_own_half(g_ffn2, sb_ffn2, "add_sibling_ffn2")
    (gr_wout,) = tn_shared(m_s, dx2b[None], tk, "dw_out")
    (gr_win,) = tn_shared(n2, dp, tk, "dw_in")
    g_mix = [halves5(gr_win, loc_bf[1]), halves5(gr_wout, loc_bf[2])]
    (dn1, da1, db1, h1), (pc_ffn2, sb_win, sb_wout) = ffn_bwd(
        df1, a1, b1, w3_1, tm_fwd, "ffn1_bwd", plan=_combine(to_chips([cs_ffn2]), to_sib(g_mix)))
    dx0, dg1 = rms_bwd(xf, dx1, dn1, g1, tm, "ffn1_rms_bwd")
    mh_ffn2 = sum_chips(cs_ffn2, pc_ffn2, "sum_chips_ffn2")
    cs_mix = [add_own_half(g, r, "add_sibling_" + gn) for g, r, gn in zip(g_mix, (sb_win, sb_wout), ("w_in", "w_out"))]
    def one(g):
        return g.reshape((1, N_CHIPS, 2) + loc_bf[0].shape[2:])

    small_part = dict(norm_ffn1=dg1, norm_mix=dgm, norm_ffn2=dg3, norm_final=dgf, ret_gn_gain=dgn, pool_scale=dps, pool_w=dpw)
    part = jnp.concatenate([_pack_small(small_part), lacc.reshape(-1, 128)], axis=0)
    n_small = part.shape[0] - lacc.size // 128
    (gr_g,), (sh_ffn2, pc_win, pc_wout) = wgrad_one(
        da1, n1, tk, "ffn1_wgrad_gate", plan=_combine(back([mh_ffn2]), to_chips(cs_mix)))
    mh_mix = [sum_chips(cs, pc, "sum_chips_" + gn) for cs, pc, gn in zip(cs_mix, (pc_win, pc_wout), ("w_in", "w_out"))]
    (gr_u,), (sb_g,) = wgrad_one(db1, n1, tk, "ffn1_wgrad_up", plan=to_sib([one(gr_g)]))
    cs_g = add_own_half(one(gr_g), sb_g, "add_sibling_ffn1_gate")
    (gr_d,), (pc_g, sb_u) = wgrad_one(
        h1, df1, tk, "ffn1_wgrad_down", plan=_combine(to_chips([cs_g]), to_sib([one(gr_u)])))
    mh_g = sum_chips(cs_g, pc_g, "sum_chips_ffn1_gate")
    cs_u = add_own_half(one(gr_u), sb_u, "add_sibling_ffn1_up")
    sh_g, sh_win, sh_wout, pc_u, sb_d = _run_plan(
        _combine(_combine(back([mh_g] + mh_mix), to_chips([cs_u])), to_sib([one(gr_d)])), "ffn1_grads_stage_a")
    mh_u = sum_chips(cs_u, pc_u, "sum_chips_ffn1_up")
    cs_d = add_own_half(one(gr_d), sb_d, "add_sibling_ffn1_down")
    sh_u, pc_d, parts = _run_plan(
        _combine(_combine(back([mh_u]), to_chips([cs_d])), gather_small_plan(part)), "ffn1_grads_stage_b")
    mh_d = sum_chips(cs_d, pc_d, "sum_chips_ffn1_down")
    (sh_d,) = _run_plan(back([mh_d]), "ffn1_grads_stage_c")
    summed = sum_small(part, parts, "sum_small")
    small_sum = summed[:n_small]
    loss = jnp.sum(summed[n_small:]) * (0.5 / D)
    for k, own, sib in zip(GROUPS[0][1], (mh_g, mh_u, mh_d), (sh_g, sh_u, sh_d)):
        update((k,), own, sib, k)
    update(GROUPS[1][1], mh_mix[0], sh_win, "w_in")
    update(GROUPS[2][1], mh_mix[1], sh_wout, "w_out")
    update(GROUPS[3][1], mh_ffn2, sh_ffn2, "ffn2")

    shapes = {k: W[k].shape for k in SMALL}
    d_, m_, v_ = adamw(_pack_small(W), small_sum, _pack_small(M), _pack_small(V), "adamw_small")
    for dst, packed in ((out_g, small_sum), (out_d, d_), (out_m, m_), (out_v, v_)):
        dst.update(_unpack_small(packed, shapes))

    grad_x = dx0.reshape(n_seq, S, D)
    return (loss, grad_x, *[out_g[k] for k in ORDER], *[out_d[k] for k in ORDER],
            *[out_m[k] for k in ORDER], *[out_v[k] for k in ORDER])
```

```python
import functools
import math
from typing import Callable, NamedTuple, Optional

import jax
import jax.numpy as jnp
from jax import lax
from jax.experimental import pallas as pl
from jax.experimental.pallas import tpu as pltpu

F32 = jnp.float32
BF16 = jnp.bfloat16
MESH = pl.DeviceIdType.MESH

N_CHIPS = 4
N_DEV = 8
CHUNK = 64
RET_HEADS = 4
RET_DK = 64
RET_DV = 128
QK_W = RET_HEADS * RET_DK
V_W = RET_HEADS * RET_DV
POOL_WINDOWS = (2, 4, 8, 16)
POOL_GC = 128
ROPE_BASE = 10000.0
RMS_EPS = 1e-6
GN_EPS = 1e-5
ADAM_LR = 0.001
ADAM_B1 = 0.9
ADAM_B2 = 0.999
ADAM_EPS = 1e-08
ADAM_WD = 0.01
ADAM_STEP = 10
MXU_W = 256
RET_BLOCK = MXU_W
VMEM_LIMIT = 56 * 1024 * 1024

NN = (((1,), (0,)), ((), ()))
NT = (((1,), (1,)), ((), ()))
TN = (((0,), (0,)), ((), ()))


def _dot(a, b, dims=NN):
    return lax.dot_general(a, b, dims, preferred_element_type=F32)


def _sigmoid(x):
    return 0.5 * jnp.tanh(0.5 * x) + 0.5


def _params(n_grid):
    return pltpu.CompilerParams(dimension_semantics=("arbitrary",) * n_grid, vmem_limit_bytes=VMEM_LIMIT)


class CommPlan(NamedTuple):
    ins: tuple
    out_shape: tuple
    sems: tuple
    start: Callable
    finish: Callable
    mid: Optional[Callable] = None
    mid_at: float = 0.85


ANY = pl.BlockSpec(memory_space=pl.ANY)


def _call(body, plan, *, name, grid, in_specs, out_specs, out_shape, args, scratch_shapes=()):
    n_grid = len(grid)
    if plan is None:
        return pl.pallas_call(body, name=name, grid=grid, in_specs=in_specs, out_specs=out_specs, out_shape=out_shape,
                              scratch_shapes=scratch_shapes, compiler_params=_params(n_grid))(*args)
    n_in, n_out, n_sc = len(in_specs), len(out_specs), len(scratch_shapes)
    p_in, p_out = len(plan.ins), len(plan.out_shape)
    total = math.prod(grid)
    mid_step = min(total - 1, int(plan.mid_at * total))

    def riding(*refs):
        ins, pins = refs[:n_in], refs[n_in:n_in + p_in]
        o0 = n_in + p_in
        outs, pouts = refs[o0:o0 + n_out], refs[o0 + n_out:o0 + n_out + p_out]
        s0 = o0 + n_out + p_out
        scratch, psems = refs[s0:s0 + n_sc], refs[s0 + n_sc:]
        step = pl.program_id(0)
        for d in range(1, n_grid):
            step = step * grid[d] + pl.program_id(d)

        @pl.when(step == 0)
        def _():
            plan.start(pins, pouts, psems)

        body(*ins, *outs, *scratch)

        if plan.mid is not None:
            @pl.when(step == mid_step)
            def _():
                plan.mid(pins, pouts, psems)

        @pl.when(step == total - 1)
        def _():
            plan.finish(pins, pouts, psems)

    res = pl.pallas_call(
        riding, name=name, grid=grid, in_specs=list(in_specs) + [ANY] * p_in, out_specs=list(out_specs) + [ANY] * p_out,
        out_shape=list(out_shape) + list(plan.out_shape), scratch_shapes=list(scratch_shapes) + list(plan.sems),
        compiler_params=_params(n_grid))(*args, *plan.ins)
    return res[:n_out], res[n_out:]


def _run_plan(plan, name):
    p_in, p_out = len(plan.ins), len(plan.out_shape)

    def body(*refs):
        pins, pouts, psems = refs[:p_in], refs[p_in:p_in + p_out], refs[p_in + p_out:]
        plan.start(pins, pouts, psems)
        if plan.mid is not None:
            plan.mid(pins, pouts, psems)
        plan.finish(pins, pouts, psems)

    return pl.pallas_call(body, name=name, in_specs=[ANY] * p_in, out_specs=[ANY] * p_out,
                          out_shape=list(plan.out_shape), scratch_shapes=list(plan.sems))(*plan.ins)


def _rms_parts(x):
    r = lax.rsqrt(jnp.mean(x * x, axis=-1, keepdims=True) + RMS_EPS)
    return r, x * r


def _rms_bwd(dn, xhat, r, gain, dres):
    dxh = dn * gain
    return dres + r * (dxh - xhat * jnp.mean(dxh * xhat, axis=-1, keepdims=True))


def _w3_specs(FS, D):
    return [pl.BlockSpec((None, None, FS, D), functools.partial(lambda i, j, k: (k, j, 0, 0), k=k)) for k in range(3)]


def ffn_fwd(x, gain, w3, tm, name, plan=None):
    T, D = x.shape
    _, J, FS, _ = w3.shape

    def body(x_ref, g_ref, wg_ref, wu_ref, wd_ref, xo_ref, a_ref, b_ref, n_ref, acc):
        j = pl.program_id(1)

        @pl.when(j == 0)
        def _():
            _, xhat = _rms_parts(x_ref[...])
            n_ref[...] = (xhat * g_ref[...]).astype(BF16)
            acc[...] = jnp.zeros_like(acc)

        n = n_ref[...]
        a = _dot(n, wg_ref[...], NT)
        b = _dot(n, wu_ref[...], NT)
        a_ref[...] = a.astype(BF16)
        b_ref[...] = b.astype(BF16)
        h = (a * _sigmoid(a) * b).astype(BF16)
        acc[...] += _dot(h, wd_ref[...])

        @pl.when(j == J - 1)
        def _():
            xo_ref[...] = x_ref[...] + 0.5 * acc[...]

    return _call(
        body, plan, name=name, grid=(T // tm, J),
        in_specs=[pl.BlockSpec((tm, D), lambda i, j: (i, 0)),
                  pl.BlockSpec((1, D), lambda i, j: (0, 0))] + _w3_specs(FS, D),
        out_specs=[pl.BlockSpec((tm, D), lambda i, j: (i, 0)),
                   pl.BlockSpec((None, tm, FS), lambda i, j: (j, i, 0)),
                   pl.BlockSpec((None, tm, FS), lambda i, j: (j, i, 0)),
                   pl.BlockSpec((tm, D), lambda i, j: (i, 0))],
        out_shape=[jax.ShapeDtypeStruct((T, D), F32),
                   jax.ShapeDtypeStruct((J, T, FS), BF16),
                   jax.ShapeDtypeStruct((J, T, FS), BF16),
                   jax.ShapeDtypeStruct((T, D), BF16)],
        scratch_shapes=[pltpu.VMEM((tm, D), F32)],
        args=(x, gain, w3, w3, w3))


def ffn_bwd(df, a_s, b_s, w3, tm, name, plan=None):
    T, D = df.shape
    _, J, FS, _ = w3.shape

    chunks = [slice(s, min(s + MXU_W, FS)) for s in range(0, FS, MXU_W)]
    nc = len(chunks)

    def body(df_ref, a_ref, b_ref, wg_ref, wu_ref, wd_ref, dn_ref, da_ref, db_ref, h_ref):
        @pl.when(pl.program_id(1) == 0)
        def _():
            dn_ref[...] = jnp.zeros_like(dn_ref)

        def elementwise(dh, sl):
            a = a_ref[:, sl].astype(F32)
            b = b_ref[:, sl].astype(F32)
            sg = _sigmoid(a)
            si = a * sg
            da = (dh * b * (sg + si * (1.0 - sg))).astype(BF16)
            db = (dh * si).astype(BF16)
            da_ref[:, sl] = da
            db_ref[:, sl] = db
            h_ref[:, sl] = (si * b).astype(BF16)
            return da, db

        df = df_ref[...]
        dh, dab, acc = [None] * nc, [None] * nc, None
        for step in range(nc + 2):
            if step < nc:
                dh[step] = _dot(df, wd_ref[chunks[step], :], NT)
            if 1 <= step <= nc:
                dab[step - 1] = elementwise(dh[step - 1], chunks[step - 1])
            if step >= 2:
                sl = chunks[step - 2]
                part = _dot(dab[step - 2][0], wg_ref[sl, :]) + _dot(dab[step - 2][1], wu_ref[sl, :])
                acc = part if acc is None else acc + part
        dn_ref[...] += acc

    tok = pl.BlockSpec((tm, D), lambda i, j: (i, 0))
    sh = pl.BlockSpec((None, tm, FS), lambda i, j: (j, i, 0))
    return _call(
        body, plan, name=name, grid=(T // tm, J),
        in_specs=[tok, sh, sh] + _w3_specs(FS, D),
        out_specs=[tok, sh, sh, sh],
        out_shape=[jax.ShapeDtypeStruct((T, D), F32),
                   jax.ShapeDtypeStruct((J, T, FS), BF16),
                   jax.ShapeDtypeStruct((J, T, FS), BF16),
                   jax.ShapeDtypeStruct((J, T, FS), BF16)],
        args=(df, a_s, b_s, w3, w3, w3))


def rms_bwd(x, dout, dn, gain, tm, name, plan=None):
    T, D = x.shape

    def body(x_ref, do_ref, dn_ref, g_ref, dx_ref, dgain_ref):
        @pl.when(pl.program_id(0) == 0)
        def _():
            dgain_ref[...] = jnp.zeros_like(dgain_ref)

        r, xhat = _rms_parts(x_ref[...])
        dn = dn_ref[...]
        dgain_ref[...] += jnp.sum(dn * xhat, axis=0, keepdims=True)
        dx_ref[...] = _rms_bwd(dn, xhat, r, g_ref[...], do_ref[...])

    tok = pl.BlockSpec((tm, D), lambda i: (i, 0))
    vec = pl.BlockSpec((1, D), lambda i: (0, 0))
    return _call(
        body, plan, name=name, grid=(T // tm,), in_specs=[tok, tok, tok, vec], out_specs=[tok, vec],
        out_shape=[jax.ShapeDtypeStruct((T, D), F32), jax.ShapeDtypeStruct((1, D), F32)],
        args=(x, dout, dn, gain))


def ffn_wgrad(da, db, h, n, df, tk, name, plan=None):
    J, T, FS = da.shape
    D = n.shape[1]

    def body(da_ref, db_ref, h_ref, n_ref, df_ref, o_ref):
        @pl.when(pl.program_id(1) == 0)
        def _():
            o_ref[...] = jnp.zeros_like(o_ref)

        nv = n_ref[...]
        o_ref[0] += _dot(da_ref[...], nv, TN)
        o_ref[1] += _dot(db_ref[...], nv, TN)
        o_ref[2] += _dot(h_ref[...], df_ref[...], TN)

    sh = pl.BlockSpec((None, tk, FS), lambda j, k: (j, k, 0))
    tok = pl.BlockSpec((tk, D), lambda j, k: (k, 0))
    return _call(
        body, plan, name=name, grid=(J, T // tk),
        in_specs=[sh, sh, sh, tok, tok],
        out_specs=[pl.BlockSpec((3, None, FS, D), lambda j, k: (0, j, 0, 0))],
        out_shape=[jax.ShapeDtypeStruct((3, J, FS, D), F32)],
        args=(da, db, h, n, df))


def wgrad_one(a, b, tk, name, plan=None):
    J, T, FS = a.shape
    D = b.shape[1]

    def body(a_ref, b_ref, o_ref):
        @pl.when(pl.program_id(1) == 0)
        def _():
            o_ref[...] = jnp.zeros_like(o_ref)

        o_ref[...] += _dot(a_ref[...], b_ref[...], TN)

    return _call(
        body, plan, name=name, grid=(J, T // tk),
        in_specs=[pl.BlockSpec((None, tk, FS), lambda j, k: (j, k, 0)), pl.BlockSpec((tk, D), lambda j, k: (k, 0))],
        out_specs=[pl.BlockSpec((None, FS, D), lambda j, k: (j, 0, 0))],
        out_shape=[jax.ShapeDtypeStruct((J, FS, D), F32)],
        args=(a, b))


def tn_shared(a, b, tk, name, plan=None):
    T, M = a.shape
    J, _, N = b.shape

    def body(a_ref, b_ref, o_ref):
        @pl.when(pl.program_id(0) == 0)
        def _():
            o_ref[...] = jnp.zeros_like(o_ref)

        a_t = a_ref[...].astype(BF16).T
        for j in range(J):
            o_ref[j] += _dot(a_t, b_ref[j].astype(BF16))

    return _call(
        body, plan, name=name, grid=(T // tk,),
        in_specs=[pl.BlockSpec((tk, M), lambda k: (k, 0)), pl.BlockSpec((J, tk, N), lambda k: (0, k, 0))],
        out_specs=[pl.BlockSpec((J, M, N), lambda k: (0, 0, 0))],
        out_shape=[jax.ShapeDtypeStruct((J, M, N), F32)],
        args=(a, b))


def in_proj(x, gain, win, tm, name):
    T, D = x.shape
    J, _, W = win.shape

    def body(x_ref, g_ref, w_ref, p_ref):
        _, xhat = _rms_parts(x_ref[...])
        n = (xhat * g_ref[...]).astype(BF16)
        for j in range(J):
            p_ref[j] = _dot(n, w_ref[j])

    return pl.pallas_call(
        body, name=name, grid=(T // tm,),
        in_specs=[pl.BlockSpec((tm, D), lambda i: (i, 0)),
                  pl.BlockSpec((1, D), lambda i: (0, 0)),
                  pl.BlockSpec((J, D, W), lambda i: (0, 0, 0))],
        out_specs=pl.BlockSpec((J, tm, W), lambda i: (0, i, 0)),
        out_shape=jax.ShapeDtypeStruct((J, T, W), F32),
        compiler_params=_params(1),
    )(x, gain, win)


def mixer_constants(S):
    B = RET_BLOCK
    half = RET_DK // 2
    freqs = ROPE_BASE ** (-jnp.arange(half, dtype=F32) * 2.0 / RET_DK)
    ang = jnp.arange(S, dtype=F32)[:, None] * freqs[None, :]
    cos = jnp.tile(jnp.cos(ang), (1, 2 * RET_HEADS))
    sin = jnp.tile(jnp.concatenate([-jnp.sin(ang), jnp.sin(ang)], axis=1), (1, RET_HEADS))
    gamma = 1.0 - 2.0 ** (-5.0 - jnp.arange(RET_HEADS, dtype=F32))
    log_g = jnp.log(gamma)
    idx = jnp.arange(B, dtype=F32)
    ci = jnp.arange(B) // CHUNK
    dist = jnp.abs(idx[:, None] - idx[None, :])
    m_intra = jnp.exp(log_g[:, None, None] * dist[None]) * (ci[None, :] <= ci[:, None])[None].astype(F32)
    lg_lane = jnp.repeat(log_g, RET_DK)
    d_q = jnp.exp(lg_lane[None, :] * (idx[:, None] + 1.0))
    d_k = jnp.exp(lg_lane[None, :] * (B - 1.0 - idx[:, None]))
    d_s = jnp.broadcast_to(jnp.exp(lg_lane * B)[:, None], (QK_W, V_W))
    bm = (jnp.arange(QK_W)[:, None] // RET_DK == jnp.arange(V_W)[None, :] // RET_DV).astype(F32)
    win = jnp.repeat(jnp.array(POOL_WINDOWS, F32), POOL_GC)
    invc = 1.0 / jnp.minimum(jnp.arange(S, dtype=F32)[:, None] + 1.0, win[None, :])
    return dict(cos=cos, sin=sin, m=m_intra, dq=d_q, dk=d_k, ds=d_s, bm=bm, invc=invc)


def _swap_halves(x):
    w = x.shape[1]
    lane = lax.broadcasted_iota(jnp.int32, x.shape, 1)
    return jnp.where((lane % RET_DK) < RET_DK // 2, pltpu.roll(x, w - RET_DK // 2, 1), pltpu.roll(x, RET_DK // 2, 1))


def _rot(x, cos, sin):
    return x * cos + _swap_halves(x) * sin


def _rot_t(d, cos, sin):
    return d * cos + _swap_halves(d * sin)


def _lane_groups(parts):
    return jnp.concatenate([p[:, POOL_GC * g:POOL_GC * (g + 1)] for g, p in enumerate(parts)], axis=1)


def _head_mask(shape, h):
    lane = lax.broadcasted_iota(jnp.int32, shape, 1)
    return (lane // RET_DK) == h


def _group_norm(o):
    yh, rs = [], []
    for h in range(RET_HEADS):
        oh = o[:, RET_DV * h:RET_DV * (h + 1)]
        xc = oh - jnp.mean(oh, axis=-1, keepdims=True)
        r = lax.rsqrt(jnp.mean(xc * xc, axis=-1, keepdims=True) + GN_EPS)
        yh.append(xc * r)
        rs.append(r)
    return yh, rs


def mixer_fwd(p, x1, consts, gn_gain, pool_w, pool_scale, wout, n_seq, S, name):
    _, T, _ = p.shape
    D = x1.shape[1]
    B = RET_BLOCK
    nb = S // B

    def body(p_ref, x1_ref, cos_ref, sin_ref, invc_ref, m_ref, dq_ref, dk_ref, ds_ref, bm_ref,
             gain_ref, pw_ref, sc_ref, wout_ref,
             x2_ref, m_out, o_out, st_out, pooled_out, state, prev_u):
        blk = pl.program_id(1)

        @pl.when(blk == 0)
        def _():
            state[...] = jnp.zeros_like(state)
            prev_u[...] = jnp.zeros_like(prev_u)

        qk = p_ref[0]
        v = p_ref[1]
        g = p_ref[2]
        u = p_ref[3]
        cos = cos_ref[...]
        sin = sin_ref[...]
        qr = _rot(qk[:, :QK_W], cos, sin) * (RET_DK ** -0.5)
        kr = _rot(qk[:, QK_W:], cos, sin)
        qb = qr.astype(BF16)
        kb = kr.astype(BF16)
        vb = v.astype(BF16)
        st = state[...]
        st_out[...] = st
        cross = _dot((qr * dq_ref[...]).astype(BF16), st.astype(BF16))
        o_parts = []
        for h in range(RET_HEADS):
            qm = jnp.where(_head_mask(qb.shape, h), qb, jnp.zeros_like(qb))
            sc = (_dot(qm, kb, NT) * m_ref[h]).astype(BF16)
            o_parts.append(_dot(sc, vb[:, RET_DV * h:RET_DV * (h + 1)]) + cross[:, RET_DV * h:RET_DV * (h + 1)])
        o = jnp.concatenate(o_parts, axis=1)
        o_out[...] = o
        kv = _dot((kr * dk_ref[...]).astype(BF16), vb, TN)
        state[...] = st * ds_ref[...] + kv * bm_ref[...]

        yh, _ = _group_norm(o)
        r = g * _sigmoid(g) * (jnp.concatenate(yh, axis=1) * gain_ref[...])

        ext = jnp.concatenate([prev_u[...], u], axis=0)
        sums = []
        run = ext
        for k in (1, 2, 4, 8):
            run = run + pltpu.roll(run, k, 0)
            sums.append(run[B:, :])
        prev_u[...] = u
        pooled = (_lane_groups(sums) * invc_ref[...] - u).astype(BF16)
        pooled_out[...] = pooled
        yp = [_dot(pooled[:, POOL_GC * gi:POOL_GC * (gi + 1)], pw_ref[gi].astype(BF16)) for gi in range(len(POOL_WINDOWS))]
        s = jnp.concatenate(yp, axis=1) * sc_ref[...]
        m = jnp.concatenate([r, s], axis=1).astype(BF16)
        m_out[...] = m
        x2_ref[...] = x1_ref[...] + _dot(m, wout_ref[...])

    tokmap = lambda b, k: (b * nb + k, 0)
    seqmap = lambda b, k: (k, 0)
    const2 = lambda b, k: (0, 0)
    const3 = lambda b, k: (0, 0, 0)
    return pl.pallas_call(
        body, name=name, grid=(n_seq, nb),
        in_specs=[pl.BlockSpec((4, B, V_W), lambda b, k: (0, b * nb + k, 0)),
                  pl.BlockSpec((B, D), tokmap),
                  pl.BlockSpec((B, QK_W), seqmap), pl.BlockSpec((B, QK_W), seqmap), pl.BlockSpec((B, V_W), seqmap),
                  pl.BlockSpec((RET_HEADS, B, B), const3),
                  pl.BlockSpec((B, QK_W), const2), pl.BlockSpec((B, QK_W), const2),
                  pl.BlockSpec((QK_W, V_W), const2), pl.BlockSpec((QK_W, V_W), const2),
                  pl.BlockSpec((1, V_W), const2), pl.BlockSpec((4, POOL_GC, POOL_GC), const3),
                  pl.BlockSpec((1, V_W), const2), pl.BlockSpec((D, D), const2)],
        out_specs=[pl.BlockSpec((B, D), tokmap), pl.BlockSpec((B, D), tokmap), pl.BlockSpec((B, V_W), tokmap),
                   pl.BlockSpec((None, QK_W, V_W), lambda b, k: (b * nb + k, 0, 0)),
                   pl.BlockSpec((B, V_W), tokmap)],
        out_shape=[jax.ShapeDtypeStruct((T, D), F32), jax.ShapeDtypeStruct((T, D), BF16),
                   jax.ShapeDtypeStruct((T, V_W), F32), jax.ShapeDtypeStruct((T // B, QK_W, V_W), F32),
                   jax.ShapeDtypeStruct((T, V_W), BF16)],
        scratch_shapes=[pltpu.VMEM((QK_W, V_W), F32), pltpu.VMEM((B, V_W), F32)],
        compiler_params=_params(2),
    )(p, x1, consts["cos"], consts["sin"], consts["invc"], consts["m"], consts["dq"], consts["dk"],
      consts["ds"], consts["bm"], gn_gain, pool_w, pool_scale, wout)


def mixer_bwd(dx3, dn3, x2, g3, x1, p, o_s, st_s, pooled_s, consts, gmix, gn_gain, pool_w, pool_scale, wout, win,
              n_seq, S, name, plan=None):
    T, D = x1.shape
    B = RET_BLOCK
    nb = S // B
    NG = len(POOL_WINDOWS)

    def body(dx3_ref, dn3_ref, x2_ref, g3_ref, x1_ref, p_ref, o_ref, st_ref, pooled_ref, cos_ref, sin_ref, invc_ref,
             m_ref, dq_ref, dk_ref, ds_ref, bm_ref, gmix_ref, gain_ref, pw_ref, sc_ref, wout_ref, win_ref,
             dx1_ref, df1_ref, dx2_ref, dp_ref, n2_ref, dg3_ref, dgmix_ref, dgain_ref, dscale_ref, dpw_ref,
             gstate, next_e):
        b = pl.program_id(0)
        blk = pl.program_id(1)

        @pl.when(jnp.logical_and(b == 0, blk == 0))
        def _():
            dg3_ref[...] = jnp.zeros_like(dg3_ref)
            dgmix_ref[...] = jnp.zeros_like(dgmix_ref)
            dgain_ref[...] = jnp.zeros_like(dgain_ref)
            dscale_ref[...] = jnp.zeros_like(dscale_ref)
            dpw_ref[...] = jnp.zeros_like(dpw_ref)

        @pl.when(blk == 0)
        def _():
            gstate[...] = jnp.zeros_like(gstate)
            next_e[...] = jnp.zeros_like(next_e)

        r2, xhat2 = _rms_parts(x2_ref[...])
        dn3 = dn3_ref[...]
        dg3_ref[...] += jnp.sum(dn3 * xhat2, axis=0, keepdims=True)
        dx2v = _rms_bwd(dn3, xhat2, r2, g3_ref[...], dx3_ref[...])
        dx2b = dx2v.astype(BF16)
        dx2_ref[...] = dx2b
        dm = _dot(dx2b, wout_ref[...], NT)
        dr = dm[:, :V_W]
        dsv = dm[:, V_W:]

        pooled = pooled_ref[...]
        scale = sc_ref[...]
        dyp = (dsv * scale).astype(BF16)
        yp, dpl = [], []
        for gi in range(NG):
            sl = slice(POOL_GC * gi, POOL_GC * (gi + 1))
            pwb = pw_ref[gi].astype(BF16)
            yp.append(_dot(pooled[:, sl], pwb))
            dpw_ref[gi] += _dot(pooled[:, sl], dyp[:, sl], TN)
            dpl.append(_dot(dyp[:, sl], pwb, NT))
        dscale_ref[...] += jnp.sum(dsv * jnp.concatenate(yp, axis=1), axis=0, keepdims=True)
        dpooled = jnp.concatenate(dpl, axis=1)
        e = dpooled * invc_ref[...]
        ext = jnp.concatenate([e, next_e[...]], axis=0)
        sums = []
        run = ext
        for k in (1, 2, 4, 8):
            run = run + pltpu.roll(run, 2 * B - k, 0)
            sums.append(run[:B, :])
        next_e[...] = e
        du = _lane_groups(sums) - dpooled

        qk = p_ref[0]
        v = p_ref[1]
        g = p_ref[2]
        sg = _sigmoid(g)
        si = g * sg
        yh, rs = _group_norm(o_ref[...])
        yhat = jnp.concatenate(yh, axis=1)
        gain = gain_ref[...]
        dg = dr * (yhat * gain) * (sg * (1.0 + g * (1.0 - sg)))
        dy = dr * si
        dgain_ref[...] += jnp.sum(dy * yhat, axis=0, keepdims=True)
        dyh = dy * gain
        do_parts = []
        for h in range(RET_HEADS):
            sl = slice(RET_DV * h, RET_DV * (h + 1))
            dh_ = dyh[:, sl]
            m1 = jnp.mean(dh_, axis=-1, keepdims=True)
            m2 = jnp.mean(dh_ * yh[h], axis=-1, keepdims=True)
            do_parts.append(rs[h] * (dh_ - m1 - yh[h] * m2))
        dob = jnp.concatenate(do_parts, axis=1).astype(BF16)

        cos = cos_ref[...]
        sin = sin_ref[...]
        qr = _rot(qk[:, :QK_W], cos, sin) * (RET_DK ** -0.5)
        kr = _rot(qk[:, QK_W:], cos, sin)
        qb = qr.astype(BF16)
        kb = kr.astype(BF16)
        vb = v.astype(BF16)
        dqd = dq_ref[...]
        dkd = dk_ref[...]
        stb = st_ref[...].astype(BF16)
        gs = gstate[...]
        gb = gs.astype(BF16)
        dqs = _dot(dob, stb, NT) * dqd
        dkr = _dot(vb, gb, NT) * dkd
        dv_cross = _dot((kr * dkd).astype(BF16), gb)
        ds_cross = _dot((qr * dqd).astype(BF16), dob, TN) * bm_ref[...]
        gstate[...] = ds_cross + gs * ds_ref[...]
        dv_parts = []
        for h in range(RET_HEADS):
            sl = slice(RET_DV * h, RET_DV * (h + 1))
            hm = _head_mask(qb.shape, h)
            qm = jnp.where(hm, qb, jnp.zeros_like(qb))
            mh = m_ref[h]
            sc = (_dot(qm, kb, NT) * mh).astype(BF16)
            dsc = (_dot(dob[:, sl], vb[:, sl], NT) * mh).astype(BF16)
            dqs = dqs + jnp.where(hm, _dot(dsc, kb), 0.0)
            dkr = dkr + jnp.where(hm, _dot(dsc, qb, TN), 0.0)
            dv_parts.append(_dot(sc, dob[:, sl], TN) + dv_cross[:, sl])
        dq = _rot_t(dqs * (RET_DK ** -0.5), cos, sin)
        dk = _rot_t(dkr, cos, sin)
        dp = [jnp.concatenate([dq, dk], axis=1).astype(BF16), jnp.concatenate(dv_parts, axis=1).astype(BF16),
              dg.astype(BF16), du.astype(BF16)]
        dn = jnp.zeros((B, D), F32)
        for jj in range(4):
            dp_ref[jj] = dp[jj]
            dn = dn + _dot(dp[jj], win_ref[jj], NT)

        x1v = x1_ref[...]
        r, xhat = _rms_parts(x1v)
        gm = gmix_ref[...]
        n2_ref[...] = (xhat * gm).astype(BF16)
        dgmix_ref[...] += jnp.sum(dn * xhat, axis=0, keepdims=True)
        dx1 = _rms_bwd(dn, xhat, r, gm, dx2v)
        dx1_ref[...] = dx1
        df1_ref[...] = (0.5 * dx1).astype(BF16)

    rev = lambda b, k: (b * nb + (nb - 1 - k), 0)
    seqrev = lambda b, k: (nb - 1 - k, 0)
    const2 = lambda b, k: (0, 0)
    const3 = lambda b, k: (0, 0, 0)
    return _call(
        body, plan, name=name, grid=(n_seq, nb),
        in_specs=[pl.BlockSpec((B, D), rev), pl.BlockSpec((B, D), rev), pl.BlockSpec((B, D), rev),
                  pl.BlockSpec((1, D), const2), pl.BlockSpec((B, D), rev),
                  pl.BlockSpec((4, B, V_W), lambda b, k: (0, b * nb + (nb - 1 - k), 0)),
                  pl.BlockSpec((B, V_W), rev),
                  pl.BlockSpec((None, QK_W, V_W), lambda b, k: (b * nb + (nb - 1 - k), 0, 0)),
                  pl.BlockSpec((B, V_W), rev),
                  pl.BlockSpec((B, QK_W), seqrev), pl.BlockSpec((B, QK_W), seqrev), pl.BlockSpec((B, V_W), seqrev),
                  pl.BlockSpec((RET_HEADS, B, B), const3),
                  pl.BlockSpec((B, QK_W), const2), pl.BlockSpec((B, QK_W), const2),
                  pl.BlockSpec((QK_W, V_W), const2), pl.BlockSpec((QK_W, V_W), const2),
                  pl.BlockSpec((1, D), const2), pl.BlockSpec((1, V_W), const2),
                  pl.BlockSpec((NG, POOL_GC, POOL_GC), const3), pl.BlockSpec((1, V_W), const2),
                  pl.BlockSpec((D, D), const2), pl.BlockSpec((4, D, V_W), const3)],
        out_specs=[pl.BlockSpec((B, D), rev), pl.BlockSpec((B, D), rev), pl.BlockSpec((B, D), rev),
                   pl.BlockSpec((4, B, V_W), lambda b, k: (0, b * nb + (nb - 1 - k), 0)),
                   pl.BlockSpec((B, D), rev),
                   pl.BlockSpec((1, D), const2), pl.BlockSpec((1, D), const2), pl.BlockSpec((1, V_W), const2),
                   pl.BlockSpec((1, V_W), const2), pl.BlockSpec((NG, POOL_GC, POOL_GC), const3)],
        out_shape=[jax.ShapeDtypeStruct((T, D), F32), jax.ShapeDtypeStruct((T, D), BF16),
                   jax.ShapeDtypeStruct((T, D), BF16), jax.ShapeDtypeStruct((4, T, V_W), BF16),
                   jax.ShapeDtypeStruct((T, D), BF16),
                   jax.ShapeDtypeStruct((1, D), F32), jax.ShapeDtypeStruct((1, D), F32), jax.ShapeDtypeStruct((1, V_W), F32),
                   jax.ShapeDtypeStruct((1, V_W), F32), jax.ShapeDtypeStruct((NG, POOL_GC, POOL_GC), F32)],
        scratch_shapes=[pltpu.VMEM((QK_W, V_W), F32), pltpu.VMEM((B, V_W), F32)],
        args=(dx3, dn3, x2, g3, x1, p, o_s, st_s, pooled_s, consts["cos"], consts["sin"], consts["invc"], consts["m"], consts["dq"],
              consts["dk"], consts["ds"], consts["bm"], gmix, gn_gain, pool_w, pool_scale, wout, win))


def loss_bwd(x3, tgt, gain, tm, name):
    T, D = x3.shape

    def body(x_ref, t_ref, g_ref, dx_ref, df_ref, lacc_ref, dg_ref):
        @pl.when(pl.program_id(0) == 0)
        def _():
            lacc_ref[...] = jnp.zeros_like(lacc_ref)
            dg_ref[...] = jnp.zeros_like(dg_ref)

        r, xhat = _rms_parts(x_ref[...])
        gv = g_ref[...]
        err = xhat * gv - t_ref[...]
        lacc_ref[...] += jnp.sum(err * err, axis=0, keepdims=True)
        dy = err * (1.0 / D)
        dg_ref[...] += jnp.sum(dy * xhat, axis=0, keepdims=True)
        dx = _rms_bwd(dy, xhat, r, gv, 0.0)
        dx_ref[...] = dx
        df_ref[...] = (0.5 * dx).astype(BF16)

    tok = pl.BlockSpec((tm, D), lambda i: (i, 0))
    vec = pl.BlockSpec((1, D), lambda i: (0, 0))
    return pl.pallas_call(
        body, name=name, grid=(T // tm,),
        in_specs=[tok, tok, vec], out_specs=[tok, tok, vec, vec],
        out_shape=[jax.ShapeDtypeStruct((T, D), F32), jax.ShapeDtypeStruct((T, D), BF16),
                   jax.ShapeDtypeStruct((1, D), F32), jax.ShapeDtypeStruct((1, D), F32)],
        compiler_params=_params(1),
    )(x3, tgt, gain)


def _adamw_math(w, g, m, v):
    m2 = ADAM_B1 * m + (1.0 - ADAM_B1) * g
    v2 = ADAM_B2 * v + (1.0 - ADAM_B2) * (g * g)
    m_hat = m2 / (1.0 - ADAM_B1 ** ADAM_STEP)
    v_hat = v2 / (1.0 - ADAM_B2 ** ADAM_STEP)
    return -ADAM_LR * (m_hat / (jnp.sqrt(v_hat) + ADAM_EPS) + ADAM_WD * w), m2, v2


def _place():
    x, y, c = lax.axis_index("x"), lax.axis_index("y"), lax.axis_index("c")
    other_chips = [(1 - x, y), (x, 1 - y), (1 - x, 1 - y)]
    return x, y, c, other_chips


def _exchange_plan(ins, out_shape, copies, n_copies):
    def descriptors(pins, pouts, psems):
        send, recv = psems
        return [pltpu.make_async_remote_copy(src_ref=s, dst_ref=d, send_sem=send.at[i], recv_sem=recv.at[i],
                                             device_id=dev, device_id_type=MESH)
                for i, (s, d, dev) in enumerate(copies(pins, pouts))]

    def start(pins, pouts, psems):
        for cp in descriptors(pins, pouts, psems):
            cp.start()

    def finish(pins, pouts, psems):
        for cp in descriptors(pins, pouts, psems):
            cp.wait()

    return CommPlan(tuple(ins), tuple(out_shape),
                    (pltpu.SemaphoreType.DMA((n_copies,)), pltpu.SemaphoreType.DMA((n_copies,))), start, finish)


def _combine(a, b):
    assert a.mid is None and b.mid is None
    ni, no, ns = len(a.ins), len(a.out_shape), len(a.sems)

    def start(pins, pouts, psems):
        a.start(pins[:ni], pouts[:no], psems[:ns])
        b.start(pins[ni:], pouts[no:], psems[ns:])

    def finish(pins, pouts, psems):
        a.finish(pins[:ni], pouts[:no], psems[:ns])
        b.finish(pins[ni:], pouts[no:], psems[ns:])

    return CommPlan(a.ins + b.ins, a.out_shape + b.out_shape, a.sems + b.sems, start, finish)


def gather_plan(shards):
    n = len(shards)

    def start(pins, pouts, psems):
        x, y, c, chips = _place()
        mine = 2 * x + y
        for w in range(n):
            for k, (px, py) in enumerate(chips):
                pltpu.make_async_remote_copy(
                    src_ref=pins[w].at[:, c], dst_ref=pouts[w].at[:, mine, c],
                    send_sem=psems[0].at[w, k], recv_sem=psems[1].at[w, k],
                    device_id=(px, py, c), device_id_type=MESH).start()

    def mid(pins, pouts, psems):
        x, y, c, chips = _place()
        for w in range(n):
            for k, (px, py) in enumerate(chips):
                landed = pouts[w].at[:, 2 * px + py, c]
                pltpu.make_async_remote_copy(
                    src_ref=landed, dst_ref=landed, send_sem=psems[0].at[w, k], recv_sem=psems[1].at[w, k],
                    device_id=(px, py, c), device_id_type=MESH).wait_recv()
                pltpu.make_async_remote_copy(
                    src_ref=landed, dst_ref=landed, send_sem=psems[2].at[w, k], recv_sem=psems[3].at[w, k],
                    device_id=(x, y, 1 - c), device_id_type=MESH).start()

    def finish(pins, pouts, psems):
        x, y, c, chips = _place()
        mine = 2 * x + y
        for w in range(n):
            for k, (px, py) in enumerate(chips):
                landed = pouts[w].at[:, 2 * px + py, 1 - c]
                cp = pltpu.make_async_remote_copy(
                    src_ref=landed, dst_ref=landed, send_sem=psems[2].at[w, k], recv_sem=psems[3].at[w, k],
                    device_id=(x, y, 1 - c), device_id_type=MESH)
                cp.wait_recv()
                cp.wait_send()
                pltpu.make_async_remote_copy(
                    src_ref=pins[w].at[:, c], dst_ref=pouts[w].at[:, mine, c],
                    send_sem=psems[0].at[w, k], recv_sem=psems[1].at[w, k],
                    device_id=(px, py, c), device_id_type=MESH).wait_send()

    return CommPlan(tuple(shards),
                    tuple(jax.ShapeDtypeStruct((s.shape[0], N_CHIPS) + s.shape[1:], s.dtype) for s in shards),
                    tuple(pltpu.SemaphoreType.DMA((n, 3)) for _ in range(4)), start, finish, mid)


def place_own(gathered, shard, name):
    L, _, Rh, C = shard.shape

    def body(chip_ref, g_ref, s_ref, o_ref):
        o_ref[...] = s_ref[...]

    return pl.pallas_call(
        body, name=name,
        grid_spec=pltpu.PrefetchScalarGridSpec(
            num_scalar_prefetch=1, grid=(L,),
            in_specs=[ANY, pl.BlockSpec((None, 2, Rh, C), lambda l, chip_ref: (l, 0, 0, 0))],
            out_specs=pl.BlockSpec((None, None, 2, Rh, C), lambda l, chip_ref: (l, chip_ref[0], 0, 0, 0))),
        out_shape=jax.ShapeDtypeStruct(gathered.shape, gathered.dtype),
        input_output_aliases={1: 0},
        compiler_params=_params(1),
    )(_chip_index(), gathered, shard)


def gather_weights(shards):
    n = len(shards)

    def body(*refs):
        ins, outs, bufs = refs[:n], refs[n:2 * n], refs[2 * n:3 * n]
        ld_sem, st_sem, ici_send, ici_recv, d2d_send, d2d_recv = refs[3 * n:]
        x, y, c, chips = _place()
        sibling = (x, y, 1 - c)
        mine = 2 * x + y
        loads = [pltpu.make_async_copy(ins[w], bufs[w], ld_sem.at[w]) for w in range(n)]
        for cp in loads:
            cp.start()
        stores, sends = [], []
        for w in range(n):
            loads[w].wait()
            cp = pltpu.make_async_copy(bufs[w], outs[w].at[:, mine], st_sem.at[w])
            cp.start()
            stores.append(cp)
            for k, (px, py) in enumerate(chips):
                cp = pltpu.make_async_remote_copy(
                    src_ref=bufs[w].at[:, c], dst_ref=outs[w].at[:, mine, c],
                    send_sem=ici_send.at[w, k], recv_sem=ici_recv.at[w, k],
                    device_id=(px, py, c), device_id_type=MESH)
                cp.start()
                sends.append(cp)
        for w in range(n):
            for k, (px, py) in enumerate(chips):
                landed = outs[w].at[:, 2 * px + py, c]
                pltpu.make_async_remote_copy(
                    src_ref=landed, dst_ref=landed, send_sem=ici_send.at[w, k], recv_sem=ici_recv.at[w, k],
                    device_id=(px, py, c), device_id_type=MESH).wait_recv()
                cp = pltpu.make_async_remote_copy(
                    src_ref=landed, dst_ref=landed, send_sem=d2d_send.at[w, k], recv_sem=d2d_recv.at[w, k],
                    device_id=sibling, device_id_type=MESH)
                cp.start()
                sends.append(cp)
        for w in range(n):
            for k, (px, py) in enumerate(chips):
                landed = outs[w].at[:, 2 * px + py, 1 - c]
                pltpu.make_async_remote_copy(
                    src_ref=landed, dst_ref=landed, send_sem=d2d_send.at[w, k], recv_sem=d2d_recv.at[w, k],
                    device_id=sibling, device_id_type=MESH).wait_recv()
        for cp in sends:
            cp.wait_send()
        for cp in stores:
            cp.wait()

    return pl.pallas_call(
        body, name="gather_weights",
        in_specs=[ANY] * n, out_specs=[ANY] * n,
        out_shape=[jax.ShapeDtypeStruct((s.shape[0], N_CHIPS) + s.shape[1:], s.dtype) for s in shards],
        scratch_shapes=[pltpu.VMEM(s.shape, s.dtype) for s in shards] +
                       [pltpu.SemaphoreType.DMA((n,)), pltpu.SemaphoreType.DMA((n,)),
                        pltpu.SemaphoreType.DMA((n, 3)), pltpu.SemaphoreType.DMA((n, 3)),
                        pltpu.SemaphoreType.DMA((n, 3)), pltpu.SemaphoreType.DMA((n, 3))],
        compiler_params=pltpu.CompilerParams(vmem_limit_bytes=VMEM_LIMIT),
    )(*shards)


def send_to_sibling_other_half(grads):
    def copies(ins, outs):
        x, y, c, _ = _place()
        return [(ins[w].at[:, :, 1 - c], outs[w], (x, y, 1 - c)) for w in range(len(grads))]

    return _exchange_plan(grads, [jax.ShapeDtypeStruct(g.shape[:2] + g.shape[3:], g.dtype) for g in grads], copies,
                          len(grads))


def _core_index():
    return jnp.reshape(lax.axis_index("c"), (1,)).astype(jnp.int32)


def _chip_index():
    return jnp.reshape(2 * lax.axis_index("x") + lax.axis_index("y"), (1,)).astype(jnp.int32)


def add_own_half(g, recv, name):
    L, J, _, Rh, C = g.shape

    def body(c_ref, g_ref, r_ref, o_ref):
        o_ref[...] = (g_ref[...] + r_ref[...]).astype(BF16)

    return pl.pallas_call(
        body, name=name,
        grid_spec=pltpu.PrefetchScalarGridSpec(
            num_scalar_prefetch=1, grid=(L, J),
            in_specs=[pl.BlockSpec((None, None, None, Rh, C), lambda l, j, c_ref: (l, j, c_ref[0], 0, 0)),
                      pl.BlockSpec((None, None, Rh, C), lambda l, j, c_ref: (l, j, 0, 0))],
            out_specs=pl.BlockSpec((None, None, Rh, C), lambda l, j, c_ref: (l, j, 0, 0))),
        out_shape=jax.ShapeDtypeStruct((L, J, Rh, C), BF16),
        compiler_params=_params(2),
    )(_core_index(), g, recv)


def exchange_between_chips(sums):
    def copies(ins, outs):
        x, y, c, chips = _place()
        return [(ins[w].at[:, 2 * px + py], outs[w].at[k], (px, py, c))
                for w in range(len(sums)) for k, (px, py) in enumerate(chips)]

    return _exchange_plan(sums, [jax.ShapeDtypeStruct((3, s.shape[0]) + s.shape[2:], s.dtype) for s in sums], copies,
                          3 * len(sums))


def sum_chips(own, recv, name):
    L, _, Rh, C = own.shape

    def body(chip_ref, o_ref, r_ref, out_ref):
        acc = o_ref[...].astype(F32)
        for k in range(3):
            acc = acc + r_ref[k].astype(F32)
        out_ref[...] = acc

    return pl.pallas_call(
        body, name=name,
        grid_spec=pltpu.PrefetchScalarGridSpec(
            num_scalar_prefetch=1, grid=(L,),
            in_specs=[pl.BlockSpec((None, None, Rh, C), lambda l, chip_ref: (l, chip_ref[0], 0, 0)),
                      pl.BlockSpec((3, None, Rh, C), lambda l, chip_ref: (0, l, 0, 0))],
            out_specs=pl.BlockSpec((None, Rh, C), lambda l, chip_ref: (l, 0, 0))),
        out_shape=jax.ShapeDtypeStruct((L, Rh, C), F32),
        compiler_params=_params(1),
    )(_chip_index(), own, recv)


def share_with_sibling(halves):
    def copies(ins, outs):
        x, y, c, _ = _place()
        return [(ins[w], outs[w], (x, y, 1 - c)) for w in range(len(halves))]

    return _exchange_plan(halves, [jax.ShapeDtypeStruct(h.shape, h.dtype) for h in halves], copies, len(halves))


def adamw_group(ws, ms, vs, own, sib, name, plan=None):
    L = len(ws)
    R, C = ws[0].shape
    Rh = R // 2
    tr = Rh // 2
    nr = Rh // tr

    def body(*refs):
        w, m, v = refs[:L], refs[L:2 * L], refs[2 * L:3 * L]
        own_ref, sib_ref = refs[3 * L], refs[3 * L + 1]
        outs = refs[3 * L + 2:]
        mine = pl.program_id(0) == lax.axis_index("c")
        for l in range(L):
            gv = jnp.where(mine, own_ref[l], sib_ref[l])
            d, m2, v2 = _adamw_math(w[l][...], gv, m[l][...], v[l][...])
            outs[4 * l][...] = d
            outs[4 * l + 1][...] = m2
            outs[4 * l + 2][...] = v2
            outs[4 * l + 3][...] = gv

    blk = pl.BlockSpec((tr, C), lambda h, r: (h * nr + r, 0))
    half = pl.BlockSpec((L, tr, C), lambda h, r: (0, r, 0))
    sds = jax.ShapeDtypeStruct((R, C), F32)
    return _call(body, plan, name=name, grid=(2, nr), in_specs=[blk] * (3 * L) + [half, half],
                 out_specs=[blk] * (4 * L), out_shape=[sds] * (4 * L), args=(*ws, *ms, *vs, own, sib))


def gather_small_plan(part):
    def copies(ins, outs):
        x, y, c, _ = _place()
        me = 4 * x + 2 * y + c
        return [(ins[0], outs[0].at[me], (x ^ ((k >> 2) & 1), y ^ ((k >> 1) & 1), c ^ (k & 1))) for k in range(1, N_DEV)]

    return _exchange_plan([part], [jax.ShapeDtypeStruct((N_DEV,) + part.shape, part.dtype)], copies, N_DEV - 1)


def sum_small(part, gathered, name):
    R, C = part.shape

    def body(p_ref, g_ref, o_ref):
        d = pl.program_id(0)
        me = 4 * lax.axis_index("x") + 2 * lax.axis_index("y") + lax.axis_index("c")
        val = jnp.where(d == me, p_ref[...], g_ref[...])

        @pl.when(d == 0)
        def _():
            o_ref[...] = val

        @pl.when(d > 0)
        def _():
            o_ref[...] += val

    return pl.pallas_call(
        body, name=name, grid=(N_DEV,),
        in_specs=[pl.BlockSpec((R, C), lambda d: (0, 0)), pl.BlockSpec((None, R, C), lambda d: (d, 0, 0))],
        out_specs=pl.BlockSpec((R, C), lambda d: (0, 0)),
        out_shape=jax.ShapeDtypeStruct((R, C), F32),
        compiler_params=_params(1),
    )(part, gathered)


SMALL = ("norm_ffn1", "norm_mix", "norm_ffn2", "norm_final", "ret_gn_gain", "pool_scale", "pool_w")
GROUPS = (("ffn1", ("ffn1_gate", "ffn1_up", "ffn1_down")), ("w_in", ("w_in",)), ("w_out", ("w_out",)),
          ("ffn2", ("ffn2_gate", "ffn2_up", "ffn2_down")))
TRANSPOSED = ("ffn1_gate", "ffn1_up", "ffn2_gate", "ffn2_up")
IN_W = 2 * QK_W + 3 * V_W
ORDER = ("norm_ffn1", "ffn1_gate", "ffn1_up", "ffn1_down", "norm_mix", "w_in", "ret_gn_gain", "pool_w", "pool_scale",
         "w_out", "norm_ffn2", "ffn2_gate", "ffn2_up", "ffn2_down", "norm_final")


SUBLANES = 8


def _small_rows(shapes):
    rows = [math.prod(shapes[k]) // 128 for k in SMALL]
    offs, off = [], 0
    for r in rows:
        offs.append(off)
        off += -(-r // SUBLANES) * SUBLANES
    return rows, offs, off


def _pack_small(d, shapes):
    rows, offs, _ = _small_rows(shapes)
    pieces = []
    for k, r in zip(SMALL, rows):
        pieces.append(d[k].reshape(r, 128))
        if r % SUBLANES:
            pieces.append(jnp.zeros((SUBLANES - r % SUBLANES, 128), F32))
    return jnp.concatenate(pieces, axis=0)


def adamw_small(W, M, V, summed, shapes):
    rows, offs, _ = _small_rows(shapes)
    n = len(SMALL)

    def body(*refs):
        w, m, v, s_ref = refs[:n], refs[n:2 * n], refs[2 * n:3 * n], refs[3 * n]
        outs = refs[3 * n + 1:]
        for i, (r, off) in enumerate(zip(rows, offs)):
            gv = s_ref[off:off + r, :]
            d, m2, v2 = _adamw_math(w[i][...], gv, m[i][...], v[i][...])
            outs[4 * i][...] = d
            outs[4 * i + 1][...] = m2
            outs[4 * i + 2][...] = v2
            outs[4 * i + 3][...] = gv

    vm = pl.BlockSpec(memory_space=pltpu.VMEM)
    flat = lambda d: [d[k].reshape(r, 128) for k, r in zip(SMALL, rows)]
    res = pl.pallas_call(
        body, name="adamw_small", in_specs=[vm] * (3 * n + 1), out_specs=[vm] * (4 * n),
        out_shape=[jax.ShapeDtypeStruct((r, 128), F32) for r in rows for _ in range(4)],
    )(*flat(W), *flat(M), *flat(V), summed)
    return {k: [a.reshape(shapes[k]) for a in res[4 * i:4 * i + 4]] for i, k in enumerate(SMALL)}


def kernel(x, norm_ffn1, ffn1_gate, ffn1_up, ffn1_down, norm_mix, w_in, ret_gn_gain, pool_w, pool_scale, w_out, norm_ffn2, ffn2_gate, ffn2_up, ffn2_down, norm_final, loss_target, m_norm_ffn1, m_ffn1_gate, m_ffn1_up, m_ffn1_down, m_norm_mix, m_w_in, m_ret_gn_gain, m_pool_w, m_pool_scale, m_w_out, m_norm_ffn2, m_ffn2_gate, m_ffn2_up, m_ffn2_down, m_norm_final, v_norm_ffn1, v_ffn1_gate, v_ffn1_up, v_ffn1_down, v_norm_mix, v_w_in, v_ret_gn_gain, v_pool_w, v_pool_scale, v_w_out, v_norm_ffn2, v_ffn2_gate, v_ffn2_up, v_ffn2_down, v_norm_final):
    W = dict(norm_ffn1=norm_ffn1, ffn1_gate=ffn1_gate, ffn1_up=ffn1_up, ffn1_down=ffn1_down, norm_mix=norm_mix, w_in=w_in,
             ret_gn_gain=ret_gn_gain, pool_w=pool_w, pool_scale=pool_scale, w_out=w_out, norm_ffn2=norm_ffn2,
             ffn2_gate=ffn2_gate, ffn2_up=ffn2_up, ffn2_down=ffn2_down, norm_final=norm_final)
    M = dict(norm_ffn1=m_norm_ffn1, ffn1_gate=m_ffn1_gate, ffn1_up=m_ffn1_up, ffn1_down=m_ffn1_down, norm_mix=m_norm_mix,
             w_in=m_w_in, ret_gn_gain=m_ret_gn_gain, pool_w=m_pool_w, pool_scale=m_pool_scale, w_out=m_w_out,
             norm_ffn2=m_norm_ffn2, ffn2_gate=m_ffn2_gate, ffn2_up=m_ffn2_up, ffn2_down=m_ffn2_down, norm_final=m_norm_final)
    V = dict(norm_ffn1=v_norm_ffn1, ffn1_gate=v_ffn1_gate, ffn1_up=v_ffn1_up, ffn1_down=v_ffn1_down, norm_mix=v_norm_mix,
             w_in=v_w_in, ret_gn_gain=v_ret_gn_gain, pool_w=v_pool_w, pool_scale=v_pool_scale, w_out=v_w_out,
             norm_ffn2=v_norm_ffn2, ffn2_gate=v_ffn2_gate, ffn2_up=v_ffn2_up, ffn2_down=v_ffn2_down, norm_final=v_norm_final)

    n_seq, S, D = x.shape
    T = n_seq * S
    tm = min(512, T // 2)
    tm_fwd = min(1024, T // 2)
    tk = min(1024, T // 2)
    xf = x.reshape(T, D)
    tgt = loss_target.reshape(T, D)

    def local(d, k):
        a = d[k][0]
        return a.T if k in TRANSPOSED else a

    def to_out(k, a):
        return (a.T if k in TRANSPOSED else a)[None]

    loc_bf = []
    for _, members in GROUPS:
        st = jnp.stack([local(W, k).astype(BF16) for k in members])
        loc_bf.append(st.reshape(st.shape[0], 2, st.shape[1] // 2, st.shape[2]))
    FS = loc_bf[0].shape[2] * 2
    w3_1 = gather_weights(loc_bf[:1])[0].reshape(3, N_CHIPS, FS, D)
    g1 = norm_ffn1.reshape(1, D)
    gm = norm_mix.reshape(1, D)
    g3 = norm_ffn2.reshape(1, D)
    gf = norm_final.reshape(1, D)
    gn_gain = ret_gn_gain.reshape(1, V_W)
    pscale = pool_scale.reshape(1, V_W)
    pw = pool_w.reshape(len(POOL_WINDOWS), POOL_GC, POOL_GC)
    consts = mixer_constants(S)

    def halves5(g, lb):
        return g.reshape(lb.shape[0], N_CHIPS, 2, lb.shape[2], lb.shape[3])

    (x1, a1, b1, n1), landed = ffn_fwd(xf, g1, w3_1, tm_fwd, "ffn1_fwd", plan=gather_plan(loc_bf[1:]))
    gathered = [place_own(g, lb, "place_own_" + gn) for g, lb, (gn, _) in zip(landed, loc_bf[1:], GROUPS[1:])]
    win_full = gathered[0].reshape(N_CHIPS, D, IN_W // N_CHIPS)
    wout_full = gathered[1].reshape(D, D)
    w3_2 = gathered[2].reshape(3, N_CHIPS, FS, D)
    p = in_proj(x1, gm, win_full, tm, "in_proj")
    x2, m_s, o_s, st_s, pooled_s = mixer_fwd(p, x1, consts, gn_gain, pw, pscale, wout_full, n_seq, S, "mixer_fwd")
    x3, a2, b2, n3 = ffn_fwd(x2, g3, w3_2, tm_fwd, "ffn2_fwd")

    dx3, df2, lacc, dgf = loss_bwd(x3, tgt, gf, tm, "loss_bwd")
    dn3, da2, db2, h2 = ffn_bwd(df2, a2, b2, w3_2, tm_fwd, "ffn2_bwd")
    to_sib, to_chips, back = send_to_sibling_other_half, exchange_between_chips, share_with_sibling
    out_g, out_d, out_m, out_v = {}, {}, {}, {}

    def update(members, own, sib, name):
        res = adamw_group([local(W, k) for k in members], [local(M, k) for k in members], [local(V, k) for k in members],
                          own, sib, "adamw_" + name)
        for l, k in enumerate(members):
            out_d[k], out_m[k], out_v[k], out_g[k] = (to_out(k, r) for r in res[4 * l:4 * l + 4])

    (gr_ffn2,) = ffn_wgrad(da2, db2, h2, n3, df2, tk, "ffn2_wgrad")
    g_ffn2 = halves5(gr_ffn2, loc_bf[3])
    (dx1, df1, dx2b, dp, n2, dg3, dgm, dgn, dps, dpw), (sb_ffn2,) = mixer_bwd(
        dx3, dn3, x2, g3, x1, p, o_s, st_s, pooled_s, consts, gm, gn_gain, pw, pscale, wout_full, win_full, n_seq, S,
        "mixer_bwd", plan=to_sib([g_ffn2]))
    cs_ffn2 = add_own_half(g_ffn2, sb_ffn2, "add_sibling_ffn2")
    (gr_wout,) = tn_shared(m_s, dx2b[None], tk, "dw_out")
    (gr_win,) = tn_shared(n2, dp, tk, "dw_in")
    g_mix = [halves5(gr_win, loc_bf[1]), halves5(gr_wout, loc_bf[2])]
    (dn1, da1, db1, h1), (pc_ffn2, sb_win, sb_wout) = ffn_bwd(
        df1, a1, b1, w3_1, tm_fwd, "ffn1_bwd", plan=_combine(to_chips([cs_ffn2]), to_sib(g_mix)))
    mh_ffn2 = sum_chips(cs_ffn2, pc_ffn2, "sum_chips_ffn2")
    cs_mix = [add_own_half(g, r, "add_sibling_" + gn) for g, r, gn in zip(g_mix, (sb_win, sb_wout), ("w_in", "w_out"))]
    def one(g):
        return g.reshape((1, N_CHIPS, 2) + loc_bf[0].shape[2:])

    tk1 = min(2 * tk, T // 2)
    (gr_g,), (sh_ffn2, pc_win, pc_wout) = wgrad_one(
        da1, n1, tk1, "ffn1_wgrad_gate", plan=_combine(back([mh_ffn2]), to_chips(cs_mix)))
    mh_mix = [sum_chips(cs, pc, "sum_chips_" + gn) for cs, pc, gn in zip(cs_mix, (pc_win, pc_wout), ("w_in", "w_out"))]
    (gr_u,), (sb_g,) = wgrad_one(db1, n1, tk1, "ffn1_wgrad_up", plan=to_sib([one(gr_g)]))
    cs_g = add_own_half(one(gr_g), sb_g, "add_sibling_ffn1_gate")
    (gr_d,), (pc_g, sb_u) = wgrad_one(
        h1, df1, tk1, "ffn1_wgrad_down", plan=_combine(to_chips([cs_g]), to_sib([one(gr_u)])))
    mh_g = sum_chips(cs_g, pc_g, "sum_chips_ffn1_gate")
    cs_u = add_own_half(one(gr_u), sb_u, "add_sibling_ffn1_up")
    (dx0, dg1), (sh_g, sh_win, sh_wout, pc_u, sb_d) = rms_bwd(
        xf, dx1, dn1, g1, tm, "ffn1_rms_bwd",
        plan=_combine(_combine(back([mh_g] + mh_mix), to_chips([cs_u])), to_sib([one(gr_d)])))
    mh_u = sum_chips(cs_u, pc_u, "sum_chips_ffn1_up")
    cs_d = add_own_half(one(gr_d), sb_d, "add_sibling_ffn1_down")
    small_part = dict(norm_ffn1=dg1, norm_mix=dgm, norm_ffn2=dg3, norm_final=dgf, ret_gn_gain=dgn, pool_scale=dps, pool_w=dpw)
    shapes = {k: W[k].shape for k in SMALL}
    part = jnp.concatenate([_pack_small(small_part, shapes), lacc.reshape(-1, 128)], axis=0)
    n_small = part.shape[0] - lacc.size // 128
    sh_u, pc_d, parts = _run_plan(
        _combine(_combine(back([mh_u]), to_chips([cs_d])), gather_small_plan(part)), "ffn1_grads_stage_b")
    mh_d = sum_chips(cs_d, pc_d, "sum_chips_ffn1_down")
    (sh_d,) = _run_plan(back([mh_d]), "ffn1_grads_stage_c")
    summed = sum_small(part, parts, "sum_small")
    loss = jnp.sum(summed[n_small:]) * (0.5 / D)
    for k, own, sib in zip(GROUPS[0][1], (mh_g, mh_u, mh_d), (sh_g, sh_u, sh_d)):
        update((k,), own, sib, k)
    update(GROUPS[1][1], mh_mix[0], sh_win, "w_in")
    update(GROUPS[2][1], mh_mix[1], sh_wout, "w_out")
    update(GROUPS[3][1], mh_ffn2, sh_ffn2, "ffn2")

    for k, (d_, m_, v_, g_) in adamw_small(W, M, V, summed, shapes).items():
        out_d[k], out_m[k], out_v[k], out_g[k] = d_, m_, v_, g_

    grad_x = dx0.reshape(n_seq, S, D)
    return (loss, grad_x, *[out_g[k] for k in ORDER], *[out_d[k] for k in ORDER],
            *[out_m[k] for k in ORDER], *[out_v[k] for k in ORDER])
```

```python
import functools
import math
from typing import Callable, NamedTuple, Optional

import jax
import jax.numpy as jnp
from jax import lax
from jax.experimental import pallas as pl
from jax.experimental.pallas import tpu as pltpu

F32 = jnp.float32
BF16 = jnp.bfloat16
MESH = pl.DeviceIdType.MESH

N_CHIPS = 4
N_DEV = 8
CHUNK = 64
RET_HEADS = 4
RET_DK = 64
RET_DV = 128
QK_W = RET_HEADS * RET_DK
V_W = RET_HEADS * RET_DV
POOL_WINDOWS = (2, 4, 8, 16)
POOL_GC = 128
ROPE_BASE = 10000.0
RMS_EPS = 1e-6
GN_EPS = 1e-5
ADAM_LR = 0.001
ADAM_B1 = 0.9
ADAM_B2 = 0.999
ADAM_EPS = 1e-08
ADAM_WD = 0.01
ADAM_STEP = 10
MXU_W = 256
RET_BLOCK = MXU_W
VMEM_LIMIT = 56 * 1024 * 1024

NN = (((1,), (0,)), ((), ()))
NT = (((1,), (1,)), ((), ()))
TN = (((0,), (0,)), ((), ()))


def _dot(a, b, dims=NN):
    return lax.dot_general(a, b, dims, preferred_element_type=F32)


def _sigmoid(x):
    return 0.5 * jnp.tanh(0.5 * x) + 0.5


def _params(n_grid):
    return pltpu.CompilerParams(dimension_semantics=("arbitrary",) * n_grid, vmem_limit_bytes=VMEM_LIMIT)


class CommPlan(NamedTuple):
    ins: tuple
    out_shape: tuple
    sems: tuple
    start: Callable
    finish: Callable
    mid: Optional[Callable] = None
    mid_at: float = 0.85


ANY = pl.BlockSpec(memory_space=pl.ANY)


def _call(body, plan, *, name, grid, in_specs, out_specs, out_shape, args, scratch_shapes=()):
    n_grid = len(grid)
    if plan is None:
        return pl.pallas_call(body, name=name, grid=grid, in_specs=in_specs, out_specs=out_specs, out_shape=out_shape,
                              scratch_shapes=scratch_shapes, compiler_params=_params(n_grid))(*args)
    n_in, n_out, n_sc = len(in_specs), len(out_specs), len(scratch_shapes)
    p_in, p_out = len(plan.ins), len(plan.out_shape)
    total = math.prod(grid)
    mid_step = min(total - 1, int(plan.mid_at * total))

    def riding(*refs):
        ins, pins = refs[:n_in], refs[n_in:n_in + p_in]
        o0 = n_in + p_in
        outs, pouts = refs[o0:o0 + n_out], refs[o0 + n_out:o0 + n_out + p_out]
        s0 = o0 + n_out + p_out
        scratch, psems = refs[s0:s0 + n_sc], refs[s0 + n_sc:]
        step = pl.program_id(0)
        for d in range(1, n_grid):
            step = step * grid[d] + pl.program_id(d)

        @pl.when(step == 0)
        def _():
            plan.start(pins, pouts, psems)

        body(*ins, *outs, *scratch)

        if plan.mid is not None:
            @pl.when(step == mid_step)
            def _():
                plan.mid(pins, pouts, psems)

        @pl.when(step == total - 1)
        def _():
            plan.finish(pins, pouts, psems)

    res = pl.pallas_call(
        riding, name=name, grid=grid, in_specs=list(in_specs) + [ANY] * p_in, out_specs=list(out_specs) + [ANY] * p_out,
        out_shape=list(out_shape) + list(plan.out_shape), scratch_shapes=list(scratch_shapes) + list(plan.sems),
        compiler_params=_params(n_grid))(*args, *plan.ins)
    return res[:n_out], res[n_out:]


def _run_plan(plan, name):
    p_in, p_out = len(plan.ins), len(plan.out_shape)

    def body(*refs):
        pins, pouts, psems = refs[:p_in], refs[p_in:p_in + p_out], refs[p_in + p_out:]
        plan.start(pins, pouts, psems)
        if plan.mid is not None:
            plan.mid(pins, pouts, psems)
        plan.finish(pins, pouts, psems)

    return pl.pallas_call(body, name=name, in_specs=[ANY] * p_in, out_specs=[ANY] * p_out,
                          out_shape=list(plan.out_shape), scratch_shapes=list(plan.sems))(*plan.ins)


def _rms_parts(x):
    r = lax.rsqrt(jnp.mean(x * x, axis=-1, keepdims=True) + RMS_EPS)
    return r, x * r


def _rms_bwd(dn, xhat, r, gain, dres):
    dxh = dn * gain
    return dres + r * (dxh - xhat * jnp.mean(dxh * xhat, axis=-1, keepdims=True))


def _w3_specs(FS, D):
    return [pl.BlockSpec((None, None, FS, D), functools.partial(lambda i, j, k: (k, j, 0, 0), k=k)) for k in range(3)]


def ffn_fwd(x, gain, w3, tm, name, plan=None):
    T, D = x.shape
    _, J, FS, _ = w3.shape

    def body(x_ref, g_ref, wg_ref, wu_ref, wd_ref, xo_ref, b_ref, si_ref, t_ref, h_ref, n_ref, acc):
        j = pl.program_id(1)

        @pl.when(j == 0)
        def _():
            _, xhat = _rms_parts(x_ref[...])
            n_ref[...] = (xhat * g_ref[...]).astype(BF16)
            acc[...] = jnp.zeros_like(acc)

        n = n_ref[...]
        a = _dot(n, wg_ref[...], NT)
        b = _dot(n, wu_ref[...], NT)
        sg = _sigmoid(a)
        si = a * sg
        h = (si * b).astype(BF16)
        b_ref[...] = b.astype(BF16)
        si_ref[...] = si.astype(BF16)
        t_ref[...] = (sg + si * (1.0 - sg)).astype(BF16)
        h_ref[...] = h
        acc[...] += _dot(h, wd_ref[...])

        @pl.when(j == J - 1)
        def _():
            xo_ref[...] = x_ref[...] + 0.5 * acc[...]

    return _call(
        body, plan, name=name, grid=(T // tm, J),
        in_specs=[pl.BlockSpec((tm, D), lambda i, j: (i, 0)),
                  pl.BlockSpec((1, D), lambda i, j: (0, 0))] + _w3_specs(FS, D),
        out_specs=[pl.BlockSpec((tm, D), lambda i, j: (i, 0))] +
                  [pl.BlockSpec((None, tm, FS), lambda i, j: (j, i, 0))] * 4 +
                  [pl.BlockSpec((tm, D), lambda i, j: (i, 0))],
        out_shape=[jax.ShapeDtypeStruct((T, D), F32)] + [jax.ShapeDtypeStruct((J, T, FS), BF16)] * 4 +
                  [jax.ShapeDtypeStruct((T, D), BF16)],
        scratch_shapes=[pltpu.VMEM((tm, D), F32)],
        args=(x, gain, w3, w3, w3))


def ffn_bwd(df, b_s, si_s, t_s, w3, tm, name, plan=None):
    T, D = df.shape
    _, J, FS, _ = w3.shape

    chunks = [slice(s, min(s + MXU_W, FS)) for s in range(0, FS, MXU_W)]
    nc = len(chunks)

    def body(df_ref, b_ref, si_ref, t_ref, wg_ref, wu_ref, wd_ref, dn_ref, da_ref, db_ref):
        @pl.when(pl.program_id(1) == 0)
        def _():
            dn_ref[...] = jnp.zeros_like(dn_ref)

        def elementwise(dh, sl):
            da = (dh * b_ref[:, sl].astype(F32) * t_ref[:, sl].astype(F32)).astype(BF16)
            db = (dh * si_ref[:, sl].astype(F32)).astype(BF16)
            da_ref[:, sl] = da
            db_ref[:, sl] = db
            return da, db

        df = df_ref[...]
        dh, dab, acc = [None] * nc, [None] * nc, None
        for step in range(nc + 2):
            if step < nc:
                dh[step] = _dot(df, wd_ref[chunks[step], :], NT)
            if 1 <= step <= nc:
                dab[step - 1] = elementwise(dh[step - 1], chunks[step - 1])
            if step >= 2:
                sl = chunks[step - 2]
                part = _dot(dab[step - 2][0], wg_ref[sl, :]) + _dot(dab[step - 2][1], wu_ref[sl, :])
                acc = part if acc is None else acc + part
        dn_ref[...] += acc

    tok = pl.BlockSpec((tm, D), lambda i, j: (i, 0))
    sh = pl.BlockSpec((None, tm, FS), lambda i, j: (j, i, 0))
    return _call(
        body, plan, name=name, grid=(T // tm, J),
        in_specs=[tok, sh, sh, sh] + _w3_specs(FS, D),
        out_specs=[tok, sh, sh],
        out_shape=[jax.ShapeDtypeStruct((T, D), F32),
                   jax.ShapeDtypeStruct((J, T, FS), BF16),
                   jax.ShapeDtypeStruct((J, T, FS), BF16)],
        args=(df, b_s, si_s, t_s, w3, w3, w3))


def rms_bwd(x, dout, dn, gain, tm, name, plan=None):
    T, D = x.shape

    def body(x_ref, do_ref, dn_ref, g_ref, dx_ref, dgain_ref):
        @pl.when(pl.program_id(0) == 0)
        def _():
            dgain_ref[...] = jnp.zeros_like(dgain_ref)

        r, xhat = _rms_parts(x_ref[...])
        dn = dn_ref[...]
        dgain_ref[...] += jnp.sum(dn * xhat, axis=0, keepdims=True)
        dx_ref[...] = _rms_bwd(dn, xhat, r, g_ref[...], do_ref[...])

    tok = pl.BlockSpec((tm, D), lambda i: (i, 0))
    vec = pl.BlockSpec((1, D), lambda i: (0, 0))
    return _call(
        body, plan, name=name, grid=(T // tm,), in_specs=[tok, tok, tok, vec], out_specs=[tok, vec],
        out_shape=[jax.ShapeDtypeStruct((T, D), F32), jax.ShapeDtypeStruct((1, D), F32)],
        args=(x, dout, dn, gain))


def ffn_wgrad(da, db, h, n, df, tk, name, plan=None):
    J, T, FS = da.shape
    D = n.shape[1]

    def body(da_ref, db_ref, h_ref, n_ref, df_ref, o_ref):
        @pl.when(pl.program_id(1) == 0)
        def _():
            o_ref[...] = jnp.zeros_like(o_ref)

        nv = n_ref[...]
        o_ref[0] += _dot(da_ref[...], nv, TN)
        o_ref[1] += _dot(db_ref[...], nv, TN)
        o_ref[2] += _dot(h_ref[...], df_ref[...], TN)

    sh = pl.BlockSpec((None, tk, FS), lambda j, k: (j, k, 0))
    tok = pl.BlockSpec((tk, D), lambda j, k: (k, 0))
    return _call(
        body, plan, name=name, grid=(J, T // tk),
        in_specs=[sh, sh, sh, tok, tok],
        out_specs=[pl.BlockSpec((3, None, FS, D), lambda j, k: (0, j, 0, 0))],
        out_shape=[jax.ShapeDtypeStruct((3, J, FS, D), F32)],
        args=(da, db, h, n, df))


def wgrad_one(a, b, tk, name, plan=None):
    J, T, FS = a.shape
    D = b.shape[1]

    def body(a_ref, b_ref, o_ref):
        @pl.when(pl.program_id(1) == 0)
        def _():
            o_ref[...] = jnp.zeros_like(o_ref)

        o_ref[...] += _dot(a_ref[...], b_ref[...], TN)

    return _call(
        body, plan, name=name, grid=(J, T // tk),
        in_specs=[pl.BlockSpec((None, tk, FS), lambda j, k: (j, k, 0)), pl.BlockSpec((tk, D), lambda j, k: (k, 0))],
        out_specs=[pl.BlockSpec((None, FS, D), lambda j, k: (j, 0, 0))],
        out_shape=[jax.ShapeDtypeStruct((J, FS, D), F32)],
        args=(a, b))


def tn_shared(a, b, tk, name, plan=None):
    T, M = a.shape
    J, _, N = b.shape

    def body(a_ref, b_ref, o_ref):
        @pl.when(pl.program_id(0) == 0)
        def _():
            o_ref[...] = jnp.zeros_like(o_ref)

        a_t = a_ref[...].astype(BF16).T
        for j in range(J):
            o_ref[j] += _dot(a_t, b_ref[j].astype(BF16))

    return _call(
        body, plan, name=name, grid=(T // tk,),
        in_specs=[pl.BlockSpec((tk, M), lambda k: (k, 0)), pl.BlockSpec((J, tk, N), lambda k: (0, k, 0))],
        out_specs=[pl.BlockSpec((J, M, N), lambda k: (0, 0, 0))],
        out_shape=[jax.ShapeDtypeStruct((J, M, N), F32)],
        args=(a, b))


def in_proj(x, gain, win, tm, name):
    T, D = x.shape
    J, _, W = win.shape

    def body(x_ref, g_ref, w_ref, p_ref):
        _, xhat = _rms_parts(x_ref[...])
        n = (xhat * g_ref[...]).astype(BF16)
        for j in range(J):
            p_ref[j] = _dot(n, w_ref[j])

    return pl.pallas_call(
        body, name=name, grid=(T // tm,),
        in_specs=[pl.BlockSpec((tm, D), lambda i: (i, 0)),
                  pl.BlockSpec((1, D), lambda i: (0, 0)),
                  pl.BlockSpec((J, D, W), lambda i: (0, 0, 0))],
        out_specs=pl.BlockSpec((J, tm, W), lambda i: (0, i, 0)),
        out_shape=jax.ShapeDtypeStruct((J, T, W), F32),
        compiler_params=_params(1),
    )(x, gain, win)


def mixer_constants(S):
    B = RET_BLOCK
    half = RET_DK // 2
    freqs = ROPE_BASE ** (-jnp.arange(half, dtype=F32) * 2.0 / RET_DK)
    ang = jnp.arange(S, dtype=F32)[:, None] * freqs[None, :]
    cos = jnp.tile(jnp.cos(ang), (1, 2 * RET_HEADS))
    sin = jnp.tile(jnp.concatenate([-jnp.sin(ang), jnp.sin(ang)], axis=1), (1, RET_HEADS))
    gamma = 1.0 - 2.0 ** (-5.0 - jnp.arange(RET_HEADS, dtype=F32))
    log_g = jnp.log(gamma)
    idx = jnp.arange(B, dtype=F32)
    ci = jnp.arange(B) // CHUNK
    dist = jnp.abs(idx[:, None] - idx[None, :])
    m_intra = jnp.exp(log_g[:, None, None] * dist[None]) * (ci[None, :] <= ci[:, None])[None].astype(F32)
    lg_lane = jnp.repeat(log_g, RET_DK)
    d_q = jnp.exp(lg_lane[None, :] * (idx[:, None] + 1.0))
    d_k = jnp.exp(lg_lane[None, :] * (B - 1.0 - idx[:, None]))
    d_s = jnp.broadcast_to(jnp.exp(lg_lane * B)[:, None], (QK_W, V_W))
    bm = (jnp.arange(QK_W)[:, None] // RET_DK == jnp.arange(V_W)[None, :] // RET_DV).astype(F32)
    win = jnp.repeat(jnp.array(POOL_WINDOWS, F32), POOL_GC)
    invc = 1.0 / jnp.minimum(jnp.arange(S, dtype=F32)[:, None] + 1.0, win[None, :])
    return dict(cos=cos, sin=sin, m=m_intra, dq=d_q, dk=d_k, ds=d_s, bm=bm, invc=invc)


def _swap_halves(x):
    w = x.shape[1]
    lane = lax.broadcasted_iota(jnp.int32, x.shape, 1)
    return jnp.where((lane % RET_DK) < RET_DK // 2, pltpu.roll(x, w - RET_DK // 2, 1), pltpu.roll(x, RET_DK // 2, 1))


def _rot(x, cos, sin):
    return x * cos + _swap_halves(x) * sin


def _rot_t(d, cos, sin):
    return d * cos + _swap_halves(d * sin)


def _lane_groups(parts):
    return jnp.concatenate([p[:, POOL_GC * g:POOL_GC * (g + 1)] for g, p in enumerate(parts)], axis=1)


def _head_mask(shape, h):
    lane = lax.broadcasted_iota(jnp.int32, shape, 1)
    return (lane // RET_DK) == h


def _group_norm(o):
    yh, rs = [], []
    for h in range(RET_HEADS):
        oh = o[:, RET_DV * h:RET_DV * (h + 1)]
        xc = oh - jnp.mean(oh, axis=-1, keepdims=True)
        r = lax.rsqrt(jnp.mean(xc * xc, axis=-1, keepdims=True) + GN_EPS)
        yh.append(xc * r)
        rs.append(r)
    return yh, rs


def mixer_fwd(p, x1, consts, gn_gain, pool_w, pool_scale, wout, n_seq, S, name):
    _, T, _ = p.shape
    D = x1.shape[1]
    B = RET_BLOCK
    nb = S // B

    def body(p_ref, x1_ref, cos_ref, sin_ref, invc_ref, m_ref, dq_ref, dk_ref, ds_ref, bm_ref,
             gain_ref, pw_ref, sc_ref, wout_ref,
             x2_ref, m_out, o_out, st_out, pooled_out, state, prev_u):
        blk = pl.program_id(1)

        @pl.when(blk == 0)
        def _():
            state[...] = jnp.zeros_like(state)
            prev_u[...] = jnp.zeros_like(prev_u)

        qk = p_ref[0]
        v = p_ref[1]
        g = p_ref[2]
        u = p_ref[3]
        cos = cos_ref[...]
        sin = sin_ref[...]
        qr = _rot(qk[:, :QK_W], cos, sin) * (RET_DK ** -0.5)
        kr = _rot(qk[:, QK_W:], cos, sin)
        qb = qr.astype(BF16)
        kb = kr.astype(BF16)
        vb = v.astype(BF16)
        st = state[...]
        st_out[...] = st
        cross = _dot((qr * dq_ref[...]).astype(BF16), st.astype(BF16))
        o_parts = []
        for h in range(RET_HEADS):
            qm = jnp.where(_head_mask(qb.shape, h), qb, jnp.zeros_like(qb))
            sc = (_dot(qm, kb, NT) * m_ref[h]).astype(BF16)
            o_parts.append(_dot(sc, vb[:, RET_DV * h:RET_DV * (h + 1)]) + cross[:, RET_DV * h:RET_DV * (h + 1)])
        o = jnp.concatenate(o_parts, axis=1)
        o_out[...] = o
        kv = _dot((kr * dk_ref[...]).astype(BF16), vb, TN)
        state[...] = st * ds_ref[...] + kv * bm_ref[...]

        yh, _ = _group_norm(o)
        r = g * _sigmoid(g) * (jnp.concatenate(yh, axis=1) * gain_ref[...])

        ext = jnp.concatenate([prev_u[...], u], axis=0)
        sums = []
        run = ext
        for k in (1, 2, 4, 8):
            run = run + pltpu.roll(run, k, 0)
            sums.append(run[B:, :])
        prev_u[...] = u
        pooled = (_lane_groups(sums) * invc_ref[...] - u).astype(BF16)
        pooled_out[...] = pooled
        yp = [_dot(pooled[:, POOL_GC * gi:POOL_GC * (gi + 1)], pw_ref[gi].astype(BF16)) for gi in range(len(POOL_WINDOWS))]
        s = jnp.concatenate(yp, axis=1) * sc_ref[...]
        m = jnp.concatenate([r, s], axis=1).astype(BF16)
        m_out[...] = m
        x2_ref[...] = x1_ref[...] + _dot(m, wout_ref[...])

    tokmap = lambda b, k: (b * nb + k, 0)
    seqmap = lambda b, k: (k, 0)
    const2 = lambda b, k: (0, 0)
    const3 = lambda b, k: (0, 0, 0)
    return pl.pallas_call(
        body, name=name, grid=(n_seq, nb),
        in_specs=[pl.BlockSpec((4, B, V_W), lambda b, k: (0, b * nb + k, 0)),
                  pl.BlockSpec((B, D), tokmap),
                  pl.BlockSpec((B, QK_W), seqmap), pl.BlockSpec((B, QK_W), seqmap), pl.BlockSpec((B, V_W), seqmap),
                  pl.BlockSpec((RET_HEADS, B, B), const3),
                  pl.BlockSpec((B, QK_W), const2), pl.BlockSpec((B, QK_W), const2),
                  pl.BlockSpec((QK_W, V_W), const2), pl.BlockSpec((QK_W, V_W), const2),
                  pl.BlockSpec((1, V_W), const2), pl.BlockSpec((4, POOL_GC, POOL_GC), const3),
                  pl.BlockSpec((1, V_W), const2), pl.BlockSpec((D, D), const2)],
        out_specs=[pl.BlockSpec((B, D), tokmap), pl.BlockSpec((B, D), tokmap), pl.BlockSpec((B, V_W), tokmap),
                   pl.BlockSpec((None, QK_W, V_W), lambda b, k: (b * nb + k, 0, 0)),
                   pl.BlockSpec((B, V_W), tokmap)],
        out_shape=[jax.ShapeDtypeStruct((T, D), F32), jax.ShapeDtypeStruct((T, D), BF16),
                   jax.ShapeDtypeStruct((T, V_W), F32), jax.ShapeDtypeStruct((T // B, QK_W, V_W), F32),
                   jax.ShapeDtypeStruct((T, V_W), BF16)],
        scratch_shapes=[pltpu.VMEM((QK_W, V_W), F32), pltpu.VMEM((B, V_W), F32)],
        compiler_params=_params(2),
    )(p, x1, consts["cos"], consts["sin"], consts["invc"], consts["m"], consts["dq"], consts["dk"],
      consts["ds"], consts["bm"], gn_gain, pool_w, pool_scale, wout)


def mixer_bwd(dx3, dn3, x2, g3, x1, p, o_s, st_s, pooled_s, consts, gmix, gn_gain, pool_w, pool_scale, wout, win,
              n_seq, S, name, plan=None):
    T, D = x1.shape
    B = RET_BLOCK
    nb = S // B
    NG = len(POOL_WINDOWS)

    def body(dx3_ref, dn3_ref, x2_ref, g3_ref, x1_ref, p_ref, o_ref, st_ref, pooled_ref, cos_ref, sin_ref, invc_ref,
             m_ref, dq_ref, dk_ref, ds_ref, bm_ref, gmix_ref, gain_ref, pw_ref, sc_ref, wout_ref, win_ref,
             dx1_ref, df1_ref, dx2_ref, dp_ref, n2_ref, dg3_ref, dgmix_ref, dgain_ref, dscale_ref, dpw_ref,
             gstate, next_e):
        b = pl.program_id(0)
        blk = pl.program_id(1)

        @pl.when(jnp.logical_and(b == 0, blk == 0))
        def _():
            dg3_ref[...] = jnp.zeros_like(dg3_ref)
            dgmix_ref[...] = jnp.zeros_like(dgmix_ref)
            dgain_ref[...] = jnp.zeros_like(dgain_ref)
            dscale_ref[...] = jnp.zeros_like(dscale_ref)
            dpw_ref[...] = jnp.zeros_like(dpw_ref)

        @pl.when(blk == 0)
        def _():
            gstate[...] = jnp.zeros_like(gstate)
            next_e[...] = jnp.zeros_like(next_e)

        r2, xhat2 = _rms_parts(x2_ref[...])
        dn3 = dn3_ref[...]
        dg3_ref[...] += jnp.sum(dn3 * xhat2, axis=0, keepdims=True)
        dx2v = _rms_bwd(dn3, xhat2, r2, g3_ref[...], dx3_ref[...])
        dx2b = dx2v.astype(BF16)
        dx2_ref[...] = dx2b
        dm = _dot(dx2b, wout_ref[...], NT)
        dr = dm[:, :V_W]
        dsv = dm[:, V_W:]

        pooled = pooled_ref[...]
        scale = sc_ref[...]
        dyp = (dsv * scale).astype(BF16)
        yp, dpl = [], []
        for gi in range(NG):
            sl = slice(POOL_GC * gi, POOL_GC * (gi + 1))
            pwb = pw_ref[gi].astype(BF16)
            yp.append(_dot(pooled[:, sl], pwb))
            dpw_ref[gi] += _dot(pooled[:, sl], dyp[:, sl], TN)
            dpl.append(_dot(dyp[:, sl], pwb, NT))
        dscale_ref[...] += jnp.sum(dsv * jnp.concatenate(yp, axis=1), axis=0, keepdims=True)
        dpooled = jnp.concatenate(dpl, axis=1)
        e = dpooled * invc_ref[...]
        ext = jnp.concatenate([e, next_e[...]], axis=0)
        sums = []
        run = ext
        for k in (1, 2, 4, 8):
            run = run + pltpu.roll(run, 2 * B - k, 0)
            sums.append(run[:B, :])
        next_e[...] = e
        du = _lane_groups(sums) - dpooled

        qk = p_ref[0]
        v = p_ref[1]
        g = p_ref[2]
        sg = _sigmoid(g)
        si = g * sg
        yh, rs = _group_norm(o_ref[...])
        yhat = jnp.concatenate(yh, axis=1)
        gain = gain_ref[...]
        dg = dr * (yhat * gain) * (sg * (1.0 + g * (1.0 - sg)))
        dy = dr * si
        dgain_ref[...] += jnp.sum(dy * yhat, axis=0, keepdims=True)
        dyh = dy * gain
        do_parts = []
        for h in range(RET_HEADS):
            sl = slice(RET_DV * h, RET_DV * (h + 1))
            dh_ = dyh[:, sl]
            m1 = jnp.mean(dh_, axis=-1, keepdims=True)
            m2 = jnp.mean(dh_ * yh[h], axis=-1, keepdims=True)
            do_parts.append(rs[h] * (dh_ - m1 - yh[h] * m2))
        dob = jnp.concatenate(do_parts, axis=1).astype(BF16)

        cos = cos_ref[...]
        sin = sin_ref[...]
        qr = _rot(qk[:, :QK_W], cos, sin) * (RET_DK ** -0.5)
        kr = _rot(qk[:, QK_W:], cos, sin)
        qb = qr.astype(BF16)
        kb = kr.astype(BF16)
        vb = v.astype(BF16)
        dqd = dq_ref[...]
        dkd = dk_ref[...]
        stb = st_ref[...].astype(BF16)
        gs = gstate[...]
        gb = gs.astype(BF16)
        dqs = _dot(dob, stb, NT) * dqd
        dkr = _dot(vb, gb, NT) * dkd
        dv_cross = _dot((kr * dkd).astype(BF16), gb)
        ds_cross = _dot((qr * dqd).astype(BF16), dob, TN) * bm_ref[...]
        gstate[...] = ds_cross + gs * ds_ref[...]
        dv_parts = []
        for h in range(RET_HEADS):
            sl = slice(RET_DV * h, RET_DV * (h + 1))
            hm = _head_mask(qb.shape, h)
            qm = jnp.where(hm, qb, jnp.zeros_like(qb))
            mh = m_ref[h]
            sc = (_dot(qm, kb, NT) * mh).astype(BF16)
            dsc = (_dot(dob[:, sl], vb[:, sl], NT) * mh).astype(BF16)
            dqs = dqs + jnp.where(hm, _dot(dsc, kb), 0.0)
            dkr = dkr + jnp.where(hm, _dot(dsc, qb, TN), 0.0)
            dv_parts.append(_dot(sc, dob[:, sl], TN) + dv_cross[:, sl])
        dq = _rot_t(dqs * (RET_DK ** -0.5), cos, sin)
        dk = _rot_t(dkr, cos, sin)
        dp = [jnp.concatenate([dq, dk], axis=1).astype(BF16), jnp.concatenate(dv_parts, axis=1).astype(BF16),
              dg.astype(BF16), du.astype(BF16)]
        dn = jnp.zeros((B, D), F32)
        for jj in range(4):
            dp_ref[jj] = dp[jj]
            dn = dn + _dot(dp[jj], win_ref[jj], NT)

        x1v = x1_ref[...]
        r, xhat = _rms_parts(x1v)
        gm = gmix_ref[...]
        n2_ref[...] = (xhat * gm).astype(BF16)
        dgmix_ref[...] += jnp.sum(dn * xhat, axis=0, keepdims=True)
        dx1 = _rms_bwd(dn, xhat, r, gm, dx2v)
        dx1_ref[...] = dx1
        df1_ref[...] = (0.5 * dx1).astype(BF16)

    rev = lambda b, k: (b * nb + (nb - 1 - k), 0)
    seqrev = lambda b, k: (nb - 1 - k, 0)
    const2 = lambda b, k: (0, 0)
    const3 = lambda b, k: (0, 0, 0)
    return _call(
        body, plan, name=name, grid=(n_seq, nb),
        in_specs=[pl.BlockSpec((B, D), rev), pl.BlockSpec((B, D), rev), pl.BlockSpec((B, D), rev),
                  pl.BlockSpec((1, D), const2), pl.BlockSpec((B, D), rev),
                  pl.BlockSpec((4, B, V_W), lambda b, k: (0, b * nb + (nb - 1 - k), 0)),
                  pl.BlockSpec((B, V_W), rev),
                  pl.BlockSpec((None, QK_W, V_W), lambda b, k: (b * nb + (nb - 1 - k), 0, 0)),
                  pl.BlockSpec((B, V_W), rev),
                  pl.BlockSpec((B, QK_W), seqrev), pl.BlockSpec((B, QK_W), seqrev), pl.BlockSpec((B, V_W), seqrev),
                  pl.BlockSpec((RET_HEADS, B, B), const3),
                  pl.BlockSpec((B, QK_W), const2), pl.BlockSpec((B, QK_W), const2),
                  pl.BlockSpec((QK_W, V_W), const2), pl.BlockSpec((QK_W, V_W), const2),
                  pl.BlockSpec((1, D), const2), pl.BlockSpec((1, V_W), const2),
                  pl.BlockSpec((NG, POOL_GC, POOL_GC), const3), pl.BlockSpec((1, V_W), const2),
                  pl.BlockSpec((D, D), const2), pl.BlockSpec((4, D, V_W), const3)],
        out_specs=[pl.BlockSpec((B, D), rev), pl.BlockSpec((B, D), rev), pl.BlockSpec((B, D), rev),
                   pl.BlockSpec((4, B, V_W), lambda b, k: (0, b * nb + (nb - 1 - k), 0)),
                   pl.BlockSpec((B, D), rev),
                   pl.BlockSpec((1, D), const2), pl.BlockSpec((1, D), const2), pl.BlockSpec((1, V_W), const2),
                   pl.BlockSpec((1, V_W), const2), pl.BlockSpec((NG, POOL_GC, POOL_GC), const3)],
        out_shape=[jax.ShapeDtypeStruct((T, D), F32), jax.ShapeDtypeStruct((T, D), BF16),
                   jax.ShapeDtypeStruct((T, D), BF16), jax.ShapeDtypeStruct((4, T, V_W), BF16),
                   jax.ShapeDtypeStruct((T, D), BF16),
                   jax.ShapeDtypeStruct((1, D), F32), jax.ShapeDtypeStruct((1, D), F32), jax.ShapeDtypeStruct((1, V_W), F32),
                   jax.ShapeDtypeStruct((1, V_W), F32), jax.ShapeDtypeStruct((NG, POOL_GC, POOL_GC), F32)],
        scratch_shapes=[pltpu.VMEM((QK_W, V_W), F32), pltpu.VMEM((B, V_W), F32)],
        args=(dx3, dn3, x2, g3, x1, p, o_s, st_s, pooled_s, consts["cos"], consts["sin"], consts["invc"], consts["m"], consts["dq"],
              consts["dk"], consts["ds"], consts["bm"], gmix, gn_gain, pool_w, pool_scale, wout, win))


def loss_bwd(x3, tgt, gain, tm, name):
    T, D = x3.shape

    def body(x_ref, t_ref, g_ref, dx_ref, df_ref, lacc_ref, dg_ref):
        @pl.when(pl.program_id(0) == 0)
        def _():
            lacc_ref[...] = jnp.zeros_like(lacc_ref)
            dg_ref[...] = jnp.zeros_like(dg_ref)

        r, xhat = _rms_parts(x_ref[...])
        gv = g_ref[...]
        err = xhat * gv - t_ref[...]
        lacc_ref[...] += jnp.sum(err * err, axis=0, keepdims=True)
        dy = err * (1.0 / D)
        dg_ref[...] += jnp.sum(dy * xhat, axis=0, keepdims=True)
        dx = _rms_bwd(dy, xhat, r, gv, 0.0)
        dx_ref[...] = dx
        df_ref[...] = (0.5 * dx).astype(BF16)

    tok = pl.BlockSpec((tm, D), lambda i: (i, 0))
    vec = pl.BlockSpec((1, D), lambda i: (0, 0))
    return pl.pallas_call(
        body, name=name, grid=(T // tm,),
        in_specs=[tok, tok, vec], out_specs=[tok, tok, vec, vec],
        out_shape=[jax.ShapeDtypeStruct((T, D), F32), jax.ShapeDtypeStruct((T, D), BF16),
                   jax.ShapeDtypeStruct((1, D), F32), jax.ShapeDtypeStruct((1, D), F32)],
        compiler_params=_params(1),
    )(x3, tgt, gain)


def _adamw_math(w, g, m, v):
    m2 = ADAM_B1 * m + (1.0 - ADAM_B1) * g
    v2 = ADAM_B2 * v + (1.0 - ADAM_B2) * (g * g)
    m_hat = m2 / (1.0 - ADAM_B1 ** ADAM_STEP)
    v_hat = v2 / (1.0 - ADAM_B2 ** ADAM_STEP)
    return -ADAM_LR * (m_hat / (jnp.sqrt(v_hat) + ADAM_EPS) + ADAM_WD * w), m2, v2


def _place():
    x, y, c = lax.axis_index("x"), lax.axis_index("y"), lax.axis_index("c")
    other_chips = [(1 - x, y), (x, 1 - y), (1 - x, 1 - y)]
    return x, y, c, other_chips


def _exchange_plan(ins, out_shape, copies, n_copies):
    def descriptors(pins, pouts, psems):
        send, recv = psems
        return [pltpu.make_async_remote_copy(src_ref=s, dst_ref=d, send_sem=send.at[i], recv_sem=recv.at[i],
                                             device_id=dev, device_id_type=MESH)
                for i, (s, d, dev) in enumerate(copies(pins, pouts))]

    def start(pins, pouts, psems):
        for cp in descriptors(pins, pouts, psems):
            cp.start()

    def finish(pins, pouts, psems):
        for cp in descriptors(pins, pouts, psems):
            cp.wait()

    return CommPlan(tuple(ins), tuple(out_shape),
                    (pltpu.SemaphoreType.DMA((n_copies,)), pltpu.SemaphoreType.DMA((n_copies,))), start, finish)


def _combine(a, b):
    assert a.mid is None and b.mid is None
    ni, no, ns = len(a.ins), len(a.out_shape), len(a.sems)

    def start(pins, pouts, psems):
        a.start(pins[:ni], pouts[:no], psems[:ns])
        b.start(pins[ni:], pouts[no:], psems[ns:])

    def finish(pins, pouts, psems):
        a.finish(pins[:ni], pouts[:no], psems[:ns])
        b.finish(pins[ni:], pouts[no:], psems[ns:])

    return CommPlan(a.ins + b.ins, a.out_shape + b.out_shape, a.sems + b.sems, start, finish)


def gather_plan(shards):
    n = len(shards)

    def start(pins, pouts, psems):
        x, y, c, chips = _place()
        mine = 2 * x + y
        for w in range(n):
            for k, (px, py) in enumerate(chips):
                pltpu.make_async_remote_copy(
                    src_ref=pins[w].at[:, c], dst_ref=pouts[w].at[:, mine, c],
                    send_sem=psems[0].at[w, k], recv_sem=psems[1].at[w, k],
                    device_id=(px, py, c), device_id_type=MESH).start()

    def mid(pins, pouts, psems):
        x, y, c, chips = _place()
        for w in range(n):
            for k, (px, py) in enumerate(chips):
                landed = pouts[w].at[:, 2 * px + py, c]
                pltpu.make_async_remote_copy(
                    src_ref=landed, dst_ref=landed, send_sem=psems[0].at[w, k], recv_sem=psems[1].at[w, k],
                    device_id=(px, py, c), device_id_type=MESH).wait_recv()
                pltpu.make_async_remote_copy(
                    src_ref=landed, dst_ref=landed, send_sem=psems[2].at[w, k], recv_sem=psems[3].at[w, k],
                    device_id=(x, y, 1 - c), device_id_type=MESH).start()

    def finish(pins, pouts, psems):
        x, y, c, chips = _place()
        mine = 2 * x + y
        for w in range(n):
            for k, (px, py) in enumerate(chips):
                landed = pouts[w].at[:, 2 * px + py, 1 - c]
                cp = pltpu.make_async_remote_copy(
                    src_ref=landed, dst_ref=landed, send_sem=psems[2].at[w, k], recv_sem=psems[3].at[w, k],
                    device_id=(x, y, 1 - c), device_id_type=MESH)
                cp.wait_recv()
                cp.wait_send()
                pltpu.make_async_remote_copy(
                    src_ref=pins[w].at[:, c], dst_ref=pouts[w].at[:, mine, c],
                    send_sem=psems[0].at[w, k], recv_sem=psems[1].at[w, k],
                    device_id=(px, py, c), device_id_type=MESH).wait_send()

    return CommPlan(tuple(shards),
                    tuple(jax.ShapeDtypeStruct((s.shape[0], N_CHIPS) + s.shape[1:], s.dtype) for s in shards),
                    tuple(pltpu.SemaphoreType.DMA((n, 3)) for _ in range(4)), start, finish, mid)


def place_own(gathered, shard, name):
    L, _, Rh, C = shard.shape

    def body(chip_ref, g_ref, s_ref, o_ref):
        o_ref[...] = s_ref[...]

    return pl.pallas_call(
        body, name=name,
        grid_spec=pltpu.PrefetchScalarGridSpec(
            num_scalar_prefetch=1, grid=(L,),
            in_specs=[ANY, pl.BlockSpec((None, 2, Rh, C), lambda l, chip_ref: (l, 0, 0, 0))],
            out_specs=pl.BlockSpec((None, None, 2, Rh, C), lambda l, chip_ref: (l, chip_ref[0], 0, 0, 0))),
        out_shape=jax.ShapeDtypeStruct(gathered.shape, gathered.dtype),
        input_output_aliases={1: 0},
        compiler_params=_params(1),
    )(_chip_index(), gathered, shard)


def gather_weights(shards):
    n = len(shards)

    def body(*refs):
        ins, outs, bufs = refs[:n], refs[n:2 * n], refs[2 * n:3 * n]
        ld_sem, st_sem, ici_send, ici_recv, d2d_send, d2d_recv = refs[3 * n:]
        x, y, c, chips = _place()
        sibling = (x, y, 1 - c)
        mine = 2 * x + y
        loads = [pltpu.make_async_copy(ins[w], bufs[w], ld_sem.at[w]) for w in range(n)]
        for cp in loads:
            cp.start()
        stores, sends = [], []
        for w in range(n):
            loads[w].wait()
            cp = pltpu.make_async_copy(bufs[w], outs[w].at[:, mine], st_sem.at[w])
            cp.start()
            stores.append(cp)
            for k, (px, py) in enumerate(chips):
                cp = pltpu.make_async_remote_copy(
                    src_ref=bufs[w].at[:, c], dst_ref=outs[w].at[:, mine, c],
                    send_sem=ici_send.at[w, k], recv_sem=ici_recv.at[w, k],
                    device_id=(px, py, c), device_id_type=MESH)
                cp.start()
                sends.append(cp)
        for w in range(n):
            for k, (px, py) in enumerate(chips):
                landed = outs[w].at[:, 2 * px + py, c]
                pltpu.make_async_remote_copy(
                    src_ref=landed, dst_ref=landed, send_sem=ici_send.at[w, k], recv_sem=ici_recv.at[w, k],
                    device_id=(px, py, c), device_id_type=MESH).wait_recv()
                cp = pltpu.make_async_remote_copy(
                    src_ref=landed, dst_ref=landed, send_sem=d2d_send.at[w, k], recv_sem=d2d_recv.at[w, k],
                    device_id=sibling, device_id_type=MESH)
                cp.start()
                sends.append(cp)
        for w in range(n):
            for k, (px, py) in enumerate(chips):
                landed = outs[w].at[:, 2 * px + py, 1 - c]
                pltpu.make_async_remote_copy(
                    src_ref=landed, dst_ref=landed, send_sem=d2d_send.at[w, k], recv_sem=d2d_recv.at[w, k],
                    device_id=sibling, device_id_type=MESH).wait_recv()
        for cp in sends:
            cp.wait_send()
        for cp in stores:
            cp.wait()

    return pl.pallas_call(
        body, name="gather_weights",
        in_specs=[ANY] * n, out_specs=[ANY] * n,
        out_shape=[jax.ShapeDtypeStruct((s.shape[0], N_CHIPS) + s.shape[1:], s.dtype) for s in shards],
        scratch_shapes=[pltpu.VMEM(s.shape, s.dtype) for s in shards] +
                       [pltpu.SemaphoreType.DMA((n,)), pltpu.SemaphoreType.DMA((n,)),
                        pltpu.SemaphoreType.DMA((n, 3)), pltpu.SemaphoreType.DMA((n, 3)),
                        pltpu.SemaphoreType.DMA((n, 3)), pltpu.SemaphoreType.DMA((n, 3))],
        compiler_params=pltpu.CompilerParams(vmem_limit_bytes=VMEM_LIMIT),
    )(*shards)


def send_to_sibling_other_half(grads):
    def copies(ins, outs):
        x, y, c, _ = _place()
        return [(ins[w].at[:, :, 1 - c], outs[w], (x, y, 1 - c)) for w in range(len(grads))]

    return _exchange_plan(grads, [jax.ShapeDtypeStruct(g.shape[:2] + g.shape[3:], g.dtype) for g in grads], copies,
                          len(grads))


def _core_index():
    return jnp.reshape(lax.axis_index("c"), (1,)).astype(jnp.int32)


def _chip_index():
    return jnp.reshape(2 * lax.axis_index("x") + lax.axis_index("y"), (1,)).astype(jnp.int32)


def add_own_half(g, recv, name):
    L, J, _, Rh, C = g.shape

    def body(c_ref, g_ref, r_ref, o_ref):
        o_ref[...] = (g_ref[...] + r_ref[...]).astype(BF16)

    return pl.pallas_call(
        body, name=name,
        grid_spec=pltpu.PrefetchScalarGridSpec(
            num_scalar_prefetch=1, grid=(L, J),
            in_specs=[pl.BlockSpec((None, None, None, Rh, C), lambda l, j, c_ref: (l, j, c_ref[0], 0, 0)),
                      pl.BlockSpec((None, None, Rh, C), lambda l, j, c_ref: (l, j, 0, 0))],
            out_specs=pl.BlockSpec((None, None, Rh, C), lambda l, j, c_ref: (l, j, 0, 0))),
        out_shape=jax.ShapeDtypeStruct((L, J, Rh, C), BF16),
        compiler_params=_params(2),
    )(_core_index(), g, recv)


def exchange_between_chips(sums):
    def copies(ins, outs):
        x, y, c, chips = _place()
        return [(ins[w].at[:, 2 * px + py], outs[w].at[k], (px, py, c))
                for w in range(len(sums)) for k, (px, py) in enumerate(chips)]

    return _exchange_plan(sums, [jax.ShapeDtypeStruct((3, s.shape[0]) + s.shape[2:], s.dtype) for s in sums], copies,
                          3 * len(sums))


def sum_chips(own, recv, name):
    L, _, Rh, C = own.shape

    def body(chip_ref, o_ref, r_ref, out_ref):
        acc = o_ref[...].astype(F32)
        for k in range(3):
            acc = acc + r_ref[k].astype(F32)
        out_ref[...] = acc

    return pl.pallas_call(
        body, name=name,
        grid_spec=pltpu.PrefetchScalarGridSpec(
            num_scalar_prefetch=1, grid=(L,),
            in_specs=[pl.BlockSpec((None, None, Rh, C), lambda l, chip_ref: (l, chip_ref[0], 0, 0)),
                      pl.BlockSpec((3, None, Rh, C), lambda l, chip_ref: (0, l, 0, 0))],
            out_specs=pl.BlockSpec((None, Rh, C), lambda l, chip_ref: (l, 0, 0))),
        out_shape=jax.ShapeDtypeStruct((L, Rh, C), F32),
        compiler_params=_params(1),
    )(_chip_index(), own, recv)


def share_with_sibling(halves):
    def copies(ins, outs):
        x, y, c, _ = _place()
        return [(ins[w], outs[w], (x, y, 1 - c)) for w in range(len(halves))]

    return _exchange_plan(halves, [jax.ShapeDtypeStruct(h.shape, h.dtype) for h in halves], copies, len(halves))


def adamw_group(ws, ms, vs, own, sib, name, plan=None):
    L = len(ws)
    R, C = ws[0].shape
    Rh = R // 2
    tr = Rh // 2
    nr = Rh // tr

    def body(*refs):
        w, m, v = refs[:L], refs[L:2 * L], refs[2 * L:3 * L]
        own_ref, sib_ref = refs[3 * L], refs[3 * L + 1]
        outs = refs[3 * L + 2:]
        mine = pl.program_id(0) == lax.axis_index("c")
        for l in range(L):
            gv = jnp.where(mine, own_ref[l], sib_ref[l])
            d, m2, v2 = _adamw_math(w[l][...], gv, m[l][...], v[l][...])
            outs[4 * l][...] = d
            outs[4 * l + 1][...] = m2
            outs[4 * l + 2][...] = v2
            outs[4 * l + 3][...] = gv

    blk = pl.BlockSpec((tr, C), lambda h, r: (h * nr + r, 0))
    half = pl.BlockSpec((L, tr, C), lambda h, r: (0, r, 0))
    sds = jax.ShapeDtypeStruct((R, C), F32)
    return _call(body, plan, name=name, grid=(2, nr), in_specs=[blk] * (3 * L) + [half, half],
                 out_specs=[blk] * (4 * L), out_shape=[sds] * (4 * L), args=(*ws, *ms, *vs, own, sib))


def gather_small_plan(part):
    def copies(ins, outs):
        x, y, c, _ = _place()
        me = 4 * x + 2 * y + c
        return [(ins[0], outs[0].at[me], (x ^ ((k >> 2) & 1), y ^ ((k >> 1) & 1), c ^ (k & 1))) for k in range(1, N_DEV)]

    return _exchange_plan([part], [jax.ShapeDtypeStruct((N_DEV,) + part.shape, part.dtype)], copies, N_DEV - 1)


def sum_small(part, gathered, name):
    R, C = part.shape

    def body(p_ref, g_ref, o_ref):
        d = pl.program_id(0)
        me = 4 * lax.axis_index("x") + 2 * lax.axis_index("y") + lax.axis_index("c")
        val = jnp.where(d == me, p_ref[...], g_ref[...])

        @pl.when(d == 0)
        def _():
            o_ref[...] = val

        @pl.when(d > 0)
        def _():
            o_ref[...] += val

    return pl.pallas_call(
        body, name=name, grid=(N_DEV,),
        in_specs=[pl.BlockSpec((R, C), lambda d: (0, 0)), pl.BlockSpec((None, R, C), lambda d: (d, 0, 0))],
        out_specs=pl.BlockSpec((R, C), lambda d: (0, 0)),
        out_shape=jax.ShapeDtypeStruct((R, C), F32),
        compiler_params=_params(1),
    )(part, gathered)


SMALL = ("norm_ffn1", "norm_mix", "norm_ffn2", "norm_final", "ret_gn_gain", "pool_scale", "pool_w")
GROUPS = (("ffn1", ("ffn1_gate", "ffn1_up", "ffn1_down")), ("w_in", ("w_in",)), ("w_out", ("w_out",)),
          ("ffn2", ("ffn2_gate", "ffn2_up", "ffn2_down")))
TRANSPOSED = ("ffn1_gate", "ffn1_up", "ffn2_gate", "ffn2_up")
IN_W = 2 * QK_W + 3 * V_W
ORDER = ("norm_ffn1", "ffn1_gate", "ffn1_up", "ffn1_down", "norm_mix", "w_in", "ret_gn_gain", "pool_w", "pool_scale",
         "w_out", "norm_ffn2", "ffn2_gate", "ffn2_up", "ffn2_down", "norm_final")


SUBLANES = 8


def _small_rows(shapes):
    rows = [math.prod(shapes[k]) // 128 for k in SMALL]
    offs, off = [], 0
    for r in rows:
        offs.append(off)
        off += -(-r // SUBLANES) * SUBLANES
    return rows, offs, off


def _pack_small(d, shapes):
    rows, offs, _ = _small_rows(shapes)
    pieces = []
    for k, r in zip(SMALL[1:], rows[1:]):
        pieces.append(d[k].reshape(r, 128))
        if r % SUBLANES:
            pieces.append(jnp.zeros((SUBLANES - r % SUBLANES, 128), F32))
    return jnp.concatenate(pieces, axis=0)


def adamw_small(W, M, V, first, summed, shapes):
    rows, offs, _ = _small_rows(shapes)
    n = len(SMALL)

    def body(*refs):
        w, m, v, f_ref, s_ref = refs[:n], refs[n:2 * n], refs[2 * n:3 * n], refs[3 * n], refs[3 * n + 1]
        outs = refs[3 * n + 2:]
        for i, (r, off) in enumerate(zip(rows, offs)):
            gv = f_ref[...] if i == 0 else s_ref[off - offs[1]:off - offs[1] + r, :]
            d, m2, v2 = _adamw_math(w[i][...], gv, m[i][...], v[i][...])
            outs[4 * i][...] = d
            outs[4 * i + 1][...] = m2
            outs[4 * i + 2][...] = v2
            outs[4 * i + 3][...] = gv

    vm = pl.BlockSpec(memory_space=pltpu.VMEM)
    flat = lambda d: [d[k].reshape(r, 128) for k, r in zip(SMALL, rows)]
    res = pl.pallas_call(
        body, name="adamw_small", in_specs=[vm] * (3 * n + 2), out_specs=[vm] * (4 * n),
        out_shape=[jax.ShapeDtypeStruct((r, 128), F32) for r in rows for _ in range(4)],
    )(*flat(W), *flat(M), *flat(V), first, summed)
    return {k: [a.reshape(shapes[k]) for a in res[4 * i:4 * i + 4]] for i, k in enumerate(SMALL)}


def kernel(x, norm_ffn1, ffn1_gate, ffn1_up, ffn1_down, norm_mix, w_in, ret_gn_gain, pool_w, pool_scale, w_out, norm_ffn2, ffn2_gate, ffn2_up, ffn2_down, norm_final, loss_target, m_norm_ffn1, m_ffn1_gate, m_ffn1_up, m_ffn1_down, m_norm_mix, m_w_in, m_ret_gn_gain, m_pool_w, m_pool_scale, m_w_out, m_norm_ffn2, m_ffn2_gate, m_ffn2_up, m_ffn2_down, m_norm_final, v_norm_ffn1, v_ffn1_gate, v_ffn1_up, v_ffn1_down, v_norm_mix, v_w_in, v_ret_gn_gain, v_pool_w, v_pool_scale, v_w_out, v_norm_ffn2, v_ffn2_gate, v_ffn2_up, v_ffn2_down, v_norm_final):
    W = dict(norm_ffn1=norm_ffn1, ffn1_gate=ffn1_gate, ffn1_up=ffn1_up, ffn1_down=ffn1_down, norm_mix=norm_mix, w_in=w_in,
             ret_gn_gain=ret_gn_gain, pool_w=pool_w, pool_scale=pool_scale, w_out=w_out, norm_ffn2=norm_ffn2,
             ffn2_gate=ffn2_gate, ffn2_up=ffn2_up, ffn2_down=ffn2_down, norm_final=norm_final)
    M = dict(norm_ffn1=m_norm_ffn1, ffn1_gate=m_ffn1_gate, ffn1_up=m_ffn1_up, ffn1_down=m_ffn1_down, norm_mix=m_norm_mix,
             w_in=m_w_in, ret_gn_gain=m_ret_gn_gain, pool_w=m_pool_w, pool_scale=m_pool_scale, w_out=m_w_out,
             norm_ffn2=m_norm_ffn2, ffn2_gate=m_ffn2_gate, ffn2_up=m_ffn2_up, ffn2_down=m_ffn2_down, norm_final=m_norm_final)
    V = dict(norm_ffn1=v_norm_ffn1, ffn1_gate=v_ffn1_gate, ffn1_up=v_ffn1_up, ffn1_down=v_ffn1_down, norm_mix=v_norm_mix,
             w_in=v_w_in, ret_gn_gain=v_ret_gn_gain, pool_w=v_pool_w, pool_scale=v_pool_scale, w_out=v_w_out,
             norm_ffn2=v_norm_ffn2, ffn2_gate=v_ffn2_gate, ffn2_up=v_ffn2_up, ffn2_down=v_ffn2_down, norm_final=v_norm_final)

    n_seq, S, D = x.shape
    T = n_seq * S
    tm = min(512, T // 2)
    tm_fwd = min(1024, T // 2)
    tk = min(1024, T // 2)
    xf = x.reshape(T, D)
    tgt = loss_target.reshape(T, D)

    def local(d, k):
        a = d[k][0]
        return a.T if k in TRANSPOSED else a

    def to_out(k, a):
        return (a.T if k in TRANSPOSED else a)[None]

    loc_bf = []
    for _, members in GROUPS:
        st = jnp.stack([local(W, k).astype(BF16) for k in members])
        loc_bf.append(st.reshape(st.shape[0], 2, st.shape[1] // 2, st.shape[2]))
    FS = loc_bf[0].shape[2] * 2
    w3_1 = gather_weights(loc_bf[:1])[0].reshape(3, N_CHIPS, FS, D)
    g1 = norm_ffn1.reshape(1, D)
    gm = norm_mix.reshape(1, D)
    g3 = norm_ffn2.reshape(1, D)
    gf = norm_final.reshape(1, D)
    gn_gain = ret_gn_gain.reshape(1, V_W)
    pscale = pool_scale.reshape(1, V_W)
    pw = pool_w.reshape(len(POOL_WINDOWS), POOL_GC, POOL_GC)
    consts = mixer_constants(S)

    def halves5(g, lb):
        return g.reshape(lb.shape[0], N_CHIPS, 2, lb.shape[2], lb.shape[3])

    (x1, b1, si1, t1, h1, n1), landed = ffn_fwd(xf, g1, w3_1, tm_fwd, "ffn1_fwd", plan=gather_plan(loc_bf[1:]))
    gathered = [place_own(g, lb, "place_own_" + gn) for g, lb, (gn, _) in zip(landed, loc_bf[1:], GROUPS[1:])]
    win_full = gathered[0].reshape(N_CHIPS, D, IN_W // N_CHIPS)
    wout_full = gathered[1].reshape(D, D)
    w3_2 = gathered[2].reshape(3, N_CHIPS, FS, D)
    p = in_proj(x1, gm, win_full, tm, "in_proj")
    x2, m_s, o_s, st_s, pooled_s = mixer_fwd(p, x1, consts, gn_gain, pw, pscale, wout_full, n_seq, S, "mixer_fwd")
    x3, b2, si2, t2, h2, n3 = ffn_fwd(x2, g3, w3_2, tm_fwd, "ffn2_fwd")

    dx3, df2, lacc, dgf = loss_bwd(x3, tgt, gf, tm, "loss_bwd")
    dn3, da2, db2 = ffn_bwd(df2, b2, si2, t2, w3_2, tm_fwd, "ffn2_bwd")
    to_sib, to_chips, back = send_to_sibling_other_half, exchange_between_chips, share_with_sibling
    out_g, out_d, out_m, out_v = {}, {}, {}, {}

    def update(members, own, sib, name):
        res = adamw_group([local(W, k) for k in members], [local(M, k) for k in members], [local(V, k) for k in members],
                          own, sib, "adamw_" + name)
        for l, k in enumerate(members):
            out_d[k], out_m[k], out_v[k], out_g[k] = (to_out(k, r) for r in res[4 * l:4 * l + 4])

    (gr_ffn2,) = ffn_wgrad(da2, db2, h2, n3, df2, tk, "ffn2_wgrad")
    g_ffn2 = halves5(gr_ffn2, loc_bf[3])
    (dx1, df1, dx2b, dp, n2, dg3, dgm, dgn, dps, dpw), (sb_ffn2,) = mixer_bwd(
        dx3, dn3, x2, g3, x1, p, o_s, st_s, pooled_s, consts, gm, gn_gain, pw, pscale, wout_full, win_full, n_seq, S,
        "mixer_bwd", plan=to_sib([g_ffn2]))
    cs_ffn2 = add_own_half(g_ffn2, sb_ffn2, "add_sibling_ffn2")
    (gr_wout,) = tn_shared(m_s, dx2b[None], tk, "dw_out")
    (gr_win,) = tn_shared(n2, dp, tk, "dw_in")
    g_mix = [halves5(gr_win, loc_bf[1]), halves5(gr_wout, loc_bf[2])]
    shapes = {k: W[k].shape for k in SMALL}
    small_part = dict(norm_mix=dgm, norm_ffn2=dg3, norm_final=dgf, ret_gn_gain=dgn, pool_scale=dps, pool_w=dpw)
    part = jnp.concatenate([_pack_small(small_part, shapes), lacc.reshape(-1, 128)], axis=0)
    n_small = part.shape[0] - lacc.size // 128
    (dn1, da1, db1), (pc_ffn2, sb_win, sb_wout, parts) = ffn_bwd(
        df1, b1, si1, t1, w3_1, tm_fwd, "ffn1_bwd",
        plan=_combine(_combine(to_chips([cs_ffn2]), to_sib(g_mix)), gather_small_plan(part)))
    summed = sum_small(part, parts, "sum_small")
    loss = jnp.sum(summed[n_small:]) * (0.5 / D)
    mh_ffn2 = sum_chips(cs_ffn2, pc_ffn2, "sum_chips_ffn2")
    cs_mix = [add_own_half(g, r, "add_sibling_" + gn) for g, r, gn in zip(g_mix, (sb_win, sb_wout), ("w_in", "w_out"))]
    def one(g):
        return g.reshape((1, N_CHIPS, 2) + loc_bf[0].shape[2:])

    tk1 = min(2 * tk, T // 2)
    (gr_g,), (sh_ffn2, pc_win, pc_wout) = wgrad_one(
        da1, n1, tk1, "ffn1_wgrad_gate", plan=_combine(back([mh_ffn2]), to_chips(cs_mix)))
    mh_mix = [sum_chips(cs, pc, "sum_chips_" + gn) for cs, pc, gn in zip(cs_mix, (pc_win, pc_wout), ("w_in", "w_out"))]
    (gr_u,), (sb_g,) = wgrad_one(db1, n1, tk1, "ffn1_wgrad_up", plan=to_sib([one(gr_g)]))
    cs_g = add_own_half(one(gr_g), sb_g, "add_sibling_ffn1_gate")
    (gr_d,), (pc_g, sb_u) = wgrad_one(
        h1, df1, tk1, "ffn1_wgrad_down", plan=_combine(to_chips([cs_g]), to_sib([one(gr_u)])))
    mh_g = sum_chips(cs_g, pc_g, "sum_chips_ffn1_gate")
    cs_u = add_own_half(one(gr_u), sb_u, "add_sibling_ffn1_up")
    (dx0, dg1), (sh_g, sh_win, sh_wout, pc_u, sb_d) = rms_bwd(
        xf, dx1, dn1, g1, tm, "ffn1_rms_bwd",
        plan=_combine(_combine(back([mh_g] + mh_mix), to_chips([cs_u])), to_sib([one(gr_d)])))
    mh_u = sum_chips(cs_u, pc_u, "sum_chips_ffn1_up")
    cs_d = add_own_half(one(gr_d), sb_d, "add_sibling_ffn1_down")
    first = dg1.reshape(-1, 128)
    sh_u, pc_d, firsts = _run_plan(
        _combine(_combine(back([mh_u]), to_chips([cs_d])), gather_small_plan(first)), "ffn1_grads_stage_b")
    mh_d = sum_chips(cs_d, pc_d, "sum_chips_ffn1_down")
    (sh_d,) = _run_plan(back([mh_d]), "ffn1_grads_stage_c")
    first_sum = sum_small(first, firsts, "sum_small_first")
    for k, own, sib in zip(GROUPS[0][1], (mh_g, mh_u, mh_d), (sh_g, sh_u, sh_d)):
        update((k,), own, sib, k)
    update(GROUPS[1][1], mh_mix[0], sh_win, "w_in")
    update(GROUPS[2][1], mh_mix[1], sh_wout, "w_out")
    update(GROUPS[3][1], mh_ffn2, sh_ffn2, "ffn2")

    for k, (d_, m_, v_, g_) in adamw_small(W, M, V, first_sum, summed, shapes).items():
        out_d[k], out_m[k], out_v[k], out_g[k] = d_, m_, v_, g_

    grad_x = dx0.reshape(n_seq, S, D)
    return (loss, grad_x, *[out_g[k] for k in ORDER], *[out_d[k] for k in ORDER],
            *[out_m[k] for k in ORDER], *[out_v[k] for k in ORDER])
```

```python
import functools
import math
from typing import Callable, NamedTuple, Optional

import jax
import jax.numpy as jnp
from jax import lax
from jax.experimental import pallas as pl
from jax.experimental.pallas import tpu as pltpu

F32 = jnp.float32
BF16 = jnp.bfloat16
MESH = pl.DeviceIdType.MESH

N_CHIPS = 4
N_DEV = 8
CHUNK = 64
RET_HEADS = 4
RET_DK = 64
RET_DV = 128
QK_W = RET_HEADS * RET_DK
V_W = RET_HEADS * RET_DV
POOL_WINDOWS = (2, 4, 8, 16)
POOL_GC = 128
ROPE_BASE = 10000.0
RMS_EPS = 1e-6
GN_EPS = 1e-5
ADAM_LR = 0.001
ADAM_B1 = 0.9
ADAM_B2 = 0.999
ADAM_EPS = 1e-08
ADAM_WD = 0.01
ADAM_STEP = 10
MXU_W = 256
RET_BLOCK = MXU_W
VMEM_LIMIT = 56 * 1024 * 1024

NN = (((1,), (0,)), ((), ()))
NT = (((1,), (1,)), ((), ()))
TN = (((0,), (0,)), ((), ()))


def _dot(a, b, dims=NN):
    return lax.dot_general(a, b, dims, preferred_element_type=F32)


def _sigmoid(x):
    return 0.5 * jnp.tanh(0.5 * x) + 0.5


def _params(n_grid):
    return pltpu.CompilerParams(dimension_semantics=("arbitrary",) * n_grid, vmem_limit_bytes=VMEM_LIMIT)


class CommPlan(NamedTuple):
    ins: tuple
    out_shape: tuple
    sems: tuple
    start: Callable
    finish: Callable
    mid: Optional[Callable] = None
    mid_at: float = 0.85


ANY = pl.BlockSpec(memory_space=pl.ANY)


def _call(body, plan, *, name, grid, in_specs, out_specs, out_shape, args, scratch_shapes=()):
    n_grid = len(grid)
    if plan is None:
        return pl.pallas_call(body, name=name, grid=grid, in_specs=in_specs, out_specs=out_specs, out_shape=out_shape,
                              scratch_shapes=scratch_shapes, compiler_params=_params(n_grid))(*args)
    n_in, n_out, n_sc = len(in_specs), len(out_specs), len(scratch_shapes)
    p_in, p_out = len(plan.ins), len(plan.out_shape)
    total = math.prod(grid)
    mid_step = min(total - 1, int(plan.mid_at * total))

    def riding(*refs):
        ins, pins = refs[:n_in], refs[n_in:n_in + p_in]
        o0 = n_in + p_in
        outs, pouts = refs[o0:o0 + n_out], refs[o0 + n_out:o0 + n_out + p_out]
        s0 = o0 + n_out + p_out
        scratch, psems = refs[s0:s0 + n_sc], refs[s0 + n_sc:]
        step = pl.program_id(0)
        for d in range(1, n_grid):
            step = step * grid[d] + pl.program_id(d)

        @pl.when(step == 0)
        def _():
            plan.start(pins, pouts, psems)

        body(*ins, *outs, *scratch)

        if plan.mid is not None:
            @pl.when(step == mid_step)
            def _():
                plan.mid(pins, pouts, psems)

        @pl.when(step == total - 1)
        def _():
            plan.finish(pins, pouts, psems)

    res = pl.pallas_call(
        riding, name=name, grid=grid, in_specs=list(in_specs) + [ANY] * p_in, out_specs=list(out_specs) + [ANY] * p_out,
        out_shape=list(out_shape) + list(plan.out_shape), scratch_shapes=list(scratch_shapes) + list(plan.sems),
        compiler_params=_params(n_grid))(*args, *plan.ins)
    return res[:n_out], res[n_out:]


def _run_plan(plan, name):
    p_in, p_out = len(plan.ins), len(plan.out_shape)

    def body(*refs):
        pins, pouts, psems = refs[:p_in], refs[p_in:p_in + p_out], refs[p_in + p_out:]
        plan.start(pins, pouts, psems)
        if plan.mid is not None:
            plan.mid(pins, pouts, psems)
        plan.finish(pins, pouts, psems)

    return pl.pallas_call(body, name=name, in_specs=[ANY] * p_in, out_specs=[ANY] * p_out,
                          out_shape=list(plan.out_shape), scratch_shapes=list(plan.sems))(*plan.ins)


def _rms_parts(x):
    r = lax.rsqrt(jnp.mean(x * x, axis=-1, keepdims=True) + RMS_EPS)
    return r, x * r


def _rms_bwd(dn, xhat, r, gain, dres):
    dxh = dn * gain
    return dres + r * (dxh - xhat * jnp.mean(dxh * xhat, axis=-1, keepdims=True))


def _w3_specs(FS, D):
    return [pl.BlockSpec((None, None, FS, D), functools.partial(lambda i, j, k: (k, j, 0, 0), k=k)) for k in range(3)]


def ffn_fwd(x, gain, w3, tm, name, plan=None):
    T, D = x.shape
    _, J, FS, _ = w3.shape

    def body(x_ref, g_ref, wg_ref, wu_ref, wd_ref, xo_ref, b_ref, si_ref, t_ref, h_ref, n_ref, acc):
        j = pl.program_id(1)

        @pl.when(j == 0)
        def _():
            _, xhat = _rms_parts(x_ref[...])
            n_ref[...] = (xhat * g_ref[...]).astype(BF16)
            acc[...] = jnp.zeros_like(acc)

        n = n_ref[...]
        a = _dot(n, wg_ref[...], NT)
        b = _dot(n, wu_ref[...], NT)
        sg = _sigmoid(a)
        si = a * sg
        h = (si * b).astype(BF16)
        b_ref[...] = b.astype(BF16)
        si_ref[...] = si.astype(BF16)
        t_ref[...] = (sg + si * (1.0 - sg)).astype(BF16)
        h_ref[...] = h
        acc[...] += _dot(h, wd_ref[...])

        @pl.when(j == J - 1)
        def _():
            xo_ref[...] = x_ref[...] + 0.5 * acc[...]

    return _call(
        body, plan, name=name, grid=(T // tm, J),
        in_specs=[pl.BlockSpec((tm, D), lambda i, j: (i, 0)),
                  pl.BlockSpec((1, D), lambda i, j: (0, 0))] + _w3_specs(FS, D),
        out_specs=[pl.BlockSpec((tm, D), lambda i, j: (i, 0))] +
                  [pl.BlockSpec((None, tm, FS), lambda i, j: (j, i, 0))] * 4 +
                  [pl.BlockSpec((tm, D), lambda i, j: (i, 0))],
        out_shape=[jax.ShapeDtypeStruct((T, D), F32)] + [jax.ShapeDtypeStruct((J, T, FS), BF16)] * 4 +
                  [jax.ShapeDtypeStruct((T, D), BF16)],
        scratch_shapes=[pltpu.VMEM((tm, D), F32)],
        args=(x, gain, w3, w3, w3))


def ffn_bwd(df, b_s, si_s, t_s, w3, tm, name, plan=None):
    T, D = df.shape
    _, J, FS, _ = w3.shape

    chunks = [slice(s, min(s + MXU_W, FS)) for s in range(0, FS, MXU_W)]
    nc = len(chunks)

    def body(df_ref, b_ref, si_ref, t_ref, wg_ref, wu_ref, wd_ref, dn_ref, da_ref, db_ref):
        @pl.when(pl.program_id(1) == 0)
        def _():
            dn_ref[...] = jnp.zeros_like(dn_ref)

        def elementwise(dh, sl):
            da = (dh * b_ref[:, sl].astype(F32) * t_ref[:, sl].astype(F32)).astype(BF16)
            db = (dh * si_ref[:, sl].astype(F32)).astype(BF16)
            da_ref[:, sl] = da
            db_ref[:, sl] = db
            return da, db

        df = df_ref[...]
        dh, dab, acc = [None] * nc, [None] * nc, None
        for step in range(nc + 2):
            if step < nc:
                dh[step] = _dot(df, wd_ref[chunks[step], :], NT)
            if 1 <= step <= nc:
                dab[step - 1] = elementwise(dh[step - 1], chunks[step - 1])
            if step >= 2:
                sl = chunks[step - 2]
                part = _dot(dab[step - 2][0], wg_ref[sl, :]) + _dot(dab[step - 2][1], wu_ref[sl, :])
                acc = part if acc is None else acc + part
        dn_ref[...] += acc

    tok = pl.BlockSpec((tm, D), lambda i, j: (i, 0))
    sh = pl.BlockSpec((None, tm, FS), lambda i, j: (j, i, 0))
    return _call(
        body, plan, name=name, grid=(T // tm, J),
        in_specs=[tok, sh, sh, sh] + _w3_specs(FS, D),
        out_specs=[tok, sh, sh],
        out_shape=[jax.ShapeDtypeStruct((T, D), F32),
                   jax.ShapeDtypeStruct((J, T, FS), BF16),
                   jax.ShapeDtypeStruct((J, T, FS), BF16)],
        args=(df, b_s, si_s, t_s, w3, w3, w3))


def rms_bwd(x, dout, dn, gain, tm, name, plan=None):
    T, D = x.shape

    def body(x_ref, do_ref, dn_ref, g_ref, dx_ref, dgain_ref):
        @pl.when(pl.program_id(0) == 0)
        def _():
            dgain_ref[...] = jnp.zeros_like(dgain_ref)

        r, xhat = _rms_parts(x_ref[...])
        dn = dn_ref[...]
        dgain_ref[...] += jnp.sum(dn * xhat, axis=0, keepdims=True)
        dx_ref[...] = _rms_bwd(dn, xhat, r, g_ref[...], do_ref[...])

    tok = pl.BlockSpec((tm, D), lambda i: (i, 0))
    vec = pl.BlockSpec((1, D), lambda i: (0, 0))
    return _call(
        body, plan, name=name, grid=(T // tm,), in_specs=[tok, tok, tok, vec], out_specs=[tok, vec],
        out_shape=[jax.ShapeDtypeStruct((T, D), F32), jax.ShapeDtypeStruct((1, D), F32)],
        args=(x, dout, dn, gain))


def ffn_wgrad(da, db, h, n, df, tk, name, plan=None):
    J, T, FS = da.shape
    D = n.shape[1]

    def body(da_ref, db_ref, h_ref, n_ref, df_ref, o_ref):
        @pl.when(pl.program_id(1) == 0)
        def _():
            o_ref[...] = jnp.zeros_like(o_ref)

        nv = n_ref[...]
        o_ref[0] += _dot(da_ref[...], nv, TN)
        o_ref[1] += _dot(db_ref[...], nv, TN)
        o_ref[2] += _dot(h_ref[...], df_ref[...], TN)

    sh = pl.BlockSpec((None, tk, FS), lambda j, k: (j, k, 0))
    tok = pl.BlockSpec((tk, D), lambda j, k: (k, 0))
    return _call(
        body, plan, name=name, grid=(J, T // tk),
        in_specs=[sh, sh, sh, tok, tok],
        out_specs=[pl.BlockSpec((3, None, FS, D), lambda j, k: (0, j, 0, 0))],
        out_shape=[jax.ShapeDtypeStruct((3, J, FS, D), F32)],
        args=(da, db, h, n, df))


def wgrad_one(a, b, tk, name, plan=None):
    J, T, FS = a.shape
    D = b.shape[1]

    def body(a_ref, b_ref, o_ref):
        @pl.when(pl.program_id(1) == 0)
        def _():
            o_ref[...] = jnp.zeros_like(o_ref)

        o_ref[...] += _dot(a_ref[...], b_ref[...], TN)

    return _call(
        body, plan, name=name, grid=(J, T // tk),
        in_specs=[pl.BlockSpec((None, tk, FS), lambda j, k: (j, k, 0)), pl.BlockSpec((tk, D), lambda j, k: (k, 0))],
        out_specs=[pl.BlockSpec((None, FS, D), lambda j, k: (j, 0, 0))],
        out_shape=[jax.ShapeDtypeStruct((J, FS, D), F32)],
        args=(a, b))


def tn_shared(a, b, tk, name, plan=None):
    T, M = a.shape
    J, _, N = b.shape

    def body(a_ref, b_ref, o_ref):
        @pl.when(pl.program_id(0) == 0)
        def _():
            o_ref[...] = jnp.zeros_like(o_ref)

        a_t = a_ref[...].astype(BF16).T
        for j in range(J):
            o_ref[j] += _dot(a_t, b_ref[j].astype(BF16))

    return _call(
        body, plan, name=name, grid=(T // tk,),
        in_specs=[pl.BlockSpec((tk, M), lambda k: (k, 0)), pl.BlockSpec((J, tk, N), lambda k: (0, k, 0))],
        out_specs=[pl.BlockSpec((J, M, N), lambda k: (0, 0, 0))],
        out_shape=[jax.ShapeDtypeStruct((J, M, N), F32)],
        args=(a, b))


def in_proj(x, gain, win, tm, name):
    T, D = x.shape
    J, _, W = win.shape

    def body(x_ref, g_ref, w_ref, p_ref):
        _, xhat = _rms_parts(x_ref[...])
        n = (xhat * g_ref[...]).astype(BF16)
        for j in range(J):
            p_ref[j] = _dot(n, w_ref[j])

    return pl.pallas_call(
        body, name=name, grid=(T // tm,),
        in_specs=[pl.BlockSpec((tm, D), lambda i: (i, 0)),
                  pl.BlockSpec((1, D), lambda i: (0, 0)),
                  pl.BlockSpec((J, D, W), lambda i: (0, 0, 0))],
        out_specs=pl.BlockSpec((J, tm, W), lambda i: (0, i, 0)),
        out_shape=jax.ShapeDtypeStruct((J, T, W), F32),
        compiler_params=_params(1),
    )(x, gain, win)


def mixer_constants(S):
    B = RET_BLOCK
    half = RET_DK // 2
    freqs = ROPE_BASE ** (-jnp.arange(half, dtype=F32) * 2.0 / RET_DK)
    ang = jnp.arange(S, dtype=F32)[:, None] * freqs[None, :]
    cos = jnp.tile(jnp.cos(ang), (1, 2 * RET_HEADS))
    sin = jnp.tile(jnp.concatenate([-jnp.sin(ang), jnp.sin(ang)], axis=1), (1, RET_HEADS))
    gamma = 1.0 - 2.0 ** (-5.0 - jnp.arange(RET_HEADS, dtype=F32))
    log_g = jnp.log(gamma)
    idx = jnp.arange(B, dtype=F32)
    ci = jnp.arange(B) // CHUNK
    dist = jnp.abs(idx[:, None] - idx[None, :])
    m_intra = jnp.exp(log_g[:, None, None] * dist[None]) * (ci[None, :] <= ci[:, None])[None].astype(F32)
    lg_lane = jnp.repeat(log_g, RET_DK)
    d_q = jnp.exp(lg_lane[None, :] * (idx[:, None] + 1.0))
    d_k = jnp.exp(lg_lane[None, :] * (B - 1.0 - idx[:, None]))
    d_s = jnp.broadcast_to(jnp.exp(lg_lane * B)[:, None], (QK_W, V_W))
    bm = (jnp.arange(QK_W)[:, None] // RET_DK == jnp.arange(V_W)[None, :] // RET_DV).astype(F32)
    win = jnp.repeat(jnp.array(POOL_WINDOWS, F32), POOL_GC)
    invc = 1.0 / jnp.minimum(jnp.arange(S, dtype=F32)[:, None] + 1.0, win[None, :])
    return dict(cos=cos, sin=sin, m=m_intra, dq=d_q, dk=d_k, ds=d_s, bm=bm, invc=invc)


def _swap_halves(x):
    w = x.shape[1]
    lane = lax.broadcasted_iota(jnp.int32, x.shape, 1)
    return jnp.where((lane % RET_DK) < RET_DK // 2, pltpu.roll(x, w - RET_DK // 2, 1), pltpu.roll(x, RET_DK // 2, 1))


def _rot(x, cos, sin):
    return x * cos + _swap_halves(x) * sin


def _rot_t(d, cos, sin):
    return d * cos + _swap_halves(d * sin)


def _lane_groups(parts):
    return jnp.concatenate([p[:, POOL_GC * g:POOL_GC * (g + 1)] for g, p in enumerate(parts)], axis=1)


def _head_mask(shape, h):
    lane = lax.broadcasted_iota(jnp.int32, shape, 1)
    return (lane // RET_DK) == h


def _group_norm(o):
    yh, rs = [], []
    for h in range(RET_HEADS):
        oh = o[:, RET_DV * h:RET_DV * (h + 1)]
        xc = oh - jnp.mean(oh, axis=-1, keepdims=True)
        r = lax.rsqrt(jnp.mean(xc * xc, axis=-1, keepdims=True) + GN_EPS)
        yh.append(xc * r)
        rs.append(r)
    return yh, rs


def mixer_fwd(p, x1, consts, gn_gain, pool_w, pool_scale, wout, n_seq, S, name):
    _, T, _ = p.shape
    D = x1.shape[1]
    B = RET_BLOCK
    nb = S // B

    def body(p_ref, x1_ref, cos_ref, sin_ref, invc_ref, m_ref, dq_ref, dk_ref, ds_ref, bm_ref,
             gain_ref, pw_ref, sc_ref, wout_ref,
             x2_ref, m_out, o_out, st_out, pooled_out, state, prev_u):
        blk = pl.program_id(1)

        @pl.when(blk == 0)
        def _():
            state[...] = jnp.zeros_like(state)
            prev_u[...] = jnp.zeros_like(prev_u)

        qk = p_ref[0]
        v = p_ref[1]
        g = p_ref[2]
        u = p_ref[3]
        cos = cos_ref[...]
        sin = sin_ref[...]
        qr = _rot(qk[:, :QK_W], cos, sin) * (RET_DK ** -0.5)
        kr = _rot(qk[:, QK_W:], cos, sin)
        qb = qr.astype(BF16)
        kb = kr.astype(BF16)
        vb = v.astype(BF16)
        st = state[...]
        st_out[...] = st
        cross = _dot((qr * dq_ref[...]).astype(BF16), st.astype(BF16))
        o_parts = []
        for h in range(RET_HEADS):
            qm = jnp.where(_head_mask(qb.shape, h), qb, jnp.zeros_like(qb))
            sc = (_dot(qm, kb, NT) * m_ref[h]).astype(BF16)
            o_parts.append(_dot(sc, vb[:, RET_DV * h:RET_DV * (h + 1)]) + cross[:, RET_DV * h:RET_DV * (h + 1)])
        o = jnp.concatenate(o_parts, axis=1)
        o_out[...] = o
        kv = _dot((kr * dk_ref[...]).astype(BF16), vb, TN)
        state[...] = st * ds_ref[...] + kv * bm_ref[...]

        yh, _ = _group_norm(o)
        r = g * _sigmoid(g) * (jnp.concatenate(yh, axis=1) * gain_ref[...])

        ext = jnp.concatenate([prev_u[...], u], axis=0)
        sums = []
        run = ext
        for k in (1, 2, 4, 8):
            run = run + pltpu.roll(run, k, 0)
            sums.append(run[B:, :])
        prev_u[...] = u
        pooled = (_lane_groups(sums) * invc_ref[...] - u).astype(BF16)
        pooled_out[...] = pooled
        yp = [_dot(pooled[:, POOL_GC * gi:POOL_GC * (gi + 1)], pw_ref[gi].astype(BF16)) for gi in range(len(POOL_WINDOWS))]
        s = jnp.concatenate(yp, axis=1) * sc_ref[...]
        m = jnp.concatenate([r, s], axis=1).astype(BF16)
        m_out[...] = m
        x2_ref[...] = x1_ref[...] + _dot(m, wout_ref[...])

    tokmap = lambda b, k: (b * nb + k, 0)
    seqmap = lambda b, k: (k, 0)
    const2 = lambda b, k: (0, 0)
    const3 = lambda b, k: (0, 0, 0)
    return pl.pallas_call(
        body, name=name, grid=(n_seq, nb),
        in_specs=[pl.BlockSpec((4, B, V_W), lambda b, k: (0, b * nb + k, 0)),
                  pl.BlockSpec((B, D), tokmap),
                  pl.BlockSpec((B, QK_W), seqmap), pl.BlockSpec((B, QK_W), seqmap), pl.BlockSpec((B, V_W), seqmap),
                  pl.BlockSpec((RET_HEADS, B, B), const3),
                  pl.BlockSpec((B, QK_W), const2), pl.BlockSpec((B, QK_W), const2),
                  pl.BlockSpec((QK_W, V_W), const2), pl.BlockSpec((QK_W, V_W), const2),
                  pl.BlockSpec((1, V_W), const2), pl.BlockSpec((4, POOL_GC, POOL_GC), const3),
                  pl.BlockSpec((1, V_W), const2), pl.BlockSpec((D, D), const2)],
        out_specs=[pl.BlockSpec((B, D), tokmap), pl.BlockSpec((B, D), tokmap), pl.BlockSpec((B, V_W), tokmap),
                   pl.BlockSpec((None, QK_W, V_W), lambda b, k: (b * nb + k, 0, 0)),
                   pl.BlockSpec((B, V_W), tokmap)],
        out_shape=[jax.ShapeDtypeStruct((T, D), F32), jax.ShapeDtypeStruct((T, D), BF16),
                   jax.ShapeDtypeStruct((T, V_W), F32), jax.ShapeDtypeStruct((T // B, QK_W, V_W), F32),
                   jax.ShapeDtypeStruct((T, V_W), BF16)],
        scratch_shapes=[pltpu.VMEM((QK_W, V_W), F32), pltpu.VMEM((B, V_W), F32)],
        compiler_params=_params(2),
    )(p, x1, consts["cos"], consts["sin"], consts["invc"], consts["m"], consts["dq"], consts["dk"],
      consts["ds"], consts["bm"], gn_gain, pool_w, pool_scale, wout)


def mixer_bwd(dx3, dn3, x2, g3, x1, p, o_s, st_s, pooled_s, consts, gmix, gn_gain, pool_w, pool_scale, wout, win,
              n_seq, S, name, plan=None):
    T, D = x1.shape
    B = RET_BLOCK
    nb = S // B
    NG = len(POOL_WINDOWS)

    def body(dx3_ref, dn3_ref, x2_ref, g3_ref, x1_ref, p_ref, o_ref, st_ref, pooled_ref, cos_ref, sin_ref, invc_ref,
             m_ref, dq_ref, dk_ref, ds_ref, bm_ref, gmix_ref, gain_ref, pw_ref, sc_ref, wout_ref, win_ref,
             dx1_ref, df1_ref, dx2_ref, dp_ref, n2_ref, dg3_ref, dgmix_ref, dgain_ref, dscale_ref, dpw_ref,
             gstate, next_e):
        b = pl.program_id(0)
        blk = pl.program_id(1)

        @pl.when(jnp.logical_and(b == 0, blk == 0))
        def _():
            dg3_ref[...] = jnp.zeros_like(dg3_ref)
            dgmix_ref[...] = jnp.zeros_like(dgmix_ref)
            dgain_ref[...] = jnp.zeros_like(dgain_ref)
            dscale_ref[...] = jnp.zeros_like(dscale_ref)
            dpw_ref[...] = jnp.zeros_like(dpw_ref)

        @pl.when(blk == 0)
        def _():
            gstate[...] = jnp.zeros_like(gstate)
            next_e[...] = jnp.zeros_like(next_e)

        r2, xhat2 = _rms_parts(x2_ref[...])
        dn3 = dn3_ref[...]
        dg3_ref[...] += jnp.sum(dn3 * xhat2, axis=0, keepdims=True)
        dx2v = _rms_bwd(dn3, xhat2, r2, g3_ref[...], dx3_ref[...])
        dx2b = dx2v.astype(BF16)
        dx2_ref[...] = dx2b
        dm = _dot(dx2b, wout_ref[...], NT)
        dr = dm[:, :V_W]
        dsv = dm[:, V_W:]

        pooled = pooled_ref[...]
        scale = sc_ref[...]
        dyp = (dsv * scale).astype(BF16)
        yp, dpl = [], []
        for gi in range(NG):
            sl = slice(POOL_GC * gi, POOL_GC * (gi + 1))
            pwb = pw_ref[gi].astype(BF16)
            yp.append(_dot(pooled[:, sl], pwb))
            dpw_ref[gi] += _dot(pooled[:, sl], dyp[:, sl], TN)
            dpl.append(_dot(dyp[:, sl], pwb, NT))
        dscale_ref[...] += jnp.sum(dsv * jnp.concatenate(yp, axis=1), axis=0, keepdims=True)
        dpooled = jnp.concatenate(dpl, axis=1)
        e = dpooled * invc_ref[...]
        ext = jnp.concatenate([e, next_e[...]], axis=0)
        sums = []
        run = ext
        for k in (1, 2, 4, 8):
            run = run + pltpu.roll(run, 2 * B - k, 0)
            sums.append(run[:B, :])
        next_e[...] = e
        du = _lane_groups(sums) - dpooled

        qk = p_ref[0]
        v = p_ref[1]
        g = p_ref[2]
        sg = _sigmoid(g)
        si = g * sg
        yh, rs = _group_norm(o_ref[...])
        yhat = jnp.concatenate(yh, axis=1)
        gain = gain_ref[...]
        dg = dr * (yhat * gain) * (sg * (1.0 + g * (1.0 - sg)))
        dy = dr * si
        dgain_ref[...] += jnp.sum(dy * yhat, axis=0, keepdims=True)
        dyh = dy * gain
        do_parts = []
        for h in range(RET_HEADS):
            sl = slice(RET_DV * h, RET_DV * (h + 1))
            dh_ = dyh[:, sl]
            m1 = jnp.mean(dh_, axis=-1, keepdims=True)
            m2 = jnp.mean(dh_ * yh[h], axis=-1, keepdims=True)
            do_parts.append(rs[h] * (dh_ - m1 - yh[h] * m2))
        dob = jnp.concatenate(do_parts, axis=1).astype(BF16)

        cos = cos_ref[...]
        sin = sin_ref[...]
        qr = _rot(qk[:, :QK_W], cos, sin) * (RET_DK ** -0.5)
        kr = _rot(qk[:, QK_W:], cos, sin)
        qb = qr.astype(BF16)
        kb = kr.astype(BF16)
        vb = v.astype(BF16)
        dqd = dq_ref[...]
        dkd = dk_ref[...]
        stb = st_ref[...].astype(BF16)
        gs = gstate[...]
        gb = gs.astype(BF16)
        dqs = _dot(dob, stb, NT) * dqd
        dkr = _dot(vb, gb, NT) * dkd
        dv_cross = _dot((kr * dkd).astype(BF16), gb)
        ds_cross = _dot((qr * dqd).astype(BF16), dob, TN) * bm_ref[...]
        gstate[...] = ds_cross + gs * ds_ref[...]
        dv_parts = []
        for h in range(RET_HEADS):
            sl = slice(RET_DV * h, RET_DV * (h + 1))
            hm = _head_mask(qb.shape, h)
            qm = jnp.where(hm, qb, jnp.zeros_like(qb))
            mh = m_ref[h]
            sc = (_dot(qm, kb, NT) * mh).astype(BF16)
            dsc = (_dot(dob[:, sl], vb[:, sl], NT) * mh).astype(BF16)
            dqs = dqs + jnp.where(hm, _dot(dsc, kb), 0.0)
            dkr = dkr + jnp.where(hm, _dot(dsc, qb, TN), 0.0)
            dv_parts.append(_dot(sc, dob[:, sl], TN) + dv_cross[:, sl])
        dq = _rot_t(dqs * (RET_DK ** -0.5), cos, sin)
        dk = _rot_t(dkr, cos, sin)
        dp = [jnp.concatenate([dq, dk], axis=1).astype(BF16), jnp.concatenate(dv_parts, axis=1).astype(BF16),
              dg.astype(BF16), du.astype(BF16)]
        dn = jnp.zeros((B, D), F32)
        for jj in range(4):
            dp_ref[jj] = dp[jj]
            dn = dn + _dot(dp[jj], win_ref[jj], NT)

        x1v = x1_ref[...]
        r, xhat = _rms_parts(x1v)
        gm = gmix_ref[...]
        n2_ref[...] = (xhat * gm).astype(BF16)
        dgmix_ref[...] += jnp.sum(dn * xhat, axis=0, keepdims=True)
        dx1 = _rms_bwd(dn, xhat, r, gm, dx2v)
        dx1_ref[...] = dx1
        df1_ref[...] = (0.5 * dx1).astype(BF16)

    rev = lambda b, k: (b * nb + (nb - 1 - k), 0)
    seqrev = lambda b, k: (nb - 1 - k, 0)
    const2 = lambda b, k: (0, 0)
    const3 = lambda b, k: (0, 0, 0)
    return _call(
        body, plan, name=name, grid=(n_seq, nb),
        in_specs=[pl.BlockSpec((B, D), rev), pl.BlockSpec((B, D), rev), pl.BlockSpec((B, D), rev),
                  pl.BlockSpec((1, D), const2), pl.BlockSpec((B, D), rev),
                  pl.BlockSpec((4, B, V_W), lambda b, k: (0, b * nb + (nb - 1 - k), 0)),
                  pl.BlockSpec((B, V_W), rev),
                  pl.BlockSpec((None, QK_W, V_W), lambda b, k: (b * nb + (nb - 1 - k), 0, 0)),
                  pl.BlockSpec((B, V_W), rev),
                  pl.BlockSpec((B, QK_W), seqrev), pl.BlockSpec((B, QK_W), seqrev), pl.BlockSpec((B, V_W), seqrev),
                  pl.BlockSpec((RET_HEADS, B, B), const3),
                  pl.BlockSpec((B, QK_W), const2), pl.BlockSpec((B, QK_W), const2),
                  pl.BlockSpec((QK_W, V_W), const2), pl.BlockSpec((QK_W, V_W), const2),
                  pl.BlockSpec((1, D), const2), pl.BlockSpec((1, V_W), const2),
                  pl.BlockSpec((NG, POOL_GC, POOL_GC), const3), pl.BlockSpec((1, V_W), const2),
                  pl.BlockSpec((D, D), const2), pl.BlockSpec((4, D, V_W), const3)],
        out_specs=[pl.BlockSpec((B, D), rev), pl.BlockSpec((B, D), rev), pl.BlockSpec((B, D), rev),
                   pl.BlockSpec((4, B, V_W), lambda b, k: (0, b * nb + (nb - 1 - k), 0)),
                   pl.BlockSpec((B, D), rev),
                   pl.BlockSpec((1, D), const2), pl.BlockSpec((1, D), const2), pl.BlockSpec((1, V_W), const2),
                   pl.BlockSpec((1, V_W), const2), pl.BlockSpec((NG, POOL_GC, POOL_GC), const3)],
        out_shape=[jax.ShapeDtypeStruct((T, D), F32), jax.ShapeDtypeStruct((T, D), BF16),
                   jax.ShapeDtypeStruct((T, D), BF16), jax.ShapeDtypeStruct((4, T, V_W), BF16),
                   jax.ShapeDtypeStruct((T, D), BF16),
                   jax.ShapeDtypeStruct((1, D), F32), jax.ShapeDtypeStruct((1, D), F32), jax.ShapeDtypeStruct((1, V_W), F32),
                   jax.ShapeDtypeStruct((1, V_W), F32), jax.ShapeDtypeStruct((NG, POOL_GC, POOL_GC), F32)],
        scratch_shapes=[pltpu.VMEM((QK_W, V_W), F32), pltpu.VMEM((B, V_W), F32)],
        args=(dx3, dn3, x2, g3, x1, p, o_s, st_s, pooled_s, consts["cos"], consts["sin"], consts["invc"], consts["m"], consts["dq"],
              consts["dk"], consts["ds"], consts["bm"], gmix, gn_gain, pool_w, pool_scale, wout, win))


def loss_bwd(x3, tgt, gain, tm, name):
    T, D = x3.shape

    def body(x_ref, t_ref, g_ref, dx_ref, df_ref, lacc_ref, dg_ref):
        @pl.when(pl.program_id(0) == 0)
        def _():
            lacc_ref[...] = jnp.zeros_like(lacc_ref)
            dg_ref[...] = jnp.zeros_like(dg_ref)

        r, xhat = _rms_parts(x_ref[...])
        gv = g_ref[...]
        err = xhat * gv - t_ref[...]
        lacc_ref[...] += jnp.sum(err * err, axis=0, keepdims=True)
        dy = err * (1.0 / D)
        dg_ref[...] += jnp.sum(dy * xhat, axis=0, keepdims=True)
        dx = _rms_bwd(dy, xhat, r, gv, 0.0)
        dx_ref[...] = dx
        df_ref[...] = (0.5 * dx).astype(BF16)

    tok = pl.BlockSpec((tm, D), lambda i: (i, 0))
    vec = pl.BlockSpec((1, D), lambda i: (0, 0))
    return pl.pallas_call(
        body, name=name, grid=(T // tm,),
        in_specs=[tok, tok, vec], out_specs=[tok, tok, vec, vec],
        out_shape=[jax.ShapeDtypeStruct((T, D), F32), jax.ShapeDtypeStruct((T, D), BF16),
                   jax.ShapeDtypeStruct((1, D), F32), jax.ShapeDtypeStruct((1, D), F32)],
        compiler_params=_params(1),
    )(x3, tgt, gain)


def _adamw_math(w, g, m, v):
    m2 = ADAM_B1 * m + (1.0 - ADAM_B1) * g
    v2 = ADAM_B2 * v + (1.0 - ADAM_B2) * (g * g)
    m_hat = m2 / (1.0 - ADAM_B1 ** ADAM_STEP)
    v_hat = v2 / (1.0 - ADAM_B2 ** ADAM_STEP)
    return -ADAM_LR * (m_hat / (jnp.sqrt(v_hat) + ADAM_EPS) + ADAM_WD * w), m2, v2


def _place():
    x, y, c = lax.axis_index("x"), lax.axis_index("y"), lax.axis_index("c")
    other_chips = [(1 - x, y), (x, 1 - y), (1 - x, 1 - y)]
    return x, y, c, other_chips


def _exchange_plan(ins, out_shape, copies, n_copies):
    def descriptors(pins, pouts, psems):
        send, recv = psems
        return [pltpu.make_async_remote_copy(src_ref=s, dst_ref=d, send_sem=send.at[i], recv_sem=recv.at[i],
                                             device_id=dev, device_id_type=MESH)
                for i, (s, d, dev) in enumerate(copies(pins, pouts))]

    def start(pins, pouts, psems):
        for cp in descriptors(pins, pouts, psems):
            cp.start()

    def finish(pins, pouts, psems):
        for cp in descriptors(pins, pouts, psems):
            cp.wait()

    return CommPlan(tuple(ins), tuple(out_shape),
                    (pltpu.SemaphoreType.DMA((n_copies,)), pltpu.SemaphoreType.DMA((n_copies,))), start, finish)


def _combine(a, b):
    assert a.mid is None and b.mid is None
    ni, no, ns = len(a.ins), len(a.out_shape), len(a.sems)

    def start(pins, pouts, psems):
        a.start(pins[:ni], pouts[:no], psems[:ns])
        b.start(pins[ni:], pouts[no:], psems[ns:])

    def finish(pins, pouts, psems):
        a.finish(pins[:ni], pouts[:no], psems[:ns])
        b.finish(pins[ni:], pouts[no:], psems[ns:])

    return CommPlan(a.ins + b.ins, a.out_shape + b.out_shape, a.sems + b.sems, start, finish)


def gather_plan(shards):
    n = len(shards)

    def start(pins, pouts, psems):
        x, y, c, chips = _place()
        mine = 2 * x + y
        for w in range(n):
            for k, (px, py) in enumerate(chips):
                pltpu.make_async_remote_copy(
                    src_ref=pins[w].at[:, c], dst_ref=pouts[w].at[:, mine, c],
                    send_sem=psems[0].at[w, k], recv_sem=psems[1].at[w, k],
                    device_id=(px, py, c), device_id_type=MESH).start()

    def mid(pins, pouts, psems):
        x, y, c, chips = _place()
        for w in range(n):
            for k, (px, py) in enumerate(chips):
                landed = pouts[w].at[:, 2 * px + py, c]
                pltpu.make_async_remote_copy(
                    src_ref=landed, dst_ref=landed, send_sem=psems[0].at[w, k], recv_sem=psems[1].at[w, k],
                    device_id=(px, py, c), device_id_type=MESH).wait_recv()
                pltpu.make_async_remote_copy(
                    src_ref=landed, dst_ref=landed, send_sem=psems[2].at[w, k], recv_sem=psems[3].at[w, k],
                    device_id=(x, y, 1 - c), device_id_type=MESH).start()

    def finish(pins, pouts, psems):
        x, y, c, chips = _place()
        mine = 2 * x + y
        for w in range(n):
            for k, (px, py) in enumerate(chips):
                landed = pouts[w].at[:, 2 * px + py, 1 - c]
                cp = pltpu.make_async_remote_copy(
                    src_ref=landed, dst_ref=landed, send_sem=psems[2].at[w, k], recv_sem=psems[3].at[w, k],
                    device_id=(x, y, 1 - c), device_id_type=MESH)
                cp.wait_recv()
                cp.wait_send()
                pltpu.make_async_remote_copy(
                    src_ref=pins[w].at[:, c], dst_ref=pouts[w].at[:, mine, c],
                    send_sem=psems[0].at[w, k], recv_sem=psems[1].at[w, k],
                    device_id=(px, py, c), device_id_type=MESH).wait_send()

    return CommPlan(tuple(shards),
                    tuple(jax.ShapeDtypeStruct((s.shape[0], N_CHIPS) + s.shape[1:], s.dtype) for s in shards),
                    tuple(pltpu.SemaphoreType.DMA((n, 3)) for _ in range(4)), start, finish, mid)


def place_own(gathered, shard, name):
    L, _, Rh, C = shard.shape

    def body(chip_ref, g_ref, s_ref, o_ref):
        o_ref[...] = s_ref[...]

    return pl.pallas_call(
        body, name=name,
        grid_spec=pltpu.PrefetchScalarGridSpec(
            num_scalar_prefetch=1, grid=(L,),
            in_specs=[ANY, pl.BlockSpec((None, 2, Rh, C), lambda l, chip_ref: (l, 0, 0, 0))],
            out_specs=pl.BlockSpec((None, None, 2, Rh, C), lambda l, chip_ref: (l, chip_ref[0], 0, 0, 0))),
        out_shape=jax.ShapeDtypeStruct(gathered.shape, gathered.dtype),
        input_output_aliases={1: 0},
        compiler_params=_params(1),
    )(_chip_index(), gathered, shard)


def gather_weights(shards):
    n = len(shards)

    def body(*refs):
        ins, outs, bufs = refs[:n], refs[n:2 * n], refs[2 * n:3 * n]
        ld_sem, st_sem, ici_send, ici_recv, d2d_send, d2d_recv = refs[3 * n:]
        x, y, c, chips = _place()
        sibling = (x, y, 1 - c)
        mine = 2 * x + y
        loads = [pltpu.make_async_copy(ins[w], bufs[w], ld_sem.at[w]) for w in range(n)]
        for cp in loads:
            cp.start()
        stores, sends = [], []
        for w in range(n):
            loads[w].wait()
            cp = pltpu.make_async_copy(bufs[w], outs[w].at[:, mine], st_sem.at[w])
            cp.start()
            stores.append(cp)
            for k, (px, py) in enumerate(chips):
                cp = pltpu.make_async_remote_copy(
                    src_ref=bufs[w].at[:, c], dst_ref=outs[w].at[:, mine, c],
                    send_sem=ici_send.at[w, k], recv_sem=ici_recv.at[w, k],
                    device_id=(px, py, c), device_id_type=MESH)
                cp.start()
                sends.append(cp)
        for w in range(n):
            for k, (px, py) in enumerate(chips):
                landed = outs[w].at[:, 2 * px + py, c]
                pltpu.make_async_remote_copy(
                    src_ref=landed, dst_ref=landed, send_sem=ici_send.at[w, k], recv_sem=ici_recv.at[w, k],
                    device_id=(px, py, c), device_id_type=MESH).wait_recv()
                cp = pltpu.make_async_remote_copy(
                    src_ref=landed, dst_ref=landed, send_sem=d2d_send.at[w, k], recv_sem=d2d_recv.at[w, k],
                    device_id=sibling, device_id_type=MESH)
                cp.start()
                sends.append(cp)
        for w in range(n):
            for k, (px, py) in enumerate(chips):
                landed = outs[w].at[:, 2 * px + py, 1 - c]
                pltpu.make_async_remote_copy(
                    src_ref=landed, dst_ref=landed, send_sem=d2d_send.at[w, k], recv_sem=d2d_recv.at[w, k],
                    device_id=sibling, device_id_type=MESH).wait_recv()
        for cp in sends:
            cp.wait_send()
        for cp in stores:
            cp.wait()

    return pl.pallas_call(
        body, name="gather_weights",
        in_specs=[ANY] * n, out_specs=[ANY] * n,
        out_shape=[jax.ShapeDtypeStruct((s.shape[0], N_CHIPS) + s.shape[1:], s.dtype) for s in shards],
        scratch_shapes=[pltpu.VMEM(s.shape, s.dtype) for s in shards] +
                       [pltpu.SemaphoreType.DMA((n,)), pltpu.SemaphoreType.DMA((n,)),
                        pltpu.SemaphoreType.DMA((n, 3)), pltpu.SemaphoreType.DMA((n, 3)),
                        pltpu.SemaphoreType.DMA((n, 3)), pltpu.SemaphoreType.DMA((n, 3))],
        compiler_params=pltpu.CompilerParams(vmem_limit_bytes=VMEM_LIMIT),
    )(*shards)


def send_to_sibling_other_half(grads):
    def copies(ins, outs):
        x, y, c, _ = _place()
        return [(ins[w].at[:, :, 1 - c], outs[w], (x, y, 1 - c)) for w in range(len(grads))]

    return _exchange_plan(grads, [jax.ShapeDtypeStruct(g.shape[:2] + g.shape[3:], g.dtype) for g in grads], copies,
                          len(grads))


def _core_index():
    return jnp.reshape(lax.axis_index("c"), (1,)).astype(jnp.int32)


def _chip_index():
    return jnp.reshape(2 * lax.axis_index("x") + lax.axis_index("y"), (1,)).astype(jnp.int32)


def add_own_half(g, recv, name):
    L, J, _, Rh, C = g.shape

    def body(c_ref, g_ref, r_ref, o_ref):
        o_ref[...] = (g_ref[...] + r_ref[...]).astype(BF16)

    return pl.pallas_call(
        body, name=name,
        grid_spec=pltpu.PrefetchScalarGridSpec(
            num_scalar_prefetch=1, grid=(L, J),
            in_specs=[pl.BlockSpec((None, None, None, Rh, C), lambda l, j, c_ref: (l, j, c_ref[0], 0, 0)),
                      pl.BlockSpec((None, None, Rh, C), lambda l, j, c_ref: (l, j, 0, 0))],
            out_specs=pl.BlockSpec((None, None, Rh, C), lambda l, j, c_ref: (l, j, 0, 0))),
        out_shape=jax.ShapeDtypeStruct((L, J, Rh, C), BF16),
        compiler_params=_params(2),
    )(_core_index(), g, recv)


def exchange_between_chips(sums):
    def copies(ins, outs):
        x, y, c, chips = _place()
        return [(ins[w].at[:, 2 * px + py], outs[w].at[k], (px, py, c))
                for w in range(len(sums)) for k, (px, py) in enumerate(chips)]

    return _exchange_plan(sums, [jax.ShapeDtypeStruct((3, s.shape[0]) + s.shape[2:], s.dtype) for s in sums], copies,
                          3 * len(sums))


def sum_chips(own, recv, name):
    L, _, Rh, C = own.shape

    def body(chip_ref, o_ref, r_ref, out_ref):
        acc = o_ref[...].astype(F32)
        for k in range(3):
            acc = acc + r_ref[k].astype(F32)
        out_ref[...] = acc

    return pl.pallas_call(
        body, name=name,
        grid_spec=pltpu.PrefetchScalarGridSpec(
            num_scalar_prefetch=1, grid=(L,),
            in_specs=[pl.BlockSpec((None, None, Rh, C), lambda l, chip_ref: (l, chip_ref[0], 0, 0)),
                      pl.BlockSpec((3, None, Rh, C), lambda l, chip_ref: (0, l, 0, 0))],
            out_specs=pl.BlockSpec((None, Rh, C), lambda l, chip_ref: (l, 0, 0))),
        out_shape=jax.ShapeDtypeStruct((L, Rh, C), F32),
        compiler_params=_params(1),
    )(_chip_index(), own, recv)


def share_with_sibling(halves):
    def copies(ins, outs):
        x, y, c, _ = _place()
        return [(ins[w], outs[w], (x, y, 1 - c)) for w in range(len(halves))]

    return _exchange_plan(halves, [jax.ShapeDtypeStruct(h.shape, h.dtype) for h in halves], copies, len(halves))


def adamw_group(ws, ms, vs, own, sib, name, plan=None):
    L = len(ws)
    R, C = ws[0].shape
    Rh = R // 2
    tr = Rh // 2
    nr = Rh // tr

    def body(*refs):
        w, m, v = refs[:L], refs[L:2 * L], refs[2 * L:3 * L]
        own_ref, sib_ref = refs[3 * L], refs[3 * L + 1]
        outs = refs[3 * L + 2:]
        mine = pl.program_id(0) == lax.axis_index("c")
        for l in range(L):
            gv = jnp.where(mine, own_ref[l], sib_ref[l])
            d, m2, v2 = _adamw_math(w[l][...], gv, m[l][...], v[l][...])
            outs[4 * l][...] = d
            outs[4 * l + 1][...] = m2
            outs[4 * l + 2][...] = v2
            outs[4 * l + 3][...] = gv

    blk = pl.BlockSpec((tr, C), lambda h, r: (h * nr + r, 0))
    half = pl.BlockSpec((L, tr, C), lambda h, r: (0, r, 0))
    sds = jax.ShapeDtypeStruct((R, C), F32)
    return _call(body, plan, name=name, grid=(2, nr), in_specs=[blk] * (3 * L) + [half, half],
                 out_specs=[blk] * (4 * L), out_shape=[sds] * (4 * L), args=(*ws, *ms, *vs, own, sib))


def gather_small_plan(part):
    def copies(ins, outs):
        x, y, c, _ = _place()
        me = 4 * x + 2 * y + c
        return [(ins[0], outs[0].at[me], (x ^ ((k >> 2) & 1), y ^ ((k >> 1) & 1), c ^ (k & 1))) for k in range(1, N_DEV)]

    return _exchange_plan([part], [jax.ShapeDtypeStruct((N_DEV,) + part.shape, part.dtype)], copies, N_DEV - 1)


def sum_small(part, gathered, name):
    R, C = part.shape

    def body(p_ref, g_ref, o_ref):
        d = pl.program_id(0)
        me = 4 * lax.axis_index("x") + 2 * lax.axis_index("y") + lax.axis_index("c")
        val = jnp.where(d == me, p_ref[...], g_ref[...])

        @pl.when(d == 0)
        def _():
            o_ref[...] = val

        @pl.when(d > 0)
        def _():
            o_ref[...] += val

    return pl.pallas_call(
        body, name=name, grid=(N_DEV,),
        in_specs=[pl.BlockSpec((R, C), lambda d: (0, 0)), pl.BlockSpec((None, R, C), lambda d: (d, 0, 0))],
        out_specs=pl.BlockSpec((R, C), lambda d: (0, 0)),
        out_shape=jax.ShapeDtypeStruct((R, C), F32),
        compiler_params=_params(1),
    )(part, gathered)


SMALL = ("norm_ffn1", "norm_mix", "norm_ffn2", "norm_final", "ret_gn_gain", "pool_scale", "pool_w")
GROUPS = (("ffn1", ("ffn1_gate", "ffn1_up", "ffn1_down")), ("w_in", ("w_in",)), ("w_out", ("w_out",)),
          ("ffn2", ("ffn2_gate", "ffn2_up", "ffn2_down")))
TRANSPOSED = ("ffn1_gate", "ffn1_up", "ffn2_gate", "ffn2_up")
IN_W = 2 * QK_W + 3 * V_W
ORDER = ("norm_ffn1", "ffn1_gate", "ffn1_up", "ffn1_down", "norm_mix", "w_in", "ret_gn_gain", "pool_w", "pool_scale",
         "w_out", "norm_ffn2", "ffn2_gate", "ffn2_up", "ffn2_down", "norm_final")


SUBLANES = 8


def _small_rows(shapes):
    rows = [math.prod(shapes[k]) // 128 for k in SMALL]
    offs, off = [], 0
    for r in rows:
        offs.append(off)
        off += -(-r // SUBLANES) * SUBLANES
    return rows, offs, off


def _pack_small(d, shapes):
    rows, offs, _ = _small_rows(shapes)
    pieces = []
    for k, r in zip(SMALL[1:], rows[1:]):
        pieces.append(d[k].reshape(r, 128))
        if r % SUBLANES:
            pieces.append(jnp.zeros((SUBLANES - r % SUBLANES, 128), F32))
    return jnp.concatenate(pieces, axis=0)


def adamw_small(W, M, V, first, summed, shapes):
    rows, offs, _ = _small_rows(shapes)
    n = len(SMALL)

    def body(*refs):
        w, m, v, f_ref, s_ref = refs[:n], refs[n:2 * n], refs[2 * n:3 * n], refs[3 * n], refs[3 * n + 1]
        outs = refs[3 * n + 2:]
        for i, (r, off) in enumerate(zip(rows, offs)):
            gv = f_ref[...] if i == 0 else s_ref[off - offs[1]:off - offs[1] + r, :]
            d, m2, v2 = _adamw_math(w[i][...], gv, m[i][...], v[i][...])
            outs[4 * i][...] = d
            outs[4 * i + 1][...] = m2
            outs[4 * i + 2][...] = v2
            outs[4 * i + 3][...] = gv

    vm = pl.BlockSpec(memory_space=pltpu.VMEM)
    flat = lambda d: [d[k].reshape(r, 128) for k, r in zip(SMALL, rows)]
    res = pl.pallas_call(
        body, name="adamw_small", in_specs=[vm] * (3 * n + 2), out_specs=[vm] * (4 * n),
        out_shape=[jax.ShapeDtypeStruct((r, 128), F32) for r in rows for _ in range(4)],
    )(*flat(W), *flat(M), *flat(V), first, summed)
    return {k: [a.reshape(shapes[k]) for a in res[4 * i:4 * i + 4]] for i, k in enumerate(SMALL)}


def kernel(x, norm_ffn1, ffn1_gate, ffn1_up, ffn1_down, norm_mix, w_in, ret_gn_gain, pool_w, pool_scale, w_out, norm_ffn2, ffn2_gate, ffn2_up, ffn2_down, norm_final, loss_target, m_norm_ffn1, m_ffn1_gate, m_ffn1_up, m_ffn1_down, m_norm_mix, m_w_in, m_ret_gn_gain, m_pool_w, m_pool_scale, m_w_out, m_norm_ffn2, m_ffn2_gate, m_ffn2_up, m_ffn2_down, m_norm_final, v_norm_ffn1, v_ffn1_gate, v_ffn1_up, v_ffn1_down, v_norm_mix, v_w_in, v_ret_gn_gain, v_pool_w, v_pool_scale, v_w_out, v_norm_ffn2, v_ffn2_gate, v_ffn2_up, v_ffn2_down, v_norm_final):
    W = dict(norm_ffn1=norm_ffn1, ffn1_gate=ffn1_gate, ffn1_up=ffn1_up, ffn1_down=ffn1_down, norm_mix=norm_mix, w_in=w_in,
             ret_gn_gain=ret_gn_gain, pool_w=pool_w, pool_scale=pool_scale, w_out=w_out, norm_ffn2=norm_ffn2,
             ffn2_gate=ffn2_gate, ffn2_up=ffn2_up, ffn2_down=ffn2_down, norm_final=norm_final)
    M = dict(norm_ffn1=m_norm_ffn1, ffn1_gate=m_ffn1_gate, ffn1_up=m_ffn1_up, ffn1_down=m_ffn1_down, norm_mix=m_norm_mix,
             w_in=m_w_in, ret_gn_gain=m_ret_gn_gain, pool_w=m_pool_w, pool_scale=m_pool_scale, w_out=m_w_out,
             norm_ffn2=m_norm_ffn2, ffn2_gate=m_ffn2_gate, ffn2_up=m_ffn2_up, ffn2_down=m_ffn2_down, norm_final=m_norm_final)
    V = dict(norm_ffn1=v_norm_ffn1, ffn1_gate=v_ffn1_gate, ffn1_up=v_ffn1_up, ffn1_down=v_ffn1_down, norm_mix=v_norm_mix,
             w_in=v_w_in, ret_gn_gain=v_ret_gn_gain, pool_w=v_pool_w, pool_scale=v_pool_scale, w_out=v_w_out,
             norm_ffn2=v_norm_ffn2, ffn2_gate=v_ffn2_gate, ffn2_up=v_ffn2_up, ffn2_down=v_ffn2_down, norm_final=v_norm_final)

    n_seq, S, D = x.shape
    T = n_seq * S
    tm = min(1024, T // 2)
    tk = min(1024, T // 2)
    tk1 = min(2048, T // 2)
    xf = x.reshape(T, D)
    tgt = loss_target.reshape(T, D)

    def local(d, k):
        a = d[k][0]
        return a.T if k in TRANSPOSED else a

    def to_out(k, a):
        return (a.T if k in TRANSPOSED else a)[None]

    loc_bf = []
    for _, members in GROUPS:
        st = jnp.stack([local(W, k).astype(BF16) for k in members])
        loc_bf.append(st.reshape(st.shape[0], 2, st.shape[1] // 2, st.shape[2]))
    FS = loc_bf[0].shape[2] * 2
    w3_1 = gather_weights(loc_bf[:1])[0].reshape(3, N_CHIPS, FS, D)
    g1 = norm_ffn1.reshape(1, D)
    gm = norm_mix.reshape(1, D)
    g3 = norm_ffn2.reshape(1, D)
    gf = norm_final.reshape(1, D)
    gn_gain = ret_gn_gain.reshape(1, V_W)
    pscale = pool_scale.reshape(1, V_W)
    pw = pool_w.reshape(len(POOL_WINDOWS), POOL_GC, POOL_GC)
    consts = mixer_constants(S)

    def halves5(g, lb):
        return g.reshape(lb.shape[0], N_CHIPS, 2, lb.shape[2], lb.shape[3])

    (x1, b1, si1, t1, h1, n1), landed = ffn_fwd(xf, g1, w3_1, tm, "ffn1_fwd", plan=gather_plan(loc_bf[1:]))
    gathered = [place_own(g, lb, "place_own_" + gn) for g, lb, (gn, _) in zip(landed, loc_bf[1:], GROUPS[1:])]
    win_full = gathered[0].reshape(N_CHIPS, D, IN_W // N_CHIPS)
    wout_full = gathered[1].reshape(D, D)
    w3_2 = gathered[2].reshape(3, N_CHIPS, FS, D)
    p = in_proj(x1, gm, win_full, tm, "in_proj")
    x2, m_s, o_s, st_s, pooled_s = mixer_fwd(p, x1, consts, gn_gain, pw, pscale, wout_full, n_seq, S, "mixer_fwd")
    x3, b2, si2, t2, h2, n3 = ffn_fwd(x2, g3, w3_2, tm, "ffn2_fwd")

    dx3, df2, lacc, dgf = loss_bwd(x3, tgt, gf, tm, "loss_bwd")
    dn3, da2, db2 = ffn_bwd(df2, b2, si2, t2, w3_2, tm, "ffn2_bwd")
    to_sib, to_chips, back = send_to_sibling_other_half, exchange_between_chips, share_with_sibling
    out_g, out_d, out_m, out_v = {}, {}, {}, {}

    def update(members, own, sib, name):
        res = adamw_group([local(W, k) for k in members], [local(M, k) for k in members], [local(V, k) for k in members],
                          own, sib, "adamw_" + name)
        for l, k in enumerate(members):
            out_d[k], out_m[k], out_v[k], out_g[k] = (to_out(k, r) for r in res[4 * l:4 * l + 4])

    (gr_ffn2,) = ffn_wgrad(da2, db2, h2, n3, df2, tk, "ffn2_wgrad")
    g_ffn2 = halves5(gr_ffn2, loc_bf[3])
    (dx1, df1, dx2b, dp, n2, dg3, dgm, dgn, dps, dpw), (sb_ffn2,) = mixer_bwd(
        dx3, dn3, x2, g3, x1, p, o_s, st_s, pooled_s, consts, gm, gn_gain, pw, pscale, wout_full, win_full, n_seq, S,
        "mixer_bwd", plan=to_sib([g_ffn2]))
    cs_ffn2 = add_own_half(g_ffn2, sb_ffn2, "add_sibling_ffn2")
    (gr_wout,) = tn_shared(m_s, dx2b[None], tk1, "dw_out")
    (gr_win,) = tn_shared(n2, dp, tk1, "dw_in")
    g_mix = [halves5(gr_win, loc_bf[1]), halves5(gr_wout, loc_bf[2])]
    shapes = {k: W[k].shape for k in SMALL}
    small_part = dict(norm_mix=dgm, norm_ffn2=dg3, norm_final=dgf, ret_gn_gain=dgn, pool_scale=dps, pool_w=dpw)
    part = jnp.concatenate([_pack_small(small_part, shapes), lacc.reshape(-1, 128)], axis=0)
    n_small = part.shape[0] - lacc.size // 128
    (dn1, da1, db1), (pc_ffn2, sb_win, sb_wout, parts) = ffn_bwd(
        df1, b1, si1, t1, w3_1, tm, "ffn1_bwd",
        plan=_combine(_combine(to_chips([cs_ffn2]), to_sib(g_mix)), gather_small_plan(part)))
    summed = sum_small(part, parts, "sum_small")
    loss = jnp.sum(summed[n_small:]) * (0.5 / D)
    mh_ffn2 = sum_chips(cs_ffn2, pc_ffn2, "sum_chips_ffn2")
    cs_mix = [add_own_half(g, r, "add_sibling_" + gn) for g, r, gn in zip(g_mix, (sb_win, sb_wout), ("w_in", "w_out"))]
    def one(g):
        return g.reshape((1, N_CHIPS, 2) + loc_bf[0].shape[2:])

    (gr_g,), (sh_ffn2, pc_win, pc_wout) = wgrad_one(
        da1, n1, tk1, "ffn1_wgrad_gate", plan=_combine(back([mh_ffn2]), to_chips(cs_mix)))
    mh_mix = [sum_chips(cs, pc, "sum_chips_" + gn) for cs, pc, gn in zip(cs_mix, (pc_win, pc_wout), ("w_in", "w_out"))]
    (gr_u,), (sb_g,) = wgrad_one(db1, n1, tk1, "ffn1_wgrad_up", plan=to_sib([one(gr_g)]))
    cs_g = add_own_half(one(gr_g), sb_g, "add_sibling_ffn1_gate")
    (gr_d,), (pc_g, sb_u) = wgrad_one(
        h1, df1, tk1, "ffn1_wgrad_down", plan=_combine(to_chips([cs_g]), to_sib([one(gr_u)])))
    mh_g = sum_chips(cs_g, pc_g, "sum_chips_ffn1_gate")
    cs_u = add_own_half(one(gr_u), sb_u, "add_sibling_ffn1_up")
    (dx0, dg1), (sh_g, sh_win, sh_wout, pc_u, sb_d) = rms_bwd(
        xf, dx1, dn1, g1, tm, "ffn1_rms_bwd",
        plan=_combine(_combine(back([mh_g] + mh_mix), to_chips([cs_u])), to_sib([one(gr_d)])))
    mh_u = sum_chips(cs_u, pc_u, "sum_chips_ffn1_up")
    cs_d = add_own_half(one(gr_d), sb_d, "add_sibling_ffn1_down")
    first = dg1.reshape(-1, 128)
    sh_u, pc_d, firsts = _run_plan(
        _combine(_combine(back([mh_u]), to_chips([cs_d])), gather_small_plan(first)), "ffn1_grads_stage_b")
    mh_d = sum_chips(cs_d, pc_d, "sum_chips_ffn1_down")
    (sh_d,) = _run_plan(back([mh_d]), "ffn1_grads_stage_c")
    first_sum = sum_small(first, firsts, "sum_small_first")
    for k, own, sib in zip(GROUPS[0][1], (mh_g, mh_u, mh_d), (sh_g, sh_u, sh_d)):
        update((k,), own, sib, k)
    update(GROUPS[1][1], mh_mix[0], sh_win, "w_in")
    update(GROUPS[2][1], mh_mix[1], sh_wout, "w_out")
    update(GROUPS[3][1], mh_ffn2, sh_ffn2, "ffn2")

    for k, (d_, m_, v_, g_) in adamw_small(W, M, V, first_sum, summed, shapes).items():
        out_d[k], out_m[k], out_v[k], out_g[k] = d_, m_, v_, g_

    grad_x = dx0.reshape(n_seq, S, D)
    return (loss, grad_x, *[out_g[k] for k in ORDER], *[out_d[k] for k in ORDER],
            *[out_m[k] for k in ORDER], *[out_v[k] for k in ORDER])
```

```python
import functools
import math
from typing import Callable, NamedTuple, Optional

import jax
import jax.numpy as jnp
from jax import lax
from jax.experimental import pallas as pl
from jax.experimental.pallas import tpu as pltpu

F32 = jnp.float32
BF16 = jnp.bfloat16
MESH = pl.DeviceIdType.MESH

N_CHIPS = 4
N_DEV = 8
CHUNK = 64
RET_HEADS = 4
RET_DK = 64
RET_DV = 128
QK_W = RET_HEADS * RET_DK
V_W = RET_HEADS * RET_DV
POOL_WINDOWS = (2, 4, 8, 16)
POOL_GC = 128
ROPE_BASE = 10000.0
RMS_EPS = 1e-6
GN_EPS = 1e-5
ADAM_LR = 0.001
ADAM_B1 = 0.9
ADAM_B2 = 0.999
ADAM_EPS = 1e-08
ADAM_WD = 0.01
ADAM_STEP = 10
MXU_W = 256
RET_BLOCK = MXU_W
VMEM_LIMIT = 56 * 1024 * 1024

NN = (((1,), (0,)), ((), ()))
NT = (((1,), (1,)), ((), ()))
TN = (((0,), (0,)), ((), ()))


def _dot(a, b, dims=NN):
    return lax.dot_general(a, b, dims, preferred_element_type=F32)


def _sigmoid(x):
    return 0.5 * jnp.tanh(0.5 * x) + 0.5


def _params(n_grid):
    return pltpu.CompilerParams(dimension_semantics=("arbitrary",) * n_grid, vmem_limit_bytes=VMEM_LIMIT)


class CommPlan(NamedTuple):
    ins: tuple
    out_shape: tuple
    sems: tuple
    start: Callable
    finish: Callable
    mid: Optional[Callable] = None
    mid_at: float = 0.85


ANY = pl.BlockSpec(memory_space=pl.ANY)


def _call(body, plan, *, name, grid, in_specs, out_specs, out_shape, args, scratch_shapes=()):
    n_grid = len(grid)
    if plan is None:
        return pl.pallas_call(body, name=name, grid=grid, in_specs=in_specs, out_specs=out_specs, out_shape=out_shape,
                              scratch_shapes=scratch_shapes, compiler_params=_params(n_grid))(*args)
    n_in, n_out, n_sc = len(in_specs), len(out_specs), len(scratch_shapes)
    p_in, p_out = len(plan.ins), len(plan.out_shape)
    total = math.prod(grid)
    mid_step = min(total - 1, int(plan.mid_at * total))

    def riding(*refs):
        ins, pins = refs[:n_in], refs[n_in:n_in + p_in]
        o0 = n_in + p_in
        outs, pouts = refs[o0:o0 + n_out], refs[o0 + n_out:o0 + n_out + p_out]
        s0 = o0 + n_out + p_out
        scratch, psems = refs[s0:s0 + n_sc], refs[s0 + n_sc:]
        step = pl.program_id(0)
        for d in range(1, n_grid):
            step = step * grid[d] + pl.program_id(d)

        @pl.when(step == 0)
        def _():
            plan.start(pins, pouts, psems)

        body(*ins, *outs, *scratch)

        if plan.mid is not None:
            @pl.when(step == mid_step)
            def _():
                plan.mid(pins, pouts, psems)

        @pl.when(step == total - 1)
        def _():
            plan.finish(pins, pouts, psems)

    res = pl.pallas_call(
        riding, name=name, grid=grid, in_specs=list(in_specs) + [ANY] * p_in, out_specs=list(out_specs) + [ANY] * p_out,
        out_shape=list(out_shape) + list(plan.out_shape), scratch_shapes=list(scratch_shapes) + list(plan.sems),
        compiler_params=_params(n_grid))(*args, *plan.ins)
    return res[:n_out], res[n_out:]


def _run_plan(plan, name):
    p_in, p_out = len(plan.ins), len(plan.out_shape)

    def body(*refs):
        pins, pouts, psems = refs[:p_in], refs[p_in:p_in + p_out], refs[p_in + p_out:]
        plan.start(pins, pouts, psems)
        if plan.mid is not None:
            plan.mid(pins, pouts, psems)
        plan.finish(pins, pouts, psems)

    return pl.pallas_call(body, name=name, in_specs=[ANY] * p_in, out_specs=[ANY] * p_out,
                          out_shape=list(plan.out_shape), scratch_shapes=list(plan.sems))(*plan.ins)


def _rms_parts(x):
    r = lax.rsqrt(jnp.mean(x * x, axis=-1, keepdims=True) + RMS_EPS)
    return r, x * r


def _rms_bwd(dn, xhat, r, gain, dres):
    dxh = dn * gain
    return dres + r * (dxh - xhat * jnp.mean(dxh * xhat, axis=-1, keepdims=True))


def _w3_specs(FS, D):
    return [pl.BlockSpec((None, None, FS, D), functools.partial(lambda i, j, k: (k, j, 0, 0), k=k)) for k in range(3)]


def ffn_fwd(x, gain, w3, tm, name, plan=None):
    T, D = x.shape
    _, J, FS, _ = w3.shape

    def body(x_ref, g_ref, wg_ref, wu_ref, wd_ref, xo_ref, b_ref, si_ref, t_ref, h_ref, n_ref, acc):
        j = pl.program_id(1)

        @pl.when(j == 0)
        def _():
            _, xhat = _rms_parts(x_ref[...])
            n_ref[...] = (xhat * g_ref[...]).astype(BF16)
            acc[...] = jnp.zeros_like(acc)

        n = n_ref[...]
        a = _dot(n, wg_ref[...], NT)
        b = _dot(n, wu_ref[...], NT)
        sg = _sigmoid(a)
        si = a * sg
        h = (si * b).astype(BF16)
        b_ref[...] = b.astype(BF16)
        si_ref[...] = si.astype(BF16)
        t_ref[...] = (sg + si * (1.0 - sg)).astype(BF16)
        h_ref[...] = h
        acc[...] += _dot(h, wd_ref[...])

        @pl.when(j == J - 1)
        def _():
            xo_ref[...] = x_ref[...] + 0.5 * acc[...]

    return _call(
        body, plan, name=name, grid=(T // tm, J),
        in_specs=[pl.BlockSpec((tm, D), lambda i, j: (i, 0)),
                  pl.BlockSpec((1, D), lambda i, j: (0, 0))] + _w3_specs(FS, D),
        out_specs=[pl.BlockSpec((tm, D), lambda i, j: (i, 0))] +
                  [pl.BlockSpec((None, tm, FS), lambda i, j: (j, i, 0))] * 4 +
                  [pl.BlockSpec((tm, D), lambda i, j: (i, 0))],
        out_shape=[jax.ShapeDtypeStruct((T, D), F32)] + [jax.ShapeDtypeStruct((J, T, FS), BF16)] * 4 +
                  [jax.ShapeDtypeStruct((T, D), BF16)],
        scratch_shapes=[pltpu.VMEM((tm, D), F32)],
        args=(x, gain, w3, w3, w3))


def ffn_bwd(df, b_s, si_s, t_s, w3, tm, name, plan=None):
    T, D = df.shape
    _, J, FS, _ = w3.shape

    chunks = [slice(s, min(s + MXU_W, FS)) for s in range(0, FS, MXU_W)]
    nc = len(chunks)

    def body(df_ref, b_ref, si_ref, t_ref, wg_ref, wu_ref, wd_ref, dn_ref, da_ref, db_ref):
        @pl.when(pl.program_id(1) == 0)
        def _():
            dn_ref[...] = jnp.zeros_like(dn_ref)

        def elementwise(dh, sl):
            da = (dh * b_ref[:, sl].astype(F32) * t_ref[:, sl].astype(F32)).astype(BF16)
            db = (dh * si_ref[:, sl].astype(F32)).astype(BF16)
            da_ref[:, sl] = da
            db_ref[:, sl] = db
            return da, db

        df = df_ref[...]
        dh, dab, acc = [None] * nc, [None] * nc, None
        for step in range(nc + 2):
            if step < nc:
                dh[step] = _dot(df, wd_ref[chunks[step], :], NT)
            if 1 <= step <= nc:
                dab[step - 1] = elementwise(dh[step - 1], chunks[step - 1])
            if step >= 2:
                sl = chunks[step - 2]
                part = _dot(dab[step - 2][0], wg_ref[sl, :]) + _dot(dab[step - 2][1], wu_ref[sl, :])
                acc = part if acc is None else acc + part
        dn_ref[...] += acc

    tok = pl.BlockSpec((tm, D), lambda i, j: (i, 0))
    sh = pl.BlockSpec((None, tm, FS), lambda i, j: (j, i, 0))
    return _call(
        body, plan, name=name, grid=(T // tm, J),
        in_specs=[tok, sh, sh, sh] + _w3_specs(FS, D),
        out_specs=[tok, sh, sh],
        out_shape=[jax.ShapeDtypeStruct((T, D), F32),
                   jax.ShapeDtypeStruct((J, T, FS), BF16),
                   jax.ShapeDtypeStruct((J, T, FS), BF16)],
        args=(df, b_s, si_s, t_s, w3, w3, w3))


def rms_bwd(x, dout, dn, gain, tm, name, plan=None):
    T, D = x.shape

    def body(x_ref, do_ref, dn_ref, g_ref, dx_ref, dgain_ref):
        @pl.when(pl.program_id(0) == 0)
        def _():
            dgain_ref[...] = jnp.zeros_like(dgain_ref)

        r, xhat = _rms_parts(x_ref[...])
        dn = dn_ref[...]
        dgain_ref[...] += jnp.sum(dn * xhat, axis=0, keepdims=True)
        dx_ref[...] = _rms_bwd(dn, xhat, r, g_ref[...], do_ref[...])

    tok = pl.BlockSpec((tm, D), lambda i: (i, 0))
    vec = pl.BlockSpec((1, D), lambda i: (0, 0))
    return _call(
        body, plan, name=name, grid=(T // tm,), in_specs=[tok, tok, tok, vec], out_specs=[tok, vec],
        out_shape=[jax.ShapeDtypeStruct((T, D), F32), jax.ShapeDtypeStruct((1, D), F32)],
        args=(x, dout, dn, gain))


def ffn_wgrad(da, db, h, n, df, tk, name, plan=None):
    J, T, FS = da.shape
    D = n.shape[1]

    def body(da_ref, db_ref, h_ref, n_ref, df_ref, o_ref):
        @pl.when(pl.program_id(1) == 0)
        def _():
            o_ref[...] = jnp.zeros_like(o_ref)

        nv = n_ref[...]
        o_ref[0] += _dot(da_ref[...], nv, TN)
        o_ref[1] += _dot(db_ref[...], nv, TN)
        o_ref[2] += _dot(h_ref[...], df_ref[...], TN)

    sh = pl.BlockSpec((None, tk, FS), lambda j, k: (j, k, 0))
    tok = pl.BlockSpec((tk, D), lambda j, k: (k, 0))
    return _call(
        body, plan, name=name, grid=(J, T // tk),
        in_specs=[sh, sh, sh, tok, tok],
        out_specs=[pl.BlockSpec((3, None, FS, D), lambda j, k: (0, j, 0, 0))],
        out_shape=[jax.ShapeDtypeStruct((3, J, FS, D), F32)],
        args=(da, db, h, n, df))


def wgrad_one(a, b, tk, name, plan=None):
    J, T, FS = a.shape
    D = b.shape[1]

    def body(a_ref, b_ref, o_ref):
        @pl.when(pl.program_id(1) == 0)
        def _():
            o_ref[...] = jnp.zeros_like(o_ref)

        o_ref[...] += _dot(a_ref[...], b_ref[...], TN)

    return _call(
        body, plan, name=name, grid=(J, T // tk),
        in_specs=[pl.BlockSpec((None, tk, FS), lambda j, k: (j, k, 0)), pl.BlockSpec((tk, D), lambda j, k: (k, 0))],
        out_specs=[pl.BlockSpec((None, FS, D), lambda j, k: (j, 0, 0))],
        out_shape=[jax.ShapeDtypeStruct((J, FS, D), F32)],
        args=(a, b))


def tn_shared(a, b, tk, name, plan=None):
    T, M = a.shape
    J, _, N = b.shape

    def body(a_ref, b_ref, o_ref):
        @pl.when(pl.program_id(0) == 0)
        def _():
            o_ref[...] = jnp.zeros_like(o_ref)

        a_t = a_ref[...].astype(BF16).T
        for j in range(J):
            o_ref[j] += _dot(a_t, b_ref[j].astype(BF16))

    return _call(
        body, plan, name=name, grid=(T // tk,),
        in_specs=[pl.BlockSpec((tk, M), lambda k: (k, 0)), pl.BlockSpec((J, tk, N), lambda k: (0, k, 0))],
        out_specs=[pl.BlockSpec((J, M, N), lambda k: (0, 0, 0))],
        out_shape=[jax.ShapeDtypeStruct((J, M, N), F32)],
        args=(a, b))


def in_proj(x, gain, win, tm, name):
    T, D = x.shape
    J, _, W = win.shape

    def body(x_ref, g_ref, w_ref, p_ref):
        _, xhat = _rms_parts(x_ref[...])
        n = (xhat * g_ref[...]).astype(BF16)
        for j in range(J):
            p_ref[j] = _dot(n, w_ref[j])

    return pl.pallas_call(
        body, name=name, grid=(T // tm,),
        in_specs=[pl.BlockSpec((tm, D), lambda i: (i, 0)),
                  pl.BlockSpec((1, D), lambda i: (0, 0)),
                  pl.BlockSpec((J, D, W), lambda i: (0, 0, 0))],
        out_specs=pl.BlockSpec((J, tm, W), lambda i: (0, i, 0)),
        out_shape=jax.ShapeDtypeStruct((J, T, W), F32),
        compiler_params=_params(1),
    )(x, gain, win)


def mixer_constants(S):
    B = RET_BLOCK
    half = RET_DK // 2
    freqs = ROPE_BASE ** (-jnp.arange(half, dtype=F32) * 2.0 / RET_DK)
    ang = jnp.arange(S, dtype=F32)[:, None] * freqs[None, :]
    cos = jnp.tile(jnp.cos(ang), (1, 2 * RET_HEADS))
    sin = jnp.tile(jnp.concatenate([-jnp.sin(ang), jnp.sin(ang)], axis=1), (1, RET_HEADS))
    gamma = 1.0 - 2.0 ** (-5.0 - jnp.arange(RET_HEADS, dtype=F32))
    log_g = jnp.log(gamma)
    idx = jnp.arange(B, dtype=F32)
    ci = jnp.arange(B) // CHUNK
    dist = jnp.abs(idx[:, None] - idx[None, :])
    m_intra = jnp.exp(log_g[:, None, None] * dist[None]) * (ci[None, :] <= ci[:, None])[None].astype(F32)
    lg_lane = jnp.repeat(log_g, RET_DK)
    d_q = jnp.exp(lg_lane[None, :] * (idx[:, None] + 1.0))
    d_k = jnp.exp(lg_lane[None, :] * (B - 1.0 - idx[:, None]))
    d_s = jnp.broadcast_to(jnp.exp(lg_lane * B)[:, None], (QK_W, V_W))
    bm = (jnp.arange(QK_W)[:, None] // RET_DK == jnp.arange(V_W)[None, :] // RET_DV).astype(F32)
    win = jnp.repeat(jnp.array(POOL_WINDOWS, F32), POOL_GC)
    invc = 1.0 / jnp.minimum(jnp.arange(S, dtype=F32)[:, None] + 1.0, win[None, :])
    return dict(cos=cos, sin=sin, m=m_intra, dq=d_q, dk=d_k, ds=d_s, bm=bm, invc=invc)


def _swap_halves(x):
    w = x.shape[1]
    lane = lax.broadcasted_iota(jnp.int32, x.shape, 1)
    return jnp.where((lane % RET_DK) < RET_DK // 2, pltpu.roll(x, w - RET_DK // 2, 1), pltpu.roll(x, RET_DK // 2, 1))


def _rot(x, cos, sin):
    return x * cos + _swap_halves(x) * sin


def _rot_t(d, cos, sin):
    return d * cos + _swap_halves(d * sin)


def _lane_groups(parts):
    return jnp.concatenate([p[:, POOL_GC * g:POOL_GC * (g + 1)] for g, p in enumerate(parts)], axis=1)


def _head_mask(shape, h):
    lane = lax.broadcasted_iota(jnp.int32, shape, 1)
    return (lane // RET_DK) == h


def _group_norm(o):
    yh, rs = [], []
    for h in range(RET_HEADS):
        oh = o[:, RET_DV * h:RET_DV * (h + 1)]
        xc = oh - jnp.mean(oh, axis=-1, keepdims=True)
        r = lax.rsqrt(jnp.mean(xc * xc, axis=-1, keepdims=True) + GN_EPS)
        yh.append(xc * r)
        rs.append(r)
    return yh, rs


def mixer_fwd(p, x1, consts, gn_gain, pool_w, pool_scale, wout, n_seq, S, name):
    _, T, _ = p.shape
    D = x1.shape[1]
    B = RET_BLOCK
    nb = S // B

    def body(p_ref, x1_ref, cos_ref, sin_ref, invc_ref, m_ref, dq_ref, dk_ref, ds_ref, bm_ref,
             gain_ref, pw_ref, sc_ref, wout_ref,
             x2_ref, m_out, o_out, st_out, pooled_out, state, prev_u):
        blk = pl.program_id(1)

        @pl.when(blk == 0)
        def _():
            state[...] = jnp.zeros_like(state)
            prev_u[...] = jnp.zeros_like(prev_u)

        qk = p_ref[0]
        v = p_ref[1]
        g = p_ref[2]
        u = p_ref[3]
        cos = cos_ref[...]
        sin = sin_ref[...]
        qr = _rot(qk[:, :QK_W], cos, sin) * (RET_DK ** -0.5)
        kr = _rot(qk[:, QK_W:], cos, sin)
        qb = qr.astype(BF16)
        kb = kr.astype(BF16)
        vb = v.astype(BF16)
        st = state[...]
        st_out[...] = st
        cross = _dot((qr * dq_ref[...]).astype(BF16), st.astype(BF16))
        o_parts = []
        for h in range(RET_HEADS):
            qm = jnp.where(_head_mask(qb.shape, h), qb, jnp.zeros_like(qb))
            sc = (_dot(qm, kb, NT) * m_ref[h]).astype(BF16)
            o_parts.append(_dot(sc, vb[:, RET_DV * h:RET_DV * (h + 1)]) + cross[:, RET_DV * h:RET_DV * (h + 1)])
        o = jnp.concatenate(o_parts, axis=1)
        o_out[...] = o
        kv = _dot((kr * dk_ref[...]).astype(BF16), vb, TN)
        state[...] = st * ds_ref[...] + kv * bm_ref[...]

        yh, _ = _group_norm(o)
        r = g * _sigmoid(g) * (jnp.concatenate(yh, axis=1) * gain_ref[...])

        ext = jnp.concatenate([prev_u[...], u], axis=0)
        sums = []
        run = ext
        for k in (1, 2, 4, 8):
            run = run + pltpu.roll(run, k, 0)
            sums.append(run[B:, :])
        prev_u[...] = u
        pooled = (_lane_groups(sums) * invc_ref[...] - u).astype(BF16)
        pooled_out[...] = pooled
        yp = [_dot(pooled[:, POOL_GC * gi:POOL_GC * (gi + 1)], pw_ref[gi].astype(BF16)) for gi in range(len(POOL_WINDOWS))]
        s = jnp.concatenate(yp, axis=1) * sc_ref[...]
        m = jnp.concatenate([r, s], axis=1).astype(BF16)
        m_out[...] = m
        x2_ref[...] = x1_ref[...] + _dot(m, wout_ref[...])

    tokmap = lambda b, k: (b * nb + k, 0)
    seqmap = lambda b, k: (k, 0)
    const2 = lambda b, k: (0, 0)
    const3 = lambda b, k: (0, 0, 0)
    return pl.pallas_call(
        body, name=name, grid=(n_seq, nb),
        in_specs=[pl.BlockSpec((4, B, V_W), lambda b, k: (0, b * nb + k, 0)),
                  pl.BlockSpec((B, D), tokmap),
                  pl.BlockSpec((B, QK_W), seqmap), pl.BlockSpec((B, QK_W), seqmap), pl.BlockSpec((B, V_W), seqmap),
                  pl.BlockSpec((RET_HEADS, B, B), const3),
                  pl.BlockSpec((B, QK_W), const2), pl.BlockSpec((B, QK_W), const2),
                  pl.BlockSpec((QK_W, V_W), const2), pl.BlockSpec((QK_W, V_W), const2),
                  pl.BlockSpec((1, V_W), const2), pl.BlockSpec((4, POOL_GC, POOL_GC), const3),
                  pl.BlockSpec((1, V_W), const2), pl.BlockSpec((D, D), const2)],
        out_specs=[pl.BlockSpec((B, D), tokmap), pl.BlockSpec((B, D), tokmap), pl.BlockSpec((B, V_W), tokmap),
                   pl.BlockSpec((None, QK_W, V_W), lambda b, k: (b * nb + k, 0, 0)),
                   pl.BlockSpec((B, V_W), tokmap)],
        out_shape=[jax.ShapeDtypeStruct((T, D), F32), jax.ShapeDtypeStruct((T, D), BF16),
                   jax.ShapeDtypeStruct((T, V_W), F32), jax.ShapeDtypeStruct((T // B, QK_W, V_W), F32),
                   jax.ShapeDtypeStruct((T, V_W), BF16)],
        scratch_shapes=[pltpu.VMEM((QK_W, V_W), F32), pltpu.VMEM((B, V_W), F32)],
        compiler_params=_params(2),
    )(p, x1, consts["cos"], consts["sin"], consts["invc"], consts["m"], consts["dq"], consts["dk"],
      consts["ds"], consts["bm"], gn_gain, pool_w, pool_scale, wout)


def mixer_bwd(dx3, dn3, x2, g3, x1, p, o_s, st_s, pooled_s, consts, gmix, gn_gain, pool_w, pool_scale, wout, win,
              n_seq, S, name, plan=None):
    T, D = x1.shape
    B = RET_BLOCK
    nb = S // B
    NG = len(POOL_WINDOWS)

    def body(dx3_ref, dn3_ref, x2_ref, g3_ref, x1_ref, p_ref, o_ref, st_ref, pooled_ref, cos_ref, sin_ref, invc_ref,
             m_ref, dq_ref, dk_ref, ds_ref, bm_ref, gmix_ref, gain_ref, pw_ref, sc_ref, wout_ref, win_ref,
             dx1_ref, df1_ref, dx2_ref, dp_ref, n2_ref, dg3_ref, dgmix_ref, dgain_ref, dscale_ref, dpw_ref,
             gstate, next_e):
        b = pl.program_id(0)
        blk = pl.program_id(1)

        @pl.when(jnp.logical_and(b == 0, blk == 0))
        def _():
            dg3_ref[...] = jnp.zeros_like(dg3_ref)
            dgmix_ref[...] = jnp.zeros_like(dgmix_ref)
            dgain_ref[...] = jnp.zeros_like(dgain_ref)
            dscale_ref[...] = jnp.zeros_like(dscale_ref)
            dpw_ref[...] = jnp.zeros_like(dpw_ref)

        @pl.when(blk == 0)
        def _():
            gstate[...] = jnp.zeros_like(gstate)
            next_e[...] = jnp.zeros_like(next_e)

        r2, xhat2 = _rms_parts(x2_ref[...])
        dn3 = dn3_ref[...]
        dg3_ref[...] += jnp.sum(dn3 * xhat2, axis=0, keepdims=True)
        dx2v = _rms_bwd(dn3, xhat2, r2, g3_ref[...], dx3_ref[...])
        dx2b = dx2v.astype(BF16)
        dx2_ref[...] = dx2b
        dm = _dot(dx2b, wout_ref[...], NT)
        dr = dm[:, :V_W]
        dsv = dm[:, V_W:]

        pooled = pooled_ref[...]
        scale = sc_ref[...]
        dyp = (dsv * scale).astype(BF16)
        yp, dpl = [], []
        for gi in range(NG):
            sl = slice(POOL_GC * gi, POOL_GC * (gi + 1))
            pwb = pw_ref[gi].astype(BF16)
            yp.append(_dot(pooled[:, sl], pwb))
            dpw_ref[gi] += _dot(pooled[:, sl], dyp[:, sl], TN)
            dpl.append(_dot(dyp[:, sl], pwb, NT))
        dscale_ref[...] += jnp.sum(dsv * jnp.concatenate(yp, axis=1), axis=0, keepdims=True)
        dpooled = jnp.concatenate(dpl, axis=1)
        e = dpooled * invc_ref[...]
        ext = jnp.concatenate([e, next_e[...]], axis=0)
        sums = []
        run = ext
        for k in (1, 2, 4, 8):
            run = run + pltpu.roll(run, 2 * B - k, 0)
            sums.append(run[:B, :])
        next_e[...] = e
        du = _lane_groups(sums) - dpooled

        qk = p_ref[0]
        v = p_ref[1]
        g = p_ref[2]
        sg = _sigmoid(g)
        si = g * sg
        yh, rs = _group_norm(o_ref[...])
        yhat = jnp.concatenate(yh, axis=1)
        gain = gain_ref[...]
        dg = dr * (yhat * gain) * (sg * (1.0 + g * (1.0 - sg)))
        dy = dr * si
        dgain_ref[...] += jnp.sum(dy * yhat, axis=0, keepdims=True)
        dyh = dy * gain
        do_parts = []
        for h in range(RET_HEADS):
            sl = slice(RET_DV * h, RET_DV * (h + 1))
            dh_ = dyh[:, sl]
            m1 = jnp.mean(dh_, axis=-1, keepdims=True)
            m2 = jnp.mean(dh_ * yh[h], axis=-1, keepdims=True)
            do_parts.append(rs[h] * (dh_ - m1 - yh[h] * m2))
        dob = jnp.concatenate(do_parts, axis=1).astype(BF16)

        cos = cos_ref[...]
        sin = sin_ref[...]
        qr = _rot(qk[:, :QK_W], cos, sin) * (RET_DK ** -0.5)
        kr = _rot(qk[:, QK_W:], cos, sin)
        qb = qr.astype(BF16)
        kb = kr.astype(BF16)
        vb = v.astype(BF16)
        dqd = dq_ref[...]
        dkd = dk_ref[...]
        stb = st_ref[...].astype(BF16)
        gs = gstate[...]
        gb = gs.astype(BF16)
        dqs = _dot(dob, stb, NT) * dqd
        dkr = _dot(vb, gb, NT) * dkd
        dv_cross = _dot((kr * dkd).astype(BF16), gb)
        ds_cross = _dot((qr * dqd).astype(BF16), dob, TN) * bm_ref[...]
        gstate[...] = ds_cross + gs * ds_ref[...]
        dv_parts = []
        for h in range(RET_HEADS):
            sl = slice(RET_DV * h, RET_DV * (h + 1))
            hm = _head_mask(qb.shape, h)
            qm = jnp.where(hm, qb, jnp.zeros_like(qb))
            mh = m_ref[h]
            sc = (_dot(qm, kb, NT) * mh).astype(BF16)
            dsc = (_dot(dob[:, sl], vb[:, sl], NT) * mh).astype(BF16)
            dqs = dqs + jnp.where(hm, _dot(dsc, kb), 0.0)
            dkr = dkr + jnp.where(hm, _dot(dsc, qb, TN), 0.0)
            dv_parts.append(_dot(sc, dob[:, sl], TN) + dv_cross[:, sl])
        dq = _rot_t(dqs * (RET_DK ** -0.5), cos, sin)
        dk = _rot_t(dkr, cos, sin)
        dp = [jnp.concatenate([dq, dk], axis=1).astype(BF16), jnp.concatenate(dv_parts, axis=1).astype(BF16),
              dg.astype(BF16), du.astype(BF16)]
        dn = jnp.zeros((B, D), F32)
        for jj in range(4):
            dp_ref[jj] = dp[jj]
            dn = dn + _dot(dp[jj], win_ref[jj], NT)

        x1v = x1_ref[...]
        r, xhat = _rms_parts(x1v)
        gm = gmix_ref[...]
        n2_ref[...] = (xhat * gm).astype(BF16)
        dgmix_ref[...] += jnp.sum(dn * xhat, axis=0, keepdims=True)
        dx1 = _rms_bwd(dn, xhat, r, gm, dx2v)
        dx1_ref[...] = dx1
        df1_ref[...] = (0.5 * dx1).astype(BF16)

    rev = lambda b, k: (b * nb + (nb - 1 - k), 0)
    seqrev = lambda b, k: (nb - 1 - k, 0)
    const2 = lambda b, k: (0, 0)
    const3 = lambda b, k: (0, 0, 0)
    return _call(
        body, plan, name=name, grid=(n_seq, nb),
        in_specs=[pl.BlockSpec((B, D), rev), pl.BlockSpec((B, D), rev), pl.BlockSpec((B, D), rev),
                  pl.BlockSpec((1, D), const2), pl.BlockSpec((B, D), rev),
                  pl.BlockSpec((4, B, V_W), lambda b, k: (0, b * nb + (nb - 1 - k), 0)),
                  pl.BlockSpec((B, V_W), rev),
                  pl.BlockSpec((None, QK_W, V_W), lambda b, k: (b * nb + (nb - 1 - k), 0, 0)),
                  pl.BlockSpec((B, V_W), rev),
                  pl.BlockSpec((B, QK_W), seqrev), pl.BlockSpec((B, QK_W), seqrev), pl.BlockSpec((B, V_W), seqrev),
                  pl.BlockSpec((RET_HEADS, B, B), const3),
                  pl.BlockSpec((B, QK_W), const2), pl.BlockSpec((B, QK_W), const2),
                  pl.BlockSpec((QK_W, V_W), const2), pl.BlockSpec((QK_W, V_W), const2),
                  pl.BlockSpec((1, D), const2), pl.BlockSpec((1, V_W), const2),
                  pl.BlockSpec((NG, POOL_GC, POOL_GC), const3), pl.BlockSpec((1, V_W), const2),
                  pl.BlockSpec((D, D), const2), pl.BlockSpec((4, D, V_W), const3)],
        out_specs=[pl.BlockSpec((B, D), rev), pl.BlockSpec((B, D), rev), pl.BlockSpec((B, D), rev),
                   pl.BlockSpec((4, B, V_W), lambda b, k: (0, b * nb + (nb - 1 - k), 0)),
                   pl.BlockSpec((B, D), rev),
                   pl.BlockSpec((1, D), const2), pl.BlockSpec((1, D), const2), pl.BlockSpec((1, V_W), const2),
                   pl.BlockSpec((1, V_W), const2), pl.BlockSpec((NG, POOL_GC, POOL_GC), const3)],
        out_shape=[jax.ShapeDtypeStruct((T, D), F32), jax.ShapeDtypeStruct((T, D), BF16),
                   jax.ShapeDtypeStruct((T, D), BF16), jax.ShapeDtypeStruct((4, T, V_W), BF16),
                   jax.ShapeDtypeStruct((T, D), BF16),
                   jax.ShapeDtypeStruct((1, D), F32), jax.ShapeDtypeStruct((1, D), F32), jax.ShapeDtypeStruct((1, V_W), F32),
                   jax.ShapeDtypeStruct((1, V_W), F32), jax.ShapeDtypeStruct((NG, POOL_GC, POOL_GC), F32)],
        scratch_shapes=[pltpu.VMEM((QK_W, V_W), F32), pltpu.VMEM((B, V_W), F32)],
        args=(dx3, dn3, x2, g3, x1, p, o_s, st_s, pooled_s, consts["cos"], consts["sin"], consts["invc"], consts["m"], consts["dq"],
              consts["dk"], consts["ds"], consts["bm"], gmix, gn_gain, pool_w, pool_scale, wout, win))


def loss_bwd(x3, tgt, gain, tm, name):
    T, D = x3.shape

    def body(x_ref, t_ref, g_ref, dx_ref, df_ref, lacc_ref, dg_ref):
        @pl.when(pl.program_id(0) == 0)
        def _():
            lacc_ref[...] = jnp.zeros_like(lacc_ref)
            dg_ref[...] = jnp.zeros_like(dg_ref)

        r, xhat = _rms_parts(x_ref[...])
        gv = g_ref[...]
        err = xhat * gv - t_ref[...]
        lacc_ref[...] += jnp.sum(err * err, axis=0, keepdims=True)
        dy = err * (1.0 / D)
        dg_ref[...] += jnp.sum(dy * xhat, axis=0, keepdims=True)
        dx = _rms_bwd(dy, xhat, r, gv, 0.0)
        dx_ref[...] = dx
        df_ref[...] = (0.5 * dx).astype(BF16)

    tok = pl.BlockSpec((tm, D), lambda i: (i, 0))
    vec = pl.BlockSpec((1, D), lambda i: (0, 0))
    return pl.pallas_call(
        body, name=name, grid=(T // tm,),
        in_specs=[tok, tok, vec], out_specs=[tok, tok, vec, vec],
        out_shape=[jax.ShapeDtypeStruct((T, D), F32), jax.ShapeDtypeStruct((T, D), BF16),
                   jax.ShapeDtypeStruct((1, D), F32), jax.ShapeDtypeStruct((1, D), F32)],
        compiler_params=_params(1),
    )(x3, tgt, gain)


def _adamw_math(w, g, m, v):
    m2 = ADAM_B1 * m + (1.0 - ADAM_B1) * g
    v2 = ADAM_B2 * v + (1.0 - ADAM_B2) * (g * g)
    m_hat = m2 / (1.0 - ADAM_B1 ** ADAM_STEP)
    v_hat = v2 / (1.0 - ADAM_B2 ** ADAM_STEP)
    return -ADAM_LR * (m_hat / (jnp.sqrt(v_hat) + ADAM_EPS) + ADAM_WD * w), m2, v2


def _place():
    x, y, c = lax.axis_index("x"), lax.axis_index("y"), lax.axis_index("c")
    other_chips = [(1 - x, y), (x, 1 - y), (1 - x, 1 - y)]
    return x, y, c, other_chips


def _exchange_plan(ins, out_shape, copies, n_copies):
    def descriptors(pins, pouts, psems):
        send, recv = psems
        return [pltpu.make_async_remote_copy(src_ref=s, dst_ref=d, send_sem=send.at[i], recv_sem=recv.at[i],
                                             device_id=dev, device_id_type=MESH)
                for i, (s, d, dev) in enumerate(copies(pins, pouts))]

    def start(pins, pouts, psems):
        for cp in descriptors(pins, pouts, psems):
            cp.start()

    def finish(pins, pouts, psems):
        for cp in descriptors(pins, pouts, psems):
            cp.wait()

    return CommPlan(tuple(ins), tuple(out_shape),
                    (pltpu.SemaphoreType.DMA((n_copies,)), pltpu.SemaphoreType.DMA((n_copies,))), start, finish)


def _combine(a, b):
    assert a.mid is None and b.mid is None
    ni, no, ns = len(a.ins), len(a.out_shape), len(a.sems)

    def start(pins, pouts, psems):
        a.start(pins[:ni], pouts[:no], psems[:ns])
        b.start(pins[ni:], pouts[no:], psems[ns:])

    def finish(pins, pouts, psems):
        a.finish(pins[:ni], pouts[:no], psems[:ns])
        b.finish(pins[ni:], pouts[no:], psems[ns:])

    return CommPlan(a.ins + b.ins, a.out_shape + b.out_shape, a.sems + b.sems, start, finish)


def gather_plan(shards):
    n = len(shards)

    def start(pins, pouts, psems):
        x, y, c, chips = _place()
        mine = 2 * x + y
        for w in range(n):
            for k, (px, py) in enumerate(chips):
                pltpu.make_async_remote_copy(
                    src_ref=pins[w].at[:, c], dst_ref=pouts[w].at[:, mine, c],
                    send_sem=psems[0].at[w, k], recv_sem=psems[1].at[w, k],
                    device_id=(px, py, c), device_id_type=MESH).start()

    def mid(pins, pouts, psems):
        x, y, c, chips = _place()
        for w in range(n):
            for k, (px, py) in enumerate(chips):
                landed = pouts[w].at[:, 2 * px + py, c]
                pltpu.make_async_remote_copy(
                    src_ref=landed, dst_ref=landed, send_sem=psems[0].at[w, k], recv_sem=psems[1].at[w, k],
                    device_id=(px, py, c), device_id_type=MESH).wait_recv()
                pltpu.make_async_remote_copy(
                    src_ref=landed, dst_ref=landed, send_sem=psems[2].at[w, k], recv_sem=psems[3].at[w, k],
                    device_id=(x, y, 1 - c), device_id_type=MESH).start()

    def finish(pins, pouts, psems):
        x, y, c, chips = _place()
        mine = 2 * x + y
        for w in range(n):
            for k, (px, py) in enumerate(chips):
                landed = pouts[w].at[:, 2 * px + py, 1 - c]
                cp = pltpu.make_async_remote_copy(
                    src_ref=landed, dst_ref=landed, send_sem=psems[2].at[w, k], recv_sem=psems[3].at[w, k],
                    device_id=(x, y, 1 - c), device_id_type=MESH)
                cp.wait_recv()
                cp.wait_send()
                pltpu.make_async_remote_copy(
                    src_ref=pins[w].at[:, c], dst_ref=pouts[w].at[:, mine, c],
                    send_sem=psems[0].at[w, k], recv_sem=psems[1].at[w, k],
                    device_id=(px, py, c), device_id_type=MESH).wait_send()

    return CommPlan(tuple(shards),
                    tuple(jax.ShapeDtypeStruct((s.shape[0], N_CHIPS) + s.shape[1:], s.dtype) for s in shards),
                    tuple(pltpu.SemaphoreType.DMA((n, 3)) for _ in range(4)), start, finish, mid)


def place_own(gathered, shard, name):
    L, _, Rh, C = shard.shape

    def body(chip_ref, g_ref, s_ref, o_ref):
        o_ref[...] = s_ref[...]

    return pl.pallas_call(
        body, name=name,
        grid_spec=pltpu.PrefetchScalarGridSpec(
            num_scalar_prefetch=1, grid=(L,),
            in_specs=[ANY, pl.BlockSpec((None, 2, Rh, C), lambda l, chip_ref: (l, 0, 0, 0))],
            out_specs=pl.BlockSpec((None, None, 2, Rh, C), lambda l, chip_ref: (l, chip_ref[0], 0, 0, 0))),
        out_shape=jax.ShapeDtypeStruct(gathered.shape, gathered.dtype),
        input_output_aliases={1: 0},
        compiler_params=_params(1),
    )(_chip_index(), gathered, shard)


def gather_weights(shards):
    n = len(shards)
    units = [(w, l) for w in range(n) for l in range(shards[w].shape[0])]
    nu = len(units)

    def body(*refs):
        ins, outs, bufs = refs[:n], refs[n:2 * n], refs[2 * n:3 * n]
        ld_sem, st_sem, own_send, own_recv, fwd_send, fwd_recv, d2d_send, d2d_recv = refs[3 * n:]
        x, y, c, _ = _place()
        sibling = (x, y, 1 - c)
        mine = 2 * x + y
        first = (x ^ (1 - c), y ^ c)
        second = (x ^ c, y ^ (1 - c))
        diagonal = (1 - x, 1 - y)

        def slot(chip):
            return 2 * chip[0] + chip[1]

        def remote(src, dst, send, recv, device):
            return pltpu.make_async_remote_copy(src_ref=src, dst_ref=dst, send_sem=send, recv_sem=recv,
                                                device_id=device, device_id_type=MESH)

        loads = [pltpu.make_async_copy(ins[w], bufs[w], ld_sem.at[w]) for w in range(n)]
        for cp in loads:
            cp.start()
        stores, sends = [], []
        for w in range(n):
            loads[w].wait()
            cp = pltpu.make_async_copy(bufs[w], outs[w].at[:, mine], st_sem.at[w])
            cp.start()
            stores.append(cp)
        for u, (w, l) in enumerate(units):
            for k, chip in enumerate((first, second)):
                cp = remote(bufs[w].at[l, c], outs[w].at[l, mine, c], own_send.at[u, k], own_recv.at[u, k], (*chip, c))
                cp.start()
                sends.append(cp)
        for u, (w, l) in enumerate(units):
            got = outs[w].at[l, slot(first), c]
            remote(got, got, own_send.at[u, 0], own_recv.at[u, 0], (*first, c)).wait_recv()
            for cp in (remote(got, got, fwd_send.at[u], fwd_recv.at[u], (*second, c)),
                       remote(got, got, d2d_send.at[u, 0], d2d_recv.at[u, 0], sibling)):
                cp.start()
                sends.append(cp)
        for u, (w, l) in enumerate(units):
            got = outs[w].at[l, slot(second), c]
            remote(got, got, own_send.at[u, 1], own_recv.at[u, 1], (*second, c)).wait_recv()
            cp = remote(got, got, d2d_send.at[u, 1], d2d_recv.at[u, 1], sibling)
            cp.start()
            sends.append(cp)
        for u, (w, l) in enumerate(units):
            got = outs[w].at[l, slot(diagonal), c]
            remote(got, got, fwd_send.at[u], fwd_recv.at[u], (*second, c)).wait_recv()
            cp = remote(got, got, d2d_send.at[u, 2], d2d_recv.at[u, 2], sibling)
            cp.start()
            sends.append(cp)
        for u, (w, l) in enumerate(units):
            for k, chip in enumerate((second, first, diagonal)):
                got = outs[w].at[l, slot(chip), 1 - c]
                remote(got, got, d2d_send.at[u, k], d2d_recv.at[u, k], sibling).wait_recv()
        for cp in sends:
            cp.wait_send()
        for cp in stores:
            cp.wait()

    return pl.pallas_call(
        body, name="gather_weights",
        in_specs=[ANY] * n, out_specs=[ANY] * n,
        out_shape=[jax.ShapeDtypeStruct((s.shape[0], N_CHIPS) + s.shape[1:], s.dtype) for s in shards],
        scratch_shapes=[pltpu.VMEM(s.shape, s.dtype) for s in shards] +
                       [pltpu.SemaphoreType.DMA((n,)), pltpu.SemaphoreType.DMA((n,)),
                        pltpu.SemaphoreType.DMA((nu, 2)), pltpu.SemaphoreType.DMA((nu, 2)),
                        pltpu.SemaphoreType.DMA((nu,)), pltpu.SemaphoreType.DMA((nu,)),
                        pltpu.SemaphoreType.DMA((nu, 3)), pltpu.SemaphoreType.DMA((nu, 3))],
        compiler_params=pltpu.CompilerParams(vmem_limit_bytes=VMEM_LIMIT),
    )(*shards)


def send_to_sibling_other_half(grads):
    def copies(ins, outs):
        x, y, c, _ = _place()
        return [(ins[w].at[:, :, 1 - c], outs[w], (x, y, 1 - c)) for w in range(len(grads))]

    return _exchange_plan(grads, [jax.ShapeDtypeStruct(g.shape[:2] + g.shape[3:], g.dtype) for g in grads], copies,
                          len(grads))


def _core_index():
    return jnp.reshape(lax.axis_index("c"), (1,)).astype(jnp.int32)


def _chip_index():
    return jnp.reshape(2 * lax.axis_index("x") + lax.axis_index("y"), (1,)).astype(jnp.int32)


def add_own_half(g, recv, name):
    L, J, _, Rh, C = g.shape

    def body(c_ref, g_ref, r_ref, o_ref):
        o_ref[...] = (g_ref[...] + r_ref[...]).astype(BF16)

    return pl.pallas_call(
        body, name=name,
        grid_spec=pltpu.PrefetchScalarGridSpec(
            num_scalar_prefetch=1, grid=(L, J),
            in_specs=[pl.BlockSpec((None, None, None, Rh, C), lambda l, j, c_ref: (l, j, c_ref[0], 0, 0)),
                      pl.BlockSpec((None, None, Rh, C), lambda l, j, c_ref: (l, j, 0, 0))],
            out_specs=pl.BlockSpec((None, None, Rh, C), lambda l, j, c_ref: (l, j, 0, 0))),
        out_shape=jax.ShapeDtypeStruct((L, J, Rh, C), BF16),
        compiler_params=_params(2),
    )(_core_index(), g, recv)


def exchange_between_chips(sums):
    def copies(ins, outs):
        x, y, c, chips = _place()
        return [(ins[w].at[:, 2 * px + py], outs[w].at[k], (px, py, c))
                for w in range(len(sums)) for k, (px, py) in enumerate(chips)]

    return _exchange_plan(sums, [jax.ShapeDtypeStruct((3, s.shape[0]) + s.shape[2:], s.dtype) for s in sums], copies,
                          3 * len(sums))


def sum_chips(own, recv, name):
    L, _, Rh, C = own.shape

    def body(chip_ref, o_ref, r_ref, out_ref):
        acc = o_ref[...].astype(F32)
        for k in range(3):
            acc = acc + r_ref[k].astype(F32)
        out_ref[...] = acc

    return pl.pallas_call(
        body, name=name,
        grid_spec=pltpu.PrefetchScalarGridSpec(
            num_scalar_prefetch=1, grid=(L,),
            in_specs=[pl.BlockSpec((None, None, Rh, C), lambda l, chip_ref: (l, chip_ref[0], 0, 0)),
                      pl.BlockSpec((3, None, Rh, C), lambda l, chip_ref: (0, l, 0, 0))],
            out_specs=pl.BlockSpec((None, Rh, C), lambda l, chip_ref: (l, 0, 0))),
        out_shape=jax.ShapeDtypeStruct((L, Rh, C), F32),
        compiler_params=_params(1),
    )(_chip_index(), own, recv)


def share_with_sibling(halves):
    def copies(ins, outs):
        x, y, c, _ = _place()
        return [(ins[w], outs[w], (x, y, 1 - c)) for w in range(len(halves))]

    return _exchange_plan(halves, [jax.ShapeDtypeStruct(h.shape, h.dtype) for h in halves], copies, len(halves))


def adamw_group(ws, ms, vs, own, sib, name, plan=None):
    L = len(ws)
    R, C = ws[0].shape
    Rh = R // 2
    tr = Rh // 2
    nr = Rh // tr

    def body(*refs):
        w, m, v = refs[:L], refs[L:2 * L], refs[2 * L:3 * L]
        own_ref, sib_ref = refs[3 * L], refs[3 * L + 1]
        outs = refs[3 * L + 2:]
        mine = pl.program_id(0) == lax.axis_index("c")
        for l in range(L):
            gv = jnp.where(mine, own_ref[l], sib_ref[l])
            d, m2, v2 = _adamw_math(w[l][...], gv, m[l][...], v[l][...])
            outs[4 * l][...] = d
            outs[4 * l + 1][...] = m2
            outs[4 * l + 2][...] = v2
            outs[4 * l + 3][...] = gv

    blk = pl.BlockSpec((tr, C), lambda h, r: (h * nr + r, 0))
    half = pl.BlockSpec((L, tr, C), lambda h, r: (0, r, 0))
    sds = jax.ShapeDtypeStruct((R, C), F32)
    return _call(body, plan, name=name, grid=(2, nr), in_specs=[blk] * (3 * L) + [half, half],
                 out_specs=[blk] * (4 * L), out_shape=[sds] * (4 * L), args=(*ws, *ms, *vs, own, sib))


def gather_small_plan(part):
    def copies(ins, outs):
        x, y, c, _ = _place()
        me = 4 * x + 2 * y + c
        return [(ins[0], outs[0].at[me], (x ^ ((k >> 2) & 1), y ^ ((k >> 1) & 1), c ^ (k & 1))) for k in range(1, N_DEV)]

    return _exchange_plan([part], [jax.ShapeDtypeStruct((N_DEV,) + part.shape, part.dtype)], copies, N_DEV - 1)


def sum_small(part, gathered, name):
    R, C = part.shape

    def body(p_ref, g_ref, o_ref):
        d = pl.program_id(0)
        me = 4 * lax.axis_index("x") + 2 * lax.axis_index("y") + lax.axis_index("c")
        val = jnp.where(d == me, p_ref[...], g_ref[...])

        @pl.when(d == 0)
        def _():
            o_ref[...] = val

        @pl.when(d > 0)
        def _():
            o_ref[...] += val

    return pl.pallas_call(
        body, name=name, grid=(N_DEV,),
        in_specs=[pl.BlockSpec((R, C), lambda d: (0, 0)), pl.BlockSpec((None, R, C), lambda d: (d, 0, 0))],
        out_specs=pl.BlockSpec((R, C), lambda d: (0, 0)),
        out_shape=jax.ShapeDtypeStruct((R, C), F32),
        compiler_params=_params(1),
    )(part, gathered)


SMALL = ("norm_ffn1", "norm_mix", "norm_ffn2", "norm_final", "ret_gn_gain", "pool_scale", "pool_w")
GROUPS = (("ffn1", ("ffn1_gate", "ffn1_up", "ffn1_down")), ("w_in", ("w_in",)), ("w_out", ("w_out",)),
          ("ffn2", ("ffn2_gate", "ffn2_up", "ffn2_down")))
TRANSPOSED = ("ffn1_gate", "ffn1_up", "ffn2_gate", "ffn2_up")
IN_W = 2 * QK_W + 3 * V_W
ORDER = ("norm_ffn1", "ffn1_gate", "ffn1_up", "ffn1_down", "norm_mix", "w_in", "ret_gn_gain", "pool_w", "pool_scale",
         "w_out", "norm_ffn2", "ffn2_gate", "ffn2_up", "ffn2_down", "norm_final")


SUBLANES = 8


def _small_rows(shapes):
    rows = [math.prod(shapes[k]) // 128 for k in SMALL]
    offs, off = [], 0
    for r in rows:
        offs.append(off)
        off += -(-r // SUBLANES) * SUBLANES
    return rows, offs, off


def _pack_small(d, shapes):
    rows, offs, _ = _small_rows(shapes)
    pieces = []
    for k, r in zip(SMALL[1:], rows[1:]):
        pieces.append(d[k].reshape(r, 128))
        if r % SUBLANES:
            pieces.append(jnp.zeros((SUBLANES - r % SUBLANES, 128), F32))
    return jnp.concatenate(pieces, axis=0)


def adamw_small(W, M, V, first, summed, shapes):
    rows, offs, _ = _small_rows(shapes)
    n = len(SMALL)

    def body(*refs):
        w, m, v, f_ref, s_ref = refs[:n], refs[n:2 * n], refs[2 * n:3 * n], refs[3 * n], refs[3 * n + 1]
        outs = refs[3 * n + 2:]
        for i, (r, off) in enumerate(zip(rows, offs)):
            gv = f_ref[...] if i == 0 else s_ref[off - offs[1]:off - offs[1] + r, :]
            d, m2, v2 = _adamw_math(w[i][...], gv, m[i][...], v[i][...])
            outs[4 * i][...] = d
            outs[4 * i + 1][...] = m2
            outs[4 * i + 2][...] = v2
            outs[4 * i + 3][...] = gv

    vm = pl.BlockSpec(memory_space=pltpu.VMEM)
    flat = lambda d: [d[k].reshape(r, 128) for k, r in zip(SMALL, rows)]
    res = pl.pallas_call(
        body, name="adamw_small", in_specs=[vm] * (3 * n + 2), out_specs=[vm] * (4 * n),
        out_shape=[jax.ShapeDtypeStruct((r, 128), F32) for r in rows for _ in range(4)],
    )(*flat(W), *flat(M), *flat(V), first, summed)
    return {k: [a.reshape(shapes[k]) for a in res[4 * i:4 * i + 4]] for i, k in enumerate(SMALL)}


def kernel(x, norm_ffn1, ffn1_gate, ffn1_up, ffn1_down, norm_mix, w_in, ret_gn_gain, pool_w, pool_scale, w_out, norm_ffn2, ffn2_gate, ffn2_up, ffn2_down, norm_final, loss_target, m_norm_ffn1, m_ffn1_gate, m_ffn1_up, m_ffn1_down, m_norm_mix, m_w_in, m_ret_gn_gain, m_pool_w, m_pool_scale, m_w_out, m_norm_ffn2, m_ffn2_gate, m_ffn2_up, m_ffn2_down, m_norm_final, v_norm_ffn1, v_ffn1_gate, v_ffn1_up, v_ffn1_down, v_norm_mix, v_w_in, v_ret_gn_gain, v_pool_w, v_pool_scale, v_w_out, v_norm_ffn2, v_ffn2_gate, v_ffn2_up, v_ffn2_down, v_norm_final):
    W = dict(norm_ffn1=norm_ffn1, ffn1_gate=ffn1_gate, ffn1_up=ffn1_up, ffn1_down=ffn1_down, norm_mix=norm_mix, w_in=w_in,
             ret_gn_gain=ret_gn_gain, pool_w=pool_w, pool_scale=pool_scale, w_out=w_out, norm_ffn2=norm_ffn2,
             ffn2_gate=ffn2_gate, ffn2_up=ffn2_up, ffn2_down=ffn2_down, norm_final=norm_final)
    M = dict(norm_ffn1=m_norm_ffn1, ffn1_gate=m_ffn1_gate, ffn1_up=m_ffn1_up, ffn1_down=m_ffn1_down, norm_mix=m_norm_mix,
             w_in=m_w_in, ret_gn_gain=m_ret_gn_gain, pool_w=m_pool_w, pool_scale=m_pool_scale, w_out=m_w_out,
             norm_ffn2=m_norm_ffn2, ffn2_gate=m_ffn2_gate, ffn2_up=m_ffn2_up, ffn2_down=m_ffn2_down, norm_final=m_norm_final)
    V = dict(norm_ffn1=v_norm_ffn1, ffn1_gate=v_ffn1_gate, ffn1_up=v_ffn1_up, ffn1_down=v_ffn1_down, norm_mix=v_norm_mix,
             w_in=v_w_in, ret_gn_gain=v_ret_gn_gain, pool_w=v_pool_w, pool_scale=v_pool_scale, w_out=v_w_out,
             norm_ffn2=v_norm_ffn2, ffn2_gate=v_ffn2_gate, ffn2_up=v_ffn2_up, ffn2_down=v_ffn2_down, norm_final=v_norm_final)

    n_seq, S, D = x.shape
    T = n_seq * S
    tm = min(1024, T // 2)
    tk = min(1024, T // 2)
    tk1 = min(2048, T // 2)
    xf = x.reshape(T, D)
    tgt = loss_target.reshape(T, D)

    def local(d, k):
        a = d[k][0]
        return a.T if k in TRANSPOSED else a

    def to_out(k, a):
        return (a.T if k in TRANSPOSED else a)[None]

    loc_bf = []
    for _, members in GROUPS:
        st = jnp.stack([local(W, k).astype(BF16) for k in members])
        loc_bf.append(st.reshape(st.shape[0], 2, st.shape[1] // 2, st.shape[2]))
    FS = loc_bf[0].shape[2] * 2
    w3_1 = gather_weights(loc_bf[:1])[0].reshape(3, N_CHIPS, FS, D)
    g1 = norm_ffn1.reshape(1, D)
    gm = norm_mix.reshape(1, D)
    g3 = norm_ffn2.reshape(1, D)
    gf = norm_final.reshape(1, D)
    gn_gain = ret_gn_gain.reshape(1, V_W)
    pscale = pool_scale.reshape(1, V_W)
    pw = pool_w.reshape(len(POOL_WINDOWS), POOL_GC, POOL_GC)
    consts = mixer_constants(S)

    def halves5(g, lb):
        return g.reshape(lb.shape[0], N_CHIPS, 2, lb.shape[2], lb.shape[3])

    (x1, b1, si1, t1, h1, n1), landed = ffn_fwd(xf, g1, w3_1, tm, "ffn1_fwd", plan=gather_plan(loc_bf[1:]))
    gathered = [place_own(g, lb, "place_own_" + gn) for g, lb, (gn, _) in zip(landed, loc_bf[1:], GROUPS[1:])]
    win_full = gathered[0].reshape(N_CHIPS, D, IN_W // N_CHIPS)
    wout_full = gathered[1].reshape(D, D)
    w3_2 = gathered[2].reshape(3, N_CHIPS, FS, D)
    p = in_proj(x1, gm, win_full, tm, "in_proj")
    x2, m_s, o_s, st_s, pooled_s = mixer_fwd(p, x1, consts, gn_gain, pw, pscale, wout_full, n_seq, S, "mixer_fwd")
    x3, b2, si2, t2, h2, n3 = ffn_fwd(x2, g3, w3_2, tm, "ffn2_fwd")

    dx3, df2, lacc, dgf = loss_bwd(x3, tgt, gf, tm, "loss_bwd")
    dn3, da2, db2 = ffn_bwd(df2, b2, si2, t2, w3_2, tm, "ffn2_bwd")
    to_sib, to_chips, back = send_to_sibling_other_half, exchange_between_chips, share_with_sibling
    out_g, out_d, out_m, out_v = {}, {}, {}, {}

    def update(members, own, sib, name):
        res = adamw_group([local(W, k) for k in members], [local(M, k) for k in members], [local(V, k) for k in members],
                          own, sib, "adamw_" + name)
        for l, k in enumerate(members):
            out_d[k], out_m[k], out_v[k], out_g[k] = (to_out(k, r) for r in res[4 * l:4 * l + 4])

    (gr_ffn2,) = ffn_wgrad(da2, db2, h2, n3, df2, tk, "ffn2_wgrad")
    g_ffn2 = halves5(gr_ffn2, loc_bf[3])
    (dx1, df1, dx2b, dp, n2, dg3, dgm, dgn, dps, dpw), (sb_ffn2,) = mixer_bwd(
        dx3, dn3, x2, g3, x1, p, o_s, st_s, pooled_s, consts, gm, gn_gain, pw, pscale, wout_full, win_full, n_seq, S,
        "mixer_bwd", plan=to_sib([g_ffn2]))
    cs_ffn2 = add_own_half(g_ffn2, sb_ffn2, "add_sibling_ffn2")
    (gr_wout,) = tn_shared(m_s, dx2b[None], tk1, "dw_out")
    (gr_win,) = tn_shared(n2, dp, tk1, "dw_in")
    g_mix = [halves5(gr_win, loc_bf[1]), halves5(gr_wout, loc_bf[2])]
    shapes = {k: W[k].shape for k in SMALL}
    small_part = dict(norm_mix=dgm, norm_ffn2=dg3, norm_final=dgf, ret_gn_gain=dgn, pool_scale=dps, pool_w=dpw)
    part = jnp.concatenate([_pack_small(small_part, shapes), lacc.reshape(-1, 128)], axis=0)
    n_small = part.shape[0] - lacc.size // 128
    (dn1, da1, db1), (pc_ffn2, sb_win, sb_wout, parts) = ffn_bwd(
        df1, b1, si1, t1, w3_1, tm, "ffn1_bwd",
        plan=_combine(_combine(to_chips([cs_ffn2]), to_sib(g_mix)), gather_small_plan(part)))
    summed = sum_small(part, parts, "sum_small")
    loss = jnp.sum(summed[n_small:]) * (0.5 / D)
    mh_ffn2 = sum_chips(cs_ffn2, pc_ffn2, "sum_chips_ffn2")
    cs_mix = [add_own_half(g, r, "add_sibling_" + gn) for g, r, gn in zip(g_mix, (sb_win, sb_wout), ("w_in", "w_out"))]
    def one(g):
        return g.reshape((1, N_CHIPS, 2) + loc_bf[0].shape[2:])

    (gr_g,), (sh_ffn2, pc_win, pc_wout) = wgrad_one(
        da1, n1, tk1, "ffn1_wgrad_gate", plan=_combine(back([mh_ffn2]), to_chips(cs_mix)))
    mh_mix = [sum_chips(cs, pc, "sum_chips_" + gn) for cs, pc, gn in zip(cs_mix, (pc_win, pc_wout), ("w_in", "w_out"))]
    (gr_u,), (sb_g,) = wgrad_one(db1, n1, tk1, "ffn1_wgrad_up", plan=to_sib([one(gr_g)]))
    cs_g = add_own_half(one(gr_g), sb_g, "add_sibling_ffn1_gate")
    (gr_d,), (pc_g, sb_u) = wgrad_one(
        h1, df1, tk1, "ffn1_wgrad_down", plan=_combine(to_chips([cs_g]), to_sib([one(gr_u)])))
    mh_g = sum_chips(cs_g, pc_g, "sum_chips_ffn1_gate")
    cs_u = add_own_half(one(gr_u), sb_u, "add_sibling_ffn1_up")
    (dx0, dg1), (sh_g, sh_win, sh_wout, pc_u, sb_d) = rms_bwd(
        xf, dx1, dn1, g1, tm, "ffn1_rms_bwd",
        plan=_combine(_combine(back([mh_g] + mh_mix), to_chips([cs_u])), to_sib([one(gr_d)])))
    mh_u = sum_chips(cs_u, pc_u, "sum_chips_ffn1_up")
    cs_d = add_own_half(one(gr_d), sb_d, "add_sibling_ffn1_down")
    first = dg1.reshape(-1, 128)
    sh_u, pc_d, firsts = _run_plan(
        _combine(_combine(back([mh_u]), to_chips([cs_d])), gather_small_plan(first)), "ffn1_grads_stage_b")
    mh_d = sum_chips(cs_d, pc_d, "sum_chips_ffn1_down")
    (sh_d,) = _run_plan(back([mh_d]), "ffn1_grads_stage_c")
    first_sum = sum_small(first, firsts, "sum_small_first")
    for k, own, sib in zip(GROUPS[0][1], (mh_g, mh_u, mh_d), (sh_g, sh_u, sh_d)):
        update((k,), own, sib, k)
    update(GROUPS[1][1], mh_mix[0], sh_win, "w_in")
    update(GROUPS[2][1], mh_mix[1], sh_wout, "w_out")
    update(GROUPS[3][1], mh_ffn2, sh_ffn2, "ffn2")

    for k, (d_, m_, v_, g_) in adamw_small(W, M, V, first_sum, summed, shapes).items():
        out_d[k], out_m[k], out_v[k], out_g[k] = d_, m_, v_, g_

    grad_x = dx0.reshape(n_seq, S, D)
    return (loss, grad_x, *[out_g[k] for k in ORDER], *[out_d[k] for k in ORDER],
            *[out_m[k] for k in ORDER], *[out_v[k] for k in ORDER])
```

```python
import functools
import math
from typing import Callable, NamedTuple, Optional

import jax
import jax.numpy as jnp
from jax import lax
from jax.experimental import pallas as pl
from jax.experimental.pallas import tpu as pltpu

F32 = jnp.float32
BF16 = jnp.bfloat16
MESH = pl.DeviceIdType.MESH

N_CHIPS = 4
N_DEV = 8
CHUNK = 64
RET_HEADS = 4
RET_DK = 64
RET_DV = 128
QK_W = RET_HEADS * RET_DK
V_W = RET_HEADS * RET_DV
POOL_WINDOWS = (2, 4, 8, 16)
POOL_GC = 128
ROPE_BASE = 10000.0
RMS_EPS = 1e-6
GN_EPS = 1e-5
ADAM_LR = 0.001
ADAM_B1 = 0.9
ADAM_B2 = 0.999
ADAM_EPS = 1e-08
ADAM_WD = 0.01
ADAM_STEP = 10
MXU_W = 256
RET_BLOCK = MXU_W
VMEM_LIMIT = 56 * 1024 * 1024

NN = (((1,), (0,)), ((), ()))
NT = (((1,), (1,)), ((), ()))
TN = (((0,), (0,)), ((), ()))


def _dot(a, b, dims=NN):
    return lax.dot_general(a, b, dims, preferred_element_type=F32)


def _sigmoid(x):
    return 0.5 * jnp.tanh(0.5 * x) + 0.5


def _params(n_grid):
    return pltpu.CompilerParams(dimension_semantics=("arbitrary",) * n_grid, vmem_limit_bytes=VMEM_LIMIT)


class CommPlan(NamedTuple):
    ins: tuple
    out_shape: tuple
    sems: tuple
    start: Callable
    finish: Callable
    mid: Optional[Callable] = None
    mid_at: float = 0.85


ANY = pl.BlockSpec(memory_space=pl.ANY)


def _call(body, plan, *, name, grid, in_specs, out_specs, out_shape, args, scratch_shapes=()):
    n_grid = len(grid)
    if plan is None:
        return pl.pallas_call(body, name=name, grid=grid, in_specs=in_specs, out_specs=out_specs, out_shape=out_shape,
                              scratch_shapes=scratch_shapes, compiler_params=_params(n_grid))(*args)
    n_in, n_out, n_sc = len(in_specs), len(out_specs), len(scratch_shapes)
    p_in, p_out = len(plan.ins), len(plan.out_shape)
    total = math.prod(grid)
    mid_step = min(total - 1, int(plan.mid_at * total))

    def riding(*refs):
        ins, pins = refs[:n_in], refs[n_in:n_in + p_in]
        o0 = n_in + p_in
        outs, pouts = refs[o0:o0 + n_out], refs[o0 + n_out:o0 + n_out + p_out]
        s0 = o0 + n_out + p_out
        scratch, psems = refs[s0:s0 + n_sc], refs[s0 + n_sc:]
        step = pl.program_id(0)
        for d in range(1, n_grid):
            step = step * grid[d] + pl.program_id(d)

        @pl.when(step == 0)
        def _():
            plan.start(pins, pouts, psems)

        body(*ins, *outs, *scratch)

        if plan.mid is not None:
            @pl.when(step == mid_step)
            def _():
                plan.mid(pins, pouts, psems)

        @pl.when(step == total - 1)
        def _():
            plan.finish(pins, pouts, psems)

    res = pl.pallas_call(
        riding, name=name, grid=grid, in_specs=list(in_specs) + [ANY] * p_in, out_specs=list(out_specs) + [ANY] * p_out,
        out_shape=list(out_shape) + list(plan.out_shape), scratch_shapes=list(scratch_shapes) + list(plan.sems),
        compiler_params=_params(n_grid))(*args, *plan.ins)
    return res[:n_out], res[n_out:]


def _run_plan(plan, name):
    p_in, p_out = len(plan.ins), len(plan.out_shape)

    def body(*refs):
        pins, pouts, psems = refs[:p_in], refs[p_in:p_in + p_out], refs[p_in + p_out:]
        plan.start(pins, pouts, psems)
        if plan.mid is not None:
            plan.mid(pins, pouts, psems)
        plan.finish(pins, pouts, psems)

    return pl.pallas_call(body, name=name, in_specs=[ANY] * p_in, out_specs=[ANY] * p_out,
                          out_shape=list(plan.out_shape), scratch_shapes=list(plan.sems))(*plan.ins)


def _rms_parts(x):
    r = lax.rsqrt(jnp.mean(x * x, axis=-1, keepdims=True) + RMS_EPS)
    return r, x * r


def _rms_bwd(dn, xhat, r, gain, dres):
    dxh = dn * gain
    return dres + r * (dxh - xhat * jnp.mean(dxh * xhat, axis=-1, keepdims=True))


def _w3_specs(FS, D):
    return [pl.BlockSpec((None, None, FS, D), functools.partial(lambda i, j, k: (k, j, 0, 0), k=k)) for k in range(3)]


def ffn_fwd(x, gain, w3, tm, name, plan=None):
    T, D = x.shape
    _, J, FS, _ = w3.shape

    def body(x_ref, g_ref, wg_ref, wu_ref, wd_ref, xo_ref, b_ref, si_ref, t_ref, n_ref, acc):
        j = pl.program_id(1)

        @pl.when(j == 0)
        def _():
            _, xhat = _rms_parts(x_ref[...])
            n_ref[...] = (xhat * g_ref[...]).astype(BF16)
            acc[...] = jnp.zeros_like(acc)

        n = n_ref[...]
        a = _dot(n, wg_ref[...], NT)
        b = _dot(n, wu_ref[...], NT)
        sg = _sigmoid(a)
        si = a * sg
        b_ref[...] = b.astype(BF16)
        si_ref[...] = si.astype(BF16)
        t_ref[...] = (sg + si * (1.0 - sg)).astype(BF16)
        acc[...] += _dot((si * b).astype(BF16), wd_ref[...])

        @pl.when(j == J - 1)
        def _():
            xo_ref[...] = x_ref[...] + 0.5 * acc[...]

    return _call(
        body, plan, name=name, grid=(T // tm, J),
        in_specs=[pl.BlockSpec((tm, D), lambda i, j: (i, 0)),
                  pl.BlockSpec((1, D), lambda i, j: (0, 0))] + _w3_specs(FS, D),
        out_specs=[pl.BlockSpec((tm, D), lambda i, j: (i, 0))] +
                  [pl.BlockSpec((None, tm, FS), lambda i, j: (j, i, 0))] * 3 +
                  [pl.BlockSpec((tm, D), lambda i, j: (i, 0))],
        out_shape=[jax.ShapeDtypeStruct((T, D), F32)] + [jax.ShapeDtypeStruct((J, T, FS), BF16)] * 3 +
                  [jax.ShapeDtypeStruct((T, D), BF16)],
        scratch_shapes=[pltpu.VMEM((tm, D), F32)],
        args=(x, gain, w3, w3, w3))


def ffn_bwd(df, b_s, si_s, t_s, w3, tm, name, plan=None):
    T, D = df.shape
    _, J, FS, _ = w3.shape

    chunks = [slice(s, min(s + MXU_W, FS)) for s in range(0, FS, MXU_W)]
    nc = len(chunks)

    def body(df_ref, b_ref, si_ref, t_ref, wg_ref, wu_ref, wd_ref, dn_ref, da_ref, db_ref):
        @pl.when(pl.program_id(1) == 0)
        def _():
            dn_ref[...] = jnp.zeros_like(dn_ref)

        def elementwise(dh, sl):
            da = (dh * b_ref[:, sl].astype(F32) * t_ref[:, sl].astype(F32)).astype(BF16)
            db = (dh * si_ref[:, sl].astype(F32)).astype(BF16)
            da_ref[:, sl] = da
            db_ref[:, sl] = db
            return da, db

        df = df_ref[...]
        dh, dab, acc = [None] * nc, [None] * nc, None
        for step in range(nc + 2):
            if step < nc:
                dh[step] = _dot(df, wd_ref[chunks[step], :], NT)
            if 1 <= step <= nc:
                dab[step - 1] = elementwise(dh[step - 1], chunks[step - 1])
            if step >= 2:
                sl = chunks[step - 2]
                part = _dot(dab[step - 2][0], wg_ref[sl, :]) + _dot(dab[step - 2][1], wu_ref[sl, :])
                acc = part if acc is None else acc + part
        dn_ref[...] += acc

    tok = pl.BlockSpec((tm, D), lambda i, j: (i, 0))
    sh = pl.BlockSpec((None, tm, FS), lambda i, j: (j, i, 0))
    return _call(
        body, plan, name=name, grid=(T // tm, J),
        in_specs=[tok, sh, sh, sh] + _w3_specs(FS, D),
        out_specs=[tok, sh, sh],
        out_shape=[jax.ShapeDtypeStruct((T, D), F32),
                   jax.ShapeDtypeStruct((J, T, FS), BF16),
                   jax.ShapeDtypeStruct((J, T, FS), BF16)],
        args=(df, b_s, si_s, t_s, w3, w3, w3))


def rms_bwd(x, dout, dn, gain, tm, name, plan=None):
    T, D = x.shape

    def body(x_ref, do_ref, dn_ref, g_ref, dx_ref, dgain_ref):
        @pl.when(pl.program_id(0) == 0)
        def _():
            dgain_ref[...] = jnp.zeros_like(dgain_ref)

        r, xhat = _rms_parts(x_ref[...])
        dn = dn_ref[...]
        dgain_ref[...] += jnp.sum(dn * xhat, axis=0, keepdims=True)
        dx_ref[...] = _rms_bwd(dn, xhat, r, g_ref[...], do_ref[...])

    tok = pl.BlockSpec((tm, D), lambda i: (i, 0))
    vec = pl.BlockSpec((1, D), lambda i: (0, 0))
    return _call(
        body, plan, name=name, grid=(T // tm,), in_specs=[tok, tok, tok, vec], out_specs=[tok, vec],
        out_shape=[jax.ShapeDtypeStruct((T, D), F32), jax.ShapeDtypeStruct((1, D), F32)],
        args=(x, dout, dn, gain))


def ffn_wgrad(da, db, si, b, n, df, tk, name, plan=None):
    J, T, FS = da.shape
    D = n.shape[1]

    def body(da_ref, db_ref, si_ref, b_ref, n_ref, df_ref, o_ref):
        @pl.when(pl.program_id(1) == 0)
        def _():
            o_ref[...] = jnp.zeros_like(o_ref)

        nv = n_ref[...]
        o_ref[0] += _dot(da_ref[...], nv, TN)
        o_ref[1] += _dot(db_ref[...], nv, TN)
        o_ref[2] += _dot(si_ref[...] * b_ref[...], df_ref[...], TN)

    sh = pl.BlockSpec((None, tk, FS), lambda j, k: (j, k, 0))
    tok = pl.BlockSpec((tk, D), lambda j, k: (k, 0))
    return _call(
        body, plan, name=name, grid=(J, T // tk),
        in_specs=[sh, sh, sh, sh, tok, tok],
        out_specs=[pl.BlockSpec((3, None, FS, D), lambda j, k: (0, j, 0, 0))],
        out_shape=[jax.ShapeDtypeStruct((3, J, FS, D), F32)],
        args=(da, db, si, b, n, df))


def wgrad_one(a, b, tk, name, plan=None, a2=None):
    J, T, FS = a.shape
    D = b.shape[1]
    lhs = (a,) if a2 is None else (a, a2)

    def body(*refs):
        o_ref = refs[-1]

        @pl.when(pl.program_id(1) == 0)
        def _():
            o_ref[...] = jnp.zeros_like(o_ref)

        av = refs[0][...] if a2 is None else refs[0][...] * refs[1][...]
        o_ref[...] += _dot(av, refs[-2][...], TN)

    return _call(
        body, plan, name=name, grid=(J, T // tk),
        in_specs=[pl.BlockSpec((None, tk, FS), lambda j, k: (j, k, 0))] * len(lhs) +
                 [pl.BlockSpec((tk, D), lambda j, k: (k, 0))],
        out_specs=[pl.BlockSpec((None, FS, D), lambda j, k: (j, 0, 0))],
        out_shape=[jax.ShapeDtypeStruct((J, FS, D), F32)],
        args=(*lhs, b))


def tn_shared(a, b, tk, name, plan=None):
    T, M = a.shape
    J, _, N = b.shape

    def body(a_ref, b_ref, o_ref):
        @pl.when(pl.program_id(0) == 0)
        def _():
            o_ref[...] = jnp.zeros_like(o_ref)

        a_t = a_ref[...].astype(BF16).T
        for j in range(J):
            o_ref[j] += _dot(a_t, b_ref[j].astype(BF16))

    return _call(
        body, plan, name=name, grid=(T // tk,),
        in_specs=[pl.BlockSpec((tk, M), lambda k: (k, 0)), pl.BlockSpec((J, tk, N), lambda k: (0, k, 0))],
        out_specs=[pl.BlockSpec((J, M, N), lambda k: (0, 0, 0))],
        out_shape=[jax.ShapeDtypeStruct((J, M, N), F32)],
        args=(a, b))


def in_proj(x, gain, win, tm, name):
    T, D = x.shape
    J, _, W = win.shape

    def body(x_ref, g_ref, w_ref, p_ref):
        _, xhat = _rms_parts(x_ref[...])
        n = (xhat * g_ref[...]).astype(BF16)
        for j in range(J):
            p_ref[j] = _dot(n, w_ref[j])

    return pl.pallas_call(
        body, name=name, grid=(T // tm,),
        in_specs=[pl.BlockSpec((tm, D), lambda i: (i, 0)),
                  pl.BlockSpec((1, D), lambda i: (0, 0)),
                  pl.BlockSpec((J, D, W), lambda i: (0, 0, 0))],
        out_specs=pl.BlockSpec((J, tm, W), lambda i: (0, i, 0)),
        out_shape=jax.ShapeDtypeStruct((J, T, W), F32),
        compiler_params=_params(1),
    )(x, gain, win)


def mixer_constants(S):
    B = RET_BLOCK
    half = RET_DK // 2
    freqs = ROPE_BASE ** (-jnp.arange(half, dtype=F32) * 2.0 / RET_DK)
    ang = jnp.arange(S, dtype=F32)[:, None] * freqs[None, :]
    cos = jnp.tile(jnp.cos(ang), (1, 2 * RET_HEADS))
    sin = jnp.tile(jnp.concatenate([-jnp.sin(ang), jnp.sin(ang)], axis=1), (1, RET_HEADS))
    gamma = 1.0 - 2.0 ** (-5.0 - jnp.arange(RET_HEADS, dtype=F32))
    log_g = jnp.log(gamma)
    idx = jnp.arange(B, dtype=F32)
    ci = jnp.arange(B) // CHUNK
    dist = jnp.abs(idx[:, None] - idx[None, :])
    m_intra = jnp.exp(log_g[:, None, None] * dist[None]) * (ci[None, :] <= ci[:, None])[None].astype(F32)
    lg_lane = jnp.repeat(log_g, RET_DK)
    d_q = jnp.exp(lg_lane[None, :] * (idx[:, None] + 1.0))
    d_k = jnp.exp(lg_lane[None, :] * (B - 1.0 - idx[:, None]))
    d_s = jnp.broadcast_to(jnp.exp(lg_lane * B)[:, None], (QK_W, V_W))
    bm = (jnp.arange(QK_W)[:, None] // RET_DK == jnp.arange(V_W)[None, :] // RET_DV).astype(F32)
    win = jnp.repeat(jnp.array(POOL_WINDOWS, F32), POOL_GC)
    invc = 1.0 / jnp.minimum(jnp.arange(S, dtype=F32)[:, None] + 1.0, win[None, :])
    return dict(cos=cos, sin=sin, m=m_intra, dq=d_q, dk=d_k, ds=d_s, bm=bm, invc=invc)


def _swap_halves(x):
    w = x.shape[1]
    lane = lax.broadcasted_iota(jnp.int32, x.shape, 1)
    return jnp.where((lane % RET_DK) < RET_DK // 2, pltpu.roll(x, w - RET_DK // 2, 1), pltpu.roll(x, RET_DK // 2, 1))


def _rot(x, cos, sin):
    return x * cos + _swap_halves(x) * sin


def _rot_t(d, cos, sin):
    return d * cos + _swap_halves(d * sin)


def _lane_groups(parts):
    return jnp.concatenate([p[:, POOL_GC * g:POOL_GC * (g + 1)] for g, p in enumerate(parts)], axis=1)


def _head_mask(shape, h):
    lane = lax.broadcasted_iota(jnp.int32, shape, 1)
    return (lane // RET_DK) == h


def _group_norm(o):
    yh, rs = [], []
    for h in range(RET_HEADS):
        oh = o[:, RET_DV * h:RET_DV * (h + 1)]
        xc = oh - jnp.mean(oh, axis=-1, keepdims=True)
        r = lax.rsqrt(jnp.mean(xc * xc, axis=-1, keepdims=True) + GN_EPS)
        yh.append(xc * r)
        rs.append(r)
    return yh, rs


def mixer_fwd(p, x1, consts, gn_gain, pool_w, pool_scale, wout, n_seq, S, name):
    _, T, _ = p.shape
    D = x1.shape[1]
    B = RET_BLOCK
    nb = S // B

    def body(p_ref, x1_ref, cos_ref, sin_ref, invc_ref, m_ref, dq_ref, dk_ref, ds_ref, bm_ref,
             gain_ref, pw_ref, sc_ref, wout_ref,
             x2_ref, m_out, o_out, st_out, pooled_out, state, prev_u):
        blk = pl.program_id(1)

        @pl.when(blk == 0)
        def _():
            state[...] = jnp.zeros_like(state)
            prev_u[...] = jnp.zeros_like(prev_u)

        qk = p_ref[0]
        v = p_ref[1]
        g = p_ref[2]
        u = p_ref[3]
        cos = cos_ref[...]
        sin = sin_ref[...]
        qr = _rot(qk[:, :QK_W], cos, sin) * (RET_DK ** -0.5)
        kr = _rot(qk[:, QK_W:], cos, sin)
        qb = qr.astype(BF16)
        kb = kr.astype(BF16)
        vb = v.astype(BF16)
        st = state[...]
        st_out[...] = st
        cross = _dot((qr * dq_ref[...]).astype(BF16), st.astype(BF16))
        o_parts = []
        for h in range(RET_HEADS):
            qm = jnp.where(_head_mask(qb.shape, h), qb, jnp.zeros_like(qb))
            sc = (_dot(qm, kb, NT) * m_ref[h]).astype(BF16)
            o_parts.append(_dot(sc, vb[:, RET_DV * h:RET_DV * (h + 1)]) + cross[:, RET_DV * h:RET_DV * (h + 1)])
        o = jnp.concatenate(o_parts, axis=1)
        o_out[...] = o
        kv = _dot((kr * dk_ref[...]).astype(BF16), vb, TN)
        state[...] = st * ds_ref[...] + kv * bm_ref[...]

        yh, _ = _group_norm(o)
        r = g * _sigmoid(g) * (jnp.concatenate(yh, axis=1) * gain_ref[...])

        ext = jnp.concatenate([prev_u[...], u], axis=0)
        sums = []
        run = ext
        for k in (1, 2, 4, 8):
            run = run + pltpu.roll(run, k, 0)
            sums.append(run[B:, :])
        prev_u[...] = u
        pooled = (_lane_groups(sums) * invc_ref[...] - u).astype(BF16)
        pooled_out[...] = pooled
        yp = [_dot(pooled[:, POOL_GC * gi:POOL_GC * (gi + 1)], pw_ref[gi].astype(BF16)) for gi in range(len(POOL_WINDOWS))]
        s = jnp.concatenate(yp, axis=1) * sc_ref[...]
        m = jnp.concatenate([r, s], axis=1).astype(BF16)
        m_out[...] = m
        x2_ref[...] = x1_ref[...] + _dot(m, wout_ref[...])

    tokmap = lambda b, k: (b * nb + k, 0)
    seqmap = lambda b, k: (k, 0)
    const2 = lambda b, k: (0, 0)
    const3 = lambda b, k: (0, 0, 0)
    return pl.pallas_call(
        body, name=name, grid=(n_seq, nb),
        in_specs=[pl.BlockSpec((4, B, V_W), lambda b, k: (0, b * nb + k, 0)),
                  pl.BlockSpec((B, D), tokmap),
                  pl.BlockSpec((B, QK_W), seqmap), pl.BlockSpec((B, QK_W), seqmap), pl.BlockSpec((B, V_W), seqmap),
                  pl.BlockSpec((RET_HEADS, B, B), const3),
                  pl.BlockSpec((B, QK_W), const2), pl.BlockSpec((B, QK_W), const2),
                  pl.BlockSpec((QK_W, V_W), const2), pl.BlockSpec((QK_W, V_W), const2),
                  pl.BlockSpec((1, V_W), const2), pl.BlockSpec((4, POOL_GC, POOL_GC), const3),
                  pl.BlockSpec((1, V_W), const2), pl.BlockSpec((D, D), const2)],
        out_specs=[pl.BlockSpec((B, D), tokmap), pl.BlockSpec((B, D), tokmap), pl.BlockSpec((B, V_W), tokmap),
                   pl.BlockSpec((None, QK_W, V_W), lambda b, k: (b * nb + k, 0, 0)),
                   pl.BlockSpec((B, V_W), tokmap)],
        out_shape=[jax.ShapeDtypeStruct((T, D), F32), jax.ShapeDtypeStruct((T, D), BF16),
                   jax.ShapeDtypeStruct((T, V_W), F32), jax.ShapeDtypeStruct((T // B, QK_W, V_W), F32),
                   jax.ShapeDtypeStruct((T, V_W), BF16)],
        scratch_shapes=[pltpu.VMEM((QK_W, V_W), F32), pltpu.VMEM((B, V_W), F32)],
        compiler_params=_params(2),
    )(p, x1, consts["cos"], consts["sin"], consts["invc"], consts["m"], consts["dq"], consts["dk"],
      consts["ds"], consts["bm"], gn_gain, pool_w, pool_scale, wout)


def mixer_bwd(dx3, dn3, x2, g3, x1, p, o_s, st_s, pooled_s, consts, gmix, gn_gain, pool_w, pool_scale, wout, win,
              n_seq, S, name, plan=None):
    T, D = x1.shape
    B = RET_BLOCK
    nb = S // B
    NG = len(POOL_WINDOWS)

    def body(dx3_ref, dn3_ref, x2_ref, g3_ref, x1_ref, p_ref, o_ref, st_ref, pooled_ref, cos_ref, sin_ref, invc_ref,
             m_ref, dq_ref, dk_ref, ds_ref, bm_ref, gmix_ref, gain_ref, pw_ref, sc_ref, wout_ref, win_ref,
             dx1_ref, df1_ref, dx2_ref, dp_ref, n2_ref, dg3_ref, dgmix_ref, dgain_ref, dscale_ref, dpw_ref,
             gstate, next_e):
        b = pl.program_id(0)
        blk = pl.program_id(1)

        @pl.when(jnp.logical_and(b == 0, blk == 0))
        def _():
            dg3_ref[...] = jnp.zeros_like(dg3_ref)
            dgmix_ref[...] = jnp.zeros_like(dgmix_ref)
            dgain_ref[...] = jnp.zeros_like(dgain_ref)
            dscale_ref[...] = jnp.zeros_like(dscale_ref)
            dpw_ref[...] = jnp.zeros_like(dpw_ref)

        @pl.when(blk == 0)
        def _():
            gstate[...] = jnp.zeros_like(gstate)
            next_e[...] = jnp.zeros_like(next_e)

        r2, xhat2 = _rms_parts(x2_ref[...])
        dn3 = dn3_ref[...]
        dg3_ref[...] += jnp.sum(dn3 * xhat2, axis=0, keepdims=True)
        dx2v = _rms_bwd(dn3, xhat2, r2, g3_ref[...], dx3_ref[...])
        dx2b = dx2v.astype(BF16)
        dx2_ref[...] = dx2b
        dm = _dot(dx2b, wout_ref[...], NT)
        dr = dm[:, :V_W]
        dsv = dm[:, V_W:]

        pooled = pooled_ref[...]
        scale = sc_ref[...]
        dyp = (dsv * scale).astype(BF16)
        yp, dpl = [], []
        for gi in range(NG):
            sl = slice(POOL_GC * gi, POOL_GC * (gi + 1))
            pwb = pw_ref[gi].astype(BF16)
            yp.append(_dot(pooled[:, sl], pwb))
            dpw_ref[gi] += _dot(pooled[:, sl], dyp[:, sl], TN)
            dpl.append(_dot(dyp[:, sl], pwb, NT))
        dscale_ref[...] += jnp.sum(dsv * jnp.concatenate(yp, axis=1), axis=0, keepdims=True)
        dpooled = jnp.concatenate(dpl, axis=1)
        e = dpooled * invc_ref[...]
        ext = jnp.concatenate([e, next_e[...]], axis=0)
        sums = []
        run = ext
        for k in (1, 2, 4, 8):
            run = run + pltpu.roll(run, 2 * B - k, 0)
            sums.append(run[:B, :])
        next_e[...] = e
        du = _lane_groups(sums) - dpooled

        qk = p_ref[0]
        v = p_ref[1]
        g = p_ref[2]
        sg = _sigmoid(g)
        si = g * sg
        yh, rs = _group_norm(o_ref[...])
        yhat = jnp.concatenate(yh, axis=1)
        gain = gain_ref[...]
        dg = dr * (yhat * gain) * (sg * (1.0 + g * (1.0 - sg)))
        dy = dr * si
        dgain_ref[...] += jnp.sum(dy * yhat, axis=0, keepdims=True)
        dyh = dy * gain
        do_parts = []
        for h in range(RET_HEADS):
            sl = slice(RET_DV * h, RET_DV * (h + 1))
            dh_ = dyh[:, sl]
            m1 = jnp.mean(dh_, axis=-1, keepdims=True)
            m2 = jnp.mean(dh_ * yh[h], axis=-1, keepdims=True)
            do_parts.append(rs[h] * (dh_ - m1 - yh[h] * m2))
        dob = jnp.concatenate(do_parts, axis=1).astype(BF16)

        cos = cos_ref[...]
        sin = sin_ref[...]
        qr = _rot(qk[:, :QK_W], cos, sin) * (RET_DK ** -0.5)
        kr = _rot(qk[:, QK_W:], cos, sin)
        qb = qr.astype(BF16)
        kb = kr.astype(BF16)
        vb = v.astype(BF16)
        dqd = dq_ref[...]
        dkd = dk_ref[...]
        stb = st_ref[...].astype(BF16)
        gs = gstate[...]
        gb = gs.astype(BF16)
        dqs = _dot(dob, stb, NT) * dqd
        dkr = _dot(vb, gb, NT) * dkd
        dv_cross = _dot((kr * dkd).astype(BF16), gb)
        ds_cross = _dot((qr * dqd).astype(BF16), dob, TN) * bm_ref[...]
        gstate[...] = ds_cross + gs * ds_ref[...]
        dv_parts = []
        for h in range(RET_HEADS):
            sl = slice(RET_DV * h, RET_DV * (h + 1))
            hm = _head_mask(qb.shape, h)
            qm = jnp.where(hm, qb, jnp.zeros_like(qb))
            mh = m_ref[h]
            sc = (_dot(qm, kb, NT) * mh).astype(BF16)
            dsc = (_dot(dob[:, sl], vb[:, sl], NT) * mh).astype(BF16)
            dqs = dqs + jnp.where(hm, _dot(dsc, kb), 0.0)
            dkr = dkr + jnp.where(hm, _dot(dsc, qb, TN), 0.0)
            dv_parts.append(_dot(sc, dob[:, sl], TN) + dv_cross[:, sl])
        dq = _rot_t(dqs * (RET_DK ** -0.5), cos, sin)
        dk = _rot_t(dkr, cos, sin)
        dp = [jnp.concatenate([dq, dk], axis=1).astype(BF16), jnp.concatenate(dv_parts, axis=1).astype(BF16),
              dg.astype(BF16), du.astype(BF16)]
        dn = jnp.zeros((B, D), F32)
        for jj in range(4):
            dp_ref[jj] = dp[jj]
            dn = dn + _dot(dp[jj], win_ref[jj], NT)

        x1v = x1_ref[...]
        r, xhat = _rms_parts(x1v)
        gm = gmix_ref[...]
        n2_ref[...] = (xhat * gm).astype(BF16)
        dgmix_ref[...] += jnp.sum(dn * xhat, axis=0, keepdims=True)
        dx1 = _rms_bwd(dn, xhat, r, gm, dx2v)
        dx1_ref[...] = dx1
        df1_ref[...] = (0.5 * dx1).astype(BF16)

    rev = lambda b, k: (b * nb + (nb - 1 - k), 0)
    seqrev = lambda b, k: (nb - 1 - k, 0)
    const2 = lambda b, k: (0, 0)
    const3 = lambda b, k: (0, 0, 0)
    return _call(
        body, plan, name=name, grid=(n_seq, nb),
        in_specs=[pl.BlockSpec((B, D), rev), pl.BlockSpec((B, D), rev), pl.BlockSpec((B, D), rev),
                  pl.BlockSpec((1, D), const2), pl.BlockSpec((B, D), rev),
                  pl.BlockSpec((4, B, V_W), lambda b, k: (0, b * nb + (nb - 1 - k), 0)),
                  pl.BlockSpec((B, V_W), rev),
                  pl.BlockSpec((None, QK_W, V_W), lambda b, k: (b * nb + (nb - 1 - k), 0, 0)),
                  pl.BlockSpec((B, V_W), rev),
                  pl.BlockSpec((B, QK_W), seqrev), pl.BlockSpec((B, QK_W), seqrev), pl.BlockSpec((B, V_W), seqrev),
                  pl.BlockSpec((RET_HEADS, B, B), const3),
                  pl.BlockSpec((B, QK_W), const2), pl.BlockSpec((B, QK_W), const2),
                  pl.BlockSpec((QK_W, V_W), const2), pl.BlockSpec((QK_W, V_W), const2),
                  pl.BlockSpec((1, D), const2), pl.BlockSpec((1, V_W), const2),
                  pl.BlockSpec((NG, POOL_GC, POOL_GC), const3), pl.BlockSpec((1, V_W), const2),
                  pl.BlockSpec((D, D), const2), pl.BlockSpec((4, D, V_W), const3)],
        out_specs=[pl.BlockSpec((B, D), rev), pl.BlockSpec((B, D), rev), pl.BlockSpec((B, D), rev),
                   pl.BlockSpec((4, B, V_W), lambda b, k: (0, b * nb + (nb - 1 - k), 0)),
                   pl.BlockSpec((B, D), rev),
                   pl.BlockSpec((1, D), const2), pl.BlockSpec((1, D), const2), pl.BlockSpec((1, V_W), const2),
                   pl.BlockSpec((1, V_W), const2), pl.BlockSpec((NG, POOL_GC, POOL_GC), const3)],
        out_shape=[jax.ShapeDtypeStruct((T, D), F32), jax.ShapeDtypeStruct((T, D), BF16),
                   jax.ShapeDtypeStruct((T, D), BF16), jax.ShapeDtypeStruct((4, T, V_W), BF16),
                   jax.ShapeDtypeStruct((T, D), BF16),
                   jax.ShapeDtypeStruct((1, D), F32), jax.ShapeDtypeStruct((1, D), F32), jax.ShapeDtypeStruct((1, V_W), F32),
                   jax.ShapeDtypeStruct((1, V_W), F32), jax.ShapeDtypeStruct((NG, POOL_GC, POOL_GC), F32)],
        scratch_shapes=[pltpu.VMEM((QK_W, V_W), F32), pltpu.VMEM((B, V_W), F32)],
        args=(dx3, dn3, x2, g3, x1, p, o_s, st_s, pooled_s, consts["cos"], consts["sin"], consts["invc"], consts["m"], consts["dq"],
              consts["dk"], consts["ds"], consts["bm"], gmix, gn_gain, pool_w, pool_scale, wout, win))


def loss_bwd(x3, tgt, gain, tm, name):
    T, D = x3.shape

    def body(x_ref, t_ref, g_ref, dx_ref, df_ref, lacc_ref, dg_ref):
        @pl.when(pl.program_id(0) == 0)
        def _():
            lacc_ref[...] = jnp.zeros_like(lacc_ref)
            dg_ref[...] = jnp.zeros_like(dg_ref)

        r, xhat = _rms_parts(x_ref[...])
        gv = g_ref[...]
        err = xhat * gv - t_ref[...]
        lacc_ref[...] += jnp.sum(err * err, axis=0, keepdims=True)
        dy = err * (1.0 / D)
        dg_ref[...] += jnp.sum(dy * xhat, axis=0, keepdims=True)
        dx = _rms_bwd(dy, xhat, r, gv, 0.0)
        dx_ref[...] = dx
        df_ref[...] = (0.5 * dx).astype(BF16)

    tok = pl.BlockSpec((tm, D), lambda i: (i, 0))
    vec = pl.BlockSpec((1, D), lambda i: (0, 0))
    return pl.pallas_call(
        body, name=name, grid=(T // tm,),
        in_specs=[tok, tok, vec], out_specs=[tok, tok, vec, vec],
        out_shape=[jax.ShapeDtypeStruct((T, D), F32), jax.ShapeDtypeStruct((T, D), BF16),
                   jax.ShapeDtypeStruct((1, D), F32), jax.ShapeDtypeStruct((1, D), F32)],
        compiler_params=_params(1),
    )(x3, tgt, gain)


def _adamw_math(w, g, m, v):
    m2 = ADAM_B1 * m + (1.0 - ADAM_B1) * g
    v2 = ADAM_B2 * v + (1.0 - ADAM_B2) * (g * g)
    m_hat = m2 / (1.0 - ADAM_B1 ** ADAM_STEP)
    v_hat = v2 / (1.0 - ADAM_B2 ** ADAM_STEP)
    return -ADAM_LR * (m_hat / (jnp.sqrt(v_hat) + ADAM_EPS) + ADAM_WD * w), m2, v2


def _place():
    x, y, c = lax.axis_index("x"), lax.axis_index("y"), lax.axis_index("c")
    other_chips = [(1 - x, y), (x, 1 - y), (1 - x, 1 - y)]
    return x, y, c, other_chips


def _exchange_plan(ins, out_shape, copies, n_copies):
    def descriptors(pins, pouts, psems):
        send, recv = psems
        return [pltpu.make_async_remote_copy(src_ref=s, dst_ref=d, send_sem=send.at[i], recv_sem=recv.at[i],
                                             device_id=dev, device_id_type=MESH)
                for i, (s, d, dev) in enumerate(copies(pins, pouts))]

    def start(pins, pouts, psems):
        for cp in descriptors(pins, pouts, psems):
            cp.start()

    def finish(pins, pouts, psems):
        for cp in descriptors(pins, pouts, psems):
            cp.wait()

    return CommPlan(tuple(ins), tuple(out_shape),
                    (pltpu.SemaphoreType.DMA((n_copies,)), pltpu.SemaphoreType.DMA((n_copies,))), start, finish)


def _combine(a, b):
    assert a.mid is None and b.mid is None
    ni, no, ns = len(a.ins), len(a.out_shape), len(a.sems)

    def start(pins, pouts, psems):
        a.start(pins[:ni], pouts[:no], psems[:ns])
        b.start(pins[ni:], pouts[no:], psems[ns:])

    def finish(pins, pouts, psems):
        a.finish(pins[:ni], pouts[:no], psems[:ns])
        b.finish(pins[ni:], pouts[no:], psems[ns:])

    return CommPlan(a.ins + b.ins, a.out_shape + b.out_shape, a.sems + b.sems, start, finish)


def gather_plan(shards):
    n = len(shards)

    def start(pins, pouts, psems):
        x, y, c, chips = _place()
        mine = 2 * x + y
        for w in range(n):
            for k, (px, py) in enumerate(chips):
                pltpu.make_async_remote_copy(
                    src_ref=pins[w].at[:, c], dst_ref=pouts[w].at[:, mine, c],
                    send_sem=psems[0].at[w, k], recv_sem=psems[1].at[w, k],
                    device_id=(px, py, c), device_id_type=MESH).start()

    def mid(pins, pouts, psems):
        x, y, c, chips = _place()
        for w in range(n):
            for k, (px, py) in enumerate(chips):
                landed = pouts[w].at[:, 2 * px + py, c]
                pltpu.make_async_remote_copy(
                    src_ref=landed, dst_ref=landed, send_sem=psems[0].at[w, k], recv_sem=psems[1].at[w, k],
                    device_id=(px, py, c), device_id_type=MESH).wait_recv()
                pltpu.make_async_remote_copy(
                    src_ref=landed, dst_ref=landed, send_sem=psems[2].at[w, k], recv_sem=psems[3].at[w, k],
                    device_id=(x, y, 1 - c), device_id_type=MESH).start()

    def finish(pins, pouts, psems):
        x, y, c, chips = _place()
        mine = 2 * x + y
        for w in range(n):
            for k, (px, py) in enumerate(chips):
                landed = pouts[w].at[:, 2 * px + py, 1 - c]
                cp = pltpu.make_async_remote_copy(
                    src_ref=landed, dst_ref=landed, send_sem=psems[2].at[w, k], recv_sem=psems[3].at[w, k],
                    device_id=(x, y, 1 - c), device_id_type=MESH)
                cp.wait_recv()
                cp.wait_send()
                pltpu.make_async_remote_copy(
                    src_ref=pins[w].at[:, c], dst_ref=pouts[w].at[:, mine, c],
                    send_sem=psems[0].at[w, k], recv_sem=psems[1].at[w, k],
                    device_id=(px, py, c), device_id_type=MESH).wait_send()

    return CommPlan(tuple(shards),
                    tuple(jax.ShapeDtypeStruct((s.shape[0], N_CHIPS) + s.shape[1:], s.dtype) for s in shards),
                    tuple(pltpu.SemaphoreType.DMA((n, 3)) for _ in range(4)), start, finish, mid)


def place_own(gathered, shard, name):
    L, _, Rh, C = shard.shape

    def body(chip_ref, g_ref, s_ref, o_ref):
        o_ref[...] = s_ref[...]

    return pl.pallas_call(
        body, name=name,
        grid_spec=pltpu.PrefetchScalarGridSpec(
            num_scalar_prefetch=1, grid=(L,),
            in_specs=[ANY, pl.BlockSpec((None, 2, Rh, C), lambda l, chip_ref: (l, 0, 0, 0))],
            out_specs=pl.BlockSpec((None, None, 2, Rh, C), lambda l, chip_ref: (l, chip_ref[0], 0, 0, 0))),
        out_shape=jax.ShapeDtypeStruct(gathered.shape, gathered.dtype),
        input_output_aliases={1: 0},
        compiler_params=_params(1),
    )(_chip_index(), gathered, shard)


def gather_weights(shards):
    n = len(shards)
    units = [(w, l) for w in range(n) for l in range(shards[w].shape[0])]
    nu = len(units)

    def body(*refs):
        ins, outs, bufs = refs[:n], refs[n:2 * n], refs[2 * n:3 * n]
        ld_sem, st_sem, own_send, own_recv, fwd_send, fwd_recv, d2d_send, d2d_recv = refs[3 * n:]
        x, y, c, _ = _place()
        sibling = (x, y, 1 - c)
        mine = 2 * x + y
        first = (x ^ (1 - c), y ^ c)
        second = (x ^ c, y ^ (1 - c))
        diagonal = (1 - x, 1 - y)

        def slot(chip):
            return 2 * chip[0] + chip[1]

        def remote(src, dst, send, recv, device):
            return pltpu.make_async_remote_copy(src_ref=src, dst_ref=dst, send_sem=send, recv_sem=recv,
                                                device_id=device, device_id_type=MESH)

        loads = [pltpu.make_async_copy(ins[w], bufs[w], ld_sem.at[w]) for w in range(n)]
        for cp in loads:
            cp.start()
        stores, sends = [], []
        for w in range(n):
            loads[w].wait()
            cp = pltpu.make_async_copy(bufs[w], outs[w].at[:, mine], st_sem.at[w])
            cp.start()
            stores.append(cp)
        for u, (w, l) in enumerate(units):
            for k, chip in enumerate((first, second)):
                cp = remote(bufs[w].at[l, c], outs[w].at[l, mine, c], own_send.at[u, k], own_recv.at[u, k], (*chip, c))
                cp.start()
                sends.append(cp)
        for u, (w, l) in enumerate(units):
            got = outs[w].at[l, slot(first), c]
            remote(got, got, own_send.at[u, 0], own_recv.at[u, 0], (*first, c)).wait_recv()
            for cp in (remote(got, got, fwd_send.at[u], fwd_recv.at[u], (*second, c)),
                       remote(got, got, d2d_send.at[u, 0], d2d_recv.at[u, 0], sibling)):
                cp.start()
                sends.append(cp)
        for u, (w, l) in enumerate(units):
            got = outs[w].at[l, slot(second), c]
            remote(got, got, own_send.at[u, 1], own_recv.at[u, 1], (*second, c)).wait_recv()
            cp = remote(got, got, d2d_send.at[u, 1], d2d_recv.at[u, 1], sibling)
            cp.start()
            sends.append(cp)
        for u, (w, l) in enumerate(units):
            got = outs[w].at[l, slot(diagonal), c]
            remote(got, got, fwd_send.at[u], fwd_recv.at[u], (*second, c)).wait_recv()
            cp = remote(got, got, d2d_send.at[u, 2], d2d_recv.at[u, 2], sibling)
            cp.start()
            sends.append(cp)
        for u, (w, l) in enumerate(units):
            for k, chip in enumerate((second, first, diagonal)):
                got = outs[w].at[l, slot(chip), 1 - c]
                remote(got, got, d2d_send.at[u, k], d2d_recv.at[u, k], sibling).wait_recv()
        for cp in sends:
            cp.wait_send()
        for cp in stores:
            cp.wait()

    return pl.pallas_call(
        body, name="gather_weights",
        in_specs=[ANY] * n, out_specs=[ANY] * n,
        out_shape=[jax.ShapeDtypeStruct((s.shape[0], N_CHIPS) + s.shape[1:], s.dtype) for s in shards],
        scratch_shapes=[pltpu.VMEM(s.shape, s.dtype) for s in shards] +
                       [pltpu.SemaphoreType.DMA((n,)), pltpu.SemaphoreType.DMA((n,)),
                        pltpu.SemaphoreType.DMA((nu, 2)), pltpu.SemaphoreType.DMA((nu, 2)),
                        pltpu.SemaphoreType.DMA((nu,)), pltpu.SemaphoreType.DMA((nu,)),
                        pltpu.SemaphoreType.DMA((nu, 3)), pltpu.SemaphoreType.DMA((nu, 3))],
        compiler_params=pltpu.CompilerParams(vmem_limit_bytes=VMEM_LIMIT),
    )(*shards)


def send_to_sibling_other_half(grads):
    def copies(ins, outs):
        x, y, c, _ = _place()
        return [(ins[w].at[:, :, 1 - c], outs[w], (x, y, 1 - c)) for w in range(len(grads))]

    return _exchange_plan(grads, [jax.ShapeDtypeStruct(g.shape[:2] + g.shape[3:], g.dtype) for g in grads], copies,
                          len(grads))


def _core_index():
    return jnp.reshape(lax.axis_index("c"), (1,)).astype(jnp.int32)


def _chip_index():
    return jnp.reshape(2 * lax.axis_index("x") + lax.axis_index("y"), (1,)).astype(jnp.int32)


def add_own_half(g, recv, name):
    L, J, _, Rh, C = g.shape

    def body(c_ref, g_ref, r_ref, o_ref):
        o_ref[...] = (g_ref[...] + r_ref[...]).astype(BF16)

    return pl.pallas_call(
        body, name=name,
        grid_spec=pltpu.PrefetchScalarGridSpec(
            num_scalar_prefetch=1, grid=(L, J),
            in_specs=[pl.BlockSpec((None, None, None, Rh, C), lambda l, j, c_ref: (l, j, c_ref[0], 0, 0)),
                      pl.BlockSpec((None, None, Rh, C), lambda l, j, c_ref: (l, j, 0, 0))],
            out_specs=pl.BlockSpec((None, None, Rh, C), lambda l, j, c_ref: (l, j, 0, 0))),
        out_shape=jax.ShapeDtypeStruct((L, J, Rh, C), BF16),
        compiler_params=_params(2),
    )(_core_index(), g, recv)


def exchange_between_chips(sums):
    def copies(ins, outs):
        x, y, c, chips = _place()
        return [(ins[w].at[:, 2 * px + py], outs[w].at[k], (px, py, c))
                for w in range(len(sums)) for k, (px, py) in enumerate(chips)]

    return _exchange_plan(sums, [jax.ShapeDtypeStruct((3, s.shape[0]) + s.shape[2:], s.dtype) for s in sums], copies,
                          3 * len(sums))


def sum_chips(own, recv, name):
    L, _, Rh, C = own.shape

    def body(chip_ref, o_ref, r_ref, out_ref):
        acc = o_ref[...].astype(F32)
        for k in range(3):
            acc = acc + r_ref[k].astype(F32)
        out_ref[...] = acc

    return pl.pallas_call(
        body, name=name,
        grid_spec=pltpu.PrefetchScalarGridSpec(
            num_scalar_prefetch=1, grid=(L,),
            in_specs=[pl.BlockSpec((None, None, Rh, C), lambda l, chip_ref: (l, chip_ref[0], 0, 0)),
                      pl.BlockSpec((3, None, Rh, C), lambda l, chip_ref: (0, l, 0, 0))],
            out_specs=pl.BlockSpec((None, Rh, C), lambda l, chip_ref: (l, 0, 0))),
        out_shape=jax.ShapeDtypeStruct((L, Rh, C), F32),
        compiler_params=_params(1),
    )(_chip_index(), own, recv)


def share_with_sibling(halves):
    def copies(ins, outs):
        x, y, c, _ = _place()
        return [(ins[w], outs[w], (x, y, 1 - c)) for w in range(len(halves))]

    return _exchange_plan(halves, [jax.ShapeDtypeStruct(h.shape, h.dtype) for h in halves], copies, len(halves))


def adamw_group(ws, ms, vs, own, sib, name, plan=None):
    L = len(ws)
    R, C = ws[0].shape
    Rh = R // 2
    tr = Rh // 2
    nr = Rh // tr

    def body(*refs):
        w, m, v = refs[:L], refs[L:2 * L], refs[2 * L:3 * L]
        own_ref, sib_ref = refs[3 * L], refs[3 * L + 1]
        outs = refs[3 * L + 2:]
        mine = pl.program_id(0) == lax.axis_index("c")
        for l in range(L):
            gv = jnp.where(mine, own_ref[l], sib_ref[l])
            d, m2, v2 = _adamw_math(w[l][...], gv, m[l][...], v[l][...])
            outs[4 * l][...] = d
            outs[4 * l + 1][...] = m2
            outs[4 * l + 2][...] = v2
            outs[4 * l + 3][...] = gv

    blk = pl.BlockSpec((tr, C), lambda h, r: (h * nr + r, 0))
    half = pl.BlockSpec((L, tr, C), lambda h, r: (0, r, 0))
    sds = jax.ShapeDtypeStruct((R, C), F32)
    return _call(body, plan, name=name, grid=(2, nr), in_specs=[blk] * (3 * L) + [half, half],
                 out_specs=[blk] * (4 * L), out_shape=[sds] * (4 * L), args=(*ws, *ms, *vs, own, sib))


def gather_small_plan(part):
    def copies(ins, outs):
        x, y, c, _ = _place()
        me = 4 * x + 2 * y + c
        return [(ins[0], outs[0].at[me], (x ^ ((k >> 2) & 1), y ^ ((k >> 1) & 1), c ^ (k & 1))) for k in range(1, N_DEV)]

    return _exchange_plan([part], [jax.ShapeDtypeStruct((N_DEV,) + part.shape, part.dtype)], copies, N_DEV - 1)


def sum_small(part, gathered, name):
    R, C = part.shape

    def body(p_ref, g_ref, o_ref):
        d = pl.program_id(0)
        me = 4 * lax.axis_index("x") + 2 * lax.axis_index("y") + lax.axis_index("c")
        val = jnp.where(d == me, p_ref[...], g_ref[...])

        @pl.when(d == 0)
        def _():
            o_ref[...] = val

        @pl.when(d > 0)
        def _():
            o_ref[...] += val

    return pl.pallas_call(
        body, name=name, grid=(N_DEV,),
        in_specs=[pl.BlockSpec((R, C), lambda d: (0, 0)), pl.BlockSpec((None, R, C), lambda d: (d, 0, 0))],
        out_specs=pl.BlockSpec((R, C), lambda d: (0, 0)),
        out_shape=jax.ShapeDtypeStruct((R, C), F32),
        compiler_params=_params(1),
    )(part, gathered)


SMALL = ("norm_ffn1", "norm_mix", "norm_ffn2", "norm_final", "ret_gn_gain", "pool_scale", "pool_w")
GROUPS = (("ffn1", ("ffn1_gate", "ffn1_up", "ffn1_down")), ("w_in", ("w_in",)), ("w_out", ("w_out",)),
          ("ffn2", ("ffn2_gate", "ffn2_up", "ffn2_down")))
TRANSPOSED = ("ffn1_gate", "ffn1_up", "ffn2_gate", "ffn2_up")
IN_W = 2 * QK_W + 3 * V_W
ORDER = ("norm_ffn1", "ffn1_gate", "ffn1_up", "ffn1_down", "norm_mix", "w_in", "ret_gn_gain", "pool_w", "pool_scale",
         "w_out", "norm_ffn2", "ffn2_gate", "ffn2_up", "ffn2_down", "norm_final")


SUBLANES = 8


def _small_rows(shapes):
    rows = [math.prod(shapes[k]) // 128 for k in SMALL]
    offs, off = [], 0
    for r in rows:
        offs.append(off)
        off += -(-r // SUBLANES) * SUBLANES
    return rows, offs, off


def _pack_small(d, shapes):
    rows, offs, _ = _small_rows(shapes)
    pieces = []
    for k, r in zip(SMALL[1:], rows[1:]):
        pieces.append(d[k].reshape(r, 128))
        if r % SUBLANES:
            pieces.append(jnp.zeros((SUBLANES - r % SUBLANES, 128), F32))
    return jnp.concatenate(pieces, axis=0)


def adamw_small(W, M, V, first, summed, shapes):
    rows, offs, _ = _small_rows(shapes)
    n = len(SMALL)

    def body(*refs):
        w, m, v, f_ref, s_ref = refs[:n], refs[n:2 * n], refs[2 * n:3 * n], refs[3 * n], refs[3 * n + 1]
        outs = refs[3 * n + 2:]
        for i, (r, off) in enumerate(zip(rows, offs)):
            gv = f_ref[...] if i == 0 else s_ref[off - offs[1]:off - offs[1] + r, :]
            d, m2, v2 = _adamw_math(w[i][...], gv, m[i][...], v[i][...])
            outs[4 * i][...] = d
            outs[4 * i + 1][...] = m2
            outs[4 * i + 2][...] = v2
            outs[4 * i + 3][...] = gv

    vm = pl.BlockSpec(memory_space=pltpu.VMEM)
    flat = lambda d: [d[k].reshape(r, 128) for k, r in zip(SMALL, rows)]
    res = pl.pallas_call(
        body, name="adamw_small", in_specs=[vm] * (3 * n + 2), out_specs=[vm] * (4 * n),
        out_shape=[jax.ShapeDtypeStruct((r, 128), F32) for r in rows for _ in range(4)],
    )(*flat(W), *flat(M), *flat(V), first, summed)
    return {k: [a.reshape(shapes[k]) for a in res[4 * i:4 * i + 4]] for i, k in enumerate(SMALL)}


def kernel(x, norm_ffn1, ffn1_gate, ffn1_up, ffn1_down, norm_mix, w_in, ret_gn_gain, pool_w, pool_scale, w_out, norm_ffn2, ffn2_gate, ffn2_up, ffn2_down, norm_final, loss_target, m_norm_ffn1, m_ffn1_gate, m_ffn1_up, m_ffn1_down, m_norm_mix, m_w_in, m_ret_gn_gain, m_pool_w, m_pool_scale, m_w_out, m_norm_ffn2, m_ffn2_gate, m_ffn2_up, m_ffn2_down, m_norm_final, v_norm_ffn1, v_ffn1_gate, v_ffn1_up, v_ffn1_down, v_norm_mix, v_w_in, v_ret_gn_gain, v_pool_w, v_pool_scale, v_w_out, v_norm_ffn2, v_ffn2_gate, v_ffn2_up, v_ffn2_down, v_norm_final):
    W = dict(norm_ffn1=norm_ffn1, ffn1_gate=ffn1_gate, ffn1_up=ffn1_up, ffn1_down=ffn1_down, norm_mix=norm_mix, w_in=w_in,
             ret_gn_gain=ret_gn_gain, pool_w=pool_w, pool_scale=pool_scale, w_out=w_out, norm_ffn2=norm_ffn2,
             ffn2_gate=ffn2_gate, ffn2_up=ffn2_up, ffn2_down=ffn2_down, norm_final=norm_final)
    M = dict(norm_ffn1=m_norm_ffn1, ffn1_gate=m_ffn1_gate, ffn1_up=m_ffn1_up, ffn1_down=m_ffn1_down, norm_mix=m_norm_mix,
             w_in=m_w_in, ret_gn_gain=m_ret_gn_gain, pool_w=m_pool_w, pool_scale=m_pool_scale, w_out=m_w_out,
             norm_ffn2=m_norm_ffn2, ffn2_gate=m_ffn2_gate, ffn2_up=m_ffn2_up, ffn2_down=m_ffn2_down, norm_final=m_norm_final)
    V = dict(norm_ffn1=v_norm_ffn1, ffn1_gate=v_ffn1_gate, ffn1_up=v_ffn1_up, ffn1_down=v_ffn1_down, norm_mix=v_norm_mix,
             w_in=v_w_in, ret_gn_gain=v_ret_gn_gain, pool_w=v_pool_w, pool_scale=v_pool_scale, w_out=v_w_out,
             norm_ffn2=v_norm_ffn2, ffn2_gate=v_ffn2_gate, ffn2_up=v_ffn2_up, ffn2_down=v_ffn2_down, norm_final=v_norm_final)

    n_seq, S, D = x.shape
    T = n_seq * S
    tm = min(1024, T // 2)
    tk = min(1024, T // 2)
    tk1 = min(2048, T // 2)
    xf = x.reshape(T, D)
    tgt = loss_target.reshape(T, D)

    def local(d, k):
        a = d[k][0]
        return a.T if k in TRANSPOSED else a

    def to_out(k, a):
        return (a.T if k in TRANSPOSED else a)[None]

    loc_bf = []
    for _, members in GROUPS:
        st = jnp.stack([local(W, k).astype(BF16) for k in members])
        loc_bf.append(st.reshape(st.shape[0], 2, st.shape[1] // 2, st.shape[2]))
    FS = loc_bf[0].shape[2] * 2
    w3_1 = gather_weights(loc_bf[:1])[0].reshape(3, N_CHIPS, FS, D)
    g1 = norm_ffn1.reshape(1, D)
    gm = norm_mix.reshape(1, D)
    g3 = norm_ffn2.reshape(1, D)
    gf = norm_final.reshape(1, D)
    gn_gain = ret_gn_gain.reshape(1, V_W)
    pscale = pool_scale.reshape(1, V_W)
    pw = pool_w.reshape(len(POOL_WINDOWS), POOL_GC, POOL_GC)
    consts = mixer_constants(S)

    def halves5(g, lb):
        return g.reshape(lb.shape[0], N_CHIPS, 2, lb.shape[2], lb.shape[3])

    (x1, b1, si1, t1, n1), landed = ffn_fwd(xf, g1, w3_1, tm, "ffn1_fwd", plan=gather_plan(loc_bf[1:]))
    gathered = [place_own(g, lb, "place_own_" + gn) for g, lb, (gn, _) in zip(landed, loc_bf[1:], GROUPS[1:])]
    win_full = gathered[0].reshape(N_CHIPS, D, IN_W // N_CHIPS)
    wout_full = gathered[1].reshape(D, D)
    w3_2 = gathered[2].reshape(3, N_CHIPS, FS, D)
    p = in_proj(x1, gm, win_full, tm, "in_proj")
    x2, m_s, o_s, st_s, pooled_s = mixer_fwd(p, x1, consts, gn_gain, pw, pscale, wout_full, n_seq, S, "mixer_fwd")
    x3, b2, si2, t2, n3 = ffn_fwd(x2, g3, w3_2, tm, "ffn2_fwd")

    dx3, df2, lacc, dgf = loss_bwd(x3, tgt, gf, tm, "loss_bwd")
    dn3, da2, db2 = ffn_bwd(df2, b2, si2, t2, w3_2, tm, "ffn2_bwd")
    to_sib, to_chips, back = send_to_sibling_other_half, exchange_between_chips, share_with_sibling
    out_g, out_d, out_m, out_v = {}, {}, {}, {}

    def update(members, own, sib, name):
        res = adamw_group([local(W, k) for k in members], [local(M, k) for k in members], [local(V, k) for k in members],
                          own, sib, "adamw_" + name)
        for l, k in enumerate(members):
            out_d[k], out_m[k], out_v[k], out_g[k] = (to_out(k, r) for r in res[4 * l:4 * l + 4])

    (gr_ffn2,) = ffn_wgrad(da2, db2, si2, b2, n3, df2, tk, "ffn2_wgrad")
    g_ffn2 = halves5(gr_ffn2, loc_bf[3])
    (dx1, df1, dx2b, dp, n2, dg3, dgm, dgn, dps, dpw), (sb_ffn2,) = mixer_bwd(
        dx3, dn3, x2, g3, x1, p, o_s, st_s, pooled_s, consts, gm, gn_gain, pw, pscale, wout_full, win_full, n_seq, S,
        "mixer_bwd", plan=to_sib([g_ffn2]))
    cs_ffn2 = add_own_half(g_ffn2, sb_ffn2, "add_sibling_ffn2")
    (gr_wout,) = tn_shared(m_s, dx2b[None], tk1, "dw_out")
    (gr_win,) = tn_shared(n2, dp, tk1, "dw_in")
    g_mix = [halves5(gr_win, loc_bf[1]), halves5(gr_wout, loc_bf[2])]
    shapes = {k: W[k].shape for k in SMALL}
    small_part = dict(norm_mix=dgm, norm_ffn2=dg3, norm_final=dgf, ret_gn_gain=dgn, pool_scale=dps, pool_w=dpw)
    part = jnp.concatenate([_pack_small(small_part, shapes), lacc.reshape(-1, 128)], axis=0)
    n_small = part.shape[0] - lacc.size // 128
    (dn1, da1, db1), (pc_ffn2, sb_win, sb_wout, parts) = ffn_bwd(
        df1, b1, si1, t1, w3_1, tm, "ffn1_bwd",
        plan=_combine(_combine(to_chips([cs_ffn2]), to_sib(g_mix)), gather_small_plan(part)))
    summed = sum_small(part, parts, "sum_small")
    loss = jnp.sum(summed[n_small:]) * (0.5 / D)
    mh_ffn2 = sum_chips(cs_ffn2, pc_ffn2, "sum_chips_ffn2")
    cs_mix = [add_own_half(g, r, "add_sibling_" + gn) for g, r, gn in zip(g_mix, (sb_win, sb_wout), ("w_in", "w_out"))]
    def one(g):
        return g.reshape((1, N_CHIPS, 2) + loc_bf[0].shape[2:])

    (gr_g,), (sh_ffn2, pc_win, pc_wout) = wgrad_one(
        da1, n1, tk1, "ffn1_wgrad_gate", plan=_combine(back([mh_ffn2]), to_chips(cs_mix)))
    mh_mix = [sum_chips(cs, pc, "sum_chips_" + gn) for cs, pc, gn in zip(cs_mix, (pc_win, pc_wout), ("w_in", "w_out"))]
    (gr_u,), (sb_g,) = wgrad_one(db1, n1, tk1, "ffn1_wgrad_up", plan=to_sib([one(gr_g)]))
    cs_g = add_own_half(one(gr_g), sb_g, "add_sibling_ffn1_gate")
    (gr_d,), (pc_g, sb_u) = wgrad_one(
        si1, df1, tk1, "ffn1_wgrad_down", plan=_combine(to_chips([cs_g]), to_sib([one(gr_u)])), a2=b1)
    mh_g = sum_chips(cs_g, pc_g, "sum_chips_ffn1_gate")
    cs_u = add_own_half(one(gr_u), sb_u, "add_sibling_ffn1_up")
    (dx0, dg1), (sh_g, sh_win, sh_wout, pc_u, sb_d) = rms_bwd(
        xf, dx1, dn1, g1, tm, "ffn1_rms_bwd",
        plan=_combine(_combine(back([mh_g] + mh_mix), to_chips([cs_u])), to_sib([one(gr_d)])))
    mh_u = sum_chips(cs_u, pc_u, "sum_chips_ffn1_up")
    cs_d = add_own_half(one(gr_d), sb_d, "add_sibling_ffn1_down")
    first = dg1.reshape(-1, 128)
    sh_u, pc_d, firsts = _run_plan(
        _combine(_combine(back([mh_u]), to_chips([cs_d])), gather_small_plan(first)), "ffn1_grads_stage_b")
    mh_d = sum_chips(cs_d, pc_d, "sum_chips_ffn1_down")
    (sh_d,) = _run_plan(back([mh_d]), "ffn1_grads_stage_c")
    first_sum = sum_small(first, firsts, "sum_small_first")
    for k, own, sib in zip(GROUPS[0][1], (mh_g, mh_u, mh_d), (sh_g, sh_u, sh_d)):
        update((k,), own, sib, k)
    update(GROUPS[1][1], mh_mix[0], sh_win, "w_in")
    update(GROUPS[2][1], mh_mix[1], sh_wout, "w_out")
    update(GROUPS[3][1], mh_ffn2, sh_ffn2, "ffn2")

    for k, (d_, m_, v_, g_) in adamw_small(W, M, V, first_sum, summed, shapes).items():
        out_d[k], out_m[k], out_v[k], out_g[k] = d_, m_, v_, g_

    grad_x = dx0.reshape(n_seq, S, D)
    return (loss, grad_x, *[out_g[k] for k in ORDER], *[out_d[k] for k in ORDER],
            *[out_m[k] for k in ORDER], *[out_v[k] for k in ORDER])
```

```python
import functools
import math
from typing import Callable, NamedTuple, Optional

import jax
import jax.numpy as jnp
from jax import lax
from jax.experimental import pallas as pl
from jax.experimental.pallas import tpu as pltpu

F32 = jnp.float32
BF16 = jnp.bfloat16
MESH = pl.DeviceIdType.MESH

N_CHIPS = 4
N_DEV = 8
CHUNK = 64
RET_HEADS = 4
RET_DK = 64
RET_DV = 128
QK_W = RET_HEADS * RET_DK
V_W = RET_HEADS * RET_DV
POOL_WINDOWS = (2, 4, 8, 16)
POOL_GC = 128
ROPE_BASE = 10000.0
RMS_EPS = 1e-6
GN_EPS = 1e-5
ADAM_LR = 0.001
ADAM_B1 = 0.9
ADAM_B2 = 0.999
ADAM_EPS = 1e-08
ADAM_WD = 0.01
ADAM_STEP = 10
MXU_W = 256
RET_BLOCK = MXU_W
VMEM_LIMIT = 56 * 1024 * 1024

NN = (((1,), (0,)), ((), ()))
NT = (((1,), (1,)), ((), ()))
TN = (((0,), (0,)), ((), ()))


def _dot(a, b, dims=NN):
    return lax.dot_general(a, b, dims, preferred_element_type=F32)


def _sigmoid(x):
    return 0.5 * jnp.tanh(0.5 * x) + 0.5


def _params(n_grid):
    return pltpu.CompilerParams(dimension_semantics=("arbitrary",) * n_grid, vmem_limit_bytes=VMEM_LIMIT)


class CommPlan(NamedTuple):
    ins: tuple
    out_shape: tuple
    sems: tuple
    start: Callable
    finish: Callable
    mid: Optional[Callable] = None
    mid_at: float = 0.85


ANY = pl.BlockSpec(memory_space=pl.ANY)


def _call(body, plan, *, name, grid, in_specs, out_specs, out_shape, args, scratch_shapes=()):
    n_grid = len(grid)
    if plan is None:
        return pl.pallas_call(body, name=name, grid=grid, in_specs=in_specs, out_specs=out_specs, out_shape=out_shape,
                              scratch_shapes=scratch_shapes, compiler_params=_params(n_grid))(*args)
    n_in, n_out, n_sc = len(in_specs), len(out_specs), len(scratch_shapes)
    p_in, p_out = len(plan.ins), len(plan.out_shape)
    total = math.prod(grid)
    mid_step = min(total - 1, int(plan.mid_at * total))

    def riding(*refs):
        ins, pins = refs[:n_in], refs[n_in:n_in + p_in]
        o0 = n_in + p_in
        outs, pouts = refs[o0:o0 + n_out], refs[o0 + n_out:o0 + n_out + p_out]
        s0 = o0 + n_out + p_out
        scratch, psems = refs[s0:s0 + n_sc], refs[s0 + n_sc:]
        step = pl.program_id(0)
        for d in range(1, n_grid):
            step = step * grid[d] + pl.program_id(d)

        @pl.when(step == 0)
        def _():
            plan.start(pins, pouts, psems)

        body(*ins, *outs, *scratch)

        if plan.mid is not None:
            @pl.when(step == mid_step)
            def _():
                plan.mid(pins, pouts, psems)

        @pl.when(step == total - 1)
        def _():
            plan.finish(pins, pouts, psems)

    res = pl.pallas_call(
        riding, name=name, grid=grid, in_specs=list(in_specs) + [ANY] * p_in, out_specs=list(out_specs) + [ANY] * p_out,
        out_shape=list(out_shape) + list(plan.out_shape), scratch_shapes=list(scratch_shapes) + list(plan.sems),
        compiler_params=_params(n_grid))(*args, *plan.ins)
    return res[:n_out], res[n_out:]


def _run_plan(plan, name):
    p_in, p_out = len(plan.ins), len(plan.out_shape)

    def body(*refs):
        pins, pouts, psems = refs[:p_in], refs[p_in:p_in + p_out], refs[p_in + p_out:]
        plan.start(pins, pouts, psems)
        if plan.mid is not None:
            plan.mid(pins, pouts, psems)
        plan.finish(pins, pouts, psems)

    return pl.pallas_call(body, name=name, in_specs=[ANY] * p_in, out_specs=[ANY] * p_out,
                          out_shape=list(plan.out_shape), scratch_shapes=list(plan.sems))(*plan.ins)


def _rms_parts(x):
    r = lax.rsqrt(jnp.mean(x * x, axis=-1, keepdims=True) + RMS_EPS)
    return r, x * r


def _rms_bwd(dn, xhat, r, gain, dres):
    dxh = dn * gain
    return dres + r * (dxh - xhat * jnp.mean(dxh * xhat, axis=-1, keepdims=True))


def _w3_specs(FS, D):
    return [pl.BlockSpec((None, None, FS, D), functools.partial(lambda i, j, k: (k, j, 0, 0), k=k)) for k in range(3)]


def ffn_fwd(x, gain, w3, tm, name, plan=None):
    T, D = x.shape
    _, J, FS, _ = w3.shape

    def body(x_ref, g_ref, wg_ref, wu_ref, wd_ref, xo_ref, b_ref, si_ref, t_ref, n_ref, acc):
        j = pl.program_id(1)

        @pl.when(j == 0)
        def _():
            _, xhat = _rms_parts(x_ref[...])
            n_ref[...] = (xhat * g_ref[...]).astype(BF16)
            acc[...] = jnp.zeros_like(acc)

        n = n_ref[...]
        a = _dot(n, wg_ref[...], NT)
        b = _dot(n, wu_ref[...], NT)
        sg = _sigmoid(a)
        si = a * sg
        b_ref[...] = b.astype(BF16)
        si_ref[...] = si.astype(BF16)
        t_ref[...] = (sg + si * (1.0 - sg)).astype(BF16)
        acc[...] += _dot((si * b).astype(BF16), wd_ref[...])

        @pl.when(j == J - 1)
        def _():
            xo_ref[...] = x_ref[...] + 0.5 * acc[...]

    return _call(
        body, plan, name=name, grid=(T // tm, J),
        in_specs=[pl.BlockSpec((tm, D), lambda i, j: (i, 0)),
                  pl.BlockSpec((1, D), lambda i, j: (0, 0))] + _w3_specs(FS, D),
        out_specs=[pl.BlockSpec((tm, D), lambda i, j: (i, 0))] +
                  [pl.BlockSpec((None, tm, FS), lambda i, j: (j, i, 0))] * 3 +
                  [pl.BlockSpec((tm, D), lambda i, j: (i, 0))],
        out_shape=[jax.ShapeDtypeStruct((T, D), F32)] + [jax.ShapeDtypeStruct((J, T, FS), BF16)] * 3 +
                  [jax.ShapeDtypeStruct((T, D), BF16)],
        scratch_shapes=[pltpu.VMEM((tm, D), F32)],
        args=(x, gain, w3, w3, w3))


def ffn_bwd(df, b_s, si_s, t_s, w3, tm, name, plan=None):
    T, D = df.shape
    _, J, FS, _ = w3.shape

    chunks = [slice(s, min(s + MXU_W, FS)) for s in range(0, FS, MXU_W)]
    nc = len(chunks)

    def body(df_ref, b_ref, si_ref, t_ref, wg_ref, wu_ref, wd_ref, dn_ref, da_ref, db_ref):
        @pl.when(pl.program_id(1) == 0)
        def _():
            dn_ref[...] = jnp.zeros_like(dn_ref)

        def elementwise(dh, sl):
            da = (dh * b_ref[:, sl].astype(F32) * t_ref[:, sl].astype(F32)).astype(BF16)
            db = (dh * si_ref[:, sl].astype(F32)).astype(BF16)
            da_ref[:, sl] = da
            db_ref[:, sl] = db
            return da, db

        df = df_ref[...]
        dh, dab, acc = [None] * nc, [None] * nc, None
        for step in range(nc + 2):
            if step < nc:
                dh[step] = _dot(df, wd_ref[chunks[step], :], NT)
            if 1 <= step <= nc:
                dab[step - 1] = elementwise(dh[step - 1], chunks[step - 1])
            if step >= 2:
                sl = chunks[step - 2]
                part = _dot(dab[step - 2][0], wg_ref[sl, :]) + _dot(dab[step - 2][1], wu_ref[sl, :])
                acc = part if acc is None else acc + part
        dn_ref[...] += acc

    tok = pl.BlockSpec((tm, D), lambda i, j: (i, 0))
    sh = pl.BlockSpec((None, tm, FS), lambda i, j: (j, i, 0))
    return _call(
        body, plan, name=name, grid=(T // tm, J),
        in_specs=[tok, sh, sh, sh] + _w3_specs(FS, D),
        out_specs=[tok, sh, sh],
        out_shape=[jax.ShapeDtypeStruct((T, D), F32),
                   jax.ShapeDtypeStruct((J, T, FS), BF16),
                   jax.ShapeDtypeStruct((J, T, FS), BF16)],
        args=(df, b_s, si_s, t_s, w3, w3, w3))


def rms_bwd(x, dout, dn, gain, tm, name, plan=None):
    T, D = x.shape

    def body(x_ref, do_ref, dn_ref, g_ref, dx_ref, dgain_ref):
        @pl.when(pl.program_id(0) == 0)
        def _():
            dgain_ref[...] = jnp.zeros_like(dgain_ref)

        r, xhat = _rms_parts(x_ref[...])
        dn = dn_ref[...]
        dgain_ref[...] += jnp.sum(dn * xhat, axis=0, keepdims=True)
        dx_ref[...] = _rms_bwd(dn, xhat, r, g_ref[...], do_ref[...])

    tok = pl.BlockSpec((tm, D), lambda i: (i, 0))
    vec = pl.BlockSpec((1, D), lambda i: (0, 0))
    return _call(
        body, plan, name=name, grid=(T // tm,), in_specs=[tok, tok, tok, vec], out_specs=[tok, vec],
        out_shape=[jax.ShapeDtypeStruct((T, D), F32), jax.ShapeDtypeStruct((1, D), F32)],
        args=(x, dout, dn, gain))


def ffn_wgrad(da, db, si, b, n, df, tk, name, plan=None):
    J, T, FS = da.shape
    D = n.shape[1]

    def body(da_ref, db_ref, si_ref, b_ref, n_ref, df_ref, o_ref):
        @pl.when(pl.program_id(1) == 0)
        def _():
            o_ref[...] = jnp.zeros_like(o_ref)

        nv = n_ref[...]
        o_ref[0] += _dot(da_ref[...], nv, TN)
        o_ref[1] += _dot(db_ref[...], nv, TN)
        o_ref[2] += _dot(si_ref[...] * b_ref[...], df_ref[...], TN)

    sh = pl.BlockSpec((None, tk, FS), lambda j, k: (j, k, 0))
    tok = pl.BlockSpec((tk, D), lambda j, k: (k, 0))
    return _call(
        body, plan, name=name, grid=(J, T // tk),
        in_specs=[sh, sh, sh, sh, tok, tok],
        out_specs=[pl.BlockSpec((3, None, FS, D), lambda j, k: (0, j, 0, 0))],
        out_shape=[jax.ShapeDtypeStruct((3, J, FS, D), F32)],
        args=(da, db, si, b, n, df))


def wgrad_one(a, b, tk, name, plan=None, a2=None):
    J, T, FS = a.shape
    D = b.shape[1]
    lhs = (a,) if a2 is None else (a, a2)

    def body(*refs):
        o_ref = refs[-1]

        @pl.when(pl.program_id(1) == 0)
        def _():
            o_ref[...] = jnp.zeros_like(o_ref)

        av = refs[0][...] if a2 is None else refs[0][...] * refs[1][...]
        o_ref[...] += _dot(av, refs[-2][...], TN)

    return _call(
        body, plan, name=name, grid=(J, T // tk),
        in_specs=[pl.BlockSpec((None, tk, FS), lambda j, k: (j, k, 0))] * len(lhs) +
                 [pl.BlockSpec((tk, D), lambda j, k: (k, 0))],
        out_specs=[pl.BlockSpec((None, FS, D), lambda j, k: (j, 0, 0))],
        out_shape=[jax.ShapeDtypeStruct((J, FS, D), F32)],
        args=(*lhs, b))


def tn_shared(a, b, tk, name, plan=None):
    T, M = a.shape
    J, _, N = b.shape

    def body(a_ref, b_ref, o_ref):
        @pl.when(pl.program_id(0) == 0)
        def _():
            o_ref[...] = jnp.zeros_like(o_ref)

        a_t = a_ref[...].astype(BF16).T
        for j in range(J):
            o_ref[j] += _dot(a_t, b_ref[j].astype(BF16))

    return _call(
        body, plan, name=name, grid=(T // tk,),
        in_specs=[pl.BlockSpec((tk, M), lambda k: (k, 0)), pl.BlockSpec((J, tk, N), lambda k: (0, k, 0))],
        out_specs=[pl.BlockSpec((J, M, N), lambda k: (0, 0, 0))],
        out_shape=[jax.ShapeDtypeStruct((J, M, N), F32)],
        args=(a, b))


def in_proj(x, gain, win, tm, name):
    T, D = x.shape
    J, _, W = win.shape

    def body(x_ref, g_ref, w_ref, p_ref):
        _, xhat = _rms_parts(x_ref[...])
        n = (xhat * g_ref[...]).astype(BF16)
        for j in range(J):
            p_ref[j] = _dot(n, w_ref[j])

    return pl.pallas_call(
        body, name=name, grid=(T // tm,),
        in_specs=[pl.BlockSpec((tm, D), lambda i: (i, 0)),
                  pl.BlockSpec((1, D), lambda i: (0, 0)),
                  pl.BlockSpec((J, D, W), lambda i: (0, 0, 0))],
        out_specs=pl.BlockSpec((J, tm, W), lambda i: (0, i, 0)),
        out_shape=jax.ShapeDtypeStruct((J, T, W), F32),
        compiler_params=_params(1),
    )(x, gain, win)


def mixer_constants(S):
    B = RET_BLOCK
    half = RET_DK // 2
    freqs = ROPE_BASE ** (-jnp.arange(half, dtype=F32) * 2.0 / RET_DK)
    ang = jnp.arange(S, dtype=F32)[:, None] * freqs[None, :]
    cos = jnp.tile(jnp.cos(ang), (1, 2 * RET_HEADS))
    sin = jnp.tile(jnp.concatenate([-jnp.sin(ang), jnp.sin(ang)], axis=1), (1, RET_HEADS))
    gamma = 1.0 - 2.0 ** (-5.0 - jnp.arange(RET_HEADS, dtype=F32))
    log_g = jnp.log(gamma)
    idx = jnp.arange(B, dtype=F32)
    ci = jnp.arange(B) // CHUNK
    dist = jnp.abs(idx[:, None] - idx[None, :])
    m_intra = jnp.exp(log_g[:, None, None] * dist[None]) * (ci[None, :] <= ci[:, None])[None].astype(F32)
    lg_lane = jnp.repeat(log_g, RET_DK)
    d_q = jnp.exp(lg_lane[None, :] * (idx[:, None] + 1.0))
    d_k = jnp.exp(lg_lane[None, :] * (B - 1.0 - idx[:, None]))
    d_s = jnp.broadcast_to(jnp.exp(lg_lane * B)[:, None], (QK_W, V_W))
    bm = (jnp.arange(QK_W)[:, None] // RET_DK == jnp.arange(V_W)[None, :] // RET_DV).astype(F32)
    win = jnp.repeat(jnp.array(POOL_WINDOWS, F32), POOL_GC)
    invc = 1.0 / jnp.minimum(jnp.arange(S, dtype=F32)[:, None] + 1.0, win[None, :])
    return dict(cos=cos, sin=sin, m=m_intra, dq=d_q, dk=d_k, ds=d_s, bm=bm, invc=invc)


def _swap_halves(x):
    w = x.shape[1]
    lane = lax.broadcasted_iota(jnp.int32, x.shape, 1)
    return jnp.where((lane % RET_DK) < RET_DK // 2, pltpu.roll(x, w - RET_DK // 2, 1), pltpu.roll(x, RET_DK // 2, 1))


def _rot(x, cos, sin):
    return x * cos + _swap_halves(x) * sin


def _rot_t(d, cos, sin):
    return d * cos + _swap_halves(d * sin)


def _lane_groups(parts):
    return jnp.concatenate([p[:, POOL_GC * g:POOL_GC * (g + 1)] for g, p in enumerate(parts)], axis=1)


def _head_mask(shape, h):
    lane = lax.broadcasted_iota(jnp.int32, shape, 1)
    return (lane // RET_DK) == h


def _group_norm(o):
    yh, rs = [], []
    for h in range(RET_HEADS):
        oh = o[:, RET_DV * h:RET_DV * (h + 1)]
        xc = oh - jnp.mean(oh, axis=-1, keepdims=True)
        r = lax.rsqrt(jnp.mean(xc * xc, axis=-1, keepdims=True) + GN_EPS)
        yh.append(xc * r)
        rs.append(r)
    return yh, rs


def mixer_fwd(p, x1, consts, gn_gain, pool_w, pool_scale, wout, n_seq, S, name):
    _, T, _ = p.shape
    D = x1.shape[1]
    B = RET_BLOCK
    nb = S // B

    def body(p_ref, x1_ref, cos_ref, sin_ref, invc_ref, m_ref, dq_ref, dk_ref, ds_ref, bm_ref,
             gain_ref, pw_ref, sc_ref, wout_ref,
             x2_ref, m_out, o_out, st_out, pooled_out, state, prev_u):
        blk = pl.program_id(1)

        @pl.when(blk == 0)
        def _():
            state[...] = jnp.zeros_like(state)
            prev_u[...] = jnp.zeros_like(prev_u)

        qk = p_ref[0]
        v = p_ref[1]
        g = p_ref[2]
        u = p_ref[3]
        cos = cos_ref[...]
        sin = sin_ref[...]
        qr = _rot(qk[:, :QK_W], cos, sin) * (RET_DK ** -0.5)
        kr = _rot(qk[:, QK_W:], cos, sin)
        qb = qr.astype(BF16)
        kb = kr.astype(BF16)
        vb = v.astype(BF16)
        st = state[...]
        st_out[...] = st
        cross = _dot((qr * dq_ref[...]).astype(BF16), st.astype(BF16))
        o_parts = []
        for h in range(RET_HEADS):
            qm = jnp.where(_head_mask(qb.shape, h), qb, jnp.zeros_like(qb))
            sc = (_dot(qm, kb, NT) * m_ref[h]).astype(BF16)
            o_parts.append(_dot(sc, vb[:, RET_DV * h:RET_DV * (h + 1)]) + cross[:, RET_DV * h:RET_DV * (h + 1)])
        o = jnp.concatenate(o_parts, axis=1)
        o_out[...] = o
        kv = _dot((kr * dk_ref[...]).astype(BF16), vb, TN)
        state[...] = st * ds_ref[...] + kv * bm_ref[...]

        yh, _ = _group_norm(o)
        r = g * _sigmoid(g) * (jnp.concatenate(yh, axis=1) * gain_ref[...])

        ext = jnp.concatenate([prev_u[...], u], axis=0)
        sums = []
        run = ext
        for k in (1, 2, 4, 8):
            run = run + pltpu.roll(run, k, 0)
            sums.append(run[B:, :])
        prev_u[...] = u
        pooled = (_lane_groups(sums) * invc_ref[...] - u).astype(BF16)
        pooled_out[...] = pooled
        yp = [_dot(pooled[:, POOL_GC * gi:POOL_GC * (gi + 1)], pw_ref[gi].astype(BF16)) for gi in range(len(POOL_WINDOWS))]
        s = jnp.concatenate(yp, axis=1) * sc_ref[...]
        m = jnp.concatenate([r, s], axis=1).astype(BF16)
        m_out[...] = m
        x2_ref[...] = x1_ref[...] + _dot(m, wout_ref[...])

    tokmap = lambda b, k: (b * nb + k, 0)
    seqmap = lambda b, k: (k, 0)
    const2 = lambda b, k: (0, 0)
    const3 = lambda b, k: (0, 0, 0)
    return pl.pallas_call(
        body, name=name, grid=(n_seq, nb),
        in_specs=[pl.BlockSpec((4, B, V_W), lambda b, k: (0, b * nb + k, 0)),
                  pl.BlockSpec((B, D), tokmap),
                  pl.BlockSpec((B, QK_W), seqmap), pl.BlockSpec((B, QK_W), seqmap), pl.BlockSpec((B, V_W), seqmap),
                  pl.BlockSpec((RET_HEADS, B, B), const3),
                  pl.BlockSpec((B, QK_W), const2), pl.BlockSpec((B, QK_W), const2),
                  pl.BlockSpec((QK_W, V_W), const2), pl.BlockSpec((QK_W, V_W), const2),
                  pl.BlockSpec((1, V_W), const2), pl.BlockSpec((4, POOL_GC, POOL_GC), const3),
                  pl.BlockSpec((1, V_W), const2), pl.BlockSpec((D, D), const2)],
        out_specs=[pl.BlockSpec((B, D), tokmap), pl.BlockSpec((B, D), tokmap), pl.BlockSpec((B, V_W), tokmap),
                   pl.BlockSpec((None, QK_W, V_W), lambda b, k: (b * nb + k, 0, 0)),
                   pl.BlockSpec((B, V_W), tokmap)],
        out_shape=[jax.ShapeDtypeStruct((T, D), F32), jax.ShapeDtypeStruct((T, D), BF16),
                   jax.ShapeDtypeStruct((T, V_W), F32), jax.ShapeDtypeStruct((T // B, QK_W, V_W), F32),
                   jax.ShapeDtypeStruct((T, V_W), BF16)],
        scratch_shapes=[pltpu.VMEM((QK_W, V_W), F32), pltpu.VMEM((B, V_W), F32)],
        compiler_params=_params(2),
    )(p, x1, consts["cos"], consts["sin"], consts["invc"], consts["m"], consts["dq"], consts["dk"],
      consts["ds"], consts["bm"], gn_gain, pool_w, pool_scale, wout)


def mixer_bwd(dx3, dn3, x2, g3, x1, p, o_s, st_s, pooled_s, consts, gmix, gn_gain, pool_w, pool_scale, wout, win,
              n_seq, S, name, plan=None):
    T, D = x1.shape
    B = RET_BLOCK
    nb = S // B
    NG = len(POOL_WINDOWS)

    def body(dx3_ref, dn3_ref, x2_ref, g3_ref, x1_ref, p_ref, o_ref, st_ref, pooled_ref, cos_ref, sin_ref, invc_ref,
             m_ref, dq_ref, dk_ref, ds_ref, bm_ref, gmix_ref, gain_ref, pw_ref, sc_ref, wout_ref, win_ref,
             dx1_ref, df1_ref, dx2_ref, dp_ref, n2_ref, dg3_ref, dgmix_ref, dgain_ref, dscale_ref, dpw_ref,
             gstate, next_e):
        b = pl.program_id(0)
        blk = pl.program_id(1)

        @pl.when(jnp.logical_and(b == 0, blk == 0))
        def _():
            dg3_ref[...] = jnp.zeros_like(dg3_ref)
            dgmix_ref[...] = jnp.zeros_like(dgmix_ref)
            dgain_ref[...] = jnp.zeros_like(dgain_ref)
            dscale_ref[...] = jnp.zeros_like(dscale_ref)
            dpw_ref[...] = jnp.zeros_like(dpw_ref)

        @pl.when(blk == 0)
        def _():
            gstate[...] = jnp.zeros_like(gstate)
            next_e[...] = jnp.zeros_like(next_e)

        r2, xhat2 = _rms_parts(x2_ref[...])
        dn3 = dn3_ref[...]
        dg3_ref[...] += jnp.sum(dn3 * xhat2, axis=0, keepdims=True)
        dx2v = _rms_bwd(dn3, xhat2, r2, g3_ref[...], dx3_ref[...])
        dx2b = dx2v.astype(BF16)
        dx2_ref[...] = dx2b
        dm = _dot(dx2b, wout_ref[...], NT)
        dr = dm[:, :V_W]
        dsv = dm[:, V_W:]

        pooled = pooled_ref[...]
        scale = sc_ref[...]
        dyp = (dsv * scale).astype(BF16)
        yp, dpl = [], []
        for gi in range(NG):
            sl = slice(POOL_GC * gi, POOL_GC * (gi + 1))
            pwb = pw_ref[gi].astype(BF16)
            yp.append(_dot(pooled[:, sl], pwb))
            dpw_ref[gi] += _dot(pooled[:, sl], dyp[:, sl], TN)
            dpl.append(_dot(dyp[:, sl], pwb, NT))
        dscale_ref[...] += jnp.sum(dsv * jnp.concatenate(yp, axis=1), axis=0, keepdims=True)
        dpooled = jnp.concatenate(dpl, axis=1)
        e = dpooled * invc_ref[...]
        ext = jnp.concatenate([e, next_e[...]], axis=0)
        sums = []
        run = ext
        for k in (1, 2, 4, 8):
            run = run + pltpu.roll(run, 2 * B - k, 0)
            sums.append(run[:B, :])
        next_e[...] = e
        du = _lane_groups(sums) - dpooled

        qk = p_ref[0]
        v = p_ref[1]
        g = p_ref[2]
        sg = _sigmoid(g)
        si = g * sg
        yh, rs = _group_norm(o_ref[...])
        yhat = jnp.concatenate(yh, axis=1)
        gain = gain_ref[...]
        dg = dr * (yhat * gain) * (sg * (1.0 + g * (1.0 - sg)))
        dy = dr * si
        dgain_ref[...] += jnp.sum(dy * yhat, axis=0, keepdims=True)
        dyh = dy * gain
        do_parts = []
        for h in range(RET_HEADS):
            sl = slice(RET_DV * h, RET_DV * (h + 1))
            dh_ = dyh[:, sl]
            m1 = jnp.mean(dh_, axis=-1, keepdims=True)
            m2 = jnp.mean(dh_ * yh[h], axis=-1, keepdims=True)
            do_parts.append(rs[h] * (dh_ - m1 - yh[h] * m2))
        dob = jnp.concatenate(do_parts, axis=1).astype(BF16)

        cos = cos_ref[...]
        sin = sin_ref[...]
        qr = _rot(qk[:, :QK_W], cos, sin) * (RET_DK ** -0.5)
        kr = _rot(qk[:, QK_W:], cos, sin)
        qb = qr.astype(BF16)
        kb = kr.astype(BF16)
        vb = v.astype(BF16)
        dqd = dq_ref[...]
        dkd = dk_ref[...]
        stb = st_ref[...].astype(BF16)
        gs = gstate[...]
        gb = gs.astype(BF16)
        dqs = _dot(dob, stb, NT) * dqd
        dkr = _dot(vb, gb, NT) * dkd
        dv_cross = _dot((kr * dkd).astype(BF16), gb)
        ds_cross = _dot((qr * dqd).astype(BF16), dob, TN) * bm_ref[...]
        gstate[...] = ds_cross + gs * ds_ref[...]
        dv_parts = []
        for h in range(RET_HEADS):
            sl = slice(RET_DV * h, RET_DV * (h + 1))
            hm = _head_mask(qb.shape, h)
            qm = jnp.where(hm, qb, jnp.zeros_like(qb))
            mh = m_ref[h]
            sc = (_dot(qm, kb, NT) * mh).astype(BF16)
            dsc = (_dot(dob[:, sl], vb[:, sl], NT) * mh).astype(BF16)
            dqs = dqs + jnp.where(hm, _dot(dsc, kb), 0.0)
            dkr = dkr + jnp.where(hm, _dot(dsc, qb, TN), 0.0)
            dv_parts.append(_dot(sc, dob[:, sl], TN) + dv_cross[:, sl])
        dq = _rot_t(dqs * (RET_DK ** -0.5), cos, sin)
        dk = _rot_t(dkr, cos, sin)
        dp = [jnp.concatenate([dq, dk], axis=1).astype(BF16), jnp.concatenate(dv_parts, axis=1).astype(BF16),
              dg.astype(BF16), du.astype(BF16)]
        dn = jnp.zeros((B, D), F32)
        for jj in range(4):
            dp_ref[jj] = dp[jj]
            dn = dn + _dot(dp[jj], win_ref[jj], NT)

        x1v = x1_ref[...]
        r, xhat = _rms_parts(x1v)
        gm = gmix_ref[...]
        n2_ref[...] = (xhat * gm).astype(BF16)
        dgmix_ref[...] += jnp.sum(dn * xhat, axis=0, keepdims=True)
        dx1 = _rms_bwd(dn, xhat, r, gm, dx2v)
        dx1_ref[...] = dx1
        df1_ref[...] = (0.5 * dx1).astype(BF16)

    rev = lambda b, k: (b * nb + (nb - 1 - k), 0)
    seqrev = lambda b, k: (nb - 1 - k, 0)
    const2 = lambda b, k: (0, 0)
    const3 = lambda b, k: (0, 0, 0)
    return _call(
        body, plan, name=name, grid=(n_seq, nb),
        in_specs=[pl.BlockSpec((B, D), rev), pl.BlockSpec((B, D), rev), pl.BlockSpec((B, D), rev),
                  pl.BlockSpec((1, D), const2), pl.BlockSpec((B, D), rev),
                  pl.BlockSpec((4, B, V_W), lambda b, k: (0, b * nb + (nb - 1 - k), 0)),
                  pl.BlockSpec((B, V_W), rev),
                  pl.BlockSpec((None, QK_W, V_W), lambda b, k: (b * nb + (nb - 1 - k), 0, 0)),
                  pl.BlockSpec((B, V_W), rev),
                  pl.BlockSpec((B, QK_W), seqrev), pl.BlockSpec((B, QK_W), seqrev), pl.BlockSpec((B, V_W), seqrev),
                  pl.BlockSpec((RET_HEADS, B, B), const3),
                  pl.BlockSpec((B, QK_W), const2), pl.BlockSpec((B, QK_W), const2),
                  pl.BlockSpec((QK_W, V_W), const2), pl.BlockSpec((QK_W, V_W), const2),
                  pl.BlockSpec((1, D), const2), pl.BlockSpec((1, V_W), const2),
                  pl.BlockSpec((NG, POOL_GC, POOL_GC), const3), pl.BlockSpec((1, V_W), const2),
                  pl.BlockSpec((D, D), const2), pl.BlockSpec((4, D, V_W), const3)],
        out_specs=[pl.BlockSpec((B, D), rev), pl.BlockSpec((B, D), rev), pl.BlockSpec((B, D), rev),
                   pl.BlockSpec((4, B, V_W), lambda b, k: (0, b * nb + (nb - 1 - k), 0)),
                   pl.BlockSpec((B, D), rev),
                   pl.BlockSpec((1, D), const2), pl.BlockSpec((1, D), const2), pl.BlockSpec((1, V_W), const2),
                   pl.BlockSpec((1, V_W), const2), pl.BlockSpec((NG, POOL_GC, POOL_GC), const3)],
        out_shape=[jax.ShapeDtypeStruct((T, D), F32), jax.ShapeDtypeStruct((T, D), BF16),
                   jax.ShapeDtypeStruct((T, D), BF16), jax.ShapeDtypeStruct((4, T, V_W), BF16),
                   jax.ShapeDtypeStruct((T, D), BF16),
                   jax.ShapeDtypeStruct((1, D), F32), jax.ShapeDtypeStruct((1, D), F32), jax.ShapeDtypeStruct((1, V_W), F32),
                   jax.ShapeDtypeStruct((1, V_W), F32), jax.ShapeDtypeStruct((NG, POOL_GC, POOL_GC), F32)],
        scratch_shapes=[pltpu.VMEM((QK_W, V_W), F32), pltpu.VMEM((B, V_W), F32)],
        args=(dx3, dn3, x2, g3, x1, p, o_s, st_s, pooled_s, consts["cos"], consts["sin"], consts["invc"], consts["m"], consts["dq"],
              consts["dk"], consts["ds"], consts["bm"], gmix, gn_gain, pool_w, pool_scale, wout, win))


def loss_bwd(x3, tgt, gain, tm, name):
    T, D = x3.shape

    def body(x_ref, t_ref, g_ref, dx_ref, df_ref, lacc_ref, dg_ref):
        @pl.when(pl.program_id(0) == 0)
        def _():
            lacc_ref[...] = jnp.zeros_like(lacc_ref)
            dg_ref[...] = jnp.zeros_like(dg_ref)

        r, xhat = _rms_parts(x_ref[...])
        gv = g_ref[...]
        err = xhat * gv - t_ref[...]
        lacc_ref[...] += jnp.sum(err * err, axis=0, keepdims=True)
        dy = err * (1.0 / D)
        dg_ref[...] += jnp.sum(dy * xhat, axis=0, keepdims=True)
        dx = _rms_bwd(dy, xhat, r, gv, 0.0)
        dx_ref[...] = dx
        df_ref[...] = (0.5 * dx).astype(BF16)

    tok = pl.BlockSpec((tm, D), lambda i: (i, 0))
    vec = pl.BlockSpec((1, D), lambda i: (0, 0))
    return pl.pallas_call(
        body, name=name, grid=(T // tm,),
        in_specs=[tok, tok, vec], out_specs=[tok, tok, vec, vec],
        out_shape=[jax.ShapeDtypeStruct((T, D), F32), jax.ShapeDtypeStruct((T, D), BF16),
                   jax.ShapeDtypeStruct((1, D), F32), jax.ShapeDtypeStruct((1, D), F32)],
        compiler_params=_params(1),
    )(x3, tgt, gain)


def _adamw_math(w, g, m, v):
    m2 = ADAM_B1 * m + (1.0 - ADAM_B1) * g
    v2 = ADAM_B2 * v + (1.0 - ADAM_B2) * (g * g)
    m_hat = m2 / (1.0 - ADAM_B1 ** ADAM_STEP)
    v_hat = v2 / (1.0 - ADAM_B2 ** ADAM_STEP)
    return -ADAM_LR * (m_hat / (jnp.sqrt(v_hat) + ADAM_EPS) + ADAM_WD * w), m2, v2


def _place():
    x, y, c = lax.axis_index("x"), lax.axis_index("y"), lax.axis_index("c")
    other_chips = [(1 - x, y), (x, 1 - y), (1 - x, 1 - y)]
    return x, y, c, other_chips


def _exchange_plan(ins, out_shape, copies, n_copies):
    def descriptors(pins, pouts, psems):
        send, recv = psems
        return [pltpu.make_async_remote_copy(src_ref=s, dst_ref=d, send_sem=send.at[i], recv_sem=recv.at[i],
                                             device_id=dev, device_id_type=MESH)
                for i, (s, d, dev) in enumerate(copies(pins, pouts))]

    def start(pins, pouts, psems):
        for cp in descriptors(pins, pouts, psems):
            cp.start()

    def finish(pins, pouts, psems):
        for cp in descriptors(pins, pouts, psems):
            cp.wait()

    return CommPlan(tuple(ins), tuple(out_shape),
                    (pltpu.SemaphoreType.DMA((n_copies,)), pltpu.SemaphoreType.DMA((n_copies,))), start, finish)


def _combine(a, b):
    assert a.mid is None and b.mid is None
    ni, no, ns = len(a.ins), len(a.out_shape), len(a.sems)

    def start(pins, pouts, psems):
        a.start(pins[:ni], pouts[:no], psems[:ns])
        b.start(pins[ni:], pouts[no:], psems[ns:])

    def finish(pins, pouts, psems):
        a.finish(pins[:ni], pouts[:no], psems[:ns])
        b.finish(pins[ni:], pouts[no:], psems[ns:])

    return CommPlan(a.ins + b.ins, a.out_shape + b.out_shape, a.sems + b.sems, start, finish)


def gather_plan(shards):
    n = len(shards)

    def start(pins, pouts, psems):
        x, y, c, chips = _place()
        mine = 2 * x + y
        for w in range(n):
            for k, (px, py) in enumerate(chips):
                pltpu.make_async_remote_copy(
                    src_ref=pins[w].at[:, c], dst_ref=pouts[w].at[:, mine, c],
                    send_sem=psems[0].at[w, k], recv_sem=psems[1].at[w, k],
                    device_id=(px, py, c), device_id_type=MESH).start()

    def mid(pins, pouts, psems):
        x, y, c, chips = _place()
        for w in range(n):
            for k, (px, py) in enumerate(chips):
                landed = pouts[w].at[:, 2 * px + py, c]
                pltpu.make_async_remote_copy(
                    src_ref=landed, dst_ref=landed, send_sem=psems[0].at[w, k], recv_sem=psems[1].at[w, k],
                    device_id=(px, py, c), device_id_type=MESH).wait_recv()
                pltpu.make_async_remote_copy(
                    src_ref=landed, dst_ref=landed, send_sem=psems[2].at[w, k], recv_sem=psems[3].at[w, k],
                    device_id=(x, y, 1 - c), device_id_type=MESH).start()

    def finish(pins, pouts, psems):
        x, y, c, chips = _place()
        mine = 2 * x + y
        for w in range(n):
            for k, (px, py) in enumerate(chips):
                landed = pouts[w].at[:, 2 * px + py, 1 - c]
                cp = pltpu.make_async_remote_copy(
                    src_ref=landed, dst_ref=landed, send_sem=psems[2].at[w, k], recv_sem=psems[3].at[w, k],
                    device_id=(x, y, 1 - c), device_id_type=MESH)
                cp.wait_recv()
                cp.wait_send()
                pltpu.make_async_remote_copy(
                    src_ref=pins[w].at[:, c], dst_ref=pouts[w].at[:, mine, c],
                    send_sem=psems[0].at[w, k], recv_sem=psems[1].at[w, k],
                    device_id=(px, py, c), device_id_type=MESH).wait_send()

    return CommPlan(tuple(shards),
                    tuple(jax.ShapeDtypeStruct((s.shape[0], N_CHIPS) + s.shape[1:], s.dtype) for s in shards),
                    tuple(pltpu.SemaphoreType.DMA((n, 3)) for _ in range(4)), start, finish, mid)


def place_own(gathered, shard, name):
    L, _, Rh, C = shard.shape

    def body(chip_ref, g_ref, s_ref, o_ref):
        o_ref[...] = s_ref[...]

    return pl.pallas_call(
        body, name=name,
        grid_spec=pltpu.PrefetchScalarGridSpec(
            num_scalar_prefetch=1, grid=(L,),
            in_specs=[ANY, pl.BlockSpec((None, 2, Rh, C), lambda l, chip_ref: (l, 0, 0, 0))],
            out_specs=pl.BlockSpec((None, None, 2, Rh, C), lambda l, chip_ref: (l, chip_ref[0], 0, 0, 0))),
        out_shape=jax.ShapeDtypeStruct(gathered.shape, gathered.dtype),
        input_output_aliases={1: 0},
        compiler_params=_params(1),
    )(_chip_index(), gathered, shard)


def gather_weights(shards):
    n = len(shards)
    units = [(w, l) for w in range(n) for l in range(shards[w].shape[0])]
    nu = len(units)

    def body(*refs):
        ins, outs, bufs = refs[:n], refs[n:2 * n], refs[2 * n:3 * n]
        ld_sem, st_sem, own_send, own_recv, fwd_send, fwd_recv, d2d_send, d2d_recv = refs[3 * n:]
        x, y, c, _ = _place()
        sibling = (x, y, 1 - c)
        mine = 2 * x + y
        first = (x ^ (1 - c), y ^ c)
        second = (x ^ c, y ^ (1 - c))
        diagonal = (1 - x, 1 - y)

        def slot(chip):
            return 2 * chip[0] + chip[1]

        def remote(src, dst, send, recv, device):
            return pltpu.make_async_remote_copy(src_ref=src, dst_ref=dst, send_sem=send, recv_sem=recv,
                                                device_id=device, device_id_type=MESH)

        loads = [pltpu.make_async_copy(ins[w], bufs[w], ld_sem.at[w]) for w in range(n)]
        for cp in loads:
            cp.start()
        stores, sends = [], []
        for w in range(n):
            loads[w].wait()
            cp = pltpu.make_async_copy(bufs[w], outs[w].at[:, mine], st_sem.at[w])
            cp.start()
            stores.append(cp)
        for u, (w, l) in enumerate(units):
            for k, chip in enumerate((first, second)):
                cp = remote(bufs[w].at[l, c], outs[w].at[l, mine, c], own_send.at[u, k], own_recv.at[u, k], (*chip, c))
                cp.start()
                sends.append(cp)
        for u, (w, l) in enumerate(units):
            got = outs[w].at[l, slot(first), c]
            remote(got, got, own_send.at[u, 0], own_recv.at[u, 0], (*first, c)).wait_recv()
            for cp in (remote(got, got, fwd_send.at[u], fwd_recv.at[u], (*second, c)),
                       remote(got, got, d2d_send.at[u, 0], d2d_recv.at[u, 0], sibling)):
                cp.start()
                sends.append(cp)
        for u, (w, l) in enumerate(units):
            got = outs[w].at[l, slot(second), c]
            remote(got, got, own_send.at[u, 1], own_recv.at[u, 1], (*second, c)).wait_recv()
            cp = remote(got, got, d2d_send.at[u, 1], d2d_recv.at[u, 1], sibling)
            cp.start()
            sends.append(cp)
        for u, (w, l) in enumerate(units):
            got = outs[w].at[l, slot(diagonal), c]
            remote(got, got, fwd_send.at[u], fwd_recv.at[u], (*second, c)).wait_recv()
            cp = remote(got, got, d2d_send.at[u, 2], d2d_recv.at[u, 2], sibling)
            cp.start()
            sends.append(cp)
        for u, (w, l) in enumerate(units):
            for k, chip in enumerate((second, first, diagonal)):
                got = outs[w].at[l, slot(chip), 1 - c]
                remote(got, got, d2d_send.at[u, k], d2d_recv.at[u, k], sibling).wait_recv()
        for cp in sends:
            cp.wait_send()
        for cp in stores:
            cp.wait()

    return pl.pallas_call(
        body, name="gather_weights",
        in_specs=[ANY] * n, out_specs=[ANY] * n,
        out_shape=[jax.ShapeDtypeStruct((s.shape[0], N_CHIPS) + s.shape[1:], s.dtype) for s in shards],
        scratch_shapes=[pltpu.VMEM(s.shape, s.dtype) for s in shards] +
                       [pltpu.SemaphoreType.DMA((n,)), pltpu.SemaphoreType.DMA((n,)),
                        pltpu.SemaphoreType.DMA((nu, 2)), pltpu.SemaphoreType.DMA((nu, 2)),
                        pltpu.SemaphoreType.DMA((nu,)), pltpu.SemaphoreType.DMA((nu,)),
                        pltpu.SemaphoreType.DMA((nu, 3)), pltpu.SemaphoreType.DMA((nu, 3))],
        compiler_params=pltpu.CompilerParams(vmem_limit_bytes=VMEM_LIMIT),
    )(*shards)


def send_to_sibling_other_half(grads):
    def copies(ins, outs):
        x, y, c, _ = _place()
        return [(ins[w].at[:, :, 1 - c], outs[w], (x, y, 1 - c)) for w in range(len(grads))]

    return _exchange_plan(grads, [jax.ShapeDtypeStruct(g.shape[:2] + g.shape[3:], g.dtype) for g in grads], copies,
                          len(grads))


def _core_index():
    return jnp.reshape(lax.axis_index("c"), (1,)).astype(jnp.int32)


def _chip_index():
    return jnp.reshape(2 * lax.axis_index("x") + lax.axis_index("y"), (1,)).astype(jnp.int32)


def add_own_half(g, recv, name):
    L, J, _, Rh, C = g.shape

    def body(c_ref, g_ref, r_ref, o_ref):
        o_ref[...] = (g_ref[...] + r_ref[...]).astype(BF16)

    return pl.pallas_call(
        body, name=name,
        grid_spec=pltpu.PrefetchScalarGridSpec(
            num_scalar_prefetch=1, grid=(L, J),
            in_specs=[pl.BlockSpec((None, None, None, Rh, C), lambda l, j, c_ref: (l, j, c_ref[0], 0, 0)),
                      pl.BlockSpec((None, None, Rh, C), lambda l, j, c_ref: (l, j, 0, 0))],
            out_specs=pl.BlockSpec((None, None, Rh, C), lambda l, j, c_ref: (l, j, 0, 0))),
        out_shape=jax.ShapeDtypeStruct((L, J, Rh, C), BF16),
        compiler_params=_params(2),
    )(_core_index(), g, recv)


def exchange_between_chips(sums):
    def copies(ins, outs):
        x, y, c, chips = _place()
        return [(ins[w].at[:, 2 * px + py], outs[w].at[k], (px, py, c))
                for w in range(len(sums)) for k, (px, py) in enumerate(chips)]

    return _exchange_plan(sums, [jax.ShapeDtypeStruct((3, s.shape[0]) + s.shape[2:], s.dtype) for s in sums], copies,
                          3 * len(sums))


def sum_chips(own, recv, name):
    L, _, Rh, C = own.shape

    def body(chip_ref, o_ref, r_ref, out_ref):
        acc = o_ref[...].astype(F32)
        for k in range(3):
            acc = acc + r_ref[k].astype(F32)
        out_ref[...] = acc

    return pl.pallas_call(
        body, name=name,
        grid_spec=pltpu.PrefetchScalarGridSpec(
            num_scalar_prefetch=1, grid=(L,),
            in_specs=[pl.BlockSpec((None, None, Rh, C), lambda l, chip_ref: (l, chip_ref[0], 0, 0)),
                      pl.BlockSpec((3, None, Rh, C), lambda l, chip_ref: (0, l, 0, 0))],
            out_specs=pl.BlockSpec((None, Rh, C), lambda l, chip_ref: (l, 0, 0))),
        out_shape=jax.ShapeDtypeStruct((L, Rh, C), F32),
        compiler_params=_params(1),
    )(_chip_index(), own, recv)


def share_with_sibling(halves):
    def copies(ins, outs):
        x, y, c, _ = _place()
        return [(ins[w], outs[w], (x, y, 1 - c)) for w in range(len(halves))]

    return _exchange_plan(halves, [jax.ShapeDtypeStruct(h.shape, h.dtype) for h in halves], copies, len(halves))


def adamw_group(ws, ms, vs, own, sib, name, plan=None):
    L = len(ws)
    R, C = ws[0].shape
    Rh = R // 2
    tr = Rh // 2
    nr = Rh // tr

    def body(*refs):
        w, m, v = refs[:L], refs[L:2 * L], refs[2 * L:3 * L]
        own_ref, sib_ref = refs[3 * L], refs[3 * L + 1]
        outs = refs[3 * L + 2:]
        mine = pl.program_id(0) == lax.axis_index("c")
        for l in range(L):
            gv = jnp.where(mine, own_ref[l], sib_ref[l])
            d, m2, v2 = _adamw_math(w[l][...], gv, m[l][...], v[l][...])
            outs[4 * l][...] = d
            outs[4 * l + 1][...] = m2
            outs[4 * l + 2][...] = v2
            outs[4 * l + 3][...] = gv

    blk = pl.BlockSpec((tr, C), lambda h, r: (h * nr + r, 0))
    half = pl.BlockSpec((L, tr, C), lambda h, r: (0, r, 0))
    sds = jax.ShapeDtypeStruct((R, C), F32)
    return _call(body, plan, name=name, grid=(2, nr), in_specs=[blk] * (3 * L) + [half, half],
                 out_specs=[blk] * (4 * L), out_shape=[sds] * (4 * L), args=(*ws, *ms, *vs, own, sib))


def gather_small_plan(part):
    def copies(ins, outs):
        x, y, c, _ = _place()
        me = 4 * x + 2 * y + c
        return [(ins[0], outs[0].at[me], (x ^ ((k >> 2) & 1), y ^ ((k >> 1) & 1), c ^ (k & 1))) for k in range(1, N_DEV)]

    return _exchange_plan([part], [jax.ShapeDtypeStruct((N_DEV,) + part.shape, part.dtype)], copies, N_DEV - 1)


def sum_small(part, gathered, name):
    R, C = part.shape

    def body(p_ref, g_ref, o_ref):
        d = pl.program_id(0)
        me = 4 * lax.axis_index("x") + 2 * lax.axis_index("y") + lax.axis_index("c")
        val = jnp.where(d == me, p_ref[...], g_ref[...])

        @pl.when(d == 0)
        def _():
            o_ref[...] = val

        @pl.when(d > 0)
        def _():
            o_ref[...] += val

    return pl.pallas_call(
        body, name=name, grid=(N_DEV,),
        in_specs=[pl.BlockSpec((R, C), lambda d: (0, 0)), pl.BlockSpec((None, R, C), lambda d: (d, 0, 0))],
        out_specs=pl.BlockSpec((R, C), lambda d: (0, 0)),
        out_shape=jax.ShapeDtypeStruct((R, C), F32),
        compiler_params=_params(1),
    )(part, gathered)


SMALL = ("norm_ffn1", "norm_mix", "norm_ffn2", "norm_final", "ret_gn_gain", "pool_scale", "pool_w")
GROUPS = (("ffn1", ("ffn1_gate", "ffn1_up", "ffn1_down")), ("w_in", ("w_in",)), ("w_out", ("w_out",)),
          ("ffn2", ("ffn2_gate", "ffn2_up", "ffn2_down")))
TRANSPOSED = ("ffn1_gate", "ffn1_up", "ffn2_gate", "ffn2_up")
IN_W = 2 * QK_W + 3 * V_W
ORDER = ("norm_ffn1", "ffn1_gate", "ffn1_up", "ffn1_down", "norm_mix", "w_in", "ret_gn_gain", "pool_w", "pool_scale",
         "w_out", "norm_ffn2", "ffn2_gate", "ffn2_up", "ffn2_down", "norm_final")


SUBLANES = 8


def _small_rows(shapes):
    rows = [math.prod(shapes[k]) // 128 for k in SMALL]
    offs, off = [], 0
    for r in rows:
        offs.append(off)
        off += -(-r // SUBLANES) * SUBLANES
    return rows, offs, off


def _pack_small(d, shapes):
    rows, offs, _ = _small_rows(shapes)
    pieces = []
    for k, r in zip(SMALL[1:], rows[1:]):
        pieces.append(d[k].reshape(r, 128))
        if r % SUBLANES:
            pieces.append(jnp.zeros((SUBLANES - r % SUBLANES, 128), F32))
    return jnp.concatenate(pieces, axis=0)


def adamw_small(W, M, V, first, summed, shapes):
    rows, offs, _ = _small_rows(shapes)
    n = len(SMALL)

    def body(*refs):
        w, m, v, f_ref, s_ref = refs[:n], refs[n:2 * n], refs[2 * n:3 * n], refs[3 * n], refs[3 * n + 1]
        outs = refs[3 * n + 2:]
        for i, (r, off) in enumerate(zip(rows, offs)):
            gv = f_ref[...] if i == 0 else s_ref[off - offs[1]:off - offs[1] + r, :]
            d, m2, v2 = _adamw_math(w[i][...], gv, m[i][...], v[i][...])
            outs[4 * i][...] = d
            outs[4 * i + 1][...] = m2
            outs[4 * i + 2][...] = v2
            outs[4 * i + 3][...] = gv

    vm = pl.BlockSpec(memory_space=pltpu.VMEM)
    flat = lambda d: [d[k].reshape(r, 128) for k, r in zip(SMALL, rows)]
    res = pl.pallas_call(
        body, name="adamw_small", in_specs=[vm] * (3 * n + 2), out_specs=[vm] * (4 * n),
        out_shape=[jax.ShapeDtypeStruct((r, 128), F32) for r in rows for _ in range(4)],
    )(*flat(W), *flat(M), *flat(V), first, summed)
    return {k: [a.reshape(shapes[k]) for a in res[4 * i:4 * i + 4]] for i, k in enumerate(SMALL)}


def kernel(x, norm_ffn1, ffn1_gate, ffn1_up, ffn1_down, norm_mix, w_in, ret_gn_gain, pool_w, pool_scale, w_out, norm_ffn2, ffn2_gate, ffn2_up, ffn2_down, norm_final, loss_target, m_norm_ffn1, m_ffn1_gate, m_ffn1_up, m_ffn1_down, m_norm_mix, m_w_in, m_ret_gn_gain, m_pool_w, m_pool_scale, m_w_out, m_norm_ffn2, m_ffn2_gate, m_ffn2_up, m_ffn2_down, m_norm_final, v_norm_ffn1, v_ffn1_gate, v_ffn1_up, v_ffn1_down, v_norm_mix, v_w_in, v_ret_gn_gain, v_pool_w, v_pool_scale, v_w_out, v_norm_ffn2, v_ffn2_gate, v_ffn2_up, v_ffn2_down, v_norm_final):
    W = dict(norm_ffn1=norm_ffn1, ffn1_gate=ffn1_gate, ffn1_up=ffn1_up, ffn1_down=ffn1_down, norm_mix=norm_mix, w_in=w_in,
             ret_gn_gain=ret_gn_gain, pool_w=pool_w, pool_scale=pool_scale, w_out=w_out, norm_ffn2=norm_ffn2,
             ffn2_gate=ffn2_gate, ffn2_up=ffn2_up, ffn2_down=ffn2_down, norm_final=norm_final)
    M = dict(norm_ffn1=m_norm_ffn1, ffn1_gate=m_ffn1_gate, ffn1_up=m_ffn1_up, ffn1_down=m_ffn1_down, norm_mix=m_norm_mix,
             w_in=m_w_in, ret_gn_gain=m_ret_gn_gain, pool_w=m_pool_w, pool_scale=m_pool_scale, w_out=m_w_out,
             norm_ffn2=m_norm_ffn2, ffn2_gate=m_ffn2_gate, ffn2_up=m_ffn2_up, ffn2_down=m_ffn2_down, norm_final=m_norm_final)
    V = dict(norm_ffn1=v_norm_ffn1, ffn1_gate=v_ffn1_gate, ffn1_up=v_ffn1_up, ffn1_down=v_ffn1_down, norm_mix=v_norm_mix,
             w_in=v_w_in, ret_gn_gain=v_ret_gn_gain, pool_w=v_pool_w, pool_scale=v_pool_scale, w_out=v_w_out,
             norm_ffn2=v_norm_ffn2, ffn2_gate=v_ffn2_gate, ffn2_up=v_ffn2_up, ffn2_down=v_ffn2_down, norm_final=v_norm_final)

    n_seq, S, D = x.shape
    T = n_seq * S
    tm = min(1024, T // 2)
    tk = min(1024, T // 2)
    tk1 = min(2048, T // 2)
    xf = x.reshape(T, D)
    tgt = loss_target.reshape(T, D)

    def local(d, k):
        a = d[k][0]
        return a.T if k in TRANSPOSED else a

    def to_out(k, a):
        return (a.T if k in TRANSPOSED else a)[None]

    loc_bf = []
    for _, members in GROUPS:
        st = jnp.stack([local(W, k).astype(BF16) for k in members])
        loc_bf.append(st.reshape(st.shape[0], 2, st.shape[1] // 2, st.shape[2]))
    FS = loc_bf[0].shape[2] * 2
    w3_1 = gather_weights(loc_bf[:1])[0].reshape(3, N_CHIPS, FS, D)
    g1 = norm_ffn1.reshape(1, D)
    gm = norm_mix.reshape(1, D)
    g3 = norm_ffn2.reshape(1, D)
    gf = norm_final.reshape(1, D)
    gn_gain = ret_gn_gain.reshape(1, V_W)
    pscale = pool_scale.reshape(1, V_W)
    pw = pool_w.reshape(len(POOL_WINDOWS), POOL_GC, POOL_GC)
    consts = mixer_constants(S)

    def halves5(g, lb):
        return g.reshape(lb.shape[0], N_CHIPS, 2, lb.shape[2], lb.shape[3])

    (x1, b1, si1, t1, n1), landed = ffn_fwd(xf, g1, w3_1, tm, "ffn1_fwd", plan=gather_plan(loc_bf[1:]))
    gathered = [place_own(g, lb, "place_own_" + gn) for g, lb, (gn, _) in zip(landed, loc_bf[1:], GROUPS[1:])]
    win_full = gathered[0].reshape(N_CHIPS, D, IN_W // N_CHIPS)
    wout_full = gathered[1].reshape(D, D)
    w3_2 = gathered[2].reshape(3, N_CHIPS, FS, D)
    p = in_proj(x1, gm, win_full, tm, "in_proj")
    x2, m_s, o_s, st_s, pooled_s = mixer_fwd(p, x1, consts, gn_gain, pw, pscale, wout_full, n_seq, S, "mixer_fwd")
    x3, b2, si2, t2, n3 = ffn_fwd(x2, g3, w3_2, tm, "ffn2_fwd")

    dx3, df2, lacc, dgf = loss_bwd(x3, tgt, gf, tm, "loss_bwd")
    dn3, da2, db2 = ffn_bwd(df2, b2, si2, t2, w3_2, tm, "ffn2_bwd")
    to_sib, to_chips, back = send_to_sibling_other_half, exchange_between_chips, share_with_sibling
    out_g, out_d, out_m, out_v = {}, {}, {}, {}

    def update(members, own, sib, name, plan=None):
        res = adamw_group([local(W, k) for k in members], [local(M, k) for k in members], [local(V, k) for k in members],
                          own, sib, "adamw_" + name, plan=plan)
        res, landed = res if plan is not None else (res, None)
        for l, k in enumerate(members):
            out_d[k], out_m[k], out_v[k], out_g[k] = (to_out(k, r) for r in res[4 * l:4 * l + 4])
        return landed

    (gr_ffn2,) = ffn_wgrad(da2, db2, si2, b2, n3, df2, tk, "ffn2_wgrad")
    g_ffn2 = halves5(gr_ffn2, loc_bf[3])
    (dx1, df1, dx2b, dp, n2, dg3, dgm, dgn, dps, dpw), (sb_ffn2,) = mixer_bwd(
        dx3, dn3, x2, g3, x1, p, o_s, st_s, pooled_s, consts, gm, gn_gain, pw, pscale, wout_full, win_full, n_seq, S,
        "mixer_bwd", plan=to_sib([g_ffn2]))
    cs_ffn2 = add_own_half(g_ffn2, sb_ffn2, "add_sibling_ffn2")
    (gr_wout,) = tn_shared(m_s, dx2b[None], tk1, "dw_out")
    (gr_win,) = tn_shared(n2, dp, tk1, "dw_in")
    g_mix = [halves5(gr_win, loc_bf[1]), halves5(gr_wout, loc_bf[2])]
    shapes = {k: W[k].shape for k in SMALL}
    small_part = dict(norm_mix=dgm, norm_ffn2=dg3, norm_final=dgf, ret_gn_gain=dgn, pool_scale=dps, pool_w=dpw)
    part = jnp.concatenate([_pack_small(small_part, shapes), lacc.reshape(-1, 128)], axis=0)
    n_small = part.shape[0] - lacc.size // 128
    (dn1, da1, db1), (pc_ffn2, sb_win, sb_wout, parts) = ffn_bwd(
        df1, b1, si1, t1, w3_1, tm, "ffn1_bwd",
        plan=_combine(_combine(to_chips([cs_ffn2]), to_sib(g_mix)), gather_small_plan(part)))
    summed = sum_small(part, parts, "sum_small")
    loss = jnp.sum(summed[n_small:]) * (0.5 / D)
    mh_ffn2 = sum_chips(cs_ffn2, pc_ffn2, "sum_chips_ffn2")
    cs_mix = [add_own_half(g, r, "add_sibling_" + gn) for g, r, gn in zip(g_mix, (sb_win, sb_wout), ("w_in", "w_out"))]
    def one(g):
        return g.reshape((1, N_CHIPS, 2) + loc_bf[0].shape[2:])

    (gr_g,), (sh_ffn2, pc_win, pc_wout) = wgrad_one(
        da1, n1, tk1, "ffn1_wgrad_gate", plan=_combine(back([mh_ffn2]), to_chips(cs_mix)))
    mh_mix = [sum_chips(cs, pc, "sum_chips_" + gn) for cs, pc, gn in zip(cs_mix, (pc_win, pc_wout), ("w_in", "w_out"))]
    (gr_u,), (sb_g,) = wgrad_one(db1, n1, tk1, "ffn1_wgrad_up", plan=to_sib([one(gr_g)]))
    cs_g = add_own_half(one(gr_g), sb_g, "add_sibling_ffn1_gate")
    (gr_d,), (pc_g, sb_u) = wgrad_one(
        si1, df1, tk1, "ffn1_wgrad_down", plan=_combine(to_chips([cs_g]), to_sib([one(gr_u)])), a2=b1)
    mh_g = sum_chips(cs_g, pc_g, "sum_chips_ffn1_gate")
    cs_u = add_own_half(one(gr_u), sb_u, "add_sibling_ffn1_up")
    (dx0, dg1), (sh_g, sh_win, sh_wout, pc_u, sb_d) = rms_bwd(
        xf, dx1, dn1, g1, tm, "ffn1_rms_bwd",
        plan=_combine(_combine(back([mh_g] + mh_mix), to_chips([cs_u])), to_sib([one(gr_d)])))
    mh_u = sum_chips(cs_u, pc_u, "sum_chips_ffn1_up")
    cs_d = add_own_half(one(gr_d), sb_d, "add_sibling_ffn1_down")
    first = dg1.reshape(-1, 128)
    sh_u, pc_d, firsts = update(GROUPS[3][1], mh_ffn2, sh_ffn2, "ffn2", plan=_combine(
        _combine(back([mh_u]), to_chips([cs_d])), gather_small_plan(first)))
    mh_d = sum_chips(cs_d, pc_d, "sum_chips_ffn1_down")
    (sh_d,) = _run_plan(back([mh_d]), "ffn1_grads_stage_c")
    first_sum = sum_small(first, firsts, "sum_small_first")
    for k, own, sib in zip(GROUPS[0][1], (mh_g, mh_u, mh_d), (sh_g, sh_u, sh_d)):
        update((k,), own, sib, k)
    update(GROUPS[1][1], mh_mix[0], sh_win, "w_in")
    update(GROUPS[2][1], mh_mix[1], sh_wout, "w_out")

    for k, (d_, m_, v_, g_) in adamw_small(W, M, V, first_sum, summed, shapes).items():
        out_d[k], out_m[k], out_v[k], out_g[k] = d_, m_, v_, g_

    grad_x = dx0.reshape(n_seq, S, D)
    return (loss, grad_x, *[out_g[k] for k in ORDER], *[out_d[k] for k in ORDER],
            *[out_m[k] for k in ORDER], *[out_v[k] for k in ORDER])
```

```python
import functools
import math
from typing import Callable, NamedTuple, Optional

import jax
import jax.numpy as jnp
from jax import lax
from jax.experimental import pallas as pl
from jax.experimental.pallas import tpu as pltpu

F32 = jnp.float32
BF16 = jnp.bfloat16
MESH = pl.DeviceIdType.MESH

N_CHIPS = 4
N_DEV = 8
CHUNK = 64
RET_HEADS = 4
RET_DK = 64
RET_DV = 128
QK_W = RET_HEADS * RET_DK
V_W = RET_HEADS * RET_DV
POOL_WINDOWS = (2, 4, 8, 16)
POOL_GC = 128
ROPE_BASE = 10000.0
RMS_EPS = 1e-6
GN_EPS = 1e-5
ADAM_LR = 0.001
ADAM_B1 = 0.9
ADAM_B2 = 0.999
ADAM_EPS = 1e-08
ADAM_WD = 0.01
ADAM_STEP = 10
MXU_W = 256
RET_BLOCK = MXU_W
VMEM_LIMIT = 56 * 1024 * 1024

NN = (((1,), (0,)), ((), ()))
NT = (((1,), (1,)), ((), ()))
TN = (((0,), (0,)), ((), ()))


def _dot(a, b, dims=NN):
    return lax.dot_general(a, b, dims, preferred_element_type=F32)


def _sigmoid(x):
    return 0.5 * jnp.tanh(0.5 * x) + 0.5


def _params(n_grid):
    return pltpu.CompilerParams(dimension_semantics=("arbitrary",) * n_grid, vmem_limit_bytes=VMEM_LIMIT)


class CommPlan(NamedTuple):
    ins: tuple
    out_shape: tuple
    sems: tuple
    start: Callable
    finish: Callable
    mid: Optional[Callable] = None
    mid_at: float = 0.85


ANY = pl.BlockSpec(memory_space=pl.ANY)


def _call(body, plan, *, name, grid, in_specs, out_specs, out_shape, args, scratch_shapes=()):
    n_grid = len(grid)
    if plan is None:
        return pl.pallas_call(body, name=name, grid=grid, in_specs=in_specs, out_specs=out_specs, out_shape=out_shape,
                              scratch_shapes=scratch_shapes, compiler_params=_params(n_grid))(*args)
    n_in, n_out, n_sc = len(in_specs), len(out_specs), len(scratch_shapes)
    p_in, p_out = len(plan.ins), len(plan.out_shape)
    total = math.prod(grid)
    mid_step = min(total - 1, int(plan.mid_at * total))

    def riding(*refs):
        ins, pins = refs[:n_in], refs[n_in:n_in + p_in]
        o0 = n_in + p_in
        outs, pouts = refs[o0:o0 + n_out], refs[o0 + n_out:o0 + n_out + p_out]
        s0 = o0 + n_out + p_out
        scratch, psems = refs[s0:s0 + n_sc], refs[s0 + n_sc:]
        step = pl.program_id(0)
        for d in range(1, n_grid):
            step = step * grid[d] + pl.program_id(d)

        @pl.when(step == 0)
        def _():
            plan.start(pins, pouts, psems)

        body(*ins, *outs, *scratch)

        if plan.mid is not None:
            @pl.when(step == mid_step)
            def _():
                plan.mid(pins, pouts, psems)

        @pl.when(step == total - 1)
        def _():
            plan.finish(pins, pouts, psems)

    res = pl.pallas_call(
        riding, name=name, grid=grid, in_specs=list(in_specs) + [ANY] * p_in, out_specs=list(out_specs) + [ANY] * p_out,
        out_shape=list(out_shape) + list(plan.out_shape), scratch_shapes=list(scratch_shapes) + list(plan.sems),
        compiler_params=_params(n_grid))(*args, *plan.ins)
    return res[:n_out], res[n_out:]


def _run_plan(plan, name):
    p_in, p_out = len(plan.ins), len(plan.out_shape)

    def body(*refs):
        pins, pouts, psems = refs[:p_in], refs[p_in:p_in + p_out], refs[p_in + p_out:]
        plan.start(pins, pouts, psems)
        if plan.mid is not None:
            plan.mid(pins, pouts, psems)
        plan.finish(pins, pouts, psems)

    return pl.pallas_call(body, name=name, in_specs=[ANY] * p_in, out_specs=[ANY] * p_out,
                          out_shape=list(plan.out_shape), scratch_shapes=list(plan.sems))(*plan.ins)


def _rms_parts(x):
    r = lax.rsqrt(jnp.mean(x * x, axis=-1, keepdims=True) + RMS_EPS)
    return r, x * r


def _rms_bwd(dn, xhat, r, gain, dres):
    dxh = dn * gain
    return dres + r * (dxh - xhat * jnp.mean(dxh * xhat, axis=-1, keepdims=True))


def _stacked(n, rows, cols):
    return pl.BlockSpec((n, None, rows, cols), lambda i, j: (0, j, i, 0))


def ffn_fwd(x, gain, w3, tm, name, plan=None):
    T, D = x.shape
    _, J, FS, _ = w3.shape

    def body(x_ref, g_ref, w_ref, xo_ref, act_ref, n_ref, acc):
        j = pl.program_id(1)

        @pl.when(j == 0)
        def _():
            _, xhat = _rms_parts(x_ref[...])
            n_ref[...] = (xhat * g_ref[...]).astype(BF16)
            acc[...] = jnp.zeros_like(acc)

        n = n_ref[...]
        a = _dot(n, w_ref[0], NT)
        b = _dot(n, w_ref[1], NT)
        sg = _sigmoid(a)
        si = a * sg
        act_ref[0] = b.astype(BF16)
        act_ref[1] = si.astype(BF16)
        act_ref[2] = (sg + si * (1.0 - sg)).astype(BF16)
        acc[...] += _dot((si * b).astype(BF16), w_ref[2])

        @pl.when(j == J - 1)
        def _():
            xo_ref[...] = x_ref[...] + 0.5 * acc[...]

    tok = pl.BlockSpec((tm, D), lambda i, j: (i, 0))
    return _call(
        body, plan, name=name, grid=(T // tm, J),
        in_specs=[tok, pl.BlockSpec((1, D), lambda i, j: (0, 0)), pl.BlockSpec((3, None, FS, D), lambda i, j: (0, j, 0, 0))],
        out_specs=[tok, _stacked(3, tm, FS), tok],
        out_shape=[jax.ShapeDtypeStruct((T, D), F32), jax.ShapeDtypeStruct((3, J, T, FS), BF16),
                   jax.ShapeDtypeStruct((T, D), BF16)],
        scratch_shapes=[pltpu.VMEM((tm, D), F32)],
        args=(x, gain, w3))


def ffn_bwd(df, act, w3, tm, name, plan=None):
    T, D = df.shape
    _, J, FS, _ = w3.shape

    chunks = [slice(s, min(s + MXU_W, FS)) for s in range(0, FS, MXU_W)]
    nc = len(chunks)

    def body(df_ref, act_ref, w_ref, dn_ref, dab_ref):
        @pl.when(pl.program_id(1) == 0)
        def _():
            dn_ref[...] = jnp.zeros_like(dn_ref)

        def elementwise(dh, sl):
            da = (dh * act_ref[0, :, sl].astype(F32) * act_ref[2, :, sl].astype(F32)).astype(BF16)
            db = (dh * act_ref[1, :, sl].astype(F32)).astype(BF16)
            dab_ref[0, :, sl] = da
            dab_ref[1, :, sl] = db
            return da, db

        df = df_ref[...]
        dh, dab, acc = [None] * nc, [None] * nc, None
        for step in range(nc + 2):
            if step < nc:
                dh[step] = _dot(df, w_ref[2, chunks[step], :], NT)
            if 1 <= step <= nc:
                dab[step - 1] = elementwise(dh[step - 1], chunks[step - 1])
            if step >= 2:
                sl = chunks[step - 2]
                part = _dot(dab[step - 2][0], w_ref[0, sl, :]) + _dot(dab[step - 2][1], w_ref[1, sl, :])
                acc = part if acc is None else acc + part
        dn_ref[...] += acc

    tok = pl.BlockSpec((tm, D), lambda i, j: (i, 0))
    return _call(
        body, plan, name=name, grid=(T // tm, J),
        in_specs=[tok, _stacked(3, tm, FS), pl.BlockSpec((3, None, FS, D), lambda i, j: (0, j, 0, 0))],
        out_specs=[tok, _stacked(2, tm, FS)],
        out_shape=[jax.ShapeDtypeStruct((T, D), F32), jax.ShapeDtypeStruct((2, J, T, FS), BF16)],
        args=(df, act, w3))


def rms_bwd(x, dout, dn, gain, tm, name, plan=None):
    T, D = x.shape

    def body(x_ref, do_ref, dn_ref, g_ref, dx_ref, dgain_ref):
        @pl.when(pl.program_id(0) == 0)
        def _():
            dgain_ref[...] = jnp.zeros_like(dgain_ref)

        r, xhat = _rms_parts(x_ref[...])
        dn = dn_ref[...]
        dgain_ref[...] += jnp.sum(dn * xhat, axis=0, keepdims=True)
        dx_ref[...] = _rms_bwd(dn, xhat, r, g_ref[...], do_ref[...])

    tok = pl.BlockSpec((tm, D), lambda i: (i, 0))
    vec = pl.BlockSpec((1, D), lambda i: (0, 0))
    return _call(
        body, plan, name=name, grid=(T // tm,), in_specs=[tok, tok, tok, vec], out_specs=[tok, vec],
        out_shape=[jax.ShapeDtypeStruct((T, D), F32), jax.ShapeDtypeStruct((1, D), F32)],
        args=(x, dout, dn, gain))


def ffn_wgrad(dab, act, n, df, tk, name, plan=None):
    _, J, T, FS = dab.shape
    D = n.shape[1]

    def body(dab_ref, act_ref, n_ref, df_ref, o_ref):
        @pl.when(pl.program_id(1) == 0)
        def _():
            o_ref[...] = jnp.zeros_like(o_ref)

        nv = n_ref[...]
        o_ref[0] += _dot(dab_ref[0], nv, TN)
        o_ref[1] += _dot(dab_ref[1], nv, TN)
        o_ref[2] += _dot(act_ref[1] * act_ref[0], df_ref[...], TN)

    sh = pl.BlockSpec((2, None, tk, FS), lambda j, k: (0, j, k, 0))
    tok = pl.BlockSpec((tk, D), lambda j, k: (k, 0))
    return _call(
        body, plan, name=name, grid=(J, T // tk),
        in_specs=[sh, sh, tok, tok],
        out_specs=[pl.BlockSpec((3, None, FS, D), lambda j, k: (0, j, 0, 0))],
        out_shape=[jax.ShapeDtypeStruct((3, J, FS, D), F32)],
        args=(dab, act, n, df))


def wgrad_one(stack, which, b, tk, name, plan=None):
    _, J, T, FS = stack.shape
    D = b.shape[1]

    def body(*refs):
        o_ref = refs[-1]

        @pl.when(pl.program_id(1) == 0)
        def _():
            o_ref[...] = jnp.zeros_like(o_ref)

        av = refs[0][...] if len(which) == 1 else refs[0][...] * refs[1][...]
        o_ref[...] += _dot(av, refs[-2][...], TN)

    return _call(
        body, plan, name=name, grid=(J, T // tk),
        in_specs=[pl.BlockSpec((None, None, tk, FS), functools.partial(lambda j, k, s: (s, j, k, 0), s=s))
                  for s in which] + [pl.BlockSpec((tk, D), lambda j, k: (k, 0))],
        out_specs=[pl.BlockSpec((None, FS, D), lambda j, k: (j, 0, 0))],
        out_shape=[jax.ShapeDtypeStruct((J, FS, D), F32)],
        args=(*([stack] * len(which)), b))


def tn_shared(a, b, tk, name, plan=None):
    T, M = a.shape
    J, _, N = b.shape

    def body(a_ref, b_ref, o_ref):
        @pl.when(pl.program_id(0) == 0)
        def _():
            o_ref[...] = jnp.zeros_like(o_ref)

        a_t = a_ref[...].astype(BF16).T
        for j in range(J):
            o_ref[j] += _dot(a_t, b_ref[j].astype(BF16))

    return _call(
        body, plan, name=name, grid=(T // tk,),
        in_specs=[pl.BlockSpec((tk, M), lambda k: (k, 0)), pl.BlockSpec((J, tk, N), lambda k: (0, k, 0))],
        out_specs=[pl.BlockSpec((J, M, N), lambda k: (0, 0, 0))],
        out_shape=[jax.ShapeDtypeStruct((J, M, N), F32)],
        args=(a, b))


def in_proj(x, gain, win, tm, name):
    T, D = x.shape
    J, _, W = win.shape

    def body(x_ref, g_ref, w_ref, p_ref):
        _, xhat = _rms_parts(x_ref[...])
        n = (xhat * g_ref[...]).astype(BF16)
        for j in range(J):
            p_ref[j] = _dot(n, w_ref[j])

    return pl.pallas_call(
        body, name=name, grid=(T // tm,),
        in_specs=[pl.BlockSpec((tm, D), lambda i: (i, 0)),
                  pl.BlockSpec((1, D), lambda i: (0, 0)),
                  pl.BlockSpec((J, D, W), lambda i: (0, 0, 0))],
        out_specs=pl.BlockSpec((J, tm, W), lambda i: (0, i, 0)),
        out_shape=jax.ShapeDtypeStruct((J, T, W), F32),
        compiler_params=_params(1),
    )(x, gain, win)


def mixer_constants(S):
    B = RET_BLOCK
    half = RET_DK // 2
    freqs = ROPE_BASE ** (-jnp.arange(half, dtype=F32) * 2.0 / RET_DK)
    ang = jnp.arange(S, dtype=F32)[:, None] * freqs[None, :]
    cos = jnp.tile(jnp.cos(ang), (1, 2 * RET_HEADS))
    sin = jnp.tile(jnp.concatenate([-jnp.sin(ang), jnp.sin(ang)], axis=1), (1, RET_HEADS))
    gamma = 1.0 - 2.0 ** (-5.0 - jnp.arange(RET_HEADS, dtype=F32))
    log_g = jnp.log(gamma)
    idx = jnp.arange(B, dtype=F32)
    ci = jnp.arange(B) // CHUNK
    dist = jnp.abs(idx[:, None] - idx[None, :])
    m_intra = jnp.exp(log_g[:, None, None] * dist[None]) * (ci[None, :] <= ci[:, None])[None].astype(F32)
    lg_lane = jnp.repeat(log_g, RET_DK)
    d_q = jnp.exp(lg_lane[None, :] * (idx[:, None] + 1.0))
    d_k = jnp.exp(lg_lane[None, :] * (B - 1.0 - idx[:, None]))
    d_s = jnp.broadcast_to(jnp.exp(lg_lane * B)[:, None], (QK_W, V_W))
    bm = (jnp.arange(QK_W)[:, None] // RET_DK == jnp.arange(V_W)[None, :] // RET_DV).astype(F32)
    win = jnp.repeat(jnp.array(POOL_WINDOWS, F32), POOL_GC)
    invc = 1.0 / jnp.minimum(jnp.arange(S, dtype=F32)[:, None] + 1.0, win[None, :])
    return dict(cos=cos, sin=sin, m=m_intra, dq=d_q, dk=d_k, ds=d_s, bm=bm, invc=invc)


def _swap_halves(x):
    w = x.shape[1]
    lane = lax.broadcasted_iota(jnp.int32, x.shape, 1)
    return jnp.where((lane % RET_DK) < RET_DK // 2, pltpu.roll(x, w - RET_DK // 2, 1), pltpu.roll(x, RET_DK // 2, 1))


def _rot(x, cos, sin):
    return x * cos + _swap_halves(x) * sin


def _rot_t(d, cos, sin):
    return d * cos + _swap_halves(d * sin)


def _lane_groups(parts):
    return jnp.concatenate([p[:, POOL_GC * g:POOL_GC * (g + 1)] for g, p in enumerate(parts)], axis=1)


def _head_mask(shape, h):
    lane = lax.broadcasted_iota(jnp.int32, shape, 1)
    return (lane // RET_DK) == h


def _group_norm(o):
    yh, rs = [], []
    for h in range(RET_HEADS):
        oh = o[:, RET_DV * h:RET_DV * (h + 1)]
        xc = oh - jnp.mean(oh, axis=-1, keepdims=True)
        r = lax.rsqrt(jnp.mean(xc * xc, axis=-1, keepdims=True) + GN_EPS)
        yh.append(xc * r)
        rs.append(r)
    return yh, rs


def mixer_fwd(p, x1, consts, gn_gain, pool_w, pool_scale, wout, n_seq, S, name):
    _, T, _ = p.shape
    D = x1.shape[1]
    B = RET_BLOCK
    nb = S // B

    def body(p_ref, x1_ref, cos_ref, sin_ref, invc_ref, m_ref, dq_ref, dk_ref, ds_ref, bm_ref,
             gain_ref, pw_ref, sc_ref, wout_ref,
             x2_ref, m_out, o_out, st_out, pooled_out, state, prev_u):
        blk = pl.program_id(1)

        @pl.when(blk == 0)
        def _():
            state[...] = jnp.zeros_like(state)
            prev_u[...] = jnp.zeros_like(prev_u)

        qk = p_ref[0]
        v = p_ref[1]
        g = p_ref[2]
        u = p_ref[3]
        cos = cos_ref[...]
        sin = sin_ref[...]
        qr = _rot(qk[:, :QK_W], cos, sin) * (RET_DK ** -0.5)
        kr = _rot(qk[:, QK_W:], cos, sin)
        qb = qr.astype(BF16)
        kb = kr.astype(BF16)
        vb = v.astype(BF16)
        st = state[...]
        st_out[...] = st
        cross = _dot((qr * dq_ref[...]).astype(BF16), st.astype(BF16))
        o_parts = []
        for h in range(RET_HEADS):
            qm = jnp.where(_head_mask(qb.shape, h), qb, jnp.zeros_like(qb))
            sc = (_dot(qm, kb, NT) * m_ref[h]).astype(BF16)
            o_parts.append(_dot(sc, vb[:, RET_DV * h:RET_DV * (h + 1)]) + cross[:, RET_DV * h:RET_DV * (h + 1)])
        o = jnp.concatenate(o_parts, axis=1)
        o_out[...] = o
        kv = _dot((kr * dk_ref[...]).astype(BF16), vb, TN)
        state[...] = st * ds_ref[...] + kv * bm_ref[...]

        yh, _ = _group_norm(o)
        r = g * _sigmoid(g) * (jnp.concatenate(yh, axis=1) * gain_ref[...])

        ext = jnp.concatenate([prev_u[...], u], axis=0)
        sums = []
        run = ext
        for k in (1, 2, 4, 8):
            run = run + pltpu.roll(run, k, 0)
            sums.append(run[B:, :])
        prev_u[...] = u
        pooled = (_lane_groups(sums) * invc_ref[...] - u).astype(BF16)
        pooled_out[...] = pooled
        yp = [_dot(pooled[:, POOL_GC * gi:POOL_GC * (gi + 1)], pw_ref[gi].astype(BF16)) for gi in range(len(POOL_WINDOWS))]
        s = jnp.concatenate(yp, axis=1) * sc_ref[...]
        m = jnp.concatenate([r, s], axis=1).astype(BF16)
        m_out[...] = m
        x2_ref[...] = x1_ref[...] + _dot(m, wout_ref[...])

    tokmap = lambda b, k: (b * nb + k, 0)
    seqmap = lambda b, k: (k, 0)
    const2 = lambda b, k: (0, 0)
    const3 = lambda b, k: (0, 0, 0)
    return pl.pallas_call(
        body, name=name, grid=(n_seq, nb),
        in_specs=[pl.BlockSpec((4, B, V_W), lambda b, k: (0, b * nb + k, 0)),
                  pl.BlockSpec((B, D), tokmap),
                  pl.BlockSpec((B, QK_W), seqmap), pl.BlockSpec((B, QK_W), seqmap), pl.BlockSpec((B, V_W), seqmap),
                  pl.BlockSpec((RET_HEADS, B, B), const3),
                  pl.BlockSpec((B, QK_W), const2), pl.BlockSpec((B, QK_W), const2),
                  pl.BlockSpec((QK_W, V_W), const2), pl.BlockSpec((QK_W, V_W), const2),
                  pl.BlockSpec((1, V_W), const2), pl.BlockSpec((4, POOL_GC, POOL_GC), const3),
                  pl.BlockSpec((1, V_W), const2), pl.BlockSpec((D, D), const2)],
        out_specs=[pl.BlockSpec((B, D), tokmap), pl.BlockSpec((B, D), tokmap), pl.BlockSpec((B, V_W), tokmap),
                   pl.BlockSpec((None, QK_W, V_W), lambda b, k: (b * nb + k, 0, 0)),
                   pl.BlockSpec((B, V_W), tokmap)],
        out_shape=[jax.ShapeDtypeStruct((T, D), F32), jax.ShapeDtypeStruct((T, D), BF16),
                   jax.ShapeDtypeStruct((T, V_W), F32), jax.ShapeDtypeStruct((T // B, QK_W, V_W), F32),
                   jax.ShapeDtypeStruct((T, V_W), BF16)],
        scratch_shapes=[pltpu.VMEM((QK_W, V_W), F32), pltpu.VMEM((B, V_W), F32)],
        compiler_params=_params(2),
    )(p, x1, consts["cos"], consts["sin"], consts["invc"], consts["m"], consts["dq"], consts["dk"],
      consts["ds"], consts["bm"], gn_gain, pool_w, pool_scale, wout)


def mixer_bwd(dx3, dn3, x2, g3, x1, p, o_s, st_s, pooled_s, consts, gmix, gn_gain, pool_w, pool_scale, wout, win,
              n_seq, S, name, plan=None):
    T, D = x1.shape
    B = RET_BLOCK
    nb = S // B
    NG = len(POOL_WINDOWS)

    def body(dx3_ref, dn3_ref, x2_ref, g3_ref, x1_ref, p_ref, o_ref, st_ref, pooled_ref, cos_ref, sin_ref, invc_ref,
             m_ref, dq_ref, dk_ref, ds_ref, bm_ref, gmix_ref, gain_ref, pw_ref, sc_ref, wout_ref, win_ref,
             dx1_ref, df1_ref, dx2_ref, dp_ref, n2_ref, dg3_ref, dgmix_ref, dgain_ref, dscale_ref, dpw_ref,
             gstate, next_e):
        b = pl.program_id(0)
        blk = pl.program_id(1)

        @pl.when(jnp.logical_and(b == 0, blk == 0))
        def _():
            dg3_ref[...] = jnp.zeros_like(dg3_ref)
            dgmix_ref[...] = jnp.zeros_like(dgmix_ref)
            dgain_ref[...] = jnp.zeros_like(dgain_ref)
            dscale_ref[...] = jnp.zeros_like(dscale_ref)
            dpw_ref[...] = jnp.zeros_like(dpw_ref)

        @pl.when(blk == 0)
        def _():
            gstate[...] = jnp.zeros_like(gstate)
            next_e[...] = jnp.zeros_like(next_e)

        r2, xhat2 = _rms_parts(x2_ref[...])
        dn3 = dn3_ref[...]
        dg3_ref[...] += jnp.sum(dn3 * xhat2, axis=0, keepdims=True)
        dx2v = _rms_bwd(dn3, xhat2, r2, g3_ref[...], dx3_ref[...])
        dx2b = dx2v.astype(BF16)
        dx2_ref[...] = dx2b
        dm = _dot(dx2b, wout_ref[...], NT)
        dr = dm[:, :V_W]
        dsv = dm[:, V_W:]

        pooled = pooled_ref[...]
        scale = sc_ref[...]
        dyp = (dsv * scale).astype(BF16)
        yp, dpl = [], []
        for gi in range(NG):
            sl = slice(POOL_GC * gi, POOL_GC * (gi + 1))
            pwb = pw_ref[gi].astype(BF16)
            yp.append(_dot(pooled[:, sl], pwb))
            dpw_ref[gi] += _dot(pooled[:, sl], dyp[:, sl], TN)
            dpl.append(_dot(dyp[:, sl], pwb, NT))
        dscale_ref[...] += jnp.sum(dsv * jnp.concatenate(yp, axis=1), axis=0, keepdims=True)
        dpooled = jnp.concatenate(dpl, axis=1)
        e = dpooled * invc_ref[...]
        ext = jnp.concatenate([e, next_e[...]], axis=0)
        sums = []
        run = ext
        for k in (1, 2, 4, 8):
            run = run + pltpu.roll(run, 2 * B - k, 0)
            sums.append(run[:B, :])
        next_e[...] = e
        du = _lane_groups(sums) - dpooled

        qk = p_ref[0]
        v = p_ref[1]
        g = p_ref[2]
        sg = _sigmoid(g)
        si = g * sg
        yh, rs = _group_norm(o_ref[...])
        yhat = jnp.concatenate(yh, axis=1)
        gain = gain_ref[...]
        dg = dr * (yhat * gain) * (sg * (1.0 + g * (1.0 - sg)))
        dy = dr * si
        dgain_ref[...] += jnp.sum(dy * yhat, axis=0, keepdims=True)
        dyh = dy * gain
        do_parts = []
        for h in range(RET_HEADS):
            sl = slice(RET_DV * h, RET_DV * (h + 1))
            dh_ = dyh[:, sl]
            m1 = jnp.mean(dh_, axis=-1, keepdims=True)
            m2 = jnp.mean(dh_ * yh[h], axis=-1, keepdims=True)
            do_parts.append(rs[h] * (dh_ - m1 - yh[h] * m2))
        dob = jnp.concatenate(do_parts, axis=1).astype(BF16)

        cos = cos_ref[...]
        sin = sin_ref[...]
        qr = _rot(qk[:, :QK_W], cos, sin) * (RET_DK ** -0.5)
        kr = _rot(qk[:, QK_W:], cos, sin)
        qb = qr.astype(BF16)
        kb = kr.astype(BF16)
        vb = v.astype(BF16)
        dqd = dq_ref[...]
        dkd = dk_ref[...]
        stb = st_ref[...].astype(BF16)
        gs = gstate[...]
        gb = gs.astype(BF16)
        dqs = _dot(dob, stb, NT) * dqd
        dkr = _dot(vb, gb, NT) * dkd
        dv_cross = _dot((kr * dkd).astype(BF16), gb)
        ds_cross = _dot((qr * dqd).astype(BF16), dob, TN) * bm_ref[...]
        gstate[...] = ds_cross + gs * ds_ref[...]
        dv_parts = []
        for h in range(RET_HEADS):
            sl = slice(RET_DV * h, RET_DV * (h + 1))
            hm = _head_mask(qb.shape, h)
            qm = jnp.where(hm, qb, jnp.zeros_like(qb))
            mh = m_ref[h]
            sc = (_dot(qm, kb, NT) * mh).astype(BF16)
            dsc = (_dot(dob[:, sl], vb[:, sl], NT) * mh).astype(BF16)
            dqs = dqs + jnp.where(hm, _dot(dsc, kb), 0.0)
            dkr = dkr + jnp.where(hm, _dot(dsc, qb, TN), 0.0)
            dv_parts.append(_dot(sc, dob[:, sl], TN) + dv_cross[:, sl])
        dq = _rot_t(dqs * (RET_DK ** -0.5), cos, sin)
        dk = _rot_t(dkr, cos, sin)
        dp = [jnp.concatenate([dq, dk], axis=1).astype(BF16), jnp.concatenate(dv_parts, axis=1).astype(BF16),
              dg.astype(BF16), du.astype(BF16)]
        dn = jnp.zeros((B, D), F32)
        for jj in range(4):
            dp_ref[jj] = dp[jj]
            dn = dn + _dot(dp[jj], win_ref[jj], NT)

        x1v = x1_ref[...]
        r, xhat = _rms_parts(x1v)
        gm = gmix_ref[...]
        n2_ref[...] = (xhat * gm).astype(BF16)
        dgmix_ref[...] += jnp.sum(dn * xhat, axis=0, keepdims=True)
        dx1 = _rms_bwd(dn, xhat, r, gm, dx2v)
        dx1_ref[...] = dx1
        df1_ref[...] = (0.5 * dx1).astype(BF16)

    rev = lambda b, k: (b * nb + (nb - 1 - k), 0)
    seqrev = lambda b, k: (nb - 1 - k, 0)
    const2 = lambda b, k: (0, 0)
    const3 = lambda b, k: (0, 0, 0)
    return _call(
        body, plan, name=name, grid=(n_seq, nb),
        in_specs=[pl.BlockSpec((B, D), rev), pl.BlockSpec((B, D), rev), pl.BlockSpec((B, D), rev),
                  pl.BlockSpec((1, D), const2), pl.BlockSpec((B, D), rev),
                  pl.BlockSpec((4, B, V_W), lambda b, k: (0, b * nb + (nb - 1 - k), 0)),
                  pl.BlockSpec((B, V_W), rev),
                  pl.BlockSpec((None, QK_W, V_W), lambda b, k: (b * nb + (nb - 1 - k), 0, 0)),
                  pl.BlockSpec((B, V_W), rev),
                  pl.BlockSpec((B, QK_W), seqrev), pl.BlockSpec((B, QK_W), seqrev), pl.BlockSpec((B, V_W), seqrev),
                  pl.BlockSpec((RET_HEADS, B, B), const3),
                  pl.BlockSpec((B, QK_W), const2), pl.BlockSpec((B, QK_W), const2),
                  pl.BlockSpec((QK_W, V_W), const2), pl.BlockSpec((QK_W, V_W), const2),
                  pl.BlockSpec((1, D), const2), pl.BlockSpec((1, V_W), const2),
                  pl.BlockSpec((NG, POOL_GC, POOL_GC), const3), pl.BlockSpec((1, V_W), const2),
                  pl.BlockSpec((D, D), const2), pl.BlockSpec((4, D, V_W), const3)],
        out_specs=[pl.BlockSpec((B, D), rev), pl.BlockSpec((B, D), rev), pl.BlockSpec((B, D), rev),
                   pl.BlockSpec((4, B, V_W), lambda b, k: (0, b * nb + (nb - 1 - k), 0)),
                   pl.BlockSpec((B, D), rev),
                   pl.BlockSpec((1, D), const2), pl.BlockSpec((1, D), const2), pl.BlockSpec((1, V_W), const2),
                   pl.BlockSpec((1, V_W), const2), pl.BlockSpec((NG, POOL_GC, POOL_GC), const3)],
        out_shape=[jax.ShapeDtypeStruct((T, D), F32), jax.ShapeDtypeStruct((T, D), BF16),
                   jax.ShapeDtypeStruct((T, D), BF16), jax.ShapeDtypeStruct((4, T, V_W), BF16),
                   jax.ShapeDtypeStruct((T, D), BF16),
                   jax.ShapeDtypeStruct((1, D), F32), jax.ShapeDtypeStruct((1, D), F32), jax.ShapeDtypeStruct((1, V_W), F32),
                   jax.ShapeDtypeStruct((1, V_W), F32), jax.ShapeDtypeStruct((NG, POOL_GC, POOL_GC), F32)],
        scratch_shapes=[pltpu.VMEM((QK_W, V_W), F32), pltpu.VMEM((B, V_W), F32)],
        args=(dx3, dn3, x2, g3, x1, p, o_s, st_s, pooled_s, consts["cos"], consts["sin"], consts["invc"], consts["m"], consts["dq"],
              consts["dk"], consts["ds"], consts["bm"], gmix, gn_gain, pool_w, pool_scale, wout, win))


def loss_bwd(x3, tgt, gain, tm, name):
    T, D = x3.shape

    def body(x_ref, t_ref, g_ref, dx_ref, df_ref, lacc_ref, dg_ref):
        @pl.when(pl.program_id(0) == 0)
        def _():
            lacc_ref[...] = jnp.zeros_like(lacc_ref)
            dg_ref[...] = jnp.zeros_like(dg_ref)

        r, xhat = _rms_parts(x_ref[...])
        gv = g_ref[...]
        err = xhat * gv - t_ref[...]
        lacc_ref[...] += jnp.sum(err * err, axis=0, keepdims=True)
        dy = err * (1.0 / D)
        dg_ref[...] += jnp.sum(dy * xhat, axis=0, keepdims=True)
        dx = _rms_bwd(dy, xhat, r, gv, 0.0)
        dx_ref[...] = dx
        df_ref[...] = (0.5 * dx).astype(BF16)

    tok = pl.BlockSpec((tm, D), lambda i: (i, 0))
    vec = pl.BlockSpec((1, D), lambda i: (0, 0))
    return pl.pallas_call(
        body, name=name, grid=(T // tm,),
        in_specs=[tok, tok, vec], out_specs=[tok, tok, vec, vec],
        out_shape=[jax.ShapeDtypeStruct((T, D), F32), jax.ShapeDtypeStruct((T, D), BF16),
                   jax.ShapeDtypeStruct((1, D), F32), jax.ShapeDtypeStruct((1, D), F32)],
        compiler_params=_params(1),
    )(x3, tgt, gain)


def _adamw_math(w, g, m, v):
    m2 = ADAM_B1 * m + (1.0 - ADAM_B1) * g
    v2 = ADAM_B2 * v + (1.0 - ADAM_B2) * (g * g)
    m_hat = m2 / (1.0 - ADAM_B1 ** ADAM_STEP)
    v_hat = v2 / (1.0 - ADAM_B2 ** ADAM_STEP)
    return -ADAM_LR * (m_hat / (jnp.sqrt(v_hat) + ADAM_EPS) + ADAM_WD * w), m2, v2


def _place():
    x, y, c = lax.axis_index("x"), lax.axis_index("y"), lax.axis_index("c")
    other_chips = [(1 - x, y), (x, 1 - y), (1 - x, 1 - y)]
    return x, y, c, other_chips


def _exchange_plan(ins, out_shape, copies, n_copies):
    def descriptors(pins, pouts, psems):
        send, recv = psems
        return [pltpu.make_async_remote_copy(src_ref=s, dst_ref=d, send_sem=send.at[i], recv_sem=recv.at[i],
                                             device_id=dev, device_id_type=MESH)
                for i, (s, d, dev) in enumerate(copies(pins, pouts))]

    def start(pins, pouts, psems):
        for cp in descriptors(pins, pouts, psems):
            cp.start()

    def finish(pins, pouts, psems):
        for cp in descriptors(pins, pouts, psems):
            cp.wait()

    return CommPlan(tuple(ins), tuple(out_shape),
                    (pltpu.SemaphoreType.DMA((n_copies,)), pltpu.SemaphoreType.DMA((n_copies,))), start, finish)


def _combine(a, b):
    assert a.mid is None and b.mid is None
    ni, no, ns = len(a.ins), len(a.out_shape), len(a.sems)

    def start(pins, pouts, psems):
        a.start(pins[:ni], pouts[:no], psems[:ns])
        b.start(pins[ni:], pouts[no:], psems[ns:])

    def finish(pins, pouts, psems):
        a.finish(pins[:ni], pouts[:no], psems[:ns])
        b.finish(pins[ni:], pouts[no:], psems[ns:])

    return CommPlan(a.ins + b.ins, a.out_shape + b.out_shape, a.sems + b.sems, start, finish)


def gather_plan(shards):
    n = len(shards)

    def start(pins, pouts, psems):
        x, y, c, chips = _place()
        mine = 2 * x + y
        for w in range(n):
            for k, (px, py) in enumerate(chips):
                pltpu.make_async_remote_copy(
                    src_ref=pins[w].at[:, c], dst_ref=pouts[w].at[:, mine, c],
                    send_sem=psems[0].at[w, k], recv_sem=psems[1].at[w, k],
                    device_id=(px, py, c), device_id_type=MESH).start()

    def mid(pins, pouts, psems):
        x, y, c, chips = _place()
        for w in range(n):
            for k, (px, py) in enumerate(chips):
                landed = pouts[w].at[:, 2 * px + py, c]
                pltpu.make_async_remote_copy(
                    src_ref=landed, dst_ref=landed, send_sem=psems[0].at[w, k], recv_sem=psems[1].at[w, k],
                    device_id=(px, py, c), device_id_type=MESH).wait_recv()
                pltpu.make_async_remote_copy(
                    src_ref=landed, dst_ref=landed, send_sem=psems[2].at[w, k], recv_sem=psems[3].at[w, k],
                    device_id=(x, y, 1 - c), device_id_type=MESH).start()

    def finish(pins, pouts, psems):
        x, y, c, chips = _place()
        mine = 2 * x + y
        for w in range(n):
            for k, (px, py) in enumerate(chips):
                landed = pouts[w].at[:, 2 * px + py, 1 - c]
                cp = pltpu.make_async_remote_copy(
                    src_ref=landed, dst_ref=landed, send_sem=psems[2].at[w, k], recv_sem=psems[3].at[w, k],
                    device_id=(x, y, 1 - c), device_id_type=MESH)
                cp.wait_recv()
                cp.wait_send()
                pltpu.make_async_remote_copy(
                    src_ref=pins[w].at[:, c], dst_ref=pouts[w].at[:, mine, c],
                    send_sem=psems[0].at[w, k], recv_sem=psems[1].at[w, k],
                    device_id=(px, py, c), device_id_type=MESH).wait_send()

    return CommPlan(tuple(shards),
                    tuple(jax.ShapeDtypeStruct((s.shape[0], N_CHIPS) + s.shape[1:], s.dtype) for s in shards),
                    tuple(pltpu.SemaphoreType.DMA((n, 3)) for _ in range(4)), start, finish, mid)


def place_own(gathered, shard, name):
    L, _, Rh, C = shard.shape

    def body(chip_ref, g_ref, s_ref, o_ref):
        o_ref[...] = s_ref[...]

    return pl.pallas_call(
        body, name=name,
        grid_spec=pltpu.PrefetchScalarGridSpec(
            num_scalar_prefetch=1, grid=(L,),
            in_specs=[ANY, pl.BlockSpec((None, 2, Rh, C), lambda l, chip_ref: (l, 0, 0, 0))],
            out_specs=pl.BlockSpec((None, None, 2, Rh, C), lambda l, chip_ref: (l, chip_ref[0], 0, 0, 0))),
        out_shape=jax.ShapeDtypeStruct(gathered.shape, gathered.dtype),
        input_output_aliases={1: 0},
        compiler_params=_params(1),
    )(_chip_index(), gathered, shard)


def gather_weights(shards):
    n = len(shards)
    units = [(w, l) for w in range(n) for l in range(shards[w].shape[0])]
    nu = len(units)

    def body(*refs):
        ins, outs, bufs = refs[:n], refs[n:2 * n], refs[2 * n:3 * n]
        ld_sem, st_sem, own_send, own_recv, fwd_send, fwd_recv, d2d_send, d2d_recv = refs[3 * n:]
        x, y, c, _ = _place()
        sibling = (x, y, 1 - c)
        mine = 2 * x + y
        first = (x ^ (1 - c), y ^ c)
        second = (x ^ c, y ^ (1 - c))
        diagonal = (1 - x, 1 - y)

        def slot(chip):
            return 2 * chip[0] + chip[1]

        def remote(src, dst, send, recv, device):
            return pltpu.make_async_remote_copy(src_ref=src, dst_ref=dst, send_sem=send, recv_sem=recv,
                                                device_id=device, device_id_type=MESH)

        loads = [pltpu.make_async_copy(ins[w], bufs[w], ld_sem.at[w]) for w in range(n)]
        for cp in loads:
            cp.start()
        stores, sends = [], []
        for w in range(n):
            loads[w].wait()
            cp = pltpu.make_async_copy(bufs[w], outs[w].at[:, mine], st_sem.at[w])
            cp.start()
            stores.append(cp)
        for u, (w, l) in enumerate(units):
            for k, chip in enumerate((first, second)):
                cp = remote(bufs[w].at[l, c], outs[w].at[l, mine, c], own_send.at[u, k], own_recv.at[u, k], (*chip, c))
                cp.start()
                sends.append(cp)
        for u, (w, l) in enumerate(units):
            got = outs[w].at[l, slot(first), c]
            remote(got, got, own_send.at[u, 0], own_recv.at[u, 0], (*first, c)).wait_recv()
            for cp in (remote(got, got, fwd_send.at[u], fwd_recv.at[u], (*second, c)),
                       remote(got, got, d2d_send.at[u, 0], d2d_recv.at[u, 0], sibling)):
                cp.start()
                sends.append(cp)
        for u, (w, l) in enumerate(units):
            got = outs[w].at[l, slot(second), c]
            remote(got, got, own_send.at[u, 1], own_recv.at[u, 1], (*second, c)).wait_recv()
            cp = remote(got, got, d2d_send.at[u, 1], d2d_recv.at[u, 1], sibling)
            cp.start()
            sends.append(cp)
        for u, (w, l) in enumerate(units):
            got = outs[w].at[l, slot(diagonal), c]
            remote(got, got, fwd_send.at[u], fwd_recv.at[u], (*second, c)).wait_recv()
            cp = remote(got, got, d2d_send.at[u, 2], d2d_recv.at[u, 2], sibling)
            cp.start()
            sends.append(cp)
        for u, (w, l) in enumerate(units):
            for k, chip in enumerate((second, first, diagonal)):
                got = outs[w].at[l, slot(chip), 1 - c]
                remote(got, got, d2d_send.at[u, k], d2d_recv.at[u, k], sibling).wait_recv()
        for cp in sends:
            cp.wait_send()
        for cp in stores:
            cp.wait()

    return pl.pallas_call(
        body, name="gather_weights",
        in_specs=[ANY] * n, out_specs=[ANY] * n,
        out_shape=[jax.ShapeDtypeStruct((s.shape[0], N_CHIPS) + s.shape[1:], s.dtype) for s in shards],
        scratch_shapes=[pltpu.VMEM(s.shape, s.dtype) for s in shards] +
                       [pltpu.SemaphoreType.DMA((n,)), pltpu.SemaphoreType.DMA((n,)),
                        pltpu.SemaphoreType.DMA((nu, 2)), pltpu.SemaphoreType.DMA((nu, 2)),
                        pltpu.SemaphoreType.DMA((nu,)), pltpu.SemaphoreType.DMA((nu,)),
                        pltpu.SemaphoreType.DMA((nu, 3)), pltpu.SemaphoreType.DMA((nu, 3))],
        compiler_params=pltpu.CompilerParams(vmem_limit_bytes=VMEM_LIMIT),
    )(*shards)


def send_to_sibling_other_half(grads):
    def copies(ins, outs):
        x, y, c, _ = _place()
        return [(ins[w].at[:, :, 1 - c], outs[w], (x, y, 1 - c)) for w in range(len(grads))]

    return _exchange_plan(grads, [jax.ShapeDtypeStruct(g.shape[:2] + g.shape[3:], g.dtype) for g in grads], copies,
                          len(grads))


def _core_index():
    return jnp.reshape(lax.axis_index("c"), (1,)).astype(jnp.int32)


def _chip_index():
    return jnp.reshape(2 * lax.axis_index("x") + lax.axis_index("y"), (1,)).astype(jnp.int32)


def add_own_half(g, recv, name):
    L, J, _, Rh, C = g.shape

    def body(c_ref, g_ref, r_ref, o_ref):
        o_ref[...] = (g_ref[...] + r_ref[...]).astype(BF16)

    return pl.pallas_call(
        body, name=name,
        grid_spec=pltpu.PrefetchScalarGridSpec(
            num_scalar_prefetch=1, grid=(L, J),
            in_specs=[pl.BlockSpec((None, None, None, Rh, C), lambda l, j, c_ref: (l, j, c_ref[0], 0, 0)),
                      pl.BlockSpec((None, None, Rh, C), lambda l, j, c_ref: (l, j, 0, 0))],
            out_specs=pl.BlockSpec((None, None, Rh, C), lambda l, j, c_ref: (l, j, 0, 0))),
        out_shape=jax.ShapeDtypeStruct((L, J, Rh, C), BF16),
        compiler_params=_params(2),
    )(_core_index(), g, recv)


def exchange_between_chips(sums):
    def copies(ins, outs):
        x, y, c, chips = _place()
        return [(ins[w].at[:, 2 * px + py], outs[w].at[k], (px, py, c))
                for w in range(len(sums)) for k, (px, py) in enumerate(chips)]

    return _exchange_plan(sums, [jax.ShapeDtypeStruct((3, s.shape[0]) + s.shape[2:], s.dtype) for s in sums], copies,
                          3 * len(sums))


def sum_chips(own, recv, name):
    L, _, Rh, C = own.shape

    def body(chip_ref, o_ref, r_ref, out_ref):
        acc = o_ref[...].astype(F32)
        for k in range(3):
            acc = acc + r_ref[k].astype(F32)
        out_ref[...] = acc

    return pl.pallas_call(
        body, name=name,
        grid_spec=pltpu.PrefetchScalarGridSpec(
            num_scalar_prefetch=1, grid=(L,),
            in_specs=[pl.BlockSpec((None, None, Rh, C), lambda l, chip_ref: (l, chip_ref[0], 0, 0)),
                      pl.BlockSpec((3, None, Rh, C), lambda l, chip_ref: (0, l, 0, 0))],
            out_specs=pl.BlockSpec((None, Rh, C), lambda l, chip_ref: (l, 0, 0))),
        out_shape=jax.ShapeDtypeStruct((L, Rh, C), F32),
        compiler_params=_params(1),
    )(_chip_index(), own, recv)


def share_with_sibling(halves):
    def copies(ins, outs):
        x, y, c, _ = _place()
        return [(ins[w], outs[w], (x, y, 1 - c)) for w in range(len(halves))]

    return _exchange_plan(halves, [jax.ShapeDtypeStruct(h.shape, h.dtype) for h in halves], copies, len(halves))


def adamw_group(ws, ms, vs, own, sib, name, plan=None):
    L = len(ws)
    R, C = ws[0].shape
    Rh = R // 2
    tr = Rh // 2
    nr = Rh // tr

    def body(*refs):
        w, m, v = refs[:L], refs[L:2 * L], refs[2 * L:3 * L]
        own_ref, sib_ref = refs[3 * L], refs[3 * L + 1]
        outs = refs[3 * L + 2:]
        mine = pl.program_id(0) == lax.axis_index("c")
        for l in range(L):
            gv = jnp.where(mine, own_ref[l], sib_ref[l])
            d, m2, v2 = _adamw_math(w[l][...], gv, m[l][...], v[l][...])
            outs[4 * l][...] = d
            outs[4 * l + 1][...] = m2
            outs[4 * l + 2][...] = v2
            outs[4 * l + 3][...] = gv

    blk = pl.BlockSpec((tr, C), lambda h, r: (h * nr + r, 0))
    half = pl.BlockSpec((L, tr, C), lambda h, r: (0, r, 0))
    sds = jax.ShapeDtypeStruct((R, C), F32)
    return _call(body, plan, name=name, grid=(2, nr), in_specs=[blk] * (3 * L) + [half, half],
                 out_specs=[blk] * (4 * L), out_shape=[sds] * (4 * L), args=(*ws, *ms, *vs, own, sib))


def gather_small_plan(part):
    def copies(ins, outs):
        x, y, c, _ = _place()
        me = 4 * x + 2 * y + c
        return [(ins[0], outs[0].at[me], (x ^ ((k >> 2) & 1), y ^ ((k >> 1) & 1), c ^ (k & 1))) for k in range(1, N_DEV)]

    return _exchange_plan([part], [jax.ShapeDtypeStruct((N_DEV,) + part.shape, part.dtype)], copies, N_DEV - 1)


def sum_small(part, gathered, name):
    R, C = part.shape

    def body(p_ref, g_ref, o_ref):
        d = pl.program_id(0)
        me = 4 * lax.axis_index("x") + 2 * lax.axis_index("y") + lax.axis_index("c")
        val = jnp.where(d == me, p_ref[...], g_ref[...])

        @pl.when(d == 0)
        def _():
            o_ref[...] = val

        @pl.when(d > 0)
        def _():
            o_ref[...] += val

    return pl.pallas_call(
        body, name=name, grid=(N_DEV,),
        in_specs=[pl.BlockSpec((R, C), lambda d: (0, 0)), pl.BlockSpec((None, R, C), lambda d: (d, 0, 0))],
        out_specs=pl.BlockSpec((R, C), lambda d: (0, 0)),
        out_shape=jax.ShapeDtypeStruct((R, C), F32),
        compiler_params=_params(1),
    )(part, gathered)


SMALL = ("norm_ffn1", "norm_mix", "norm_ffn2", "norm_final", "ret_gn_gain", "pool_scale", "pool_w")
GROUPS = (("ffn1", ("ffn1_gate", "ffn1_up", "ffn1_down")), ("w_in", ("w_in",)), ("w_out", ("w_out",)),
          ("ffn2", ("ffn2_gate", "ffn2_up", "ffn2_down")))
TRANSPOSED = ("ffn1_gate", "ffn1_up", "ffn2_gate", "ffn2_up")
IN_W = 2 * QK_W + 3 * V_W
ORDER = ("norm_ffn1", "ffn1_gate", "ffn1_up", "ffn1_down", "norm_mix", "w_in", "ret_gn_gain", "pool_w", "pool_scale",
         "w_out", "norm_ffn2", "ffn2_gate", "ffn2_up", "ffn2_down", "norm_final")


SUBLANES = 8


def _small_rows(shapes):
    rows = [math.prod(shapes[k]) // 128 for k in SMALL]
    offs, off = [], 0
    for r in rows:
        offs.append(off)
        off += -(-r // SUBLANES) * SUBLANES
    return rows, offs, off


def _pack_small(d, shapes):
    rows, offs, _ = _small_rows(shapes)
    pieces = []
    for k, r in zip(SMALL[1:], rows[1:]):
        pieces.append(d[k].reshape(r, 128))
        if r % SUBLANES:
            pieces.append(jnp.zeros((SUBLANES - r % SUBLANES, 128), F32))
    return jnp.concatenate(pieces, axis=0)


def adamw_small(W, M, V, first, summed, shapes):
    rows, offs, _ = _small_rows(shapes)
    n = len(SMALL)

    def body(*refs):
        w, m, v, f_ref, s_ref = refs[:n], refs[n:2 * n], refs[2 * n:3 * n], refs[3 * n], refs[3 * n + 1]
        outs = refs[3 * n + 2:]
        for i, (r, off) in enumerate(zip(rows, offs)):
            gv = f_ref[...] if i == 0 else s_ref[off - offs[1]:off - offs[1] + r, :]
            d, m2, v2 = _adamw_math(w[i][...], gv, m[i][...], v[i][...])
            outs[4 * i][...] = d
            outs[4 * i + 1][...] = m2
            outs[4 * i + 2][...] = v2
            outs[4 * i + 3][...] = gv

    vm = pl.BlockSpec(memory_space=pltpu.VMEM)
    flat = lambda d: [d[k].reshape(r, 128) for k, r in zip(SMALL, rows)]
    res = pl.pallas_call(
        body, name="adamw_small", in_specs=[vm] * (3 * n + 2), out_specs=[vm] * (4 * n),
        out_shape=[jax.ShapeDtypeStruct((r, 128), F32) for r in rows for _ in range(4)],
    )(*flat(W), *flat(M), *flat(V), first, summed)
    return {k: [a.reshape(shapes[k]) for a in res[4 * i:4 * i + 4]] for i, k in enumerate(SMALL)}


def kernel(x, norm_ffn1, ffn1_gate, ffn1_up, ffn1_down, norm_mix, w_in, ret_gn_gain, pool_w, pool_scale, w_out, norm_ffn2, ffn2_gate, ffn2_up, ffn2_down, norm_final, loss_target, m_norm_ffn1, m_ffn1_gate, m_ffn1_up, m_ffn1_down, m_norm_mix, m_w_in, m_ret_gn_gain, m_pool_w, m_pool_scale, m_w_out, m_norm_ffn2, m_ffn2_gate, m_ffn2_up, m_ffn2_down, m_norm_final, v_norm_ffn1, v_ffn1_gate, v_ffn1_up, v_ffn1_down, v_norm_mix, v_w_in, v_ret_gn_gain, v_pool_w, v_pool_scale, v_w_out, v_norm_ffn2, v_ffn2_gate, v_ffn2_up, v_ffn2_down, v_norm_final):
    W = dict(norm_ffn1=norm_ffn1, ffn1_gate=ffn1_gate, ffn1_up=ffn1_up, ffn1_down=ffn1_down, norm_mix=norm_mix, w_in=w_in,
             ret_gn_gain=ret_gn_gain, pool_w=pool_w, pool_scale=pool_scale, w_out=w_out, norm_ffn2=norm_ffn2,
             ffn2_gate=ffn2_gate, ffn2_up=ffn2_up, ffn2_down=ffn2_down, norm_final=norm_final)
    M = dict(norm_ffn1=m_norm_ffn1, ffn1_gate=m_ffn1_gate, ffn1_up=m_ffn1_up, ffn1_down=m_ffn1_down, norm_mix=m_norm_mix,
             w_in=m_w_in, ret_gn_gain=m_ret_gn_gain, pool_w=m_pool_w, pool_scale=m_pool_scale, w_out=m_w_out,
             norm_ffn2=m_norm_ffn2, ffn2_gate=m_ffn2_gate, ffn2_up=m_ffn2_up, ffn2_down=m_ffn2_down, norm_final=m_norm_final)
    V = dict(norm_ffn1=v_norm_ffn1, ffn1_gate=v_ffn1_gate, ffn1_up=v_ffn1_up, ffn1_down=v_ffn1_down, norm_mix=v_norm_mix,
             w_in=v_w_in, ret_gn_gain=v_ret_gn_gain, pool_w=v_pool_w, pool_scale=v_pool_scale, w_out=v_w_out,
             norm_ffn2=v_norm_ffn2, ffn2_gate=v_ffn2_gate, ffn2_up=v_ffn2_up, ffn2_down=v_ffn2_down, norm_final=v_norm_final)

    n_seq, S, D = x.shape
    T = n_seq * S
    tm = min(1024, T // 2)
    tk = min(1024, T // 2)
    tk1 = min(2048, T // 2)
    xf = x.reshape(T, D)
    tgt = loss_target.reshape(T, D)

    def local(d, k):
        a = d[k][0]
        return a.T if k in TRANSPOSED else a

    def to_out(k, a):
        return (a.T if k in TRANSPOSED else a)[None]

    loc_bf = []
    for _, members in GROUPS:
        st = jnp.stack([local(W, k).astype(BF16) for k in members])
        loc_bf.append(st.reshape(st.shape[0], 2, st.shape[1] // 2, st.shape[2]))
    FS = loc_bf[0].shape[2] * 2
    w3_1 = gather_weights(loc_bf[:1])[0].reshape(3, N_CHIPS, FS, D)
    g1 = norm_ffn1.reshape(1, D)
    gm = norm_mix.reshape(1, D)
    g3 = norm_ffn2.reshape(1, D)
    gf = norm_final.reshape(1, D)
    gn_gain = ret_gn_gain.reshape(1, V_W)
    pscale = pool_scale.reshape(1, V_W)
    pw = pool_w.reshape(len(POOL_WINDOWS), POOL_GC, POOL_GC)
    consts = mixer_constants(S)

    def halves5(g, lb):
        return g.reshape(lb.shape[0], N_CHIPS, 2, lb.shape[2], lb.shape[3])

    (x1, act1, n1), landed = ffn_fwd(xf, g1, w3_1, tm, "ffn1_fwd", plan=gather_plan(loc_bf[1:]))
    gathered = [place_own(g, lb, "place_own_" + gn) for g, lb, (gn, _) in zip(landed, loc_bf[1:], GROUPS[1:])]
    win_full = gathered[0].reshape(N_CHIPS, D, IN_W // N_CHIPS)
    wout_full = gathered[1].reshape(D, D)
    w3_2 = gathered[2].reshape(3, N_CHIPS, FS, D)
    p = in_proj(x1, gm, win_full, tm, "in_proj")
    x2, m_s, o_s, st_s, pooled_s = mixer_fwd(p, x1, consts, gn_gain, pw, pscale, wout_full, n_seq, S, "mixer_fwd")
    x3, act2, n3 = ffn_fwd(x2, g3, w3_2, tm, "ffn2_fwd")

    dx3, df2, lacc, dgf = loss_bwd(x3, tgt, gf, tm, "loss_bwd")
    dn3, dab2 = ffn_bwd(df2, act2, w3_2, tm, "ffn2_bwd")
    to_sib, to_chips, back = send_to_sibling_other_half, exchange_between_chips, share_with_sibling
    out_g, out_d, out_m, out_v = {}, {}, {}, {}

    def update(members, own, sib, name):
        res = adamw_group([local(W, k) for k in members], [local(M, k) for k in members], [local(V, k) for k in members],
                          own, sib, "adamw_" + name)
        for l, k in enumerate(members):
            out_d[k], out_m[k], out_v[k], out_g[k] = (to_out(k, r) for r in res[4 * l:4 * l + 4])

    (gr_ffn2,) = ffn_wgrad(dab2, act2, n3, df2, tk, "ffn2_wgrad")
    g_ffn2 = halves5(gr_ffn2, loc_bf[3])
    (dx1, df1, dx2b, dp, n2, dg3, dgm, dgn, dps, dpw), (sb_ffn2,) = mixer_bwd(
        dx3, dn3, x2, g3, x1, p, o_s, st_s, pooled_s, consts, gm, gn_gain, pw, pscale, wout_full, win_full, n_seq, S,
        "mixer_bwd", plan=to_sib([g_ffn2]))
    cs_ffn2 = add_own_half(g_ffn2, sb_ffn2, "add_sibling_ffn2")
    (gr_wout,) = tn_shared(m_s, dx2b[None], tk1, "dw_out")
    (gr_win,) = tn_shared(n2, dp, tk1, "dw_in")
    g_mix = [halves5(gr_win, loc_bf[1]), halves5(gr_wout, loc_bf[2])]
    shapes = {k: W[k].shape for k in SMALL}
    small_part = dict(norm_mix=dgm, norm_ffn2=dg3, norm_final=dgf, ret_gn_gain=dgn, pool_scale=dps, pool_w=dpw)
    part = jnp.concatenate([_pack_small(small_part, shapes), lacc.reshape(-1, 128)], axis=0)
    n_small = part.shape[0] - lacc.size // 128
    (dn1, dab1), (pc_ffn2, sb_win, sb_wout, parts) = ffn_bwd(
        df1, act1, w3_1, tm, "ffn1_bwd",
        plan=_combine(_combine(to_chips([cs_ffn2]), to_sib(g_mix)), gather_small_plan(part)))
    summed = sum_small(part, parts, "sum_small")
    loss = jnp.sum(summed[n_small:]) * (0.5 / D)
    mh_ffn2 = sum_chips(cs_ffn2, pc_ffn2, "sum_chips_ffn2")
    cs_mix = [add_own_half(g, r, "add_sibling_" + gn) for g, r, gn in zip(g_mix, (sb_win, sb_wout), ("w_in", "w_out"))]
    def one(g):
        return g.reshape((1, N_CHIPS, 2) + loc_bf[0].shape[2:])

    (gr_g,), (sh_ffn2, pc_win, pc_wout) = wgrad_one(
        dab1, (0,), n1, tk1, "ffn1_wgrad_gate", plan=_combine(back([mh_ffn2]), to_chips(cs_mix)))
    mh_mix = [sum_chips(cs, pc, "sum_chips_" + gn) for cs, pc, gn in zip(cs_mix, (pc_win, pc_wout), ("w_in", "w_out"))]
    (gr_u,), (sb_g,) = wgrad_one(dab1, (1,), n1, tk1, "ffn1_wgrad_up", plan=to_sib([one(gr_g)]))
    cs_g = add_own_half(one(gr_g), sb_g, "add_sibling_ffn1_gate")
    (gr_d,), (pc_g, sb_u) = wgrad_one(
        act1, (1, 0), df1, tk1, "ffn1_wgrad_down", plan=_combine(to_chips([cs_g]), to_sib([one(gr_u)])))
    mh_g = sum_chips(cs_g, pc_g, "sum_chips_ffn1_gate")
    cs_u = add_own_half(one(gr_u), sb_u, "add_sibling_ffn1_up")
    (dx0, dg1), (sh_g, sh_win, sh_wout, pc_u, sb_d) = rms_bwd(
        xf, dx1, dn1, g1, tm, "ffn1_rms_bwd",
        plan=_combine(_combine(back([mh_g] + mh_mix), to_chips([cs_u])), to_sib([one(gr_d)])))
    mh_u = sum_chips(cs_u, pc_u, "sum_chips_ffn1_up")
    cs_d = add_own_half(one(gr_d), sb_d, "add_sibling_ffn1_down")
    first = dg1.reshape(-1, 128)
    sh_u, pc_d, firsts = _run_plan(
        _combine(_combine(back([mh_u]), to_chips([cs_d])), gather_small_plan(first)), "ffn1_grads_stage_b")
    mh_d = sum_chips(cs_d, pc_d, "sum_chips_ffn1_down")
    (sh_d,) = _run_plan(back([mh_d]), "ffn1_grads_stage_c")
    first_sum = sum_small(first, firsts, "sum_small_first")
    for k, own, sib in zip(GROUPS[0][1], (mh_g, mh_u, mh_d), (sh_g, sh_u, sh_d)):
        update((k,), own, sib, k)
    update(GROUPS[1][1], mh_mix[0], sh_win, "w_in")
    update(GROUPS[2][1], mh_mix[1], sh_wout, "w_out")
    update(GROUPS[3][1], mh_ffn2, sh_ffn2, "ffn2")

    for k, (d_, m_, v_, g_) in adamw_small(W, M, V, first_sum, summed, shapes).items():
        out_d[k], out_m[k], out_v[k], out_g[k] = d_, m_, v_, g_

    grad_x = dx0.reshape(n_seq, S, D)
    return (loss, grad_x, *[out_g[k] for k in ORDER], *[out_d[k] for k in ORDER],
            *[out_m[k] for k in ORDER], *[out_v[k] for k in ORDER])
```

```python
import functools
import math
from typing import Callable, NamedTuple, Optional

import jax
import jax.numpy as jnp
from jax import lax
from jax.experimental import pallas as pl
from jax.experimental.pallas import tpu as pltpu

F32 = jnp.float32
BF16 = jnp.bfloat16
MESH = pl.DeviceIdType.MESH

N_CHIPS = 4
N_DEV = 8
CHUNK = 64
RET_HEADS = 4
RET_DK = 64
RET_DV = 128
QK_W = RET_HEADS * RET_DK
V_W = RET_HEADS * RET_DV
POOL_WINDOWS = (2, 4, 8, 16)
POOL_GC = 128
ROPE_BASE = 10000.0
RMS_EPS = 1e-6
GN_EPS = 1e-5
ADAM_LR = 0.001
ADAM_B1 = 0.9
ADAM_B2 = 0.999
ADAM_EPS = 1e-08
ADAM_WD = 0.01
ADAM_STEP = 10
MXU_W = 256
RET_BLOCK = MXU_W
VMEM_LIMIT = 56 * 1024 * 1024

NN = (((1,), (0,)), ((), ()))
NT = (((1,), (1,)), ((), ()))
TN = (((0,), (0,)), ((), ()))


def _dot(a, b, dims=NN):
    return lax.dot_general(a, b, dims, preferred_element_type=F32)


def _sigmoid(x):
    return 0.5 * jnp.tanh(0.5 * x) + 0.5


def _params(n_grid):
    return pltpu.CompilerParams(dimension_semantics=("arbitrary",) * n_grid, vmem_limit_bytes=VMEM_LIMIT)


class CommPlan(NamedTuple):
    ins: tuple
    out_shape: tuple
    sems: tuple
    start: Callable
    finish: Callable
    mid: Optional[Callable] = None
    mid_at: float = 0.85


ANY = pl.BlockSpec(memory_space=pl.ANY)


def _call(body, plan, *, name, grid, in_specs, out_specs, out_shape, args, scratch_shapes=()):
    n_grid = len(grid)
    if plan is None:
        return pl.pallas_call(body, name=name, grid=grid, in_specs=in_specs, out_specs=out_specs, out_shape=out_shape,
                              scratch_shapes=scratch_shapes, compiler_params=_params(n_grid))(*args)
    n_in, n_out, n_sc = len(in_specs), len(out_specs), len(scratch_shapes)
    p_in, p_out = len(plan.ins), len(plan.out_shape)
    total = math.prod(grid)
    mid_step = min(total - 1, int(plan.mid_at * total))

    def riding(*refs):
        ins, pins = refs[:n_in], refs[n_in:n_in + p_in]
        o0 = n_in + p_in
        outs, pouts = refs[o0:o0 + n_out], refs[o0 + n_out:o0 + n_out + p_out]
        s0 = o0 + n_out + p_out
        scratch, psems = refs[s0:s0 + n_sc], refs[s0 + n_sc:]
        step = pl.program_id(0)
        for d in range(1, n_grid):
            step = step * grid[d] + pl.program_id(d)

        @pl.when(step == 0)
        def _():
            plan.start(pins, pouts, psems)

        body(*ins, *outs, *scratch)

        if plan.mid is not None:
            @pl.when(step == mid_step)
            def _():
                plan.mid(pins, pouts, psems)

        @pl.when(step == total - 1)
        def _():
            plan.finish(pins, pouts, psems)

    res = pl.pallas_call(
        riding, name=name, grid=grid, in_specs=list(in_specs) + [ANY] * p_in, out_specs=list(out_specs) + [ANY] * p_out,
        out_shape=list(out_shape) + list(plan.out_shape), scratch_shapes=list(scratch_shapes) + list(plan.sems),
        compiler_params=_params(n_grid))(*args, *plan.ins)
    return res[:n_out], res[n_out:]


def _run_plan(plan, name):
    p_in, p_out = len(plan.ins), len(plan.out_shape)

    def body(*refs):
        pins, pouts, psems = refs[:p_in], refs[p_in:p_in + p_out], refs[p_in + p_out:]
        plan.start(pins, pouts, psems)
        if plan.mid is not None:
            plan.mid(pins, pouts, psems)
        plan.finish(pins, pouts, psems)

    return pl.pallas_call(body, name=name, in_specs=[ANY] * p_in, out_specs=[ANY] * p_out,
                          out_shape=list(plan.out_shape), scratch_shapes=list(plan.sems))(*plan.ins)


def _rms_parts(x):
    r = lax.rsqrt(jnp.mean(x * x, axis=-1, keepdims=True) + RMS_EPS)
    return r, x * r


def _rms_bwd(dn, xhat, r, gain, dres):
    dxh = dn * gain
    return dres + r * (dxh - xhat * jnp.mean(dxh * xhat, axis=-1, keepdims=True))


RING = 3


def _ring_scratch(w3):
    _, _, FS, D = w3.shape
    return [pltpu.VMEM((RING, 3, FS, D), w3.dtype), pltpu.SemaphoreType.DMA((RING,))]


def _ring_weights(w_hbm, ring, sem, step, total):
    J = w_hbm.shape[1]

    def copy(s):
        return pltpu.make_async_copy(w_hbm.at[:, s % J], ring.at[s % RING], sem.at[s % RING])

    @pl.when(step == 0)
    def _():
        for s in range(RING - 1):
            copy(s).start()

    @pl.when(step + (RING - 1) < total)
    def _():
        copy(step + (RING - 1)).start()

    copy(step).wait()
    return ring.at[step % RING]


def ffn_fwd(x, gain, w3, tm, name, plan=None):
    T, D = x.shape
    _, J, FS, _ = w3.shape
    total = (T // tm) * J
    assert total >= RING - 1

    def body(x_ref, g_ref, w_hbm, xo_ref, b_ref, si_ref, t_ref, n_ref, acc, ring, ring_sem):
        j = pl.program_id(1)
        w = _ring_weights(w_hbm, ring, ring_sem, pl.program_id(0) * J + j, total)

        @pl.when(j == 0)
        def _():
            _, xhat = _rms_parts(x_ref[...])
            n_ref[...] = (xhat * g_ref[...]).astype(BF16)
            acc[...] = jnp.zeros_like(acc)

        n = n_ref[...]
        a = _dot(n, w[0], NT)
        b = _dot(n, w[1], NT)
        sg = _sigmoid(a)
        si = a * sg
        b_ref[...] = b.astype(BF16)
        si_ref[...] = si.astype(BF16)
        t_ref[...] = (sg + si * (1.0 - sg)).astype(BF16)
        acc[...] += _dot((si * b).astype(BF16), w[2])

        @pl.when(j == J - 1)
        def _():
            xo_ref[...] = x_ref[...] + 0.5 * acc[...]

    return _call(
        body, plan, name=name, grid=(T // tm, J),
        in_specs=[pl.BlockSpec((tm, D), lambda i, j: (i, 0)),
                  pl.BlockSpec((1, D), lambda i, j: (0, 0)), ANY],
        out_specs=[pl.BlockSpec((tm, D), lambda i, j: (i, 0))] +
                  [pl.BlockSpec((None, tm, FS), lambda i, j: (j, i, 0))] * 3 +
                  [pl.BlockSpec((tm, D), lambda i, j: (i, 0))],
        out_shape=[jax.ShapeDtypeStruct((T, D), F32)] + [jax.ShapeDtypeStruct((J, T, FS), BF16)] * 3 +
                  [jax.ShapeDtypeStruct((T, D), BF16)],
        scratch_shapes=[pltpu.VMEM((tm, D), F32)] + _ring_scratch(w3),
        args=(x, gain, w3))


def ffn_bwd(df, b_s, si_s, t_s, w3, tm, name, plan=None):
    T, D = df.shape
    _, J, FS, _ = w3.shape

    chunks = [slice(s, min(s + MXU_W, FS)) for s in range(0, FS, MXU_W)]
    nc = len(chunks)
    total = (T // tm) * J
    assert total >= RING - 1

    def body(df_ref, b_ref, si_ref, t_ref, w_hbm, dn_ref, da_ref, db_ref, ring, ring_sem):
        w = _ring_weights(w_hbm, ring, ring_sem, pl.program_id(0) * J + pl.program_id(1), total)

        @pl.when(pl.program_id(1) == 0)
        def _():
            dn_ref[...] = jnp.zeros_like(dn_ref)

        def elementwise(dh, sl):
            da = (dh * b_ref[:, sl].astype(F32) * t_ref[:, sl].astype(F32)).astype(BF16)
            db = (dh * si_ref[:, sl].astype(F32)).astype(BF16)
            da_ref[:, sl] = da
            db_ref[:, sl] = db
            return da, db

        df = df_ref[...]
        dh, dab, acc = [None] * nc, [None] * nc, None
        for step in range(nc + 2):
            if step < nc:
                dh[step] = _dot(df, w[2, chunks[step], :], NT)
            if 1 <= step <= nc:
                dab[step - 1] = elementwise(dh[step - 1], chunks[step - 1])
            if step >= 2:
                sl = chunks[step - 2]
                part = _dot(dab[step - 2][0], w[0, sl, :]) + _dot(dab[step - 2][1], w[1, sl, :])
                acc = part if acc is None else acc + part
        dn_ref[...] += acc

    tok = pl.BlockSpec((tm, D), lambda i, j: (i, 0))
    sh = pl.BlockSpec((None, tm, FS), lambda i, j: (j, i, 0))
    return _call(
        body, plan, name=name, grid=(T // tm, J),
        in_specs=[tok, sh, sh, sh, ANY],
        out_specs=[tok, sh, sh],
        out_shape=[jax.ShapeDtypeStruct((T, D), F32),
                   jax.ShapeDtypeStruct((J, T, FS), BF16),
                   jax.ShapeDtypeStruct((J, T, FS), BF16)],
        scratch_shapes=_ring_scratch(w3),
        args=(df, b_s, si_s, t_s, w3))


def rms_bwd(x, dout, dn, gain, tm, name, plan=None):
    T, D = x.shape

    def body(x_ref, do_ref, dn_ref, g_ref, dx_ref, dgain_ref):
        @pl.when(pl.program_id(0) == 0)
        def _():
            dgain_ref[...] = jnp.zeros_like(dgain_ref)

        r, xhat = _rms_parts(x_ref[...])
        dn = dn_ref[...]
        dgain_ref[...] += jnp.sum(dn * xhat, axis=0, keepdims=True)
        dx_ref[...] = _rms_bwd(dn, xhat, r, g_ref[...], do_ref[...])

    tok = pl.BlockSpec((tm, D), lambda i: (i, 0))
    vec = pl.BlockSpec((1, D), lambda i: (0, 0))
    return _call(
        body, plan, name=name, grid=(T // tm,), in_specs=[tok, tok, tok, vec], out_specs=[tok, vec],
        out_shape=[jax.ShapeDtypeStruct((T, D), F32), jax.ShapeDtypeStruct((1, D), F32)],
        args=(x, dout, dn, gain))


def ffn_wgrad(da, db, si, b, n, df, tk, name, plan=None):
    J, T, FS = da.shape
    D = n.shape[1]

    def body(da_ref, db_ref, si_ref, b_ref, n_ref, df_ref, o_ref):
        @pl.when(pl.program_id(1) == 0)
        def _():
            o_ref[...] = jnp.zeros_like(o_ref)

        nv = n_ref[...]
        o_ref[0] += _dot(da_ref[...], nv, TN)
        o_ref[1] += _dot(db_ref[...], nv, TN)
        o_ref[2] += _dot(si_ref[...] * b_ref[...], df_ref[...], TN)

    sh = pl.BlockSpec((None, tk, FS), lambda j, k: (j, k, 0))
    tok = pl.BlockSpec((tk, D), lambda j, k: (k, 0))
    return _call(
        body, plan, name=name, grid=(J, T // tk),
        in_specs=[sh, sh, sh, sh, tok, tok],
        out_specs=[pl.BlockSpec((3, None, FS, D), lambda j, k: (0, j, 0, 0))],
        out_shape=[jax.ShapeDtypeStruct((3, J, FS, D), F32)],
        args=(da, db, si, b, n, df))


def wgrad_one(a, b, tk, name, plan=None, a2=None):
    J, T, FS = a.shape
    D = b.shape[1]
    lhs = (a,) if a2 is None else (a, a2)

    def body(*refs):
        o_ref = refs[-1]

        @pl.when(pl.program_id(1) == 0)
        def _():
            o_ref[...] = jnp.zeros_like(o_ref)

        av = refs[0][...] if a2 is None else refs[0][...] * refs[1][...]
        o_ref[...] += _dot(av, refs[-2][...], TN)

    return _call(
        body, plan, name=name, grid=(J, T // tk),
        in_specs=[pl.BlockSpec((None, tk, FS), lambda j, k: (j, k, 0))] * len(lhs) +
                 [pl.BlockSpec((tk, D), lambda j, k: (k, 0))],
        out_specs=[pl.BlockSpec((None, FS, D), lambda j, k: (j, 0, 0))],
        out_shape=[jax.ShapeDtypeStruct((J, FS, D), F32)],
        args=(*lhs, b))


def tn_shared(a, b, tk, name, plan=None):
    T, M = a.shape
    J, _, N = b.shape

    def body(a_ref, b_ref, o_ref):
        @pl.when(pl.program_id(0) == 0)
        def _():
            o_ref[...] = jnp.zeros_like(o_ref)

        a_t = a_ref[...].astype(BF16).T
        for j in range(J):
            o_ref[j] += _dot(a_t, b_ref[j].astype(BF16))

    return _call(
        body, plan, name=name, grid=(T // tk,),
        in_specs=[pl.BlockSpec((tk, M), lambda k: (k, 0)), pl.BlockSpec((J, tk, N), lambda k: (0, k, 0))],
        out_specs=[pl.BlockSpec((J, M, N), lambda k: (0, 0, 0))],
        out_shape=[jax.ShapeDtypeStruct((J, M, N), F32)],
        args=(a, b))


def in_proj(x, gain, win, tm, name):
    T, D = x.shape
    J, _, W = win.shape

    def body(x_ref, g_ref, w_ref, p_ref):
        _, xhat = _rms_parts(x_ref[...])
        n = (xhat * g_ref[...]).astype(BF16)
        for j in range(J):
            p_ref[j] = _dot(n, w_ref[j])

    return pl.pallas_call(
        body, name=name, grid=(T // tm,),
        in_specs=[pl.BlockSpec((tm, D), lambda i: (i, 0)),
                  pl.BlockSpec((1, D), lambda i: (0, 0)),
                  pl.BlockSpec((J, D, W), lambda i: (0, 0, 0))],
        out_specs=pl.BlockSpec((J, tm, W), lambda i: (0, i, 0)),
        out_shape=jax.ShapeDtypeStruct((J, T, W), F32),
        compiler_params=_params(1),
    )(x, gain, win)


def mixer_constants(S):
    B = RET_BLOCK
    half = RET_DK // 2
    freqs = ROPE_BASE ** (-jnp.arange(half, dtype=F32) * 2.0 / RET_DK)
    ang = jnp.arange(S, dtype=F32)[:, None] * freqs[None, :]
    cos = jnp.tile(jnp.cos(ang), (1, 2 * RET_HEADS))
    sin = jnp.tile(jnp.concatenate([-jnp.sin(ang), jnp.sin(ang)], axis=1), (1, RET_HEADS))
    gamma = 1.0 - 2.0 ** (-5.0 - jnp.arange(RET_HEADS, dtype=F32))
    log_g = jnp.log(gamma)
    idx = jnp.arange(B, dtype=F32)
    ci = jnp.arange(B) // CHUNK
    dist = jnp.abs(idx[:, None] - idx[None, :])
    m_intra = jnp.exp(log_g[:, None, None] * dist[None]) * (ci[None, :] <= ci[:, None])[None].astype(F32)
    lg_lane = jnp.repeat(log_g, RET_DK)
    d_q = jnp.exp(lg_lane[None, :] * (idx[:, None] + 1.0))
    d_k = jnp.exp(lg_lane[None, :] * (B - 1.0 - idx[:, None]))
    d_s = jnp.broadcast_to(jnp.exp(lg_lane * B)[:, None], (QK_W, V_W))
    bm = (jnp.arange(QK_W)[:, None] // RET_DK == jnp.arange(V_W)[None, :] // RET_DV).astype(F32)
    win = jnp.repeat(jnp.array(POOL_WINDOWS, F32), POOL_GC)
    invc = 1.0 / jnp.minimum(jnp.arange(S, dtype=F32)[:, None] + 1.0, win[None, :])
    return dict(cos=cos, sin=sin, m=m_intra, dq=d_q, dk=d_k, ds=d_s, bm=bm, invc=invc)


def _swap_halves(x):
    w = x.shape[1]
    lane = lax.broadcasted_iota(jnp.int32, x.shape, 1)
    return jnp.where((lane % RET_DK) < RET_DK // 2, pltpu.roll(x, w - RET_DK // 2, 1), pltpu.roll(x, RET_DK // 2, 1))


def _rot(x, cos, sin):
    return x * cos + _swap_halves(x) * sin


def _rot_t(d, cos, sin):
    return d * cos + _swap_halves(d * sin)


def _lane_groups(parts):
    return jnp.concatenate([p[:, POOL_GC * g:POOL_GC * (g + 1)] for g, p in enumerate(parts)], axis=1)


def _head_mask(shape, h):
    lane = lax.broadcasted_iota(jnp.int32, shape, 1)
    return (lane // RET_DK) == h


def _group_norm(o):
    yh, rs = [], []
    for h in range(RET_HEADS):
        oh = o[:, RET_DV * h:RET_DV * (h + 1)]
        xc = oh - jnp.mean(oh, axis=-1, keepdims=True)
        r = lax.rsqrt(jnp.mean(xc * xc, axis=-1, keepdims=True) + GN_EPS)
        yh.append(xc * r)
        rs.append(r)
    return yh, rs


def mixer_fwd(p, x1, consts, gn_gain, pool_w, pool_scale, wout, n_seq, S, name):
    _, T, _ = p.shape
    D = x1.shape[1]
    B = RET_BLOCK
    nb = S // B

    def body(p_ref, x1_ref, cos_ref, sin_ref, invc_ref, m_ref, dq_ref, dk_ref, ds_ref, bm_ref,
             gain_ref, pw_ref, sc_ref, wout_ref,
             x2_ref, m_out, o_out, st_out, pooled_out, state, prev_u):
        blk = pl.program_id(1)

        @pl.when(blk == 0)
        def _():
            state[...] = jnp.zeros_like(state)
            prev_u[...] = jnp.zeros_like(prev_u)

        qk = p_ref[0]
        v = p_ref[1]
        g = p_ref[2]
        u = p_ref[3]
        cos = cos_ref[...]
        sin = sin_ref[...]
        qr = _rot(qk[:, :QK_W], cos, sin) * (RET_DK ** -0.5)
        kr = _rot(qk[:, QK_W:], cos, sin)
        qb = qr.astype(BF16)
        kb = kr.astype(BF16)
        vb = v.astype(BF16)
        st = state[...]
        st_out[...] = st
        cross = _dot((qr * dq_ref[...]).astype(BF16), st.astype(BF16))
        o_parts = []
        for h in range(RET_HEADS):
            qm = jnp.where(_head_mask(qb.shape, h), qb, jnp.zeros_like(qb))
            sc = (_dot(qm, kb, NT) * m_ref[h]).astype(BF16)
            o_parts.append(_dot(sc, vb[:, RET_DV * h:RET_DV * (h + 1)]) + cross[:, RET_DV * h:RET_DV * (h + 1)])
        o = jnp.concatenate(o_parts, axis=1)
        o_out[...] = o
        kv = _dot((kr * dk_ref[...]).astype(BF16), vb, TN)
        state[...] = st * ds_ref[...] + kv * bm_ref[...]

        yh, _ = _group_norm(o)
        r = g * _sigmoid(g) * (jnp.concatenate(yh, axis=1) * gain_ref[...])

        ext = jnp.concatenate([prev_u[...], u], axis=0)
        sums = []
        run = ext
        for k in (1, 2, 4, 8):
            run = run + pltpu.roll(run, k, 0)
            sums.append(run[B:, :])
        prev_u[...] = u
        pooled = (_lane_groups(sums) * invc_ref[...] - u).astype(BF16)
        pooled_out[...] = pooled
        yp = [_dot(pooled[:, POOL_GC * gi:POOL_GC * (gi + 1)], pw_ref[gi].astype(BF16)) for gi in range(len(POOL_WINDOWS))]
        s = jnp.concatenate(yp, axis=1) * sc_ref[...]
        m = jnp.concatenate([r, s], axis=1).astype(BF16)
        m_out[...] = m
        x2_ref[...] = x1_ref[...] + _dot(m, wout_ref[...])

    tokmap = lambda b, k: (b * nb + k, 0)
    seqmap = lambda b, k: (k, 0)
    const2 = lambda b, k: (0, 0)
    const3 = lambda b, k: (0, 0, 0)
    return pl.pallas_call(
        body, name=name, grid=(n_seq, nb),
        in_specs=[pl.BlockSpec((4, B, V_W), lambda b, k: (0, b * nb + k, 0)),
                  pl.BlockSpec((B, D), tokmap),
                  pl.BlockSpec((B, QK_W), seqmap), pl.BlockSpec((B, QK_W), seqmap), pl.BlockSpec((B, V_W), seqmap),
                  pl.BlockSpec((RET_HEADS, B, B), const3),
                  pl.BlockSpec((B, QK_W), const2), pl.BlockSpec((B, QK_W), const2),
                  pl.BlockSpec((QK_W, V_W), const2), pl.BlockSpec((QK_W, V_W), const2),
                  pl.BlockSpec((1, V_W), const2), pl.BlockSpec((4, POOL_GC, POOL_GC), const3),
                  pl.BlockSpec((1, V_W), const2), pl.BlockSpec((D, D), const2)],
        out_specs=[pl.BlockSpec((B, D), tokmap), pl.BlockSpec((B, D), tokmap), pl.BlockSpec((B, V_W), tokmap),
                   pl.BlockSpec((None, QK_W, V_W), lambda b, k: (b * nb + k, 0, 0)),
                   pl.BlockSpec((B, V_W), tokmap)],
        out_shape=[jax.ShapeDtypeStruct((T, D), F32), jax.ShapeDtypeStruct((T, D), BF16),
                   jax.ShapeDtypeStruct((T, V_W), F32), jax.ShapeDtypeStruct((T // B, QK_W, V_W), F32),
                   jax.ShapeDtypeStruct((T, V_W), BF16)],
        scratch_shapes=[pltpu.VMEM((QK_W, V_W), F32), pltpu.VMEM((B, V_W), F32)],
        compiler_params=_params(2),
    )(p, x1, consts["cos"], consts["sin"], consts["invc"], consts["m"], consts["dq"], consts["dk"],
      consts["ds"], consts["bm"], gn_gain, pool_w, pool_scale, wout)


def mixer_bwd(dx3, dn3, x2, g3, x1, p, o_s, st_s, pooled_s, consts, gmix, gn_gain, pool_w, pool_scale, wout, win,
              n_seq, S, name, plan=None):
    T, D = x1.shape
    B = RET_BLOCK
    nb = S // B
    NG = len(POOL_WINDOWS)

    def body(dx3_ref, dn3_ref, x2_ref, g3_ref, x1_ref, p_ref, o_ref, st_ref, pooled_ref, cos_ref, sin_ref, invc_ref,
             m_ref, dq_ref, dk_ref, ds_ref, bm_ref, gmix_ref, gain_ref, pw_ref, sc_ref, wout_ref, win_ref,
             dx1_ref, df1_ref, dx2_ref, dp_ref, n2_ref, dg3_ref, dgmix_ref, dgain_ref, dscale_ref, dpw_ref,
             gstate, next_e):
        b = pl.program_id(0)
        blk = pl.program_id(1)

        @pl.when(jnp.logical_and(b == 0, blk == 0))
        def _():
            dg3_ref[...] = jnp.zeros_like(dg3_ref)
            dgmix_ref[...] = jnp.zeros_like(dgmix_ref)
            dgain_ref[...] = jnp.zeros_like(dgain_ref)
            dscale_ref[...] = jnp.zeros_like(dscale_ref)
            dpw_ref[...] = jnp.zeros_like(dpw_ref)

        @pl.when(blk == 0)
        def _():
            gstate[...] = jnp.zeros_like(gstate)
            next_e[...] = jnp.zeros_like(next_e)

        r2, xhat2 = _rms_parts(x2_ref[...])
        dn3 = dn3_ref[...]
        dg3_ref[...] += jnp.sum(dn3 * xhat2, axis=0, keepdims=True)
        dx2v = _rms_bwd(dn3, xhat2, r2, g3_ref[...], dx3_ref[...])
        dx2b = dx2v.astype(BF16)
        dx2_ref[...] = dx2b
        dm = _dot(dx2b, wout_ref[...], NT)
        dr = dm[:, :V_W]
        dsv = dm[:, V_W:]

        pooled = pooled_ref[...]
        scale = sc_ref[...]
        dyp = (dsv * scale).astype(BF16)
        yp, dpl = [], []
        for gi in range(NG):
            sl = slice(POOL_GC * gi, POOL_GC * (gi + 1))
            pwb = pw_ref[gi].astype(BF16)
            yp.append(_dot(pooled[:, sl], pwb))
            dpw_ref[gi] += _dot(pooled[:, sl], dyp[:, sl], TN)
            dpl.append(_dot(dyp[:, sl], pwb, NT))
        dscale_ref[...] += jnp.sum(dsv * jnp.concatenate(yp, axis=1), axis=0, keepdims=True)
        dpooled = jnp.concatenate(dpl, axis=1)
        e = dpooled * invc_ref[...]
        ext = jnp.concatenate([e, next_e[...]], axis=0)
        sums = []
        run = ext
        for k in (1, 2, 4, 8):
            run = run + pltpu.roll(run, 2 * B - k, 0)
            sums.append(run[:B, :])
        next_e[...] = e
        du = _lane_groups(sums) - dpooled

        qk = p_ref[0]
        v = p_ref[1]
        g = p_ref[2]
        sg = _sigmoid(g)
        si = g * sg
        yh, rs = _group_norm(o_ref[...])
        yhat = jnp.concatenate(yh, axis=1)
        gain = gain_ref[...]
        dg = dr * (yhat * gain) * (sg * (1.0 + g * (1.0 - sg)))
        dy = dr * si
        dgain_ref[...] += jnp.sum(dy * yhat, axis=0, keepdims=True)
        dyh = dy * gain
        do_parts = []
        for h in range(RET_HEADS):
            sl = slice(RET_DV * h, RET_DV * (h + 1))
            dh_ = dyh[:, sl]
            m1 = jnp.mean(dh_, axis=-1, keepdims=True)
            m2 = jnp.mean(dh_ * yh[h], axis=-1, keepdims=True)
            do_parts.append(rs[h] * (dh_ - m1 - yh[h] * m2))
        dob = jnp.concatenate(do_parts, axis=1).astype(BF16)

        cos = cos_ref[...]
        sin = sin_ref[...]
        qr = _rot(qk[:, :QK_W], cos, sin) * (RET_DK ** -0.5)
        kr = _rot(qk[:, QK_W:], cos, sin)
        qb = qr.astype(BF16)
        kb = kr.astype(BF16)
        vb = v.astype(BF16)
        dqd = dq_ref[...]
        dkd = dk_ref[...]
        stb = st_ref[...].astype(BF16)
        gs = gstate[...]
        gb = gs.astype(BF16)
        dqs = _dot(dob, stb, NT) * dqd
        dkr = _dot(vb, gb, NT) * dkd
        dv_cross = _dot((kr * dkd).astype(BF16), gb)
        ds_cross = _dot((qr * dqd).astype(BF16), dob, TN) * bm_ref[...]
        gstate[...] = ds_cross + gs * ds_ref[...]
        dv_parts = []
        for h in range(RET_HEADS):
            sl = slice(RET_DV * h, RET_DV * (h + 1))
            hm = _head_mask(qb.shape, h)
            qm = jnp.where(hm, qb, jnp.zeros_like(qb))
            mh = m_ref[h]
            sc = (_dot(qm, kb, NT) * mh).astype(BF16)
            dsc = (_dot(dob[:, sl], vb[:, sl], NT) * mh).astype(BF16)
            dqs = dqs + jnp.where(hm, _dot(dsc, kb), 0.0)
            dkr = dkr + jnp.where(hm, _dot(dsc, qb, TN), 0.0)
            dv_parts.append(_dot(sc, dob[:, sl], TN) + dv_cross[:, sl])
        dq = _rot_t(dqs * (RET_DK ** -0.5), cos, sin)
        dk = _rot_t(dkr, cos, sin)
        dp = [jnp.concatenate([dq, dk], axis=1).astype(BF16), jnp.concatenate(dv_parts, axis=1).astype(BF16),
              dg.astype(BF16), du.astype(BF16)]
        dn = jnp.zeros((B, D), F32)
        for jj in range(4):
            dp_ref[jj] = dp[jj]
            dn = dn + _dot(dp[jj], win_ref[jj], NT)

        x1v = x1_ref[...]
        r, xhat = _rms_parts(x1v)
        gm = gmix_ref[...]
        n2_ref[...] = (xhat * gm).astype(BF16)
        dgmix_ref[...] += jnp.sum(dn * xhat, axis=0, keepdims=True)
        dx1 = _rms_bwd(dn, xhat, r, gm, dx2v)
        dx1_ref[...] = dx1
        df1_ref[...] = (0.5 * dx1).astype(BF16)

    rev = lambda b, k: (b * nb + (nb - 1 - k), 0)
    seqrev = lambda b, k: (nb - 1 - k, 0)
    const2 = lambda b, k: (0, 0)
    const3 = lambda b, k: (0, 0, 0)
    return _call(
        body, plan, name=name, grid=(n_seq, nb),
        in_specs=[pl.BlockSpec((B, D), rev), pl.BlockSpec((B, D), rev), pl.BlockSpec((B, D), rev),
                  pl.BlockSpec((1, D), const2), pl.BlockSpec((B, D), rev),
                  pl.BlockSpec((4, B, V_W), lambda b, k: (0, b * nb + (nb - 1 - k), 0)),
                  pl.BlockSpec((B, V_W), rev),
                  pl.BlockSpec((None, QK_W, V_W), lambda b, k: (b * nb + (nb - 1 - k), 0, 0)),
                  pl.BlockSpec((B, V_W), rev),
                  pl.BlockSpec((B, QK_W), seqrev), pl.BlockSpec((B, QK_W), seqrev), pl.BlockSpec((B, V_W), seqrev),
                  pl.BlockSpec((RET_HEADS, B, B), const3),
                  pl.BlockSpec((B, QK_W), const2), pl.BlockSpec((B, QK_W), const2),
                  pl.BlockSpec((QK_W, V_W), const2), pl.BlockSpec((QK_W, V_W), const2),
                  pl.BlockSpec((1, D), const2), pl.BlockSpec((1, V_W), const2),
                  pl.BlockSpec((NG, POOL_GC, POOL_GC), const3), pl.BlockSpec((1, V_W), const2),
                  pl.BlockSpec((D, D), const2), pl.BlockSpec((4, D, V_W), const3)],
        out_specs=[pl.BlockSpec((B, D), rev), pl.BlockSpec((B, D), rev), pl.BlockSpec((B, D), rev),
                   pl.BlockSpec((4, B, V_W), lambda b, k: (0, b * nb + (nb - 1 - k), 0)),
                   pl.BlockSpec((B, D), rev),
                   pl.BlockSpec((1, D), const2), pl.BlockSpec((1, D), const2), pl.BlockSpec((1, V_W), const2),
                   pl.BlockSpec((1, V_W), const2), pl.BlockSpec((NG, POOL_GC, POOL_GC), const3)],
        out_shape=[jax.ShapeDtypeStruct((T, D), F32), jax.ShapeDtypeStruct((T, D), BF16),
                   jax.ShapeDtypeStruct((T, D), BF16), jax.ShapeDtypeStruct((4, T, V_W), BF16),
                   jax.ShapeDtypeStruct((T, D), BF16),
                   jax.ShapeDtypeStruct((1, D), F32), jax.ShapeDtypeStruct((1, D), F32), jax.ShapeDtypeStruct((1, V_W), F32),
                   jax.ShapeDtypeStruct((1, V_W), F32), jax.ShapeDtypeStruct((NG, POOL_GC, POOL_GC), F32)],
        scratch_shapes=[pltpu.VMEM((QK_W, V_W), F32), pltpu.VMEM((B, V_W), F32)],
        args=(dx3, dn3, x2, g3, x1, p, o_s, st_s, pooled_s, consts["cos"], consts["sin"], consts["invc"], consts["m"], consts["dq"],
              consts["dk"], consts["ds"], consts["bm"], gmix, gn_gain, pool_w, pool_scale, wout, win))


def loss_bwd(x3, tgt, gain, tm, name):
    T, D = x3.shape

    def body(x_ref, t_ref, g_ref, dx_ref, df_ref, lacc_ref, dg_ref):
        @pl.when(pl.program_id(0) == 0)
        def _():
            lacc_ref[...] = jnp.zeros_like(lacc_ref)
            dg_ref[...] = jnp.zeros_like(dg_ref)

        r, xhat = _rms_parts(x_ref[...])
        gv = g_ref[...]
        err = xhat * gv - t_ref[...]
        lacc_ref[...] += jnp.sum(err * err, axis=0, keepdims=True)
        dy = err * (1.0 / D)
        dg_ref[...] += jnp.sum(dy * xhat, axis=0, keepdims=True)
        dx = _rms_bwd(dy, xhat, r, gv, 0.0)
        dx_ref[...] = dx
        df_ref[...] = (0.5 * dx).astype(BF16)

    tok = pl.BlockSpec((tm, D), lambda i: (i, 0))
    vec = pl.BlockSpec((1, D), lambda i: (0, 0))
    return pl.pallas_call(
        body, name=name, grid=(T // tm,),
        in_specs=[tok, tok, vec], out_specs=[tok, tok, vec, vec],
        out_shape=[jax.ShapeDtypeStruct((T, D), F32), jax.ShapeDtypeStruct((T, D), BF16),
                   jax.ShapeDtypeStruct((1, D), F32), jax.ShapeDtypeStruct((1, D), F32)],
        compiler_params=_params(1),
    )(x3, tgt, gain)


def _adamw_math(w, g, m, v):
    m2 = ADAM_B1 * m + (1.0 - ADAM_B1) * g
    v2 = ADAM_B2 * v + (1.0 - ADAM_B2) * (g * g)
    m_hat = m2 / (1.0 - ADAM_B1 ** ADAM_STEP)
    v_hat = v2 / (1.0 - ADAM_B2 ** ADAM_STEP)
    return -ADAM_LR * (m_hat / (jnp.sqrt(v_hat) + ADAM_EPS) + ADAM_WD * w), m2, v2


def _place():
    x, y, c = lax.axis_index("x"), lax.axis_index("y"), lax.axis_index("c")
    other_chips = [(1 - x, y), (x, 1 - y), (1 - x, 1 - y)]
    return x, y, c, other_chips


def _exchange_plan(ins, out_shape, copies, n_copies):
    def descriptors(pins, pouts, psems):
        send, recv = psems
        return [pltpu.make_async_remote_copy(src_ref=s, dst_ref=d, send_sem=send.at[i], recv_sem=recv.at[i],
                                             device_id=dev, device_id_type=MESH)
                for i, (s, d, dev) in enumerate(copies(pins, pouts))]

    def start(pins, pouts, psems):
        for cp in descriptors(pins, pouts, psems):
            cp.start()

    def finish(pins, pouts, psems):
        for cp in descriptors(pins, pouts, psems):
            cp.wait()

    return CommPlan(tuple(ins), tuple(out_shape),
                    (pltpu.SemaphoreType.DMA((n_copies,)), pltpu.SemaphoreType.DMA((n_copies,))), start, finish)


def _combine(a, b):
    assert a.mid is None and b.mid is None
    ni, no, ns = len(a.ins), len(a.out_shape), len(a.sems)

    def start(pins, pouts, psems):
        a.start(pins[:ni], pouts[:no], psems[:ns])
        b.start(pins[ni:], pouts[no:], psems[ns:])

    def finish(pins, pouts, psems):
        a.finish(pins[:ni], pouts[:no], psems[:ns])
        b.finish(pins[ni:], pouts[no:], psems[ns:])

    return CommPlan(a.ins + b.ins, a.out_shape + b.out_shape, a.sems + b.sems, start, finish)


def gather_plan(shards):
    n = len(shards)

    def start(pins, pouts, psems):
        x, y, c, chips = _place()
        mine = 2 * x + y
        for w in range(n):
            for k, (px, py) in enumerate(chips):
                pltpu.make_async_remote_copy(
                    src_ref=pins[w].at[:, c], dst_ref=pouts[w].at[:, mine, c],
                    send_sem=psems[0].at[w, k], recv_sem=psems[1].at[w, k],
                    device_id=(px, py, c), device_id_type=MESH).start()

    def mid(pins, pouts, psems):
        x, y, c, chips = _place()
        for w in range(n):
            for k, (px, py) in enumerate(chips):
                landed = pouts[w].at[:, 2 * px + py, c]
                pltpu.make_async_remote_copy(
                    src_ref=landed, dst_ref=landed, send_sem=psems[0].at[w, k], recv_sem=psems[1].at[w, k],
                    device_id=(px, py, c), device_id_type=MESH).wait_recv()
                pltpu.make_async_remote_copy(
                    src_ref=landed, dst_ref=landed, send_sem=psems[2].at[w, k], recv_sem=psems[3].at[w, k],
                    device_id=(x, y, 1 - c), device_id_type=MESH).start()

    def finish(pins, pouts, psems):
        x, y, c, chips = _place()
        mine = 2 * x + y
        for w in range(n):
            for k, (px, py) in enumerate(chips):
                landed = pouts[w].at[:, 2 * px + py, 1 - c]
                cp = pltpu.make_async_remote_copy(
                    src_ref=landed, dst_ref=landed, send_sem=psems[2].at[w, k], recv_sem=psems[3].at[w, k],
                    device_id=(x, y, 1 - c), device_id_type=MESH)
                cp.wait_recv()
                cp.wait_send()
                pltpu.make_async_remote_copy(
                    src_ref=pins[w].at[:, c], dst_ref=pouts[w].at[:, mine, c],
                    send_sem=psems[0].at[w, k], recv_sem=psems[1].at[w, k],
                    device_id=(px, py, c), device_id_type=MESH).wait_send()

    return CommPlan(tuple(shards),
                    tuple(jax.ShapeDtypeStruct((s.shape[0], N_CHIPS) + s.shape[1:], s.dtype) for s in shards),
                    tuple(pltpu.SemaphoreType.DMA((n, 3)) for _ in range(4)), start, finish, mid)


def place_own(gathered, shard, name):
    L, _, Rh, C = shard.shape

    def body(chip_ref, g_ref, s_ref, o_ref):
        o_ref[...] = s_ref[...]

    return pl.pallas_call(
        body, name=name,
        grid_spec=pltpu.PrefetchScalarGridSpec(
            num_scalar_prefetch=1, grid=(L,),
            in_specs=[ANY, pl.BlockSpec((None, 2, Rh, C), lambda l, chip_ref: (l, 0, 0, 0))],
            out_specs=pl.BlockSpec((None, None, 2, Rh, C), lambda l, chip_ref: (l, chip_ref[0], 0, 0, 0))),
        out_shape=jax.ShapeDtypeStruct(gathered.shape, gathered.dtype),
        input_output_aliases={1: 0},
        compiler_params=_params(1),
    )(_chip_index(), gathered, shard)


def gather_weights(shards):
    n = len(shards)
    units = [(w, l) for w in range(n) for l in range(shards[w].shape[0])]
    nu = len(units)

    def body(*refs):
        ins, outs, bufs = refs[:n], refs[n:2 * n], refs[2 * n:3 * n]
        ld_sem, st_sem, own_send, own_recv, fwd_send, fwd_recv, d2d_send, d2d_recv = refs[3 * n:]
        x, y, c, _ = _place()
        sibling = (x, y, 1 - c)
        mine = 2 * x + y
        first = (x ^ (1 - c), y ^ c)
        second = (x ^ c, y ^ (1 - c))
        diagonal = (1 - x, 1 - y)

        def slot(chip):
            return 2 * chip[0] + chip[1]

        def remote(src, dst, send, recv, device):
            return pltpu.make_async_remote_copy(src_ref=src, dst_ref=dst, send_sem=send, recv_sem=recv,
                                                device_id=device, device_id_type=MESH)

        loads = [pltpu.make_async_copy(ins[w], bufs[w], ld_sem.at[w]) for w in range(n)]
        for cp in loads:
            cp.start()
        stores, sends = [], []
        for w in range(n):
            loads[w].wait()
            cp = pltpu.make_async_copy(bufs[w], outs[w].at[:, mine], st_sem.at[w])
            cp.start()
            stores.append(cp)
        for u, (w, l) in enumerate(units):
            for k, chip in enumerate((first, second)):
                cp = remote(bufs[w].at[l, c], outs[w].at[l, mine, c], own_send.at[u, k], own_recv.at[u, k], (*chip, c))
                cp.start()
                sends.append(cp)
        for u, (w, l) in enumerate(units):
            got = outs[w].at[l, slot(first), c]
            remote(got, got, own_send.at[u, 0], own_recv.at[u, 0], (*first, c)).wait_recv()
            for cp in (remote(got, got, fwd_send.at[u], fwd_recv.at[u], (*second, c)),
                       remote(got, got, d2d_send.at[u, 0], d2d_recv.at[u, 0], sibling)):
                cp.start()
                sends.append(cp)
        for u, (w, l) in enumerate(units):
            got = outs[w].at[l, slot(second), c]
            remote(got, got, own_send.at[u, 1], own_recv.at[u, 1], (*second, c)).wait_recv()
            cp = remote(got, got, d2d_send.at[u, 1], d2d_recv.at[u, 1], sibling)
            cp.start()
            sends.append(cp)
        for u, (w, l) in enumerate(units):
            got = outs[w].at[l, slot(diagonal), c]
            remote(got, got, fwd_send.at[u], fwd_recv.at[u], (*second, c)).wait_recv()
            cp = remote(got, got, d2d_send.at[u, 2], d2d_recv.at[u, 2], sibling)
            cp.start()
            sends.append(cp)
        for u, (w, l) in enumerate(units):
            for k, chip in enumerate((second, first, diagonal)):
                got = outs[w].at[l, slot(chip), 1 - c]
                remote(got, got, d2d_send.at[u, k], d2d_recv.at[u, k], sibling).wait_recv()
        for cp in sends:
            cp.wait_send()
        for cp in stores:
            cp.wait()

    return pl.pallas_call(
        body, name="gather_weights",
        in_specs=[ANY] * n, out_specs=[ANY] * n,
        out_shape=[jax.ShapeDtypeStruct((s.shape[0], N_CHIPS) + s.shape[1:], s.dtype) for s in shards],
        scratch_shapes=[pltpu.VMEM(s.shape, s.dtype) for s in shards] +
                       [pltpu.SemaphoreType.DMA((n,)), pltpu.SemaphoreType.DMA((n,)),
                        pltpu.SemaphoreType.DMA((nu, 2)), pltpu.SemaphoreType.DMA((nu, 2)),
                        pltpu.SemaphoreType.DMA((nu,)), pltpu.SemaphoreType.DMA((nu,)),
                        pltpu.SemaphoreType.DMA((nu, 3)), pltpu.SemaphoreType.DMA((nu, 3))],
        compiler_params=pltpu.CompilerParams(vmem_limit_bytes=VMEM_LIMIT),
    )(*shards)


def send_to_sibling_other_half(grads):
    def copies(ins, outs):
        x, y, c, _ = _place()
        return [(ins[w].at[:, :, 1 - c], outs[w], (x, y, 1 - c)) for w in range(len(grads))]

    return _exchange_plan(grads, [jax.ShapeDtypeStruct(g.shape[:2] + g.shape[3:], g.dtype) for g in grads], copies,
                          len(grads))


def _core_index():
    return jnp.reshape(lax.axis_index("c"), (1,)).astype(jnp.int32)


def _chip_index():
    return jnp.reshape(2 * lax.axis_index("x") + lax.axis_index("y"), (1,)).astype(jnp.int32)


def add_own_half(g, recv, name):
    L, J, _, Rh, C = g.shape

    def body(c_ref, g_ref, r_ref, o_ref):
        o_ref[...] = (g_ref[...] + r_ref[...]).astype(BF16)

    return pl.pallas_call(
        body, name=name,
        grid_spec=pltpu.PrefetchScalarGridSpec(
            num_scalar_prefetch=1, grid=(L, J),
            in_specs=[pl.BlockSpec((None, None, None, Rh, C), lambda l, j, c_ref: (l, j, c_ref[0], 0, 0)),
                      pl.BlockSpec((None, None, Rh, C), lambda l, j, c_ref: (l, j, 0, 0))],
            out_specs=pl.BlockSpec((None, None, Rh, C), lambda l, j, c_ref: (l, j, 0, 0))),
        out_shape=jax.ShapeDtypeStruct((L, J, Rh, C), BF16),
        compiler_params=_params(2),
    )(_core_index(), g, recv)


def exchange_between_chips(sums):
    def copies(ins, outs):
        x, y, c, chips = _place()
        return [(ins[w].at[:, 2 * px + py], outs[w].at[k], (px, py, c))
                for w in range(len(sums)) for k, (px, py) in enumerate(chips)]

    return _exchange_plan(sums, [jax.ShapeDtypeStruct((3, s.shape[0]) + s.shape[2:], s.dtype) for s in sums], copies,
                          3 * len(sums))


def sum_chips(own, recv, name):
    L, _, Rh, C = own.shape

    def body(chip_ref, o_ref, r_ref, out_ref):
        acc = o_ref[...].astype(F32)
        for k in range(3):
            acc = acc + r_ref[k].astype(F32)
        out_ref[...] = acc

    return pl.pallas_call(
        body, name=name,
        grid_spec=pltpu.PrefetchScalarGridSpec(
            num_scalar_prefetch=1, grid=(L,),
            in_specs=[pl.BlockSpec((None, None, Rh, C), lambda l, chip_ref: (l, chip_ref[0], 0, 0)),
                      pl.BlockSpec((3, None, Rh, C), lambda l, chip_ref: (0, l, 0, 0))],
            out_specs=pl.BlockSpec((None, Rh, C), lambda l, chip_ref: (l, 0, 0))),
        out_shape=jax.ShapeDtypeStruct((L, Rh, C), F32),
        compiler_params=_params(1),
    )(_chip_index(), own, recv)


def share_with_sibling(halves):
    def copies(ins, outs):
        x, y, c, _ = _place()
        return [(ins[w], outs[w], (x, y, 1 - c)) for w in range(len(halves))]

    return _exchange_plan(halves, [jax.ShapeDtypeStruct(h.shape, h.dtype) for h in halves], copies, len(halves))


def adamw_group(ws, ms, vs, own, sib, name, plan=None):
    L = len(ws)
    R, C = ws[0].shape
    Rh = R // 2
    tr = Rh // 2
    nr = Rh // tr

    def body(*refs):
        w, m, v = refs[:L], refs[L:2 * L], refs[2 * L:3 * L]
        own_ref, sib_ref = refs[3 * L], refs[3 * L + 1]
        outs = refs[3 * L + 2:]
        mine = pl.program_id(0) == lax.axis_index("c")
        for l in range(L):
            gv = jnp.where(mine, own_ref[l], sib_ref[l])
            d, m2, v2 = _adamw_math(w[l][...], gv, m[l][...], v[l][...])
            outs[4 * l][...] = d
            outs[4 * l + 1][...] = m2
            outs[4 * l + 2][...] = v2
            outs[4 * l + 3][...] = gv

    blk = pl.BlockSpec((tr, C), lambda h, r: (h * nr + r, 0))
    half = pl.BlockSpec((L, tr, C), lambda h, r: (0, r, 0))
    sds = jax.ShapeDtypeStruct((R, C), F32)
    return _call(body, plan, name=name, grid=(2, nr), in_specs=[blk] * (3 * L) + [half, half],
                 out_specs=[blk] * (4 * L), out_shape=[sds] * (4 * L), args=(*ws, *ms, *vs, own, sib))


def gather_small_plan(part):
    def copies(ins, outs):
        x, y, c, _ = _place()
        me = 4 * x + 2 * y + c
        return [(ins[0], outs[0].at[me], (x ^ ((k >> 2) & 1), y ^ ((k >> 1) & 1), c ^ (k & 1))) for k in range(1, N_DEV)]

    return _exchange_plan([part], [jax.ShapeDtypeStruct((N_DEV,) + part.shape, part.dtype)], copies, N_DEV - 1)


def sum_small(part, gathered, name):
    R, C = part.shape

    def body(p_ref, g_ref, o_ref):
        d = pl.program_id(0)
        me = 4 * lax.axis_index("x") + 2 * lax.axis_index("y") + lax.axis_index("c")
        val = jnp.where(d == me, p_ref[...], g_ref[...])

        @pl.when(d == 0)
        def _():
            o_ref[...] = val

        @pl.when(d > 0)
        def _():
            o_ref[...] += val

    return pl.pallas_call(
        body, name=name, grid=(N_DEV,),
        in_specs=[pl.BlockSpec((R, C), lambda d: (0, 0)), pl.BlockSpec((None, R, C), lambda d: (d, 0, 0))],
        out_specs=pl.BlockSpec((R, C), lambda d: (0, 0)),
        out_shape=jax.ShapeDtypeStruct((R, C), F32),
        compiler_params=_params(1),
    )(part, gathered)


SMALL = ("norm_ffn1", "norm_mix", "norm_ffn2", "norm_final", "ret_gn_gain", "pool_scale", "pool_w")
GROUPS = (("ffn1", ("ffn1_gate", "ffn1_up", "ffn1_down")), ("w_in", ("w_in",)), ("w_out", ("w_out",)),
          ("ffn2", ("ffn2_gate", "ffn2_up", "ffn2_down")))
TRANSPOSED = ("ffn1_gate", "ffn1_up", "ffn2_gate", "ffn2_up")
IN_W = 2 * QK_W + 3 * V_W
ORDER = ("norm_ffn1", "ffn1_gate", "ffn1_up", "ffn1_down", "norm_mix", "w_in", "ret_gn_gain", "pool_w", "pool_scale",
         "w_out", "norm_ffn2", "ffn2_gate", "ffn2_up", "ffn2_down", "norm_final")


SUBLANES = 8


def _small_rows(shapes):
    rows = [math.prod(shapes[k]) // 128 for k in SMALL]
    offs, off = [], 0
    for r in rows:
        offs.append(off)
        off += -(-r // SUBLANES) * SUBLANES
    return rows, offs, off


def _pack_small(d, shapes):
    rows, offs, _ = _small_rows(shapes)
    pieces = []
    for k, r in zip(SMALL[1:], rows[1:]):
        pieces.append(d[k].reshape(r, 128))
        if r % SUBLANES:
            pieces.append(jnp.zeros((SUBLANES - r % SUBLANES, 128), F32))
    return jnp.concatenate(pieces, axis=0)


def adamw_small(W, M, V, first, summed, shapes):
    rows, offs, _ = _small_rows(shapes)
    n = len(SMALL)

    def body(*refs):
        w, m, v, f_ref, s_ref = refs[:n], refs[n:2 * n], refs[2 * n:3 * n], refs[3 * n], refs[3 * n + 1]
        outs = refs[3 * n + 2:]
        for i, (r, off) in enumerate(zip(rows, offs)):
            gv = f_ref[...] if i == 0 else s_ref[off - offs[1]:off - offs[1] + r, :]
            d, m2, v2 = _adamw_math(w[i][...], gv, m[i][...], v[i][...])
            outs[4 * i][...] = d
            outs[4 * i + 1][...] = m2
            outs[4 * i + 2][...] = v2
            outs[4 * i + 3][...] = gv

    vm = pl.BlockSpec(memory_space=pltpu.VMEM)
    flat = lambda d: [d[k].reshape(r, 128) for k, r in zip(SMALL, rows)]
    res = pl.pallas_call(
        body, name="adamw_small", in_specs=[vm] * (3 * n + 2), out_specs=[vm] * (4 * n),
        out_shape=[jax.ShapeDtypeStruct((r, 128), F32) for r in rows for _ in range(4)],
    )(*flat(W), *flat(M), *flat(V), first, summed)
    return {k: [a.reshape(shapes[k]) for a in res[4 * i:4 * i + 4]] for i, k in enumerate(SMALL)}


def kernel(x, norm_ffn1, ffn1_gate, ffn1_up, ffn1_down, norm_mix, w_in, ret_gn_gain, pool_w, pool_scale, w_out, norm_ffn2, ffn2_gate, ffn2_up, ffn2_down, norm_final, loss_target, m_norm_ffn1, m_ffn1_gate, m_ffn1_up, m_ffn1_down, m_norm_mix, m_w_in, m_ret_gn_gain, m_pool_w, m_pool_scale, m_w_out, m_norm_ffn2, m_ffn2_gate, m_ffn2_up, m_ffn2_down, m_norm_final, v_norm_ffn1, v_ffn1_gate, v_ffn1_up, v_ffn1_down, v_norm_mix, v_w_in, v_ret_gn_gain, v_pool_w, v_pool_scale, v_w_out, v_norm_ffn2, v_ffn2_gate, v_ffn2_up, v_ffn2_down, v_norm_final):
    W = dict(norm_ffn1=norm_ffn1, ffn1_gate=ffn1_gate, ffn1_up=ffn1_up, ffn1_down=ffn1_down, norm_mix=norm_mix, w_in=w_in,
             ret_gn_gain=ret_gn_gain, pool_w=pool_w, pool_scale=pool_scale, w_out=w_out, norm_ffn2=norm_ffn2,
             ffn2_gate=ffn2_gate, ffn2_up=ffn2_up, ffn2_down=ffn2_down, norm_final=norm_final)
    M = dict(norm_ffn1=m_norm_ffn1, ffn1_gate=m_ffn1_gate, ffn1_up=m_ffn1_up, ffn1_down=m_ffn1_down, norm_mix=m_norm_mix,
             w_in=m_w_in, ret_gn_gain=m_ret_gn_gain, pool_w=m_pool_w, pool_scale=m_pool_scale, w_out=m_w_out,
             norm_ffn2=m_norm_ffn2, ffn2_gate=m_ffn2_gate, ffn2_up=m_ffn2_up, ffn2_down=m_ffn2_down, norm_final=m_norm_final)
    V = dict(norm_ffn1=v_norm_ffn1, ffn1_gate=v_ffn1_gate, ffn1_up=v_ffn1_up, ffn1_down=v_ffn1_down, norm_mix=v_norm_mix,
             w_in=v_w_in, ret_gn_gain=v_ret_gn_gain, pool_w=v_pool_w, pool_scale=v_pool_scale, w_out=v_w_out,
             norm_ffn2=v_norm_ffn2, ffn2_gate=v_ffn2_gate, ffn2_up=v_ffn2_up, ffn2_down=v_ffn2_down, norm_final=v_norm_final)

    n_seq, S, D = x.shape
    T = n_seq * S
    tm = min(1024, T // 2)
    tk = min(1024, T // 2)
    tk1 = min(2048, T // 2)
    xf = x.reshape(T, D)
    tgt = loss_target.reshape(T, D)

    def local(d, k):
        a = d[k][0]
        return a.T if k in TRANSPOSED else a

    def to_out(k, a):
        return (a.T if k in TRANSPOSED else a)[None]

    loc_bf = []
    for _, members in GROUPS:
        st = jnp.stack([local(W, k).astype(BF16) for k in members])
        loc_bf.append(st.reshape(st.shape[0], 2, st.shape[1] // 2, st.shape[2]))
    FS = loc_bf[0].shape[2] * 2
    w3_1 = gather_weights(loc_bf[:1])[0].reshape(3, N_CHIPS, FS, D)
    g1 = norm_ffn1.reshape(1, D)
    gm = norm_mix.reshape(1, D)
    g3 = norm_ffn2.reshape(1, D)
    gf = norm_final.reshape(1, D)
    gn_gain = ret_gn_gain.reshape(1, V_W)
    pscale = pool_scale.reshape(1, V_W)
    pw = pool_w.reshape(len(POOL_WINDOWS), POOL_GC, POOL_GC)
    consts = mixer_constants(S)

    def halves5(g, lb):
        return g.reshape(lb.shape[0], N_CHIPS, 2, lb.shape[2], lb.shape[3])

    (x1, b1, si1, t1, n1), landed = ffn_fwd(xf, g1, w3_1, tm, "ffn1_fwd", plan=gather_plan(loc_bf[1:]))
    gathered = [place_own(g, lb, "place_own_" + gn) for g, lb, (gn, _) in zip(landed, loc_bf[1:], GROUPS[1:])]
    win_full = gathered[0].reshape(N_CHIPS, D, IN_W // N_CHIPS)
    wout_full = gathered[1].reshape(D, D)
    w3_2 = gathered[2].reshape(3, N_CHIPS, FS, D)
    p = in_proj(x1, gm, win_full, tm, "in_proj")
    x2, m_s, o_s, st_s, pooled_s = mixer_fwd(p, x1, consts, gn_gain, pw, pscale, wout_full, n_seq, S, "mixer_fwd")
    x3, b2, si2, t2, n3 = ffn_fwd(x2, g3, w3_2, tm, "ffn2_fwd")

    dx3, df2, lacc, dgf = loss_bwd(x3, tgt, gf, tm, "loss_bwd")
    dn3, da2, db2 = ffn_bwd(df2, b2, si2, t2, w3_2, tm, "ffn2_bwd")
    to_sib, to_chips, back = send_to_sibling_other_half, exchange_between_chips, share_with_sibling
    out_g, out_d, out_m, out_v = {}, {}, {}, {}

    def update(members, own, sib, name):
        res = adamw_group([local(W, k) for k in members], [local(M, k) for k in members], [local(V, k) for k in members],
                          own, sib, "adamw_" + name)
        for l, k in enumerate(members):
            out_d[k], out_m[k], out_v[k], out_g[k] = (to_out(k, r) for r in res[4 * l:4 * l + 4])

    (gr_ffn2,) = ffn_wgrad(da2, db2, si2, b2, n3, df2, tk, "ffn2_wgrad")
    g_ffn2 = halves5(gr_ffn2, loc_bf[3])
    (dx1, df1, dx2b, dp, n2, dg3, dgm, dgn, dps, dpw), (sb_ffn2,) = mixer_bwd(
        dx3, dn3, x2, g3, x1, p, o_s, st_s, pooled_s, consts, gm, gn_gain, pw, pscale, wout_full, win_full, n_seq, S,
        "mixer_bwd", plan=to_sib([g_ffn2]))
    cs_ffn2 = add_own_half(g_ffn2, sb_ffn2, "add_sibling_ffn2")
    (gr_wout,) = tn_shared(m_s, dx2b[None], tk1, "dw_out")
    (gr_win,) = tn_shared(n2, dp, tk1, "dw_in")
    g_mix = [halves5(gr_win, loc_bf[1]), halves5(gr_wout, loc_bf[2])]
    shapes = {k: W[k].shape for k in SMALL}
    small_part = dict(norm_mix=dgm, norm_ffn2=dg3, norm_final=dgf, ret_gn_gain=dgn, pool_scale=dps, pool_w=dpw)
    part = jnp.concatenate([_pack_small(small_part, shapes), lacc.reshape(-1, 128)], axis=0)
    n_small = part.shape[0] - lacc.size // 128
    (dn1, da1, db1), (pc_ffn2, sb_win, sb_wout, parts) = ffn_bwd(
        df1, b1, si1, t1, w3_1, tm, "ffn1_bwd",
        plan=_combine(_combine(to_chips([cs_ffn2]), to_sib(g_mix)), gather_small_plan(part)))
    summed = sum_small(part, parts, "sum_small")
    loss = jnp.sum(summed[n_small:]) * (0.5 / D)
    mh_ffn2 = sum_chips(cs_ffn2, pc_ffn2, "sum_chips_ffn2")
    cs_mix = [add_own_half(g, r, "add_sibling_" + gn) for g, r, gn in zip(g_mix, (sb_win, sb_wout), ("w_in", "w_out"))]
    def one(g):
        return g.reshape((1, N_CHIPS, 2) + loc_bf[0].shape[2:])

    (gr_g,), (sh_ffn2, pc_win, pc_wout) = wgrad_one(
        da1, n1, tk1, "ffn1_wgrad_gate", plan=_combine(back([mh_ffn2]), to_chips(cs_mix)))
    mh_mix = [sum_chips(cs, pc, "sum_chips_" + gn) for cs, pc, gn in zip(cs_mix, (pc_win, pc_wout), ("w_in", "w_out"))]
    (gr_u,), (sb_g,) = wgrad_one(db1, n1, tk1, "ffn1_wgrad_up", plan=to_sib([one(gr_g)]))
    cs_g = add_own_half(one(gr_g), sb_g, "add_sibling_ffn1_gate")
    (gr_d,), (pc_g, sb_u) = wgrad_one(
        si1, df1, tk1, "ffn1_wgrad_down", plan=_combine(to_chips([cs_g]), to_sib([one(gr_u)])), a2=b1)
    mh_g = sum_chips(cs_g, pc_g, "sum_chips_ffn1_gate")
    cs_u = add_own_half(one(gr_u), sb_u, "add_sibling_ffn1_up")
    (dx0, dg1), (sh_g, sh_win, sh_wout, pc_u, sb_d) = rms_bwd(
        xf, dx1, dn1, g1, tm, "ffn1_rms_bwd",
        plan=_combine(_combine(back([mh_g] + mh_mix), to_chips([cs_u])), to_sib([one(gr_d)])))
    mh_u = sum_chips(cs_u, pc_u, "sum_chips_ffn1_up")
    cs_d = add_own_half(one(gr_d), sb_d, "add_sibling_ffn1_down")
    first = dg1.reshape(-1, 128)
    sh_u, pc_d, firsts = _run_plan(
        _combine(_combine(back([mh_u]), to_chips([cs_d])), gather_small_plan(first)), "ffn1_grads_stage_b")
    mh_d = sum_chips(cs_d, pc_d, "sum_chips_ffn1_down")
    (sh_d,) = _run_plan(back([mh_d]), "ffn1_grads_stage_c")
    first_sum = sum_small(first, firsts, "sum_small_first")
    for k, own, sib in zip(GROUPS[0][1], (mh_g, mh_u, mh_d), (sh_g, sh_u, sh_d)):
        update((k,), own, sib, k)
    update(GROUPS[1][1], mh_mix[0], sh_win, "w_in")
    update(GROUPS[2][1], mh_mix[1], sh_wout, "w_out")
    update(GROUPS[3][1], mh_ffn2, sh_ffn2, "ffn2")

    for k, (d_, m_, v_, g_) in adamw_small(W, M, V, first_sum, summed, shapes).items():
        out_d[k], out_m[k], out_v[k], out_g[k] = d_, m_, v_, g_

    grad_x = dx0.reshape(n_seq, S, D)
    return (loss, grad_x, *[out_g[k] for k in ORDER], *[out_d[k] for k in ORDER],
            *[out_m[k] for k in ORDER], *[out_v[k] for k in ORDER])
```

```python
import functools
import math
from typing import Callable, NamedTuple, Optional

import jax
import jax.numpy as jnp
from jax import lax
from jax.experimental import pallas as pl
from jax.experimental.pallas import tpu as pltpu

F32 = jnp.float32
BF16 = jnp.bfloat16
MESH = pl.DeviceIdType.MESH

N_CHIPS = 4
N_DEV = 8
CHUNK = 64
RET_HEADS = 4
RET_DK = 64
RET_DV = 128
QK_W = RET_HEADS * RET_DK
V_W = RET_HEADS * RET_DV
POOL_WINDOWS = (2, 4, 8, 16)
POOL_GC = 128
ROPE_BASE = 10000.0
RMS_EPS = 1e-6
GN_EPS = 1e-5
ADAM_LR = 0.001
ADAM_B1 = 0.9
ADAM_B2 = 0.999
ADAM_EPS = 1e-08
ADAM_WD = 0.01
ADAM_STEP = 10
MXU_W = 256
RET_BLOCK = MXU_W
VMEM_LIMIT = 56 * 1024 * 1024

NN = (((1,), (0,)), ((), ()))
NT = (((1,), (1,)), ((), ()))
TN = (((0,), (0,)), ((), ()))


def _dot(a, b, dims=NN):
    return lax.dot_general(a, b, dims, preferred_element_type=F32)


def _sigmoid(x):
    return 0.5 * jnp.tanh(0.5 * x) + 0.5


def _params(n_grid):
    return pltpu.CompilerParams(dimension_semantics=("arbitrary",) * n_grid, vmem_limit_bytes=VMEM_LIMIT)


class CommPlan(NamedTuple):
    ins: tuple
    out_shape: tuple
    sems: tuple
    start: Callable
    finish: Callable
    mid: Optional[Callable] = None
    mid_at: float = 0.85


ANY = pl.BlockSpec(memory_space=pl.ANY)


def _call(body, plan, *, name, grid, in_specs, out_specs, out_shape, args, scratch_shapes=()):
    n_grid = len(grid)
    if plan is None:
        return pl.pallas_call(body, name=name, grid=grid, in_specs=in_specs, out_specs=out_specs, out_shape=out_shape,
                              scratch_shapes=scratch_shapes, compiler_params=_params(n_grid))(*args)
    n_in, n_out, n_sc = len(in_specs), len(out_specs), len(scratch_shapes)
    p_in, p_out = len(plan.ins), len(plan.out_shape)
    total = math.prod(grid)
    mid_step = min(total - 1, int(plan.mid_at * total))

    def riding(*refs):
        ins, pins = refs[:n_in], refs[n_in:n_in + p_in]
        o0 = n_in + p_in
        outs, pouts = refs[o0:o0 + n_out], refs[o0 + n_out:o0 + n_out + p_out]
        s0 = o0 + n_out + p_out
        scratch, psems = refs[s0:s0 + n_sc], refs[s0 + n_sc:]
        step = pl.program_id(0)
        for d in range(1, n_grid):
            step = step * grid[d] + pl.program_id(d)

        @pl.when(step == 0)
        def _():
            plan.start(pins, pouts, psems)

        body(*ins, *outs, *scratch)

        if plan.mid is not None:
            @pl.when(step == mid_step)
            def _():
                plan.mid(pins, pouts, psems)

        @pl.when(step == total - 1)
        def _():
            plan.finish(pins, pouts, psems)

    res = pl.pallas_call(
        riding, name=name, grid=grid, in_specs=list(in_specs) + [ANY] * p_in, out_specs=list(out_specs) + [ANY] * p_out,
        out_shape=list(out_shape) + list(plan.out_shape), scratch_shapes=list(scratch_shapes) + list(plan.sems),
        compiler_params=_params(n_grid))(*args, *plan.ins)
    return res[:n_out], res[n_out:]


def _run_plan(plan, name):
    p_in, p_out = len(plan.ins), len(plan.out_shape)

    def body(*refs):
        pins, pouts, psems = refs[:p_in], refs[p_in:p_in + p_out], refs[p_in + p_out:]
        plan.start(pins, pouts, psems)
        if plan.mid is not None:
            plan.mid(pins, pouts, psems)
        plan.finish(pins, pouts, psems)

    return pl.pallas_call(body, name=name, in_specs=[ANY] * p_in, out_specs=[ANY] * p_out,
                          out_shape=list(plan.out_shape), scratch_shapes=list(plan.sems))(*plan.ins)


def _rms_parts(x):
    r = lax.rsqrt(jnp.mean(x * x, axis=-1, keepdims=True) + RMS_EPS)
    return r, x * r


def _rms_bwd(dn, xhat, r, gain, dres):
    dxh = dn * gain
    return dres + r * (dxh - xhat * jnp.mean(dxh * xhat, axis=-1, keepdims=True))


def _w3_specs(FS, D):
    return [pl.BlockSpec((None, None, FS, D), functools.partial(lambda i, j, k: (k, j, 0, 0), k=k)) for k in range(3)]


def ffn_fwd(x, gain, w3, tm, name, plan=None):
    T, D = x.shape
    _, J, FS, _ = w3.shape

    def body(x_ref, g_ref, wg_ref, wu_ref, wd_ref, xo_ref, b_ref, si_ref, t_ref, n_ref, acc):
        j = pl.program_id(1)

        @pl.when(j == 0)
        def _():
            _, xhat = _rms_parts(x_ref[...])
            n_ref[...] = (xhat * g_ref[...]).astype(BF16)
            acc[...] = jnp.zeros_like(acc)

        n = n_ref[...]
        a = _dot(n, wg_ref[...], NT)
        b = _dot(n, wu_ref[...], NT)
        sg = _sigmoid(a)
        si = a * sg
        b_ref[...] = b.astype(BF16)
        si_ref[...] = si.astype(BF16)
        t_ref[...] = (sg + si * (1.0 - sg)).astype(BF16)
        acc[...] += _dot((si * b).astype(BF16), wd_ref[...])

        @pl.when(j == J - 1)
        def _():
            xo_ref[...] = x_ref[...] + 0.5 * acc[...]

    return _call(
        body, plan, name=name, grid=(T // tm, J),
        in_specs=[pl.BlockSpec((tm, D), lambda i, j: (i, 0)),
                  pl.BlockSpec((1, D), lambda i, j: (0, 0))] + _w3_specs(FS, D),
        out_specs=[pl.BlockSpec((tm, D), lambda i, j: (i, 0))] +
                  [pl.BlockSpec((None, tm, FS), lambda i, j: (j, i, 0))] * 3 +
                  [pl.BlockSpec((tm, D), lambda i, j: (i, 0))],
        out_shape=[jax.ShapeDtypeStruct((T, D), F32)] + [jax.ShapeDtypeStruct((J, T, FS), BF16)] * 3 +
                  [jax.ShapeDtypeStruct((T, D), BF16)],
        scratch_shapes=[pltpu.VMEM((tm, D), F32)],
        args=(x, gain, w3, w3, w3))


def ffn_bwd(df, b_s, si_s, t_s, w3, tm, name, plan=None):
    T, D = df.shape
    _, J, FS, _ = w3.shape

    chunks = [slice(s, min(s + MXU_W, FS)) for s in range(0, FS, MXU_W)]
    nc = len(chunks)

    def body(df_ref, b_ref, si_ref, t_ref, wg_ref, wu_ref, wd_ref, dn_ref, da_ref, db_ref):
        @pl.when(pl.program_id(1) == 0)
        def _():
            dn_ref[...] = jnp.zeros_like(dn_ref)

        def elementwise(dh, rows, sl):
            da = (dh * b_ref[rows, sl].astype(F32) * t_ref[rows, sl].astype(F32)).astype(BF16)
            db = (dh * si_ref[rows, sl].astype(F32)).astype(BF16)
            da_ref[rows, sl] = da
            db_ref[rows, sl] = db
            return da, db

        for r0 in range(0, tm, tm // 2):
            rows = slice(r0, r0 + tm // 2)
            df = df_ref[rows, :]
            dh, dab, acc = [None] * nc, [None] * nc, None
            for step in range(nc + 2):
                if step < nc:
                    dh[step] = _dot(df, wd_ref[chunks[step], :], NT)
                if 1 <= step <= nc:
                    dab[step - 1] = elementwise(dh[step - 1], rows, chunks[step - 1])
                if step >= 2:
                    sl = chunks[step - 2]
                    part = _dot(dab[step - 2][0], wg_ref[sl, :]) + _dot(dab[step - 2][1], wu_ref[sl, :])
                    acc = part if acc is None else acc + part
            dn_ref[rows, :] += acc

    tok = pl.BlockSpec((tm, D), lambda i, j: (i, 0))
    sh = pl.BlockSpec((None, tm, FS), lambda i, j: (j, i, 0))
    return _call(
        body, plan, name=name, grid=(T // tm, J),
        in_specs=[tok, sh, sh, sh] + _w3_specs(FS, D),
        out_specs=[tok, sh, sh],
        out_shape=[jax.ShapeDtypeStruct((T, D), F32),
                   jax.ShapeDtypeStruct((J, T, FS), BF16),
                   jax.ShapeDtypeStruct((J, T, FS), BF16)],
        args=(df, b_s, si_s, t_s, w3, w3, w3))


def rms_bwd(x, dout, dn, gain, tm, name, plan=None):
    T, D = x.shape

    def body(x_ref, do_ref, dn_ref, g_ref, dx_ref, dgain_ref):
        @pl.when(pl.program_id(0) == 0)
        def _():
            dgain_ref[...] = jnp.zeros_like(dgain_ref)

        r, xhat = _rms_parts(x_ref[...])
        dn = dn_ref[...]
        dgain_ref[...] += jnp.sum(dn * xhat, axis=0, keepdims=True)
        dx_ref[...] = _rms_bwd(dn, xhat, r, g_ref[...], do_ref[...])

    tok = pl.BlockSpec((tm, D), lambda i: (i, 0))
    vec = pl.BlockSpec((1, D), lambda i: (0, 0))
    return _call(
        body, plan, name=name, grid=(T // tm,), in_specs=[tok, tok, tok, vec], out_specs=[tok, vec],
        out_shape=[jax.ShapeDtypeStruct((T, D), F32), jax.ShapeDtypeStruct((1, D), F32)],
        args=(x, dout, dn, gain))


def ffn_wgrad(da, db, si, b, n, df, tk, name, plan=None):
    J, T, FS = da.shape
    D = n.shape[1]

    def body(da_ref, db_ref, si_ref, b_ref, n_ref, df_ref, o_ref):
        @pl.when(pl.program_id(1) == 0)
        def _():
            o_ref[...] = jnp.zeros_like(o_ref)

        nv = n_ref[...]
        o_ref[0] += _dot(da_ref[...], nv, TN)
        o_ref[1] += _dot(db_ref[...], nv, TN)
        o_ref[2] += _dot(si_ref[...] * b_ref[...], df_ref[...], TN)

    sh = pl.BlockSpec((None, tk, FS), lambda j, k: (j, k, 0))
    tok = pl.BlockSpec((tk, D), lambda j, k: (k, 0))
    return _call(
        body, plan, name=name, grid=(J, T // tk),
        in_specs=[sh, sh, sh, sh, tok, tok],
        out_specs=[pl.BlockSpec((3, None, FS, D), lambda j, k: (0, j, 0, 0))],
        out_shape=[jax.ShapeDtypeStruct((3, J, FS, D), F32)],
        args=(da, db, si, b, n, df))


def wgrad_one(a, b, tk, name, plan=None, a2=None):
    J, T, FS = a.shape
    D = b.shape[1]
    lhs = (a,) if a2 is None else (a, a2)

    def body(*refs):
        o_ref = refs[-1]

        @pl.when(pl.program_id(1) == 0)
        def _():
            o_ref[...] = jnp.zeros_like(o_ref)

        av = refs[0][...] if a2 is None else refs[0][...] * refs[1][...]
        o_ref[...] += _dot(av, refs[-2][...], TN)

    return _call(
        body, plan, name=name, grid=(J, T // tk),
        in_specs=[pl.BlockSpec((None, tk, FS), lambda j, k: (j, k, 0))] * len(lhs) +
                 [pl.BlockSpec((tk, D), lambda j, k: (k, 0))],
        out_specs=[pl.BlockSpec((None, FS, D), lambda j, k: (j, 0, 0))],
        out_shape=[jax.ShapeDtypeStruct((J, FS, D), F32)],
        args=(*lhs, b))


def tn_shared(a, b, tk, name, plan=None):
    T, M = a.shape
    J, _, N = b.shape

    def body(a_ref, b_ref, o_ref):
        @pl.when(pl.program_id(0) == 0)
        def _():
            o_ref[...] = jnp.zeros_like(o_ref)

        a_t = a_ref[...].astype(BF16).T
        for j in range(J):
            o_ref[j] += _dot(a_t, b_ref[j].astype(BF16))

    return _call(
        body, plan, name=name, grid=(T // tk,),
        in_specs=[pl.BlockSpec((tk, M), lambda k: (k, 0)), pl.BlockSpec((J, tk, N), lambda k: (0, k, 0))],
        out_specs=[pl.BlockSpec((J, M, N), lambda k: (0, 0, 0))],
        out_shape=[jax.ShapeDtypeStruct((J, M, N), F32)],
        args=(a, b))


def in_proj(x, gain, win, tm, name):
    T, D = x.shape
    J, _, W = win.shape

    def body(x_ref, g_ref, w_ref, p_ref):
        _, xhat = _rms_parts(x_ref[...])
        n = (xhat * g_ref[...]).astype(BF16)
        for j in range(J):
            p_ref[j] = _dot(n, w_ref[j])

    return pl.pallas_call(
        body, name=name, grid=(T // tm,),
        in_specs=[pl.BlockSpec((tm, D), lambda i: (i, 0)),
                  pl.BlockSpec((1, D), lambda i: (0, 0)),
                  pl.BlockSpec((J, D, W), lambda i: (0, 0, 0))],
        out_specs=pl.BlockSpec((J, tm, W), lambda i: (0, i, 0)),
        out_shape=jax.ShapeDtypeStruct((J, T, W), F32),
        compiler_params=_params(1),
    )(x, gain, win)


def mixer_constants(S):
    B = RET_BLOCK
    half = RET_DK // 2
    freqs = ROPE_BASE ** (-jnp.arange(half, dtype=F32) * 2.0 / RET_DK)
    ang = jnp.arange(S, dtype=F32)[:, None] * freqs[None, :]
    cos = jnp.tile(jnp.cos(ang), (1, 2 * RET_HEADS))
    sin = jnp.tile(jnp.concatenate([-jnp.sin(ang), jnp.sin(ang)], axis=1), (1, RET_HEADS))
    gamma = 1.0 - 2.0 ** (-5.0 - jnp.arange(RET_HEADS, dtype=F32))
    log_g = jnp.log(gamma)
    idx = jnp.arange(B, dtype=F32)
    ci = jnp.arange(B) // CHUNK
    dist = jnp.abs(idx[:, None] - idx[None, :])
    m_intra = jnp.exp(log_g[:, None, None] * dist[None]) * (ci[None, :] <= ci[:, None])[None].astype(F32)
    lg_lane = jnp.repeat(log_g, RET_DK)
    d_q = jnp.exp(lg_lane[None, :] * (idx[:, None] + 1.0))
    d_k = jnp.exp(lg_lane[None, :] * (B - 1.0 - idx[:, None]))
    d_s = jnp.broadcast_to(jnp.exp(lg_lane * B)[:, None], (QK_W, V_W))
    bm = (jnp.arange(QK_W)[:, None] // RET_DK == jnp.arange(V_W)[None, :] // RET_DV).astype(F32)
    win = jnp.repeat(jnp.array(POOL_WINDOWS, F32), POOL_GC)
    invc = 1.0 / jnp.minimum(jnp.arange(S, dtype=F32)[:, None] + 1.0, win[None, :])
    return dict(cos=cos, sin=sin, m=m_intra, dq=d_q, dk=d_k, ds=d_s, bm=bm, invc=invc)


def _swap_halves(x):
    w = x.shape[1]
    lane = lax.broadcasted_iota(jnp.int32, x.shape, 1)
    return jnp.where((lane % RET_DK) < RET_DK // 2, pltpu.roll(x, w - RET_DK // 2, 1), pltpu.roll(x, RET_DK // 2, 1))


def _rot(x, cos, sin):
    return x * cos + _swap_halves(x) * sin


def _rot_t(d, cos, sin):
    return d * cos + _swap_halves(d * sin)


def _lane_groups(parts):
    return jnp.concatenate([p[:, POOL_GC * g:POOL_GC * (g + 1)] for g, p in enumerate(parts)], axis=1)


def _head_mask(shape, h):
    lane = lax.broadcasted_iota(jnp.int32, shape, 1)
    return (lane // RET_DK) == h


def _group_norm(o):
    yh, rs = [], []
    for h in range(RET_HEADS):
        oh = o[:, RET_DV * h:RET_DV * (h + 1)]
        xc = oh - jnp.mean(oh, axis=-1, keepdims=True)
        r = lax.rsqrt(jnp.mean(xc * xc, axis=-1, keepdims=True) + GN_EPS)
        yh.append(xc * r)
        rs.append(r)
    return yh, rs


def mixer_fwd(p, x1, consts, gn_gain, pool_w, pool_scale, wout, n_seq, S, name):
    _, T, _ = p.shape
    D = x1.shape[1]
    B = RET_BLOCK
    nb = S // B

    def body(p_ref, x1_ref, cos_ref, sin_ref, invc_ref, m_ref, dq_ref, dk_ref, ds_ref, bm_ref,
             gain_ref, pw_ref, sc_ref, wout_ref,
             x2_ref, m_out, o_out, st_out, pooled_out, state, prev_u):
        blk = pl.program_id(1)

        @pl.when(blk == 0)
        def _():
            state[...] = jnp.zeros_like(state)
            prev_u[...] = jnp.zeros_like(prev_u)

        qk = p_ref[0]
        v = p_ref[1]
        g = p_ref[2]
        u = p_ref[3]
        cos = cos_ref[...]
        sin = sin_ref[...]
        qr = _rot(qk[:, :QK_W], cos, sin) * (RET_DK ** -0.5)
        kr = _rot(qk[:, QK_W:], cos, sin)
        qb = qr.astype(BF16)
        kb = kr.astype(BF16)
        vb = v.astype(BF16)
        st = state[...]
        st_out[...] = st
        cross = _dot((qr * dq_ref[...]).astype(BF16), st.astype(BF16))
        o_parts = []
        for h in range(RET_HEADS):
            qm = jnp.where(_head_mask(qb.shape, h), qb, jnp.zeros_like(qb))
            sc = (_dot(qm, kb, NT) * m_ref[h]).astype(BF16)
            o_parts.append(_dot(sc, vb[:, RET_DV * h:RET_DV * (h + 1)]) + cross[:, RET_DV * h:RET_DV * (h + 1)])
        o = jnp.concatenate(o_parts, axis=1)
        o_out[...] = o
        kv = _dot((kr * dk_ref[...]).astype(BF16), vb, TN)
        state[...] = st * ds_ref[...] + kv * bm_ref[...]

        yh, _ = _group_norm(o)
        r = g * _sigmoid(g) * (jnp.concatenate(yh, axis=1) * gain_ref[...])

        ext = jnp.concatenate([prev_u[...], u], axis=0)
        sums = []
        run = ext
        for k in (1, 2, 4, 8):
            run = run + pltpu.roll(run, k, 0)
            sums.append(run[B:, :])
        prev_u[...] = u
        pooled = (_lane_groups(sums) * invc_ref[...] - u).astype(BF16)
        pooled_out[...] = pooled
        yp = [_dot(pooled[:, POOL_GC * gi:POOL_GC * (gi + 1)], pw_ref[gi].astype(BF16)) for gi in range(len(POOL_WINDOWS))]
        s = jnp.concatenate(yp, axis=1) * sc_ref[...]
        m = jnp.concatenate([r, s], axis=1).astype(BF16)
        m_out[...] = m
        x2_ref[...] = x1_ref[...] + _dot(m, wout_ref[...])

    tokmap = lambda b, k: (b * nb + k, 0)
    seqmap = lambda b, k: (k, 0)
    const2 = lambda b, k: (0, 0)
    const3 = lambda b, k: (0, 0, 0)
    return pl.pallas_call(
        body, name=name, grid=(n_seq, nb),
        in_specs=[pl.BlockSpec((4, B, V_W), lambda b, k: (0, b * nb + k, 0)),
                  pl.BlockSpec((B, D), tokmap),
                  pl.BlockSpec((B, QK_W), seqmap), pl.BlockSpec((B, QK_W), seqmap), pl.BlockSpec((B, V_W), seqmap),
                  pl.BlockSpec((RET_HEADS, B, B), const3),
                  pl.BlockSpec((B, QK_W), const2), pl.BlockSpec((B, QK_W), const2),
                  pl.BlockSpec((QK_W, V_W), const2), pl.BlockSpec((QK_W, V_W), const2),
                  pl.BlockSpec((1, V_W), const2), pl.BlockSpec((4, POOL_GC, POOL_GC), const3),
                  pl.BlockSpec((1, V_W), const2), pl.BlockSpec((D, D), const2)],
        out_specs=[pl.BlockSpec((B, D), tokmap), pl.BlockSpec((B, D), tokmap), pl.BlockSpec((B, V_W), tokmap),
                   pl.BlockSpec((None, QK_W, V_W), lambda b, k: (b * nb + k, 0, 0)),
                   pl.BlockSpec((B, V_W), tokmap)],
        out_shape=[jax.ShapeDtypeStruct((T, D), F32), jax.ShapeDtypeStruct((T, D), BF16),
                   jax.ShapeDtypeStruct((T, V_W), F32), jax.ShapeDtypeStruct((T // B, QK_W, V_W), F32),
                   jax.ShapeDtypeStruct((T, V_W), BF16)],
        scratch_shapes=[pltpu.VMEM((QK_W, V_W), F32), pltpu.VMEM((B, V_W), F32)],
        compiler_params=_params(2),
    )(p, x1, consts["cos"], consts["sin"], consts["invc"], consts["m"], consts["dq"], consts["dk"],
      consts["ds"], consts["bm"], gn_gain, pool_w, pool_scale, wout)


def mixer_bwd(dx3, dn3, x2, g3, x1, p, o_s, st_s, pooled_s, consts, gmix, gn_gain, pool_w, pool_scale, wout, win,
              n_seq, S, name, plan=None):
    T, D = x1.shape
    B = RET_BLOCK
    nb = S // B
    NG = len(POOL_WINDOWS)

    def body(dx3_ref, dn3_ref, x2_ref, g3_ref, x1_ref, p_ref, o_ref, st_ref, pooled_ref, cos_ref, sin_ref, invc_ref,
             m_ref, dq_ref, dk_ref, ds_ref, bm_ref, gmix_ref, gain_ref, pw_ref, sc_ref, wout_ref, win_ref,
             dx1_ref, df1_ref, dx2_ref, dp_ref, n2_ref, dg3_ref, dgmix_ref, dgain_ref, dscale_ref, dpw_ref,
             gstate, next_e):
        b = pl.program_id(0)
        blk = pl.program_id(1)

        @pl.when(jnp.logical_and(b == 0, blk == 0))
        def _():
            dg3_ref[...] = jnp.zeros_like(dg3_ref)
            dgmix_ref[...] = jnp.zeros_like(dgmix_ref)
            dgain_ref[...] = jnp.zeros_like(dgain_ref)
            dscale_ref[...] = jnp.zeros_like(dscale_ref)
            dpw_ref[...] = jnp.zeros_like(dpw_ref)

        @pl.when(blk == 0)
        def _():
            gstate[...] = jnp.zeros_like(gstate)
            next_e[...] = jnp.zeros_like(next_e)

        r2, xhat2 = _rms_parts(x2_ref[...])
        dn3 = dn3_ref[...]
        dg3_ref[...] += jnp.sum(dn3 * xhat2, axis=0, keepdims=True)
        dx2v = _rms_bwd(dn3, xhat2, r2, g3_ref[...], dx3_ref[...])
        dx2b = dx2v.astype(BF16)
        dx2_ref[...] = dx2b
        dm = _dot(dx2b, wout_ref[...], NT)
        dr = dm[:, :V_W]
        dsv = dm[:, V_W:]

        pooled = pooled_ref[...]
        scale = sc_ref[...]
        dyp = (dsv * scale).astype(BF16)
        yp, dpl = [], []
        for gi in range(NG):
            sl = slice(POOL_GC * gi, POOL_GC * (gi + 1))
            pwb = pw_ref[gi].astype(BF16)
            yp.append(_dot(pooled[:, sl], pwb))
            dpw_ref[gi] += _dot(pooled[:, sl], dyp[:, sl], TN)
            dpl.append(_dot(dyp[:, sl], pwb, NT))
        dscale_ref[...] += jnp.sum(dsv * jnp.concatenate(yp, axis=1), axis=0, keepdims=True)
        dpooled = jnp.concatenate(dpl, axis=1)
        e = dpooled * invc_ref[...]
        ext = jnp.concatenate([e, next_e[...]], axis=0)
        sums = []
        run = ext
        for k in (1, 2, 4, 8):
            run = run + pltpu.roll(run, 2 * B - k, 0)
            sums.append(run[:B, :])
        next_e[...] = e
        du = _lane_groups(sums) - dpooled

        qk = p_ref[0]
        v = p_ref[1]
        g = p_ref[2]
        sg = _sigmoid(g)
        si = g * sg
        yh, rs = _group_norm(o_ref[...])
        yhat = jnp.concatenate(yh, axis=1)
        gain = gain_ref[...]
        dg = dr * (yhat * gain) * (sg * (1.0 + g * (1.0 - sg)))
        dy = dr * si
        dgain_ref[...] += jnp.sum(dy * yhat, axis=0, keepdims=True)
        dyh = dy * gain
        do_parts = []
        for h in range(RET_HEADS):
            sl = slice(RET_DV * h, RET_DV * (h + 1))
            dh_ = dyh[:, sl]
            m1 = jnp.mean(dh_, axis=-1, keepdims=True)
            m2 = jnp.mean(dh_ * yh[h], axis=-1, keepdims=True)
            do_parts.append(rs[h] * (dh_ - m1 - yh[h] * m2))
        dob = jnp.concatenate(do_parts, axis=1).astype(BF16)

        cos = cos_ref[...]
        sin = sin_ref[...]
        qr = _rot(qk[:, :QK_W], cos, sin) * (RET_DK ** -0.5)
        kr = _rot(qk[:, QK_W:], cos, sin)
        qb = qr.astype(BF16)
        kb = kr.astype(BF16)
        vb = v.astype(BF16)
        dqd = dq_ref[...]
        dkd = dk_ref[...]
        stb = st_ref[...].astype(BF16)
        gs = gstate[...]
        gb = gs.astype(BF16)
        dqs = _dot(dob, stb, NT) * dqd
        dkr = _dot(vb, gb, NT) * dkd
        dv_cross = _dot((kr * dkd).astype(BF16), gb)
        ds_cross = _dot((qr * dqd).astype(BF16), dob, TN) * bm_ref[...]
        gstate[...] = ds_cross + gs * ds_ref[...]
        dv_parts = []
        for h in range(RET_HEADS):
            sl = slice(RET_DV * h, RET_DV * (h + 1))
            hm = _head_mask(qb.shape, h)
            qm = jnp.where(hm, qb, jnp.zeros_like(qb))
            mh = m_ref[h]
            sc = (_dot(qm, kb, NT) * mh).astype(BF16)
            dsc = (_dot(dob[:, sl], vb[:, sl], NT) * mh).astype(BF16)
            dqs = dqs + jnp.where(hm, _dot(dsc, kb), 0.0)
            dkr = dkr + jnp.where(hm, _dot(dsc, qb, TN), 0.0)
            dv_parts.append(_dot(sc, dob[:, sl], TN) + dv_cross[:, sl])
        dq = _rot_t(dqs * (RET_DK ** -0.5), cos, sin)
        dk = _rot_t(dkr, cos, sin)
        dp = [jnp.concatenate([dq, dk], axis=1).astype(BF16), jnp.concatenate(dv_parts, axis=1).astype(BF16),
              dg.astype(BF16), du.astype(BF16)]
        dn = jnp.zeros((B, D), F32)
        for jj in range(4):
            dp_ref[jj] = dp[jj]
            dn = dn + _dot(dp[jj], win_ref[jj], NT)

        x1v = x1_ref[...]
        r, xhat = _rms_parts(x1v)
        gm = gmix_ref[...]
        n2_ref[...] = (xhat * gm).astype(BF16)
        dgmix_ref[...] += jnp.sum(dn * xhat, axis=0, keepdims=True)
        dx1 = _rms_bwd(dn, xhat, r, gm, dx2v)
        dx1_ref[...] = dx1
        df1_ref[...] = (0.5 * dx1).astype(BF16)

    rev = lambda b, k: (b * nb + (nb - 1 - k), 0)
    seqrev = lambda b, k: (nb - 1 - k, 0)
    const2 = lambda b, k: (0, 0)
    const3 = lambda b, k: (0, 0, 0)
    return _call(
        body, plan, name=name, grid=(n_seq, nb),
        in_specs=[pl.BlockSpec((B, D), rev), pl.BlockSpec((B, D), rev), pl.BlockSpec((B, D), rev),
                  pl.BlockSpec((1, D), const2), pl.BlockSpec((B, D), rev),
                  pl.BlockSpec((4, B, V_W), lambda b, k: (0, b * nb + (nb - 1 - k), 0)),
                  pl.BlockSpec((B, V_W), rev),
                  pl.BlockSpec((None, QK_W, V_W), lambda b, k: (b * nb + (nb - 1 - k), 0, 0)),
                  pl.BlockSpec((B, V_W), rev),
                  pl.BlockSpec((B, QK_W), seqrev), pl.BlockSpec((B, QK_W), seqrev), pl.BlockSpec((B, V_W), seqrev),
                  pl.BlockSpec((RET_HEADS, B, B), const3),
                  pl.BlockSpec((B, QK_W), const2), pl.BlockSpec((B, QK_W), const2),
                  pl.BlockSpec((QK_W, V_W), const2), pl.BlockSpec((QK_W, V_W), const2),
                  pl.BlockSpec((1, D), const2), pl.BlockSpec((1, V_W), const2),
                  pl.BlockSpec((NG, POOL_GC, POOL_GC), const3), pl.BlockSpec((1, V_W), const2),
                  pl.BlockSpec((D, D), const2), pl.BlockSpec((4, D, V_W), const3)],
        out_specs=[pl.BlockSpec((B, D), rev), pl.BlockSpec((B, D), rev), pl.BlockSpec((B, D), rev),
                   pl.BlockSpec((4, B, V_W), lambda b, k: (0, b * nb + (nb - 1 - k), 0)),
                   pl.BlockSpec((B, D), rev),
                   pl.BlockSpec((1, D), const2), pl.BlockSpec((1, D), const2), pl.BlockSpec((1, V_W), const2),
                   pl.BlockSpec((1, V_W), const2), pl.BlockSpec((NG, POOL_GC, POOL_GC), const3)],
        out_shape=[jax.ShapeDtypeStruct((T, D), F32), jax.ShapeDtypeStruct((T, D), BF16),
                   jax.ShapeDtypeStruct((T, D), BF16), jax.ShapeDtypeStruct((4, T, V_W), BF16),
                   jax.ShapeDtypeStruct((T, D), BF16),
                   jax.ShapeDtypeStruct((1, D), F32), jax.ShapeDtypeStruct((1, D), F32), jax.ShapeDtypeStruct((1, V_W), F32),
                   jax.ShapeDtypeStruct((1, V_W), F32), jax.ShapeDtypeStruct((NG, POOL_GC, POOL_GC), F32)],
        scratch_shapes=[pltpu.VMEM((QK_W, V_W), F32), pltpu.VMEM((B, V_W), F32)],
        args=(dx3, dn3, x2, g3, x1, p, o_s, st_s, pooled_s, consts["cos"], consts["sin"], consts["invc"], consts["m"], consts["dq"],
              consts["dk"], consts["ds"], consts["bm"], gmix, gn_gain, pool_w, pool_scale, wout, win))


def loss_bwd(x3, tgt, gain, tm, name):
    T, D = x3.shape

    def body(x_ref, t_ref, g_ref, dx_ref, df_ref, lacc_ref, dg_ref):
        @pl.when(pl.program_id(0) == 0)
        def _():
            lacc_ref[...] = jnp.zeros_like(lacc_ref)
            dg_ref[...] = jnp.zeros_like(dg_ref)

        r, xhat = _rms_parts(x_ref[...])
        gv = g_ref[...]
        err = xhat * gv - t_ref[...]
        lacc_ref[...] += jnp.sum(err * err, axis=0, keepdims=True)
        dy = err * (1.0 / D)
        dg_ref[...] += jnp.sum(dy * xhat, axis=0, keepdims=True)
        dx = _rms_bwd(dy, xhat, r, gv, 0.0)
        dx_ref[...] = dx
        df_ref[...] = (0.5 * dx).astype(BF16)

    tok = pl.BlockSpec((tm, D), lambda i: (i, 0))
    vec = pl.BlockSpec((1, D), lambda i: (0, 0))
    return pl.pallas_call(
        body, name=name, grid=(T // tm,),
        in_specs=[tok, tok, vec], out_specs=[tok, tok, vec, vec],
        out_shape=[jax.ShapeDtypeStruct((T, D), F32), jax.ShapeDtypeStruct((T, D), BF16),
                   jax.ShapeDtypeStruct((1, D), F32), jax.ShapeDtypeStruct((1, D), F32)],
        compiler_params=_params(1),
    )(x3, tgt, gain)


def _adamw_math(w, g, m, v):
    m2 = ADAM_B1 * m + (1.0 - ADAM_B1) * g
    v2 = ADAM_B2 * v + (1.0 - ADAM_B2) * (g * g)
    m_hat = m2 / (1.0 - ADAM_B1 ** ADAM_STEP)
    v_hat = v2 / (1.0 - ADAM_B2 ** ADAM_STEP)
    return -ADAM_LR * (m_hat / (jnp.sqrt(v_hat) + ADAM_EPS) + ADAM_WD * w), m2, v2


def _place():
    x, y, c = lax.axis_index("x"), lax.axis_index("y"), lax.axis_index("c")
    other_chips = [(1 - x, y), (x, 1 - y), (1 - x, 1 - y)]
    return x, y, c, other_chips


def _exchange_plan(ins, out_shape, copies, n_copies):
    def descriptors(pins, pouts, psems):
        send, recv = psems
        return [pltpu.make_async_remote_copy(src_ref=s, dst_ref=d, send_sem=send.at[i], recv_sem=recv.at[i],
                                             device_id=dev, device_id_type=MESH)
                for i, (s, d, dev) in enumerate(copies(pins, pouts))]

    def start(pins, pouts, psems):
        for cp in descriptors(pins, pouts, psems):
            cp.start()

    def finish(pins, pouts, psems):
        for cp in descriptors(pins, pouts, psems):
            cp.wait()

    return CommPlan(tuple(ins), tuple(out_shape),
                    (pltpu.SemaphoreType.DMA((n_copies,)), pltpu.SemaphoreType.DMA((n_copies,))), start, finish)


def _combine(a, b):
    assert a.mid is None and b.mid is None
    ni, no, ns = len(a.ins), len(a.out_shape), len(a.sems)

    def start(pins, pouts, psems):
        a.start(pins[:ni], pouts[:no], psems[:ns])
        b.start(pins[ni:], pouts[no:], psems[ns:])

    def finish(pins, pouts, psems):
        a.finish(pins[:ni], pouts[:no], psems[:ns])
        b.finish(pins[ni:], pouts[no:], psems[ns:])

    return CommPlan(a.ins + b.ins, a.out_shape + b.out_shape, a.sems + b.sems, start, finish)


def gather_plan(shards):
    n = len(shards)

    def start(pins, pouts, psems):
        x, y, c, chips = _place()
        mine = 2 * x + y
        for w in range(n):
            for k, (px, py) in enumerate(chips):
                pltpu.make_async_remote_copy(
                    src_ref=pins[w].at[:, c], dst_ref=pouts[w].at[:, mine, c],
                    send_sem=psems[0].at[w, k], recv_sem=psems[1].at[w, k],
                    device_id=(px, py, c), device_id_type=MESH).start()

    def mid(pins, pouts, psems):
        x, y, c, chips = _place()
        for w in range(n):
            for k, (px, py) in enumerate(chips):
                landed = pouts[w].at[:, 2 * px + py, c]
                pltpu.make_async_remote_copy(
                    src_ref=landed, dst_ref=landed, send_sem=psems[0].at[w, k], recv_sem=psems[1].at[w, k],
                    device_id=(px, py, c), device_id_type=MESH).wait_recv()
                pltpu.make_async_remote_copy(
                    src_ref=landed, dst_ref=landed, send_sem=psems[2].at[w, k], recv_sem=psems[3].at[w, k],
                    device_id=(x, y, 1 - c), device_id_type=MESH).start()

    def finish(pins, pouts, psems):
        x, y, c, chips = _place()
        mine = 2 * x + y
        for w in range(n):
            for k, (px, py) in enumerate(chips):
                landed = pouts[w].at[:, 2 * px + py, 1 - c]
                cp = pltpu.make_async_remote_copy(
                    src_ref=landed, dst_ref=landed, send_sem=psems[2].at[w, k], recv_sem=psems[3].at[w, k],
                    device_id=(x, y, 1 - c), device_id_type=MESH)
                cp.wait_recv()
                cp.wait_send()
                pltpu.make_async_remote_copy(
                    src_ref=pins[w].at[:, c], dst_ref=pouts[w].at[:, mine, c],
                    send_sem=psems[0].at[w, k], recv_sem=psems[1].at[w, k],
                    device_id=(px, py, c), device_id_type=MESH).wait_send()

    return CommPlan(tuple(shards),
                    tuple(jax.ShapeDtypeStruct((s.shape[0], N_CHIPS) + s.shape[1:], s.dtype) for s in shards),
                    tuple(pltpu.SemaphoreType.DMA((n, 3)) for _ in range(4)), start, finish, mid)


def place_own(gathered, shard, name):
    L, _, Rh, C = shard.shape

    def body(chip_ref, g_ref, s_ref, o_ref):
        o_ref[...] = s_ref[...]

    return pl.pallas_call(
        body, name=name,
        grid_spec=pltpu.PrefetchScalarGridSpec(
            num_scalar_prefetch=1, grid=(L,),
            in_specs=[ANY, pl.BlockSpec((None, 2, Rh, C), lambda l, chip_ref: (l, 0, 0, 0))],
            out_specs=pl.BlockSpec((None, None, 2, Rh, C), lambda l, chip_ref: (l, chip_ref[0], 0, 0, 0))),
        out_shape=jax.ShapeDtypeStruct(gathered.shape, gathered.dtype),
        input_output_aliases={1: 0},
        compiler_params=_params(1),
    )(_chip_index(), gathered, shard)


def gather_weights(shards):
    n = len(shards)
    units = [(w, l) for w in range(n) for l in range(shards[w].shape[0])]
    nu = len(units)

    def body(*refs):
        ins, outs, bufs = refs[:n], refs[n:2 * n], refs[2 * n:3 * n]
        ld_sem, st_sem, own_send, own_recv, fwd_send, fwd_recv, d2d_send, d2d_recv = refs[3 * n:]
        x, y, c, _ = _place()
        sibling = (x, y, 1 - c)
        mine = 2 * x + y
        first = (x ^ (1 - c), y ^ c)
        second = (x ^ c, y ^ (1 - c))
        diagonal = (1 - x, 1 - y)

        def slot(chip):
            return 2 * chip[0] + chip[1]

        def remote(src, dst, send, recv, device):
            return pltpu.make_async_remote_copy(src_ref=src, dst_ref=dst, send_sem=send, recv_sem=recv,
                                                device_id=device, device_id_type=MESH)

        loads = [pltpu.make_async_copy(ins[w], bufs[w], ld_sem.at[w]) for w in range(n)]
        for cp in loads:
            cp.start()
        stores, sends = [], []
        for w in range(n):
            loads[w].wait()
            cp = pltpu.make_async_copy(bufs[w], outs[w].at[:, mine], st_sem.at[w])
            cp.start()
            stores.append(cp)
        for u, (w, l) in enumerate(units):
            for k, chip in enumerate((first, second)):
                cp = remote(bufs[w].at[l, c], outs[w].at[l, mine, c], own_send.at[u, k], own_recv.at[u, k], (*chip, c))
                cp.start()
                sends.append(cp)
        for u, (w, l) in enumerate(units):
            got = outs[w].at[l, slot(first), c]
            remote(got, got, own_send.at[u, 0], own_recv.at[u, 0], (*first, c)).wait_recv()
            for cp in (remote(got, got, fwd_send.at[u], fwd_recv.at[u], (*second, c)),
                       remote(got, got, d2d_send.at[u, 0], d2d_recv.at[u, 0], sibling)):
                cp.start()
                sends.append(cp)
        for u, (w, l) in enumerate(units):
            got = outs[w].at[l, slot(second), c]
            remote(got, got, own_send.at[u, 1], own_recv.at[u, 1], (*second, c)).wait_recv()
            cp = remote(got, got, d2d_send.at[u, 1], d2d_recv.at[u, 1], sibling)
            cp.start()
            sends.append(cp)
        for u, (w, l) in enumerate(units):
            got = outs[w].at[l, slot(diagonal), c]
            remote(got, got, fwd_send.at[u], fwd_recv.at[u], (*second, c)).wait_recv()
            cp = remote(got, got, d2d_send.at[u, 2], d2d_recv.at[u, 2], sibling)
            cp.start()
            sends.append(cp)
        for u, (w, l) in enumerate(units):
            for k, chip in enumerate((second, first, diagonal)):
                got = outs[w].at[l, slot(chip), 1 - c]
                remote(got, got, d2d_send.at[u, k], d2d_recv.at[u, k], sibling).wait_recv()
        for cp in sends:
            cp.wait_send()
        for cp in stores:
            cp.wait()

    return pl.pallas_call(
        body, name="gather_weights",
        in_specs=[ANY] * n, out_specs=[ANY] * n,
        out_shape=[jax.ShapeDtypeStruct((s.shape[0], N_CHIPS) + s.shape[1:], s.dtype) for s in shards],
        scratch_shapes=[pltpu.VMEM(s.shape, s.dtype) for s in shards] +
                       [pltpu.SemaphoreType.DMA((n,)), pltpu.SemaphoreType.DMA((n,)),
                        pltpu.SemaphoreType.DMA((nu, 2)), pltpu.SemaphoreType.DMA((nu, 2)),
                        pltpu.SemaphoreType.DMA((nu,)), pltpu.SemaphoreType.DMA((nu,)),
                        pltpu.SemaphoreType.DMA((nu, 3)), pltpu.SemaphoreType.DMA((nu, 3))],
        compiler_params=pltpu.CompilerParams(vmem_limit_bytes=VMEM_LIMIT),
    )(*shards)


def send_to_sibling_other_half(grads):
    def copies(ins, outs):
        x, y, c, _ = _place()
        return [(ins[w].at[:, :, 1 - c], outs[w], (x, y, 1 - c)) for w in range(len(grads))]

    return _exchange_plan(grads, [jax.ShapeDtypeStruct(g.shape[:2] + g.shape[3:], g.dtype) for g in grads], copies,
                          len(grads))


def _core_index():
    return jnp.reshape(lax.axis_index("c"), (1,)).astype(jnp.int32)


def _chip_index():
    return jnp.reshape(2 * lax.axis_index("x") + lax.axis_index("y"), (1,)).astype(jnp.int32)


def add_own_half(g, recv, name):
    L, J, _, Rh, C = g.shape

    def body(c_ref, g_ref, r_ref, o_ref):
        o_ref[...] = (g_ref[...] + r_ref[...]).astype(BF16)

    return pl.pallas_call(
        body, name=name,
        grid_spec=pltpu.PrefetchScalarGridSpec(
            num_scalar_prefetch=1, grid=(L, J),
            in_specs=[pl.BlockSpec((None, None, None, Rh, C), lambda l, j, c_ref: (l, j, c_ref[0], 0, 0)),
                      pl.BlockSpec((None, None, Rh, C), lambda l, j, c_ref: (l, j, 0, 0))],
            out_specs=pl.BlockSpec((None, None, Rh, C), lambda l, j, c_ref: (l, j, 0, 0))),
        out_shape=jax.ShapeDtypeStruct((L, J, Rh, C), BF16),
        compiler_params=_params(2),
    )(_core_index(), g, recv)


def exchange_between_chips(sums):
    def copies(ins, outs):
        x, y, c, chips = _place()
        return [(ins[w].at[:, 2 * px + py], outs[w].at[k], (px, py, c))
                for w in range(len(sums)) for k, (px, py) in enumerate(chips)]

    return _exchange_plan(sums, [jax.ShapeDtypeStruct((3, s.shape[0]) + s.shape[2:], s.dtype) for s in sums], copies,
                          3 * len(sums))


def sum_chips(own, recv, name):
    L, _, Rh, C = own.shape

    def body(chip_ref, o_ref, r_ref, out_ref):
        acc = o_ref[...].astype(F32)
        for k in range(3):
            acc = acc + r_ref[k].astype(F32)
        out_ref[...] = acc

    return pl.pallas_call(
        body, name=name,
        grid_spec=pltpu.PrefetchScalarGridSpec(
            num_scalar_prefetch=1, grid=(L,),
            in_specs=[pl.BlockSpec((None, None, Rh, C), lambda l, chip_ref: (l, chip_ref[0], 0, 0)),
                      pl.BlockSpec((3, None, Rh, C), lambda l, chip_ref: (0, l, 0, 0))],
            out_specs=pl.BlockSpec((None, Rh, C), lambda l, chip_ref: (l, 0, 0))),
        out_shape=jax.ShapeDtypeStruct((L, Rh, C), F32),
        compiler_params=_params(1),
    )(_chip_index(), own, recv)


def share_with_sibling(halves):
    def copies(ins, outs):
        x, y, c, _ = _place()
        return [(ins[w], outs[w], (x, y, 1 - c)) for w in range(len(halves))]

    return _exchange_plan(halves, [jax.ShapeDtypeStruct(h.shape, h.dtype) for h in halves], copies, len(halves))


def adamw_group(ws, ms, vs, own, sib, name, plan=None):
    L = len(ws)
    R, C = ws[0].shape
    Rh = R // 2
    tr = Rh // 2
    nr = Rh // tr

    def body(*refs):
        w, m, v = refs[:L], refs[L:2 * L], refs[2 * L:3 * L]
        own_ref, sib_ref = refs[3 * L], refs[3 * L + 1]
        outs = refs[3 * L + 2:]
        mine = pl.program_id(0) == lax.axis_index("c")
        for l in range(L):
            gv = jnp.where(mine, own_ref[l], sib_ref[l])
            d, m2, v2 = _adamw_math(w[l][...], gv, m[l][...], v[l][...])
            outs[4 * l][...] = d
            outs[4 * l + 1][...] = m2
            outs[4 * l + 2][...] = v2
            outs[4 * l + 3][...] = gv

    blk = pl.BlockSpec((tr, C), lambda h, r: (h * nr + r, 0))
    half = pl.BlockSpec((L, tr, C), lambda h, r: (0, r, 0))
    sds = jax.ShapeDtypeStruct((R, C), F32)
    return _call(body, plan, name=name, grid=(2, nr), in_specs=[blk] * (3 * L) + [half, half],
                 out_specs=[blk] * (4 * L), out_shape=[sds] * (4 * L), args=(*ws, *ms, *vs, own, sib))


def gather_small_plan(part):
    def copies(ins, outs):
        x, y, c, _ = _place()
        me = 4 * x + 2 * y + c
        return [(ins[0], outs[0].at[me], (x ^ ((k >> 2) & 1), y ^ ((k >> 1) & 1), c ^ (k & 1))) for k in range(1, N_DEV)]

    return _exchange_plan([part], [jax.ShapeDtypeStruct((N_DEV,) + part.shape, part.dtype)], copies, N_DEV - 1)


def sum_small(part, gathered, name):
    R, C = part.shape

    def body(p_ref, g_ref, o_ref):
        d = pl.program_id(0)
        me = 4 * lax.axis_index("x") + 2 * lax.axis_index("y") + lax.axis_index("c")
        val = jnp.where(d == me, p_ref[...], g_ref[...])

        @pl.when(d == 0)
        def _():
            o_ref[...] = val

        @pl.when(d > 0)
        def _():
            o_ref[...] += val

    return pl.pallas_call(
        body, name=name, grid=(N_DEV,),
        in_specs=[pl.BlockSpec((R, C), lambda d: (0, 0)), pl.BlockSpec((None, R, C), lambda d: (d, 0, 0))],
        out_specs=pl.BlockSpec((R, C), lambda d: (0, 0)),
        out_shape=jax.ShapeDtypeStruct((R, C), F32),
        compiler_params=_params(1),
    )(part, gathered)


SMALL = ("norm_ffn1", "norm_mix", "norm_ffn2", "norm_final", "ret_gn_gain", "pool_scale", "pool_w")
GROUPS = (("ffn1", ("ffn1_gate", "ffn1_up", "ffn1_down")), ("w_in", ("w_in",)), ("w_out", ("w_out",)),
          ("ffn2", ("ffn2_gate", "ffn2_up", "ffn2_down")))
TRANSPOSED = ("ffn1_gate", "ffn1_up", "ffn2_gate", "ffn2_up")
IN_W = 2 * QK_W + 3 * V_W
ORDER = ("norm_ffn1", "ffn1_gate", "ffn1_up", "ffn1_down", "norm_mix", "w_in", "ret_gn_gain", "pool_w", "pool_scale",
         "w_out", "norm_ffn2", "ffn2_gate", "ffn2_up", "ffn2_down", "norm_final")


SUBLANES = 8


def _small_rows(shapes):
    rows = [math.prod(shapes[k]) // 128 for k in SMALL]
    offs, off = [], 0
    for r in rows:
        offs.append(off)
        off += -(-r // SUBLANES) * SUBLANES
    return rows, offs, off


def _pack_small(d, shapes):
    rows, offs, _ = _small_rows(shapes)
    pieces = []
    for k, r in zip(SMALL[1:], rows[1:]):
        pieces.append(d[k].reshape(r, 128))
        if r % SUBLANES:
            pieces.append(jnp.zeros((SUBLANES - r % SUBLANES, 128), F32))
    return jnp.concatenate(pieces, axis=0)


def adamw_small(W, M, V, first, summed, shapes):
    rows, offs, _ = _small_rows(shapes)
    n = len(SMALL)

    def body(*refs):
        w, m, v, f_ref, s_ref = refs[:n], refs[n:2 * n], refs[2 * n:3 * n], refs[3 * n], refs[3 * n + 1]
        outs = refs[3 * n + 2:]
        for i, (r, off) in enumerate(zip(rows, offs)):
            gv = f_ref[...] if i == 0 else s_ref[off - offs[1]:off - offs[1] + r, :]
            d, m2, v2 = _adamw_math(w[i][...], gv, m[i][...], v[i][...])
            outs[4 * i][...] = d
            outs[4 * i + 1][...] = m2
            outs[4 * i + 2][...] = v2
            outs[4 * i + 3][...] = gv

    vm = pl.BlockSpec(memory_space=pltpu.VMEM)
    flat = lambda d: [d[k].reshape(r, 128) for k, r in zip(SMALL, rows)]
    res = pl.pallas_call(
        body, name="adamw_small", in_specs=[vm] * (3 * n + 2), out_specs=[vm] * (4 * n),
        out_shape=[jax.ShapeDtypeStruct((r, 128), F32) for r in rows for _ in range(4)],
    )(*flat(W), *flat(M), *flat(V), first, summed)
    return {k: [a.reshape(shapes[k]) for a in res[4 * i:4 * i + 4]] for i, k in enumerate(SMALL)}


def kernel(x, norm_ffn1, ffn1_gate, ffn1_up, ffn1_down, norm_mix, w_in, ret_gn_gain, pool_w, pool_scale, w_out, norm_ffn2, ffn2_gate, ffn2_up, ffn2_down, norm_final, loss_target, m_norm_ffn1, m_ffn1_gate, m_ffn1_up, m_ffn1_down, m_norm_mix, m_w_in, m_ret_gn_gain, m_pool_w, m_pool_scale, m_w_out, m_norm_ffn2, m_ffn2_gate, m_ffn2_up, m_ffn2_down, m_norm_final, v_norm_ffn1, v_ffn1_gate, v_ffn1_up, v_ffn1_down, v_norm_mix, v_w_in, v_ret_gn_gain, v_pool_w, v_pool_scale, v_w_out, v_norm_ffn2, v_ffn2_gate, v_ffn2_up, v_ffn2_down, v_norm_final):
    W = dict(norm_ffn1=norm_ffn1, ffn1_gate=ffn1_gate, ffn1_up=ffn1_up, ffn1_down=ffn1_down, norm_mix=norm_mix, w_in=w_in,
             ret_gn_gain=ret_gn_gain, pool_w=pool_w, pool_scale=pool_scale, w_out=w_out, norm_ffn2=norm_ffn2,
             ffn2_gate=ffn2_gate, ffn2_up=ffn2_up, ffn2_down=ffn2_down, norm_final=norm_final)
    M = dict(norm_ffn1=m_norm_ffn1, ffn1_gate=m_ffn1_gate, ffn1_up=m_ffn1_up, ffn1_down=m_ffn1_down, norm_mix=m_norm_mix,
             w_in=m_w_in, ret_gn_gain=m_ret_gn_gain, pool_w=m_pool_w, pool_scale=m_pool_scale, w_out=m_w_out,
             norm_ffn2=m_norm_ffn2, ffn2_gate=m_ffn2_gate, ffn2_up=m_ffn2_up, ffn2_down=m_ffn2_down, norm_final=m_norm_final)
    V = dict(norm_ffn1=v_norm_ffn1, ffn1_gate=v_ffn1_gate, ffn1_up=v_ffn1_up, ffn1_down=v_ffn1_down, norm_mix=v_norm_mix,
             w_in=v_w_in, ret_gn_gain=v_ret_gn_gain, pool_w=v_pool_w, pool_scale=v_pool_scale, w_out=v_w_out,
             norm_ffn2=v_norm_ffn2, ffn2_gate=v_ffn2_gate, ffn2_up=v_ffn2_up, ffn2_down=v_ffn2_down, norm_final=v_norm_final)

    n_seq, S, D = x.shape
    T = n_seq * S
    tm = min(1024, T // 2)
    tk = min(1024, T // 2)
    tk1 = min(2048, T // 2)
    xf = x.reshape(T, D)
    tgt = loss_target.reshape(T, D)

    def local(d, k):
        a = d[k][0]
        return a.T if k in TRANSPOSED else a

    def to_out(k, a):
        return (a.T if k in TRANSPOSED else a)[None]

    loc_bf = []
    for _, members in GROUPS:
        st = jnp.stack([local(W, k).astype(BF16) for k in members])
        loc_bf.append(st.reshape(st.shape[0], 2, st.shape[1] // 2, st.shape[2]))
    FS = loc_bf[0].shape[2] * 2
    w3_1 = gather_weights(loc_bf[:1])[0].reshape(3, N_CHIPS, FS, D)
    g1 = norm_ffn1.reshape(1, D)
    gm = norm_mix.reshape(1, D)
    g3 = norm_ffn2.reshape(1, D)
    gf = norm_final.reshape(1, D)
    gn_gain = ret_gn_gain.reshape(1, V_W)
    pscale = pool_scale.reshape(1, V_W)
    pw = pool_w.reshape(len(POOL_WINDOWS), POOL_GC, POOL_GC)
    consts = mixer_constants(S)

    def halves5(g, lb):
        return g.reshape(lb.shape[0], N_CHIPS, 2, lb.shape[2], lb.shape[3])

    (x1, b1, si1, t1, n1), landed = ffn_fwd(xf, g1, w3_1, tm, "ffn1_fwd", plan=gather_plan(loc_bf[1:]))
    gathered = [place_own(g, lb, "place_own_" + gn) for g, lb, (gn, _) in zip(landed, loc_bf[1:], GROUPS[1:])]
    win_full = gathered[0].reshape(N_CHIPS, D, IN_W // N_CHIPS)
    wout_full = gathered[1].reshape(D, D)
    w3_2 = gathered[2].reshape(3, N_CHIPS, FS, D)
    p = in_proj(x1, gm, win_full, tm, "in_proj")
    x2, m_s, o_s, st_s, pooled_s = mixer_fwd(p, x1, consts, gn_gain, pw, pscale, wout_full, n_seq, S, "mixer_fwd")
    x3, b2, si2, t2, n3 = ffn_fwd(x2, g3, w3_2, tm, "ffn2_fwd")

    dx3, df2, lacc, dgf = loss_bwd(x3, tgt, gf, tm, "loss_bwd")
    dn3, da2, db2 = ffn_bwd(df2, b2, si2, t2, w3_2, tm, "ffn2_bwd")
    to_sib, to_chips, back = send_to_sibling_other_half, exchange_between_chips, share_with_sibling
    out_g, out_d, out_m, out_v = {}, {}, {}, {}

    def update(members, own, sib, name):
        res = adamw_group([local(W, k) for k in members], [local(M, k) for k in members], [local(V, k) for k in members],
                          own, sib, "adamw_" + name)
        for l, k in enumerate(members):
            out_d[k], out_m[k], out_v[k], out_g[k] = (to_out(k, r) for r in res[4 * l:4 * l + 4])

    (gr_ffn2,) = ffn_wgrad(da2, db2, si2, b2, n3, df2, tk, "ffn2_wgrad")
    g_ffn2 = halves5(gr_ffn2, loc_bf[3])
    (dx1, df1, dx2b, dp, n2, dg3, dgm, dgn, dps, dpw), (sb_ffn2,) = mixer_bwd(
        dx3, dn3, x2, g3, x1, p, o_s, st_s, pooled_s, consts, gm, gn_gain, pw, pscale, wout_full, win_full, n_seq, S,
        "mixer_bwd", plan=to_sib([g_ffn2]))
    cs_ffn2 = add_own_half(g_ffn2, sb_ffn2, "add_sibling_ffn2")
    (gr_wout,) = tn_shared(m_s, dx2b[None], tk1, "dw_out")
    (gr_win,) = tn_shared(n2, dp, tk1, "dw_in")
    g_mix = [halves5(gr_win, loc_bf[1]), halves5(gr_wout, loc_bf[2])]
    shapes = {k: W[k].shape for k in SMALL}
    small_part = dict(norm_mix=dgm, norm_ffn2=dg3, norm_final=dgf, ret_gn_gain=dgn, pool_scale=dps, pool_w=dpw)
    part = jnp.concatenate([_pack_small(small_part, shapes), lacc.reshape(-1, 128)], axis=0)
    n_small = part.shape[0] - lacc.size // 128
    (dn1, da1, db1), (pc_ffn2, sb_win, sb_wout, parts) = ffn_bwd(
        df1, b1, si1, t1, w3_1, tm, "ffn1_bwd",
        plan=_combine(_combine(to_chips([cs_ffn2]), to_sib(g_mix)), gather_small_plan(part)))
    summed = sum_small(part, parts, "sum_small")
    loss = jnp.sum(summed[n_small:]) * (0.5 / D)
    mh_ffn2 = sum_chips(cs_ffn2, pc_ffn2, "sum_chips_ffn2")
    cs_mix = [add_own_half(g, r, "add_sibling_" + gn) for g, r, gn in zip(g_mix, (sb_win, sb_wout), ("w_in", "w_out"))]
    def one(g):
        return g.reshape((1, N_CHIPS, 2) + loc_bf[0].shape[2:])

    (gr_g,), (sh_ffn2, pc_win, pc_wout) = wgrad_one(
        da1, n1, tk1, "ffn1_wgrad_gate", plan=_combine(back([mh_ffn2]), to_chips(cs_mix)))
    mh_mix = [sum_chips(cs, pc, "sum_chips_" + gn) for cs, pc, gn in zip(cs_mix, (pc_win, pc_wout), ("w_in", "w_out"))]
    (gr_u,), (sb_g,) = wgrad_one(db1, n1, tk1, "ffn1_wgrad_up", plan=to_sib([one(gr_g)]))
    cs_g = add_own_half(one(gr_g), sb_g, "add_sibling_ffn1_gate")
    (gr_d,), (pc_g, sb_u) = wgrad_one(
        si1, df1, tk1, "ffn1_wgrad_down", plan=_combine(to_chips([cs_g]), to_sib([one(gr_u)])), a2=b1)
    mh_g = sum_chips(cs_g, pc_g, "sum_chips_ffn1_gate")
    cs_u = add_own_half(one(gr_u), sb_u, "add_sibling_ffn1_up")
    (dx0, dg1), (sh_g, sh_win, sh_wout, pc_u, sb_d) = rms_bwd(
        xf, dx1, dn1, g1, tm, "ffn1_rms_bwd",
        plan=_combine(_combine(back([mh_g] + mh_mix), to_chips([cs_u])), to_sib([one(gr_d)])))
    mh_u = sum_chips(cs_u, pc_u, "sum_chips_ffn1_up")
    cs_d = add_own_half(one(gr_d), sb_d, "add_sibling_ffn1_down")
    first = dg1.reshape(-1, 128)
    sh_u, pc_d, firsts = _run_plan(
        _combine(_combine(back([mh_u]), to_chips([cs_d])), gather_small_plan(first)), "ffn1_grads_stage_b")
    mh_d = sum_chips(cs_d, pc_d, "sum_chips_ffn1_down")
    (sh_d,) = _run_plan(back([mh_d]), "ffn1_grads_stage_c")
    first_sum = sum_small(first, firsts, "sum_small_first")
    for k, own, sib in zip(GROUPS[0][1], (mh_g, mh_u, mh_d), (sh_g, sh_u, sh_d)):
        update((k,), own, sib, k)
    update(GROUPS[1][1], mh_mix[0], sh_win, "w_in")
    update(GROUPS[2][1], mh_mix[1], sh_wout, "w_out")
    update(GROUPS[3][1], mh_ffn2, sh_ffn2, "ffn2")

    for k, (d_, m_, v_, g_) in adamw_small(W, M, V, first_sum, summed, shapes).items():
        out_d[k], out_m[k], out_v[k], out_g[k] = d_, m_, v_, g_

    grad_x = dx0.reshape(n_seq, S, D)
    return (loss, grad_x, *[out_g[k] for k in ORDER], *[out_d[k] for k in ORDER],
            *[out_m[k] for k in ORDER], *[out_v[k] for k in ORDER])
```

```python
import functools
import math
from typing import Callable, NamedTuple, Optional

import jax
import jax.numpy as jnp
from jax import lax
from jax.experimental import pallas as pl
from jax.experimental.pallas import tpu as pltpu

F32 = jnp.float32
BF16 = jnp.bfloat16
MESH = pl.DeviceIdType.MESH

N_CHIPS = 4
N_DEV = 8
CHUNK = 64
RET_HEADS = 4
RET_DK = 64
RET_DV = 128
QK_W = RET_HEADS * RET_DK
V_W = RET_HEADS * RET_DV
POOL_WINDOWS = (2, 4, 8, 16)
POOL_GC = 128
ROPE_BASE = 10000.0
RMS_EPS = 1e-6
GN_EPS = 1e-5
ADAM_LR = 0.001
ADAM_B1 = 0.9
ADAM_B2 = 0.999
ADAM_EPS = 1e-08
ADAM_WD = 0.01
ADAM_STEP = 10
MXU_W = 256
RET_BLOCK = MXU_W
VMEM_LIMIT = 56 * 1024 * 1024

NN = (((1,), (0,)), ((), ()))
NT = (((1,), (1,)), ((), ()))
TN = (((0,), (0,)), ((), ()))


def _dot(a, b, dims=NN):
    return lax.dot_general(a, b, dims, preferred_element_type=F32)


def _sigmoid(x):
    return 0.5 * jnp.tanh(0.5 * x) + 0.5


def _params(n_grid):
    return pltpu.CompilerParams(dimension_semantics=("arbitrary",) * n_grid, vmem_limit_bytes=VMEM_LIMIT)


class CommPlan(NamedTuple):
    ins: tuple
    out_shape: tuple
    sems: tuple
    start: Callable
    finish: Callable
    mid: Optional[Callable] = None
    mid_at: float = 0.85


ANY = pl.BlockSpec(memory_space=pl.ANY)


def _call(body, plan, *, name, grid, in_specs, out_specs, out_shape, args, scratch_shapes=()):
    n_grid = len(grid)
    if plan is None:
        return pl.pallas_call(body, name=name, grid=grid, in_specs=in_specs, out_specs=out_specs, out_shape=out_shape,
                              scratch_shapes=scratch_shapes, compiler_params=_params(n_grid))(*args)
    n_in, n_out, n_sc = len(in_specs), len(out_specs), len(scratch_shapes)
    p_in, p_out = len(plan.ins), len(plan.out_shape)
    total = math.prod(grid)
    mid_step = min(total - 1, int(plan.mid_at * total))

    def riding(*refs):
        ins, pins = refs[:n_in], refs[n_in:n_in + p_in]
        o0 = n_in + p_in
        outs, pouts = refs[o0:o0 + n_out], refs[o0 + n_out:o0 + n_out + p_out]
        s0 = o0 + n_out + p_out
        scratch, psems = refs[s0:s0 + n_sc], refs[s0 + n_sc:]
        step = pl.program_id(0)
        for d in range(1, n_grid):
            step = step * grid[d] + pl.program_id(d)

        @pl.when(step == 0)
        def _():
            plan.start(pins, pouts, psems)

        body(*ins, *outs, *scratch)

        if plan.mid is not None:
            @pl.when(step == mid_step)
            def _():
                plan.mid(pins, pouts, psems)

        @pl.when(step == total - 1)
        def _():
            plan.finish(pins, pouts, psems)

    res = pl.pallas_call(
        riding, name=name, grid=grid, in_specs=list(in_specs) + [ANY] * p_in, out_specs=list(out_specs) + [ANY] * p_out,
        out_shape=list(out_shape) + list(plan.out_shape), scratch_shapes=list(scratch_shapes) + list(plan.sems),
        compiler_params=_params(n_grid))(*args, *plan.ins)
    return res[:n_out], res[n_out:]


def _run_plan(plan, name):
    p_in, p_out = len(plan.ins), len(plan.out_shape)

    def body(*refs):
        pins, pouts, psems = refs[:p_in], refs[p_in:p_in + p_out], refs[p_in + p_out:]
        plan.start(pins, pouts, psems)
        if plan.mid is not None:
            plan.mid(pins, pouts, psems)
        plan.finish(pins, pouts, psems)

    return pl.pallas_call(body, name=name, in_specs=[ANY] * p_in, out_specs=[ANY] * p_out,
                          out_shape=list(plan.out_shape), scratch_shapes=list(plan.sems))(*plan.ins)


def _rms_parts(x):
    r = lax.rsqrt(jnp.mean(x * x, axis=-1, keepdims=True) + RMS_EPS)
    return r, x * r


def _rms_bwd(dn, xhat, r, gain, dres):
    dxh = dn * gain
    return dres + r * (dxh - xhat * jnp.mean(dxh * xhat, axis=-1, keepdims=True))


def _w3_specs(FS, D):
    return [pl.BlockSpec((None, None, FS, D), functools.partial(lambda i, j, k: (k, j, 0, 0), k=k)) for k in range(3)]


def ffn_fwd(x, gain, w3, tm, name, plan=None):
    T, D = x.shape
    _, J, FS, _ = w3.shape

    def body(x_ref, g_ref, wg_ref, wu_ref, wd_ref, xo_ref, b_ref, si_ref, t_ref, n_ref, acc):
        j = pl.program_id(1)

        @pl.when(j == 0)
        def _():
            _, xhat = _rms_parts(x_ref[...])
            n_ref[...] = (xhat * g_ref[...]).astype(BF16)
            acc[...] = jnp.zeros_like(acc)

        n = n_ref[...]
        a = _dot(n, wg_ref[...], NT)
        b = _dot(n, wu_ref[...], NT)
        sg = _sigmoid(a)
        si = a * sg
        b_ref[...] = b.astype(BF16)
        si_ref[...] = si.astype(BF16)
        t_ref[...] = (sg + si * (1.0 - sg)).astype(BF16)
        acc[...] += _dot((si * b).astype(BF16), wd_ref[...])

        @pl.when(j == J - 1)
        def _():
            xo_ref[...] = x_ref[...] + 0.5 * acc[...]

    return _call(
        body, plan, name=name, grid=(T // tm, J),
        in_specs=[pl.BlockSpec((tm, D), lambda i, j: (i, 0)),
                  pl.BlockSpec((1, D), lambda i, j: (0, 0))] + _w3_specs(FS, D),
        out_specs=[pl.BlockSpec((tm, D), lambda i, j: (i, 0))] +
                  [pl.BlockSpec((None, tm, FS), lambda i, j: (j, i, 0))] * 3 +
                  [pl.BlockSpec((tm, D), lambda i, j: (i, 0))],
        out_shape=[jax.ShapeDtypeStruct((T, D), F32)] + [jax.ShapeDtypeStruct((J, T, FS), BF16)] * 3 +
                  [jax.ShapeDtypeStruct((T, D), BF16)],
        scratch_shapes=[pltpu.VMEM((tm, D), F32)],
        args=(x, gain, w3, w3, w3))


def ffn_bwd(df, b_s, si_s, t_s, w3, tm, name, plan=None):
    T, D = df.shape
    _, J, FS, _ = w3.shape

    chunks = [slice(s, min(s + MXU_W, FS)) for s in range(0, FS, MXU_W)]
    nc = len(chunks)

    def body(df_ref, b_ref, si_ref, t_ref, wg_ref, wu_ref, wd_ref, dn_ref, da_ref, db_ref):
        @pl.when(pl.program_id(1) == 0)
        def _():
            dn_ref[...] = jnp.zeros_like(dn_ref)

        def elementwise(dh, rows, sl):
            da = (dh * b_ref[rows, sl].astype(F32) * t_ref[rows, sl].astype(F32)).astype(BF16)
            db = (dh * si_ref[rows, sl].astype(F32)).astype(BF16)
            da_ref[rows, sl] = da
            db_ref[rows, sl] = db
            return da, db

        for r0 in range(0, tm, tm // 4):
            rows = slice(r0, r0 + tm // 4)
            df = df_ref[rows, :]
            dh, dab, acc = [None] * nc, [None] * nc, None
            for step in range(nc + 2):
                if step < nc:
                    dh[step] = _dot(df, wd_ref[chunks[step], :], NT)
                if 1 <= step <= nc:
                    dab[step - 1] = elementwise(dh[step - 1], rows, chunks[step - 1])
                if step >= 2:
                    sl = chunks[step - 2]
                    part = _dot(dab[step - 2][0], wg_ref[sl, :]) + _dot(dab[step - 2][1], wu_ref[sl, :])
                    acc = part if acc is None else acc + part
            dn_ref[rows, :] += acc

    tok = pl.BlockSpec((tm, D), lambda i, j: (i, 0))
    sh = pl.BlockSpec((None, tm, FS), lambda i, j: (j, i, 0))
    return _call(
        body, plan, name=name, grid=(T // tm, J),
        in_specs=[tok, sh, sh, sh] + _w3_specs(FS, D),
        out_specs=[tok, sh, sh],
        out_shape=[jax.ShapeDtypeStruct((T, D), F32),
                   jax.ShapeDtypeStruct((J, T, FS), BF16),
                   jax.ShapeDtypeStruct((J, T, FS), BF16)],
        args=(df, b_s, si_s, t_s, w3, w3, w3))


def rms_bwd(x, dout, dn, gain, tm, name, plan=None):
    T, D = x.shape

    def body(x_ref, do_ref, dn_ref, g_ref, dx_ref, dgain_ref):
        @pl.when(pl.program_id(0) == 0)
        def _():
            dgain_ref[...] = jnp.zeros_like(dgain_ref)

        r, xhat = _rms_parts(x_ref[...])
        dn = dn_ref[...]
        dgain_ref[...] += jnp.sum(dn * xhat, axis=0, keepdims=True)
        dx_ref[...] = _rms_bwd(dn, xhat, r, g_ref[...], do_ref[...])

    tok = pl.BlockSpec((tm, D), lambda i: (i, 0))
    vec = pl.BlockSpec((1, D), lambda i: (0, 0))
    return _call(
        body, plan, name=name, grid=(T // tm,), in_specs=[tok, tok, tok, vec], out_specs=[tok, vec],
        out_shape=[jax.ShapeDtypeStruct((T, D), F32), jax.ShapeDtypeStruct((1, D), F32)],
        args=(x, dout, dn, gain))


def ffn_wgrad(da, db, si, b, n, df, tk, name, plan=None):
    J, T, FS = da.shape
    D = n.shape[1]

    def body(da_ref, db_ref, si_ref, b_ref, n_ref, df_ref, o_ref):
        @pl.when(pl.program_id(1) == 0)
        def _():
            o_ref[...] = jnp.zeros_like(o_ref)

        nv = n_ref[...]
        o_ref[0] += _dot(da_ref[...], nv, TN)
        o_ref[1] += _dot(db_ref[...], nv, TN)
        o_ref[2] += _dot(si_ref[...] * b_ref[...], df_ref[...], TN)

    sh = pl.BlockSpec((None, tk, FS), lambda j, k: (j, k, 0))
    tok = pl.BlockSpec((tk, D), lambda j, k: (k, 0))
    return _call(
        body, plan, name=name, grid=(J, T // tk),
        in_specs=[sh, sh, sh, sh, tok, tok],
        out_specs=[pl.BlockSpec((3, None, FS, D), lambda j, k: (0, j, 0, 0))],
        out_shape=[jax.ShapeDtypeStruct((3, J, FS, D), F32)],
        args=(da, db, si, b, n, df))


def wgrad_one(a, b, tk, name, plan=None, a2=None):
    J, T, FS = a.shape
    D = b.shape[1]
    lhs = (a,) if a2 is None else (a, a2)

    def body(*refs):
        o_ref = refs[-1]

        @pl.when(pl.program_id(1) == 0)
        def _():
            o_ref[...] = jnp.zeros_like(o_ref)

        av = refs[0][...] if a2 is None else refs[0][...] * refs[1][...]
        o_ref[...] += _dot(av, refs[-2][...], TN)

    return _call(
        body, plan, name=name, grid=(J, T // tk),
        in_specs=[pl.BlockSpec((None, tk, FS), lambda j, k: (j, k, 0))] * len(lhs) +
                 [pl.BlockSpec((tk, D), lambda j, k: (k, 0))],
        out_specs=[pl.BlockSpec((None, FS, D), lambda j, k: (j, 0, 0))],
        out_shape=[jax.ShapeDtypeStruct((J, FS, D), F32)],
        args=(*lhs, b))


def tn_shared(a, b, tk, name, plan=None):
    T, M = a.shape
    J, _, N = b.shape

    def body(a_ref, b_ref, o_ref):
        @pl.when(pl.program_id(0) == 0)
        def _():
            o_ref[...] = jnp.zeros_like(o_ref)

        a_t = a_ref[...].astype(BF16).T
        for j in range(J):
            o_ref[j] += _dot(a_t, b_ref[j].astype(BF16))

    return _call(
        body, plan, name=name, grid=(T // tk,),
        in_specs=[pl.BlockSpec((tk, M), lambda k: (k, 0)), pl.BlockSpec((J, tk, N), lambda k: (0, k, 0))],
        out_specs=[pl.BlockSpec((J, M, N), lambda k: (0, 0, 0))],
        out_shape=[jax.ShapeDtypeStruct((J, M, N), F32)],
        args=(a, b))


def in_proj(x, gain, win, tm, name):
    T, D = x.shape
    J, _, W = win.shape

    def body(x_ref, g_ref, w_ref, p_ref):
        _, xhat = _rms_parts(x_ref[...])
        n = (xhat * g_ref[...]).astype(BF16)
        for j in range(J):
            p_ref[j] = _dot(n, w_ref[j])

    return pl.pallas_call(
        body, name=name, grid=(T // tm,),
        in_specs=[pl.BlockSpec((tm, D), lambda i: (i, 0)),
                  pl.BlockSpec((1, D), lambda i: (0, 0)),
                  pl.BlockSpec((J, D, W), lambda i: (0, 0, 0))],
        out_specs=pl.BlockSpec((J, tm, W), lambda i: (0, i, 0)),
        out_shape=jax.ShapeDtypeStruct((J, T, W), F32),
        compiler_params=_params(1),
    )(x, gain, win)


def mixer_constants(S):
    B = RET_BLOCK
    half = RET_DK // 2
    freqs = ROPE_BASE ** (-jnp.arange(half, dtype=F32) * 2.0 / RET_DK)
    ang = jnp.arange(S, dtype=F32)[:, None] * freqs[None, :]
    cos = jnp.tile(jnp.cos(ang), (1, 2 * RET_HEADS))
    sin = jnp.tile(jnp.concatenate([-jnp.sin(ang), jnp.sin(ang)], axis=1), (1, RET_HEADS))
    gamma = 1.0 - 2.0 ** (-5.0 - jnp.arange(RET_HEADS, dtype=F32))
    log_g = jnp.log(gamma)
    idx = jnp.arange(B, dtype=F32)
    ci = jnp.arange(B) // CHUNK
    dist = jnp.abs(idx[:, None] - idx[None, :])
    m_intra = jnp.exp(log_g[:, None, None] * dist[None]) * (ci[None, :] <= ci[:, None])[None].astype(F32)
    lg_lane = jnp.repeat(log_g, RET_DK)
    d_q = jnp.exp(lg_lane[None, :] * (idx[:, None] + 1.0))
    d_k = jnp.exp(lg_lane[None, :] * (B - 1.0 - idx[:, None]))
    d_s = jnp.broadcast_to(jnp.exp(lg_lane * B)[:, None], (QK_W, V_W))
    bm = (jnp.arange(QK_W)[:, None] // RET_DK == jnp.arange(V_W)[None, :] // RET_DV).astype(F32)
    win = jnp.repeat(jnp.array(POOL_WINDOWS, F32), POOL_GC)
    invc = 1.0 / jnp.minimum(jnp.arange(S, dtype=F32)[:, None] + 1.0, win[None, :])
    return dict(cos=cos, sin=sin, m=m_intra, dq=d_q, dk=d_k, ds=d_s, bm=bm, invc=invc)


def _swap_halves(x):
    w = x.shape[1]
    lane = lax.broadcasted_iota(jnp.int32, x.shape, 1)
    return jnp.where((lane % RET_DK) < RET_DK // 2, pltpu.roll(x, w - RET_DK // 2, 1), pltpu.roll(x, RET_DK // 2, 1))


def _rot(x, cos, sin):
    return x * cos + _swap_halves(x) * sin


def _rot_t(d, cos, sin):
    return d * cos + _swap_halves(d * sin)


def _lane_groups(parts):
    return jnp.concatenate([p[:, POOL_GC * g:POOL_GC * (g + 1)] for g, p in enumerate(parts)], axis=1)


def _head_mask(shape, h):
    lane = lax.broadcasted_iota(jnp.int32, shape, 1)
    return (lane // RET_DK) == h


def _group_norm(o):
    yh, rs = [], []
    for h in range(RET_HEADS):
        oh = o[:, RET_DV * h:RET_DV * (h + 1)]
        xc = oh - jnp.mean(oh, axis=-1, keepdims=True)
        r = lax.rsqrt(jnp.mean(xc * xc, axis=-1, keepdims=True) + GN_EPS)
        yh.append(xc * r)
        rs.append(r)
    return yh, rs


def mixer_fwd(p, x1, consts, gn_gain, pool_w, pool_scale, wout, n_seq, S, name):
    _, T, _ = p.shape
    D = x1.shape[1]
    B = RET_BLOCK
    nb = S // B

    def body(p_ref, x1_ref, cos_ref, sin_ref, invc_ref, m_ref, dq_ref, dk_ref, ds_ref, bm_ref,
             gain_ref, pw_ref, sc_ref, wout_ref,
             x2_ref, m_out, o_out, st_out, pooled_out, state, prev_u):
        blk = pl.program_id(1)

        @pl.when(blk == 0)
        def _():
            state[...] = jnp.zeros_like(state)
            prev_u[...] = jnp.zeros_like(prev_u)

        qk = p_ref[0]
        v = p_ref[1]
        g = p_ref[2]
        u = p_ref[3]
        cos = cos_ref[...]
        sin = sin_ref[...]
        qr = _rot(qk[:, :QK_W], cos, sin) * (RET_DK ** -0.5)
        kr = _rot(qk[:, QK_W:], cos, sin)
        qb = qr.astype(BF16)
        kb = kr.astype(BF16)
        vb = v.astype(BF16)
        st = state[...]
        st_out[...] = st
        cross = _dot((qr * dq_ref[...]).astype(BF16), st.astype(BF16))
        o_parts = []
        for h in range(RET_HEADS):
            qm = jnp.where(_head_mask(qb.shape, h), qb, jnp.zeros_like(qb))
            sc = (_dot(qm, kb, NT) * m_ref[h]).astype(BF16)
            o_parts.append(_dot(sc, vb[:, RET_DV * h:RET_DV * (h + 1)]) + cross[:, RET_DV * h:RET_DV * (h + 1)])
        o = jnp.concatenate(o_parts, axis=1)
        o_out[...] = o
        kv = _dot((kr * dk_ref[...]).astype(BF16), vb, TN)
        state[...] = st * ds_ref[...] + kv * bm_ref[...]

        yh, _ = _group_norm(o)
        r = g * _sigmoid(g) * (jnp.concatenate(yh, axis=1) * gain_ref[...])

        ext = jnp.concatenate([prev_u[...], u], axis=0)
        sums = []
        run = ext
        for k in (1, 2, 4, 8):
            run = run + pltpu.roll(run, k, 0)
            sums.append(run[B:, :])
        prev_u[...] = u
        pooled = (_lane_groups(sums) * invc_ref[...] - u).astype(BF16)
        pooled_out[...] = pooled
        yp = [_dot(pooled[:, POOL_GC * gi:POOL_GC * (gi + 1)], pw_ref[gi].astype(BF16)) for gi in range(len(POOL_WINDOWS))]
        s = jnp.concatenate(yp, axis=1) * sc_ref[...]
        m = jnp.concatenate([r, s], axis=1).astype(BF16)
        m_out[...] = m
        x2_ref[...] = x1_ref[...] + _dot(m, wout_ref[...])

    tokmap = lambda b, k: (b * nb + k, 0)
    seqmap = lambda b, k: (k, 0)
    const2 = lambda b, k: (0, 0)
    const3 = lambda b, k: (0, 0, 0)
    return pl.pallas_call(
        body, name=name, grid=(n_seq, nb),
        in_specs=[pl.BlockSpec((4, B, V_W), lambda b, k: (0, b * nb + k, 0)),
                  pl.BlockSpec((B, D), tokmap),
                  pl.BlockSpec((B, QK_W), seqmap), pl.BlockSpec((B, QK_W), seqmap), pl.BlockSpec((B, V_W), seqmap),
                  pl.BlockSpec((RET_HEADS, B, B), const3),
                  pl.BlockSpec((B, QK_W), const2), pl.BlockSpec((B, QK_W), const2),
                  pl.BlockSpec((QK_W, V_W), const2), pl.BlockSpec((QK_W, V_W), const2),
                  pl.BlockSpec((1, V_W), const2), pl.BlockSpec((4, POOL_GC, POOL_GC), const3),
                  pl.BlockSpec((1, V_W), const2), pl.BlockSpec((D, D), const2)],
        out_specs=[pl.BlockSpec((B, D), tokmap), pl.BlockSpec((B, D), tokmap), pl.BlockSpec((B, V_W), tokmap),
                   pl.BlockSpec((None, QK_W, V_W), lambda b, k: (b * nb + k, 0, 0)),
                   pl.BlockSpec((B, V_W), tokmap)],
        out_shape=[jax.ShapeDtypeStruct((T, D), F32), jax.ShapeDtypeStruct((T, D), BF16),
                   jax.ShapeDtypeStruct((T, V_W), F32), jax.ShapeDtypeStruct((T // B, QK_W, V_W), F32),
                   jax.ShapeDtypeStruct((T, V_W), BF16)],
        scratch_shapes=[pltpu.VMEM((QK_W, V_W), F32), pltpu.VMEM((B, V_W), F32)],
        compiler_params=_params(2),
    )(p, x1, consts["cos"], consts["sin"], consts["invc"], consts["m"], consts["dq"], consts["dk"],
      consts["ds"], consts["bm"], gn_gain, pool_w, pool_scale, wout)


def mixer_bwd(dx3, dn3, x2, g3, x1, p, o_s, st_s, pooled_s, consts, gmix, gn_gain, pool_w, pool_scale, wout, win,
              n_seq, S, name, plan=None):
    T, D = x1.shape
    B = RET_BLOCK
    nb = S // B
    NG = len(POOL_WINDOWS)

    def body(dx3_ref, dn3_ref, x2_ref, g3_ref, x1_ref, p_ref, o_ref, st_ref, pooled_ref, cos_ref, sin_ref, invc_ref,
             m_ref, dq_ref, dk_ref, ds_ref, bm_ref, gmix_ref, gain_ref, pw_ref, sc_ref, wout_ref, win_ref,
             dx1_ref, df1_ref, dx2_ref, dp_ref, n2_ref, dg3_ref, dgmix_ref, dgain_ref, dscale_ref, dpw_ref,
             gstate, next_e):
        b = pl.program_id(0)
        blk = pl.program_id(1)

        @pl.when(jnp.logical_and(b == 0, blk == 0))
        def _():
            dg3_ref[...] = jnp.zeros_like(dg3_ref)
            dgmix_ref[...] = jnp.zeros_like(dgmix_ref)
            dgain_ref[...] = jnp.zeros_like(dgain_ref)
            dscale_ref[...] = jnp.zeros_like(dscale_ref)
            dpw_ref[...] = jnp.zeros_like(dpw_ref)

        @pl.when(blk == 0)
        def _():
            gstate[...] = jnp.zeros_like(gstate)
            next_e[...] = jnp.zeros_like(next_e)

        r2, xhat2 = _rms_parts(x2_ref[...])
        dn3 = dn3_ref[...]
        dg3_ref[...] += jnp.sum(dn3 * xhat2, axis=0, keepdims=True)
        dx2v = _rms_bwd(dn3, xhat2, r2, g3_ref[...], dx3_ref[...])
        dx2b = dx2v.astype(BF16)
        dx2_ref[...] = dx2b
        dm = _dot(dx2b, wout_ref[...], NT)
        dr = dm[:, :V_W]
        dsv = dm[:, V_W:]

        pooled = pooled_ref[...]
        scale = sc_ref[...]
        dyp = (dsv * scale).astype(BF16)
        yp, dpl = [], []
        for gi in range(NG):
            sl = slice(POOL_GC * gi, POOL_GC * (gi + 1))
            pwb = pw_ref[gi].astype(BF16)
            yp.append(_dot(pooled[:, sl], pwb))
            dpw_ref[gi] += _dot(pooled[:, sl], dyp[:, sl], TN)
            dpl.append(_dot(dyp[:, sl], pwb, NT))
        dscale_ref[...] += jnp.sum(dsv * jnp.concatenate(yp, axis=1), axis=0, keepdims=True)
        dpooled = jnp.concatenate(dpl, axis=1)
        e = dpooled * invc_ref[...]
        ext = jnp.concatenate([e, next_e[...]], axis=0)
        sums = []
        run = ext
        for k in (1, 2, 4, 8):
            run = run + pltpu.roll(run, 2 * B - k, 0)
            sums.append(run[:B, :])
        next_e[...] = e
        du = _lane_groups(sums) - dpooled

        qk = p_ref[0]
        v = p_ref[1]
        g = p_ref[2]
        sg = _sigmoid(g)
        si = g * sg
        yh, rs = _group_norm(o_ref[...])
        yhat = jnp.concatenate(yh, axis=1)
        gain = gain_ref[...]
        dg = dr * (yhat * gain) * (sg * (1.0 + g * (1.0 - sg)))
        dy = dr * si
        dgain_ref[...] += jnp.sum(dy * yhat, axis=0, keepdims=True)
        dyh = dy * gain
        do_parts = []
        for h in range(RET_HEADS):
            sl = slice(RET_DV * h, RET_DV * (h + 1))
            dh_ = dyh[:, sl]
            m1 = jnp.mean(dh_, axis=-1, keepdims=True)
            m2 = jnp.mean(dh_ * yh[h], axis=-1, keepdims=True)
            do_parts.append(rs[h] * (dh_ - m1 - yh[h] * m2))
        dob = jnp.concatenate(do_parts, axis=1).astype(BF16)

        cos = cos_ref[...]
        sin = sin_ref[...]
        qr = _rot(qk[:, :QK_W], cos, sin) * (RET_DK ** -0.5)
        kr = _rot(qk[:, QK_W:], cos, sin)
        qb = qr.astype(BF16)
        kb = kr.astype(BF16)
        vb = v.astype(BF16)
        dqd = dq_ref[...]
        dkd = dk_ref[...]
        stb = st_ref[...].astype(BF16)
        gs = gstate[...]
        gb = gs.astype(BF16)
        dqs = _dot(dob, stb, NT) * dqd
        dkr = _dot(vb, gb, NT) * dkd
        dv_cross = _dot((kr * dkd).astype(BF16), gb)
        ds_cross = _dot((qr * dqd).astype(BF16), dob, TN) * bm_ref[...]
        gstate[...] = ds_cross + gs * ds_ref[...]
        dv_parts = []
        for h in range(RET_HEADS):
            sl = slice(RET_DV * h, RET_DV * (h + 1))
            hm = _head_mask(qb.shape, h)
            qm = jnp.where(hm, qb, jnp.zeros_like(qb))
            mh = m_ref[h]
            sc = (_dot(qm, kb, NT) * mh).astype(BF16)
            dsc = (_dot(dob[:, sl], vb[:, sl], NT) * mh).astype(BF16)
            dqs = dqs + jnp.where(hm, _dot(dsc, kb), 0.0)
            dkr = dkr + jnp.where(hm, _dot(dsc, qb, TN), 0.0)
            dv_parts.append(_dot(sc, dob[:, sl], TN) + dv_cross[:, sl])
        dq = _rot_t(dqs * (RET_DK ** -0.5), cos, sin)
        dk = _rot_t(dkr, cos, sin)
        dp = [jnp.concatenate([dq, dk], axis=1).astype(BF16), jnp.concatenate(dv_parts, axis=1).astype(BF16),
              dg.astype(BF16), du.astype(BF16)]
        dn = jnp.zeros((B, D), F32)
        for jj in range(4):
            dp_ref[jj] = dp[jj]
            dn = dn + _dot(dp[jj], win_ref[jj], NT)

        x1v = x1_ref[...]
        r, xhat = _rms_parts(x1v)
        gm = gmix_ref[...]
        n2_ref[...] = (xhat * gm).astype(BF16)
        dgmix_ref[...] += jnp.sum(dn * xhat, axis=0, keepdims=True)
        dx1 = _rms_bwd(dn, xhat, r, gm, dx2v)
        dx1_ref[...] = dx1
        df1_ref[...] = (0.5 * dx1).astype(BF16)

    rev = lambda b, k: (b * nb + (nb - 1 - k), 0)
    seqrev = lambda b, k: (nb - 1 - k, 0)
    const2 = lambda b, k: (0, 0)
    const3 = lambda b, k: (0, 0, 0)
    return _call(
        body, plan, name=name, grid=(n_seq, nb),
        in_specs=[pl.BlockSpec((B, D), rev), pl.BlockSpec((B, D), rev), pl.BlockSpec((B, D), rev),
                  pl.BlockSpec((1, D), const2), pl.BlockSpec((B, D), rev),
                  pl.BlockSpec((4, B, V_W), lambda b, k: (0, b * nb + (nb - 1 - k), 0)),
                  pl.BlockSpec((B, V_W), rev),
                  pl.BlockSpec((None, QK_W, V_W), lambda b, k: (b * nb + (nb - 1 - k), 0, 0)),
                  pl.BlockSpec((B, V_W), rev),
                  pl.BlockSpec((B, QK_W), seqrev), pl.BlockSpec((B, QK_W), seqrev), pl.BlockSpec((B, V_W), seqrev),
                  pl.BlockSpec((RET_HEADS, B, B), const3),
                  pl.BlockSpec((B, QK_W), const2), pl.BlockSpec((B, QK_W), const2),
                  pl.BlockSpec((QK_W, V_W), const2), pl.BlockSpec((QK_W, V_W), const2),
                  pl.BlockSpec((1, D), const2), pl.BlockSpec((1, V_W), const2),
                  pl.BlockSpec((NG, POOL_GC, POOL_GC), const3), pl.BlockSpec((1, V_W), const2),
                  pl.BlockSpec((D, D), const2), pl.BlockSpec((4, D, V_W), const3)],
        out_specs=[pl.BlockSpec((B, D), rev), pl.BlockSpec((B, D), rev), pl.BlockSpec((B, D), rev),
                   pl.BlockSpec((4, B, V_W), lambda b, k: (0, b * nb + (nb - 1 - k), 0)),
                   pl.BlockSpec((B, D), rev),
                   pl.BlockSpec((1, D), const2), pl.BlockSpec((1, D), const2), pl.BlockSpec((1, V_W), const2),
                   pl.BlockSpec((1, V_W), const2), pl.BlockSpec((NG, POOL_GC, POOL_GC), const3)],
        out_shape=[jax.ShapeDtypeStruct((T, D), F32), jax.ShapeDtypeStruct((T, D), BF16),
                   jax.ShapeDtypeStruct((T, D), BF16), jax.ShapeDtypeStruct((4, T, V_W), BF16),
                   jax.ShapeDtypeStruct((T, D), BF16),
                   jax.ShapeDtypeStruct((1, D), F32), jax.ShapeDtypeStruct((1, D), F32), jax.ShapeDtypeStruct((1, V_W), F32),
                   jax.ShapeDtypeStruct((1, V_W), F32), jax.ShapeDtypeStruct((NG, POOL_GC, POOL_GC), F32)],
        scratch_shapes=[pltpu.VMEM((QK_W, V_W), F32), pltpu.VMEM((B, V_W), F32)],
        args=(dx3, dn3, x2, g3, x1, p, o_s, st_s, pooled_s, consts["cos"], consts["sin"], consts["invc"], consts["m"], consts["dq"],
              consts["dk"], consts["ds"], consts["bm"], gmix, gn_gain, pool_w, pool_scale, wout, win))


def loss_bwd(x3, tgt, gain, tm, name):
    T, D = x3.shape

    def body(x_ref, t_ref, g_ref, dx_ref, df_ref, lacc_ref, dg_ref):
        @pl.when(pl.program_id(0) == 0)
        def _():
            lacc_ref[...] = jnp.zeros_like(lacc_ref)
            dg_ref[...] = jnp.zeros_like(dg_ref)

        r, xhat = _rms_parts(x_ref[...])
        gv = g_ref[...]
        err = xhat * gv - t_ref[...]
        lacc_ref[...] += jnp.sum(err * err, axis=0, keepdims=True)
        dy = err * (1.0 / D)
        dg_ref[...] += jnp.sum(dy * xhat, axis=0, keepdims=True)
        dx = _rms_bwd(dy, xhat, r, gv, 0.0)
        dx_ref[...] = dx
        df_ref[...] = (0.5 * dx).astype(BF16)

    tok = pl.BlockSpec((tm, D), lambda i: (i, 0))
    vec = pl.BlockSpec((1, D), lambda i: (0, 0))
    return pl.pallas_call(
        body, name=name, grid=(T // tm,),
        in_specs=[tok, tok, vec], out_specs=[tok, tok, vec, vec],
        out_shape=[jax.ShapeDtypeStruct((T, D), F32), jax.ShapeDtypeStruct((T, D), BF16),
                   jax.ShapeDtypeStruct((1, D), F32), jax.ShapeDtypeStruct((1, D), F32)],
        compiler_params=_params(1),
    )(x3, tgt, gain)


def _adamw_math(w, g, m, v):
    m2 = ADAM_B1 * m + (1.0 - ADAM_B1) * g
    v2 = ADAM_B2 * v + (1.0 - ADAM_B2) * (g * g)
    m_hat = m2 / (1.0 - ADAM_B1 ** ADAM_STEP)
    v_hat = v2 / (1.0 - ADAM_B2 ** ADAM_STEP)
    return -ADAM_LR * (m_hat / (jnp.sqrt(v_hat) + ADAM_EPS) + ADAM_WD * w), m2, v2


def _place():
    x, y, c = lax.axis_index("x"), lax.axis_index("y"), lax.axis_index("c")
    other_chips = [(1 - x, y), (x, 1 - y), (1 - x, 1 - y)]
    return x, y, c, other_chips


def _exchange_plan(ins, out_shape, copies, n_copies):
    def descriptors(pins, pouts, psems):
        send, recv = psems
        return [pltpu.make_async_remote_copy(src_ref=s, dst_ref=d, send_sem=send.at[i], recv_sem=recv.at[i],
                                             device_id=dev, device_id_type=MESH)
                for i, (s, d, dev) in enumerate(copies(pins, pouts))]

    def start(pins, pouts, psems):
        for cp in descriptors(pins, pouts, psems):
            cp.start()

    def finish(pins, pouts, psems):
        for cp in descriptors(pins, pouts, psems):
            cp.wait()

    return CommPlan(tuple(ins), tuple(out_shape),
                    (pltpu.SemaphoreType.DMA((n_copies,)), pltpu.SemaphoreType.DMA((n_copies,))), start, finish)


def _combine(a, b):
    assert a.mid is None and b.mid is None
    ni, no, ns = len(a.ins), len(a.out_shape), len(a.sems)

    def start(pins, pouts, psems):
        a.start(pins[:ni], pouts[:no], psems[:ns])
        b.start(pins[ni:], pouts[no:], psems[ns:])

    def finish(pins, pouts, psems):
        a.finish(pins[:ni], pouts[:no], psems[:ns])
        b.finish(pins[ni:], pouts[no:], psems[ns:])

    return CommPlan(a.ins + b.ins, a.out_shape + b.out_shape, a.sems + b.sems, start, finish)


def gather_plan(shards):
    n = len(shards)

    def start(pins, pouts, psems):
        x, y, c, chips = _place()
        mine = 2 * x + y
        for w in range(n):
            for k, (px, py) in enumerate(chips):
                pltpu.make_async_remote_copy(
                    src_ref=pins[w].at[:, c], dst_ref=pouts[w].at[:, mine, c],
                    send_sem=psems[0].at[w, k], recv_sem=psems[1].at[w, k],
                    device_id=(px, py, c), device_id_type=MESH).start()

    def mid(pins, pouts, psems):
        x, y, c, chips = _place()
        for w in range(n):
            for k, (px, py) in enumerate(chips):
                landed = pouts[w].at[:, 2 * px + py, c]
                pltpu.make_async_remote_copy(
                    src_ref=landed, dst_ref=landed, send_sem=psems[0].at[w, k], recv_sem=psems[1].at[w, k],
                    device_id=(px, py, c), device_id_type=MESH).wait_recv()
                pltpu.make_async_remote_copy(
                    src_ref=landed, dst_ref=landed, send_sem=psems[2].at[w, k], recv_sem=psems[3].at[w, k],
                    device_id=(x, y, 1 - c), device_id_type=MESH).start()

    def finish(pins, pouts, psems):
        x, y, c, chips = _place()
        mine = 2 * x + y
        for w in range(n):
            for k, (px, py) in enumerate(chips):
                landed = pouts[w].at[:, 2 * px + py, 1 - c]
                cp = pltpu.make_async_remote_copy(
                    src_ref=landed, dst_ref=landed, send_sem=psems[2].at[w, k], recv_sem=psems[3].at[w, k],
                    device_id=(x, y, 1 - c), device_id_type=MESH)
                cp.wait_recv()
                cp.wait_send()
                pltpu.make_async_remote_copy(
                    src_ref=pins[w].at[:, c], dst_ref=pouts[w].at[:, mine, c],
                    send_sem=psems[0].at[w, k], recv_sem=psems[1].at[w, k],
                    device_id=(px, py, c), device_id_type=MESH).wait_send()

    return CommPlan(tuple(shards),
                    tuple(jax.ShapeDtypeStruct((s.shape[0], N_CHIPS) + s.shape[1:], s.dtype) for s in shards),
                    tuple(pltpu.SemaphoreType.DMA((n, 3)) for _ in range(4)), start, finish, mid)


def place_own(gathered, shard, name):
    L, _, Rh, C = shard.shape

    def body(chip_ref, g_ref, s_ref, o_ref):
        o_ref[...] = s_ref[...]

    return pl.pallas_call(
        body, name=name,
        grid_spec=pltpu.PrefetchScalarGridSpec(
            num_scalar_prefetch=1, grid=(L,),
            in_specs=[ANY, pl.BlockSpec((None, 2, Rh, C), lambda l, chip_ref: (l, 0, 0, 0))],
            out_specs=pl.BlockSpec((None, None, 2, Rh, C), lambda l, chip_ref: (l, chip_ref[0], 0, 0, 0))),
        out_shape=jax.ShapeDtypeStruct(gathered.shape, gathered.dtype),
        input_output_aliases={1: 0},
        compiler_params=_params(1),
    )(_chip_index(), gathered, shard)


def gather_weights(shards):
    n = len(shards)
    units = [(w, l) for w in range(n) for l in range(shards[w].shape[0])]
    nu = len(units)

    def body(*refs):
        ins, outs, bufs = refs[:n], refs[n:2 * n], refs[2 * n:3 * n]
        ld_sem, st_sem, own_send, own_recv, fwd_send, fwd_recv, d2d_send, d2d_recv = refs[3 * n:]
        x, y, c, _ = _place()
        sibling = (x, y, 1 - c)
        mine = 2 * x + y
        first = (x ^ (1 - c), y ^ c)
        second = (x ^ c, y ^ (1 - c))
        diagonal = (1 - x, 1 - y)

        def slot(chip):
            return 2 * chip[0] + chip[1]

        def remote(src, dst, send, recv, device):
            return pltpu.make_async_remote_copy(src_ref=src, dst_ref=dst, send_sem=send, recv_sem=recv,
                                                device_id=device, device_id_type=MESH)

        loads = [pltpu.make_async_copy(ins[w], bufs[w], ld_sem.at[w]) for w in range(n)]
        for cp in loads:
            cp.start()
        stores, sends = [], []
        for w in range(n):
            loads[w].wait()
            cp = pltpu.make_async_copy(bufs[w], outs[w].at[:, mine], st_sem.at[w])
            cp.start()
            stores.append(cp)
        for u, (w, l) in enumerate(units):
            for k, chip in enumerate((first, second)):
                cp = remote(bufs[w].at[l, c], outs[w].at[l, mine, c], own_send.at[u, k], own_recv.at[u, k], (*chip, c))
                cp.start()
                sends.append(cp)
        for u, (w, l) in enumerate(units):
            got = outs[w].at[l, slot(first), c]
            remote(got, got, own_send.at[u, 0], own_recv.at[u, 0], (*first, c)).wait_recv()
            for cp in (remote(got, got, fwd_send.at[u], fwd_recv.at[u], (*second, c)),
                       remote(got, got, d2d_send.at[u, 0], d2d_recv.at[u, 0], sibling)):
                cp.start()
                sends.append(cp)
        for u, (w, l) in enumerate(units):
            got = outs[w].at[l, slot(second), c]
            remote(got, got, own_send.at[u, 1], own_recv.at[u, 1], (*second, c)).wait_recv()
            cp = remote(got, got, d2d_send.at[u, 1], d2d_recv.at[u, 1], sibling)
            cp.start()
            sends.append(cp)
        for u, (w, l) in enumerate(units):
            got = outs[w].at[l, slot(diagonal), c]
            remote(got, got, fwd_send.at[u], fwd_recv.at[u], (*second, c)).wait_recv()
            cp = remote(got, got, d2d_send.at[u, 2], d2d_recv.at[u, 2], sibling)
            cp.start()
            sends.append(cp)
        for u, (w, l) in enumerate(units):
            for k, chip in enumerate((second, first, diagonal)):
                got = outs[w].at[l, slot(chip), 1 - c]
                remote(got, got, d2d_send.at[u, k], d2d_recv.at[u, k], sibling).wait_recv()
        for cp in sends:
            cp.wait_send()
        for cp in stores:
            cp.wait()

    return pl.pallas_call(
        body, name="gather_weights",
        in_specs=[ANY] * n, out_specs=[ANY] * n,
        out_shape=[jax.ShapeDtypeStruct((s.shape[0], N_CHIPS) + s.shape[1:], s.dtype) for s in shards],
        scratch_shapes=[pltpu.VMEM(s.shape, s.dtype) for s in shards] +
                       [pltpu.SemaphoreType.DMA((n,)), pltpu.SemaphoreType.DMA((n,)),
                        pltpu.SemaphoreType.DMA((nu, 2)), pltpu.SemaphoreType.DMA((nu, 2)),
                        pltpu.SemaphoreType.DMA((nu,)), pltpu.SemaphoreType.DMA((nu,)),
                        pltpu.SemaphoreType.DMA((nu, 3)), pltpu.SemaphoreType.DMA((nu, 3))],
        compiler_params=pltpu.CompilerParams(vmem_limit_bytes=VMEM_LIMIT),
    )(*shards)


def send_to_sibling_other_half(grads):
    def copies(ins, outs):
        x, y, c, _ = _place()
        return [(ins[w].at[:, :, 1 - c], outs[w], (x, y, 1 - c)) for w in range(len(grads))]

    return _exchange_plan(grads, [jax.ShapeDtypeStruct(g.shape[:2] + g.shape[3:], g.dtype) for g in grads], copies,
                          len(grads))


def _core_index():
    return jnp.reshape(lax.axis_index("c"), (1,)).astype(jnp.int32)


def _chip_index():
    return jnp.reshape(2 * lax.axis_index("x") + lax.axis_index("y"), (1,)).astype(jnp.int32)


def add_own_half(g, recv, name):
    L, J, _, Rh, C = g.shape

    def body(c_ref, g_ref, r_ref, o_ref):
        o_ref[...] = (g_ref[...] + r_ref[...]).astype(BF16)

    return pl.pallas_call(
        body, name=name,
        grid_spec=pltpu.PrefetchScalarGridSpec(
            num_scalar_prefetch=1, grid=(L, J),
            in_specs=[pl.BlockSpec((None, None, None, Rh, C), lambda l, j, c_ref: (l, j, c_ref[0], 0, 0)),
                      pl.BlockSpec((None, None, Rh, C), lambda l, j, c_ref: (l, j, 0, 0))],
            out_specs=pl.BlockSpec((None, None, Rh, C), lambda l, j, c_ref: (l, j, 0, 0))),
        out_shape=jax.ShapeDtypeStruct((L, J, Rh, C), BF16),
        compiler_params=_params(2),
    )(_core_index(), g, recv)


def exchange_between_chips(sums):
    def copies(ins, outs):
        x, y, c, chips = _place()
        return [(ins[w].at[:, 2 * px + py], outs[w].at[k], (px, py, c))
                for w in range(len(sums)) for k, (px, py) in enumerate(chips)]

    return _exchange_plan(sums, [jax.ShapeDtypeStruct((3, s.shape[0]) + s.shape[2:], s.dtype) for s in sums], copies,
                          3 * len(sums))


def sum_chips(own, recv, name):
    L, _, Rh, C = own.shape

    def body(chip_ref, o_ref, r_ref, out_ref):
        acc = o_ref[...].astype(F32)
        for k in range(3):
            acc = acc + r_ref[k].astype(F32)
        out_ref[...] = acc

    return pl.pallas_call(
        body, name=name,
        grid_spec=pltpu.PrefetchScalarGridSpec(
            num_scalar_prefetch=1, grid=(L,),
            in_specs=[pl.BlockSpec((None, None, Rh, C), lambda l, chip_ref: (l, chip_ref[0], 0, 0)),
                      pl.BlockSpec((3, None, Rh, C), lambda l, chip_ref: (0, l, 0, 0))],
            out_specs=pl.BlockSpec((None, Rh, C), lambda l, chip_ref: (l, 0, 0))),
        out_shape=jax.ShapeDtypeStruct((L, Rh, C), F32),
        compiler_params=_params(1),
    )(_chip_index(), own, recv)


def share_with_sibling(halves):
    def copies(ins, outs):
        x, y, c, _ = _place()
        return [(ins[w], outs[w], (x, y, 1 - c)) for w in range(len(halves))]

    return _exchange_plan(halves, [jax.ShapeDtypeStruct(h.shape, h.dtype) for h in halves], copies, len(halves))


def adamw_group(ws, ms, vs, own, sib, name, plan=None):
    L = len(ws)
    R, C = ws[0].shape
    Rh = R // 2
    tr = Rh // 2
    nr = Rh // tr

    def body(*refs):
        w, m, v = refs[:L], refs[L:2 * L], refs[2 * L:3 * L]
        own_ref, sib_ref = refs[3 * L], refs[3 * L + 1]
        outs = refs[3 * L + 2:]
        mine = pl.program_id(0) == lax.axis_index("c")
        for l in range(L):
            gv = jnp.where(mine, own_ref[l], sib_ref[l])
            d, m2, v2 = _adamw_math(w[l][...], gv, m[l][...], v[l][...])
            outs[4 * l][...] = d
            outs[4 * l + 1][...] = m2
            outs[4 * l + 2][...] = v2
            outs[4 * l + 3][...] = gv

    blk = pl.BlockSpec((tr, C), lambda h, r: (h * nr + r, 0))
    half = pl.BlockSpec((L, tr, C), lambda h, r: (0, r, 0))
    sds = jax.ShapeDtypeStruct((R, C), F32)
    return _call(body, plan, name=name, grid=(2, nr), in_specs=[blk] * (3 * L) + [half, half],
                 out_specs=[blk] * (4 * L), out_shape=[sds] * (4 * L), args=(*ws, *ms, *vs, own, sib))


def gather_small_plan(part):
    def copies(ins, outs):
        x, y, c, _ = _place()
        me = 4 * x + 2 * y + c
        return [(ins[0], outs[0].at[me], (x ^ ((k >> 2) & 1), y ^ ((k >> 1) & 1), c ^ (k & 1))) for k in range(1, N_DEV)]

    return _exchange_plan([part], [jax.ShapeDtypeStruct((N_DEV,) + part.shape, part.dtype)], copies, N_DEV - 1)


def sum_small(part, gathered, name):
    R, C = part.shape

    def body(p_ref, g_ref, o_ref):
        d = pl.program_id(0)
        me = 4 * lax.axis_index("x") + 2 * lax.axis_index("y") + lax.axis_index("c")
        val = jnp.where(d == me, p_ref[...], g_ref[...])

        @pl.when(d == 0)
        def _():
            o_ref[...] = val

        @pl.when(d > 0)
        def _():
            o_ref[...] += val

    return pl.pallas_call(
        body, name=name, grid=(N_DEV,),
        in_specs=[pl.BlockSpec((R, C), lambda d: (0, 0)), pl.BlockSpec((None, R, C), lambda d: (d, 0, 0))],
        out_specs=pl.BlockSpec((R, C), lambda d: (0, 0)),
        out_shape=jax.ShapeDtypeStruct((R, C), F32),
        compiler_params=_params(1),
    )(part, gathered)


SMALL = ("norm_ffn1", "norm_mix", "norm_ffn2", "norm_final", "ret_gn_gain", "pool_scale", "pool_w")
GROUPS = (("ffn1", ("ffn1_gate", "ffn1_up", "ffn1_down")), ("w_in", ("w_in",)), ("w_out", ("w_out",)),
          ("ffn2", ("ffn2_gate", "ffn2_up", "ffn2_down")))
TRANSPOSED = ("ffn1_gate", "ffn1_up", "ffn2_gate", "ffn2_up")
IN_W = 2 * QK_W + 3 * V_W
ORDER = ("norm_ffn1", "ffn1_gate", "ffn1_up", "ffn1_down", "norm_mix", "w_in", "ret_gn_gain", "pool_w", "pool_scale",
         "w_out", "norm_ffn2", "ffn2_gate", "ffn2_up", "ffn2_down", "norm_final")


SUBLANES = 8


def _small_rows(shapes):
    rows = [math.prod(shapes[k]) // 128 for k in SMALL]
    offs, off = [], 0
    for r in rows:
        offs.append(off)
        off += -(-r // SUBLANES) * SUBLANES
    return rows, offs, off


def _pack_small(d, shapes):
    rows, offs, _ = _small_rows(shapes)
    pieces = []
    for k, r in zip(SMALL[1:], rows[1:]):
        pieces.append(d[k].reshape(r, 128))
        if r % SUBLANES:
            pieces.append(jnp.zeros((SUBLANES - r % SUBLANES, 128), F32))
    return jnp.concatenate(pieces, axis=0)


def adamw_small(W, M, V, first, summed, shapes):
    rows, offs, _ = _small_rows(shapes)
    n = len(SMALL)

    def body(*refs):
        w, m, v, f_ref, s_ref = refs[:n], refs[n:2 * n], refs[2 * n:3 * n], refs[3 * n], refs[3 * n + 1]
        outs = refs[3 * n + 2:]
        for i, (r, off) in enumerate(zip(rows, offs)):
            gv = f_ref[...] if i == 0 else s_ref[off - offs[1]:off - offs[1] + r, :]
            d, m2, v2 = _adamw_math(w[i][...], gv, m[i][...], v[i][...])
            outs[4 * i][...] = d
            outs[4 * i + 1][...] = m2
            outs[4 * i + 2][...] = v2
            outs[4 * i + 3][...] = gv

    vm = pl.BlockSpec(memory_space=pltpu.VMEM)
    flat = lambda d: [d[k].reshape(r, 128) for k, r in zip(SMALL, rows)]
    res = pl.pallas_call(
        body, name="adamw_small", in_specs=[vm] * (3 * n + 2), out_specs=[vm] * (4 * n),
        out_shape=[jax.ShapeDtypeStruct((r, 128), F32) for r in rows for _ in range(4)],
    )(*flat(W), *flat(M), *flat(V), first, summed)
    return {k: [a.reshape(shapes[k]) for a in res[4 * i:4 * i + 4]] for i, k in enumerate(SMALL)}


def kernel(x, norm_ffn1, ffn1_gate, ffn1_up, ffn1_down, norm_mix, w_in, ret_gn_gain, pool_w, pool_scale, w_out, norm_ffn2, ffn2_gate, ffn2_up, ffn2_down, norm_final, loss_target, m_norm_ffn1, m_ffn1_gate, m_ffn1_up, m_ffn1_down, m_norm_mix, m_w_in, m_ret_gn_gain, m_pool_w, m_pool_scale, m_w_out, m_norm_ffn2, m_ffn2_gate, m_ffn2_up, m_ffn2_down, m_norm_final, v_norm_ffn1, v_ffn1_gate, v_ffn1_up, v_ffn1_down, v_norm_mix, v_w_in, v_ret_gn_gain, v_pool_w, v_pool_scale, v_w_out, v_norm_ffn2, v_ffn2_gate, v_ffn2_up, v_ffn2_down, v_norm_final):
    W = dict(norm_ffn1=norm_ffn1, ffn1_gate=ffn1_gate, ffn1_up=ffn1_up, ffn1_down=ffn1_down, norm_mix=norm_mix, w_in=w_in,
             ret_gn_gain=ret_gn_gain, pool_w=pool_w, pool_scale=pool_scale, w_out=w_out, norm_ffn2=norm_ffn2,
             ffn2_gate=ffn2_gate, ffn2_up=ffn2_up, ffn2_down=ffn2_down, norm_final=norm_final)
    M = dict(norm_ffn1=m_norm_ffn1, ffn1_gate=m_ffn1_gate, ffn1_up=m_ffn1_up, ffn1_down=m_ffn1_down, norm_mix=m_norm_mix,
             w_in=m_w_in, ret_gn_gain=m_ret_gn_gain, pool_w=m_pool_w, pool_scale=m_pool_scale, w_out=m_w_out,
             norm_ffn2=m_norm_ffn2, ffn2_gate=m_ffn2_gate, ffn2_up=m_ffn2_up, ffn2_down=m_ffn2_down, norm_final=m_norm_final)
    V = dict(norm_ffn1=v_norm_ffn1, ffn1_gate=v_ffn1_gate, ffn1_up=v_ffn1_up, ffn1_down=v_ffn1_down, norm_mix=v_norm_mix,
             w_in=v_w_in, ret_gn_gain=v_ret_gn_gain, pool_w=v_pool_w, pool_scale=v_pool_scale, w_out=v_w_out,
             norm_ffn2=v_norm_ffn2, ffn2_gate=v_ffn2_gate, ffn2_up=v_ffn2_up, ffn2_down=v_ffn2_down, norm_final=v_norm_final)

    n_seq, S, D = x.shape
    T = n_seq * S
    tm = min(1024, T // 2)
    tk = min(1024, T // 2)
    tk1 = min(2048, T // 2)
    xf = x.reshape(T, D)
    tgt = loss_target.reshape(T, D)

    def local(d, k):
        a = d[k][0]
        return a.T if k in TRANSPOSED else a

    def to_out(k, a):
        return (a.T if k in TRANSPOSED else a)[None]

    loc_bf = []
    for _, members in GROUPS:
        st = jnp.stack([local(W, k).astype(BF16) for k in members])
        loc_bf.append(st.reshape(st.shape[0], 2, st.shape[1] // 2, st.shape[2]))
    FS = loc_bf[0].shape[2] * 2
    w3_1 = gather_weights(loc_bf[:1])[0].reshape(3, N_CHIPS, FS, D)
    g1 = norm_ffn1.reshape(1, D)
    gm = norm_mix.reshape(1, D)
    g3 = norm_ffn2.reshape(1, D)
    gf = norm_final.reshape(1, D)
    gn_gain = ret_gn_gain.reshape(1, V_W)
    pscale = pool_scale.reshape(1, V_W)
    pw = pool_w.reshape(len(POOL_WINDOWS), POOL_GC, POOL_GC)
    consts = mixer_constants(S)

    def halves5(g, lb):
        return g.reshape(lb.shape[0], N_CHIPS, 2, lb.shape[2], lb.shape[3])

    (x1, b1, si1, t1, n1), landed = ffn_fwd(xf, g1, w3_1, tm, "ffn1_fwd", plan=gather_plan(loc_bf[1:]))
    gathered = [place_own(g, lb, "place_own_" + gn) for g, lb, (gn, _) in zip(landed, loc_bf[1:], GROUPS[1:])]
    win_full = gathered[0].reshape(N_CHIPS, D, IN_W // N_CHIPS)
    wout_full = gathered[1].reshape(D, D)
    w3_2 = gathered[2].reshape(3, N_CHIPS, FS, D)
    p = in_proj(x1, gm, win_full, tm, "in_proj")
    x2, m_s, o_s, st_s, pooled_s = mixer_fwd(p, x1, consts, gn_gain, pw, pscale, wout_full, n_seq, S, "mixer_fwd")
    x3, b2, si2, t2, n3 = ffn_fwd(x2, g3, w3_2, tm, "ffn2_fwd")

    dx3, df2, lacc, dgf = loss_bwd(x3, tgt, gf, tm, "loss_bwd")
    dn3, da2, db2 = ffn_bwd(df2, b2, si2, t2, w3_2, tm, "ffn2_bwd")
    to_sib, to_chips, back = send_to_sibling_other_half, exchange_between_chips, share_with_sibling
    out_g, out_d, out_m, out_v = {}, {}, {}, {}

    def update(members, own, sib, name):
        res = adamw_group([local(W, k) for k in members], [local(M, k) for k in members], [local(V, k) for k in members],
                          own, sib, "adamw_" + name)
        for l, k in enumerate(members):
            out_d[k], out_m[k], out_v[k], out_g[k] = (to_out(k, r) for r in res[4 * l:4 * l + 4])

    (gr_ffn2,) = ffn_wgrad(da2, db2, si2, b2, n3, df2, tk, "ffn2_wgrad")
    g_ffn2 = halves5(gr_ffn2, loc_bf[3])
    (dx1, df1, dx2b, dp, n2, dg3, dgm, dgn, dps, dpw), (sb_ffn2,) = mixer_bwd(
        dx3, dn3, x2, g3, x1, p, o_s, st_s, pooled_s, consts, gm, gn_gain, pw, pscale, wout_full, win_full, n_seq, S,
        "mixer_bwd", plan=to_sib([g_ffn2]))
    cs_ffn2 = add_own_half(g_ffn2, sb_ffn2, "add_sibling_ffn2")
    (gr_wout,) = tn_shared(m_s, dx2b[None], tk1, "dw_out")
    (gr_win,) = tn_shared(n2, dp, tk1, "dw_in")
    g_mix = [halves5(gr_win, loc_bf[1]), halves5(gr_wout, loc_bf[2])]
    shapes = {k: W[k].shape for k in SMALL}
    small_part = dict(norm_mix=dgm, norm_ffn2=dg3, norm_final=dgf, ret_gn_gain=dgn, pool_scale=dps, pool_w=dpw)
    part = jnp.concatenate([_pack_small(small_part, shapes), lacc.reshape(-1, 128)], axis=0)
    n_small = part.shape[0] - lacc.size // 128
    (dn1, da1, db1), (pc_ffn2, sb_win, sb_wout, parts) = ffn_bwd(
        df1, b1, si1, t1, w3_1, tm, "ffn1_bwd",
        plan=_combine(_combine(to_chips([cs_ffn2]), to_sib(g_mix)), gather_small_plan(part)))
    summed = sum_small(part, parts, "sum_small")
    loss = jnp.sum(summed[n_small:]) * (0.5 / D)
    mh_ffn2 = sum_chips(cs_ffn2, pc_ffn2, "sum_chips_ffn2")
    cs_mix = [add_own_half(g, r, "add_sibling_" + gn) for g, r, gn in zip(g_mix, (sb_win, sb_wout), ("w_in", "w_out"))]
    def one(g):
        return g.reshape((1, N_CHIPS, 2) + loc_bf[0].shape[2:])

    (gr_g,), (sh_ffn2, pc_win, pc_wout) = wgrad_one(
        da1, n1, tk1, "ffn1_wgrad_gate", plan=_combine(back([mh_ffn2]), to_chips(cs_mix)))
    mh_mix = [sum_chips(cs, pc, "sum_chips_" + gn) for cs, pc, gn in zip(cs_mix, (pc_win, pc_wout), ("w_in", "w_out"))]
    (gr_u,), (sb_g,) = wgrad_one(db1, n1, tk1, "ffn1_wgrad_up", plan=to_sib([one(gr_g)]))
    cs_g = add_own_half(one(gr_g), sb_g, "add_sibling_ffn1_gate")
    (gr_d,), (pc_g, sb_u) = wgrad_one(
        si1, df1, tk1, "ffn1_wgrad_down", plan=_combine(to_chips([cs_g]), to_sib([one(gr_u)])), a2=b1)
    mh_g = sum_chips(cs_g, pc_g, "sum_chips_ffn1_gate")
    cs_u = add_own_half(one(gr_u), sb_u, "add_sibling_ffn1_up")
    (dx0, dg1), (sh_g, sh_win, sh_wout, pc_u, sb_d) = rms_bwd(
        xf, dx1, dn1, g1, tm, "ffn1_rms_bwd",
        plan=_combine(_combine(back([mh_g] + mh_mix), to_chips([cs_u])), to_sib([one(gr_d)])))
    mh_u = sum_chips(cs_u, pc_u, "sum_chips_ffn1_up")
    cs_d = add_own_half(one(gr_d), sb_d, "add_sibling_ffn1_down")
    first = dg1.reshape(-1, 128)
    sh_u, pc_d, firsts = _run_plan(
        _combine(_combine(back([mh_u]), to_chips([cs_d])), gather_small_plan(first)), "ffn1_grads_stage_b")
    mh_d = sum_chips(cs_d, pc_d, "sum_chips_ffn1_down")
    (sh_d,) = _run_plan(back([mh_d]), "ffn1_grads_stage_c")
    first_sum = sum_small(first, firsts, "sum_small_first")
    for k, own, sib in zip(GROUPS[0][1], (mh_g, mh_u, mh_d), (sh_g, sh_u, sh_d)):
        update((k,), own, sib, k)
    update(GROUPS[1][1], mh_mix[0], sh_win, "w_in")
    update(GROUPS[2][1], mh_mix[1], sh_wout, "w_out")
    update(GROUPS[3][1], mh_ffn2, sh_ffn2, "ffn2")

    for k, (d_, m_, v_, g_) in adamw_small(W, M, V, first_sum, summed, shapes).items():
        out_d[k], out_m[k], out_v[k], out_g[k] = d_, m_, v_, g_

    grad_x = dx0.reshape(n_seq, S, D)
    return (loss, grad_x, *[out_g[k] for k in ORDER], *[out_d[k] for k in ORDER],
            *[out_m[k] for k in ORDER], *[out_v[k] for k in ORDER])
```
